```python
import math
import jax, jax.numpy as jnp
from jax import lax
import numpy as np

D_MODEL = 1024
BATCH = 4
SEQ = 4096
DEPTH = 1

N_MEM = 256
D_MIX = D_MODEL
DN_HEADS = 4
DN_HEAD_DIM = 128
DN_WIDTH = DN_HEADS * DN_HEAD_DIM
CONV_WIDTH = 4
CHUNK = 64
S5_WIDTH = D_MIX - DN_WIDTH
S5_CH_PER_GROUP = 16
S5_GROUPS = S5_WIDTH // S5_CH_PER_GROUP
S5_STATE = 64
X_HEADS = 4
X_HEAD_DIM = D_MODEL // X_HEADS
D_FF = -(-8 * D_MODEL // (3 * 256)) * 256
EPS = 1e-6

OFF_Q = 0
OFF_K = OFF_Q + DN_WIDTH
OFF_V = OFF_K + DN_WIDTH
OFF_Z = OFF_V + DN_WIDTH
OFF_A = OFF_Z + DN_WIDTH
OFF_B = OFF_A + DN_HEADS
OFF_U = OFF_B + DN_HEADS
D_IN = OFF_U + S5_WIDTH

kernel_name = "hymba_deltanet_s5_memxattn_block"


def _rmsnorm(x, gain):
    x32 = x.astype(jnp.float32)
    y = x32 * lax.rsqrt(jnp.mean(x32 * x32, axis=-1, keepdims=True) + EPS)
    return (y * gain.astype(jnp.float32)).astype(x.dtype)


def _l2norm(x):
    return x * lax.rsqrt(jnp.sum(x * x, axis=-1, keepdims=True) + EPS)


def _causal_dwconv(x, w):
    c = x.shape[-1]
    return lax.conv_general_dilated(
        x, w.astype(x.dtype)[:, None, :], window_strides=(1,),
        padding=((CONV_WIDTH - 1, 0),),
        dimension_numbers=("NWC", "WIO", "NWC"), feature_group_count=c)


def _gated_delta_rule(q, k, v, g, beta):
    bsz, t, h, dk = q.shape
    dv = v.shape[-1]
    n = t // CHUNK

    def chunk(a):
        a = a.reshape((bsz, n, CHUNK, h) + a.shape[3:])
        return jnp.moveaxis(a, 3, 2)

    q, k, v, g, beta = chunk(q), chunk(k), chunk(v), chunk(g), chunk(beta)
    g_cum = jnp.cumsum(g, axis=-1)
    causal = jnp.tril(jnp.ones((CHUNK, CHUNK), dtype=bool))
    strict = jnp.tril(jnp.ones((CHUNK, CHUNK), dtype=bool), k=-1)
    decay = jnp.exp(jnp.where(causal, g_cum[..., :, None] - g_cum[..., None, :], -jnp.inf))

    kb = k * beta[..., None]
    lower = jnp.where(strict, jnp.einsum("bnhcd,bnhsd->bnhcs", kb, k) * decay, 0.0)
    tmat = jnp.eye(CHUNK, dtype=jnp.float32) + lower
    rhs = jnp.concatenate([v * beta[..., None], kb * jnp.exp(g_cum)[..., None]], axis=-1)
    sol = lax.linalg.triangular_solve(tmat, rhs, left_side=True, lower=True)
    u, w = sol[..., :dv], sol[..., dv:]

    attn = jnp.where(causal, jnp.einsum("bnhcd,bnhsd->bnhcs", q, k) * decay, 0.0)
    qg = q * jnp.exp(g_cum)[..., None]
    g_last = g_cum[..., -1]
    k_dec = k * jnp.exp(g_last[..., None] - g_cum)[..., None]
    a_last = jnp.exp(g_last)

    def step(s, inp):
        qg_c, kd_c, u_c, w_c, at_c, al_c = inp
        v_new = u_c - jnp.einsum("bhck,bhkv->bhcv", w_c, s)
        o_c = jnp.einsum("bhck,bhkv->bhcv", qg_c, s) + jnp.einsum("bhcs,bhsv->bhcv", at_c, v_new)
        s = s * al_c[..., None, None] + jnp.einsum("bhck,bhcv->bhkv", kd_c, v_new)
        return s, o_c

    xs = tuple(jnp.moveaxis(a, 1, 0) for a in (qg, k_dec, u, w, attn, a_last))
    s0 = jnp.zeros((bsz, h, dk, dv), jnp.float32)
    _, o = lax.scan(step, s0, xs)
    o = jnp.transpose(o, (1, 0, 3, 2, 4))
    return o.reshape(bsz, t, h, dv)


def _s5(u, a_re, a_im, b_re, b_im, c_re, c_im, d, log_dt):
    f = jnp.float32
    u = u.astype(f)
    a_re, a_im = a_re.astype(f), a_im.astype(f)
    b_re, b_im = b_re.astype(f), b_im.astype(f)
    dt = jnp.exp(log_dt.astype(f))[:, None]
    mag = jnp.exp(a_re * dt)
    ab_re, ab_im = mag * jnp.cos(a_im * dt), mag * jnp.sin(a_im * dt)
    den = a_re * a_re + a_im * a_im
    nr, ni = ab_re - 1.0, ab_im
    co_re = (nr * a_re + ni * a_im) / den
    co_im = (ni * a_re - nr * a_im) / den
    bb_re = co_re[..., None] * b_re - co_im[..., None] * b_im
    bb_im = co_re[..., None] * b_im + co_im[..., None] * b_re
    bu_re = jnp.einsum("btgc,gpc->tbgp", u, bb_re)
    bu_im = jnp.einsum("btgc,gpc->tbgp", u, bb_im)
    t = u.shape[1]
    la_re = jnp.broadcast_to(ab_re, (t, 1) + ab_re.shape)
    la_im = jnp.broadcast_to(ab_im, (t, 1) + ab_im.shape)

    def combine(e1, e2):
        a1r, a1i, b1r, b1i = e1
        a2r, a2i, b2r, b2i = e2
        return (a1r * a2r - a1i * a2i,
                a1r * a2i + a1i * a2r,
                a2r * b1r - a2i * b1i + b2r,
                a2r * b1i + a2i * b1r + b2i)

    _, _, x_re, x_im = lax.associative_scan(combine, (la_re, la_im, bu_re, bu_im), axis=0)
    y = (jnp.einsum("tbgp,gcp->btgc", x_re, c_re.astype(f))
         - jnp.einsum("tbgp,gcp->btgc", x_im, c_im.astype(f)))
    return y + d.astype(f) * u


def _mixer(xn, w_in, conv_w, dn_a_log, dn_dt_bias, dn_norm_g,
           s5_a_re, s5_a_im, s5_b_re, s5_b_im, s5_c_re, s5_c_im, s5_d, s5_log_dt,
           s5_w_glu, s5_b_glu, w_out):
    bsz, t, _ = xn.shape
    dt_ = xn.dtype
    proj = xn @ w_in

    qkv = jax.nn.silu(_causal_dwconv(proj[..., OFF_Q:OFF_Z], conv_w)).astype(jnp.float32)
    q = qkv[..., :DN_WIDTH].reshape(bsz, t, DN_HEADS, DN_HEAD_DIM)
    k = qkv[..., DN_WIDTH:2 * DN_WIDTH].reshape(bsz, t, DN_HEADS, DN_HEAD_DIM)
    v = qkv[..., 2 * DN_WIDTH:].reshape(bsz, t, DN_HEADS, DN_HEAD_DIM)
    q = _l2norm(q) * (DN_HEAD_DIM ** -0.5)
    k = _l2norm(k)
    z = proj[..., OFF_Z:OFF_A].astype(jnp.float32).reshape(bsz, t, DN_HEADS, DN_HEAD_DIM)
    a = proj[..., OFF_A:OFF_B].astype(jnp.float32)
    b = proj[..., OFF_B:OFF_U].astype(jnp.float32)
    g = -jnp.exp(dn_a_log.astype(jnp.float32)) * jax.nn.softplus(a + dn_dt_bias.astype(jnp.float32))
    beta = jax.nn.sigmoid(b)
    o = _gated_delta_rule(q, k, v, g, beta)
    o = o * lax.rsqrt(jnp.mean(o * o, axis=-1, keepdims=True) + EPS)
    o = o * dn_norm_g.astype(jnp.float32) * jax.nn.silu(z)
    o = o.reshape(bsz, t, DN_WIDTH).astype(dt_)

    u = proj[..., OFF_U:].reshape(bsz, t, S5_GROUPS, S5_CH_PER_GROUP)
    y = _s5(u, s5_a_re, s5_a_im, s5_b_re, s5_b_im, s5_c_re, s5_c_im, s5_d, s5_log_dt)
    y = jax.nn.gelu(y.reshape(bsz, t, S5_WIDTH)).astype(dt_)
    y = y * jax.nn.sigmoid(y @ s5_w_glu + s5_b_glu)

    return jnp.concatenate([o, y], axis=-1) @ w_out


def _cross_attn(hn, mem, norm_mem_g, w_xq, w_xk, w_xv, w_xo):
    bsz, t, _ = hn.shape
    mn = _rmsnorm(mem, norm_mem_g)
    q = (hn @ w_xq).reshape(bsz, t, X_HEADS, X_HEAD_DIM)
    k = (mn @ w_xk).reshape(bsz, N_MEM, X_HEADS, X_HEAD_DIM)
    v = (mn @ w_xv).reshape(bsz, N_MEM, X_HEADS, X_HEAD_DIM)
    s = jnp.einsum("bthd,bmhd->bhtm", q, k).astype(jnp.float32) * (X_HEAD_DIM ** -0.5)
    p = jax.nn.softmax(s, axis=-1).astype(hn.dtype)
    o = jnp.einsum("bhtm,bmhd->bthd", p, v).reshape(bsz, t, D_MODEL)
    return o @ w_xo


def _swiglu(hn, w_gate, w_up, w_down):
    return (jax.nn.silu(hn @ w_gate) * (hn @ w_up)) @ w_down


def setup_inputs(seed: int = 0) -> dict:
    key = jax.random.key(seed)
    ks = iter(jax.random.split(key, 40))
    nrm = lambda shape, scale: jax.random.normal(next(ks), shape, jnp.float32) * scale
    unif = lambda shape, lo, hi: jax.random.uniform(next(ks), shape, jnp.float32, lo, hi)
    L = DEPTH
    x = nrm((BATCH, SEQ, D_MODEL), 1.0)
    mem = nrm((BATCH, N_MEM, D_MODEL), 1.0)
    gain = lambda n: 1.0 + nrm((L, n), 0.02)
    dt_dn = jnp.exp(unif((L, DN_HEADS), math.log(1e-3), math.log(1e-1)))
    n_idx = jnp.arange(S5_STATE, dtype=jnp.float32)
    return {
        "x": x,
        "mem": mem,
        "norm_mix_g": gain(D_MODEL),
        "w_in": nrm((L, D_MODEL, D_IN), D_MODEL ** -0.5),
        "conv_w": nrm((L, CONV_WIDTH, 3 * DN_WIDTH), CONV_WIDTH ** -0.5),
        "dn_a_log": jnp.log(unif((L, DN_HEADS), 1.0, 16.0)),
        "dn_dt_bias": dt_dn + jnp.log(-jnp.expm1(-dt_dn)),
        "dn_norm_g": gain(DN_HEAD_DIM),
        "s5_a_re": -0.5 + nrm((L, S5_GROUPS, S5_STATE), 0.01),
        "s5_a_im": math.pi * n_idx + nrm((L, S5_GROUPS, S5_STATE), 0.01),
        "s5_b_re": nrm((L, S5_GROUPS, S5_STATE, S5_CH_PER_GROUP), (2 * S5_CH_PER_GROUP) ** -0.5),
        "s5_b_im": nrm((L, S5_GROUPS, S5_STATE, S5_CH_PER_GROUP), (2 * S5_CH_PER_GROUP) ** -0.5),
        "s5_c_re": nrm((L, S5_GROUPS, S5_CH_PER_GROUP, S5_STATE), (2 * S5_STATE) ** -0.5),
        "s5_c_im": nrm((L, S5_GROUPS, S5_CH_PER_GROUP, S5_STATE), (2 * S5_STATE) ** -0.5),
        "s5_d": nrm((L, S5_GROUPS, S5_CH_PER_GROUP), 1.0),
        "s5_log_dt": unif((L, S5_GROUPS), math.log(1e-3), math.log(1e-1)),
        "s5_w_glu": nrm((L, S5_WIDTH, S5_WIDTH), S5_WIDTH ** -0.5),
        "s5_b_glu": nrm((L, S5_WIDTH), 0.01),
        "w_out": nrm((L, D_MIX, D_MODEL), D_MIX ** -0.5),
        "norm_x_g": gain(D_MODEL),
        "norm_mem_g": gain(D_MODEL),
        "w_xq": nrm((L, D_MODEL, D_MODEL), D_MODEL ** -0.5),
        "w_xk": nrm((L, D_MODEL, D_MODEL), D_MODEL ** -0.5),
        "w_xv": nrm((L, D_MODEL, D_MODEL), D_MODEL ** -0.5),
        "w_xo": nrm((L, D_MODEL, D_MODEL), D_MODEL ** -0.5),
        "norm_ffn_g": gain(D_MODEL),
        "w_gate": nrm((L, D_MODEL, D_FF), D_MODEL ** -0.5),
        "w_up": nrm((L, D_MODEL, D_FF), D_MODEL ** -0.5),
        "w_down": nrm((L, D_FF, D_MODEL), D_FF ** -0.5),
        "norm_final_g": 1.0 + nrm((D_MODEL,), 0.02),
    }


def reference(x, mem, norm_mix_g, w_in, conv_w, dn_a_log, dn_dt_bias, dn_norm_g,
              s5_a_re, s5_a_im, s5_b_re, s5_b_im, s5_c_re, s5_c_im, s5_d, s5_log_dt,
              s5_w_glu, s5_b_glu, w_out, norm_x_g, norm_mem_g, w_xq, w_xk, w_xv, w_xo,
              norm_ffn_g, w_gate, w_up, w_down, norm_final_g):
    h = x
    for l in range(DEPTH):
        h = h + _mixer(_rmsnorm(h, norm_mix_g[l]), w_in[l], conv_w[l], dn_a_log[l],
                       dn_dt_bias[l], dn_norm_g[l], s5_a_re[l], s5_a_im[l], s5_b_re[l],
                       s5_b_im[l], s5_c_re[l], s5_c_im[l], s5_d[l], s5_log_dt[l],
                       s5_w_glu[l], s5_b_glu[l], w_out[l])
        h = h + _cross_attn(_rmsnorm(h, norm_x_g[l]), mem, norm_mem_g[l],
                            w_xq[l], w_xk[l], w_xv[l], w_xo[l])
        h = h + _swiglu(_rmsnorm(h, norm_ffn_g[l]), w_gate[l], w_up[l], w_down[l])
    return _rmsnorm(h, norm_final_g)
```

```python
import functools
import math

import jax
import jax.numpy as jnp
from jax import lax
from jax.experimental import pallas as pl
from jax.experimental.pallas import tpu as pltpu

F32 = jnp.float32
BF16 = jnp.bfloat16
EPS = 1e-6
HIGHEST = lax.Precision.HIGHEST

LANES = 128
SUBLANES = 8
DN_CHUNK = 64
S5_CHUNK = 64
X_HEADS = 4
VMEM_LIMIT = 56 * 1024 * 1024


def _bdot(a, b):
    return jnp.dot(a.astype(BF16), b.astype(BF16), preferred_element_type=F32)


def _fdot(a, b):
    return jnp.dot(a, b, precision=HIGHEST, preferred_element_type=F32)


def _rms(x, g):
    return x * lax.rsqrt(jnp.mean(x * x, axis=-1, keepdims=True) + EPS) * g


def _params(*sem):
    return pltpu.CompilerParams(dimension_semantics=sem, vmem_limit_bytes=VMEM_LIMIT)


def _in_proj_body(x_ref, g_ref, w_ref, cw_ref, alog_ref, dtb_ref,
                  q_ref, k_ref, kt_ref, v_ref, z_ref, u_ref, gates_ref, gt_ref,
                  buf_ref, *, tm, dn_w, nh, dh, cw_n):
    t = pl.program_id(1)
    qkv_w = 3 * dn_w

    @pl.when(t == 0)
    def _():
        buf_ref[0:SUBLANES, :] = jnp.zeros((SUBLANES, qkv_w), F32)

    xn = _rms(x_ref[...], g_ref[...])
    p = jnp.dot(xn.astype(BF16), w_ref[...], preferred_element_type=F32)

    buf_ref[SUBLANES:SUBLANES + tm, :] = p[:, :qkv_w]
    cw = cw_ref[...]
    acc = cw[cw_n - 1:cw_n, :] * p[:, :qkv_w]
    for j in range(cw_n - 1):
        off = SUBLANES - (cw_n - 1) + j
        acc = acc + cw[j:j + 1, :] * buf_ref[off:off + tm, :]
    buf_ref[0:SUBLANES, :] = buf_ref[tm:tm + SUBLANES, :]
    qkv = acc * jax.nn.sigmoid(acc)

    q_parts, k_parts = [], []
    for h in range(nh):
        qh = qkv[:, h * dh:(h + 1) * dh]
        kh = qkv[:, dn_w + h * dh:dn_w + (h + 1) * dh]
        q_parts.append(qh * (lax.rsqrt(jnp.sum(qh * qh, -1, keepdims=True) + EPS) * (dh ** -0.5)))
        k_parts.append(kh * lax.rsqrt(jnp.sum(kh * kh, -1, keepdims=True) + EPS))
    kn = jnp.concatenate(k_parts, axis=1)
    q_ref[...] = jnp.concatenate(q_parts, axis=1)
    k_ref[...] = kn
    kt_ref[0] = kn.T
    v_ref[...] = qkv[:, 2 * dn_w:3 * dn_w]
    z_ref[...] = p[:, 3 * dn_w:4 * dn_w]
    u_ref[...] = p[:, 4 * dn_w:5 * dn_w]

    ab = p[:, 5 * dn_w:5 * dn_w + LANES]
    lane = lax.broadcasted_iota(jnp.int32, ab.shape, 1)
    g = -jnp.exp(alog_ref[...]) * jnp.logaddexp(ab + dtb_ref[...], 0.0)
    gates = jnp.where(lane < nh, g, jax.nn.sigmoid(ab))
    gates_ref[...] = gates
    gt_ref[0] = gates.T[0:SUBLANES, :]


def _in_proj(x2, bsz, seq, g, w_cat, conv_w, alog, dtb, *, nh, dh, tm):
    nt, d = x2.shape
    dn_w = nh * dh
    n_t = seq // tm
    cw_n = conv_w.shape[0]
    wcols = w_cat.shape[1]
    row = lambda b, t: (b * n_t + t, 0)
    const = lambda b, t: (0, 0)
    out_rows = lambda w: pl.BlockSpec((tm, w), row)
    body = functools.partial(_in_proj_body, tm=tm, dn_w=dn_w, nh=nh, dh=dh, cw_n=cw_n)
    return pl.pallas_call(
        body,
        grid=(bsz, n_t),
        in_specs=[
            pl.BlockSpec((tm, d), row),
            pl.BlockSpec((1, d), const),
            pl.BlockSpec((d, wcols), const),
            pl.BlockSpec((cw_n, 3 * dn_w), const),
            pl.BlockSpec((1, LANES), const),
            pl.BlockSpec((1, LANES), const),
        ],
        out_specs=[
            out_rows(dn_w), out_rows(dn_w),
            pl.BlockSpec((1, dn_w, tm), lambda b, t: (b, 0, t)),
            out_rows(dn_w), out_rows(dn_w), out_rows(dn_w), out_rows(LANES),
            pl.BlockSpec((1, SUBLANES, tm), lambda b, t: (b, 0, t)),
        ],
        out_shape=[
            jax.ShapeDtypeStruct((nt, dn_w), F32), jax.ShapeDtypeStruct((nt, dn_w), F32),
            jax.ShapeDtypeStruct((bsz, dn_w, seq), F32),
            jax.ShapeDtypeStruct((nt, dn_w), F32), jax.ShapeDtypeStruct((nt, dn_w), F32),
            jax.ShapeDtypeStruct((nt, dn_w), F32), jax.ShapeDtypeStruct((nt, LANES), F32),
            jax.ShapeDtypeStruct((bsz, SUBLANES, seq), F32),
        ],
        scratch_shapes=[pltpu.VMEM((tm + SUBLANES, 3 * dn_w), F32)],
        compiler_params=_params("arbitrary", "arbitrary"),
        name="in_proj",
    )(x2, g, w_cat, conv_w, alog, dtb)


def _delta_body(q_ref, k_ref, kt_ref, v_ref, z_ref, gates_ref, gt_ref, ng_ref,
                o_ref, s_ref, *, tb, nh, dh, ch):
    t = pl.program_id(1)

    @pl.when(t == 0)
    def _():
        s_ref[...] = jnp.zeros(s_ref.shape, F32)

    r = lax.broadcasted_iota(jnp.int32, (tb, tb), 0)
    c = lax.broadcasted_iota(jnp.int32, (tb, tb), 1)
    same = (r // ch) == (c // ch)
    lower_bd = jnp.where(same & (c <= r), 1.0, 0.0).astype(F32)
    upper_bd = jnp.where(same & (r <= c), 1.0, 0.0).astype(F32)
    gates = gates_ref[...]
    gcum = _fdot(lower_bd, gates)
    gcum_t = _fdot(gt_ref[0], upper_bd)

    ri = lax.broadcasted_iota(jnp.int32, (ch, ch), 0)
    ci = lax.broadcasted_iota(jnp.int32, (ch, ch), 1)
    causal = ci <= ri
    strict = ci < ri
    eye = jnp.where(ci == ri, 1.0, 0.0).astype(F32)
    n_sq = max(1, (ch - 1).bit_length() - 1)

    for cidx in range(tb // ch):
        rows = slice(cidx * ch, (cidx + 1) * ch)
        outs = []
        for h in range(nh):
            hl = slice(h * dh, (h + 1) * dh)
            gcol = gcum[rows, h:h + 1]
            grow = gcum_t[h:h + 1, rows]
            beta = gates[rows, nh + h:nh + h + 1]
            qh = q_ref[rows, hl]
            kh = k_ref[rows, hl]
            vh = v_ref[rows, hl]
            kth = kt_ref[0, hl, rows]
            decay = jnp.exp(jnp.where(causal, gcol - grow, -jnp.inf))
            kb = kh * beta
            kk = _bdot(kb, kth)
            qk = _bdot(qh, kth)
            pw = -jnp.where(strict, kk * decay, 0.0)
            tinv = eye + pw
            for _ in range(n_sq):
                pw = _fdot(pw, pw)
                tinv = tinv + _fdot(tinv, pw)
            attn = qk * decay
            egc = jnp.exp(gcol)
            u = _fdot(tinv, vh * beta)
            w = _fdot(tinv, kb * egc)
            s = s_ref[h]
            v_new = u - _bdot(w, s)
            o = _bdot(qh * egc, s) + _bdot(attn, v_new)
            g_last = gcol[ch - 1:ch, :]
            kdec_t = kth * jnp.exp(g_last - grow)
            s_ref[h] = s * jnp.exp(g_last) + _bdot(kdec_t, v_new)
            o = o * lax.rsqrt(jnp.mean(o * o, -1, keepdims=True) + EPS)
            zh = z_ref[rows, hl]
            outs.append(o * ng_ref[...] * (zh * jax.nn.sigmoid(zh)))
        o_ref[rows, :] = jnp.concatenate(outs, axis=1)


def _delta(q, k, kt, v, z, gates, gt, ng, bsz, seq, *, nh, dh, tb):
    nt, dn_w = q.shape
    n_t = seq // tb
    row = lambda b, t: (b * n_t + t, 0)
    rows = pl.BlockSpec((tb, dn_w), row)
    body = functools.partial(_delta_body, tb=tb, nh=nh, dh=dh, ch=DN_CHUNK)
    return pl.pallas_call(
        body,
        grid=(bsz, n_t),
        in_specs=[
            rows, rows,
            pl.BlockSpec((1, dn_w, tb), lambda b, t: (b, 0, t)),
            rows, rows,
            pl.BlockSpec((tb, LANES), row),
            pl.BlockSpec((1, SUBLANES, tb), lambda b, t: (b, 0, t)),
            pl.BlockSpec((1, dh), lambda b, t: (0, 0)),
        ],
        out_specs=rows,
        out_shape=jax.ShapeDtypeStruct((nt, dn_w), F32),
        scratch_shapes=[pltpu.VMEM((nh, dh, dh), F32)],
        compiler_params=_params("arbitrary", "arbitrary"),
        name="delta",
    )(q, k, kt, v, z, gates, gt, ng)


def _s5_zoh_body(are_ref, aim_ref, ldt_ref, lre_ref, lim_ref, cre_ref, cim_ref):
    a_re = are_ref[...]
    a_im = aim_ref[...]
    dt = jnp.exp(ldt_ref[...])
    lre = a_re * dt
    lim = a_im * dt
    mag = jnp.exp(lre)
    ab_re = mag * jnp.cos(lim)
    ab_im = mag * jnp.sin(lim)
    den = a_re * a_re + a_im * a_im
    nr = ab_re - 1.0
    ni = ab_im
    lre_ref[...] = lre
    lim_ref[...] = lim
    cre_ref[...] = (nr * a_re + ni * a_im) / den
    cim_ref[...] = (ni * a_re - nr * a_im) / den


def _s5_tables_body(lre_ref, lim_ref, cor_ref, coi_ref, bre_ref, bim_ref,
                    pin_re, pin_im, pout_re, pout_im, al_re, al_im, bbre_ref, bbim_ref,
                    *, chunk, mid):
    lre = lre_ref[...]
    lim = lim_ref[...]
    n = lax.broadcasted_iota(jnp.int32, (chunk, 1), 0).astype(F32) - float(mid)
    m_out = jnp.exp(lre * n)
    pout_re[...] = m_out * jnp.cos(lim * n)
    pout_im[...] = m_out * jnp.sin(lim * n)
    m_in = jnp.exp(-(lre * n))
    pin_re[...] = m_in * jnp.cos(lim * n)
    pin_im[...] = -(m_in * jnp.sin(lim * n))
    m_al = jnp.exp(lre * float(chunk))
    al_re[...] = m_al * jnp.cos(lim * float(chunk))
    al_im[...] = m_al * jnp.sin(lim * float(chunk))
    cor = cor_ref[...]
    coi = coi_ref[...]
    b_re = bre_ref[...]
    b_im = bim_ref[...]
    bbre_ref[...] = cor * b_re - coi * b_im
    bbim_ref[...] = cor * b_im + coi * b_re


def _s5_prep(a_re, a_im, log_dt, b_re, b_im, *, chunk):
    g, p = a_re.shape
    cg = b_re.shape[-1]
    gp = g * p
    shp = jax.ShapeDtypeStruct((g, p), F32)
    lre, lim, cor, coi = pl.pallas_call(
        _s5_zoh_body, out_shape=[shp, shp, shp, shp], name="s5_zoh",
    )(a_re, a_im, log_dt.reshape(g, 1))
    tab = jax.ShapeDtypeStruct((chunk, gp), F32)
    one = jax.ShapeDtypeStruct((1, gp), F32)
    bb = jax.ShapeDtypeStruct((gp, cg), F32)
    body = functools.partial(_s5_tables_body, chunk=chunk, mid=chunk // 2)
    return pl.pallas_call(
        body, out_shape=[tab, tab, tab, tab, one, one, bb, bb], name="s5_tables",
    )(lre.reshape(1, gp), lim.reshape(1, gp), cor.reshape(gp, 1), coi.reshape(gp, 1),
      b_re.reshape(gp, cg), b_im.reshape(gp, cg))


def _s5_body(u_ref, wb_ref, wc_ref, pin_re, pin_im, pout_re, pout_im, al_re, al_im,
             d_ref, wglu_ref, bglu_ref, y_ref, carry_ref, xs_ref, *, ts, chunk, n_half):
    t = pl.program_id(1)

    @pl.when(t == 0)
    def _():
        carry_ref[...] = jnp.zeros(carry_ref.shape, F32)

    u = u_ref[...]
    ub = u.astype(BF16)
    hw = u.shape[1] // n_half
    sw = pin_re.shape[2]
    ri = lax.broadcasted_iota(jnp.int32, (chunk, chunk), 0)
    ci = lax.broadcasted_iota(jnp.int32, (chunk, chunk), 1)
    tri = jnp.where(ci <= ri, 1.0, 0.0).astype(BF16)
    ys = []
    for hf in range(n_half):
        bu = jnp.dot(ub[:, hf * hw:(hf + 1) * hw], wb_ref[hf], preferred_element_type=F32)
        pr, pi = pin_re[hf], pin_im[hf]
        qr, qi = pout_re[hf], pout_im[hf]
        ar, ai = al_re[hf], al_im[hf]
        cre = carry_ref[2 * hf:2 * hf + 1, :]
        cim = carry_ref[2 * hf + 1:2 * hf + 2, :]
        for cidx in range(ts // chunk):
            rows = slice(cidx * chunk, (cidx + 1) * chunk)
            bre = bu[rows, :sw]
            bim = bu[rows, sw:]
            sre = pr * bre - pi * bim
            sim = pr * bim + pi * bre
            cs = jnp.dot(tri, jnp.concatenate([sre, sim], axis=1).astype(BF16),
                         preferred_element_type=F32)
            tre = cs[:, :sw] + cre
            tim = cs[:, sw:] + cim
            xs_ref[rows, :sw] = (qr * tre - qi * tim).astype(BF16)
            xs_ref[rows, sw:] = (qr * tim + qi * tre).astype(BF16)
            lre = tre[chunk - 1:chunk, :]
            lim = tim[chunk - 1:chunk, :]
            cre = ar * lre - ai * lim
            cim = ar * lim + ai * lre
        carry_ref[2 * hf:2 * hf + 1, :] = cre
        carry_ref[2 * hf + 1:2 * hf + 2, :] = cim
        ys.append(jnp.dot(xs_ref[...], wc_ref[hf], preferred_element_type=F32))
    y = jnp.concatenate(ys, axis=1) + d_ref[...] * u
    y = jax.nn.gelu(y)
    gate = jax.nn.sigmoid(jnp.dot(y.astype(BF16), wglu_ref[...], preferred_element_type=F32)
                          + bglu_ref[...])
    y_ref[...] = y * gate


def _s5(u, wb, wc, tabs, d, wglu, bglu, bsz, seq, *, ts, chunk):
    nt, width = u.shape
    n_half = wb.shape[0]
    sw2 = wb.shape[2]
    sw = sw2 // 2
    n_t = seq // ts
    row = lambda b, t: (b * n_t + t, 0)
    c2 = lambda b, t: (0, 0)
    c3 = lambda b, t: (0, 0, 0)
    tab = pl.BlockSpec((n_half, chunk, sw), c3)
    one = pl.BlockSpec((n_half, 1, sw), c3)
    body = functools.partial(_s5_body, ts=ts, chunk=chunk, n_half=n_half)
    return pl.pallas_call(
        body,
        grid=(bsz, n_t),
        in_specs=[
            pl.BlockSpec((ts, width), row),
            pl.BlockSpec(wb.shape, c3),
            pl.BlockSpec(wc.shape, c3),
            tab, tab, tab, tab, one, one,
            pl.BlockSpec((1, width), c2),
            pl.BlockSpec(wglu.shape, c2),
            pl.BlockSpec((1, width), c2),
        ],
        out_specs=pl.BlockSpec((ts, width), row),
        out_shape=jax.ShapeDtypeStruct((nt, width), F32),
        scratch_shapes=[pltpu.VMEM((2 * n_half, sw), F32), pltpu.VMEM((ts, sw2), BF16)],
        compiler_params=_params("arbitrary", "arbitrary"),
        name="s5",
    )(u, wb, wc, *tabs, d, wglu, bglu)


def _mem_kv_body(m_ref, g_ref, wk_ref, wv_ref, kt_ref, v_ref):
    mn = _rms(m_ref[0], g_ref[...]).astype(BF16)
    k = jnp.dot(mn, wk_ref[...], preferred_element_type=F32)
    kt_ref[0] = k.T.astype(BF16)
    v_ref[0] = jnp.dot(mn, wv_ref[...], preferred_element_type=F32).astype(BF16)


def _mem_kv(mem, g, wk, wv):
    bsz, n_mem, d = mem.shape
    c2 = lambda b: (0, 0)
    return pl.pallas_call(
        _mem_kv_body,
        grid=(bsz,),
        in_specs=[
            pl.BlockSpec((1, n_mem, d), lambda b: (b, 0, 0)),
            pl.BlockSpec((1, d), c2),
            pl.BlockSpec((d, d), c2),
            pl.BlockSpec((d, d), c2),
        ],
        out_specs=[
            pl.BlockSpec((1, d, n_mem), lambda b: (b, 0, 0)),
            pl.BlockSpec((1, n_mem, d), lambda b: (b, 0, 0)),
        ],
        out_shape=[
            jax.ShapeDtypeStruct((bsz, d, n_mem), BF16),
            jax.ShapeDtypeStruct((bsz, n_mem, d), BF16),
        ],
        compiler_params=_params("arbitrary"),
        name="mem_kv",
    )(mem, g, wk, wv)


def _mix_attn_body(x_ref, o_ref, y_ref, wo_ref, gx_ref, wq_ref, kt_ref, v_ref, wxo_ref,
                   h_ref, *, xh):
    dn_w = o_ref.shape[1]
    mix = (jnp.dot(o_ref[...].astype(BF16), wo_ref[0:dn_w, :], preferred_element_type=F32)
           + jnp.dot(y_ref[...].astype(BF16), wo_ref[dn_w:, :], preferred_element_type=F32))
    h1 = x_ref[...] + mix
    hn = _rms(h1, gx_ref[...]).astype(BF16)
    q = jnp.dot(hn, wq_ref[...], preferred_element_type=F32).astype(BF16)
    d = q.shape[1]
    xd = d // xh
    scale = xd ** -0.5
    parts = []
    for h in range(xh):
        hl = slice(h * xd, (h + 1) * xd)
        s = jnp.dot(q[:, hl], kt_ref[0, hl, :], preferred_element_type=F32) * scale
        e = jnp.exp(s - jnp.max(s, axis=-1, keepdims=True))
        p = (e / jnp.sum(e, axis=-1, keepdims=True)).astype(BF16)
        parts.append(jnp.dot(p, v_ref[0, :, hl], preferred_element_type=F32).astype(BF16))
    att = jnp.concatenate(parts, axis=1)
    h_ref[...] = h1 + jnp.dot(att, wxo_ref[...], preferred_element_type=F32)


def _mix_attn(x2, o, y, w_out, gx, wq, kt, v, wxo, bsz, seq, *, tm):
    nt, d = x2.shape
    dn_w = o.shape[1]
    n_mem = v.shape[1]
    n_t = seq // tm
    row = lambda b, t: (b * n_t + t, 0)
    c2 = lambda b, t: (0, 0)
    body = functools.partial(_mix_attn_body, xh=X_HEADS)
    return pl.pallas_call(
        body,
        grid=(bsz, n_t),
        in_specs=[
            pl.BlockSpec((tm, d), row),
            pl.BlockSpec((tm, dn_w), row),
            pl.BlockSpec((tm, y.shape[1]), row),
            pl.BlockSpec(w_out.shape, c2),
            pl.BlockSpec((1, d), c2),
            pl.BlockSpec((d, d), c2),
            pl.BlockSpec((1, d, n_mem), lambda b, t: (b, 0, 0)),
            pl.BlockSpec((1, n_mem, d), lambda b, t: (b, 0, 0)),
            pl.BlockSpec((d, d), c2),
        ],
        out_specs=pl.BlockSpec((tm, d), row),
        out_shape=jax.ShapeDtypeStruct((nt, d), F32),
        compiler_params=_params("arbitrary", "arbitrary"),
        name="mix_attn",
    )(x2, o, y, w_out, gx, wq, kt, v, wxo)


def _ffn_body(h_ref, gf_ref, wg_ref, wu_ref, wd_ref, gl_ref, out_ref, *, final):
    h = h_ref[...]
    hn = _rms(h, gf_ref[...]).astype(BF16)
    gate = jnp.dot(hn, wg_ref[...], preferred_element_type=F32)
    up = jnp.dot(hn, wu_ref[...], preferred_element_type=F32)
    act = (gate * jax.nn.sigmoid(gate) * up).astype(BF16)
    h3 = h + jnp.dot(act, wd_ref[...], preferred_element_type=F32)
    out_ref[...] = _rms(h3, gl_ref[...]) if final else h3


def _ffn(h, gf, wg, wu, wd, gl, *, tm, final):
    nt, d = h.shape
    dff = wg.shape[1]
    c2 = lambda i: (0, 0)
    row = lambda i: (i, 0)
    return pl.pallas_call(
        functools.partial(_ffn_body, final=final),
        grid=(nt // tm,),
        in_specs=[
            pl.BlockSpec((tm, d), row),
            pl.BlockSpec((1, d), c2),
            pl.BlockSpec((d, dff), c2, pipeline_mode=pl.Buffered(1)),
            pl.BlockSpec((d, dff), c2, pipeline_mode=pl.Buffered(1)),
            pl.BlockSpec((dff, d), c2, pipeline_mode=pl.Buffered(1)),
            pl.BlockSpec((1, d), c2),
        ],
        out_specs=pl.BlockSpec((tm, d), row),
        out_shape=jax.ShapeDtypeStruct((nt, d), F32),
        compiler_params=_params("arbitrary"),
        name="ffn",
    )(h, gf, wg, wu, wd, gl)


def _block_diag(blocks):
    n, r, c = blocks.shape
    eye = jnp.eye(n, dtype=blocks.dtype)
    return (blocks[:, :, None, :] * eye[:, None, :, None]).reshape(n * r, n * c)


def _layer(h2, mem, bsz, seq, norm_mix_g, w_in, conv_w, dn_a_log, dn_dt_bias, dn_norm_g,
           s5_a_re, s5_a_im, s5_b_re, s5_b_im, s5_c_re, s5_c_im, s5_d, s5_log_dt,
           s5_w_glu, s5_b_glu, w_out, norm_x_g, norm_mem_g, w_xq, w_xk, w_xv, w_xo):
    d = h2.shape[1]
    nh = dn_a_log.shape[0]
    dh = dn_norm_g.shape[0]
    dn_w = nh * dh
    n_grp, n_state = s5_a_re.shape
    cg = s5_b_re.shape[-1]
    s5_w = n_grp * cg

    off_a = 4 * dn_w
    off_u = off_a + 2 * nh
    w_ab = jnp.pad(w_in[:, off_a:off_u], ((0, 0), (0, LANES - 2 * nh)))
    w_cat = jnp.concatenate([w_in[:, :off_a], w_in[:, off_u:], w_ab], axis=1).astype(BF16)
    alog = jnp.pad(dn_a_log, (0, LANES - nh)).reshape(1, LANES)
    dtb = jnp.pad(dn_dt_bias, (0, LANES - nh)).reshape(1, LANES)

    q, k, kt, v, z, u, gates, gt = _in_proj(
        h2, bsz, seq, norm_mix_g.reshape(1, d), w_cat, conv_w, alog, dtb, nh=nh, dh=dh, tm=256)
    o = _delta(q, k, kt, v, z, gates, gt, dn_norm_g.reshape(1, dh), bsz, seq, nh=nh, dh=dh, tb=128)

    pin_re, pin_im, pout_re, pout_im, al_re, al_im, bb_re, bb_im = _s5_prep(
        s5_a_re, s5_a_im, s5_log_dt, s5_b_re, s5_b_im, chunk=S5_CHUNK)
    n_half = 2
    gh = n_grp // n_half
    sw = gh * n_state
    split = lambda a: a.reshape(a.shape[0], n_half, sw).transpose(1, 0, 2)
    tabs = tuple(split(a) for a in (pin_re, pin_im, pout_re, pout_im, al_re, al_im))
    bbr = bb_re.reshape(n_half, gh, n_state, cg).transpose(0, 1, 3, 2)
    bbi = bb_im.reshape(n_half, gh, n_state, cg).transpose(0, 1, 3, 2)
    wb = jnp.stack([jnp.concatenate([_block_diag(bbr[i]), _block_diag(bbi[i])], axis=1)
                    for i in range(n_half)]).astype(BF16)
    ccr = s5_c_re.reshape(n_half, gh, cg, n_state).transpose(0, 1, 3, 2)
    cci = s5_c_im.reshape(n_half, gh, cg, n_state).transpose(0, 1, 3, 2)
    wc = jnp.stack([jnp.concatenate([_block_diag(ccr[i]), -_block_diag(cci[i])], axis=0)
                    for i in range(n_half)]).astype(BF16)
    y = _s5(u, wb, wc, tabs, s5_d.reshape(1, s5_w), s5_w_glu.astype(BF16),
            s5_b_glu.reshape(1, s5_w), bsz, seq, ts=256, chunk=S5_CHUNK)

    kt_mem, v_mem = _mem_kv(mem, norm_mem_g.reshape(1, d), w_xk.astype(BF16), w_xv.astype(BF16))
    return _mix_attn(h2, o, y, w_out.astype(BF16), norm_x_g.reshape(1, d), w_xq.astype(BF16),
                     kt_mem, v_mem, w_xo.astype(BF16), bsz, seq, tm=256)


def kernel(x, mem, norm_mix_g, w_in, conv_w, dn_a_log, dn_dt_bias, dn_norm_g, s5_a_re, s5_a_im,
           s5_b_re, s5_b_im, s5_c_re, s5_c_im, s5_d, s5_log_dt, s5_w_glu, s5_b_glu, w_out,
           norm_x_g, norm_mem_g, w_xq, w_xk, w_xv, w_xo, norm_ffn_g, w_gate, w_up, w_down,
           norm_final_g):
    bsz, seq, d = x.shape
    depth = w_in.shape[0]
    h = x.reshape(bsz * seq, d)
    for l in range(depth):
        h = _layer(h, mem, bsz, seq, norm_mix_g[l], w_in[l], conv_w[l], dn_a_log[l],
                   dn_dt_bias[l], dn_norm_g[l], s5_a_re[l], s5_a_im[l], s5_b_re[l], s5_b_im[l],
                   s5_c_re[l], s5_c_im[l], s5_d[l], s5_log_dt[l], s5_w_glu[l], s5_b_glu[l],
                   w_out[l], norm_x_g[l], norm_mem_g[l], w_xq[l], w_xk[l], w_xv[l], w_xo[l])
        h = _ffn(h, norm_ffn_g[l].reshape(1, d), w_gate[l].astype(BF16), w_up[l].astype(BF16),
                 w_down[l].astype(BF16), norm_final_g.reshape(1, d), tm=256, final=l == depth - 1)
    return h.reshape(bsz, seq, d)
```

```python
import functools
import math

import jax
import jax.numpy as jnp
from jax import lax
from jax.experimental import pallas as pl
from jax.experimental.pallas import tpu as pltpu

F32 = jnp.float32
BF16 = jnp.bfloat16
EPS = 1e-6
HIGHEST = lax.Precision.HIGHEST

LANES = 128
SUBLANES = 8
DN_CHUNK = 64
S5_CHUNK = 64
X_HEADS = 4
VMEM_LIMIT = 56 * 1024 * 1024


def _bdot(a, b):
    return jnp.dot(a.astype(BF16), b.astype(BF16), preferred_element_type=F32)


def _fdot(a, b):
    return jnp.dot(a, b, precision=HIGHEST, preferred_element_type=F32)


def _rms(x, g):
    return x * lax.rsqrt(jnp.mean(x * x, axis=-1, keepdims=True) + EPS) * g


def _params(*sem):
    return pltpu.CompilerParams(dimension_semantics=sem, vmem_limit_bytes=VMEM_LIMIT)


def _in_proj_body(x_ref, g_ref, w_ref, cw_ref, alog_ref, dtb_ref,
                  q_ref, k_ref, kt_ref, v_ref, z_ref, u_ref, gt_ref,
                  buf_ref, *, tm, dn_w, nh, dh, cw_n):
    t = pl.program_id(1)
    qkv_w = 3 * dn_w

    @pl.when(t == 0)
    def _():
        buf_ref[0:SUBLANES, :] = jnp.zeros((SUBLANES, qkv_w), F32)

    xn = _rms(x_ref[...], g_ref[...])
    p = jnp.dot(xn.astype(BF16), w_ref[...], preferred_element_type=F32)

    buf_ref[SUBLANES:SUBLANES + tm, :] = p[:, :qkv_w]
    cw = cw_ref[...]
    acc = cw[cw_n - 1:cw_n, :] * p[:, :qkv_w]
    for j in range(cw_n - 1):
        off = SUBLANES - (cw_n - 1) + j
        acc = acc + cw[j:j + 1, :] * buf_ref[off:off + tm, :]
    buf_ref[0:SUBLANES, :] = buf_ref[tm:tm + SUBLANES, :]
    qkv = acc * jax.nn.sigmoid(acc)

    q_parts, k_parts = [], []
    for h in range(nh):
        qh = qkv[:, h * dh:(h + 1) * dh]
        kh = qkv[:, dn_w + h * dh:dn_w + (h + 1) * dh]
        q_parts.append(qh * (lax.rsqrt(jnp.sum(qh * qh, -1, keepdims=True) + EPS) * (dh ** -0.5)))
        k_parts.append(kh * lax.rsqrt(jnp.sum(kh * kh, -1, keepdims=True) + EPS))
    kn = jnp.concatenate(k_parts, axis=1)
    q_ref[...] = jnp.concatenate(q_parts, axis=1)
    k_ref[...] = kn
    kt_ref[0] = kn.T
    v_ref[...] = qkv[:, 2 * dn_w:3 * dn_w]
    z_ref[...] = p[:, 3 * dn_w:4 * dn_w]
    u_ref[...] = p[:, 4 * dn_w:5 * dn_w]

    ab = p[:, 5 * dn_w:5 * dn_w + LANES]
    lane = lax.broadcasted_iota(jnp.int32, ab.shape, 1)
    g = -jnp.exp(alog_ref[...]) * jnp.logaddexp(ab + dtb_ref[...], 0.0)
    gates = jnp.where(lane < nh, g, jax.nn.sigmoid(ab))
    gt_ref[0] = gates.T[0:SUBLANES, :]


def _in_proj(x2, bsz, seq, g, w_cat, conv_w, alog, dtb, *, nh, dh, tm):
    nt, d = x2.shape
    dn_w = nh * dh
    n_t = seq // tm
    cw_n = conv_w.shape[0]
    wcols = w_cat.shape[1]
    row = lambda b, t: (b * n_t + t, 0)
    const = lambda b, t: (0, 0)
    out_rows = lambda w: pl.BlockSpec((tm, w), row)
    body = functools.partial(_in_proj_body, tm=tm, dn_w=dn_w, nh=nh, dh=dh, cw_n=cw_n)
    return pl.pallas_call(
        body,
        grid=(bsz, n_t),
        in_specs=[
            pl.BlockSpec((tm, d), row),
            pl.BlockSpec((1, d), const),
            pl.BlockSpec((d, wcols), const),
            pl.BlockSpec((cw_n, 3 * dn_w), const),
            pl.BlockSpec((1, LANES), const),
            pl.BlockSpec((1, LANES), const),
        ],
        out_specs=[
            out_rows(dn_w), out_rows(dn_w),
            pl.BlockSpec((1, dn_w, tm), lambda b, t: (b, 0, t)),
            out_rows(dn_w), out_rows(dn_w), out_rows(dn_w),
            pl.BlockSpec((1, SUBLANES, tm), lambda b, t: (b, 0, t)),
        ],
        out_shape=[
            jax.ShapeDtypeStruct((nt, dn_w), F32), jax.ShapeDtypeStruct((nt, dn_w), F32),
            jax.ShapeDtypeStruct((bsz, dn_w, seq), F32),
            jax.ShapeDtypeStruct((nt, dn_w), F32), jax.ShapeDtypeStruct((nt, dn_w), F32),
            jax.ShapeDtypeStruct((nt, dn_w), F32),
            jax.ShapeDtypeStruct((bsz, SUBLANES, seq), F32),
        ],
        scratch_shapes=[pltpu.VMEM((tm + SUBLANES, 3 * dn_w), F32)],
        compiler_params=_params("arbitrary", "arbitrary"),
        name="in_proj",
    )(x2, g, w_cat, conv_w, alog, dtb)


def _delta_body(q_ref, k_ref, kt_ref, v_ref, z_ref, gt_ref, ng_ref,
                o_ref, s_ref, *, tb, nh, dh, ch):
    t = pl.program_id(1)

    @pl.when(t == 0)
    def _():
        s_ref[...] = jnp.zeros(s_ref.shape, F32)

    nc = tb // ch
    r = lax.broadcasted_iota(jnp.int32, (tb, tb), 0)
    c = lax.broadcasted_iota(jnp.int32, (tb, tb), 1)
    same = (r // ch) == (c // ch)
    causal_bd = same & (c <= r)
    strict_bd = same & (c < r)
    gt = gt_ref[0]
    gcum_t = _fdot(gt, jnp.where(same & (r <= c), 1.0, 0.0))
    gtot_t = _fdot(gt, jnp.where(same, 1.0, 0.0))
    gcum = gcum_t.T
    gates = gt.T

    def pack(x):
        acc = x[0:ch, :]
        for i in range(1, nc):
            acc = acc + x[i * ch:(i + 1) * ch, :]
        return acc

    def unpack(x):
        return jnp.where(same, jnp.concatenate([x] * nc, axis=0), 0.0).astype(BF16)

    n_sq = max(1, (ch - 1).bit_length() - 1)

    u_all, w_all, qg_all, attn_all, kdt_all = [], [], [], [], []
    for h in range(nh):
        hl = slice(h * dh, (h + 1) * dh)
        gcol = gcum[:, h:h + 1]
        grow = gcum_t[h:h + 1, :]
        beta = gates[:, nh + h:nh + h + 1]
        qh = q_ref[:, hl]
        kth = kt_ref[0, hl, :]
        kb = k_ref[:, hl] * beta
        gram = _bdot(jnp.concatenate([kb, qh], axis=0), kth)
        decay = jnp.exp(jnp.where(causal_bd, gcol - grow, -jnp.inf))
        p_bd = -jnp.where(strict_bd, gram[:tb] * decay, 0.0)
        attn_all.append(gram[tb:] * decay)
        ppk = pack(p_bd)
        apk = ppk
        ppk = jnp.dot(ppk.astype(BF16), p_bd.astype(BF16), preferred_element_type=F32)
        for j in range(1, n_sq + 1):
            bd = unpack(ppk)
            if j < n_sq:
                res = jnp.dot(jnp.concatenate([apk, ppk], axis=0).astype(BF16), bd,
                              preferred_element_type=F32)
                apk = apk + ppk + res[:ch]
                ppk = res[ch:]
            else:
                apk = apk + ppk + jnp.dot(apk.astype(BF16), bd, preferred_element_type=F32)
        egc = jnp.exp(gcol)
        rhs = jnp.concatenate([v_ref[:, hl] * beta, kb * egc], axis=1)
        uw = rhs + jnp.dot(unpack(apk), rhs.astype(BF16), preferred_element_type=F32)
        u_all.append(uw[:, :dh])
        w_all.append(uw[:, dh:])
        qg_all.append(qh * egc)
        kdt_all.append(kth * jnp.exp(gtot_t[h:h + 1, :] - grow))

    s_all = [s_ref[h] for h in range(nh)]
    for cidx in range(nc):
        rows = slice(cidx * ch, (cidx + 1) * ch)
        outs = []
        for h in range(nh):
            hl = slice(h * dh, (h + 1) * dh)
            s = s_all[h]
            v_new = u_all[h][rows] - _bdot(w_all[h][rows], s)
            o = _bdot(qg_all[h][rows], s) + _bdot(attn_all[h][rows, rows], v_new)
            a_last = jnp.exp(gtot_t[h:h + 1, cidx * ch:cidx * ch + 1])
            s_all[h] = s * a_last + _bdot(kdt_all[h][:, rows], v_new)
            o = o * lax.rsqrt(jnp.mean(o * o, -1, keepdims=True) + EPS)
            zh = z_ref[rows, hl]
            outs.append(o * ng_ref[...] * (zh * jax.nn.sigmoid(zh)))
        o_ref[rows, :] = jnp.concatenate(outs, axis=1)
    for h in range(nh):
        s_ref[h] = s_all[h]


def _delta(q, k, kt, v, z, gt, ng, bsz, seq, *, nh, dh, tb):
    nt, dn_w = q.shape
    n_t = seq // tb
    row = lambda b, t: (b * n_t + t, 0)
    rows = pl.BlockSpec((tb, dn_w), row)
    body = functools.partial(_delta_body, tb=tb, nh=nh, dh=dh, ch=DN_CHUNK)
    return pl.pallas_call(
        body,
        grid=(bsz, n_t),
        in_specs=[
            rows, rows,
            pl.BlockSpec((1, dn_w, tb), lambda b, t: (b, 0, t)),
            rows, rows,
            pl.BlockSpec((1, SUBLANES, tb), lambda b, t: (b, 0, t)),
            pl.BlockSpec((1, dh), lambda b, t: (0, 0)),
        ],
        out_specs=rows,
        out_shape=jax.ShapeDtypeStruct((nt, dn_w), F32),
        scratch_shapes=[pltpu.VMEM((nh, dh, dh), F32)],
        compiler_params=_params("arbitrary", "arbitrary"),
        name="delta",
    )(q, k, kt, v, z, gt, ng)


def _s5_zoh_body(are_ref, aim_ref, ldt_ref, lre_ref, lim_ref, cre_ref, cim_ref):
    a_re = are_ref[...]
    a_im = aim_ref[...]
    dt = jnp.exp(ldt_ref[...])
    lre = a_re * dt
    lim = a_im * dt
    mag = jnp.exp(lre)
    ab_re = mag * jnp.cos(lim)
    ab_im = mag * jnp.sin(lim)
    den = a_re * a_re + a_im * a_im
    nr = ab_re - 1.0
    ni = ab_im
    lre_ref[...] = lre
    lim_ref[...] = lim
    cre_ref[...] = (nr * a_re + ni * a_im) / den
    cim_ref[...] = (ni * a_re - nr * a_im) / den


def _s5_tables_body(lre_ref, lim_ref, cor_ref, coi_ref, bre_ref, bim_ref,
                    pin_re, pin_im, pout_re, pout_im, al_re, al_im, bbre_ref, bbim_ref,
                    *, chunk, mid):
    lre = lre_ref[...]
    lim = lim_ref[...]
    n = lax.broadcasted_iota(jnp.int32, (chunk, 1), 0).astype(F32) - float(mid)
    m_out = jnp.exp(lre * n)
    pout_re[...] = m_out * jnp.cos(lim * n)
    pout_im[...] = m_out * jnp.sin(lim * n)
    m_in = jnp.exp(-(lre * n))
    pin_re[...] = m_in * jnp.cos(lim * n)
    pin_im[...] = -(m_in * jnp.sin(lim * n))
    m_al = jnp.exp(lre * float(chunk))
    al_re[...] = m_al * jnp.cos(lim * float(chunk))
    al_im[...] = m_al * jnp.sin(lim * float(chunk))
    cor = cor_ref[...]
    coi = coi_ref[...]
    b_re = bre_ref[...]
    b_im = bim_ref[...]
    bbre_ref[...] = cor * b_re - coi * b_im
    bbim_ref[...] = cor * b_im + coi * b_re


def _s5_prep(a_re, a_im, log_dt, b_re, b_im, *, chunk):
    g, p = a_re.shape
    cg = b_re.shape[-1]
    gp = g * p
    shp = jax.ShapeDtypeStruct((g, p), F32)
    lre, lim, cor, coi = pl.pallas_call(
        _s5_zoh_body, out_shape=[shp, shp, shp, shp], name="s5_zoh",
    )(a_re, a_im, log_dt.reshape(g, 1))
    tab = jax.ShapeDtypeStruct((chunk, gp), F32)
    one = jax.ShapeDtypeStruct((1, gp), F32)
    bb = jax.ShapeDtypeStruct((gp, cg), F32)
    body = functools.partial(_s5_tables_body, chunk=chunk, mid=chunk // 2)
    return pl.pallas_call(
        body, out_shape=[tab, tab, tab, tab, one, one, bb, bb], name="s5_tables",
    )(lre.reshape(1, gp), lim.reshape(1, gp), cor.reshape(gp, 1), coi.reshape(gp, 1),
      b_re.reshape(gp, cg), b_im.reshape(gp, cg))


def _s5_body(u_ref, wb_ref, wc_ref, pin_re, pin_im, pout_re, pout_im, al_re, al_im,
             d_ref, wglu_ref, bglu_ref, y_ref, carry_ref, xs_ref, *, ts, chunk, n_half):
    t = pl.program_id(1)

    @pl.when(t == 0)
    def _():
        carry_ref[...] = jnp.zeros(carry_ref.shape, F32)

    u = u_ref[...]
    ub = u.astype(BF16)
    hw = u.shape[1] // n_half
    sw = pin_re.shape[2]
    ri = lax.broadcasted_iota(jnp.int32, (chunk, chunk), 0)
    ci = lax.broadcasted_iota(jnp.int32, (chunk, chunk), 1)
    tri = jnp.where(ci <= ri, 1.0, 0.0).astype(BF16)
    ys = []
    for hf in range(n_half):
        bu = jnp.dot(ub[:, hf * hw:(hf + 1) * hw], wb_ref[hf], preferred_element_type=F32)
        pr, pi = pin_re[hf], pin_im[hf]
        qr, qi = pout_re[hf], pout_im[hf]
        ar, ai = al_re[hf], al_im[hf]
        cre = carry_ref[2 * hf:2 * hf + 1, :]
        cim = carry_ref[2 * hf + 1:2 * hf + 2, :]
        for cidx in range(ts // chunk):
            rows = slice(cidx * chunk, (cidx + 1) * chunk)
            bre = bu[rows, :sw]
            bim = bu[rows, sw:]
            sre = pr * bre - pi * bim
            sim = pr * bim + pi * bre
            cs = jnp.dot(tri, jnp.concatenate([sre, sim], axis=1).astype(BF16),
                         preferred_element_type=F32)
            tre = cs[:, :sw] + cre
            tim = cs[:, sw:] + cim
            xs_ref[rows, :sw] = (qr * tre - qi * tim).astype(BF16)
            xs_ref[rows, sw:] = (qr * tim + qi * tre).astype(BF16)
            lre = tre[chunk - 1:chunk, :]
            lim = tim[chunk - 1:chunk, :]
            cre = ar * lre - ai * lim
            cim = ar * lim + ai * lre
        carry_ref[2 * hf:2 * hf + 1, :] = cre
        carry_ref[2 * hf + 1:2 * hf + 2, :] = cim
        ys.append(jnp.dot(xs_ref[...], wc_ref[hf], preferred_element_type=F32))
    y = jnp.concatenate(ys, axis=1) + d_ref[...] * u
    y = jax.nn.gelu(y)
    gate = jax.nn.sigmoid(jnp.dot(y.astype(BF16), wglu_ref[...], preferred_element_type=F32)
                          + bglu_ref[...])
    y_ref[...] = y * gate


def _s5(u, wb, wc, tabs, d, wglu, bglu, bsz, seq, *, ts, chunk):
    nt, width = u.shape
    n_half = wb.shape[0]
    sw2 = wb.shape[2]
    sw = sw2 // 2
    n_t = seq // ts
    row = lambda b, t: (b * n_t + t, 0)
    c2 = lambda b, t: (0, 0)
    c3 = lambda b, t: (0, 0, 0)
    tab = pl.BlockSpec((n_half, chunk, sw), c3)
    one = pl.BlockSpec((n_half, 1, sw), c3)
    body = functools.partial(_s5_body, ts=ts, chunk=chunk, n_half=n_half)
    return pl.pallas_call(
        body,
        grid=(bsz, n_t),
        in_specs=[
            pl.BlockSpec((ts, width), row),
            pl.BlockSpec(wb.shape, c3),
            pl.BlockSpec(wc.shape, c3),
            tab, tab, tab, tab, one, one,
            pl.BlockSpec((1, width), c2),
            pl.BlockSpec(wglu.shape, c2),
            pl.BlockSpec((1, width), c2),
        ],
        out_specs=pl.BlockSpec((ts, width), row),
        out_shape=jax.ShapeDtypeStruct((nt, width), F32),
        scratch_shapes=[pltpu.VMEM((2 * n_half, sw), F32), pltpu.VMEM((ts, sw2), BF16)],
        compiler_params=_params("arbitrary", "arbitrary"),
        name="s5",
    )(u, wb, wc, *tabs, d, wglu, bglu)


def _mem_kv_body(m_ref, g_ref, wk_ref, wv_ref, kt_ref, v_ref):
    mn = _rms(m_ref[0], g_ref[...]).astype(BF16)
    k = jnp.dot(mn, wk_ref[...], preferred_element_type=F32)
    kt_ref[0] = k.T.astype(BF16)
    v_ref[0] = jnp.dot(mn, wv_ref[...], preferred_element_type=F32).astype(BF16)


def _mem_kv(mem, g, wk, wv):
    bsz, n_mem, d = mem.shape
    c2 = lambda b: (0, 0)
    return pl.pallas_call(
        _mem_kv_body,
        grid=(bsz,),
        in_specs=[
            pl.BlockSpec((1, n_mem, d), lambda b: (b, 0, 0)),
            pl.BlockSpec((1, d), c2),
            pl.BlockSpec((d, d), c2),
            pl.BlockSpec((d, d), c2),
        ],
        out_specs=[
            pl.BlockSpec((1, d, n_mem), lambda b: (b, 0, 0)),
            pl.BlockSpec((1, n_mem, d), lambda b: (b, 0, 0)),
        ],
        out_shape=[
            jax.ShapeDtypeStruct((bsz, d, n_mem), BF16),
            jax.ShapeDtypeStruct((bsz, n_mem, d), BF16),
        ],
        compiler_params=_params("arbitrary"),
        name="mem_kv",
    )(mem, g, wk, wv)


def _mix_attn_body(x_ref, o_ref, y_ref, wo_ref, gx_ref, wq_ref, kt_ref, v_ref, wxo_ref,
                   h_ref, *, xh):
    dn_w = o_ref.shape[1]
    mix = (jnp.dot(o_ref[...].astype(BF16), wo_ref[0:dn_w, :], preferred_element_type=F32)
           + jnp.dot(y_ref[...].astype(BF16), wo_ref[dn_w:, :], preferred_element_type=F32))
    h1 = x_ref[...] + mix
    hn = _rms(h1, gx_ref[...]).astype(BF16)
    q = jnp.dot(hn, wq_ref[...], preferred_element_type=F32).astype(BF16)
    d = q.shape[1]
    xd = d // xh
    scale = xd ** -0.5
    parts = []
    for h in range(xh):
        hl = slice(h * xd, (h + 1) * xd)
        s = jnp.dot(q[:, hl], kt_ref[0, hl, :], preferred_element_type=F32) * scale
        e = jnp.exp(s - jnp.max(s, axis=-1, keepdims=True))
        p = (e / jnp.sum(e, axis=-1, keepdims=True)).astype(BF16)
        parts.append(jnp.dot(p, v_ref[0, :, hl], preferred_element_type=F32).astype(BF16))
    att = jnp.concatenate(parts, axis=1)
    h_ref[...] = h1 + jnp.dot(att, wxo_ref[...], preferred_element_type=F32)


def _mix_attn(x2, o, y, w_out, gx, wq, kt, v, wxo, bsz, seq, *, tm):
    nt, d = x2.shape
    dn_w = o.shape[1]
    n_mem = v.shape[1]
    n_t = seq // tm
    row = lambda b, t: (b * n_t + t, 0)
    c2 = lambda b, t: (0, 0)
    body = functools.partial(_mix_attn_body, xh=X_HEADS)
    return pl.pallas_call(
        body,
        grid=(bsz, n_t),
        in_specs=[
            pl.BlockSpec((tm, d), row),
            pl.BlockSpec((tm, dn_w), row),
            pl.BlockSpec((tm, y.shape[1]), row),
            pl.BlockSpec(w_out.shape, c2),
            pl.BlockSpec((1, d), c2),
            pl.BlockSpec((d, d), c2),
            pl.BlockSpec((1, d, n_mem), lambda b, t: (b, 0, 0)),
            pl.BlockSpec((1, n_mem, d), lambda b, t: (b, 0, 0)),
            pl.BlockSpec((d, d), c2),
        ],
        out_specs=pl.BlockSpec((tm, d), row),
        out_shape=jax.ShapeDtypeStruct((nt, d), F32),
        compiler_params=_params("arbitrary", "arbitrary"),
        name="mix_attn",
    )(x2, o, y, w_out, gx, wq, kt, v, wxo)


def _ffn_body(h_ref, gf_ref, wg_ref, wu_ref, wd_ref, gl_ref, out_ref, *, final):
    h = h_ref[...]
    hn = _rms(h, gf_ref[...]).astype(BF16)
    gate = jnp.dot(hn, wg_ref[...], preferred_element_type=F32)
    up = jnp.dot(hn, wu_ref[...], preferred_element_type=F32)
    act = (gate * jax.nn.sigmoid(gate) * up).astype(BF16)
    h3 = h + jnp.dot(act, wd_ref[...], preferred_element_type=F32)
    out_ref[...] = _rms(h3, gl_ref[...]) if final else h3


def _ffn(h, gf, wg, wu, wd, gl, *, tm, final):
    nt, d = h.shape
    dff = wg.shape[1]
    c2 = lambda i: (0, 0)
    row = lambda i: (i, 0)
    return pl.pallas_call(
        functools.partial(_ffn_body, final=final),
        grid=(nt // tm,),
        in_specs=[
            pl.BlockSpec((tm, d), row),
            pl.BlockSpec((1, d), c2),
            pl.BlockSpec((d, dff), c2, pipeline_mode=pl.Buffered(1)),
            pl.BlockSpec((d, dff), c2, pipeline_mode=pl.Buffered(1)),
            pl.BlockSpec((dff, d), c2, pipeline_mode=pl.Buffered(1)),
            pl.BlockSpec((1, d), c2),
        ],
        out_specs=pl.BlockSpec((tm, d), row),
        out_shape=jax.ShapeDtypeStruct((nt, d), F32),
        compiler_params=_params("arbitrary"),
        name="ffn",
    )(h, gf, wg, wu, wd, gl)


def _block_diag(blocks):
    n, r, c = blocks.shape
    eye = jnp.eye(n, dtype=blocks.dtype)
    return (blocks[:, :, None, :] * eye[:, None, :, None]).reshape(n * r, n * c)


def _layer(h2, mem, bsz, seq, norm_mix_g, w_in, conv_w, dn_a_log, dn_dt_bias, dn_norm_g,
           s5_a_re, s5_a_im, s5_b_re, s5_b_im, s5_c_re, s5_c_im, s5_d, s5_log_dt,
           s5_w_glu, s5_b_glu, w_out, norm_x_g, norm_mem_g, w_xq, w_xk, w_xv, w_xo):
    d = h2.shape[1]
    nh = dn_a_log.shape[0]
    dh = dn_norm_g.shape[0]
    dn_w = nh * dh
    n_grp, n_state = s5_a_re.shape
    cg = s5_b_re.shape[-1]
    s5_w = n_grp * cg

    off_a = 4 * dn_w
    off_u = off_a + 2 * nh
    w_ab = jnp.pad(w_in[:, off_a:off_u], ((0, 0), (0, LANES - 2 * nh)))
    w_cat = jnp.concatenate([w_in[:, :off_a], w_in[:, off_u:], w_ab], axis=1).astype(BF16)
    alog = jnp.pad(dn_a_log, (0, LANES - nh)).reshape(1, LANES)
    dtb = jnp.pad(dn_dt_bias, (0, LANES - nh)).reshape(1, LANES)

    q, k, kt, v, z, u, gt = _in_proj(
        h2, bsz, seq, norm_mix_g.reshape(1, d), w_cat, conv_w, alog, dtb, nh=nh, dh=dh, tm=256)
    o = _delta(q, k, kt, v, z, gt, dn_norm_g.reshape(1, dh), bsz, seq, nh=nh, dh=dh, tb=256)

    pin_re, pin_im, pout_re, pout_im, al_re, al_im, bb_re, bb_im = _s5_prep(
        s5_a_re, s5_a_im, s5_log_dt, s5_b_re, s5_b_im, chunk=S5_CHUNK)
    n_half = 2
    gh = n_grp // n_half
    sw = gh * n_state
    split = lambda a: a.reshape(a.shape[0], n_half, sw).transpose(1, 0, 2)
    tabs = tuple(split(a) for a in (pin_re, pin_im, pout_re, pout_im, al_re, al_im))
    bbr = bb_re.reshape(n_half, gh, n_state, cg).transpose(0, 1, 3, 2)
    bbi = bb_im.reshape(n_half, gh, n_state, cg).transpose(0, 1, 3, 2)
    wb = jnp.stack([jnp.concatenate([_block_diag(bbr[i]), _block_diag(bbi[i])], axis=1)
                    for i in range(n_half)]).astype(BF16)
    ccr = s5_c_re.reshape(n_half, gh, cg, n_state).transpose(0, 1, 3, 2)
    cci = s5_c_im.reshape(n_half, gh, cg, n_state).transpose(0, 1, 3, 2)
    wc = jnp.stack([jnp.concatenate([_block_diag(ccr[i]), -_block_diag(cci[i])], axis=0)
                    for i in range(n_half)]).astype(BF16)
    y = _s5(u, wb, wc, tabs, s5_d.reshape(1, s5_w), s5_w_glu.astype(BF16),
            s5_b_glu.reshape(1, s5_w), bsz, seq, ts=256, chunk=S5_CHUNK)

    kt_mem, v_mem = _mem_kv(mem, norm_mem_g.reshape(1, d), w_xk.astype(BF16), w_xv.astype(BF16))
    return _mix_attn(h2, o, y, w_out.astype(BF16), norm_x_g.reshape(1, d), w_xq.astype(BF16),
                     kt_mem, v_mem, w_xo.astype(BF16), bsz, seq, tm=256)


def kernel(x, mem, norm_mix_g, w_in, conv_w, dn_a_log, dn_dt_bias, dn_norm_g, s5_a_re, s5_a_im,
           s5_b_re, s5_b_im, s5_c_re, s5_c_im, s5_d, s5_log_dt, s5_w_glu, s5_b_glu, w_out,
           norm_x_g, norm_mem_g, w_xq, w_xk, w_xv, w_xo, norm_ffn_g, w_gate, w_up, w_down,
           norm_final_g):
    bsz, seq, d = x.shape
    depth = w_in.shape[0]
    h = x.reshape(bsz * seq, d)
    for l in range(depth):
        h = _layer(h, mem, bsz, seq, norm_mix_g[l], w_in[l], conv_w[l], dn_a_log[l],
                   dn_dt_bias[l], dn_norm_g[l], s5_a_re[l], s5_a_im[l], s5_b_re[l], s5_b_im[l],
                   s5_c_re[l], s5_c_im[l], s5_d[l], s5_log_dt[l], s5_w_glu[l], s5_b_glu[l],
                   w_out[l], norm_x_g[l], norm_mem_g[l], w_xq[l], w_xk[l], w_xv[l], w_xo[l])
        h = _ffn(h, norm_ffn_g[l].reshape(1, d), w_gate[l].astype(BF16), w_up[l].astype(BF16),
                 w_down[l].astype(BF16), norm_final_g.reshape(1, d), tm=256, final=l == depth - 1)
    return h.reshape(bsz, seq, d)
```

```python
import functools
import math

import jax
import jax.numpy as jnp
from jax import lax
from jax.experimental import pallas as pl
from jax.experimental.pallas import tpu as pltpu

F32 = jnp.float32
BF16 = jnp.bfloat16
EPS = 1e-6
HIGHEST = lax.Precision.HIGHEST

LANES = 128
SUBLANES = 8
DN_CHUNK = 64
S5_CHUNK = 64
X_HEADS = 4
VMEM_LIMIT = 56 * 1024 * 1024


def _bdot(a, b):
    return jnp.dot(a.astype(BF16), b.astype(BF16), preferred_element_type=F32)


def _fdot(a, b):
    return jnp.dot(a, b, precision=HIGHEST, preferred_element_type=F32)


def _rms(x, g):
    return x * lax.rsqrt(jnp.mean(x * x, axis=-1, keepdims=True) + EPS) * g


def _params(*sem):
    return pltpu.CompilerParams(dimension_semantics=sem, vmem_limit_bytes=VMEM_LIMIT)


def _in_proj_body(x_ref, g_ref, w_ref, cw_ref, alog_ref, dtb_ref,
                  q_ref, k_ref, kt_ref, v_ref, z_ref, u_ref, gt_ref,
                  buf_ref, *, tm, dn_w, nh, dh, cw_n):
    t = pl.program_id(1)
    qkv_w = 3 * dn_w

    @pl.when(t == 0)
    def _():
        buf_ref[:, 0:SUBLANES, :] = jnp.zeros((qkv_w // LANES, SUBLANES, LANES), F32)

    xnb = _rms(x_ref[...], g_ref[...]).astype(BF16)
    gw = 2 * dh

    def conv_silu(p, cols):
        outs = []
        for i in range(gw // LANES):
            c0 = cols.start + i * LANES
            s = c0 // LANES
            pc = p[:, i * LANES:(i + 1) * LANES]
            buf_ref[s, SUBLANES:SUBLANES + tm, :] = pc
            acc = cw_ref[cw_n - 1:cw_n, c0:c0 + LANES] * pc
            for j in range(cw_n - 1):
                off = SUBLANES - (cw_n - 1) + j
                acc = acc + cw_ref[j:j + 1, c0:c0 + LANES] * buf_ref[s, off:off + tm, :]
            buf_ref[s, 0:SUBLANES, :] = buf_ref[s, tm:tm + SUBLANES, :]
            outs.append(acc * jax.nn.sigmoid(acc))
        return jnp.concatenate(outs, axis=1)

    def l2n(a, scale):
        parts = []
        for i in range(gw // dh):
            ah = a[:, i * dh:(i + 1) * dh]
            parts.append(ah * (lax.rsqrt(jnp.sum(ah * ah, -1, keepdims=True) + EPS) * scale))
        return jnp.concatenate(parts, axis=1)

    def ep_q(p, c0):
        q_ref[:, c0:c0 + gw] = l2n(conv_silu(p, slice(c0, c0 + gw)), dh ** -0.5)

    def ep_k(p, c0):
        kn = l2n(conv_silu(p, slice(dn_w + c0, dn_w + c0 + gw)), 1.0)
        k_ref[:, c0:c0 + gw] = kn
        kt_ref[0, c0:c0 + gw, :] = kn.T

    def ep_v(p, c0):
        v_ref[:, c0:c0 + gw] = conv_silu(p, slice(2 * dn_w + c0, 2 * dn_w + c0 + gw))

    def ep_z(p, c0):
        z_ref[:, c0:c0 + gw] = p

    def ep_u(p, c0):
        u_ref[:, c0:c0 + gw] = p

    def ep_gates(ab, c0):
        lane = lax.broadcasted_iota(jnp.int32, ab.shape, 1)
        g = -jnp.exp(alog_ref[...]) * jnp.logaddexp(ab + dtb_ref[...], 0.0)
        gates = jnp.where(lane < nh, g, jax.nn.sigmoid(ab))
        gt_ref[0] = gates.T[0:SUBLANES, :]

    groups = []
    for i, ep in enumerate((ep_q, ep_k, ep_v, ep_z, ep_u)):
        groups += [(ep, i * dn_w + c0, gw, c0) for c0 in range(0, dn_w, gw)]
    groups.append((ep_gates, 5 * dn_w, LANES, 0))

    prev = None
    for ep, w0, width, c0 in groups:
        p = jnp.dot(xnb, w_ref[:, w0:w0 + width], preferred_element_type=F32)
        if prev is not None:
            prev[0](prev[1], prev[2])
        prev = (ep, p, c0)
    prev[0](prev[1], prev[2])


def _in_proj(x2, bsz, seq, g, w_cat, conv_w, alog, dtb, *, nh, dh, tm):
    nt, d = x2.shape
    dn_w = nh * dh
    n_t = seq // tm
    cw_n = conv_w.shape[0]
    wcols = w_cat.shape[1]
    row = lambda b, t: (b * n_t + t, 0)
    const = lambda b, t: (0, 0)
    out_rows = lambda w: pl.BlockSpec((tm, w), row)
    body = functools.partial(_in_proj_body, tm=tm, dn_w=dn_w, nh=nh, dh=dh, cw_n=cw_n)
    return pl.pallas_call(
        body,
        grid=(bsz, n_t),
        in_specs=[
            pl.BlockSpec((tm, d), row),
            pl.BlockSpec((1, d), const),
            pl.BlockSpec((d, wcols), const),
            pl.BlockSpec((cw_n, 3 * dn_w), const),
            pl.BlockSpec((1, LANES), const),
            pl.BlockSpec((1, LANES), const),
        ],
        out_specs=[
            out_rows(dn_w), out_rows(dn_w),
            pl.BlockSpec((1, dn_w, tm), lambda b, t: (b, 0, t)),
            out_rows(dn_w), out_rows(dn_w), out_rows(dn_w),
            pl.BlockSpec((1, SUBLANES, tm), lambda b, t: (b, 0, t)),
        ],
        out_shape=[
            jax.ShapeDtypeStruct((nt, dn_w), F32), jax.ShapeDtypeStruct((nt, dn_w), F32),
            jax.ShapeDtypeStruct((bsz, dn_w, seq), F32),
            jax.ShapeDtypeStruct((nt, dn_w), F32), jax.ShapeDtypeStruct((nt, dn_w), F32),
            jax.ShapeDtypeStruct((nt, dn_w), F32),
            jax.ShapeDtypeStruct((bsz, SUBLANES, seq), F32),
        ],
        scratch_shapes=[pltpu.VMEM((3 * dn_w // LANES, tm + SUBLANES, LANES), F32)],
        compiler_params=_params("arbitrary", "arbitrary"),
        name="in_proj",
    )(x2, g, w_cat, conv_w, alog, dtb)


def _delta_body(q_ref, k_ref, kt_ref, v_ref, z_ref, gt_ref, ng_ref,
                o_ref, s_ref, u_sc, w_sc, qg_sc, attn_sc, kdt_sc, gtot_sc, *, tb, nh, dh, ch):
    t = pl.program_id(1)
    nc = tb // ch

    wr = t % 2
    rd = 1 - wr

    @pl.when(t == 0)
    def _():
        for ref in (s_ref, u_sc, w_sc, qg_sc, attn_sc, kdt_sc, gtot_sc):
            ref[...] = jnp.zeros(ref.shape, ref.dtype)

    heads = range(nh)
    hls = [slice(h * dh, (h + 1) * dh) for h in heads]
    mm = functools.partial(jnp.dot, preferred_element_type=F32)

    def phase_b():
        gtot_b = gtot_sc[rd]
        s_all = [s_ref[h] for h in heads]
        for cidx in range(nc):
            rows = slice(cidx * ch, (cidx + 1) * ch)
            sb = [s_all[h].astype(BF16) for h in heads]
            ws = [mm(w_sc[rd, h, rows, :], sb[h]) for h in heads]
            qs = [mm(qg_sc[rd, h, rows, :], sb[h]) for h in heads]
            yield
            vb = [(u_sc[rd, h, rows, :] - ws[h]).astype(BF16) for h in heads]
            av = [mm(attn_sc[rd, h, rows, rows], vb[h]) for h in heads]
            kv = [mm(kdt_sc[rd, h, :, rows], vb[h]) for h in heads]
            yield
            outs = []
            for h in heads:
                a_last = jnp.exp(gtot_b[h:h + 1, cidx * ch:cidx * ch + 1])
                s_all[h] = s_all[h] * a_last + kv[h]
                o = qs[h] + av[h]
                o = o * lax.rsqrt(jnp.mean(o * o, -1, keepdims=True) + EPS)
                zh = z_ref[rows, hls[h]]
                outs.append(o * ng_ref[...] * (zh * jax.nn.sigmoid(zh)))
            o_ref[rows, :] = jnp.concatenate(outs, axis=1)
        for h in heads:
            s_ref[h] = s_all[h]

    r = lax.broadcasted_iota(jnp.int32, (tb, tb), 0)
    c = lax.broadcasted_iota(jnp.int32, (tb, tb), 1)
    same = (r // ch) == (c // ch)
    causal_bd = same & (c <= r)
    strict_bd = same & (c < r)
    gt = gt_ref[0]
    gcum_t = _fdot(gt, jnp.where(same & (r <= c), 1.0, 0.0))
    gtot_t = _fdot(gt, jnp.where(same, 1.0, 0.0))
    gcum = gcum_t.T
    gates = gt.T

    def pack(x):
        acc = x[0:ch, :]
        for i in range(1, nc):
            acc = acc + x[i * ch:(i + 1) * ch, :]
        return acc

    def unpack(x):
        return jnp.where(same, jnp.concatenate([x] * nc, axis=0), 0.0).astype(BF16)

    n_sq = max(1, (ch - 1).bit_length() - 1)

    def phase_a():
        gcol = [gcum[:, h:h + 1] for h in heads]
        grow = [gcum_t[h:h + 1, :] for h in heads]
        beta = [gates[:, nh + h:nh + h + 1] for h in heads]
        kb = [k_ref[:, hls[h]] * beta[h] for h in heads]
        gram = [mm(jnp.concatenate([kb[h], q_ref[:, hls[h]]], axis=0).astype(BF16),
                   kt_ref[0, hls[h], :].astype(BF16)) for h in heads]
        yield
        ppk, apk = [], []
        for h in heads:
            decay = jnp.exp(jnp.where(causal_bd, gcol[h] - grow[h], -jnp.inf))
            p_bd = -jnp.where(strict_bd, gram[h][:tb] * decay, 0.0)
            attn_sc[wr, h] = (gram[h][tb:] * decay).astype(BF16)
            pk = pack(p_bd)
            apk.append(pk)
            ppk.append(mm(pk.astype(BF16), p_bd.astype(BF16)))
        yield
        for j in range(1, n_sq + 1):
            res = []
            for h in heads:
                lhs = jnp.concatenate([apk[h], ppk[h]], axis=0) if j < n_sq else apk[h]
                res.append(mm(lhs.astype(BF16), unpack(ppk[h])))
            yield
            for h in heads:
                apk[h] = apk[h] + ppk[h] + res[h][:ch]
                if j < n_sq:
                    ppk[h] = res[h][ch:]
        egc = [jnp.exp(gcol[h]) for h in heads]
        rhs = [jnp.concatenate([v_ref[:, hls[h]] * beta[h], kb[h] * egc[h]], axis=1) for h in heads]
        prod = [mm(unpack(apk[h]), rhs[h].astype(BF16)) for h in heads]
        yield
        for h in heads:
            uw = rhs[h] + prod[h]
            u_sc[wr, h] = uw[:, :dh]
            w_sc[wr, h] = uw[:, dh:].astype(BF16)
            qg_sc[wr, h] = (q_ref[:, hls[h]] * egc[h]).astype(BF16)
            kdt_sc[wr, h] = (kt_ref[0, hls[h], :] * jnp.exp(gtot_t[h:h + 1, :] - grow[h])).astype(BF16)
        gtot_sc[wr] = gtot_t

    pending = [phase_b(), phase_a()]
    while pending:
        for gen in list(pending):
            try:
                next(gen)
            except StopIteration:
                pending.remove(gen)


def _delta(q, k, kt, v, z, gt, ng, bsz, seq, *, nh, dh, tb):
    nt, dn_w = q.shape
    n_t = seq // tb
    a_blk = lambda t: jnp.minimum(t, n_t - 1)
    b_blk = lambda t: jnp.maximum(t - 1, 0)
    a_rows = pl.BlockSpec((tb, dn_w), lambda b, t: (b * n_t + a_blk(t), 0))
    b_rows = pl.BlockSpec((tb, dn_w), lambda b, t: (b * n_t + b_blk(t), 0))
    body = functools.partial(_delta_body, tb=tb, nh=nh, dh=dh, ch=DN_CHUNK)
    return pl.pallas_call(
        body,
        grid=(bsz, n_t + 1),
        in_specs=[
            a_rows, a_rows,
            pl.BlockSpec((1, dn_w, tb), lambda b, t: (b, 0, a_blk(t))),
            a_rows, b_rows,
            pl.BlockSpec((1, SUBLANES, tb), lambda b, t: (b, 0, a_blk(t))),
            pl.BlockSpec((1, dh), lambda b, t: (0, 0)),
        ],
        out_specs=b_rows,
        out_shape=jax.ShapeDtypeStruct((nt, dn_w), F32),
        scratch_shapes=[
            pltpu.VMEM((nh, dh, dh), F32),
            pltpu.VMEM((2, nh, tb, dh), F32),
            pltpu.VMEM((2, nh, tb, dh), BF16),
            pltpu.VMEM((2, nh, tb, dh), BF16),
            pltpu.VMEM((2, nh, tb, tb), BF16),
            pltpu.VMEM((2, nh, dh, tb), BF16),
            pltpu.VMEM((2, SUBLANES, tb), F32),
        ],
        compiler_params=_params("arbitrary", "arbitrary"),
        name="delta",
    )(q, k, kt, v, z, gt, ng)


def _s5_zoh_body(are_ref, aim_ref, ldt_ref, lre_ref, lim_ref, cre_ref, cim_ref):
    a_re = are_ref[...]
    a_im = aim_ref[...]
    dt = jnp.exp(ldt_ref[...])
    lre = a_re * dt
    lim = a_im * dt
    mag = jnp.exp(lre)
    ab_re = mag * jnp.cos(lim)
    ab_im = mag * jnp.sin(lim)
    den = a_re * a_re + a_im * a_im
    nr = ab_re - 1.0
    ni = ab_im
    lre_ref[...] = lre
    lim_ref[...] = lim
    cre_ref[...] = (nr * a_re + ni * a_im) / den
    cim_ref[...] = (ni * a_re - nr * a_im) / den


def _s5_tables_body(lre_ref, lim_ref, cor_ref, coi_ref, bre_ref, bim_ref,
                    pin_re, pin_im, pout_re, pout_im, al_re, al_im, bbre_ref, bbim_ref,
                    *, chunk, mid):
    lre = lre_ref[...]
    lim = lim_ref[...]
    n = lax.broadcasted_iota(jnp.int32, (chunk, 1), 0).astype(F32) - float(mid)
    m_out = jnp.exp(lre * n)
    pout_re[...] = m_out * jnp.cos(lim * n)
    pout_im[...] = m_out * jnp.sin(lim * n)
    m_in = jnp.exp(-(lre * n))
    pin_re[...] = m_in * jnp.cos(lim * n)
    pin_im[...] = -(m_in * jnp.sin(lim * n))
    m_al = jnp.exp(lre * float(chunk))
    al_re[...] = m_al * jnp.cos(lim * float(chunk))
    al_im[...] = m_al * jnp.sin(lim * float(chunk))
    cor = cor_ref[...]
    coi = coi_ref[...]
    b_re = bre_ref[...]
    b_im = bim_ref[...]
    bbre_ref[...] = cor * b_re - coi * b_im
    bbim_ref[...] = cor * b_im + coi * b_re


def _s5_prep(a_re, a_im, log_dt, b_re, b_im, *, chunk):
    g, p = a_re.shape
    cg = b_re.shape[-1]
    gp = g * p
    shp = jax.ShapeDtypeStruct((g, p), F32)
    lre, lim, cor, coi = pl.pallas_call(
        _s5_zoh_body, out_shape=[shp, shp, shp, shp], name="s5_zoh",
    )(a_re, a_im, log_dt.reshape(g, 1))
    tab = jax.ShapeDtypeStruct((chunk, gp), F32)
    one = jax.ShapeDtypeStruct((1, gp), F32)
    bb = jax.ShapeDtypeStruct((gp, cg), F32)
    body = functools.partial(_s5_tables_body, chunk=chunk, mid=chunk // 2)
    return pl.pallas_call(
        body, out_shape=[tab, tab, tab, tab, one, one, bb, bb], name="s5_tables",
    )(lre.reshape(1, gp), lim.reshape(1, gp), cor.reshape(gp, 1), coi.reshape(gp, 1),
      b_re.reshape(gp, cg), b_im.reshape(gp, cg))


def _s5_body(u_ref, wb_ref, wc_ref, pin_re, pin_im, pout_re, pout_im, al_re, al_im,
             d_ref, wglu_ref, bglu_ref, y_ref, carry_ref, xs_ref, *, ts, chunk, n_half):
    t = pl.program_id(1)

    @pl.when(t == 0)
    def _():
        carry_ref[...] = jnp.zeros(carry_ref.shape, F32)

    u = u_ref[...]
    ub = u.astype(BF16)
    hw = u.shape[1] // n_half
    sw = pin_re.shape[2]
    ri = lax.broadcasted_iota(jnp.int32, (chunk, chunk), 0)
    ci = lax.broadcasted_iota(jnp.int32, (chunk, chunk), 1)
    tri = jnp.where(ci <= ri, 1.0, 0.0).astype(BF16)
    ys = []
    for hf in range(n_half):
        bu = jnp.dot(ub[:, hf * hw:(hf + 1) * hw], wb_ref[hf], preferred_element_type=F32)
        pr, pi = pin_re[hf], pin_im[hf]
        qr, qi = pout_re[hf], pout_im[hf]
        ar, ai = al_re[hf], al_im[hf]
        cre = carry_ref[2 * hf:2 * hf + 1, :]
        cim = carry_ref[2 * hf + 1:2 * hf + 2, :]
        for cidx in range(ts // chunk):
            rows = slice(cidx * chunk, (cidx + 1) * chunk)
            bre = bu[rows, :sw]
            bim = bu[rows, sw:]
            sre = pr * bre - pi * bim
            sim = pr * bim + pi * bre
            cs = jnp.dot(tri, jnp.concatenate([sre, sim], axis=1).astype(BF16),
                         preferred_element_type=F32)
            tre = cs[:, :sw] + cre
            tim = cs[:, sw:] + cim
            xs_ref[rows, :sw] = (qr * tre - qi * tim).astype(BF16)
            xs_ref[rows, sw:] = (qr * tim + qi * tre).astype(BF16)
            lre = tre[chunk - 1:chunk, :]
            lim = tim[chunk - 1:chunk, :]
            cre = ar * lre - ai * lim
            cim = ar * lim + ai * lre
        carry_ref[2 * hf:2 * hf + 1, :] = cre
        carry_ref[2 * hf + 1:2 * hf + 2, :] = cim
        ys.append(jnp.dot(xs_ref[...], wc_ref[hf], preferred_element_type=F32))
    y = jnp.concatenate(ys, axis=1) + d_ref[...] * u
    y = jax.nn.gelu(y)
    gate = jax.nn.sigmoid(jnp.dot(y.astype(BF16), wglu_ref[...], preferred_element_type=F32)
                          + bglu_ref[...])
    y_ref[...] = y * gate


def _s5(u, wb, wc, tabs, d, wglu, bglu, bsz, seq, *, ts, chunk):
    nt, width = u.shape
    n_half = wb.shape[0]
    sw2 = wb.shape[2]
    sw = sw2 // 2
    n_t = seq // ts
    row = lambda b, t: (b * n_t + t, 0)
    c2 = lambda b, t: (0, 0)
    c3 = lambda b, t: (0, 0, 0)
    tab = pl.BlockSpec((n_half, chunk, sw), c3)
    one = pl.BlockSpec((n_half, 1, sw), c3)
    body = functools.partial(_s5_body, ts=ts, chunk=chunk, n_half=n_half)
    return pl.pallas_call(
        body,
        grid=(bsz, n_t),
        in_specs=[
            pl.BlockSpec((ts, width), row),
            pl.BlockSpec(wb.shape, c3),
            pl.BlockSpec(wc.shape, c3),
            tab, tab, tab, tab, one, one,
            pl.BlockSpec((1, width), c2),
            pl.BlockSpec(wglu.shape, c2),
            pl.BlockSpec((1, width), c2),
        ],
        out_specs=pl.BlockSpec((ts, width), row),
        out_shape=jax.ShapeDtypeStruct((nt, width), F32),
        scratch_shapes=[pltpu.VMEM((2 * n_half, sw), F32), pltpu.VMEM((ts, sw2), BF16)],
        compiler_params=_params("arbitrary", "arbitrary"),
        name="s5",
    )(u, wb, wc, *tabs, d, wglu, bglu)


def _mem_kv_body(m_ref, g_ref, wk_ref, wv_ref, kt_ref, v_ref):
    mn = _rms(m_ref[0], g_ref[...]).astype(BF16)
    k = jnp.dot(mn, wk_ref[...], preferred_element_type=F32)
    kt_ref[0] = k.T.astype(BF16)
    v_ref[0] = jnp.dot(mn, wv_ref[...], preferred_element_type=F32).astype(BF16)


def _mem_kv(mem, g, wk, wv):
    bsz, n_mem, d = mem.shape
    c2 = lambda b: (0, 0)
    return pl.pallas_call(
        _mem_kv_body,
        grid=(bsz,),
        in_specs=[
            pl.BlockSpec((1, n_mem, d), lambda b: (b, 0, 0)),
            pl.BlockSpec((1, d), c2),
            pl.BlockSpec((d, d), c2),
            pl.BlockSpec((d, d), c2),
        ],
        out_specs=[
            pl.BlockSpec((1, d, n_mem), lambda b: (b, 0, 0)),
            pl.BlockSpec((1, n_mem, d), lambda b: (b, 0, 0)),
        ],
        out_shape=[
            jax.ShapeDtypeStruct((bsz, d, n_mem), BF16),
            jax.ShapeDtypeStruct((bsz, n_mem, d), BF16),
        ],
        compiler_params=_params("arbitrary"),
        name="mem_kv",
    )(mem, g, wk, wv)


def _mix_attn_body(x_ref, o_ref, y_ref, wo_ref, gx_ref, wq_ref, kt_ref, v_ref, wxo_ref,
                   h_ref, *, xh):
    dn_w = o_ref.shape[1]
    mix = (jnp.dot(o_ref[...].astype(BF16), wo_ref[0:dn_w, :], preferred_element_type=F32)
           + jnp.dot(y_ref[...].astype(BF16), wo_ref[dn_w:, :], preferred_element_type=F32))
    h1 = x_ref[...] + mix
    hn = _rms(h1, gx_ref[...]).astype(BF16)
    q = jnp.dot(hn, wq_ref[...], preferred_element_type=F32).astype(BF16)
    d = q.shape[1]
    xd = d // xh
    scale = xd ** -0.5
    parts = []
    for h in range(xh):
        hl = slice(h * xd, (h + 1) * xd)
        s = jnp.dot(q[:, hl], kt_ref[0, hl, :], preferred_element_type=F32) * scale
        e = jnp.exp(s - jnp.max(s, axis=-1, keepdims=True))
        p = (e / jnp.sum(e, axis=-1, keepdims=True)).astype(BF16)
        parts.append(jnp.dot(p, v_ref[0, :, hl], preferred_element_type=F32).astype(BF16))
    att = jnp.concatenate(parts, axis=1)
    h_ref[...] = h1 + jnp.dot(att, wxo_ref[...], preferred_element_type=F32)


def _mix_attn(x2, o, y, w_out, gx, wq, kt, v, wxo, bsz, seq, *, tm):
    nt, d = x2.shape
    dn_w = o.shape[1]
    n_mem = v.shape[1]
    n_t = seq // tm
    row = lambda b, t: (b * n_t + t, 0)
    c2 = lambda b, t: (0, 0)
    body = functools.partial(_mix_attn_body, xh=X_HEADS)
    return pl.pallas_call(
        body,
        grid=(bsz, n_t),
        in_specs=[
            pl.BlockSpec((tm, d), row),
            pl.BlockSpec((tm, dn_w), row),
            pl.BlockSpec((tm, y.shape[1]), row),
            pl.BlockSpec(w_out.shape, c2),
            pl.BlockSpec((1, d), c2),
            pl.BlockSpec((d, d), c2),
            pl.BlockSpec((1, d, n_mem), lambda b, t: (b, 0, 0)),
            pl.BlockSpec((1, n_mem, d), lambda b, t: (b, 0, 0)),
            pl.BlockSpec((d, d), c2),
        ],
        out_specs=pl.BlockSpec((tm, d), row),
        out_shape=jax.ShapeDtypeStruct((nt, d), F32),
        compiler_params=_params("arbitrary", "arbitrary"),
        name="mix_attn",
    )(x2, o, y, w_out, gx, wq, kt, v, wxo)


def _ffn_body(h_ref, gf_ref, wg_ref, wu_ref, wd_ref, gl_ref, out_ref, *, final, fc):
    h = h_ref[...]
    hn = _rms(h, gf_ref[...]).astype(BF16)
    dff = wg_ref.shape[1]

    def down(gate, up, c0):
        act = (gate * jax.nn.sigmoid(gate) * up).astype(BF16)
        return jnp.dot(act, wd_ref[c0:c0 + fc, :], preferred_element_type=F32)

    h3 = h
    prev = None
    for c0 in range(0, dff, fc):
        gate = jnp.dot(hn, wg_ref[:, c0:c0 + fc], preferred_element_type=F32)
        up = jnp.dot(hn, wu_ref[:, c0:c0 + fc], preferred_element_type=F32)
        if prev is not None:
            h3 = h3 + down(*prev)
        prev = (gate, up, c0)
    h3 = h3 + down(*prev)
    out_ref[...] = _rms(h3, gl_ref[...]) if final else h3


def _ffn(h, gf, wg, wu, wd, gl, *, tm, final):
    nt, d = h.shape
    dff = wg.shape[1]
    c2 = lambda i: (0, 0)
    row = lambda i: (i, 0)
    return pl.pallas_call(
        functools.partial(_ffn_body, final=final, fc=2 * LANES),
        grid=(nt // tm,),
        in_specs=[
            pl.BlockSpec((tm, d), row),
            pl.BlockSpec((1, d), c2),
            pl.BlockSpec((d, dff), c2, pipeline_mode=pl.Buffered(1)),
            pl.BlockSpec((d, dff), c2, pipeline_mode=pl.Buffered(1)),
            pl.BlockSpec((dff, d), c2, pipeline_mode=pl.Buffered(1)),
            pl.BlockSpec((1, d), c2),
        ],
        out_specs=pl.BlockSpec((tm, d), row),
        out_shape=jax.ShapeDtypeStruct((nt, d), F32),
        compiler_params=_params("arbitrary"),
        name="ffn",
    )(h, gf, wg, wu, wd, gl)


def _block_diag(blocks):
    n, r, c = blocks.shape
    eye = jnp.eye(n, dtype=blocks.dtype)
    return (blocks[:, :, None, :] * eye[:, None, :, None]).reshape(n * r, n * c)


def _layer(h2, mem, bsz, seq, norm_mix_g, w_in, conv_w, dn_a_log, dn_dt_bias, dn_norm_g,
           s5_a_re, s5_a_im, s5_b_re, s5_b_im, s5_c_re, s5_c_im, s5_d, s5_log_dt,
           s5_w_glu, s5_b_glu, w_out, norm_x_g, norm_mem_g, w_xq, w_xk, w_xv, w_xo):
    d = h2.shape[1]
    nh = dn_a_log.shape[0]
    dh = dn_norm_g.shape[0]
    dn_w = nh * dh
    n_grp, n_state = s5_a_re.shape
    cg = s5_b_re.shape[-1]
    s5_w = n_grp * cg

    off_a = 4 * dn_w
    off_u = off_a + 2 * nh
    w_ab = jnp.pad(w_in[:, off_a:off_u], ((0, 0), (0, LANES - 2 * nh)))
    w_cat = jnp.concatenate([w_in[:, :off_a], w_in[:, off_u:], w_ab], axis=1).astype(BF16)
    alog = jnp.pad(dn_a_log, (0, LANES - nh)).reshape(1, LANES)
    dtb = jnp.pad(dn_dt_bias, (0, LANES - nh)).reshape(1, LANES)

    q, k, kt, v, z, u, gt = _in_proj(
        h2, bsz, seq, norm_mix_g.reshape(1, d), w_cat, conv_w, alog, dtb, nh=nh, dh=dh, tm=256)
    o = _delta(q, k, kt, v, z, gt, dn_norm_g.reshape(1, dh), bsz, seq, nh=nh, dh=dh, tb=256)

    pin_re, pin_im, pout_re, pout_im, al_re, al_im, bb_re, bb_im = _s5_prep(
        s5_a_re, s5_a_im, s5_log_dt, s5_b_re, s5_b_im, chunk=S5_CHUNK)
    n_half = 2
    gh = n_grp // n_half
    sw = gh * n_state
    split = lambda a: a.reshape(a.shape[0], n_half, sw).transpose(1, 0, 2)
    tabs = tuple(split(a) for a in (pin_re, pin_im, pout_re, pout_im, al_re, al_im))
    bbr = bb_re.reshape(n_half, gh, n_state, cg).transpose(0, 1, 3, 2)
    bbi = bb_im.reshape(n_half, gh, n_state, cg).transpose(0, 1, 3, 2)
    wb = jnp.stack([jnp.concatenate([_block_diag(bbr[i]), _block_diag(bbi[i])], axis=1)
                    for i in range(n_half)]).astype(BF16)
    ccr = s5_c_re.reshape(n_half, gh, cg, n_state).transpose(0, 1, 3, 2)
    cci = s5_c_im.reshape(n_half, gh, cg, n_state).transpose(0, 1, 3, 2)
    wc = jnp.stack([jnp.concatenate([_block_diag(ccr[i]), -_block_diag(cci[i])], axis=0)
                    for i in range(n_half)]).astype(BF16)
    y = _s5(u, wb, wc, tabs, s5_d.reshape(1, s5_w), s5_w_glu.astype(BF16),
            s5_b_glu.reshape(1, s5_w), bsz, seq, ts=256, chunk=S5_CHUNK)

    kt_mem, v_mem = _mem_kv(mem, norm_mem_g.reshape(1, d), w_xk.astype(BF16), w_xv.astype(BF16))
    return _mix_attn(h2, o, y, w_out.astype(BF16), norm_x_g.reshape(1, d), w_xq.astype(BF16),
                     kt_mem, v_mem, w_xo.astype(BF16), bsz, seq, tm=256)


def kernel(x, mem, norm_mix_g, w_in, conv_w, dn_a_log, dn_dt_bias, dn_norm_g, s5_a_re, s5_a_im,
           s5_b_re, s5_b_im, s5_c_re, s5_c_im, s5_d, s5_log_dt, s5_w_glu, s5_b_glu, w_out,
           norm_x_g, norm_mem_g, w_xq, w_xk, w_xv, w_xo, norm_ffn_g, w_gate, w_up, w_down,
           norm_final_g):
    bsz, seq, d = x.shape
    depth = w_in.shape[0]
    h = x.reshape(bsz * seq, d)
    for l in range(depth):
        h = _layer(h, mem, bsz, seq, norm_mix_g[l], w_in[l], conv_w[l], dn_a_log[l],
                   dn_dt_bias[l], dn_norm_g[l], s5_a_re[l], s5_a_im[l], s5_b_re[l], s5_b_im[l],
                   s5_c_re[l], s5_c_im[l], s5_d[l], s5_log_dt[l], s5_w_glu[l], s5_b_glu[l],
                   w_out[l], norm_x_g[l], norm_mem_g[l], w_xq[l], w_xk[l], w_xv[l], w_xo[l])
        h = _ffn(h, norm_ffn_g[l].reshape(1, d), w_gate[l].astype(BF16), w_up[l].astype(BF16),
                 w_down[l].astype(BF16), norm_final_g.reshape(1, d), tm=256, final=l == depth - 1)
    return h.reshape(bsz, seq, d)
```

```python
import functools
import math

import jax
import jax.numpy as jnp
from jax import lax
from jax.experimental import pallas as pl
from jax.experimental.pallas import tpu as pltpu

F32 = jnp.float32
BF16 = jnp.bfloat16
EPS = 1e-6
HIGHEST = lax.Precision.HIGHEST

LANES = 128
SUBLANES = 8
DN_CHUNK = 64
S5_CHUNK = 64
X_HEADS = 4
VMEM_LIMIT = 56 * 1024 * 1024


def _bdot(a, b):
    return jnp.dot(a.astype(BF16), b.astype(BF16), preferred_element_type=F32)


def _fdot(a, b):
    return jnp.dot(a, b, precision=HIGHEST, preferred_element_type=F32)


def _rms(x, g):
    return x * lax.rsqrt(jnp.mean(x * x, axis=-1, keepdims=True) + EPS) * g


def _interleave(gens):
    pending = list(gens)
    while pending:
        for gen in list(pending):
            try:
                next(gen)
            except StopIteration:
                pending.remove(gen)


def _params(*sem):
    return pltpu.CompilerParams(dimension_semantics=sem, vmem_limit_bytes=VMEM_LIMIT)


def _in_proj_body(x_ref, g_ref, w_ref, cw_ref, alog_ref, dtb_ref,
                  q_ref, k_ref, kt_ref, v_ref, z_ref, u_ref, gt_ref,
                  buf_ref, *, tm, dn_w, nh, dh, cw_n):
    t = pl.program_id(1)
    qkv_w = 3 * dn_w

    @pl.when(t == 0)
    def _():
        buf_ref[:, 0:SUBLANES, :] = jnp.zeros((qkv_w // LANES, SUBLANES, LANES), F32)

    xnb = _rms(x_ref[...], g_ref[...]).astype(BF16)
    gw = 2 * dh

    def conv_silu(p, cols):
        outs = []
        for i in range(gw // LANES):
            c0 = cols.start + i * LANES
            s = c0 // LANES
            pc = p[:, i * LANES:(i + 1) * LANES]
            buf_ref[s, SUBLANES:SUBLANES + tm, :] = pc
            acc = cw_ref[cw_n - 1:cw_n, c0:c0 + LANES] * pc
            for j in range(cw_n - 1):
                off = SUBLANES - (cw_n - 1) + j
                acc = acc + cw_ref[j:j + 1, c0:c0 + LANES] * buf_ref[s, off:off + tm, :]
            buf_ref[s, 0:SUBLANES, :] = buf_ref[s, tm:tm + SUBLANES, :]
            outs.append(acc * jax.nn.sigmoid(acc))
        return jnp.concatenate(outs, axis=1)

    def l2n(a, scale):
        parts = []
        for i in range(gw // dh):
            ah = a[:, i * dh:(i + 1) * dh]
            parts.append(ah * (lax.rsqrt(jnp.sum(ah * ah, -1, keepdims=True) + EPS) * scale))
        return jnp.concatenate(parts, axis=1)

    def ep_q(p, c0):
        q_ref[:, c0:c0 + gw] = l2n(conv_silu(p, slice(c0, c0 + gw)), dh ** -0.5)

    def ep_k(p, c0):
        kn = l2n(conv_silu(p, slice(dn_w + c0, dn_w + c0 + gw)), 1.0)
        k_ref[:, c0:c0 + gw] = kn
        kt_ref[0, c0:c0 + gw, :] = kn.T

    def ep_v(p, c0):
        v_ref[:, c0:c0 + gw] = conv_silu(p, slice(2 * dn_w + c0, 2 * dn_w + c0 + gw))

    def ep_z(p, c0):
        z_ref[:, c0:c0 + gw] = p

    def ep_u(p, c0):
        u_ref[:, c0:c0 + gw] = p

    def ep_gates(ab, c0):
        lane = lax.broadcasted_iota(jnp.int32, ab.shape, 1)
        g = -jnp.exp(alog_ref[...]) * jnp.logaddexp(ab + dtb_ref[...], 0.0)
        gates = jnp.where(lane < nh, g, jax.nn.sigmoid(ab))
        gt_ref[0] = gates.T[0:SUBLANES, :]

    groups = []
    for i, ep in enumerate((ep_q, ep_k, ep_v, ep_z, ep_u)):
        groups += [(ep, i * dn_w + c0, gw, c0) for c0 in range(0, dn_w, gw)]
    groups.append((ep_gates, 5 * dn_w, LANES, 0))

    prev = None
    for ep, w0, width, c0 in groups:
        p = jnp.dot(xnb, w_ref[:, w0:w0 + width], preferred_element_type=F32)
        if prev is not None:
            prev[0](prev[1], prev[2])
        prev = (ep, p, c0)
    prev[0](prev[1], prev[2])


def _in_proj(x2, bsz, seq, g, w_cat, conv_w, alog, dtb, *, nh, dh, tm):
    nt, d = x2.shape
    dn_w = nh * dh
    n_t = seq // tm
    cw_n = conv_w.shape[0]
    wcols = w_cat.shape[1]
    row = lambda b, t: (b * n_t + t, 0)
    const = lambda b, t: (0, 0)
    out_rows = lambda w: pl.BlockSpec((tm, w), row)
    body = functools.partial(_in_proj_body, tm=tm, dn_w=dn_w, nh=nh, dh=dh, cw_n=cw_n)
    return pl.pallas_call(
        body,
        grid=(bsz, n_t),
        in_specs=[
            pl.BlockSpec((tm, d), row),
            pl.BlockSpec((1, d), const),
            pl.BlockSpec((d, wcols), const),
            pl.BlockSpec((cw_n, 3 * dn_w), const),
            pl.BlockSpec((1, LANES), const),
            pl.BlockSpec((1, LANES), const),
        ],
        out_specs=[
            out_rows(dn_w), out_rows(dn_w),
            pl.BlockSpec((1, dn_w, tm), lambda b, t: (b, 0, t)),
            out_rows(dn_w), out_rows(dn_w), out_rows(dn_w),
            pl.BlockSpec((1, SUBLANES, tm), lambda b, t: (b, 0, t)),
        ],
        out_shape=[
            jax.ShapeDtypeStruct((nt, dn_w), F32), jax.ShapeDtypeStruct((nt, dn_w), F32),
            jax.ShapeDtypeStruct((bsz, dn_w, seq), F32),
            jax.ShapeDtypeStruct((nt, dn_w), F32), jax.ShapeDtypeStruct((nt, dn_w), F32),
            jax.ShapeDtypeStruct((nt, dn_w), F32),
            jax.ShapeDtypeStruct((bsz, SUBLANES, seq), F32),
        ],
        scratch_shapes=[pltpu.VMEM((3 * dn_w // LANES, tm + SUBLANES, LANES), F32)],
        compiler_params=_params("arbitrary", "arbitrary"),
        name="in_proj",
    )(x2, g, w_cat, conv_w, alog, dtb)


def _delta_body(q_ref, k_ref, kt_ref, v_ref, z_ref, gt_ref, ng_ref,
                o_ref, s_ref, u_sc, w_sc, qg_sc, attn_sc, kdt_sc, gtot_sc, *, tb, nh, dh, ch):
    t = pl.program_id(1)
    nc = tb // ch

    wr = t % 2
    rd = 1 - wr

    @pl.when(t == 0)
    def _():
        for ref in (s_ref, u_sc, w_sc, qg_sc, attn_sc, kdt_sc, gtot_sc):
            ref[...] = jnp.zeros(ref.shape, ref.dtype)

    heads = range(nh)
    hls = [slice(h * dh, (h + 1) * dh) for h in heads]
    mm = functools.partial(jnp.dot, preferred_element_type=F32)

    def phase_b():
        gtot_b = gtot_sc[rd]
        s_all = [s_ref[h] for h in heads]
        for cidx in range(nc):
            rows = slice(cidx * ch, (cidx + 1) * ch)
            sb = [s_all[h].astype(BF16) for h in heads]
            ws = [mm(w_sc[rd, h, rows, :], sb[h]) for h in heads]
            qs = [mm(qg_sc[rd, h, rows, :], sb[h]) for h in heads]
            yield
            vb = [(u_sc[rd, h, rows, :] - ws[h]).astype(BF16) for h in heads]
            av = [mm(attn_sc[rd, h, rows, rows], vb[h]) for h in heads]
            kv = [mm(kdt_sc[rd, h, :, rows], vb[h]) for h in heads]
            yield
            outs = []
            for h in heads:
                a_last = jnp.exp(gtot_b[h:h + 1, cidx * ch:cidx * ch + 1])
                s_all[h] = s_all[h] * a_last + kv[h]
                o = qs[h] + av[h]
                o = o * lax.rsqrt(jnp.mean(o * o, -1, keepdims=True) + EPS)
                zh = z_ref[rows, hls[h]]
                outs.append(o * ng_ref[...] * (zh * jax.nn.sigmoid(zh)))
            o_ref[rows, :] = jnp.concatenate(outs, axis=1)
        for h in heads:
            s_ref[h] = s_all[h]

    r = lax.broadcasted_iota(jnp.int32, (tb, tb), 0)
    c = lax.broadcasted_iota(jnp.int32, (tb, tb), 1)
    same = (r // ch) == (c // ch)
    causal_bd = same & (c <= r)
    strict_bd = same & (c < r)
    gt = gt_ref[0]
    gcum_t = _fdot(gt, jnp.where(same & (r <= c), 1.0, 0.0))
    gtot_t = _fdot(gt, jnp.where(same, 1.0, 0.0))
    gcum = gcum_t.T
    gates = gt.T

    def pack(x):
        acc = x[0:ch, :]
        for i in range(1, nc):
            acc = acc + x[i * ch:(i + 1) * ch, :]
        return acc

    def unpack(x):
        return jnp.where(same, jnp.concatenate([x] * nc, axis=0), 0.0).astype(BF16)

    n_sq = max(1, (ch - 1).bit_length() - 1)

    def phase_a():
        gcol = [gcum[:, h:h + 1] for h in heads]
        grow = [gcum_t[h:h + 1, :] for h in heads]
        beta = [gates[:, nh + h:nh + h + 1] for h in heads]
        kb = [k_ref[:, hls[h]] * beta[h] for h in heads]
        gram = [mm(jnp.concatenate([kb[h], q_ref[:, hls[h]]], axis=0).astype(BF16),
                   kt_ref[0, hls[h], :].astype(BF16)) for h in heads]
        yield
        ppk, apk = [], []
        for h in heads:
            decay = jnp.exp(jnp.where(causal_bd, gcol[h] - grow[h], -jnp.inf))
            p_bd = -jnp.where(strict_bd, gram[h][:tb] * decay, 0.0)
            attn_sc[wr, h] = (gram[h][tb:] * decay).astype(BF16)
            pk = pack(p_bd)
            apk.append(pk)
            ppk.append(mm(pk.astype(BF16), p_bd.astype(BF16)))
        yield
        for j in range(1, n_sq + 1):
            res = []
            for h in heads:
                lhs = jnp.concatenate([apk[h], ppk[h]], axis=0) if j < n_sq else apk[h]
                res.append(mm(lhs.astype(BF16), unpack(ppk[h])))
            yield
            for h in heads:
                apk[h] = apk[h] + ppk[h] + res[h][:ch]
                if j < n_sq:
                    ppk[h] = res[h][ch:]
        egc = [jnp.exp(gcol[h]) for h in heads]
        rhs = [jnp.concatenate([v_ref[:, hls[h]] * beta[h], kb[h] * egc[h]], axis=1) for h in heads]
        prod = [mm(unpack(apk[h]), rhs[h].astype(BF16)) for h in heads]
        yield
        for h in heads:
            uw = rhs[h] + prod[h]
            u_sc[wr, h] = uw[:, :dh]
            w_sc[wr, h] = uw[:, dh:].astype(BF16)
            qg_sc[wr, h] = (q_ref[:, hls[h]] * egc[h]).astype(BF16)
            kdt_sc[wr, h] = (kt_ref[0, hls[h], :] * jnp.exp(gtot_t[h:h + 1, :] - grow[h])).astype(BF16)
        gtot_sc[wr] = gtot_t

    _interleave([phase_b(), phase_a()])


def _delta(q, k, kt, v, z, gt, ng, bsz, seq, *, nh, dh, tb):
    nt, dn_w = q.shape
    n_t = seq // tb
    a_blk = lambda t: jnp.minimum(t, n_t - 1)
    b_blk = lambda t: jnp.maximum(t - 1, 0)
    a_rows = pl.BlockSpec((tb, dn_w), lambda b, t: (b * n_t + a_blk(t), 0))
    b_rows = pl.BlockSpec((tb, dn_w), lambda b, t: (b * n_t + b_blk(t), 0))
    body = functools.partial(_delta_body, tb=tb, nh=nh, dh=dh, ch=DN_CHUNK)
    return pl.pallas_call(
        body,
        grid=(bsz, n_t + 1),
        in_specs=[
            a_rows, a_rows,
            pl.BlockSpec((1, dn_w, tb), lambda b, t: (b, 0, a_blk(t))),
            a_rows, b_rows,
            pl.BlockSpec((1, SUBLANES, tb), lambda b, t: (b, 0, a_blk(t))),
            pl.BlockSpec((1, dh), lambda b, t: (0, 0)),
        ],
        out_specs=b_rows,
        out_shape=jax.ShapeDtypeStruct((nt, dn_w), F32),
        scratch_shapes=[
            pltpu.VMEM((nh, dh, dh), F32),
            pltpu.VMEM((2, nh, tb, dh), F32),
            pltpu.VMEM((2, nh, tb, dh), BF16),
            pltpu.VMEM((2, nh, tb, dh), BF16),
            pltpu.VMEM((2, nh, tb, tb), BF16),
            pltpu.VMEM((2, nh, dh, tb), BF16),
            pltpu.VMEM((2, SUBLANES, tb), F32),
        ],
        compiler_params=_params("arbitrary", "arbitrary"),
        name="delta",
    )(q, k, kt, v, z, gt, ng)


def _s5_zoh_body(are_ref, aim_ref, ldt_ref, lre_ref, lim_ref, cre_ref, cim_ref):
    a_re = are_ref[...]
    a_im = aim_ref[...]
    dt = jnp.exp(ldt_ref[...])
    lre = a_re * dt
    lim = a_im * dt
    mag = jnp.exp(lre)
    ab_re = mag * jnp.cos(lim)
    ab_im = mag * jnp.sin(lim)
    den = a_re * a_re + a_im * a_im
    nr = ab_re - 1.0
    ni = ab_im
    lre_ref[...] = lre
    lim_ref[...] = lim
    cre_ref[...] = (nr * a_re + ni * a_im) / den
    cim_ref[...] = (ni * a_re - nr * a_im) / den


def _s5_tables_body(lre_ref, lim_ref, cor_ref, coi_ref, bre_ref, bim_ref,
                    pin_re, pin_im, pout_re, pout_im, al_re, al_im, bbre_ref, bbim_ref,
                    *, chunk, mid):
    lre = lre_ref[...]
    lim = lim_ref[...]
    n = lax.broadcasted_iota(jnp.int32, (chunk, 1), 0).astype(F32) - float(mid)
    m_out = jnp.exp(lre * n)
    pout_re[...] = m_out * jnp.cos(lim * n)
    pout_im[...] = m_out * jnp.sin(lim * n)
    m_in = jnp.exp(-(lre * n))
    pin_re[...] = m_in * jnp.cos(lim * n)
    pin_im[...] = -(m_in * jnp.sin(lim * n))
    m_al = jnp.exp(lre * float(chunk))
    al_re[...] = m_al * jnp.cos(lim * float(chunk))
    al_im[...] = m_al * jnp.sin(lim * float(chunk))
    cor = cor_ref[...]
    coi = coi_ref[...]
    b_re = bre_ref[...]
    b_im = bim_ref[...]
    bbre_ref[...] = cor * b_re - coi * b_im
    bbim_ref[...] = cor * b_im + coi * b_re


def _s5_prep(a_re, a_im, log_dt, b_re, b_im, *, chunk):
    g, p = a_re.shape
    cg = b_re.shape[-1]
    gp = g * p
    shp = jax.ShapeDtypeStruct((g, p), F32)
    lre, lim, cor, coi = pl.pallas_call(
        _s5_zoh_body, out_shape=[shp, shp, shp, shp], name="s5_zoh",
    )(a_re, a_im, log_dt.reshape(g, 1))
    tab = jax.ShapeDtypeStruct((chunk, gp), F32)
    one = jax.ShapeDtypeStruct((1, gp), F32)
    bb = jax.ShapeDtypeStruct((gp, cg), F32)
    body = functools.partial(_s5_tables_body, chunk=chunk, mid=chunk // 2)
    return pl.pallas_call(
        body, out_shape=[tab, tab, tab, tab, one, one, bb, bb], name="s5_tables",
    )(lre.reshape(1, gp), lim.reshape(1, gp), cor.reshape(gp, 1), coi.reshape(gp, 1),
      b_re.reshape(gp, cg), b_im.reshape(gp, cg))


def _s5_body(u_ref, wb_ref, wc_ref, pin_re, pin_im, pout_re, pout_im, al_re, al_im,
             d_ref, wglu_ref, bglu_ref, y_ref, carry_ref, xs_ref, *, ts, chunk, n_half):
    t = pl.program_id(1)

    @pl.when(t == 0)
    def _():
        carry_ref[...] = jnp.zeros(carry_ref.shape, F32)

    u = u_ref[...]
    ub = u.astype(BF16)
    hw = u.shape[1] // n_half
    sw = pin_re.shape[2]
    ri = lax.broadcasted_iota(jnp.int32, (chunk, chunk), 0)
    ci = lax.broadcasted_iota(jnp.int32, (chunk, chunk), 1)
    tri = jnp.where(ci <= ri, 1.0, 0.0).astype(BF16)
    ys = [None] * n_half

    def half(hf):
        bu = jnp.dot(ub[:, hf * hw:(hf + 1) * hw], wb_ref[hf], preferred_element_type=F32)
        yield
        pr, pi = pin_re[hf], pin_im[hf]
        qr, qi = pout_re[hf], pout_im[hf]
        ar, ai = al_re[hf], al_im[hf]
        carry = [carry_ref[2 * hf:2 * hf + 1, :], carry_ref[2 * hf + 1:2 * hf + 2, :]]

        def prefix(cidx):
            rows = slice(cidx * chunk, (cidx + 1) * chunk)
            bre = bu[rows, :sw]
            bim = bu[rows, sw:]
            sre = pr * bre - pi * bim
            sim = pr * bim + pi * bre
            return jnp.dot(tri, jnp.concatenate([sre, sim], axis=1).astype(BF16),
                           preferred_element_type=F32)

        def finish(cidx, cs):
            rows = slice(cidx * chunk, (cidx + 1) * chunk)
            tre = cs[:, :sw] + carry[0]
            tim = cs[:, sw:] + carry[1]
            xs_ref[hf, rows, :sw] = (qr * tre - qi * tim).astype(BF16)
            xs_ref[hf, rows, sw:] = (qr * tim + qi * tre).astype(BF16)
            lre = tre[chunk - 1:chunk, :]
            lim = tim[chunk - 1:chunk, :]
            carry[0] = ar * lre - ai * lim
            carry[1] = ar * lim + ai * lre

        prev = None
        for cidx in range(ts // chunk):
            cs = prefix(cidx)
            if prev is not None:
                finish(*prev)
            prev = (cidx, cs)
            yield
        finish(*prev)
        carry_ref[2 * hf:2 * hf + 1, :] = carry[0]
        carry_ref[2 * hf + 1:2 * hf + 2, :] = carry[1]
        yield
        ys[hf] = jnp.dot(xs_ref[hf], wc_ref[hf], preferred_element_type=F32)

    _interleave([half(hf) for hf in range(n_half)])
    y = jnp.concatenate(ys, axis=1) + d_ref[...] * u
    y = jax.nn.gelu(y)
    gate = jax.nn.sigmoid(jnp.dot(y.astype(BF16), wglu_ref[...], preferred_element_type=F32)
                          + bglu_ref[...])
    y_ref[...] = y * gate


def _s5(u, wb, wc, tabs, d, wglu, bglu, bsz, seq, *, ts, chunk):
    nt, width = u.shape
    n_half = wb.shape[0]
    sw2 = wb.shape[2]
    sw = sw2 // 2
    n_t = seq // ts
    row = lambda b, t: (b * n_t + t, 0)
    c2 = lambda b, t: (0, 0)
    c3 = lambda b, t: (0, 0, 0)
    tab = pl.BlockSpec((n_half, chunk, sw), c3)
    one = pl.BlockSpec((n_half, 1, sw), c3)
    body = functools.partial(_s5_body, ts=ts, chunk=chunk, n_half=n_half)
    return pl.pallas_call(
        body,
        grid=(bsz, n_t),
        in_specs=[
            pl.BlockSpec((ts, width), row),
            pl.BlockSpec(wb.shape, c3),
            pl.BlockSpec(wc.shape, c3),
            tab, tab, tab, tab, one, one,
            pl.BlockSpec((1, width), c2),
            pl.BlockSpec(wglu.shape, c2),
            pl.BlockSpec((1, width), c2),
        ],
        out_specs=pl.BlockSpec((ts, width), row),
        out_shape=jax.ShapeDtypeStruct((nt, width), F32),
        scratch_shapes=[pltpu.VMEM((2 * n_half, sw), F32), pltpu.VMEM((n_half, ts, sw2), BF16)],
        compiler_params=_params("arbitrary", "arbitrary"),
        name="s5",
    )(u, wb, wc, *tabs, d, wglu, bglu)


def _mem_kv_body(m_ref, g_ref, wk_ref, wv_ref, kt_ref, v_ref):
    mn = _rms(m_ref[0], g_ref[...]).astype(BF16)
    k = jnp.dot(mn, wk_ref[...], preferred_element_type=F32)
    kt_ref[0] = k.T.astype(BF16)
    v_ref[0] = jnp.dot(mn, wv_ref[...], preferred_element_type=F32).astype(BF16)


def _mem_kv(mem, g, wk, wv):
    bsz, n_mem, d = mem.shape
    c2 = lambda b: (0, 0)
    return pl.pallas_call(
        _mem_kv_body,
        grid=(bsz,),
        in_specs=[
            pl.BlockSpec((1, n_mem, d), lambda b: (b, 0, 0)),
            pl.BlockSpec((1, d), c2),
            pl.BlockSpec((d, d), c2),
            pl.BlockSpec((d, d), c2),
        ],
        out_specs=[
            pl.BlockSpec((1, d, n_mem), lambda b: (b, 0, 0)),
            pl.BlockSpec((1, n_mem, d), lambda b: (b, 0, 0)),
        ],
        out_shape=[
            jax.ShapeDtypeStruct((bsz, d, n_mem), BF16),
            jax.ShapeDtypeStruct((bsz, n_mem, d), BF16),
        ],
        compiler_params=_params("arbitrary"),
        name="mem_kv",
    )(mem, g, wk, wv)


def _mix_attn_body(x_ref, o_ref, y_ref, wo_ref, gx_ref, wq_ref, kt_ref, v_ref, wxo_ref,
                   h_ref, *, xh, n_sub):
    dn_w = o_ref.shape[1]
    d = x_ref.shape[1]
    xd = d // xh
    scale = xd ** -0.5
    sub = x_ref.shape[0] // n_sub
    mm = functools.partial(jnp.dot, preferred_element_type=F32)

    def rows_gen(r0):
        rows = slice(r0, r0 + sub)
        mix = (mm(o_ref[rows, :].astype(BF16), wo_ref[0:dn_w, :])
               + mm(y_ref[rows, :].astype(BF16), wo_ref[dn_w:, :]))
        yield
        h1 = x_ref[rows, :] + mix
        q = mm(_rms(h1, gx_ref[...]).astype(BF16), wq_ref[...])
        yield
        q = q.astype(BF16)
        hls = [slice(h * xd, (h + 1) * xd) for h in range(xh)]
        s = [mm(q[:, hl], kt_ref[0, hl, :]) for hl in hls]
        yield
        parts = []
        for h in range(xh):
            sh = s[h] * scale
            e = jnp.exp(sh - jnp.max(sh, axis=-1, keepdims=True))
            p = (e / jnp.sum(e, axis=-1, keepdims=True)).astype(BF16)
            parts.append(mm(p, v_ref[0, :, hls[h]]))
            if h % 2 == 1:
                yield
        att = jnp.concatenate([p.astype(BF16) for p in parts], axis=1)
        proj = mm(att, wxo_ref[...])
        yield
        h_ref[rows, :] = h1 + proj

    _interleave([rows_gen(i * sub) for i in range(n_sub)])


def _mix_attn(x2, o, y, w_out, gx, wq, kt, v, wxo, bsz, seq, *, tm):
    nt, d = x2.shape
    dn_w = o.shape[1]
    n_mem = v.shape[1]
    n_t = seq // tm
    row = lambda b, t: (b * n_t + t, 0)
    c2 = lambda b, t: (0, 0)
    body = functools.partial(_mix_attn_body, xh=X_HEADS, n_sub=2)
    return pl.pallas_call(
        body,
        grid=(bsz, n_t),
        in_specs=[
            pl.BlockSpec((tm, d), row),
            pl.BlockSpec((tm, dn_w), row),
            pl.BlockSpec((tm, y.shape[1]), row),
            pl.BlockSpec(w_out.shape, c2),
            pl.BlockSpec((1, d), c2),
            pl.BlockSpec((d, d), c2),
            pl.BlockSpec((1, d, n_mem), lambda b, t: (b, 0, 0)),
            pl.BlockSpec((1, n_mem, d), lambda b, t: (b, 0, 0)),
            pl.BlockSpec((d, d), c2),
        ],
        out_specs=pl.BlockSpec((tm, d), row),
        out_shape=jax.ShapeDtypeStruct((nt, d), F32),
        compiler_params=_params("arbitrary", "arbitrary"),
        name="mix_attn",
    )(x2, o, y, w_out, gx, wq, kt, v, wxo)


def _ffn_body(h_ref, gf_ref, wg_ref, wu_ref, wd_ref, gl_ref, out_ref, *, final, fc):
    h = h_ref[...]
    hn = _rms(h, gf_ref[...]).astype(BF16)
    dff = wg_ref.shape[1]

    def down(gate, up, c0):
        act = (gate * jax.nn.sigmoid(gate) * up).astype(BF16)
        return jnp.dot(act, wd_ref[c0:c0 + fc, :], preferred_element_type=F32)

    h3 = h
    prev = None
    for c0 in range(0, dff, fc):
        gate = jnp.dot(hn, wg_ref[:, c0:c0 + fc], preferred_element_type=F32)
        up = jnp.dot(hn, wu_ref[:, c0:c0 + fc], preferred_element_type=F32)
        if prev is not None:
            h3 = h3 + down(*prev)
        prev = (gate, up, c0)
    h3 = h3 + down(*prev)
    out_ref[...] = _rms(h3, gl_ref[...]) if final else h3


def _ffn(h, gf, wg, wu, wd, gl, *, tm, final):
    nt, d = h.shape
    dff = wg.shape[1]
    c2 = lambda i: (0, 0)
    row = lambda i: (i, 0)
    return pl.pallas_call(
        functools.partial(_ffn_body, final=final, fc=2 * LANES),
        grid=(nt // tm,),
        in_specs=[
            pl.BlockSpec((tm, d), row),
            pl.BlockSpec((1, d), c2),
            pl.BlockSpec((d, dff), c2, pipeline_mode=pl.Buffered(1)),
            pl.BlockSpec((d, dff), c2, pipeline_mode=pl.Buffered(1)),
            pl.BlockSpec((dff, d), c2, pipeline_mode=pl.Buffered(1)),
            pl.BlockSpec((1, d), c2),
        ],
        out_specs=pl.BlockSpec((tm, d), row),
        out_shape=jax.ShapeDtypeStruct((nt, d), F32),
        compiler_params=_params("arbitrary"),
        name="ffn",
    )(h, gf, wg, wu, wd, gl)


def _block_diag(blocks):
    n, r, c = blocks.shape
    eye = jnp.eye(n, dtype=blocks.dtype)
    return (blocks[:, :, None, :] * eye[:, None, :, None]).reshape(n * r, n * c)


def _layer(h2, mem, bsz, seq, norm_mix_g, w_in, conv_w, dn_a_log, dn_dt_bias, dn_norm_g,
           s5_a_re, s5_a_im, s5_b_re, s5_b_im, s5_c_re, s5_c_im, s5_d, s5_log_dt,
           s5_w_glu, s5_b_glu, w_out, norm_x_g, norm_mem_g, w_xq, w_xk, w_xv, w_xo):
    d = h2.shape[1]
    nh = dn_a_log.shape[0]
    dh = dn_norm_g.shape[0]
    dn_w = nh * dh
    n_grp, n_state = s5_a_re.shape
    cg = s5_b_re.shape[-1]
    s5_w = n_grp * cg

    off_a = 4 * dn_w
    off_u = off_a + 2 * nh
    w_ab = jnp.pad(w_in[:, off_a:off_u], ((0, 0), (0, LANES - 2 * nh)))
    w_cat = jnp.concatenate([w_in[:, :off_a], w_in[:, off_u:], w_ab], axis=1).astype(BF16)
    alog = jnp.pad(dn_a_log, (0, LANES - nh)).reshape(1, LANES)
    dtb = jnp.pad(dn_dt_bias, (0, LANES - nh)).reshape(1, LANES)

    q, k, kt, v, z, u, gt = _in_proj(
        h2, bsz, seq, norm_mix_g.reshape(1, d), w_cat, conv_w, alog, dtb, nh=nh, dh=dh, tm=512)
    o = _delta(q, k, kt, v, z, gt, dn_norm_g.reshape(1, dh), bsz, seq, nh=nh, dh=dh, tb=256)

    pin_re, pin_im, pout_re, pout_im, al_re, al_im, bb_re, bb_im = _s5_prep(
        s5_a_re, s5_a_im, s5_log_dt, s5_b_re, s5_b_im, chunk=S5_CHUNK)
    n_half = 2
    gh = n_grp // n_half
    sw = gh * n_state
    split = lambda a: a.reshape(a.shape[0], n_half, sw).transpose(1, 0, 2)
    tabs = tuple(split(a) for a in (pin_re, pin_im, pout_re, pout_im, al_re, al_im))
    bbr = bb_re.reshape(n_half, gh, n_state, cg).transpose(0, 1, 3, 2)
    bbi = bb_im.reshape(n_half, gh, n_state, cg).transpose(0, 1, 3, 2)
    wb = jnp.stack([jnp.concatenate([_block_diag(bbr[i]), _block_diag(bbi[i])], axis=1)
                    for i in range(n_half)]).astype(BF16)
    ccr = s5_c_re.reshape(n_half, gh, cg, n_state).transpose(0, 1, 3, 2)
    cci = s5_c_im.reshape(n_half, gh, cg, n_state).transpose(0, 1, 3, 2)
    wc = jnp.stack([jnp.concatenate([_block_diag(ccr[i]), -_block_diag(cci[i])], axis=0)
                    for i in range(n_half)]).astype(BF16)
    y = _s5(u, wb, wc, tabs, s5_d.reshape(1, s5_w), s5_w_glu.astype(BF16),
            s5_b_glu.reshape(1, s5_w), bsz, seq, ts=256, chunk=S5_CHUNK)

    kt_mem, v_mem = _mem_kv(mem, norm_mem_g.reshape(1, d), w_xk.astype(BF16), w_xv.astype(BF16))
    return _mix_attn(h2, o, y, w_out.astype(BF16), norm_x_g.reshape(1, d), w_xq.astype(BF16),
                     kt_mem, v_mem, w_xo.astype(BF16), bsz, seq, tm=512)


def kernel(x, mem, norm_mix_g, w_in, conv_w, dn_a_log, dn_dt_bias, dn_norm_g, s5_a_re, s5_a_im,
           s5_b_re, s5_b_im, s5_c_re, s5_c_im, s5_d, s5_log_dt, s5_w_glu, s5_b_glu, w_out,
           norm_x_g, norm_mem_g, w_xq, w_xk, w_xv, w_xo, norm_ffn_g, w_gate, w_up, w_down,
           norm_final_g):
    bsz, seq, d = x.shape
    depth = w_in.shape[0]
    h = x.reshape(bsz * seq, d)
    for l in range(depth):
        h = _layer(h, mem, bsz, seq, norm_mix_g[l], w_in[l], conv_w[l], dn_a_log[l],
                   dn_dt_bias[l], dn_norm_g[l], s5_a_re[l], s5_a_im[l], s5_b_re[l], s5_b_im[l],
                   s5_c_re[l], s5_c_im[l], s5_d[l], s5_log_dt[l], s5_w_glu[l], s5_b_glu[l],
                   w_out[l], norm_x_g[l], norm_mem_g[l], w_xq[l], w_xk[l], w_xv[l], w_xo[l])
        h = _ffn(h, norm_ffn_g[l].reshape(1, d), w_gate[l].astype(BF16), w_up[l].astype(BF16),
                 w_down[l].astype(BF16), norm_final_g.reshape(1, d), tm=512, final=l == depth - 1)
    return h.reshape(bsz, seq, d)
```

```python
import functools
import math

import jax
import jax.numpy as jnp
from jax import lax
from jax.experimental import pallas as pl
from jax.experimental.pallas import tpu as pltpu

F32 = jnp.float32
BF16 = jnp.bfloat16
EPS = 1e-6
HIGHEST = lax.Precision.HIGHEST

LANES = 128
SUBLANES = 8
DN_CHUNK = 64
S5_CHUNK = 64
X_HEADS = 4
VMEM_LIMIT = 56 * 1024 * 1024


def _bdot(a, b):
    return jnp.dot(a.astype(BF16), b.astype(BF16), preferred_element_type=F32)


def _fdot(a, b):
    return jnp.dot(a, b, precision=HIGHEST, preferred_element_type=F32)


def _rms(x, g):
    return x * lax.rsqrt(jnp.mean(x * x, axis=-1, keepdims=True) + EPS) * g


def _interleave(gens):
    pending = list(gens)
    while pending:
        for gen in list(pending):
            try:
                next(gen)
            except StopIteration:
                pending.remove(gen)


def _params(*sem):
    return pltpu.CompilerParams(dimension_semantics=sem, vmem_limit_bytes=VMEM_LIMIT)


def _in_proj_body(x_ref, g_ref, w_ref, cw_ref, alog_ref, dtb_ref,
                  q_ref, k_ref, kt_ref, v_ref, z_ref, u_ref, gt_ref,
                  buf_ref, *, tm, dn_w, nh, dh, cw_n):
    t = pl.program_id(1)
    qkv_w = 3 * dn_w

    @pl.when(t == 0)
    def _():
        buf_ref[:, 0:SUBLANES, :] = jnp.zeros((qkv_w // LANES, SUBLANES, LANES), F32)

    xnb = _rms(x_ref[...], g_ref[...]).astype(BF16)
    gw = 2 * dh

    def conv_silu(p, cols):
        outs = []
        for i in range(gw // LANES):
            c0 = cols.start + i * LANES
            s = c0 // LANES
            pc = p[:, i * LANES:(i + 1) * LANES]
            buf_ref[s, SUBLANES:SUBLANES + tm, :] = pc
            acc = cw_ref[cw_n - 1:cw_n, c0:c0 + LANES] * pc
            for j in range(cw_n - 1):
                off = SUBLANES - (cw_n - 1) + j
                acc = acc + cw_ref[j:j + 1, c0:c0 + LANES] * buf_ref[s, off:off + tm, :]
            buf_ref[s, 0:SUBLANES, :] = buf_ref[s, tm:tm + SUBLANES, :]
            outs.append(acc * jax.nn.sigmoid(acc))
        return jnp.concatenate(outs, axis=1)

    def l2n(a, scale):
        parts = []
        for i in range(gw // dh):
            ah = a[:, i * dh:(i + 1) * dh]
            parts.append(ah * (lax.rsqrt(jnp.sum(ah * ah, -1, keepdims=True) + EPS) * scale))
        return jnp.concatenate(parts, axis=1)

    def ep_q(p, c0):
        q_ref[:, c0:c0 + gw] = l2n(conv_silu(p, slice(c0, c0 + gw)), dh ** -0.5)

    def ep_k(p, c0):
        kn = l2n(conv_silu(p, slice(dn_w + c0, dn_w + c0 + gw)), 1.0)
        k_ref[:, c0:c0 + gw] = kn
        kt_ref[0, c0:c0 + gw, :] = kn.T

    def ep_v(p, c0):
        v_ref[:, c0:c0 + gw] = conv_silu(p, slice(2 * dn_w + c0, 2 * dn_w + c0 + gw))

    def ep_z(p, c0):
        z_ref[:, c0:c0 + gw] = p

    def ep_u(p, c0):
        u_ref[:, c0:c0 + gw] = p

    def ep_gates(ab, c0):
        lane = lax.broadcasted_iota(jnp.int32, ab.shape, 1)
        g = -jnp.exp(alog_ref[...]) * jnp.logaddexp(ab + dtb_ref[...], 0.0)
        gates = jnp.where(lane < nh, g, jax.nn.sigmoid(ab))
        gt_ref[0] = gates.T[0:SUBLANES, :]

    groups = []
    for i, ep in enumerate((ep_q, ep_k, ep_v, ep_z, ep_u)):
        groups += [(ep, i * dn_w + c0, gw, c0) for c0 in range(0, dn_w, gw)]
    groups.append((ep_gates, 5 * dn_w, LANES, 0))

    prev = None
    for ep, w0, width, c0 in groups:
        p = jnp.dot(xnb, w_ref[:, w0:w0 + width], preferred_element_type=F32)
        if prev is not None:
            prev[0](prev[1], prev[2])
        prev = (ep, p, c0)
    prev[0](prev[1], prev[2])


def _in_proj(x2, bsz, seq, g, w_cat, conv_w, alog, dtb, *, nh, dh, tm):
    nt, d = x2.shape
    dn_w = nh * dh
    n_t = seq // tm
    cw_n = conv_w.shape[0]
    wcols = w_cat.shape[1]
    row = lambda b, t: (b * n_t + t, 0)
    const = lambda b, t: (0, 0)
    out_rows = lambda w: pl.BlockSpec((tm, w), row)
    body = functools.partial(_in_proj_body, tm=tm, dn_w=dn_w, nh=nh, dh=dh, cw_n=cw_n)
    return pl.pallas_call(
        body,
        grid=(bsz, n_t),
        in_specs=[
            pl.BlockSpec((tm, d), row),
            pl.BlockSpec((1, d), const),
            pl.BlockSpec((d, wcols), const),
            pl.BlockSpec((cw_n, 3 * dn_w), const),
            pl.BlockSpec((1, LANES), const),
            pl.BlockSpec((1, LANES), const),
        ],
        out_specs=[
            out_rows(dn_w), out_rows(dn_w),
            pl.BlockSpec((1, dn_w, tm), lambda b, t: (b, 0, t)),
            out_rows(dn_w), out_rows(dn_w), out_rows(dn_w),
            pl.BlockSpec((1, SUBLANES, tm), lambda b, t: (b, 0, t)),
        ],
        out_shape=[
            jax.ShapeDtypeStruct((nt, dn_w), F32), jax.ShapeDtypeStruct((nt, dn_w), F32),
            jax.ShapeDtypeStruct((bsz, dn_w, seq), F32),
            jax.ShapeDtypeStruct((nt, dn_w), F32), jax.ShapeDtypeStruct((nt, dn_w), F32),
            jax.ShapeDtypeStruct((nt, dn_w), F32),
            jax.ShapeDtypeStruct((bsz, SUBLANES, seq), F32),
        ],
        scratch_shapes=[pltpu.VMEM((3 * dn_w // LANES, tm + SUBLANES, LANES), F32)],
        compiler_params=_params("arbitrary", "arbitrary"),
        name="in_proj",
    )(x2, g, w_cat, conv_w, alog, dtb)


def _delta_body(q_ref, k_ref, kt_ref, v_ref, z_ref, gt_ref, ng_ref,
                o_ref, s_ref, u_sc, w_sc, qg_sc, attn_sc, kdt_sc, gtot_sc, *, tb, nh, dh, ch):
    t = pl.program_id(1)
    nc = tb // ch

    wr = t % 2
    rd = 1 - wr

    @pl.when(t == 0)
    def _():
        for ref in (s_ref, u_sc, w_sc, qg_sc, attn_sc, kdt_sc, gtot_sc):
            ref[...] = jnp.zeros(ref.shape, ref.dtype)

    heads = range(nh)
    hls = [slice(h * dh, (h + 1) * dh) for h in heads]
    mm = functools.partial(jnp.dot, preferred_element_type=F32)

    def phase_b():
        gtot_b = gtot_sc[rd]
        s_all = [s_ref[h] for h in heads]
        for cidx in range(nc):
            rows = slice(cidx * ch, (cidx + 1) * ch)
            sb = [s_all[h].astype(BF16) for h in heads]
            ws = [mm(w_sc[rd, h, rows, :], sb[h]) for h in heads]
            qs = [mm(qg_sc[rd, h, rows, :], sb[h]) for h in heads]
            yield
            vb = [(u_sc[rd, h, rows, :] - ws[h]).astype(BF16) for h in heads]
            av = [mm(attn_sc[rd, h, rows, rows], vb[h]) for h in heads]
            kv = [mm(kdt_sc[rd, h, :, rows], vb[h]) for h in heads]
            yield
            outs = []
            for h in heads:
                a_last = jnp.exp(gtot_b[h:h + 1, cidx * ch:cidx * ch + 1])
                s_all[h] = s_all[h] * a_last + kv[h]
                o = qs[h] + av[h]
                o = o * lax.rsqrt(jnp.mean(o * o, -1, keepdims=True) + EPS)
                zh = z_ref[rows, hls[h]]
                outs.append(o * ng_ref[...] * (zh * jax.nn.sigmoid(zh)))
            o_ref[rows, :] = jnp.concatenate(outs, axis=1)
        for h in heads:
            s_ref[h] = s_all[h]

    r = lax.broadcasted_iota(jnp.int32, (tb, tb), 0)
    c = lax.broadcasted_iota(jnp.int32, (tb, tb), 1)
    same = (r // ch) == (c // ch)
    causal_bd = same & (c <= r)
    strict_bd = same & (c < r)
    gt = gt_ref[0]
    gcum_t = _fdot(gt, jnp.where(same & (r <= c), 1.0, 0.0))
    gtot_t = _fdot(gt, jnp.where(same, 1.0, 0.0))
    gcum = gcum_t.T
    gates = gt.T

    def pack(x):
        acc = x[0:ch, :]
        for i in range(1, nc):
            acc = acc + x[i * ch:(i + 1) * ch, :]
        return acc

    def unpack(x):
        return jnp.where(same, jnp.concatenate([x] * nc, axis=0), 0.0).astype(BF16)

    n_sq = max(1, (ch - 1).bit_length() - 1)

    def phase_a():
        gcol = [gcum[:, h:h + 1] for h in heads]
        grow = [gcum_t[h:h + 1, :] for h in heads]
        beta = [gates[:, nh + h:nh + h + 1] for h in heads]
        kb = [k_ref[:, hls[h]] * beta[h] for h in heads]
        gram = [mm(jnp.concatenate([kb[h], q_ref[:, hls[h]]], axis=0).astype(BF16),
                   kt_ref[0, hls[h], :].astype(BF16)) for h in heads]
        yield
        ppk, apk = [], []
        for h in heads:
            decay = jnp.exp(jnp.where(causal_bd, gcol[h] - grow[h], -jnp.inf))
            p_bd = -jnp.where(strict_bd, gram[h][:tb] * decay, 0.0)
            attn_sc[wr, h] = (gram[h][tb:] * decay).astype(BF16)
            pk = pack(p_bd)
            apk.append(pk)
            ppk.append(mm(pk.astype(BF16), p_bd.astype(BF16)))
        yield
        for j in range(1, n_sq + 1):
            res = []
            for h in heads:
                lhs = jnp.concatenate([apk[h], ppk[h]], axis=0) if j < n_sq else apk[h]
                res.append(mm(lhs.astype(BF16), unpack(ppk[h])))
            yield
            for h in heads:
                apk[h] = apk[h] + ppk[h] + res[h][:ch]
                if j < n_sq:
                    ppk[h] = res[h][ch:]
        egc = [jnp.exp(gcol[h]) for h in heads]
        rhs = [jnp.concatenate([v_ref[:, hls[h]] * beta[h], kb[h] * egc[h]], axis=1) for h in heads]
        prod = [mm(unpack(apk[h]), rhs[h].astype(BF16)) for h in heads]
        yield
        for h in heads:
            uw = rhs[h] + prod[h]
            u_sc[wr, h] = uw[:, :dh]
            w_sc[wr, h] = uw[:, dh:].astype(BF16)
            qg_sc[wr, h] = (q_ref[:, hls[h]] * egc[h]).astype(BF16)
            kdt_sc[wr, h] = (kt_ref[0, hls[h], :] * jnp.exp(gtot_t[h:h + 1, :] - grow[h])).astype(BF16)
        gtot_sc[wr] = gtot_t

    _interleave([phase_b(), phase_a()])


def _delta(q, k, kt, v, z, gt, ng, bsz, seq, *, nh, dh, tb):
    nt, dn_w = q.shape
    n_t = seq // tb
    a_blk = lambda t: jnp.minimum(t, n_t - 1)
    b_blk = lambda t: jnp.maximum(t - 1, 0)
    a_rows = pl.BlockSpec((tb, dn_w), lambda b, t: (b * n_t + a_blk(t), 0))
    b_rows = pl.BlockSpec((tb, dn_w), lambda b, t: (b * n_t + b_blk(t), 0))
    body = functools.partial(_delta_body, tb=tb, nh=nh, dh=dh, ch=DN_CHUNK)
    return pl.pallas_call(
        body,
        grid=(bsz, n_t + 1),
        in_specs=[
            a_rows, a_rows,
            pl.BlockSpec((1, dn_w, tb), lambda b, t: (b, 0, a_blk(t))),
            a_rows, b_rows,
            pl.BlockSpec((1, SUBLANES, tb), lambda b, t: (b, 0, a_blk(t))),
            pl.BlockSpec((1, dh), lambda b, t: (0, 0)),
        ],
        out_specs=b_rows,
        out_shape=jax.ShapeDtypeStruct((nt, dn_w), F32),
        scratch_shapes=[
            pltpu.VMEM((nh, dh, dh), F32),
            pltpu.VMEM((2, nh, tb, dh), F32),
            pltpu.VMEM((2, nh, tb, dh), BF16),
            pltpu.VMEM((2, nh, tb, dh), BF16),
            pltpu.VMEM((2, nh, tb, tb), BF16),
            pltpu.VMEM((2, nh, dh, tb), BF16),
            pltpu.VMEM((2, SUBLANES, tb), F32),
        ],
        compiler_params=_params("arbitrary", "arbitrary"),
        name="delta",
    )(q, k, kt, v, z, gt, ng)


def _s5_prep_body(are_ref, aim_ref, ldt_ref, btr_ref, bti_ref, ctr_ref, cti_ref,
                  wb_ref, wc_ref, pin_re, pin_im, pout_re, pout_im, al_re, al_im,
                  *, chunk, n_half, n_state, cg):
    a_re = are_ref[...]
    a_im = aim_ref[...]
    dt = jnp.exp(ldt_ref[...])
    lre = a_re * dt
    lim = a_im * dt
    mag = jnp.exp(lre)
    nr = mag * jnp.cos(lim) - 1.0
    ni = mag * jnp.sin(lim)
    den = a_re * a_re + a_im * a_im
    cor = (nr * a_re + ni * a_im) / den
    coi = (ni * a_re - nr * a_im) / den
    bb_re = cor * btr_ref[...] - coi * bti_ref[...]
    bb_im = cor * bti_ref[...] + coi * btr_ref[...]

    n = lax.broadcasted_iota(jnp.int32, (chunk, 1), 0).astype(F32) - float(chunk // 2)
    m_out = jnp.exp(lre * n)
    m_in = jnp.exp(-(lre * n))
    ang = lim * n
    tabs = (m_in * jnp.cos(ang), -(m_in * jnp.sin(ang)), m_out * jnp.cos(ang), m_out * jnp.sin(ang))
    m_al = jnp.exp(lre * float(chunk))
    alr = m_al * jnp.cos(lim * float(chunk))
    ali = m_al * jnp.sin(lim * float(chunk))

    gp = a_re.shape[1]
    sw = gp // n_half
    gh = sw // n_state
    hw = gh * cg
    r = lax.broadcasted_iota(jnp.int32, (hw, sw), 0)
    c = lax.broadcasted_iota(jnp.int32, (hw, sw), 1)
    on_b = (r // cg) == (c // n_state)
    r2 = lax.broadcasted_iota(jnp.int32, (sw, hw), 0)
    c2 = lax.broadcasted_iota(jnp.int32, (sw, hw), 1)
    on_c = (r2 // n_state) == (c2 // cg)
    for hf in range(n_half):
        ls = slice(hf * sw, (hf + 1) * sw)
        for ref, tab in zip((pin_re, pin_im, pout_re, pout_im), tabs):
            ref[hf] = tab[:, ls].astype(ref.dtype)
        al_re[hf] = alr[:, ls]
        al_im[hf] = ali[:, ls]
        tile_b = lambda x: jnp.concatenate([x[:, ls]] * gh, axis=0)
        wb_ref[hf, :, :sw] = jnp.where(on_b, tile_b(bb_re), 0.0).astype(wb_ref.dtype)
        wb_ref[hf, :, sw:] = jnp.where(on_b, tile_b(bb_im), 0.0).astype(wb_ref.dtype)
        cs = slice(hf * hw, (hf + 1) * hw)
        tile_c = lambda x: jnp.concatenate([x[:, cs]] * gh, axis=0)
        wc_ref[hf, :sw, :] = jnp.where(on_c, tile_c(ctr_ref[...]), 0.0).astype(wc_ref.dtype)
        wc_ref[hf, sw:, :] = jnp.where(on_c, -tile_c(cti_ref[...]), 0.0).astype(wc_ref.dtype)


def _s5_prep(a_re, a_im, log_dt, b_re, b_im, c_re, c_im, *, chunk, n_half):
    g, p = a_re.shape
    cg = b_re.shape[-1]
    gp = g * p
    sw = gp // n_half
    hw = g * cg // n_half
    row = lambda a: a.reshape(1, gp)
    tab = jax.ShapeDtypeStruct((n_half, chunk, sw), BF16)
    one = jax.ShapeDtypeStruct((n_half, 1, sw), F32)
    body = functools.partial(_s5_prep_body, chunk=chunk, n_half=n_half, n_state=p, cg=cg)
    return pl.pallas_call(
        body,
        out_shape=[jax.ShapeDtypeStruct((n_half, hw, 2 * sw), BF16),
                   jax.ShapeDtypeStruct((n_half, 2 * sw, hw), BF16),
                   tab, tab, tab, tab, one, one],
        compiler_params=pltpu.CompilerParams(vmem_limit_bytes=VMEM_LIMIT),
        name="s5_prep",
    )(row(a_re), row(a_im), row(jnp.broadcast_to(log_dt[:, None], (g, p))),
      b_re.reshape(gp, cg).T, b_im.reshape(gp, cg).T,
      c_re.transpose(2, 0, 1).reshape(p, g * cg), c_im.transpose(2, 0, 1).reshape(p, g * cg))


def _s5_body(u_ref, wb_ref, wc_ref, pin_re, pin_im, pout_re, pout_im, al_re, al_im,
             d_ref, wglu_ref, bglu_ref, y_ref, carry_ref, xs_ref, *, ts, chunk, n_half):
    t = pl.program_id(1)

    @pl.when(t == 0)
    def _():
        carry_ref[...] = jnp.zeros(carry_ref.shape, F32)

    u = u_ref[...]
    ub = u.astype(BF16)
    hw = u.shape[1] // n_half
    sw = pin_re.shape[2]
    ri = lax.broadcasted_iota(jnp.int32, (chunk, chunk), 0)
    ci = lax.broadcasted_iota(jnp.int32, (chunk, chunk), 1)
    tri = jnp.where(ci <= ri, 1.0, 0.0).astype(BF16)
    ys = [None] * n_half

    def half(hf):
        bu = jnp.dot(ub[:, hf * hw:(hf + 1) * hw], wb_ref[hf], preferred_element_type=F32)
        yield
        pr, pi = pin_re[hf], pin_im[hf]
        qr, qi = pout_re[hf], pout_im[hf]
        ar, ai = al_re[hf], al_im[hf]
        carry = [carry_ref[2 * hf:2 * hf + 1, :], carry_ref[2 * hf + 1:2 * hf + 2, :]]

        def prefix(cidx):
            rows = slice(cidx * chunk, (cidx + 1) * chunk)
            bre = bu[rows, :sw].astype(BF16)
            bim = bu[rows, sw:].astype(BF16)
            sre = pr * bre - pi * bim
            sim = pr * bim + pi * bre
            return jnp.dot(tri, jnp.concatenate([sre, sim], axis=1), preferred_element_type=F32)

        def finish(cidx, cs):
            rows = slice(cidx * chunk, (cidx + 1) * chunk)
            tre = cs[:, :sw] + carry[0]
            tim = cs[:, sw:] + carry[1]
            trb = tre.astype(BF16)
            tib = tim.astype(BF16)
            xs_ref[hf, rows, :sw] = qr * trb - qi * tib
            xs_ref[hf, rows, sw:] = qr * tib + qi * trb
            lre = tre[chunk - 1:chunk, :]
            lim = tim[chunk - 1:chunk, :]
            carry[0] = ar * lre - ai * lim
            carry[1] = ar * lim + ai * lre

        prev = None
        for cidx in range(ts // chunk):
            cs = prefix(cidx)
            if prev is not None:
                finish(*prev)
            prev = (cidx, cs)
            yield
        finish(*prev)
        carry_ref[2 * hf:2 * hf + 1, :] = carry[0]
        carry_ref[2 * hf + 1:2 * hf + 2, :] = carry[1]
        yield
        ys[hf] = jnp.dot(xs_ref[hf], wc_ref[hf], preferred_element_type=F32)

    _interleave([half(hf) for hf in range(n_half)])
    y = jnp.concatenate(ys, axis=1) + d_ref[...] * u
    y = jax.nn.gelu(y)
    gate = jax.nn.sigmoid(jnp.dot(y.astype(BF16), wglu_ref[...], preferred_element_type=F32)
                          + bglu_ref[...])
    y_ref[...] = y * gate


def _s5(u, wb, wc, tabs, d, wglu, bglu, bsz, seq, *, ts, chunk):
    nt, width = u.shape
    n_half = wb.shape[0]
    sw2 = wb.shape[2]
    sw = sw2 // 2
    n_t = seq // ts
    row = lambda b, t: (b * n_t + t, 0)
    c2 = lambda b, t: (0, 0)
    c3 = lambda b, t: (0, 0, 0)
    tab = pl.BlockSpec((n_half, chunk, sw), c3)
    one = pl.BlockSpec((n_half, 1, sw), c3)
    body = functools.partial(_s5_body, ts=ts, chunk=chunk, n_half=n_half)
    return pl.pallas_call(
        body,
        grid=(bsz, n_t),
        in_specs=[
            pl.BlockSpec((ts, width), row),
            pl.BlockSpec(wb.shape, c3),
            pl.BlockSpec(wc.shape, c3),
            tab, tab, tab, tab, one, one,
            pl.BlockSpec((1, width), c2),
            pl.BlockSpec(wglu.shape, c2),
            pl.BlockSpec((1, width), c2),
        ],
        out_specs=pl.BlockSpec((ts, width), row),
        out_shape=jax.ShapeDtypeStruct((nt, width), F32),
        scratch_shapes=[pltpu.VMEM((2 * n_half, sw), F32), pltpu.VMEM((n_half, ts, sw2), BF16)],
        compiler_params=_params("arbitrary", "arbitrary"),
        name="s5",
    )(u, wb, wc, *tabs, d, wglu, bglu)


def _mem_kv_body(m_ref, g_ref, wk_ref, wv_ref, kt_ref, v_ref):
    mn = _rms(m_ref[0], g_ref[...]).astype(BF16)
    k = jnp.dot(mn, wk_ref[...], preferred_element_type=F32)
    kt_ref[0] = k.T.astype(BF16)
    v_ref[0] = jnp.dot(mn, wv_ref[...], preferred_element_type=F32).astype(BF16)


def _mem_kv(mem, g, wk, wv):
    bsz, n_mem, d = mem.shape
    c2 = lambda b: (0, 0)
    return pl.pallas_call(
        _mem_kv_body,
        grid=(bsz,),
        in_specs=[
            pl.BlockSpec((1, n_mem, d), lambda b: (b, 0, 0)),
            pl.BlockSpec((1, d), c2),
            pl.BlockSpec((d, d), c2),
            pl.BlockSpec((d, d), c2),
        ],
        out_specs=[
            pl.BlockSpec((1, d, n_mem), lambda b: (b, 0, 0)),
            pl.BlockSpec((1, n_mem, d), lambda b: (b, 0, 0)),
        ],
        out_shape=[
            jax.ShapeDtypeStruct((bsz, d, n_mem), BF16),
            jax.ShapeDtypeStruct((bsz, n_mem, d), BF16),
        ],
        compiler_params=_params("arbitrary"),
        name="mem_kv",
    )(mem, g, wk, wv)


def _mix_attn_body(x_ref, o_ref, y_ref, wo_ref, gx_ref, wq_ref, kt_ref, v_ref, wxo_ref,
                   h_ref, *, xh, n_sub):
    dn_w = o_ref.shape[1]
    d = x_ref.shape[1]
    xd = d // xh
    scale = xd ** -0.5
    sub = x_ref.shape[0] // n_sub
    mm = functools.partial(jnp.dot, preferred_element_type=F32)

    def rows_gen(r0):
        rows = slice(r0, r0 + sub)
        mix = (mm(o_ref[rows, :].astype(BF16), wo_ref[0:dn_w, :])
               + mm(y_ref[rows, :].astype(BF16), wo_ref[dn_w:, :]))
        yield
        h1 = x_ref[rows, :] + mix
        q = mm(_rms(h1, gx_ref[...]).astype(BF16), wq_ref[...])
        yield
        q = q.astype(BF16)
        hls = [slice(h * xd, (h + 1) * xd) for h in range(xh)]
        s = [mm(q[:, hl], kt_ref[0, hl, :]) for hl in hls]
        yield
        parts = []
        for h in range(xh):
            sh = s[h] * scale
            e = jnp.exp(sh - jnp.max(sh, axis=-1, keepdims=True))
            p = (e / jnp.sum(e, axis=-1, keepdims=True)).astype(BF16)
            parts.append(mm(p, v_ref[0, :, hls[h]]))
            if h % 2 == 1:
                yield
        att = jnp.concatenate([p.astype(BF16) for p in parts], axis=1)
        proj = mm(att, wxo_ref[...])
        yield
        h_ref[rows, :] = h1 + proj

    _interleave([rows_gen(i * sub) for i in range(n_sub)])


def _mix_attn(x2, o, y, w_out, gx, wq, kt, v, wxo, bsz, seq, *, tm):
    nt, d = x2.shape
    dn_w = o.shape[1]
    n_mem = v.shape[1]
    n_t = seq // tm
    row = lambda b, t: (b * n_t + t, 0)
    c2 = lambda b, t: (0, 0)
    body = functools.partial(_mix_attn_body, xh=X_HEADS, n_sub=2)
    return pl.pallas_call(
        body,
        grid=(bsz, n_t),
        in_specs=[
            pl.BlockSpec((tm, d), row),
            pl.BlockSpec((tm, dn_w), row),
            pl.BlockSpec((tm, y.shape[1]), row),
            pl.BlockSpec(w_out.shape, c2),
            pl.BlockSpec((1, d), c2),
            pl.BlockSpec((d, d), c2),
            pl.BlockSpec((1, d, n_mem), lambda b, t: (b, 0, 0)),
            pl.BlockSpec((1, n_mem, d), lambda b, t: (b, 0, 0)),
            pl.BlockSpec((d, d), c2),
        ],
        out_specs=pl.BlockSpec((tm, d), row),
        out_shape=jax.ShapeDtypeStruct((nt, d), F32),
        compiler_params=_params("arbitrary", "arbitrary"),
        name="mix_attn",
    )(x2, o, y, w_out, gx, wq, kt, v, wxo)


def _ffn_body(h_ref, gf_ref, wg_ref, wu_ref, wd_ref, gl_ref, out_ref, *, final, fc):
    h = h_ref[...]
    hn = _rms(h, gf_ref[...]).astype(BF16)
    dff = wg_ref.shape[1]

    def down(gate, up, c0):
        act = (gate * jax.nn.sigmoid(gate) * up).astype(BF16)
        return jnp.dot(act, wd_ref[c0:c0 + fc, :], preferred_element_type=F32)

    h3 = h
    prev = None
    for c0 in range(0, dff, fc):
        gate = jnp.dot(hn, wg_ref[:, c0:c0 + fc], preferred_element_type=F32)
        up = jnp.dot(hn, wu_ref[:, c0:c0 + fc], preferred_element_type=F32)
        if prev is not None:
            h3 = h3 + down(*prev)
        prev = (gate, up, c0)
    h3 = h3 + down(*prev)
    out_ref[...] = _rms(h3, gl_ref[...]) if final else h3


def _ffn(h, gf, wg, wu, wd, gl, *, tm, final):
    nt, d = h.shape
    dff = wg.shape[1]
    c2 = lambda i: (0, 0)
    row = lambda i: (i, 0)
    return pl.pallas_call(
        functools.partial(_ffn_body, final=final, fc=2 * LANES),
        grid=(nt // tm,),
        in_specs=[
            pl.BlockSpec((tm, d), row),
            pl.BlockSpec((1, d), c2),
            pl.BlockSpec((d, dff), c2, pipeline_mode=pl.Buffered(1)),
            pl.BlockSpec((d, dff), c2, pipeline_mode=pl.Buffered(1)),
            pl.BlockSpec((dff, d), c2, pipeline_mode=pl.Buffered(1)),
            pl.BlockSpec((1, d), c2),
        ],
        out_specs=pl.BlockSpec((tm, d), row),
        out_shape=jax.ShapeDtypeStruct((nt, d), F32),
        compiler_params=_params("arbitrary"),
        name="ffn",
    )(h, gf, wg, wu, wd, gl)


def _layer(h2, mem, bsz, seq, norm_mix_g, w_in, conv_w, dn_a_log, dn_dt_bias, dn_norm_g,
           s5_a_re, s5_a_im, s5_b_re, s5_b_im, s5_c_re, s5_c_im, s5_d, s5_log_dt,
           s5_w_glu, s5_b_glu, w_out, norm_x_g, norm_mem_g, w_xq, w_xk, w_xv, w_xo):
    d = h2.shape[1]
    nh = dn_a_log.shape[0]
    dh = dn_norm_g.shape[0]
    dn_w = nh * dh
    s5_w = s5_a_re.shape[0] * s5_b_re.shape[-1]

    off_a = 4 * dn_w
    off_u = off_a + 2 * nh
    w_ab = jnp.pad(w_in[:, off_a:off_u], ((0, 0), (0, LANES - 2 * nh)))
    w_cat = jnp.concatenate([w_in[:, :off_a], w_in[:, off_u:], w_ab], axis=1).astype(BF16)
    alog = jnp.pad(dn_a_log, (0, LANES - nh)).reshape(1, LANES)
    dtb = jnp.pad(dn_dt_bias, (0, LANES - nh)).reshape(1, LANES)

    q, k, kt, v, z, u, gt = _in_proj(
        h2, bsz, seq, norm_mix_g.reshape(1, d), w_cat, conv_w, alog, dtb, nh=nh, dh=dh, tm=512)
    o = _delta(q, k, kt, v, z, gt, dn_norm_g.reshape(1, dh), bsz, seq, nh=nh, dh=dh, tb=256)

    wb, wc, *tabs = _s5_prep(s5_a_re, s5_a_im, s5_log_dt, s5_b_re, s5_b_im, s5_c_re, s5_c_im,
                             chunk=S5_CHUNK, n_half=2)
    y = _s5(u, wb, wc, tabs, s5_d.reshape(1, s5_w), s5_w_glu.astype(BF16),
            s5_b_glu.reshape(1, s5_w), bsz, seq, ts=256, chunk=S5_CHUNK)

    kt_mem, v_mem = _mem_kv(mem, norm_mem_g.reshape(1, d), w_xk.astype(BF16), w_xv.astype(BF16))
    return _mix_attn(h2, o, y, w_out.astype(BF16), norm_x_g.reshape(1, d), w_xq.astype(BF16),
                     kt_mem, v_mem, w_xo.astype(BF16), bsz, seq, tm=512)


def kernel(x, mem, norm_mix_g, w_in, conv_w, dn_a_log, dn_dt_bias, dn_norm_g, s5_a_re, s5_a_im,
           s5_b_re, s5_b_im, s5_c_re, s5_c_im, s5_d, s5_log_dt, s5_w_glu, s5_b_glu, w_out,
           norm_x_g, norm_mem_g, w_xq, w_xk, w_xv, w_xo, norm_ffn_g, w_gate, w_up, w_down,
           norm_final_g):
    bsz, seq, d = x.shape
    depth = w_in.shape[0]
    h = x.reshape(bsz * seq, d)
    for l in range(depth):
        h = _layer(h, mem, bsz, seq, norm_mix_g[l], w_in[l], conv_w[l], dn_a_log[l],
                   dn_dt_bias[l], dn_norm_g[l], s5_a_re[l], s5_a_im[l], s5_b_re[l], s5_b_im[l],
                   s5_c_re[l], s5_c_im[l], s5_d[l], s5_log_dt[l], s5_w_glu[l], s5_b_glu[l],
                   w_out[l], norm_x_g[l], norm_mem_g[l], w_xq[l], w_xk[l], w_xv[l], w_xo[l])
        h = _ffn(h, norm_ffn_g[l].reshape(1, d), w_gate[l].astype(BF16), w_up[l].astype(BF16),
                 w_down[l].astype(BF16), norm_final_g.reshape(1, d), tm=512, final=l == depth - 1)
    return h.reshape(bsz, seq, d)
```

```python
import functools
import math

import jax
import jax.numpy as jnp
from jax import lax
from jax.experimental import pallas as pl
from jax.experimental.pallas import tpu as pltpu

F32 = jnp.float32
BF16 = jnp.bfloat16
EPS = 1e-6
HIGHEST = lax.Precision.HIGHEST

LANES = 128
SUBLANES = 8
DN_CHUNK = 64
S5_CHUNK = 64
X_HEADS = 4
VMEM_LIMIT = 56 * 1024 * 1024


def _bdot(a, b):
    return jnp.dot(a.astype(BF16), b.astype(BF16), preferred_element_type=F32)


def _fdot(a, b):
    return jnp.dot(a, b, precision=HIGHEST, preferred_element_type=F32)


def _rms(x, g):
    return x * lax.rsqrt(jnp.mean(x * x, axis=-1, keepdims=True) + EPS) * g


def _interleave(gens):
    pending = list(gens)
    while pending:
        for gen in list(pending):
            try:
                next(gen)
            except StopIteration:
                pending.remove(gen)


def _params(*sem):
    return pltpu.CompilerParams(dimension_semantics=sem, vmem_limit_bytes=VMEM_LIMIT)


def _in_proj_stages(x_ref, g_ref, w_ref, cw_ref, alog_ref, dtb_ref,
                    q_ref, k_ref, kt_ref, v_ref, z_ref, u_ref, gt_ref,
                    buf_ref, *, tm, dn_w, nh, dh, cw_n):
    xnb = _rms(x_ref[...], g_ref[...]).astype(BF16)
    gw = 2 * dh

    def conv_silu(p, cols):
        outs = []
        for i in range(gw // LANES):
            c0 = cols.start + i * LANES
            s = c0 // LANES
            pc = p[:, i * LANES:(i + 1) * LANES]
            buf_ref[s, SUBLANES:SUBLANES + tm, :] = pc
            acc = cw_ref[cw_n - 1:cw_n, c0:c0 + LANES] * pc
            for j in range(cw_n - 1):
                off = SUBLANES - (cw_n - 1) + j
                acc = acc + cw_ref[j:j + 1, c0:c0 + LANES] * buf_ref[s, off:off + tm, :]
            buf_ref[s, 0:SUBLANES, :] = buf_ref[s, tm:tm + SUBLANES, :]
            outs.append(acc * jax.nn.sigmoid(acc))
        return jnp.concatenate(outs, axis=1)

    def l2n(a, scale):
        parts = []
        for i in range(gw // dh):
            ah = a[:, i * dh:(i + 1) * dh]
            parts.append(ah * (lax.rsqrt(jnp.sum(ah * ah, -1, keepdims=True) + EPS) * scale))
        return jnp.concatenate(parts, axis=1)

    def ep_q(p, c0):
        q_ref[:, c0:c0 + gw] = l2n(conv_silu(p, slice(c0, c0 + gw)), dh ** -0.5)

    def ep_k(p, c0):
        kn = l2n(conv_silu(p, slice(dn_w + c0, dn_w + c0 + gw)), 1.0)
        k_ref[:, c0:c0 + gw] = kn
        kt_ref[c0:c0 + gw, :] = kn.T

    def ep_v(p, c0):
        v_ref[:, c0:c0 + gw] = conv_silu(p, slice(2 * dn_w + c0, 2 * dn_w + c0 + gw))

    def ep_z(p, c0):
        z_ref[:, c0:c0 + gw] = p

    def ep_u(p, c0):
        u_ref[:, c0:c0 + gw] = p

    def ep_gates(ab, c0):
        lane = lax.broadcasted_iota(jnp.int32, ab.shape, 1)
        g = -jnp.exp(alog_ref[...]) * jnp.logaddexp(ab + dtb_ref[...], 0.0)
        gates = jnp.where(lane < nh, g, jax.nn.sigmoid(ab))
        gt_ref[...] = gates.T[0:SUBLANES, :]

    groups = []
    for i, ep in enumerate((ep_q, ep_k, ep_v, ep_z, ep_u)):
        groups += [(ep, i * dn_w + c0, gw, c0) for c0 in range(0, dn_w, gw)]
    groups.append((ep_gates, 5 * dn_w, LANES, 0))

    prev = None
    for ep, w0, width, c0 in groups:
        p = jnp.dot(xnb, w_ref[:, w0:w0 + width], preferred_element_type=F32)
        if prev is not None:
            prev[0](prev[1], prev[2])
        prev = (ep, p, c0)
        yield
    prev[0](prev[1], prev[2])


def _delta_stages(q_ref, k_ref, kt_ref, v_ref, z_ref, gt_ref, ng_ref,
                  o_ref, s_ref, u_sc, w_sc, qg_sc, attn_sc, kdt_sc, gtot_sc, wr, rd,
                  *, tb, nh, dh, ch):
    nc = tb // ch
    heads = range(nh)
    hls = [slice(h * dh, (h + 1) * dh) for h in heads]
    mm = functools.partial(jnp.dot, preferred_element_type=F32)

    def phase_b():
        gtot_b = gtot_sc[rd]
        s_all = [s_ref[h] for h in heads]
        for cidx in range(nc):
            rows = slice(cidx * ch, (cidx + 1) * ch)
            sb = [s_all[h].astype(BF16) for h in heads]
            ws = [mm(w_sc[rd, h, rows, :], sb[h]) for h in heads]
            qs = [mm(qg_sc[rd, h, rows, :], sb[h]) for h in heads]
            yield
            vb = [(u_sc[rd, h, rows, :] - ws[h]).astype(BF16) for h in heads]
            av = [mm(attn_sc[rd, h, rows, rows], vb[h]) for h in heads]
            kv = [mm(kdt_sc[rd, h, :, rows], vb[h]) for h in heads]
            yield
            outs = []
            for h in heads:
                a_last = jnp.exp(gtot_b[h:h + 1, cidx * ch:cidx * ch + 1])
                s_all[h] = s_all[h] * a_last + kv[h]
                o = qs[h] + av[h]
                o = o * lax.rsqrt(jnp.mean(o * o, -1, keepdims=True) + EPS)
                zh = z_ref[rows, hls[h]]
                outs.append(o * ng_ref[...] * (zh * jax.nn.sigmoid(zh)))
            o_ref[rows, :] = jnp.concatenate(outs, axis=1)
        for h in heads:
            s_ref[h] = s_all[h]

    r = lax.broadcasted_iota(jnp.int32, (tb, tb), 0)
    c = lax.broadcasted_iota(jnp.int32, (tb, tb), 1)
    same = (r // ch) == (c // ch)
    causal_bd = same & (c <= r)
    strict_bd = same & (c < r)
    gt = gt_ref[...]
    gcum_t = _fdot(gt, jnp.where(same & (r <= c), 1.0, 0.0))
    gtot_t = _fdot(gt, jnp.where(same, 1.0, 0.0))
    gcum = gcum_t.T
    gates = gt.T

    def pack(x):
        acc = x[0:ch, :]
        for i in range(1, nc):
            acc = acc + x[i * ch:(i + 1) * ch, :]
        return acc

    def unpack(x):
        return jnp.where(same, jnp.concatenate([x] * nc, axis=0), 0.0).astype(BF16)

    n_sq = max(1, (ch - 1).bit_length() - 1)

    def phase_a():
        gcol = [gcum[:, h:h + 1] for h in heads]
        grow = [gcum_t[h:h + 1, :] for h in heads]
        beta = [gates[:, nh + h:nh + h + 1] for h in heads]
        kb = [k_ref[:, hls[h]] * beta[h] for h in heads]
        gram = [mm(jnp.concatenate([kb[h], q_ref[:, hls[h]]], axis=0).astype(BF16),
                   kt_ref[hls[h], :].astype(BF16)) for h in heads]
        yield
        ppk, apk = [], []
        for h in heads:
            decay = jnp.exp(jnp.where(causal_bd, gcol[h] - grow[h], -jnp.inf))
            p_bd = -jnp.where(strict_bd, gram[h][:tb] * decay, 0.0)
            attn_sc[wr, h] = (gram[h][tb:] * decay).astype(BF16)
            pk = pack(p_bd)
            apk.append(pk)
            ppk.append(mm(pk.astype(BF16), p_bd.astype(BF16)))
        yield
        for j in range(1, n_sq + 1):
            res = []
            for h in heads:
                lhs = jnp.concatenate([apk[h], ppk[h]], axis=0) if j < n_sq else apk[h]
                res.append(mm(lhs.astype(BF16), unpack(ppk[h])))
            yield
            for h in heads:
                apk[h] = apk[h] + ppk[h] + res[h][:ch]
                if j < n_sq:
                    ppk[h] = res[h][ch:]
        egc = [jnp.exp(gcol[h]) for h in heads]
        rhs = [jnp.concatenate([v_ref[:, hls[h]] * beta[h], kb[h] * egc[h]], axis=1) for h in heads]
        prod = [mm(unpack(apk[h]), rhs[h].astype(BF16)) for h in heads]
        yield
        for h in heads:
            uw = rhs[h] + prod[h]
            u_sc[wr, h] = uw[:, :dh]
            w_sc[wr, h] = uw[:, dh:].astype(BF16)
            qg_sc[wr, h] = (q_ref[:, hls[h]] * egc[h]).astype(BF16)
            kdt_sc[wr, h] = (kt_ref[hls[h], :] * jnp.exp(gtot_t[h:h + 1, :] - grow[h])).astype(BF16)
        gtot_sc[wr] = gtot_t

    return phase_b(), phase_a()


def _mixer_in_body(x_ref, g_ref, w_ref, cw_ref, alog_ref, dtb_ref, ng_ref,
                   u_ref, o_ref,
                   buf_ref, q_sc, k_sc, kt_sc, v_sc, z_sc, gt_sc,
                   s_ref, u_sc, w_sc, qg_sc, attn_sc, kdt_sc, gtot_sc,
                   *, tb, n_t, nh, dh, cw_n):
    s = pl.program_id(0)
    dn_w = nh * dh

    @pl.when(s == 0)
    def _():
        for ref in (buf_ref, q_sc, k_sc, kt_sc, v_sc, z_sc, gt_sc,
                    s_ref, u_sc, w_sc, qg_sc, attn_sc, kdt_sc, gtot_sc):
            ref[...] = jnp.zeros(ref.shape, ref.dtype)

    @pl.when(s % n_t == 0)
    def _():
        buf_ref[:, 0:SUBLANES, :] = jnp.zeros((3 * dn_w // LANES, SUBLANES, LANES), F32)

    @pl.when((s - 2) % n_t == 0)
    def _():
        s_ref[...] = jnp.zeros(s_ref.shape, F32)

    cur = s % 2
    prv = 1 - cur
    zw = s % 3
    zr = (s + 1) % 3
    proj = _in_proj_stages(
        x_ref, g_ref, w_ref, cw_ref, alog_ref, dtb_ref,
        q_sc.at[cur], k_sc.at[cur], kt_sc.at[cur], v_sc.at[cur], z_sc.at[zw], u_ref, gt_sc.at[cur],
        buf_ref, tm=tb, dn_w=dn_w, nh=nh, dh=dh, cw_n=cw_n)
    phase_b, phase_a = _delta_stages(
        q_sc.at[prv], k_sc.at[prv], kt_sc.at[prv], v_sc.at[prv], z_sc.at[zr], gt_sc.at[prv], ng_ref,
        o_ref, s_ref, u_sc, w_sc, qg_sc, attn_sc, kdt_sc, gtot_sc, cur, prv,
        tb=tb, nh=nh, dh=dh, ch=DN_CHUNK)
    _interleave([phase_b, phase_a, proj])


def _mixer_in(x2, seq, g, w_cat, conv_w, alog, dtb, ng, *, nh, dh, tb):
    nt, d = x2.shape
    dn_w = nh * dh
    n_blk = nt // tb
    cw_n = conv_w.shape[0]
    const = lambda s: (0, 0)
    in_blk = lambda s: (jnp.minimum(s, n_blk - 1), 0)
    out_blk = lambda s: (jnp.maximum(s - 2, 0), 0)
    body = functools.partial(_mixer_in_body, tb=tb, n_t=seq // tb, nh=nh, dh=dh, cw_n=cw_n)
    two = lambda *shape: pltpu.VMEM((2,) + shape, F32)
    return pl.pallas_call(
        body,
        grid=(n_blk + 2,),
        in_specs=[
            pl.BlockSpec((tb, d), in_blk),
            pl.BlockSpec((1, d), const),
            pl.BlockSpec(w_cat.shape, const),
            pl.BlockSpec((cw_n, 3 * dn_w), const),
            pl.BlockSpec((1, LANES), const),
            pl.BlockSpec((1, LANES), const),
            pl.BlockSpec((1, dh), const),
        ],
        out_specs=[pl.BlockSpec((tb, dn_w), in_blk), pl.BlockSpec((tb, dn_w), out_blk)],
        out_shape=[jax.ShapeDtypeStruct((nt, dn_w), F32), jax.ShapeDtypeStruct((nt, dn_w), F32)],
        scratch_shapes=[
            pltpu.VMEM((3 * dn_w // LANES, tb + SUBLANES, LANES), F32),
            two(tb, dn_w), two(tb, dn_w), two(dn_w, tb), two(tb, dn_w),
            pltpu.VMEM((3, tb, dn_w), F32),
            two(SUBLANES, tb),
            pltpu.VMEM((nh, dh, dh), F32),
            pltpu.VMEM((2, nh, tb, dh), F32),
            pltpu.VMEM((2, nh, tb, dh), BF16),
            pltpu.VMEM((2, nh, tb, dh), BF16),
            pltpu.VMEM((2, nh, tb, tb), BF16),
            pltpu.VMEM((2, nh, dh, tb), BF16),
            pltpu.VMEM((2, SUBLANES, tb), F32),
        ],
        compiler_params=_params("arbitrary"),
        name="mixer_in",
    )(x2, g, w_cat, conv_w, alog, dtb, ng)


def _s5_prep_body(are_ref, aim_ref, ldt_ref, btr_ref, bti_ref, ctr_ref, cti_ref,
                  wb_ref, wc_ref, pin_re, pin_im, pout_re, pout_im, al_re, al_im,
                  *, chunk, n_half, n_state, cg):
    a_re = are_ref[...]
    a_im = aim_ref[...]
    dt = jnp.exp(ldt_ref[...])
    lre = a_re * dt
    lim = a_im * dt
    mag = jnp.exp(lre)
    nr = mag * jnp.cos(lim) - 1.0
    ni = mag * jnp.sin(lim)
    den = a_re * a_re + a_im * a_im
    cor = (nr * a_re + ni * a_im) / den
    coi = (ni * a_re - nr * a_im) / den
    bb_re = cor * btr_ref[...] - coi * bti_ref[...]
    bb_im = cor * bti_ref[...] + coi * btr_ref[...]

    n = lax.broadcasted_iota(jnp.int32, (chunk, 1), 0).astype(F32) - float(chunk // 2)
    m_out = jnp.exp(lre * n)
    m_in = jnp.exp(-(lre * n))
    ang = lim * n
    tabs = (m_in * jnp.cos(ang), -(m_in * jnp.sin(ang)), m_out * jnp.cos(ang), m_out * jnp.sin(ang))
    m_al = jnp.exp(lre * float(chunk))
    alr = m_al * jnp.cos(lim * float(chunk))
    ali = m_al * jnp.sin(lim * float(chunk))

    gp = a_re.shape[1]
    sw = gp // n_half
    gh = sw // n_state
    hw = gh * cg
    r = lax.broadcasted_iota(jnp.int32, (hw, sw), 0)
    c = lax.broadcasted_iota(jnp.int32, (hw, sw), 1)
    on_b = (r // cg) == (c // n_state)
    r2 = lax.broadcasted_iota(jnp.int32, (sw, hw), 0)
    c2 = lax.broadcasted_iota(jnp.int32, (sw, hw), 1)
    on_c = (r2 // n_state) == (c2 // cg)
    for hf in range(n_half):
        ls = slice(hf * sw, (hf + 1) * sw)
        for ref, tab in zip((pin_re, pin_im, pout_re, pout_im), tabs):
            ref[hf] = tab[:, ls].astype(ref.dtype)
        al_re[hf] = alr[:, ls]
        al_im[hf] = ali[:, ls]
        tile_b = lambda x: jnp.concatenate([x[:, ls]] * gh, axis=0)
        wb_ref[hf, :, :sw] = jnp.where(on_b, tile_b(bb_re), 0.0).astype(wb_ref.dtype)
        wb_ref[hf, :, sw:] = jnp.where(on_b, tile_b(bb_im), 0.0).astype(wb_ref.dtype)
        cs = slice(hf * hw, (hf + 1) * hw)
        tile_c = lambda x: jnp.concatenate([x[:, cs]] * gh, axis=0)
        wc_ref[hf, :sw, :] = jnp.where(on_c, tile_c(ctr_ref[...]), 0.0).astype(wc_ref.dtype)
        wc_ref[hf, sw:, :] = jnp.where(on_c, -tile_c(cti_ref[...]), 0.0).astype(wc_ref.dtype)


def _s5_prep(a_re, a_im, log_dt, b_re, b_im, c_re, c_im, *, chunk, n_half):
    g, p = a_re.shape
    cg = b_re.shape[-1]
    gp = g * p
    sw = gp // n_half
    hw = g * cg // n_half
    row = lambda a: a.reshape(1, gp)
    tab = jax.ShapeDtypeStruct((n_half, chunk, sw), BF16)
    one = jax.ShapeDtypeStruct((n_half, 1, sw), F32)
    body = functools.partial(_s5_prep_body, chunk=chunk, n_half=n_half, n_state=p, cg=cg)
    return pl.pallas_call(
        body,
        out_shape=[jax.ShapeDtypeStruct((n_half, hw, 2 * sw), BF16),
                   jax.ShapeDtypeStruct((n_half, 2 * sw, hw), BF16),
                   tab, tab, tab, tab, one, one],
        compiler_params=pltpu.CompilerParams(vmem_limit_bytes=VMEM_LIMIT),
        name="s5_prep",
    )(row(a_re), row(a_im), row(jnp.broadcast_to(log_dt[:, None], (g, p))),
      b_re.reshape(gp, cg).T, b_im.reshape(gp, cg).T,
      c_re.transpose(2, 0, 1).reshape(p, g * cg), c_im.transpose(2, 0, 1).reshape(p, g * cg))


def _s5_body(u_ref, wb_ref, wc_ref, pin_re, pin_im, pout_re, pout_im, al_re, al_im,
             d_ref, wglu_ref, bglu_ref, y_ref, carry_ref, xs_ref, *, ts, chunk, n_half):
    t = pl.program_id(1)

    @pl.when(t == 0)
    def _():
        carry_ref[...] = jnp.zeros(carry_ref.shape, F32)

    u = u_ref[...]
    ub = u.astype(BF16)
    hw = u.shape[1] // n_half
    sw = pin_re.shape[2]
    ri = lax.broadcasted_iota(jnp.int32, (chunk, chunk), 0)
    ci = lax.broadcasted_iota(jnp.int32, (chunk, chunk), 1)
    tri = jnp.where(ci <= ri, 1.0, 0.0).astype(BF16)
    ys = [None] * n_half

    def half(hf):
        bu = jnp.dot(ub[:, hf * hw:(hf + 1) * hw], wb_ref[hf], preferred_element_type=F32)
        yield
        pr, pi = pin_re[hf], pin_im[hf]
        qr, qi = pout_re[hf], pout_im[hf]
        ar, ai = al_re[hf], al_im[hf]
        carry = [carry_ref[2 * hf:2 * hf + 1, :], carry_ref[2 * hf + 1:2 * hf + 2, :]]

        def prefix(cidx):
            rows = slice(cidx * chunk, (cidx + 1) * chunk)
            bre = bu[rows, :sw].astype(BF16)
            bim = bu[rows, sw:].astype(BF16)
            sre = pr * bre - pi * bim
            sim = pr * bim + pi * bre
            return jnp.dot(tri, jnp.concatenate([sre, sim], axis=1), preferred_element_type=F32)

        def finish(cidx, cs):
            rows = slice(cidx * chunk, (cidx + 1) * chunk)
            tre = cs[:, :sw] + carry[0]
            tim = cs[:, sw:] + carry[1]
            trb = tre.astype(BF16)
            tib = tim.astype(BF16)
            xs_ref[hf, rows, :sw] = qr * trb - qi * tib
            xs_ref[hf, rows, sw:] = qr * tib + qi * trb
            lre = tre[chunk - 1:chunk, :]
            lim = tim[chunk - 1:chunk, :]
            carry[0] = ar * lre - ai * lim
            carry[1] = ar * lim + ai * lre

        prev = None
        for cidx in range(ts // chunk):
            cs = prefix(cidx)
            if prev is not None:
                finish(*prev)
            prev = (cidx, cs)
            yield
        finish(*prev)
        carry_ref[2 * hf:2 * hf + 1, :] = carry[0]
        carry_ref[2 * hf + 1:2 * hf + 2, :] = carry[1]
        yield
        ys[hf] = jnp.dot(xs_ref[hf], wc_ref[hf], preferred_element_type=F32)

    _interleave([half(hf) for hf in range(n_half)])
    y = jnp.concatenate(ys, axis=1) + d_ref[...] * u
    y = jax.nn.gelu(y)
    gate = jax.nn.sigmoid(jnp.dot(y.astype(BF16), wglu_ref[...], preferred_element_type=F32)
                          + bglu_ref[...])
    y_ref[...] = y * gate


def _s5(u, wb, wc, tabs, d, wglu, bglu, bsz, seq, *, ts, chunk):
    nt, width = u.shape
    n_half = wb.shape[0]
    sw2 = wb.shape[2]
    sw = sw2 // 2
    n_t = seq // ts
    row = lambda b, t: (b * n_t + t, 0)
    c2 = lambda b, t: (0, 0)
    c3 = lambda b, t: (0, 0, 0)
    tab = pl.BlockSpec((n_half, chunk, sw), c3)
    one = pl.BlockSpec((n_half, 1, sw), c3)
    body = functools.partial(_s5_body, ts=ts, chunk=chunk, n_half=n_half)
    return pl.pallas_call(
        body,
        grid=(bsz, n_t),
        in_specs=[
            pl.BlockSpec((ts, width), row),
            pl.BlockSpec(wb.shape, c3),
            pl.BlockSpec(wc.shape, c3),
            tab, tab, tab, tab, one, one,
            pl.BlockSpec((1, width), c2),
            pl.BlockSpec(wglu.shape, c2),
            pl.BlockSpec((1, width), c2),
        ],
        out_specs=pl.BlockSpec((ts, width), row),
        out_shape=jax.ShapeDtypeStruct((nt, width), F32),
        scratch_shapes=[pltpu.VMEM((2 * n_half, sw), F32), pltpu.VMEM((n_half, ts, sw2), BF16)],
        compiler_params=_params("arbitrary", "arbitrary"),
        name="s5",
    )(u, wb, wc, *tabs, d, wglu, bglu)


def _mem_kv_body(m_ref, g_ref, wk_ref, wv_ref, kt_ref, v_ref):
    mn = _rms(m_ref[0], g_ref[...]).astype(BF16)
    k = jnp.dot(mn, wk_ref[...], preferred_element_type=F32)
    kt_ref[0] = k.T.astype(BF16)
    v_ref[0] = jnp.dot(mn, wv_ref[...], preferred_element_type=F32).astype(BF16)


def _mem_kv(mem, g, wk, wv):
    bsz, n_mem, d = mem.shape
    c2 = lambda b: (0, 0)
    return pl.pallas_call(
        _mem_kv_body,
        grid=(bsz,),
        in_specs=[
            pl.BlockSpec((1, n_mem, d), lambda b: (b, 0, 0)),
            pl.BlockSpec((1, d), c2),
            pl.BlockSpec((d, d), c2),
            pl.BlockSpec((d, d), c2),
        ],
        out_specs=[
            pl.BlockSpec((1, d, n_mem), lambda b: (b, 0, 0)),
            pl.BlockSpec((1, n_mem, d), lambda b: (b, 0, 0)),
        ],
        out_shape=[
            jax.ShapeDtypeStruct((bsz, d, n_mem), BF16),
            jax.ShapeDtypeStruct((bsz, n_mem, d), BF16),
        ],
        compiler_params=_params("arbitrary"),
        name="mem_kv",
    )(mem, g, wk, wv)


def _mix_attn_body(x_ref, o_ref, y_ref, wo_ref, gx_ref, wq_ref, kt_ref, v_ref, wxo_ref,
                   h_ref, *, xh, n_sub):
    dn_w = o_ref.shape[1]
    d = x_ref.shape[1]
    xd = d // xh
    scale = xd ** -0.5
    sub = x_ref.shape[0] // n_sub
    mm = functools.partial(jnp.dot, preferred_element_type=F32)

    def rows_gen(r0):
        rows = slice(r0, r0 + sub)
        mix = (mm(o_ref[rows, :].astype(BF16), wo_ref[0:dn_w, :])
               + mm(y_ref[rows, :].astype(BF16), wo_ref[dn_w:, :]))
        yield
        h1 = x_ref[rows, :] + mix
        q = mm(_rms(h1, gx_ref[...]).astype(BF16), wq_ref[...])
        yield
        q = q.astype(BF16)
        hls = [slice(h * xd, (h + 1) * xd) for h in range(xh)]
        s = [mm(q[:, hl], kt_ref[0, hl, :]) for hl in hls]
        yield
        parts = []
        for h in range(xh):
            sh = s[h] * scale
            e = jnp.exp(sh - jnp.max(sh, axis=-1, keepdims=True))
            p = (e / jnp.sum(e, axis=-1, keepdims=True)).astype(BF16)
            parts.append(mm(p, v_ref[0, :, hls[h]]))
            if h % 2 == 1:
                yield
        att = jnp.concatenate([p.astype(BF16) for p in parts], axis=1)
        proj = mm(att, wxo_ref[...])
        yield
        h_ref[rows, :] = h1 + proj

    _interleave([rows_gen(i * sub) for i in range(n_sub)])


def _mix_attn(x2, o, y, w_out, gx, wq, kt, v, wxo, bsz, seq, *, tm):
    nt, d = x2.shape
    dn_w = o.shape[1]
    n_mem = v.shape[1]
    n_t = seq // tm
    row = lambda b, t: (b * n_t + t, 0)
    c2 = lambda b, t: (0, 0)
    body = functools.partial(_mix_attn_body, xh=X_HEADS, n_sub=2)
    return pl.pallas_call(
        body,
        grid=(bsz, n_t),
        in_specs=[
            pl.BlockSpec((tm, d), row),
            pl.BlockSpec((tm, dn_w), row),
            pl.BlockSpec((tm, y.shape[1]), row),
            pl.BlockSpec(w_out.shape, c2),
            pl.BlockSpec((1, d), c2),
            pl.BlockSpec((d, d), c2),
            pl.BlockSpec((1, d, n_mem), lambda b, t: (b, 0, 0)),
            pl.BlockSpec((1, n_mem, d), lambda b, t: (b, 0, 0)),
            pl.BlockSpec((d, d), c2),
        ],
        out_specs=pl.BlockSpec((tm, d), row),
        out_shape=jax.ShapeDtypeStruct((nt, d), F32),
        compiler_params=_params("arbitrary", "arbitrary"),
        name="mix_attn",
    )(x2, o, y, w_out, gx, wq, kt, v, wxo)


def _ffn_body(h_ref, gf_ref, wg_ref, wu_ref, wd_ref, gl_ref, out_ref, *, final, fc):
    h = h_ref[...]
    hn = _rms(h, gf_ref[...]).astype(BF16)
    dff = wg_ref.shape[1]

    def down(gate, up, c0):
        act = (gate * jax.nn.sigmoid(gate) * up).astype(BF16)
        return jnp.dot(act, wd_ref[c0:c0 + fc, :], preferred_element_type=F32)

    h3 = h
    prev = None
    for c0 in range(0, dff, fc):
        gate = jnp.dot(hn, wg_ref[:, c0:c0 + fc], preferred_element_type=F32)
        up = jnp.dot(hn, wu_ref[:, c0:c0 + fc], preferred_element_type=F32)
        if prev is not None:
            h3 = h3 + down(*prev)
        prev = (gate, up, c0)
    h3 = h3 + down(*prev)
    out_ref[...] = _rms(h3, gl_ref[...]) if final else h3


def _ffn(h, gf, wg, wu, wd, gl, *, tm, final):
    nt, d = h.shape
    dff = wg.shape[1]
    c2 = lambda i: (0, 0)
    row = lambda i: (i, 0)
    return pl.pallas_call(
        functools.partial(_ffn_body, final=final, fc=2 * LANES),
        grid=(nt // tm,),
        in_specs=[
            pl.BlockSpec((tm, d), row),
            pl.BlockSpec((1, d), c2),
            pl.BlockSpec((d, dff), c2, pipeline_mode=pl.Buffered(1)),
            pl.BlockSpec((d, dff), c2, pipeline_mode=pl.Buffered(1)),
            pl.BlockSpec((dff, d), c2, pipeline_mode=pl.Buffered(1)),
            pl.BlockSpec((1, d), c2),
        ],
        out_specs=pl.BlockSpec((tm, d), row),
        out_shape=jax.ShapeDtypeStruct((nt, d), F32),
        compiler_params=_params("arbitrary"),
        name="ffn",
    )(h, gf, wg, wu, wd, gl)


def _layer(h2, mem, bsz, seq, norm_mix_g, w_in, conv_w, dn_a_log, dn_dt_bias, dn_norm_g,
           s5_a_re, s5_a_im, s5_b_re, s5_b_im, s5_c_re, s5_c_im, s5_d, s5_log_dt,
           s5_w_glu, s5_b_glu, w_out, norm_x_g, norm_mem_g, w_xq, w_xk, w_xv, w_xo):
    d = h2.shape[1]
    nh = dn_a_log.shape[0]
    dh = dn_norm_g.shape[0]
    dn_w = nh * dh
    s5_w = s5_a_re.shape[0] * s5_b_re.shape[-1]

    off_a = 4 * dn_w
    off_u = off_a + 2 * nh
    w_ab = jnp.pad(w_in[:, off_a:off_u], ((0, 0), (0, LANES - 2 * nh)))
    w_cat = jnp.concatenate([w_in[:, :off_a], w_in[:, off_u:], w_ab], axis=1).astype(BF16)
    alog = jnp.pad(dn_a_log, (0, LANES - nh)).reshape(1, LANES)
    dtb = jnp.pad(dn_dt_bias, (0, LANES - nh)).reshape(1, LANES)

    u, o = _mixer_in(h2, seq, norm_mix_g.reshape(1, d), w_cat, conv_w, alog, dtb,
                     dn_norm_g.reshape(1, dh), nh=nh, dh=dh, tb=256)

    wb, wc, *tabs = _s5_prep(s5_a_re, s5_a_im, s5_log_dt, s5_b_re, s5_b_im, s5_c_re, s5_c_im,
                             chunk=S5_CHUNK, n_half=2)
    y = _s5(u, wb, wc, tabs, s5_d.reshape(1, s5_w), s5_w_glu.astype(BF16),
            s5_b_glu.reshape(1, s5_w), bsz, seq, ts=256, chunk=S5_CHUNK)

    kt_mem, v_mem = _mem_kv(mem, norm_mem_g.reshape(1, d), w_xk.astype(BF16), w_xv.astype(BF16))
    return _mix_attn(h2, o, y, w_out.astype(BF16), norm_x_g.reshape(1, d), w_xq.astype(BF16),
                     kt_mem, v_mem, w_xo.astype(BF16), bsz, seq, tm=512)


def kernel(x, mem, norm_mix_g, w_in, conv_w, dn_a_log, dn_dt_bias, dn_norm_g, s5_a_re, s5_a_im,
           s5_b_re, s5_b_im, s5_c_re, s5_c_im, s5_d, s5_log_dt, s5_w_glu, s5_b_glu, w_out,
           norm_x_g, norm_mem_g, w_xq, w_xk, w_xv, w_xo, norm_ffn_g, w_gate, w_up, w_down,
           norm_final_g):
    bsz, seq, d = x.shape
    depth = w_in.shape[0]
    h = x.reshape(bsz * seq, d)
    for l in range(depth):
        h = _layer(h, mem, bsz, seq, norm_mix_g[l], w_in[l], conv_w[l], dn_a_log[l],
                   dn_dt_bias[l], dn_norm_g[l], s5_a_re[l], s5_a_im[l], s5_b_re[l], s5_b_im[l],
                   s5_c_re[l], s5_c_im[l], s5_d[l], s5_log_dt[l], s5_w_glu[l], s5_b_glu[l],
                   w_out[l], norm_x_g[l], norm_mem_g[l], w_xq[l], w_xk[l], w_xv[l], w_xo[l])
        h = _ffn(h, norm_ffn_g[l].reshape(1, d), w_gate[l].astype(BF16), w_up[l].astype(BF16),
                 w_down[l].astype(BF16), norm_final_g.reshape(1, d), tm=512, final=l == depth - 1)
    return h.reshape(bsz, seq, d)
```

```python
import functools
import math

import jax
import jax.numpy as jnp
from jax import lax
from jax.experimental import pallas as pl
from jax.experimental.pallas import tpu as pltpu

F32 = jnp.float32
BF16 = jnp.bfloat16
EPS = 1e-6
HIGHEST = lax.Precision.HIGHEST

LANES = 128
SUBLANES = 8
DN_CHUNK = 64
S5_CHUNK = 64
X_HEADS = 4
VMEM_LIMIT = 56 * 1024 * 1024


def _bdot(a, b):
    return jnp.dot(a.astype(BF16), b.astype(BF16), preferred_element_type=F32)


def _fdot(a, b):
    return jnp.dot(a, b, precision=HIGHEST, preferred_element_type=F32)


def _rms(x, g):
    return x * lax.rsqrt(jnp.mean(x * x, axis=-1, keepdims=True) + EPS) * g


def _interleave(gens):
    pending = list(gens)
    while pending:
        for gen in list(pending):
            try:
                next(gen)
            except StopIteration:
                pending.remove(gen)


def _params(*sem):
    return pltpu.CompilerParams(dimension_semantics=sem, vmem_limit_bytes=VMEM_LIMIT)


def _in_proj_stages(x_ref, g_ref, w_ref, cw_ref, alog_ref, dtb_ref,
                    q_ref, k_ref, kt_ref, v_ref, z_ref, u_ref, gt_ref,
                    buf_ref, *, tm, dn_w, nh, dh, cw_n):
    xnb = _rms(x_ref[...], g_ref[...]).astype(BF16)
    gw = 2 * dh

    def conv_silu(p, cols):
        outs = []
        for i in range(gw // LANES):
            c0 = cols.start + i * LANES
            s = c0 // LANES
            pc = p[:, i * LANES:(i + 1) * LANES]
            buf_ref[s, SUBLANES:SUBLANES + tm, :] = pc
            acc = cw_ref[cw_n - 1:cw_n, c0:c0 + LANES] * pc
            for j in range(cw_n - 1):
                off = SUBLANES - (cw_n - 1) + j
                acc = acc + cw_ref[j:j + 1, c0:c0 + LANES] * buf_ref[s, off:off + tm, :]
            buf_ref[s, 0:SUBLANES, :] = buf_ref[s, tm:tm + SUBLANES, :]
            outs.append(acc * jax.nn.sigmoid(acc))
        return jnp.concatenate(outs, axis=1)

    def l2n(a, scale):
        parts = []
        for i in range(gw // dh):
            ah = a[:, i * dh:(i + 1) * dh]
            parts.append(ah * (lax.rsqrt(jnp.sum(ah * ah, -1, keepdims=True) + EPS) * scale))
        return jnp.concatenate(parts, axis=1)

    def ep_q(p, c0):
        q_ref[:, c0:c0 + gw] = l2n(conv_silu(p, slice(c0, c0 + gw)), dh ** -0.5)

    def ep_k(p, c0):
        kn = l2n(conv_silu(p, slice(dn_w + c0, dn_w + c0 + gw)), 1.0)
        k_ref[:, c0:c0 + gw] = kn
        kt_ref[c0:c0 + gw, :] = kn.T

    def ep_v(p, c0):
        v_ref[:, c0:c0 + gw] = conv_silu(p, slice(2 * dn_w + c0, 2 * dn_w + c0 + gw))

    def ep_z(p, c0):
        z_ref[:, c0:c0 + gw] = p

    def ep_u(p, c0):
        u_ref[:, c0:c0 + gw] = p

    def ep_gates(ab, c0):
        lane = lax.broadcasted_iota(jnp.int32, ab.shape, 1)
        g = -jnp.exp(alog_ref[...]) * jnp.logaddexp(ab + dtb_ref[...], 0.0)
        gates = jnp.where(lane < nh, g, jax.nn.sigmoid(ab))
        gt_ref[...] = gates.T[0:SUBLANES, :]

    groups = []
    for i, ep in enumerate((ep_q, ep_k, ep_v, ep_z, ep_u)):
        groups += [(ep, i * dn_w + c0, gw, c0) for c0 in range(0, dn_w, gw)]
    groups.append((ep_gates, 5 * dn_w, LANES, 0))

    prev = None
    for ep, w0, width, c0 in groups:
        p = jnp.dot(xnb, w_ref[:, w0:w0 + width], preferred_element_type=F32)
        if prev is not None:
            prev[0](prev[1], prev[2])
        prev = (ep, p, c0)
        yield
    prev[0](prev[1], prev[2])


def _delta_stages(q_ref, k_ref, kt_ref, v_ref, z_ref, gt_ref, ng_ref,
                  o_ref, s_ref, u_sc, w_sc, qg_sc, attn_sc, kdt_sc, gtot_sc, wr, rd,
                  *, tb, nh, dh, ch):
    nc = tb // ch
    heads = range(nh)
    hls = [slice(h * dh, (h + 1) * dh) for h in heads]
    mm = functools.partial(jnp.dot, preferred_element_type=F32)

    def phase_b():
        gtot_b = gtot_sc[rd]
        s_all = [s_ref[h] for h in heads]
        for cidx in range(nc):
            rows = slice(cidx * ch, (cidx + 1) * ch)
            sb = [s_all[h].astype(BF16) for h in heads]
            ws = [mm(w_sc[rd, h, rows, :], sb[h]) for h in heads]
            qs = [mm(qg_sc[rd, h, rows, :], sb[h]) for h in heads]
            yield
            vb = [(u_sc[rd, h, rows, :] - ws[h]).astype(BF16) for h in heads]
            av = [mm(attn_sc[rd, h, rows, rows], vb[h]) for h in heads]
            kv = [mm(kdt_sc[rd, h, :, rows], vb[h]) for h in heads]
            yield
            outs = []
            for h in heads:
                a_last = jnp.exp(gtot_b[h:h + 1, cidx * ch:cidx * ch + 1])
                s_all[h] = s_all[h] * a_last + kv[h]
                o = qs[h] + av[h]
                o = o * lax.rsqrt(jnp.mean(o * o, -1, keepdims=True) + EPS)
                zh = z_ref[rows, hls[h]]
                outs.append(o * ng_ref[...] * (zh * jax.nn.sigmoid(zh)))
            o_ref[rows, :] = jnp.concatenate(outs, axis=1)
        for h in heads:
            s_ref[h] = s_all[h]

    r = lax.broadcasted_iota(jnp.int32, (tb, tb), 0)
    c = lax.broadcasted_iota(jnp.int32, (tb, tb), 1)
    same = (r // ch) == (c // ch)
    causal_bd = same & (c <= r)
    strict_bd = same & (c < r)
    gt = gt_ref[...]
    gcum_t = _fdot(gt, jnp.where(same & (r <= c), 1.0, 0.0))
    gtot_t = _fdot(gt, jnp.where(same, 1.0, 0.0))
    gcum = gcum_t.T
    gates = gt.T

    def pack(x):
        acc = x[0:ch, :]
        for i in range(1, nc):
            acc = acc + x[i * ch:(i + 1) * ch, :]
        return acc

    def unpack(x):
        return jnp.where(same, jnp.concatenate([x] * nc, axis=0), 0.0).astype(BF16)

    n_sq = max(1, (ch - 1).bit_length() - 1)

    def phase_a():
        gcol = [gcum[:, h:h + 1] for h in heads]
        grow = [gcum_t[h:h + 1, :] for h in heads]
        beta = [gates[:, nh + h:nh + h + 1] for h in heads]
        kb = [k_ref[:, hls[h]] * beta[h] for h in heads]
        gram = [mm(jnp.concatenate([kb[h], q_ref[:, hls[h]]], axis=0).astype(BF16),
                   kt_ref[hls[h], :].astype(BF16)) for h in heads]
        yield
        ppk, apk = [], []
        for h in heads:
            decay = jnp.exp(jnp.where(causal_bd, gcol[h] - grow[h], -jnp.inf))
            p_bd = -jnp.where(strict_bd, gram[h][:tb] * decay, 0.0)
            attn_sc[wr, h] = (gram[h][tb:] * decay).astype(BF16)
            pk = pack(p_bd)
            apk.append(pk)
            ppk.append(mm(pk.astype(BF16), p_bd.astype(BF16)))
        yield
        for j in range(1, n_sq + 1):
            res = []
            for h in heads:
                lhs = jnp.concatenate([apk[h], ppk[h]], axis=0) if j < n_sq else apk[h]
                res.append(mm(lhs.astype(BF16), unpack(ppk[h])))
            yield
            for h in heads:
                apk[h] = apk[h] + ppk[h] + res[h][:ch]
                if j < n_sq:
                    ppk[h] = res[h][ch:]
        egc = [jnp.exp(gcol[h]) for h in heads]
        rhs = [jnp.concatenate([v_ref[:, hls[h]] * beta[h], kb[h] * egc[h]], axis=1) for h in heads]
        prod = [mm(unpack(apk[h]), rhs[h].astype(BF16)) for h in heads]
        yield
        for h in heads:
            uw = rhs[h] + prod[h]
            u_sc[wr, h] = uw[:, :dh]
            w_sc[wr, h] = uw[:, dh:].astype(BF16)
            qg_sc[wr, h] = (q_ref[:, hls[h]] * egc[h]).astype(BF16)
            kdt_sc[wr, h] = (kt_ref[hls[h], :] * jnp.exp(gtot_t[h:h + 1, :] - grow[h])).astype(BF16)
        gtot_sc[wr] = gtot_t

    return phase_b(), phase_a()


def _mixer_in_body(x_ref, g_ref, w_ref, cw_ref, alog_ref, dtb_ref, ng_ref,
                   wb_ref, wc_ref, pin_re, pin_im, pout_re, pout_im, al_re, al_im,
                   d_ref, wglu_ref, bglu_ref,
                   y_ref, o_ref,
                   buf_ref, q_sc, k_sc, kt_sc, v_sc, z_sc, gt_sc, us_sc,
                   s_ref, u_sc, w_sc, qg_sc, attn_sc, kdt_sc, gtot_sc, carry_ref, xs_ref,
                   *, tb, n_t, n_blk, nh, dh, cw_n):
    s = pl.program_id(0)
    dn_w = nh * dh

    @pl.when(s == 0)
    def _():
        for ref in (buf_ref, q_sc, k_sc, kt_sc, v_sc, z_sc, gt_sc, us_sc,
                    s_ref, u_sc, w_sc, qg_sc, attn_sc, kdt_sc, gtot_sc, carry_ref, xs_ref):
            ref[...] = jnp.zeros(ref.shape, ref.dtype)

    @pl.when(s % n_t == 0)
    def _():
        buf_ref[:, 0:SUBLANES, :] = jnp.zeros((3 * dn_w // LANES, SUBLANES, LANES), F32)

    @pl.when((s - 1) % n_t == 0)
    def _():
        carry_ref[...] = jnp.zeros(carry_ref.shape, F32)

    @pl.when((s - 2) % n_t == 0)
    def _():
        s_ref[...] = jnp.zeros(s_ref.shape, F32)

    cur = s % 2
    prv = 1 - cur
    zw = s % 3
    zr = (s + 1) % 3
    proj = _in_proj_stages(
        x_ref, g_ref, w_ref, cw_ref, alog_ref, dtb_ref,
        q_sc.at[cur], k_sc.at[cur], kt_sc.at[cur], v_sc.at[cur], z_sc.at[zw], us_sc.at[cur],
        gt_sc.at[cur], buf_ref, tm=tb, dn_w=dn_w, nh=nh, dh=dh, cw_n=cw_n)
    scan = _s5_stages(
        us_sc.at[prv], wb_ref, wc_ref, pin_re, pin_im, pout_re, pout_im, al_re, al_im,
        d_ref, wglu_ref, bglu_ref, y_ref, carry_ref, xs_ref, s <= n_blk,
        ts=tb, chunk=S5_CHUNK, n_half=wb_ref.shape[0])
    phase_b, phase_a = _delta_stages(
        q_sc.at[prv], k_sc.at[prv], kt_sc.at[prv], v_sc.at[prv], z_sc.at[zr], gt_sc.at[prv], ng_ref,
        o_ref, s_ref, u_sc, w_sc, qg_sc, attn_sc, kdt_sc, gtot_sc, cur, prv,
        tb=tb, nh=nh, dh=dh, ch=DN_CHUNK)
    _interleave([phase_b, phase_a, scan, proj])


def _mixer_in(x2, seq, g, w_cat, conv_w, alog, dtb, ng, wb, wc, tabs, d_skip, wglu, bglu,
              *, nh, dh, tb):
    nt, d = x2.shape
    dn_w = nh * dh
    s5_w = d_skip.shape[1]
    n_blk = nt // tb
    cw_n = conv_w.shape[0]
    n_half, _, sw2 = wb.shape
    const = lambda s: (0, 0)
    const3 = lambda s: (0, 0, 0)
    whole = lambda a: pl.BlockSpec(a.shape, const3 if a.ndim == 3 else const)
    in_blk = lambda s: (jnp.minimum(s, n_blk - 1), 0)
    body = functools.partial(_mixer_in_body, tb=tb, n_t=seq // tb, n_blk=n_blk, nh=nh, dh=dh,
                             cw_n=cw_n)
    two = lambda *shape: pltpu.VMEM((2,) + shape, F32)
    return pl.pallas_call(
        body,
        grid=(n_blk + 2,),
        in_specs=[pl.BlockSpec((tb, d), in_blk)]
        + [whole(a) for a in (g, w_cat, conv_w, alog, dtb, ng, wb, wc, *tabs, d_skip, wglu, bglu)],
        out_specs=[pl.BlockSpec((tb, s5_w), lambda s: (jnp.clip(s - 1, 0, n_blk - 1), 0)),
                   pl.BlockSpec((tb, dn_w), lambda s: (jnp.maximum(s - 2, 0), 0))],
        out_shape=[jax.ShapeDtypeStruct((nt, s5_w), F32), jax.ShapeDtypeStruct((nt, dn_w), F32)],
        scratch_shapes=[
            pltpu.VMEM((3 * dn_w // LANES, tb + SUBLANES, LANES), F32),
            two(tb, dn_w), two(tb, dn_w), two(dn_w, tb), two(tb, dn_w),
            pltpu.VMEM((3, tb, dn_w), F32),
            two(SUBLANES, tb),
            two(tb, s5_w),
            pltpu.VMEM((nh, dh, dh), F32),
            pltpu.VMEM((2, nh, tb, dh), F32),
            pltpu.VMEM((2, nh, tb, dh), BF16),
            pltpu.VMEM((2, nh, tb, dh), BF16),
            pltpu.VMEM((2, nh, tb, tb), BF16),
            pltpu.VMEM((2, nh, dh, tb), BF16),
            pltpu.VMEM((2, SUBLANES, tb), F32),
            pltpu.VMEM((2 * n_half, sw2 // 2), F32),
            pltpu.VMEM((n_half, tb, sw2), BF16),
        ],
        compiler_params=_params("arbitrary"),
        name="mixer_in",
    )(x2, g, w_cat, conv_w, alog, dtb, ng, wb, wc, *tabs, d_skip, wglu, bglu)


def _s5_prep_body(are_ref, aim_ref, ldt_ref, btr_ref, bti_ref, ctr_ref, cti_ref,
                  wb_ref, wc_ref, pin_re, pin_im, pout_re, pout_im, al_re, al_im,
                  *, chunk, n_half, n_state, cg):
    a_re = are_ref[...]
    a_im = aim_ref[...]
    dt = jnp.exp(ldt_ref[...])
    lre = a_re * dt
    lim = a_im * dt
    mag = jnp.exp(lre)
    nr = mag * jnp.cos(lim) - 1.0
    ni = mag * jnp.sin(lim)
    den = a_re * a_re + a_im * a_im
    cor = (nr * a_re + ni * a_im) / den
    coi = (ni * a_re - nr * a_im) / den
    bb_re = cor * btr_ref[...] - coi * bti_ref[...]
    bb_im = cor * bti_ref[...] + coi * btr_ref[...]

    n = lax.broadcasted_iota(jnp.int32, (chunk, 1), 0).astype(F32) - float(chunk // 2)
    m_out = jnp.exp(lre * n)
    m_in = jnp.exp(-(lre * n))
    ang = lim * n
    tabs = (m_in * jnp.cos(ang), -(m_in * jnp.sin(ang)), m_out * jnp.cos(ang), m_out * jnp.sin(ang))
    m_al = jnp.exp(lre * float(chunk))
    alr = m_al * jnp.cos(lim * float(chunk))
    ali = m_al * jnp.sin(lim * float(chunk))

    gp = a_re.shape[1]
    sw = gp // n_half
    gh = sw // n_state
    hw = gh * cg
    r = lax.broadcasted_iota(jnp.int32, (hw, sw), 0)
    c = lax.broadcasted_iota(jnp.int32, (hw, sw), 1)
    on_b = (r // cg) == (c // n_state)
    r2 = lax.broadcasted_iota(jnp.int32, (sw, hw), 0)
    c2 = lax.broadcasted_iota(jnp.int32, (sw, hw), 1)
    on_c = (r2 // n_state) == (c2 // cg)
    for hf in range(n_half):
        ls = slice(hf * sw, (hf + 1) * sw)
        for ref, tab in zip((pin_re, pin_im, pout_re, pout_im), tabs):
            ref[hf] = tab[:, ls].astype(ref.dtype)
        al_re[hf] = alr[:, ls]
        al_im[hf] = ali[:, ls]
        tile_b = lambda x: jnp.concatenate([x[:, ls]] * gh, axis=0)
        wb_ref[hf, :, :sw] = jnp.where(on_b, tile_b(bb_re), 0.0).astype(wb_ref.dtype)
        wb_ref[hf, :, sw:] = jnp.where(on_b, tile_b(bb_im), 0.0).astype(wb_ref.dtype)
        cs = slice(hf * hw, (hf + 1) * hw)
        tile_c = lambda x: jnp.concatenate([x[:, cs]] * gh, axis=0)
        wc_ref[hf, :sw, :] = jnp.where(on_c, tile_c(ctr_ref[...]), 0.0).astype(wc_ref.dtype)
        wc_ref[hf, sw:, :] = jnp.where(on_c, -tile_c(cti_ref[...]), 0.0).astype(wc_ref.dtype)


def _s5_prep(a_re, a_im, log_dt, b_re, b_im, c_re, c_im, *, chunk, n_half):
    g, p = a_re.shape
    cg = b_re.shape[-1]
    gp = g * p
    sw = gp // n_half
    hw = g * cg // n_half
    row = lambda a: a.reshape(1, gp)
    tab = jax.ShapeDtypeStruct((n_half, chunk, sw), BF16)
    one = jax.ShapeDtypeStruct((n_half, 1, sw), F32)
    body = functools.partial(_s5_prep_body, chunk=chunk, n_half=n_half, n_state=p, cg=cg)
    return pl.pallas_call(
        body,
        out_shape=[jax.ShapeDtypeStruct((n_half, hw, 2 * sw), BF16),
                   jax.ShapeDtypeStruct((n_half, 2 * sw, hw), BF16),
                   tab, tab, tab, tab, one, one],
        compiler_params=pltpu.CompilerParams(vmem_limit_bytes=VMEM_LIMIT),
        name="s5_prep",
    )(row(a_re), row(a_im), row(jnp.broadcast_to(log_dt[:, None], (g, p))),
      b_re.reshape(gp, cg).T, b_im.reshape(gp, cg).T,
      c_re.transpose(2, 0, 1).reshape(p, g * cg), c_im.transpose(2, 0, 1).reshape(p, g * cg))


def _s5_stages(u_ref, wb_ref, wc_ref, pin_re, pin_im, pout_re, pout_im, al_re, al_im,
               d_ref, wglu_ref, bglu_ref, y_ref, carry_ref, xs_ref, store_ok,
               *, ts, chunk, n_half):
    u = u_ref[...]
    ub = u.astype(BF16)
    hw = u.shape[1] // n_half
    sw = pin_re.shape[2]
    ri = lax.broadcasted_iota(jnp.int32, (chunk, chunk), 0)
    ci = lax.broadcasted_iota(jnp.int32, (chunk, chunk), 1)
    tri = jnp.where(ci <= ri, 1.0, 0.0).astype(BF16)
    ys = [None] * n_half

    def half(hf):
        bu = jnp.dot(ub[:, hf * hw:(hf + 1) * hw], wb_ref[hf], preferred_element_type=F32)
        yield
        pr, pi = pin_re[hf], pin_im[hf]
        qr, qi = pout_re[hf], pout_im[hf]
        ar, ai = al_re[hf], al_im[hf]
        carry = [carry_ref[2 * hf:2 * hf + 1, :], carry_ref[2 * hf + 1:2 * hf + 2, :]]

        def prefix(cidx):
            rows = slice(cidx * chunk, (cidx + 1) * chunk)
            bre = bu[rows, :sw].astype(BF16)
            bim = bu[rows, sw:].astype(BF16)
            sre = pr * bre - pi * bim
            sim = pr * bim + pi * bre
            return jnp.dot(tri, jnp.concatenate([sre, sim], axis=1), preferred_element_type=F32)

        def finish(cidx, cs):
            rows = slice(cidx * chunk, (cidx + 1) * chunk)
            tre = cs[:, :sw] + carry[0]
            tim = cs[:, sw:] + carry[1]
            trb = tre.astype(BF16)
            tib = tim.astype(BF16)
            xs_ref[hf, rows, :sw] = qr * trb - qi * tib
            xs_ref[hf, rows, sw:] = qr * tib + qi * trb
            lre = tre[chunk - 1:chunk, :]
            lim = tim[chunk - 1:chunk, :]
            carry[0] = ar * lre - ai * lim
            carry[1] = ar * lim + ai * lre

        prev = None
        for cidx in range(ts // chunk):
            cs = prefix(cidx)
            if prev is not None:
                finish(*prev)
            prev = (cidx, cs)
            yield
        finish(*prev)
        carry_ref[2 * hf:2 * hf + 1, :] = carry[0]
        carry_ref[2 * hf + 1:2 * hf + 2, :] = carry[1]
        yield
        ys[hf] = jnp.dot(xs_ref[hf], wc_ref[hf], preferred_element_type=F32)

    pending = [half(hf) for hf in range(n_half)]
    while pending:
        for gen in list(pending):
            try:
                next(gen)
            except StopIteration:
                pending.remove(gen)
        yield
    y = jnp.concatenate(ys, axis=1) + d_ref[...] * u
    y = jax.nn.gelu(y)
    gate = jnp.dot(y.astype(BF16), wglu_ref[...], preferred_element_type=F32)
    yield
    out = y * jax.nn.sigmoid(gate + bglu_ref[...])

    @pl.when(store_ok)
    def _():
        y_ref[...] = out


def _mem_kv_body(m_ref, g_ref, wk_ref, wv_ref, kt_ref, v_ref):
    mn = _rms(m_ref[0], g_ref[...]).astype(BF16)
    k = jnp.dot(mn, wk_ref[...], preferred_element_type=F32)
    kt_ref[0] = k.T.astype(BF16)
    v_ref[0] = jnp.dot(mn, wv_ref[...], preferred_element_type=F32).astype(BF16)


def _mem_kv(mem, g, wk, wv):
    bsz, n_mem, d = mem.shape
    c2 = lambda b: (0, 0)
    return pl.pallas_call(
        _mem_kv_body,
        grid=(bsz,),
        in_specs=[
            pl.BlockSpec((1, n_mem, d), lambda b: (b, 0, 0)),
            pl.BlockSpec((1, d), c2),
            pl.BlockSpec((d, d), c2),
            pl.BlockSpec((d, d), c2),
        ],
        out_specs=[
            pl.BlockSpec((1, d, n_mem), lambda b: (b, 0, 0)),
            pl.BlockSpec((1, n_mem, d), lambda b: (b, 0, 0)),
        ],
        out_shape=[
            jax.ShapeDtypeStruct((bsz, d, n_mem), BF16),
            jax.ShapeDtypeStruct((bsz, n_mem, d), BF16),
        ],
        compiler_params=_params("arbitrary"),
        name="mem_kv",
    )(mem, g, wk, wv)


def _mix_attn_body(x_ref, o_ref, y_ref, wo_ref, gx_ref, wq_ref, kt_ref, v_ref, wxo_ref,
                   h_ref, *, xh, n_sub):
    dn_w = o_ref.shape[1]
    d = x_ref.shape[1]
    xd = d // xh
    scale = xd ** -0.5
    sub = x_ref.shape[0] // n_sub
    mm = functools.partial(jnp.dot, preferred_element_type=F32)

    def rows_gen(r0):
        rows = slice(r0, r0 + sub)
        mix = (mm(o_ref[rows, :].astype(BF16), wo_ref[0:dn_w, :])
               + mm(y_ref[rows, :].astype(BF16), wo_ref[dn_w:, :]))
        yield
        h1 = x_ref[rows, :] + mix
        q = mm(_rms(h1, gx_ref[...]).astype(BF16), wq_ref[...])
        yield
        q = q.astype(BF16)
        hls = [slice(h * xd, (h + 1) * xd) for h in range(xh)]
        s = [mm(q[:, hl], kt_ref[0, hl, :]) for hl in hls]
        yield
        parts = []
        for h in range(xh):
            sh = s[h] * scale
            e = jnp.exp(sh - jnp.max(sh, axis=-1, keepdims=True))
            p = (e / jnp.sum(e, axis=-1, keepdims=True)).astype(BF16)
            parts.append(mm(p, v_ref[0, :, hls[h]]))
            if h % 2 == 1:
                yield
        att = jnp.concatenate([p.astype(BF16) for p in parts], axis=1)
        proj = mm(att, wxo_ref[...])
        yield
        h_ref[rows, :] = h1 + proj

    _interleave([rows_gen(i * sub) for i in range(n_sub)])


def _mix_attn(x2, o, y, w_out, gx, wq, kt, v, wxo, bsz, seq, *, tm):
    nt, d = x2.shape
    dn_w = o.shape[1]
    n_mem = v.shape[1]
    n_t = seq // tm
    row = lambda b, t: (b * n_t + t, 0)
    c2 = lambda b, t: (0, 0)
    body = functools.partial(_mix_attn_body, xh=X_HEADS, n_sub=2)
    return pl.pallas_call(
        body,
        grid=(bsz, n_t),
        in_specs=[
            pl.BlockSpec((tm, d), row),
            pl.BlockSpec((tm, dn_w), row),
            pl.BlockSpec((tm, y.shape[1]), row),
            pl.BlockSpec(w_out.shape, c2),
            pl.BlockSpec((1, d), c2),
            pl.BlockSpec((d, d), c2),
            pl.BlockSpec((1, d, n_mem), lambda b, t: (b, 0, 0)),
            pl.BlockSpec((1, n_mem, d), lambda b, t: (b, 0, 0)),
            pl.BlockSpec((d, d), c2),
        ],
        out_specs=pl.BlockSpec((tm, d), row),
        out_shape=jax.ShapeDtypeStruct((nt, d), F32),
        compiler_params=_params("arbitrary", "arbitrary"),
        name="mix_attn",
    )(x2, o, y, w_out, gx, wq, kt, v, wxo)


def _ffn_body(h_ref, gf_ref, wg_ref, wu_ref, wd_ref, gl_ref, out_ref, *, final, fc):
    h = h_ref[...]
    hn = _rms(h, gf_ref[...]).astype(BF16)
    dff = wg_ref.shape[1]

    def down(gate, up, c0):
        act = (gate * jax.nn.sigmoid(gate) * up).astype(BF16)
        return jnp.dot(act, wd_ref[c0:c0 + fc, :], preferred_element_type=F32)

    h3 = h
    prev = None
    for c0 in range(0, dff, fc):
        gate = jnp.dot(hn, wg_ref[:, c0:c0 + fc], preferred_element_type=F32)
        up = jnp.dot(hn, wu_ref[:, c0:c0 + fc], preferred_element_type=F32)
        if prev is not None:
            h3 = h3 + down(*prev)
        prev = (gate, up, c0)
    h3 = h3 + down(*prev)
    out_ref[...] = _rms(h3, gl_ref[...]) if final else h3


def _ffn(h, gf, wg, wu, wd, gl, *, tm, final):
    nt, d = h.shape
    dff = wg.shape[1]
    c2 = lambda i: (0, 0)
    row = lambda i: (i, 0)
    return pl.pallas_call(
        functools.partial(_ffn_body, final=final, fc=2 * LANES),
        grid=(nt // tm,),
        in_specs=[
            pl.BlockSpec((tm, d), row),
            pl.BlockSpec((1, d), c2),
            pl.BlockSpec((d, dff), c2, pipeline_mode=pl.Buffered(1)),
            pl.BlockSpec((d, dff), c2, pipeline_mode=pl.Buffered(1)),
            pl.BlockSpec((dff, d), c2, pipeline_mode=pl.Buffered(1)),
            pl.BlockSpec((1, d), c2),
        ],
        out_specs=pl.BlockSpec((tm, d), row),
        out_shape=jax.ShapeDtypeStruct((nt, d), F32),
        compiler_params=_params("arbitrary"),
        name="ffn",
    )(h, gf, wg, wu, wd, gl)


def _layer(h2, mem, bsz, seq, norm_mix_g, w_in, conv_w, dn_a_log, dn_dt_bias, dn_norm_g,
           s5_a_re, s5_a_im, s5_b_re, s5_b_im, s5_c_re, s5_c_im, s5_d, s5_log_dt,
           s5_w_glu, s5_b_glu, w_out, norm_x_g, norm_mem_g, w_xq, w_xk, w_xv, w_xo):
    d = h2.shape[1]
    nh = dn_a_log.shape[0]
    dh = dn_norm_g.shape[0]
    dn_w = nh * dh
    s5_w = s5_a_re.shape[0] * s5_b_re.shape[-1]

    off_a = 4 * dn_w
    off_u = off_a + 2 * nh
    w_ab = jnp.pad(w_in[:, off_a:off_u], ((0, 0), (0, LANES - 2 * nh)))
    w_cat = jnp.concatenate([w_in[:, :off_a], w_in[:, off_u:], w_ab], axis=1).astype(BF16)
    alog = jnp.pad(dn_a_log, (0, LANES - nh)).reshape(1, LANES)
    dtb = jnp.pad(dn_dt_bias, (0, LANES - nh)).reshape(1, LANES)

    wb, wc, *tabs = _s5_prep(s5_a_re, s5_a_im, s5_log_dt, s5_b_re, s5_b_im, s5_c_re, s5_c_im,
                             chunk=S5_CHUNK, n_half=2)
    y, o = _mixer_in(h2, seq, norm_mix_g.reshape(1, d), w_cat, conv_w, alog, dtb,
                     dn_norm_g.reshape(1, dh), wb, wc, tabs, s5_d.reshape(1, s5_w),
                     s5_w_glu.astype(BF16), s5_b_glu.reshape(1, s5_w), nh=nh, dh=dh, tb=256)

    kt_mem, v_mem = _mem_kv(mem, norm_mem_g.reshape(1, d), w_xk.astype(BF16), w_xv.astype(BF16))
    return _mix_attn(h2, o, y, w_out.astype(BF16), norm_x_g.reshape(1, d), w_xq.astype(BF16),
                     kt_mem, v_mem, w_xo.astype(BF16), bsz, seq, tm=512)


def kernel(x, mem, norm_mix_g, w_in, conv_w, dn_a_log, dn_dt_bias, dn_norm_g, s5_a_re, s5_a_im,
           s5_b_re, s5_b_im, s5_c_re, s5_c_im, s5_d, s5_log_dt, s5_w_glu, s5_b_glu, w_out,
           norm_x_g, norm_mem_g, w_xq, w_xk, w_xv, w_xo, norm_ffn_g, w_gate, w_up, w_down,
           norm_final_g):
    bsz, seq, d = x.shape
    depth = w_in.shape[0]
    h = x.reshape(bsz * seq, d)
    for l in range(depth):
        h = _layer(h, mem, bsz, seq, norm_mix_g[l], w_in[l], conv_w[l], dn_a_log[l],
                   dn_dt_bias[l], dn_norm_g[l], s5_a_re[l], s5_a_im[l], s5_b_re[l], s5_b_im[l],
                   s5_c_re[l], s5_c_im[l], s5_d[l], s5_log_dt[l], s5_w_glu[l], s5_b_glu[l],
                   w_out[l], norm_x_g[l], norm_mem_g[l], w_xq[l], w_xk[l], w_xv[l], w_xo[l])
        h = _ffn(h, norm_ffn_g[l].reshape(1, d), w_gate[l].astype(BF16), w_up[l].astype(BF16),
                 w_down[l].astype(BF16), norm_final_g.reshape(1, d), tm=512, final=l == depth - 1)
    return h.reshape(bsz, seq, d)
```

```python
import functools

import jax
import jax.numpy as jnp
from jax import lax
from jax.experimental import pallas as pl
from jax.experimental.pallas import tpu as pltpu

F32 = jnp.float32
BF16 = jnp.bfloat16
EPS = 1e-6
HIGHEST = lax.Precision.HIGHEST

LANES = 128
SUBLANES = 8
DN_CHUNK = 64
S5_CHUNK = 64
X_HEADS = 4
VMEM_LIMIT = 56 * 1024 * 1024


def _bdot(a, b):
    return jnp.dot(a.astype(BF16), b.astype(BF16), preferred_element_type=F32)


def _fdot(a, b):
    return jnp.dot(a, b, precision=HIGHEST, preferred_element_type=F32)


def _rms(x, g):
    return x * lax.rsqrt(jnp.mean(x * x, axis=-1, keepdims=True) + EPS) * g


def _interleave(gens):
    pending = list(gens)
    while pending:
        for gen in list(pending):
            try:
                next(gen)
            except StopIteration:
                pending.remove(gen)


def _params(*sem):
    return pltpu.CompilerParams(dimension_semantics=sem, vmem_limit_bytes=VMEM_LIMIT)


def _in_proj_stages(x_ref, g_ref, w_ref, cw_ref, alog_ref, dtb_ref,
                    q_ref, k_ref, kt_ref, v_ref, z_ref, u_ref, gt_ref,
                    buf_ref, *, tm, dn_w, nh, dh, cw_n):
    xnb = _rms(x_ref[...], g_ref[...]).astype(BF16)
    gw = 2 * dh

    def conv_silu(p, cols):
        outs = []
        for i in range(gw // LANES):
            c0 = cols.start + i * LANES
            s = c0 // LANES
            pc = p[:, i * LANES:(i + 1) * LANES]
            buf_ref[s, SUBLANES:SUBLANES + tm, :] = pc
            acc = cw_ref[cw_n - 1:cw_n, c0:c0 + LANES] * pc
            for j in range(cw_n - 1):
                off = SUBLANES - (cw_n - 1) + j
                acc = acc + cw_ref[j:j + 1, c0:c0 + LANES] * buf_ref[s, off:off + tm, :]
            buf_ref[s, 0:SUBLANES, :] = buf_ref[s, tm:tm + SUBLANES, :]
            outs.append(acc * jax.nn.sigmoid(acc))
        return jnp.concatenate(outs, axis=1)

    def l2n(a, scale):
        parts = []
        for i in range(gw // dh):
            ah = a[:, i * dh:(i + 1) * dh]
            parts.append(ah * (lax.rsqrt(jnp.sum(ah * ah, -1, keepdims=True) + EPS) * scale))
        return jnp.concatenate(parts, axis=1)

    def ep_q(p, c0):
        q_ref[:, c0:c0 + gw] = l2n(conv_silu(p, slice(c0, c0 + gw)), dh ** -0.5)

    def ep_k(p, c0):
        kn = l2n(conv_silu(p, slice(dn_w + c0, dn_w + c0 + gw)), 1.0)
        k_ref[:, c0:c0 + gw] = kn
        kt_ref[c0:c0 + gw, :] = kn.T

    def ep_v(p, c0):
        v_ref[:, c0:c0 + gw] = conv_silu(p, slice(2 * dn_w + c0, 2 * dn_w + c0 + gw))

    def ep_z(p, c0):
        z_ref[:, c0:c0 + gw] = p

    def ep_u(p, c0):
        u_ref[:, c0:c0 + gw] = p

    def ep_gates(ab, c0):
        lane = lax.broadcasted_iota(jnp.int32, ab.shape, 1)
        g = -jnp.exp(alog_ref[...]) * jnp.logaddexp(ab + dtb_ref[...], 0.0)
        gates = jnp.where(lane < nh, g, jax.nn.sigmoid(ab))
        gt_ref[...] = gates.T[0:SUBLANES, :]

    groups = []
    for i, ep in enumerate((ep_q, ep_k, ep_v, ep_z, ep_u)):
        groups += [(ep, i * dn_w + c0, gw, c0) for c0 in range(0, dn_w, gw)]
    groups.append((ep_gates, 5 * dn_w, LANES, 0))

    prev = None
    for ep, w0, width, c0 in groups:
        p = jnp.dot(xnb, w_ref[:, w0:w0 + width], preferred_element_type=F32)
        if prev is not None:
            prev[0](prev[1], prev[2])
        prev = (ep, p, c0)
        yield
    prev[0](prev[1], prev[2])


def _delta_stages(q_ref, k_ref, kt_ref, v_ref, z_ref, gt_ref, ng_ref,
                  o_ref, s_ref, u_sc, w_sc, qg_sc, attn_sc, kdt_sc, gtot_sc, wr, rd,
                  *, tb, nh, dh, ch):
    nc = tb // ch
    heads = range(nh)
    hls = [slice(h * dh, (h + 1) * dh) for h in heads]
    mm = functools.partial(jnp.dot, preferred_element_type=F32)

    def phase_b():
        gtot_b = gtot_sc[rd]
        s_all = [s_ref[h] for h in heads]
        for cidx in range(nc):
            rows = slice(cidx * ch, (cidx + 1) * ch)
            sb = [s_all[h].astype(BF16) for h in heads]
            ws = [mm(w_sc[rd, h, rows, :], sb[h]) for h in heads]
            qs = [mm(qg_sc[rd, h, rows, :], sb[h]) for h in heads]
            yield
            vb = [(u_sc[rd, h, rows, :] - ws[h]).astype(BF16) for h in heads]
            av = [mm(attn_sc[rd, h, rows, rows], vb[h]) for h in heads]
            kv = [mm(kdt_sc[rd, h, :, rows], vb[h]) for h in heads]
            yield
            outs = []
            for h in heads:
                a_last = jnp.exp(gtot_b[h:h + 1, cidx * ch:cidx * ch + 1])
                s_all[h] = s_all[h] * a_last + kv[h]
                o = qs[h] + av[h]
                o = o * lax.rsqrt(jnp.mean(o * o, -1, keepdims=True) + EPS)
                zh = z_ref[rows, hls[h]]
                outs.append(o * ng_ref[...] * (zh * jax.nn.sigmoid(zh)))
            o_ref[rows, :] = jnp.concatenate(outs, axis=1)
        for h in heads:
            s_ref[h] = s_all[h]

    r = lax.broadcasted_iota(jnp.int32, (tb, tb), 0)
    c = lax.broadcasted_iota(jnp.int32, (tb, tb), 1)
    same = (r // ch) == (c // ch)
    causal_bd = same & (c <= r)
    strict_bd = same & (c < r)
    gt = gt_ref[...]
    gcum_t = _fdot(gt, jnp.where(same & (r <= c), 1.0, 0.0))
    gtot_t = _fdot(gt, jnp.where(same, 1.0, 0.0))
    gcum = gcum_t.T
    gates = gt.T

    def pack(x):
        acc = x[0:ch, :]
        for i in range(1, nc):
            acc = acc + x[i * ch:(i + 1) * ch, :]
        return acc

    def unpack(x):
        return jnp.where(same, jnp.concatenate([x] * nc, axis=0), 0.0).astype(BF16)

    n_sq = max(1, (ch - 1).bit_length() - 1)

    def phase_a():
        gcol = [gcum[:, h:h + 1] for h in heads]
        grow = [gcum_t[h:h + 1, :] for h in heads]
        beta = [gates[:, nh + h:nh + h + 1] for h in heads]
        kb = [k_ref[:, hls[h]] * beta[h] for h in heads]
        gram = [mm(jnp.concatenate([kb[h], q_ref[:, hls[h]]], axis=0).astype(BF16),
                   kt_ref[hls[h], :].astype(BF16)) for h in heads]
        yield
        ppk, apk = [], []
        for h in heads:
            decay = jnp.exp(jnp.where(causal_bd, gcol[h] - grow[h], -jnp.inf))
            p_bd = -jnp.where(strict_bd, gram[h][:tb] * decay, 0.0)
            attn_sc[wr, h] = (gram[h][tb:] * decay).astype(BF16)
            pk = pack(p_bd)
            apk.append(pk)
            ppk.append(mm(pk.astype(BF16), p_bd.astype(BF16)))
        yield
        for j in range(1, n_sq + 1):
            res = []
            for h in heads:
                lhs = jnp.concatenate([apk[h], ppk[h]], axis=0) if j < n_sq else apk[h]
                res.append(mm(lhs.astype(BF16), unpack(ppk[h])))
            yield
            for h in heads:
                apk[h] = apk[h] + ppk[h] + res[h][:ch]
                if j < n_sq:
                    ppk[h] = res[h][ch:]
        egc = [jnp.exp(gcol[h]) for h in heads]
        rhs = [jnp.concatenate([v_ref[:, hls[h]] * beta[h], kb[h] * egc[h]], axis=1) for h in heads]
        prod = [mm(unpack(apk[h]), rhs[h].astype(BF16)) for h in heads]
        yield
        for h in heads:
            uw = rhs[h] + prod[h]
            u_sc[wr, h] = uw[:, :dh]
            w_sc[wr, h] = uw[:, dh:].astype(BF16)
            qg_sc[wr, h] = (q_ref[:, hls[h]] * egc[h]).astype(BF16)
            kdt_sc[wr, h] = (kt_ref[hls[h], :] * jnp.exp(gtot_t[h:h + 1, :] - grow[h])).astype(BF16)
        gtot_sc[wr] = gtot_t

    return phase_b(), phase_a()


def _mixer_in_body(x_ref, g_ref, w_ref, cw_ref, alog_ref, dtb_ref, ng_ref,
                   u_ref, o_ref,
                   buf_ref, q_sc, k_sc, kt_sc, v_sc, z_sc, gt_sc,
                   s_ref, u_sc, w_sc, qg_sc, attn_sc, kdt_sc, gtot_sc,
                   *, tb, n_t, nh, dh, cw_n):
    s = pl.program_id(0)
    dn_w = nh * dh

    @pl.when(s == 0)
    def _():
        for ref in (buf_ref, q_sc, k_sc, kt_sc, v_sc, z_sc, gt_sc,
                    s_ref, u_sc, w_sc, qg_sc, attn_sc, kdt_sc, gtot_sc):
            ref[...] = jnp.zeros(ref.shape, ref.dtype)

    @pl.when(s % n_t == 0)
    def _():
        buf_ref[:, 0:SUBLANES, :] = jnp.zeros((3 * dn_w // LANES, SUBLANES, LANES), F32)

    @pl.when((s - 2) % n_t == 0)
    def _():
        s_ref[...] = jnp.zeros(s_ref.shape, F32)

    cur = s % 2
    prv = 1 - cur
    zw = s % 3
    zr = (s + 1) % 3
    proj = _in_proj_stages(
        x_ref, g_ref, w_ref, cw_ref, alog_ref, dtb_ref,
        q_sc.at[cur], k_sc.at[cur], kt_sc.at[cur], v_sc.at[cur], z_sc.at[zw], u_ref, gt_sc.at[cur],
        buf_ref, tm=tb, dn_w=dn_w, nh=nh, dh=dh, cw_n=cw_n)
    phase_b, phase_a = _delta_stages(
        q_sc.at[prv], k_sc.at[prv], kt_sc.at[prv], v_sc.at[prv], z_sc.at[zr], gt_sc.at[prv], ng_ref,
        o_ref, s_ref, u_sc, w_sc, qg_sc, attn_sc, kdt_sc, gtot_sc, cur, prv,
        tb=tb, nh=nh, dh=dh, ch=DN_CHUNK)
    _interleave([phase_b, phase_a, proj])


def _mixer_in(x2, seq, g, w_cat, conv_w, alog, dtb, ng, *, nh, dh, tb):
    nt, d = x2.shape
    dn_w = nh * dh
    n_blk = nt // tb
    cw_n = conv_w.shape[0]
    const = lambda s: (0, 0)
    in_blk = lambda s: (jnp.minimum(s, n_blk - 1), 0)
    out_blk = lambda s: (jnp.maximum(s - 2, 0), 0)
    body = functools.partial(_mixer_in_body, tb=tb, n_t=seq // tb, nh=nh, dh=dh, cw_n=cw_n)
    two = lambda *shape: pltpu.VMEM((2,) + shape, F32)
    return pl.pallas_call(
        body,
        grid=(n_blk + 2,),
        in_specs=[pl.BlockSpec((tb, d), in_blk)]
        + [pl.BlockSpec(a.shape, const) for a in (g, w_cat, conv_w, alog, dtb, ng)],
        out_specs=[pl.BlockSpec((tb, dn_w), in_blk), pl.BlockSpec((tb, dn_w), out_blk)],
        out_shape=[jax.ShapeDtypeStruct((nt, dn_w), F32), jax.ShapeDtypeStruct((nt, dn_w), F32)],
        scratch_shapes=[
            pltpu.VMEM((3 * dn_w // LANES, tb + SUBLANES, LANES), F32),
            two(tb, dn_w), two(tb, dn_w), two(dn_w, tb), two(tb, dn_w),
            pltpu.VMEM((3, tb, dn_w), F32),
            two(SUBLANES, tb),
            pltpu.VMEM((nh, dh, dh), F32),
            pltpu.VMEM((2, nh, tb, dh), F32),
            pltpu.VMEM((2, nh, tb, dh), BF16),
            pltpu.VMEM((2, nh, tb, dh), BF16),
            pltpu.VMEM((2, nh, tb, tb), BF16),
            pltpu.VMEM((2, nh, dh, tb), BF16),
            pltpu.VMEM((2, SUBLANES, tb), F32),
        ],
        compiler_params=_params("arbitrary"),
        name="mixer_in",
    )(x2, g, w_cat, conv_w, alog, dtb, ng)


def _s5_prep_body(are_ref, aim_ref, ldt_ref, btr_ref, bti_ref, ctr_ref, cti_ref,
                  wb_ref, wc_ref, pin_re, pin_im, pout_re, pout_im, al_re, al_im,
                  *, chunk, n_half, n_state, cg):
    a_re = are_ref[...]
    a_im = aim_ref[...]
    dt = jnp.exp(ldt_ref[...])
    lre = a_re * dt
    lim = a_im * dt
    mag = jnp.exp(lre)
    nr = mag * jnp.cos(lim) - 1.0
    ni = mag * jnp.sin(lim)
    den = a_re * a_re + a_im * a_im
    cor = (nr * a_re + ni * a_im) / den
    coi = (ni * a_re - nr * a_im) / den
    bb_re = cor * btr_ref[...] - coi * bti_ref[...]
    bb_im = cor * bti_ref[...] + coi * btr_ref[...]

    n = lax.broadcasted_iota(jnp.int32, (chunk, 1), 0).astype(F32) - float(chunk // 2)
    m_out = jnp.exp(lre * n)
    m_in = jnp.exp(-(lre * n))
    ang = lim * n
    tabs = (m_in * jnp.cos(ang), -(m_in * jnp.sin(ang)), m_out * jnp.cos(ang), m_out * jnp.sin(ang))
    m_al = jnp.exp(lre * float(chunk))
    alr = m_al * jnp.cos(lim * float(chunk))
    ali = m_al * jnp.sin(lim * float(chunk))

    gp = a_re.shape[1]
    sw = gp // n_half
    gh = sw // n_state
    hw = gh * cg
    r = lax.broadcasted_iota(jnp.int32, (hw, sw), 0)
    c = lax.broadcasted_iota(jnp.int32, (hw, sw), 1)
    on_b = (r // cg) == (c // n_state)
    r2 = lax.broadcasted_iota(jnp.int32, (sw, hw), 0)
    c2 = lax.broadcasted_iota(jnp.int32, (sw, hw), 1)
    on_c = (r2 // n_state) == (c2 // cg)
    for hf in range(n_half):
        ls = slice(hf * sw, (hf + 1) * sw)
        for ref, tab in zip((pin_re, pin_im, pout_re, pout_im), tabs):
            ref[hf] = tab[:, ls].astype(ref.dtype)
        al_re[hf] = alr[:, ls]
        al_im[hf] = ali[:, ls]
        tile_b = lambda x: jnp.concatenate([x[:, ls]] * gh, axis=0)
        wb_ref[hf, :, :sw] = jnp.where(on_b, tile_b(bb_re), 0.0).astype(wb_ref.dtype)
        wb_ref[hf, :, sw:] = jnp.where(on_b, tile_b(bb_im), 0.0).astype(wb_ref.dtype)
        cs = slice(hf * hw, (hf + 1) * hw)
        tile_c = lambda x: jnp.concatenate([x[:, cs]] * gh, axis=0)
        wc_ref[hf, :sw, :] = jnp.where(on_c, tile_c(ctr_ref[...]), 0.0).astype(wc_ref.dtype)
        wc_ref[hf, sw:, :] = jnp.where(on_c, -tile_c(cti_ref[...]), 0.0).astype(wc_ref.dtype)


def _s5_prep(a_re, a_im, log_dt, b_re, b_im, c_re, c_im, *, chunk, n_half):
    g, p = a_re.shape
    cg = b_re.shape[-1]
    gp = g * p
    sw = gp // n_half
    hw = g * cg // n_half
    row = lambda a: a.reshape(1, gp)
    tab = jax.ShapeDtypeStruct((n_half, chunk, sw), BF16)
    one = jax.ShapeDtypeStruct((n_half, 1, sw), F32)
    body = functools.partial(_s5_prep_body, chunk=chunk, n_half=n_half, n_state=p, cg=cg)
    return pl.pallas_call(
        body,
        out_shape=[jax.ShapeDtypeStruct((n_half, hw, 2 * sw), BF16),
                   jax.ShapeDtypeStruct((n_half, 2 * sw, hw), BF16),
                   tab, tab, tab, tab, one, one],
        compiler_params=pltpu.CompilerParams(vmem_limit_bytes=VMEM_LIMIT),
        name="s5_prep",
    )(row(a_re), row(a_im), row(jnp.broadcast_to(log_dt[:, None], (g, p))),
      b_re.reshape(gp, cg).T, b_im.reshape(gp, cg).T,
      c_re.transpose(2, 0, 1).reshape(p, g * cg), c_im.transpose(2, 0, 1).reshape(p, g * cg))


def _s5_body(u_ref, wb_ref, wc_ref, pin_re, pin_im, pout_re, pout_im, al_re, al_im,
             d_ref, wglu_ref, bglu_ref, y_ref, carry_ref, xs_ref, *, ts, chunk, n_half):
    t = pl.program_id(1)

    @pl.when(t == 0)
    def _():
        carry_ref[...] = jnp.zeros(carry_ref.shape, F32)

    u = u_ref[...]
    ub = u.astype(BF16)
    hw = u.shape[1] // n_half
    sw = pin_re.shape[2]
    ri = lax.broadcasted_iota(jnp.int32, (chunk, chunk), 0)
    ci = lax.broadcasted_iota(jnp.int32, (chunk, chunk), 1)
    tri = jnp.where(ci <= ri, 1.0, 0.0).astype(BF16)
    ys = [None] * n_half

    def half(hf):
        bu = jnp.dot(ub[:, hf * hw:(hf + 1) * hw], wb_ref[hf], preferred_element_type=F32)
        yield
        pr, pi = pin_re[hf], pin_im[hf]
        qr, qi = pout_re[hf], pout_im[hf]
        ar, ai = al_re[hf], al_im[hf]
        carry = [carry_ref[2 * hf:2 * hf + 1, :], carry_ref[2 * hf + 1:2 * hf + 2, :]]

        def prefix(cidx):
            rows = slice(cidx * chunk, (cidx + 1) * chunk)
            bre = bu[rows, :sw].astype(BF16)
            bim = bu[rows, sw:].astype(BF16)
            sre = pr * bre - pi * bim
            sim = pr * bim + pi * bre
            return jnp.dot(tri, jnp.concatenate([sre, sim], axis=1), preferred_element_type=F32)

        def finish(cidx, cs):
            rows = slice(cidx * chunk, (cidx + 1) * chunk)
            tre = cs[:, :sw] + carry[0]
            tim = cs[:, sw:] + carry[1]
            trb = tre.astype(BF16)
            tib = tim.astype(BF16)
            xs_ref[hf, rows, :sw] = qr * trb - qi * tib
            xs_ref[hf, rows, sw:] = qr * tib + qi * trb
            lre = tre[chunk - 1:chunk, :]
            lim = tim[chunk - 1:chunk, :]
            carry[0] = ar * lre - ai * lim
            carry[1] = ar * lim + ai * lre

        prev = None
        for cidx in range(ts // chunk):
            cs = prefix(cidx)
            if prev is not None:
                finish(*prev)
            prev = (cidx, cs)
            yield
        finish(*prev)
        carry_ref[2 * hf:2 * hf + 1, :] = carry[0]
        carry_ref[2 * hf + 1:2 * hf + 2, :] = carry[1]
        yield
        ys[hf] = jnp.dot(xs_ref[hf], wc_ref[hf], preferred_element_type=F32)

    _interleave([half(hf) for hf in range(n_half)])
    y = jnp.concatenate(ys, axis=1) + d_ref[...] * u
    y = jax.nn.gelu(y)
    gate = jax.nn.sigmoid(jnp.dot(y.astype(BF16), wglu_ref[...], preferred_element_type=F32)
                          + bglu_ref[...])
    y_ref[...] = y * gate


def _s5(u, wb, wc, tabs, d, wglu, bglu, bsz, seq, *, ts, chunk):
    nt, width = u.shape
    n_half = wb.shape[0]
    sw2 = wb.shape[2]
    sw = sw2 // 2
    n_t = seq // ts
    row = lambda b, t: (b * n_t + t, 0)
    c2 = lambda b, t: (0, 0)
    c3 = lambda b, t: (0, 0, 0)
    tab = pl.BlockSpec((n_half, chunk, sw), c3)
    one = pl.BlockSpec((n_half, 1, sw), c3)
    body = functools.partial(_s5_body, ts=ts, chunk=chunk, n_half=n_half)
    return pl.pallas_call(
        body,
        grid=(bsz, n_t),
        in_specs=[
            pl.BlockSpec((ts, width), row),
            pl.BlockSpec(wb.shape, c3),
            pl.BlockSpec(wc.shape, c3),
            tab, tab, tab, tab, one, one,
            pl.BlockSpec((1, width), c2),
            pl.BlockSpec(wglu.shape, c2),
            pl.BlockSpec((1, width), c2),
        ],
        out_specs=pl.BlockSpec((ts, width), row),
        out_shape=jax.ShapeDtypeStruct((nt, width), F32),
        scratch_shapes=[pltpu.VMEM((2 * n_half, sw), F32), pltpu.VMEM((n_half, ts, sw2), BF16)],
        compiler_params=_params("arbitrary", "arbitrary"),
        name="s5",
    )(u, wb, wc, *tabs, d, wglu, bglu)


def _mem_kv_body(m_ref, g_ref, wk_ref, wv_ref, kt_ref, v_ref):
    mn = _rms(m_ref[0], g_ref[...]).astype(BF16)
    k = jnp.dot(mn, wk_ref[...], preferred_element_type=F32)
    kt_ref[0] = k.T.astype(BF16)
    v_ref[0] = jnp.dot(mn, wv_ref[...], preferred_element_type=F32).astype(BF16)


def _mem_kv(mem, g, wk, wv):
    bsz, n_mem, d = mem.shape
    c2 = lambda b: (0, 0)
    return pl.pallas_call(
        _mem_kv_body,
        grid=(bsz,),
        in_specs=[
            pl.BlockSpec((1, n_mem, d), lambda b: (b, 0, 0)),
            pl.BlockSpec((1, d), c2),
            pl.BlockSpec((d, d), c2),
            pl.BlockSpec((d, d), c2),
        ],
        out_specs=[
            pl.BlockSpec((1, d, n_mem), lambda b: (b, 0, 0)),
            pl.BlockSpec((1, n_mem, d), lambda b: (b, 0, 0)),
        ],
        out_shape=[
            jax.ShapeDtypeStruct((bsz, d, n_mem), BF16),
            jax.ShapeDtypeStruct((bsz, n_mem, d), BF16),
        ],
        compiler_params=_params("arbitrary"),
        name="mem_kv",
    )(mem, g, wk, wv)


def _mix_attn_body(x_ref, o_ref, y_ref, wo_ref, gx_ref, wq_ref, kt_ref, v_ref, wxo_ref,
                   h_ref, *, xh, n_sub):
    dn_w = o_ref.shape[1]
    d = x_ref.shape[1]
    xd = d // xh
    scale = xd ** -0.5
    sub = x_ref.shape[0] // n_sub
    mm = functools.partial(jnp.dot, preferred_element_type=F32)

    def rows_gen(r0):
        rows = slice(r0, r0 + sub)
        mix = (mm(o_ref[rows, :].astype(BF16), wo_ref[0:dn_w, :])
               + mm(y_ref[rows, :].astype(BF16), wo_ref[dn_w:, :]))
        yield
        h1 = x_ref[rows, :] + mix
        q = mm(_rms(h1, gx_ref[...]).astype(BF16), wq_ref[...])
        yield
        q = q.astype(BF16)
        hls = [slice(h * xd, (h + 1) * xd) for h in range(xh)]
        s = [mm(q[:, hl], kt_ref[0, hl, :]) for hl in hls]
        yield
        parts = []
        for h in range(xh):
            sh = s[h] * scale
            e = jnp.exp(sh - jnp.max(sh, axis=-1, keepdims=True))
            p = (e / jnp.sum(e, axis=-1, keepdims=True)).astype(BF16)
            parts.append(mm(p, v_ref[0, :, hls[h]]))
            if h % 2 == 1:
                yield
        att = jnp.concatenate([p.astype(BF16) for p in parts], axis=1)
        proj = mm(att, wxo_ref[...])
        yield
        h_ref[rows, :] = h1 + proj

    _interleave([rows_gen(i * sub) for i in range(n_sub)])


def _mix_attn(x2, o, y, w_out, gx, wq, kt, v, wxo, bsz, seq, *, tm):
    nt, d = x2.shape
    dn_w = o.shape[1]
    n_mem = v.shape[1]
    n_t = seq // tm
    row = lambda b, t: (b * n_t + t, 0)
    c2 = lambda b, t: (0, 0)
    body = functools.partial(_mix_attn_body, xh=X_HEADS, n_sub=2)
    return pl.pallas_call(
        body,
        grid=(bsz, n_t),
        in_specs=[
            pl.BlockSpec((tm, d), row),
            pl.BlockSpec((tm, dn_w), row),
            pl.BlockSpec((tm, y.shape[1]), row),
            pl.BlockSpec(w_out.shape, c2),
            pl.BlockSpec((1, d), c2),
            pl.BlockSpec((d, d), c2),
            pl.BlockSpec((1, d, n_mem), lambda b, t: (b, 0, 0)),
            pl.BlockSpec((1, n_mem, d), lambda b, t: (b, 0, 0)),
            pl.BlockSpec((d, d), c2),
        ],
        out_specs=pl.BlockSpec((tm, d), row),
        out_shape=jax.ShapeDtypeStruct((nt, d), F32),
        compiler_params=_params("arbitrary", "arbitrary"),
        name="mix_attn",
    )(x2, o, y, w_out, gx, wq, kt, v, wxo)


def _ffn_body(h_ref, gf_ref, wg_ref, wu_ref, wd_ref, gl_ref, out_ref, *, final, fc):
    h = h_ref[...]
    hn = _rms(h, gf_ref[...]).astype(BF16)
    dff = wg_ref.shape[1]

    def down(gate, up, c0):
        act = (gate * jax.nn.sigmoid(gate) * up).astype(BF16)
        return jnp.dot(act, wd_ref[c0:c0 + fc, :], preferred_element_type=F32)

    h3 = h
    prev = None
    for c0 in range(0, dff, fc):
        gate = jnp.dot(hn, wg_ref[:, c0:c0 + fc], preferred_element_type=F32)
        up = jnp.dot(hn, wu_ref[:, c0:c0 + fc], preferred_element_type=F32)
        if prev is not None:
            h3 = h3 + down(*prev)
        prev = (gate, up, c0)
    h3 = h3 + down(*prev)
    out_ref[...] = _rms(h3, gl_ref[...]) if final else h3


def _ffn(h, gf, wg, wu, wd, gl, *, tm, final):
    nt, d = h.shape
    dff = wg.shape[1]
    c2 = lambda i: (0, 0)
    row = lambda i: (i, 0)
    return pl.pallas_call(
        functools.partial(_ffn_body, final=final, fc=2 * LANES),
        grid=(nt // tm,),
        in_specs=[
            pl.BlockSpec((tm, d), row),
            pl.BlockSpec((1, d), c2),
            pl.BlockSpec((d, dff), c2, pipeline_mode=pl.Buffered(1)),
            pl.BlockSpec((d, dff), c2, pipeline_mode=pl.Buffered(1)),
            pl.BlockSpec((dff, d), c2, pipeline_mode=pl.Buffered(1)),
            pl.BlockSpec((1, d), c2),
        ],
        out_specs=pl.BlockSpec((tm, d), row),
        out_shape=jax.ShapeDtypeStruct((nt, d), F32),
        compiler_params=_params("arbitrary"),
        name="ffn",
    )(h, gf, wg, wu, wd, gl)


def _layer(h2, mem, bsz, seq, norm_mix_g, w_in, conv_w, dn_a_log, dn_dt_bias, dn_norm_g,
           s5_a_re, s5_a_im, s5_b_re, s5_b_im, s5_c_re, s5_c_im, s5_d, s5_log_dt,
           s5_w_glu, s5_b_glu, w_out, norm_x_g, norm_mem_g, w_xq, w_xk, w_xv, w_xo):
    d = h2.shape[1]
    nh = dn_a_log.shape[0]
    dh = dn_norm_g.shape[0]
    dn_w = nh * dh
    s5_w = s5_a_re.shape[0] * s5_b_re.shape[-1]

    off_a = 4 * dn_w
    off_u = off_a + 2 * nh
    w_ab = jnp.pad(w_in[:, off_a:off_u], ((0, 0), (0, LANES - 2 * nh)))
    w_cat = jnp.concatenate([w_in[:, :off_a], w_in[:, off_u:], w_ab], axis=1).astype(BF16)
    alog = jnp.pad(dn_a_log, (0, LANES - nh)).reshape(1, LANES)
    dtb = jnp.pad(dn_dt_bias, (0, LANES - nh)).reshape(1, LANES)

    u, o = _mixer_in(h2, seq, norm_mix_g.reshape(1, d), w_cat, conv_w, alog, dtb,
                     dn_norm_g.reshape(1, dh), nh=nh, dh=dh, tb=256)

    wb, wc, *tabs = _s5_prep(s5_a_re, s5_a_im, s5_log_dt, s5_b_re, s5_b_im, s5_c_re, s5_c_im,
                             chunk=S5_CHUNK, n_half=2)
    y = _s5(u, wb, wc, tabs, s5_d.reshape(1, s5_w), s5_w_glu.astype(BF16),
            s5_b_glu.reshape(1, s5_w), bsz, seq, ts=512, chunk=S5_CHUNK)

    kt_mem, v_mem = _mem_kv(mem, norm_mem_g.reshape(1, d), w_xk.astype(BF16), w_xv.astype(BF16))
    return _mix_attn(h2, o, y, w_out.astype(BF16), norm_x_g.reshape(1, d), w_xq.astype(BF16),
                     kt_mem, v_mem, w_xo.astype(BF16), bsz, seq, tm=1024)


def kernel(x, mem, norm_mix_g, w_in, conv_w, dn_a_log, dn_dt_bias, dn_norm_g, s5_a_re, s5_a_im,
           s5_b_re, s5_b_im, s5_c_re, s5_c_im, s5_d, s5_log_dt, s5_w_glu, s5_b_glu, w_out,
           norm_x_g, norm_mem_g, w_xq, w_xk, w_xv, w_xo, norm_ffn_g, w_gate, w_up, w_down,
           norm_final_g):
    bsz, seq, d = x.shape
    depth = w_in.shape[0]
    h = x.reshape(bsz * seq, d)
    for l in range(depth):
        h = _layer(h, mem, bsz, seq, norm_mix_g[l], w_in[l], conv_w[l], dn_a_log[l],
                   dn_dt_bias[l], dn_norm_g[l], s5_a_re[l], s5_a_im[l], s5_b_re[l], s5_b_im[l],
                   s5_c_re[l], s5_c_im[l], s5_d[l], s5_log_dt[l], s5_w_glu[l], s5_b_glu[l],
                   w_out[l], norm_x_g[l], norm_mem_g[l], w_xq[l], w_xk[l], w_xv[l], w_xo[l])
        h = _ffn(h, norm_ffn_g[l].reshape(1, d), w_gate[l].astype(BF16), w_up[l].astype(BF16),
                 w_down[l].astype(BF16), norm_final_g.reshape(1, d), tm=1024, final=l == depth - 1)
    return h.reshape(bsz, seq, d)
```

```python
import functools

import jax
import jax.numpy as jnp
from jax import lax
from jax.experimental import pallas as pl
from jax.experimental.pallas import tpu as pltpu

F32 = jnp.float32
BF16 = jnp.bfloat16
EPS = 1e-6
HIGHEST = lax.Precision.HIGHEST

LANES = 128
SUBLANES = 8
DN_CHUNK = 64
S5_CHUNK = 64
X_HEADS = 4
VMEM_LIMIT = 56 * 1024 * 1024


def _bdot(a, b):
    return jnp.dot(a.astype(BF16), b.astype(BF16), preferred_element_type=F32)


def _fdot(a, b):
    return jnp.dot(a, b, precision=HIGHEST, preferred_element_type=F32)


def _rms(x, g):
    return x * lax.rsqrt(jnp.mean(x * x, axis=-1, keepdims=True) + EPS) * g


def _interleave(gens, stages_per_round=None):
    pending = [(gen, 1 if stages_per_round is None else stages_per_round[i])
               for i, gen in enumerate(gens)]
    while pending:
        for item in list(pending):
            gen, n = item
            try:
                for _ in range(n):
                    next(gen)
            except StopIteration:
                pending.remove(item)


def _params(*sem):
    return pltpu.CompilerParams(dimension_semantics=sem, vmem_limit_bytes=VMEM_LIMIT)


def _in_proj_stages(x_ref, g_ref, w_ref, cw_ref, alog_ref, dtb_ref,
                    q_ref, k_ref, kt_ref, v_ref, z_ref, u_ref, gt_ref,
                    buf_ref, *, tm, dn_w, nh, dh, cw_n):
    xnb = _rms(x_ref[...], g_ref[...]).astype(BF16)
    gw = 2 * dh

    def conv_silu(p, cols):
        outs = []
        for i in range(gw // LANES):
            c0 = cols.start + i * LANES
            s = c0 // LANES
            pc = p[:, i * LANES:(i + 1) * LANES]
            buf_ref[s, SUBLANES:SUBLANES + tm, :] = pc
            acc = cw_ref[cw_n - 1:cw_n, c0:c0 + LANES] * pc
            for j in range(cw_n - 1):
                off = SUBLANES - (cw_n - 1) + j
                acc = acc + cw_ref[j:j + 1, c0:c0 + LANES] * buf_ref[s, off:off + tm, :]
            buf_ref[s, 0:SUBLANES, :] = buf_ref[s, tm:tm + SUBLANES, :]
            outs.append(acc * jax.nn.sigmoid(acc))
        return jnp.concatenate(outs, axis=1)

    def l2n(a, scale):
        parts = []
        for i in range(gw // dh):
            ah = a[:, i * dh:(i + 1) * dh]
            parts.append(ah * (lax.rsqrt(jnp.sum(ah * ah, -1, keepdims=True) + EPS) * scale))
        return jnp.concatenate(parts, axis=1)

    def ep_q(p, c0):
        q_ref[:, c0:c0 + gw] = l2n(conv_silu(p, slice(c0, c0 + gw)), dh ** -0.5)

    def ep_k(p, c0):
        kn = l2n(conv_silu(p, slice(dn_w + c0, dn_w + c0 + gw)), 1.0)
        k_ref[:, c0:c0 + gw] = kn
        kt_ref[c0:c0 + gw, :] = kn.T

    def ep_v(p, c0):
        v_ref[:, c0:c0 + gw] = conv_silu(p, slice(2 * dn_w + c0, 2 * dn_w + c0 + gw))

    def ep_z(p, c0):
        z_ref[:, c0:c0 + gw] = p

    def ep_u(p, c0):
        u_ref[:, c0:c0 + gw] = p

    def ep_gates(ab, c0):
        lane = lax.broadcasted_iota(jnp.int32, ab.shape, 1)
        g = -jnp.exp(alog_ref[...]) * jnp.logaddexp(ab + dtb_ref[...], 0.0)
        gates = jnp.where(lane < nh, g, jax.nn.sigmoid(ab))
        gt_ref[...] = gates.T[0:SUBLANES, :]

    groups = []
    for i, ep in enumerate((ep_q, ep_k, ep_v, ep_z, ep_u)):
        groups += [(ep, i * dn_w + c0, gw, c0) for c0 in range(0, dn_w, gw)]
    groups.append((ep_gates, 5 * dn_w, LANES, 0))

    prev = None
    for ep, w0, width, c0 in groups:
        p = jnp.dot(xnb, w_ref[:, w0:w0 + width], preferred_element_type=F32)
        if prev is not None:
            prev[0](prev[1], prev[2])
        prev = (ep, p, c0)
        yield
    prev[0](prev[1], prev[2])


def _delta_stages(q_ref, k_ref, kt_ref, v_ref, z_ref, gt_ref, ng_ref,
                  o_ref, s_ref, u_sc, w_sc, qg_sc, attn_sc, kdt_sc, gtot_sc, wr, rd,
                  *, tb, nh, dh, ch):
    nc = tb // ch
    heads = range(nh)
    hls = [slice(h * dh, (h + 1) * dh) for h in heads]
    mm = functools.partial(jnp.dot, preferred_element_type=F32)

    def phase_b():
        gtot_b = gtot_sc[rd]
        s_all = [s_ref[h] for h in heads]
        for cidx in range(nc):
            rows = slice(cidx * ch, (cidx + 1) * ch)
            sb = [s_all[h].astype(BF16) for h in heads]
            ws = [mm(w_sc[rd, h, rows, :], sb[h]) for h in heads]
            qs = [mm(qg_sc[rd, h, rows, :], sb[h]) for h in heads]
            yield
            vb = [(u_sc[rd, h, rows, :] - ws[h]).astype(BF16) for h in heads]
            av = [mm(attn_sc[rd, h, rows, rows], vb[h]) for h in heads]
            kv = [mm(kdt_sc[rd, h, :, rows], vb[h]) for h in heads]
            yield
            outs = []
            for h in heads:
                a_last = jnp.exp(gtot_b[h:h + 1, cidx * ch:cidx * ch + 1])
                s_all[h] = s_all[h] * a_last + kv[h]
                o = qs[h] + av[h]
                o = o * lax.rsqrt(jnp.mean(o * o, -1, keepdims=True) + EPS)
                zh = z_ref[rows, hls[h]]
                outs.append(o * ng_ref[...] * (zh * jax.nn.sigmoid(zh)))
            o_ref[rows, :] = jnp.concatenate(outs, axis=1)
        for h in heads:
            s_ref[h] = s_all[h]

    r = lax.broadcasted_iota(jnp.int32, (tb, tb), 0)
    c = lax.broadcasted_iota(jnp.int32, (tb, tb), 1)
    same = (r // ch) == (c // ch)
    causal_bd = same & (c <= r)
    strict_bd = same & (c < r)
    gt = gt_ref[...]
    gcum_t = _fdot(gt, jnp.where(same & (r <= c), 1.0, 0.0))
    gtot_t = _fdot(gt, jnp.where(same, 1.0, 0.0))
    gcum = gcum_t.T
    gates = gt.T

    def pack(x):
        acc = x[0:ch, :]
        for i in range(1, nc):
            acc = acc + x[i * ch:(i + 1) * ch, :]
        return acc

    def unpack(x):
        return jnp.where(same, jnp.concatenate([x] * nc, axis=0), 0.0).astype(BF16)

    n_sq = max(1, (ch - 1).bit_length() - 1)

    def phase_a():
        gcol = [gcum[:, h:h + 1] for h in heads]
        grow = [gcum_t[h:h + 1, :] for h in heads]
        beta = [gates[:, nh + h:nh + h + 1] for h in heads]
        kb = [k_ref[:, hls[h]] * beta[h] for h in heads]
        gram = [mm(jnp.concatenate([kb[h], q_ref[:, hls[h]]], axis=0).astype(BF16),
                   kt_ref[hls[h], :].astype(BF16)) for h in heads]
        yield
        ppk, apk = [], []
        for h in heads:
            decay = jnp.exp(jnp.where(causal_bd, gcol[h] - grow[h], -jnp.inf))
            p_bd = -jnp.where(strict_bd, gram[h][:tb] * decay, 0.0)
            attn_sc[wr, h] = (gram[h][tb:] * decay).astype(BF16)
            pk = pack(p_bd)
            apk.append(pk)
            ppk.append(mm(pk.astype(BF16), p_bd.astype(BF16)))
        yield
        for j in range(1, n_sq + 1):
            res = []
            for h in heads:
                lhs = jnp.concatenate([apk[h], ppk[h]], axis=0) if j < n_sq else apk[h]
                res.append(mm(lhs.astype(BF16), unpack(ppk[h])))
            yield
            for h in heads:
                apk[h] = apk[h] + ppk[h] + res[h][:ch]
                if j < n_sq:
                    ppk[h] = res[h][ch:]
        egc = [jnp.exp(gcol[h]) for h in heads]
        rhs = [jnp.concatenate([v_ref[:, hls[h]] * beta[h], kb[h] * egc[h]], axis=1) for h in heads]
        prod = [mm(unpack(apk[h]), rhs[h].astype(BF16)) for h in heads]
        yield
        for h in heads:
            uw = rhs[h] + prod[h]
            u_sc[wr, h] = uw[:, :dh]
            w_sc[wr, h] = uw[:, dh:].astype(BF16)
            qg_sc[wr, h] = (q_ref[:, hls[h]] * egc[h]).astype(BF16)
            kdt_sc[wr, h] = (kt_ref[hls[h], :] * jnp.exp(gtot_t[h:h + 1, :] - grow[h])).astype(BF16)
        gtot_sc[wr] = gtot_t

    return phase_b(), phase_a()


def _mixer_in_body(x_ref, g_ref, w_ref, cw_ref, alog_ref, dtb_ref, ng_ref,
                   u_ref, o_ref,
                   buf_ref, q_sc, k_sc, kt_sc, v_sc, z_sc, gt_sc,
                   s_ref, u_sc, w_sc, qg_sc, attn_sc, kdt_sc, gtot_sc,
                   *, tb, n_t, nh, dh, cw_n):
    s = pl.program_id(0)
    dn_w = nh * dh

    @pl.when(s == 0)
    def _():
        for ref in (buf_ref, q_sc, k_sc, kt_sc, v_sc, z_sc, gt_sc,
                    s_ref, u_sc, w_sc, qg_sc, attn_sc, kdt_sc, gtot_sc):
            ref[...] = jnp.zeros(ref.shape, ref.dtype)

    @pl.when(s % n_t == 0)
    def _():
        buf_ref[:, 0:SUBLANES, :] = jnp.zeros((3 * dn_w // LANES, SUBLANES, LANES), F32)

    @pl.when((s - 2) % n_t == 0)
    def _():
        s_ref[...] = jnp.zeros(s_ref.shape, F32)

    cur = s % 2
    prv = 1 - cur
    zw = s % 3
    zr = (s + 1) % 3
    proj = _in_proj_stages(
        x_ref, g_ref, w_ref, cw_ref, alog_ref, dtb_ref,
        q_sc.at[cur], k_sc.at[cur], kt_sc.at[cur], v_sc.at[cur], z_sc.at[zw], u_ref, gt_sc.at[cur],
        buf_ref, tm=tb, dn_w=dn_w, nh=nh, dh=dh, cw_n=cw_n)
    phase_b, phase_a = _delta_stages(
        q_sc.at[prv], k_sc.at[prv], kt_sc.at[prv], v_sc.at[prv], z_sc.at[zr], gt_sc.at[prv], ng_ref,
        o_ref, s_ref, u_sc, w_sc, qg_sc, attn_sc, kdt_sc, gtot_sc, cur, prv,
        tb=tb, nh=nh, dh=dh, ch=DN_CHUNK)
    _interleave([proj, phase_b, phase_a])


def _mixer_in(x2, seq, g, w_cat, conv_w, alog, dtb, ng, *, nh, dh, tb):
    nt, d = x2.shape
    dn_w = nh * dh
    n_blk = nt // tb
    cw_n = conv_w.shape[0]
    const = lambda s: (0, 0)
    in_blk = lambda s: (jnp.minimum(s, n_blk - 1), 0)
    out_blk = lambda s: (jnp.maximum(s - 2, 0), 0)
    body = functools.partial(_mixer_in_body, tb=tb, n_t=seq // tb, nh=nh, dh=dh, cw_n=cw_n)
    two = lambda *shape: pltpu.VMEM((2,) + shape, F32)
    return pl.pallas_call(
        body,
        grid=(n_blk + 2,),
        in_specs=[pl.BlockSpec((tb, d), in_blk)]
        + [pl.BlockSpec(a.shape, const) for a in (g, w_cat, conv_w, alog, dtb, ng)],
        out_specs=[pl.BlockSpec((tb, dn_w), in_blk), pl.BlockSpec((tb, dn_w), out_blk)],
        out_shape=[jax.ShapeDtypeStruct((nt, dn_w), F32), jax.ShapeDtypeStruct((nt, dn_w), F32)],
        scratch_shapes=[
            pltpu.VMEM((3 * dn_w // LANES, tb + SUBLANES, LANES), F32),
            two(tb, dn_w), two(tb, dn_w), two(dn_w, tb), two(tb, dn_w),
            pltpu.VMEM((3, tb, dn_w), F32),
            two(SUBLANES, tb),
            pltpu.VMEM((nh, dh, dh), F32),
            pltpu.VMEM((2, nh, tb, dh), F32),
            pltpu.VMEM((2, nh, tb, dh), BF16),
            pltpu.VMEM((2, nh, tb, dh), BF16),
            pltpu.VMEM((2, nh, tb, tb), BF16),
            pltpu.VMEM((2, nh, dh, tb), BF16),
            pltpu.VMEM((2, SUBLANES, tb), F32),
        ],
        compiler_params=_params("arbitrary"),
        name="mixer_in",
    )(x2, g, w_cat, conv_w, alog, dtb, ng)


def _s5_prep_body(are_ref, aim_ref, ldt_ref, btr_ref, bti_ref, ctr_ref, cti_ref,
                  wb_ref, wc_ref, pin_re, pin_im, pout_re, pout_im, al_re, al_im,
                  *, chunk, n_half, n_state, cg):
    a_re = are_ref[...]
    a_im = aim_ref[...]
    dt = jnp.exp(ldt_ref[...])
    lre = a_re * dt
    lim = a_im * dt
    mag = jnp.exp(lre)
    nr = mag * jnp.cos(lim) - 1.0
    ni = mag * jnp.sin(lim)
    den = a_re * a_re + a_im * a_im
    cor = (nr * a_re + ni * a_im) / den
    coi = (ni * a_re - nr * a_im) / den
    bb_re = cor * btr_ref[...] - coi * bti_ref[...]
    bb_im = cor * bti_ref[...] + coi * btr_ref[...]

    n = lax.broadcasted_iota(jnp.int32, (chunk, 1), 0).astype(F32) - float(chunk // 2)
    m_out = jnp.exp(lre * n)
    m_in = jnp.exp(-(lre * n))
    ang = lim * n
    tabs = (m_in * jnp.cos(ang), -(m_in * jnp.sin(ang)), m_out * jnp.cos(ang), m_out * jnp.sin(ang))
    m_al = jnp.exp(lre * float(chunk))
    alr = m_al * jnp.cos(lim * float(chunk))
    ali = m_al * jnp.sin(lim * float(chunk))

    gp = a_re.shape[1]
    sw = gp // n_half
    gh = sw // n_state
    hw = gh * cg
    r = lax.broadcasted_iota(jnp.int32, (hw, sw), 0)
    c = lax.broadcasted_iota(jnp.int32, (hw, sw), 1)
    on_b = (r // cg) == (c // n_state)
    r2 = lax.broadcasted_iota(jnp.int32, (sw, hw), 0)
    c2 = lax.broadcasted_iota(jnp.int32, (sw, hw), 1)
    on_c = (r2 // n_state) == (c2 // cg)
    for hf in range(n_half):
        ls = slice(hf * sw, (hf + 1) * sw)
        for ref, tab in zip((pin_re, pin_im, pout_re, pout_im), tabs):
            ref[hf] = tab[:, ls].astype(ref.dtype)
        al_re[hf] = alr[:, ls]
        al_im[hf] = ali[:, ls]
        tile_b = lambda x: jnp.concatenate([x[:, ls]] * gh, axis=0)
        wb_ref[hf, :, :sw] = jnp.where(on_b, tile_b(bb_re), 0.0).astype(wb_ref.dtype)
        wb_ref[hf, :, sw:] = jnp.where(on_b, tile_b(bb_im), 0.0).astype(wb_ref.dtype)
        cs = slice(hf * hw, (hf + 1) * hw)
        tile_c = lambda x: jnp.concatenate([x[:, cs]] * gh, axis=0)
        wc_ref[hf, :sw, :] = jnp.where(on_c, tile_c(ctr_ref[...]), 0.0).astype(wc_ref.dtype)
        wc_ref[hf, sw:, :] = jnp.where(on_c, -tile_c(cti_ref[...]), 0.0).astype(wc_ref.dtype)


def _s5_prep(a_re, a_im, log_dt, b_re, b_im, c_re, c_im, *, chunk, n_half):
    g, p = a_re.shape
    cg = b_re.shape[-1]
    gp = g * p
    sw = gp // n_half
    hw = g * cg // n_half
    row = lambda a: a.reshape(1, gp)
    tab = jax.ShapeDtypeStruct((n_half, chunk, sw), BF16)
    one = jax.ShapeDtypeStruct((n_half, 1, sw), F32)
    body = functools.partial(_s5_prep_body, chunk=chunk, n_half=n_half, n_state=p, cg=cg)
    return pl.pallas_call(
        body,
        out_shape=[jax.ShapeDtypeStruct((n_half, hw, 2 * sw), BF16),
                   jax.ShapeDtypeStruct((n_half, 2 * sw, hw), BF16),
                   tab, tab, tab, tab, one, one],
        compiler_params=pltpu.CompilerParams(vmem_limit_bytes=VMEM_LIMIT),
        name="s5_prep",
    )(row(a_re), row(a_im), row(jnp.broadcast_to(log_dt[:, None], (g, p))),
      b_re.reshape(gp, cg).T, b_im.reshape(gp, cg).T,
      c_re.transpose(2, 0, 1).reshape(p, g * cg), c_im.transpose(2, 0, 1).reshape(p, g * cg))


def _s5_body(u_ref, wb_ref, wc_ref, pin_re, pin_im, pout_re, pout_im, al_re, al_im,
             d_ref, wglu_ref, bglu_ref, y_ref, carry_ref, xs_ref, *, ts, chunk, n_half):
    t = pl.program_id(1)

    @pl.when(t == 0)
    def _():
        carry_ref[...] = jnp.zeros(carry_ref.shape, F32)

    u = u_ref[...]
    ub = u.astype(BF16)
    hw = u.shape[1] // n_half
    sw = pin_re.shape[2]
    ri = lax.broadcasted_iota(jnp.int32, (chunk, chunk), 0)
    ci = lax.broadcasted_iota(jnp.int32, (chunk, chunk), 1)
    tri = jnp.where(ci <= ri, 1.0, 0.0).astype(BF16)
    ys = [None] * n_half

    def half(hf):
        bu = jnp.dot(ub[:, hf * hw:(hf + 1) * hw], wb_ref[hf], preferred_element_type=F32)
        yield
        pr, pi = pin_re[hf], pin_im[hf]
        qr, qi = pout_re[hf], pout_im[hf]
        ar, ai = al_re[hf], al_im[hf]
        carry = [carry_ref[2 * hf:2 * hf + 1, :], carry_ref[2 * hf + 1:2 * hf + 2, :]]

        def prefix(cidx):
            rows = slice(cidx * chunk, (cidx + 1) * chunk)
            bre = bu[rows, :sw].astype(BF16)
            bim = bu[rows, sw:].astype(BF16)
            sre = pr * bre - pi * bim
            sim = pr * bim + pi * bre
            return jnp.dot(tri, jnp.concatenate([sre, sim], axis=1), preferred_element_type=F32)

        def finish(cidx, cs):
            rows = slice(cidx * chunk, (cidx + 1) * chunk)
            tre = cs[:, :sw] + carry[0]
            tim = cs[:, sw:] + carry[1]
            trb = tre.astype(BF16)
            tib = tim.astype(BF16)
            xs_ref[hf, rows, :sw] = qr * trb - qi * tib
            xs_ref[hf, rows, sw:] = qr * tib + qi * trb
            lre = tre[chunk - 1:chunk, :]
            lim = tim[chunk - 1:chunk, :]
            carry[0] = ar * lre - ai * lim
            carry[1] = ar * lim + ai * lre

        prev = None
        for cidx in range(ts // chunk):
            cs = prefix(cidx)
            if prev is not None:
                finish(*prev)
            prev = (cidx, cs)
            yield
        finish(*prev)
        carry_ref[2 * hf:2 * hf + 1, :] = carry[0]
        carry_ref[2 * hf + 1:2 * hf + 2, :] = carry[1]
        yield
        ys[hf] = jnp.dot(xs_ref[hf], wc_ref[hf], preferred_element_type=F32)

    _interleave([half(hf) for hf in range(n_half)])
    y = jnp.concatenate(ys, axis=1) + d_ref[...] * u
    y = jax.nn.gelu(y)
    gate = jax.nn.sigmoid(jnp.dot(y.astype(BF16), wglu_ref[...], preferred_element_type=F32)
                          + bglu_ref[...])
    y_ref[...] = y * gate


def _s5(u, wb, wc, tabs, d, wglu, bglu, bsz, seq, *, ts, chunk):
    nt, width = u.shape
    n_half = wb.shape[0]
    sw2 = wb.shape[2]
    sw = sw2 // 2
    n_t = seq // ts
    row = lambda b, t: (b * n_t + t, 0)
    c2 = lambda b, t: (0, 0)
    c3 = lambda b, t: (0, 0, 0)
    tab = pl.BlockSpec((n_half, chunk, sw), c3)
    one = pl.BlockSpec((n_half, 1, sw), c3)
    body = functools.partial(_s5_body, ts=ts, chunk=chunk, n_half=n_half)
    return pl.pallas_call(
        body,
        grid=(bsz, n_t),
        in_specs=[
            pl.BlockSpec((ts, width), row),
            pl.BlockSpec(wb.shape, c3),
            pl.BlockSpec(wc.shape, c3),
            tab, tab, tab, tab, one, one,
            pl.BlockSpec((1, width), c2),
            pl.BlockSpec(wglu.shape, c2),
            pl.BlockSpec((1, width), c2),
        ],
        out_specs=pl.BlockSpec((ts, width), row),
        out_shape=jax.ShapeDtypeStruct((nt, width), F32),
        scratch_shapes=[pltpu.VMEM((2 * n_half, sw), F32), pltpu.VMEM((n_half, ts, sw2), BF16)],
        compiler_params=_params("arbitrary", "arbitrary"),
        name="s5",
    )(u, wb, wc, *tabs, d, wglu, bglu)


def _mem_kv_body(m_ref, g_ref, wk_ref, wv_ref, kt_ref, v_ref):
    mn = _rms(m_ref[0], g_ref[...]).astype(BF16)
    k = jnp.dot(mn, wk_ref[...].astype(BF16), preferred_element_type=F32)
    kt_ref[0] = k.T.astype(BF16)
    v_ref[0] = jnp.dot(mn, wv_ref[...].astype(BF16), preferred_element_type=F32).astype(BF16)


def _mem_kv(mem, g, wk, wv):
    bsz, n_mem, d = mem.shape
    c2 = lambda b: (0, 0)
    return pl.pallas_call(
        _mem_kv_body,
        grid=(bsz,),
        in_specs=[
            pl.BlockSpec((1, n_mem, d), lambda b: (b, 0, 0)),
            pl.BlockSpec((1, d), c2),
            pl.BlockSpec((d, d), c2),
            pl.BlockSpec((d, d), c2),
        ],
        out_specs=[
            pl.BlockSpec((1, d, n_mem), lambda b: (b, 0, 0)),
            pl.BlockSpec((1, n_mem, d), lambda b: (b, 0, 0)),
        ],
        out_shape=[
            jax.ShapeDtypeStruct((bsz, d, n_mem), BF16),
            jax.ShapeDtypeStruct((bsz, n_mem, d), BF16),
        ],
        compiler_params=_params("arbitrary"),
        name="mem_kv",
    )(mem, g, wk, wv)


def _mix_attn_body(x_ref, o_ref, y_ref, wo_ref, gx_ref, wq_ref, kt_ref, v_ref, wxo_ref,
                   h_ref, *, xh, n_sub):
    dn_w = o_ref.shape[1]
    d = x_ref.shape[1]
    xd = d // xh
    scale = xd ** -0.5
    sub = x_ref.shape[0] // n_sub
    mm = functools.partial(jnp.dot, preferred_element_type=F32)
    wo = wo_ref[...].astype(BF16)
    wq = wq_ref[...].astype(BF16)
    wxo = wxo_ref[...].astype(BF16)

    def rows_gen(r0):
        rows = slice(r0, r0 + sub)
        mix = (mm(o_ref[rows, :].astype(BF16), wo[0:dn_w, :])
               + mm(y_ref[rows, :].astype(BF16), wo[dn_w:, :]))
        yield
        h1 = x_ref[rows, :] + mix
        q = mm(_rms(h1, gx_ref[...]).astype(BF16), wq)
        yield
        q = q.astype(BF16)
        hls = [slice(h * xd, (h + 1) * xd) for h in range(xh)]
        s = [mm(q[:, hl], kt_ref[0, hl, :]) for hl in hls]
        yield
        parts = []
        for h in range(xh):
            sh = s[h] * scale
            e = jnp.exp(sh - jnp.max(sh, axis=-1, keepdims=True))
            p = (e / jnp.sum(e, axis=-1, keepdims=True)).astype(BF16)
            parts.append(mm(p, v_ref[0, :, hls[h]]))
            if h % 2 == 1:
                yield
        att = jnp.concatenate([p.astype(BF16) for p in parts], axis=1)
        proj = mm(att, wxo)
        yield
        h_ref[rows, :] = h1 + proj

    _interleave([rows_gen(i * sub) for i in range(n_sub)])


def _mix_attn(x2, o, y, w_out, gx, wq, kt, v, wxo, bsz, seq, *, tm):
    nt, d = x2.shape
    dn_w = o.shape[1]
    n_mem = v.shape[1]
    n_t = seq // tm
    row = lambda b, t: (b * n_t + t, 0)
    c2 = lambda b, t: (0, 0)
    body = functools.partial(_mix_attn_body, xh=X_HEADS, n_sub=2)
    return pl.pallas_call(
        body,
        grid=(bsz, n_t),
        in_specs=[
            pl.BlockSpec((tm, d), row),
            pl.BlockSpec((tm, dn_w), row),
            pl.BlockSpec((tm, y.shape[1]), row),
            pl.BlockSpec(w_out.shape, c2, pipeline_mode=pl.Buffered(1)),
            pl.BlockSpec((1, d), c2),
            pl.BlockSpec((d, d), c2, pipeline_mode=pl.Buffered(1)),
            pl.BlockSpec((1, d, n_mem), lambda b, t: (b, 0, 0)),
            pl.BlockSpec((1, n_mem, d), lambda b, t: (b, 0, 0)),
            pl.BlockSpec((d, d), c2, pipeline_mode=pl.Buffered(1)),
        ],
        out_specs=pl.BlockSpec((tm, d), row),
        out_shape=jax.ShapeDtypeStruct((nt, d), F32),
        compiler_params=_params("arbitrary", "arbitrary"),
        name="mix_attn",
    )(x2, o, y, w_out, gx, wq, kt, v, wxo)


def _ffn_body(h_ref, gf_ref, wg_ref, wu_ref, wd_ref, gl_ref, out_ref, *, final, fc):
    h = h_ref[...]
    hn = _rms(h, gf_ref[...]).astype(BF16)
    dff = wg_ref.shape[1]

    def down(gate, up, c0):
        act = (gate * jax.nn.sigmoid(gate) * up).astype(BF16)
        return jnp.dot(act, wd_ref[c0:c0 + fc, :].astype(BF16), preferred_element_type=F32)

    h3 = h
    prev = None
    for c0 in range(0, dff, fc):
        gate = jnp.dot(hn, wg_ref[:, c0:c0 + fc].astype(BF16), preferred_element_type=F32)
        up = jnp.dot(hn, wu_ref[:, c0:c0 + fc].astype(BF16), preferred_element_type=F32)
        if prev is not None:
            h3 = h3 + down(*prev)
        prev = (gate, up, c0)
    h3 = h3 + down(*prev)
    out_ref[...] = _rms(h3, gl_ref[...]) if final else h3


def _ffn(h, gf, wg, wu, wd, gl, *, tm, final):
    nt, d = h.shape
    dff = wg.shape[1]
    c2 = lambda i: (0, 0)
    row = lambda i: (i, 0)
    return pl.pallas_call(
        functools.partial(_ffn_body, final=final, fc=2 * LANES),
        grid=(nt // tm,),
        in_specs=[
            pl.BlockSpec((tm, d), row),
            pl.BlockSpec((1, d), c2),
            pl.BlockSpec((d, dff), c2, pipeline_mode=pl.Buffered(1)),
            pl.BlockSpec((d, dff), c2, pipeline_mode=pl.Buffered(1)),
            pl.BlockSpec((dff, d), c2, pipeline_mode=pl.Buffered(1)),
            pl.BlockSpec((1, d), c2),
        ],
        out_specs=pl.BlockSpec((tm, d), row),
        out_shape=jax.ShapeDtypeStruct((nt, d), F32),
        compiler_params=_params("arbitrary"),
        name="ffn",
    )(h, gf, wg, wu, wd, gl)


def _layer(h2, mem, bsz, seq, norm_mix_g, w_in, conv_w, dn_a_log, dn_dt_bias, dn_norm_g,
           s5_a_re, s5_a_im, s5_b_re, s5_b_im, s5_c_re, s5_c_im, s5_d, s5_log_dt,
           s5_w_glu, s5_b_glu, w_out, norm_x_g, norm_mem_g, w_xq, w_xk, w_xv, w_xo):
    d = h2.shape[1]
    nh = dn_a_log.shape[0]
    dh = dn_norm_g.shape[0]
    dn_w = nh * dh
    s5_w = s5_a_re.shape[0] * s5_b_re.shape[-1]

    off_a = 4 * dn_w
    off_u = off_a + 2 * nh
    w_ab = jnp.pad(w_in[:, off_a:off_u], ((0, 0), (0, LANES - 2 * nh)))
    w_cat = jnp.concatenate([w_in[:, :off_a], w_in[:, off_u:], w_ab], axis=1).astype(BF16)
    alog = jnp.pad(dn_a_log, (0, LANES - nh)).reshape(1, LANES)
    dtb = jnp.pad(dn_dt_bias, (0, LANES - nh)).reshape(1, LANES)

    u, o = _mixer_in(h2, seq, norm_mix_g.reshape(1, d), w_cat, conv_w, alog, dtb,
                     dn_norm_g.reshape(1, dh), nh=nh, dh=dh, tb=256)

    wb, wc, *tabs = _s5_prep(s5_a_re, s5_a_im, s5_log_dt, s5_b_re, s5_b_im, s5_c_re, s5_c_im,
                             chunk=S5_CHUNK, n_half=2)
    y = _s5(u, wb, wc, tabs, s5_d.reshape(1, s5_w), s5_w_glu.astype(BF16),
            s5_b_glu.reshape(1, s5_w), bsz, seq, ts=512, chunk=S5_CHUNK)

    kt_mem, v_mem = _mem_kv(mem, norm_mem_g.reshape(1, d), w_xk, w_xv)
    return _mix_attn(h2, o, y, w_out, norm_x_g.reshape(1, d), w_xq, kt_mem, v_mem, w_xo,
                     bsz, seq, tm=1024)


def kernel(x, mem, norm_mix_g, w_in, conv_w, dn_a_log, dn_dt_bias, dn_norm_g, s5_a_re, s5_a_im,
           s5_b_re, s5_b_im, s5_c_re, s5_c_im, s5_d, s5_log_dt, s5_w_glu, s5_b_glu, w_out,
           norm_x_g, norm_mem_g, w_xq, w_xk, w_xv, w_xo, norm_ffn_g, w_gate, w_up, w_down,
           norm_final_g):
    bsz, seq, d = x.shape
    depth = w_in.shape[0]
    h = x.reshape(bsz * seq, d)
    for l in range(depth):
        h = _layer(h, mem, bsz, seq, norm_mix_g[l], w_in[l], conv_w[l], dn_a_log[l],
                   dn_dt_bias[l], dn_norm_g[l], s5_a_re[l], s5_a_im[l], s5_b_re[l], s5_b_im[l],
                   s5_c_re[l], s5_c_im[l], s5_d[l], s5_log_dt[l], s5_w_glu[l], s5_b_glu[l],
                   w_out[l], norm_x_g[l], norm_mem_g[l], w_xq[l], w_xk[l], w_xv[l], w_xo[l])
        h = _ffn(h, norm_ffn_g[l].reshape(1, d), w_gate[l], w_up[l], w_down[l],
                 norm_final_g.reshape(1, d), tm=512, final=l == depth - 1)
    return h.reshape(bsz, seq, d)
```

```python
import functools

import jax
import jax.numpy as jnp
from jax import lax
from jax.experimental import pallas as pl
from jax.experimental.pallas import tpu as pltpu

F32 = jnp.float32
BF16 = jnp.bfloat16
EPS = 1e-6
HIGHEST = lax.Precision.HIGHEST

LANES = 128
SUBLANES = 8
DN_CHUNK = 64
S5_CHUNK = 64
X_HEADS = 4
VMEM_LIMIT = 58 * 1024 * 1024


def _bdot(a, b):
    return jnp.dot(a.astype(BF16), b.astype(BF16), preferred_element_type=F32)


def _fdot(a, b):
    return jnp.dot(a, b, precision=HIGHEST, preferred_element_type=F32)


def _rms(x, g):
    return x * lax.rsqrt(jnp.mean(x * x, axis=-1, keepdims=True) + EPS) * g


def _interleave(gens, stages_per_round=None):
    pending = [(gen, 1 if stages_per_round is None else stages_per_round[i])
               for i, gen in enumerate(gens)]
    while pending:
        for item in list(pending):
            gen, n = item
            try:
                for _ in range(n):
                    next(gen)
            except StopIteration:
                pending.remove(item)


def _params(*sem):
    return pltpu.CompilerParams(dimension_semantics=sem, vmem_limit_bytes=VMEM_LIMIT)


def _in_proj_stages(x_ref, g_ref, w_ref, cw_ref, alog_ref, dtb_ref,
                    q_ref, k_ref, kt_ref, v_ref, z_ref, u_ref, gt_ref,
                    buf_ref, *, tm, dn_w, nh, dh, cw_n):
    xnb = _rms(x_ref[...], g_ref[...]).astype(BF16)
    gw = 2 * dh

    def conv_silu(p, cols):
        outs = []
        for i in range(gw // LANES):
            c0 = cols.start + i * LANES
            s = c0 // LANES
            pc = p[:, i * LANES:(i + 1) * LANES]
            buf_ref[s, SUBLANES:SUBLANES + tm, :] = pc
            acc = cw_ref[cw_n - 1:cw_n, c0:c0 + LANES] * pc
            for j in range(cw_n - 1):
                off = SUBLANES - (cw_n - 1) + j
                acc = acc + cw_ref[j:j + 1, c0:c0 + LANES] * buf_ref[s, off:off + tm, :]
            buf_ref[s, 0:SUBLANES, :] = buf_ref[s, tm:tm + SUBLANES, :]
            outs.append(acc * jax.nn.sigmoid(acc))
        return jnp.concatenate(outs, axis=1)

    def l2n(a, scale):
        parts = []
        for i in range(gw // dh):
            ah = a[:, i * dh:(i + 1) * dh]
            parts.append(ah * (lax.rsqrt(jnp.sum(ah * ah, -1, keepdims=True) + EPS) * scale))
        return jnp.concatenate(parts, axis=1)

    def ep_q(p, c0):
        q_ref[:, c0:c0 + gw] = l2n(conv_silu(p, slice(c0, c0 + gw)), dh ** -0.5)

    def ep_k(p, c0):
        kn = l2n(conv_silu(p, slice(dn_w + c0, dn_w + c0 + gw)), 1.0)
        k_ref[:, c0:c0 + gw] = kn
        kt_ref[c0:c0 + gw, :] = kn.T

    def ep_v(p, c0):
        v_ref[:, c0:c0 + gw] = conv_silu(p, slice(2 * dn_w + c0, 2 * dn_w + c0 + gw))

    def ep_z(p, c0):
        z_ref[:, c0:c0 + gw] = p

    def ep_u(p, c0):
        u_ref[:, c0:c0 + gw] = p

    def ep_gates(ab, c0):
        lane = lax.broadcasted_iota(jnp.int32, ab.shape, 1)
        g = -jnp.exp(alog_ref[...]) * jnp.logaddexp(ab + dtb_ref[...], 0.0)
        gates = jnp.where(lane < nh, g, jax.nn.sigmoid(ab))
        gt_ref[...] = gates.T[0:SUBLANES, :]

    groups = []
    for i, ep in enumerate((ep_q, ep_k, ep_v, ep_z, ep_u)):
        groups += [(ep, i * dn_w + c0, gw, c0) for c0 in range(0, dn_w, gw)]
    groups.append((ep_gates, 5 * dn_w, LANES, 0))

    prev = None
    for ep, w0, width, c0 in groups:
        p = jnp.dot(xnb, w_ref[:, w0:w0 + width], preferred_element_type=F32)
        if prev is not None:
            prev[0](prev[1], prev[2])
        prev = (ep, p, c0)
        yield
    prev[0](prev[1], prev[2])


def _delta_stages(q_ref, k_ref, kt_ref, v_ref, z_ref, gt_ref, ng_ref,
                  o_ref, s_ref, u_sc, w_sc, qg_sc, attn_sc, kdt_sc, gtot_sc, wr, rd,
                  *, tb, nh, dh, ch):
    nc = tb // ch
    heads = range(nh)
    hls = [slice(h * dh, (h + 1) * dh) for h in heads]
    mm = functools.partial(jnp.dot, preferred_element_type=F32)

    def phase_b():
        gtot_b = gtot_sc[rd]
        s_all = [s_ref[h] for h in heads]
        for cidx in range(nc):
            rows = slice(cidx * ch, (cidx + 1) * ch)
            sb = [s_all[h].astype(BF16) for h in heads]
            ws = [mm(w_sc[rd, h, rows, :], sb[h]) for h in heads]
            qs = [mm(qg_sc[rd, h, rows, :], sb[h]) for h in heads]
            yield
            vb = [(u_sc[rd, h, rows, :] - ws[h]).astype(BF16) for h in heads]
            av = [mm(attn_sc[rd, h, rows, rows], vb[h]) for h in heads]
            kv = [mm(kdt_sc[rd, h, :, rows], vb[h]) for h in heads]
            yield
            outs = []
            for h in heads:
                a_last = jnp.exp(gtot_b[h:h + 1, cidx * ch:cidx * ch + 1])
                s_all[h] = s_all[h] * a_last + kv[h]
                o = qs[h] + av[h]
                o = o * lax.rsqrt(jnp.mean(o * o, -1, keepdims=True) + EPS)
                zh = z_ref[rows, hls[h]]
                outs.append(o * ng_ref[...] * (zh * jax.nn.sigmoid(zh)))
            o_ref[rows, :] = jnp.concatenate(outs, axis=1)
        for h in heads:
            s_ref[h] = s_all[h]

    r = lax.broadcasted_iota(jnp.int32, (tb, tb), 0)
    c = lax.broadcasted_iota(jnp.int32, (tb, tb), 1)
    same = (r // ch) == (c // ch)
    causal_bd = same & (c <= r)
    strict_bd = same & (c < r)
    gt = gt_ref[...]
    gcum_t = _fdot(gt, jnp.where(same & (r <= c), 1.0, 0.0))
    gtot_t = _fdot(gt, jnp.where(same, 1.0, 0.0))
    gcum = gcum_t.T
    gates = gt.T

    def pack(x):
        acc = x[0:ch, :]
        for i in range(1, nc):
            acc = acc + x[i * ch:(i + 1) * ch, :]
        return acc

    def unpack(x):
        return jnp.where(same, jnp.concatenate([x] * nc, axis=0), 0.0).astype(BF16)

    n_sq = max(1, (ch - 1).bit_length() - 1)

    def phase_a():
        gcol = [gcum[:, h:h + 1] for h in heads]
        grow = [gcum_t[h:h + 1, :] for h in heads]
        beta = [gates[:, nh + h:nh + h + 1] for h in heads]
        kb = [k_ref[:, hls[h]] * beta[h] for h in heads]
        gram = [mm(jnp.concatenate([kb[h], q_ref[:, hls[h]]], axis=0).astype(BF16),
                   kt_ref[hls[h], :].astype(BF16)) for h in heads]
        yield
        ppk, apk = [], []
        for h in heads:
            decay = jnp.exp(jnp.where(causal_bd, gcol[h] - grow[h], -jnp.inf))
            p_bd = -jnp.where(strict_bd, gram[h][:tb] * decay, 0.0)
            attn_sc[wr, h] = (gram[h][tb:] * decay).astype(BF16)
            pk = pack(p_bd)
            apk.append(pk)
            ppk.append(mm(pk.astype(BF16), p_bd.astype(BF16)))
        yield
        for j in range(1, n_sq + 1):
            res = []
            for h in heads:
                lhs = jnp.concatenate([apk[h], ppk[h]], axis=0) if j < n_sq else apk[h]
                res.append(mm(lhs.astype(BF16), unpack(ppk[h])))
            yield
            for h in heads:
                apk[h] = apk[h] + ppk[h] + res[h][:ch]
                if j < n_sq:
                    ppk[h] = res[h][ch:]
        egc = [jnp.exp(gcol[h]) for h in heads]
        rhs = [jnp.concatenate([v_ref[:, hls[h]] * beta[h], kb[h] * egc[h]], axis=1) for h in heads]
        prod = [mm(unpack(apk[h]), rhs[h].astype(BF16)) for h in heads]
        yield
        for h in heads:
            uw = rhs[h] + prod[h]
            u_sc[wr, h] = uw[:, :dh]
            w_sc[wr, h] = uw[:, dh:].astype(BF16)
            qg_sc[wr, h] = (q_ref[:, hls[h]] * egc[h]).astype(BF16)
            kdt_sc[wr, h] = (kt_ref[hls[h], :] * jnp.exp(gtot_t[h:h + 1, :] - grow[h])).astype(BF16)
        gtot_sc[wr] = gtot_t

    return phase_b(), phase_a()


def _mixer_in_body(x_ref, g_ref, w_ref, cw_ref, alog_ref, dtb_ref, ng_ref,
                   u_ref, o_ref,
                   buf_ref, q_sc, k_sc, kt_sc, v_sc, z_sc, gt_sc,
                   s_ref, u_sc, w_sc, qg_sc, attn_sc, kdt_sc, gtot_sc,
                   *, tb, n_t, nh, dh, cw_n):
    s = pl.program_id(0)
    dn_w = nh * dh

    @pl.when(s == 0)
    def _():
        for ref in (buf_ref, q_sc, k_sc, kt_sc, v_sc, z_sc, gt_sc,
                    s_ref, u_sc, w_sc, qg_sc, attn_sc, kdt_sc, gtot_sc):
            ref[...] = jnp.zeros(ref.shape, ref.dtype)

    @pl.when(s % n_t == 0)
    def _():
        buf_ref[:, 0:SUBLANES, :] = jnp.zeros((3 * dn_w // LANES, SUBLANES, LANES), F32)

    @pl.when((s - 2) % n_t == 0)
    def _():
        s_ref[...] = jnp.zeros(s_ref.shape, F32)

    cur = s % 2
    prv = 1 - cur
    zw = s % 3
    zr = (s + 1) % 3
    proj = _in_proj_stages(
        x_ref, g_ref, w_ref, cw_ref, alog_ref, dtb_ref,
        q_sc.at[cur], k_sc.at[cur], kt_sc.at[cur], v_sc.at[cur], z_sc.at[zw], u_ref, gt_sc.at[cur],
        buf_ref, tm=tb, dn_w=dn_w, nh=nh, dh=dh, cw_n=cw_n)
    phase_b, phase_a = _delta_stages(
        q_sc.at[prv], k_sc.at[prv], kt_sc.at[prv], v_sc.at[prv], z_sc.at[zr], gt_sc.at[prv], ng_ref,
        o_ref, s_ref, u_sc, w_sc, qg_sc, attn_sc, kdt_sc, gtot_sc, cur, prv,
        tb=tb, nh=nh, dh=dh, ch=DN_CHUNK)
    _interleave([proj, phase_b, phase_a])


def _mixer_in(x2, seq, g, w_cat, conv_w, alog, dtb, ng, *, nh, dh, tb):
    nt, d = x2.shape
    dn_w = nh * dh
    n_blk = nt // tb
    cw_n = conv_w.shape[0]
    const = lambda s: (0, 0)
    in_blk = lambda s: (jnp.minimum(s, n_blk - 1), 0)
    out_blk = lambda s: (jnp.maximum(s - 2, 0), 0)
    body = functools.partial(_mixer_in_body, tb=tb, n_t=seq // tb, nh=nh, dh=dh, cw_n=cw_n)
    two = lambda *shape: pltpu.VMEM((2,) + shape, F32)
    return pl.pallas_call(
        body,
        grid=(n_blk + 2,),
        in_specs=[pl.BlockSpec((tb, d), in_blk)]
        + [pl.BlockSpec(a.shape, const) for a in (g, w_cat, conv_w, alog, dtb, ng)],
        out_specs=[pl.BlockSpec((tb, dn_w), in_blk), pl.BlockSpec((tb, dn_w), out_blk)],
        out_shape=[jax.ShapeDtypeStruct((nt, dn_w), F32), jax.ShapeDtypeStruct((nt, dn_w), F32)],
        scratch_shapes=[
            pltpu.VMEM((3 * dn_w // LANES, tb + SUBLANES, LANES), F32),
            two(tb, dn_w), two(tb, dn_w), two(dn_w, tb), two(tb, dn_w),
            pltpu.VMEM((3, tb, dn_w), F32),
            two(SUBLANES, tb),
            pltpu.VMEM((nh, dh, dh), F32),
            pltpu.VMEM((2, nh, tb, dh), F32),
            pltpu.VMEM((2, nh, tb, dh), BF16),
            pltpu.VMEM((2, nh, tb, dh), BF16),
            pltpu.VMEM((2, nh, tb, tb), BF16),
            pltpu.VMEM((2, nh, dh, tb), BF16),
            pltpu.VMEM((2, SUBLANES, tb), F32),
        ],
        compiler_params=_params("arbitrary"),
        name="mixer_in",
    )(x2, g, w_cat, conv_w, alog, dtb, ng)


def _s5_prep_body(are_ref, aim_ref, ldt_ref, btr_ref, bti_ref, ctr_ref, cti_ref,
                  wb_ref, wc_ref, pin_re, pin_im, pout_re, pout_im, al_re, al_im,
                  *, chunk, n_half, n_state, cg):
    a_re = are_ref[...]
    a_im = aim_ref[...]
    dt = jnp.exp(ldt_ref[...])
    lre = a_re * dt
    lim = a_im * dt
    mag = jnp.exp(lre)
    nr = mag * jnp.cos(lim) - 1.0
    ni = mag * jnp.sin(lim)
    den = a_re * a_re + a_im * a_im
    cor = (nr * a_re + ni * a_im) / den
    coi = (ni * a_re - nr * a_im) / den
    bb_re = cor * btr_ref[...] - coi * bti_ref[...]
    bb_im = cor * bti_ref[...] + coi * btr_ref[...]

    n = lax.broadcasted_iota(jnp.int32, (chunk, 1), 0).astype(F32) - float(chunk // 2)
    m_out = jnp.exp(lre * n)
    m_in = jnp.exp(-(lre * n))
    ang = lim * n
    tabs = (m_in * jnp.cos(ang), -(m_in * jnp.sin(ang)), m_out * jnp.cos(ang), m_out * jnp.sin(ang))
    m_al = jnp.exp(lre * float(chunk))
    alr = m_al * jnp.cos(lim * float(chunk))
    ali = m_al * jnp.sin(lim * float(chunk))

    gp = a_re.shape[1]
    sw = gp // n_half
    gh = sw // n_state
    hw = gh * cg
    r = lax.broadcasted_iota(jnp.int32, (hw, sw), 0)
    c = lax.broadcasted_iota(jnp.int32, (hw, sw), 1)
    on_b = (r // cg) == (c // n_state)
    r2 = lax.broadcasted_iota(jnp.int32, (sw, hw), 0)
    c2 = lax.broadcasted_iota(jnp.int32, (sw, hw), 1)
    on_c = (r2 // n_state) == (c2 // cg)
    for hf in range(n_half):
        ls = slice(hf * sw, (hf + 1) * sw)
        for ref, tab in zip((pin_re, pin_im, pout_re, pout_im), tabs):
            ref[hf] = tab[:, ls].astype(ref.dtype)
        al_re[hf] = alr[:, ls]
        al_im[hf] = ali[:, ls]
        tile_b = lambda x: jnp.concatenate([x[:, ls]] * gh, axis=0)
        wb_ref[hf, :, :sw] = jnp.where(on_b, tile_b(bb_re), 0.0).astype(wb_ref.dtype)
        wb_ref[hf, :, sw:] = jnp.where(on_b, tile_b(bb_im), 0.0).astype(wb_ref.dtype)
        cs = slice(hf * hw, (hf + 1) * hw)
        tile_c = lambda x: jnp.concatenate([x[:, cs]] * gh, axis=0)
        wc_ref[hf, :sw, :] = jnp.where(on_c, tile_c(ctr_ref[...]), 0.0).astype(wc_ref.dtype)
        wc_ref[hf, sw:, :] = jnp.where(on_c, -tile_c(cti_ref[...]), 0.0).astype(wc_ref.dtype)


def _s5_prep(a_re, a_im, log_dt, b_re, b_im, c_re, c_im, *, chunk, n_half):
    g, p = a_re.shape
    cg = b_re.shape[-1]
    gp = g * p
    sw = gp // n_half
    hw = g * cg // n_half
    row = lambda a: a.reshape(1, gp)
    tab = jax.ShapeDtypeStruct((n_half, chunk, sw), BF16)
    one = jax.ShapeDtypeStruct((n_half, 1, sw), F32)
    body = functools.partial(_s5_prep_body, chunk=chunk, n_half=n_half, n_state=p, cg=cg)
    return pl.pallas_call(
        body,
        out_shape=[jax.ShapeDtypeStruct((n_half, hw, 2 * sw), BF16),
                   jax.ShapeDtypeStruct((n_half, 2 * sw, hw), BF16),
                   tab, tab, tab, tab, one, one],
        compiler_params=pltpu.CompilerParams(vmem_limit_bytes=VMEM_LIMIT),
        name="s5_prep",
    )(row(a_re), row(a_im), row(jnp.broadcast_to(log_dt[:, None], (g, p))),
      b_re.reshape(gp, cg).T, b_im.reshape(gp, cg).T,
      c_re.transpose(2, 0, 1).reshape(p, g * cg), c_im.transpose(2, 0, 1).reshape(p, g * cg))


def _s5_body(u_ref, wb_ref, wc_ref, pin_re, pin_im, pout_re, pout_im, al_re, al_im,
             d_ref, wglu_ref, bglu_ref, y_ref, carry_ref, xs_ref, *, ts, chunk, n_half):
    t = pl.program_id(1)

    @pl.when(t == 0)
    def _():
        carry_ref[...] = jnp.zeros(carry_ref.shape, F32)

    u = u_ref[...]
    ub = u.astype(BF16)
    hw = u.shape[1] // n_half
    sw = pin_re.shape[2]
    ri = lax.broadcasted_iota(jnp.int32, (chunk, chunk), 0)
    ci = lax.broadcasted_iota(jnp.int32, (chunk, chunk), 1)
    tri = jnp.where(ci <= ri, 1.0, 0.0).astype(BF16)
    ys = [None] * n_half

    def half(hf):
        bu = jnp.dot(ub[:, hf * hw:(hf + 1) * hw], wb_ref[hf], preferred_element_type=F32)
        yield
        pr, pi = pin_re[hf], pin_im[hf]
        qr, qi = pout_re[hf], pout_im[hf]
        ar, ai = al_re[hf], al_im[hf]
        carry = [carry_ref[2 * hf:2 * hf + 1, :], carry_ref[2 * hf + 1:2 * hf + 2, :]]

        def prefix(cidx):
            rows = slice(cidx * chunk, (cidx + 1) * chunk)
            bre = bu[rows, :sw].astype(BF16)
            bim = bu[rows, sw:].astype(BF16)
            sre = pr * bre - pi * bim
            sim = pr * bim + pi * bre
            return jnp.dot(tri, jnp.concatenate([sre, sim], axis=1), preferred_element_type=F32)

        def finish(cidx, cs):
            rows = slice(cidx * chunk, (cidx + 1) * chunk)
            tre = cs[:, :sw] + carry[0]
            tim = cs[:, sw:] + carry[1]
            trb = tre.astype(BF16)
            tib = tim.astype(BF16)
            xs_ref[hf, rows, :sw] = qr * trb - qi * tib
            xs_ref[hf, rows, sw:] = qr * tib + qi * trb
            lre = tre[chunk - 1:chunk, :]
            lim = tim[chunk - 1:chunk, :]
            carry[0] = ar * lre - ai * lim
            carry[1] = ar * lim + ai * lre

        prev = None
        for cidx in range(ts // chunk):
            cs = prefix(cidx)
            if prev is not None:
                finish(*prev)
            prev = (cidx, cs)
            yield
        finish(*prev)
        carry_ref[2 * hf:2 * hf + 1, :] = carry[0]
        carry_ref[2 * hf + 1:2 * hf + 2, :] = carry[1]
        yield
        ys[hf] = jnp.dot(xs_ref[hf], wc_ref[hf], preferred_element_type=F32)

    _interleave([half(hf) for hf in range(n_half)])
    y = jnp.concatenate(ys, axis=1) + d_ref[...] * u
    y = jax.nn.gelu(y)
    gate = jax.nn.sigmoid(jnp.dot(y.astype(BF16), wglu_ref[...].astype(BF16),
                                  preferred_element_type=F32)
                          + bglu_ref[...])
    y_ref[...] = y * gate


def _s5(u, wb, wc, tabs, d, wglu, bglu, bsz, seq, *, ts, chunk):
    nt, width = u.shape
    n_half = wb.shape[0]
    sw2 = wb.shape[2]
    sw = sw2 // 2
    n_t = seq // ts
    row = lambda b, t: (b * n_t + t, 0)
    c2 = lambda b, t: (0, 0)
    c3 = lambda b, t: (0, 0, 0)
    tab = pl.BlockSpec((n_half, chunk, sw), c3)
    one = pl.BlockSpec((n_half, 1, sw), c3)
    body = functools.partial(_s5_body, ts=ts, chunk=chunk, n_half=n_half)
    return pl.pallas_call(
        body,
        grid=(bsz, n_t),
        in_specs=[
            pl.BlockSpec((ts, width), row),
            pl.BlockSpec(wb.shape, c3),
            pl.BlockSpec(wc.shape, c3),
            tab, tab, tab, tab, one, one,
            pl.BlockSpec((1, width), c2),
            pl.BlockSpec(wglu.shape, c2),
            pl.BlockSpec((1, width), c2),
        ],
        out_specs=pl.BlockSpec((ts, width), row),
        out_shape=jax.ShapeDtypeStruct((nt, width), F32),
        scratch_shapes=[pltpu.VMEM((2 * n_half, sw), F32), pltpu.VMEM((n_half, ts, sw2), BF16)],
        compiler_params=_params("arbitrary", "arbitrary"),
        name="s5",
    )(u, wb, wc, *tabs, d, wglu, bglu)


def _mem_kv_body(m_ref, g_ref, wk_ref, wv_ref, kt_ref, v_ref):
    bsz, n_mem, _ = v_ref.shape
    mn = _rms(m_ref[...], g_ref[...]).astype(BF16)
    k = jnp.dot(mn, wk_ref[...].astype(BF16), preferred_element_type=F32)
    v = jnp.dot(mn, wv_ref[...].astype(BF16), preferred_element_type=F32).astype(BF16)
    for b in range(bsz):
        rows = slice(b * n_mem, (b + 1) * n_mem)
        kt_ref[b] = k[rows].T.astype(BF16)
        v_ref[b] = v[rows]


def _mem_kv(mem, g, wk, wv):
    bsz, n_mem, d = mem.shape
    return pl.pallas_call(
        _mem_kv_body,
        out_shape=[
            jax.ShapeDtypeStruct((bsz, d, n_mem), BF16),
            jax.ShapeDtypeStruct((bsz, n_mem, d), BF16),
        ],
        compiler_params=pltpu.CompilerParams(vmem_limit_bytes=VMEM_LIMIT),
        name="mem_kv",
    )(mem.reshape(bsz * n_mem, d), g, wk, wv)


def _mix_attn_body(x_ref, o_ref, y_ref, wo_ref, gx_ref, wq_ref, kt_ref, v_ref, wxo_ref,
                   h_ref, *, xh, n_sub):
    dn_w = o_ref.shape[1]
    d = x_ref.shape[1]
    xd = d // xh
    scale = xd ** -0.5
    sub = x_ref.shape[0] // n_sub
    mm = functools.partial(jnp.dot, preferred_element_type=F32)
    wo = wo_ref[...].astype(BF16)
    wq = wq_ref[...].astype(BF16)
    wxo = wxo_ref[...].astype(BF16)

    def rows_gen(r0):
        rows = slice(r0, r0 + sub)
        mix = (mm(o_ref[rows, :].astype(BF16), wo[0:dn_w, :])
               + mm(y_ref[rows, :].astype(BF16), wo[dn_w:, :]))
        yield
        h1 = x_ref[rows, :] + mix
        q = mm(_rms(h1, gx_ref[...]).astype(BF16), wq)
        yield
        q = q.astype(BF16)
        hls = [slice(h * xd, (h + 1) * xd) for h in range(xh)]
        s = [mm(q[:, hl], kt_ref[0, hl, :]) for hl in hls]
        yield
        parts = []
        for h in range(xh):
            sh = s[h] * scale
            e = jnp.exp(sh - jnp.max(sh, axis=-1, keepdims=True))
            p = (e / jnp.sum(e, axis=-1, keepdims=True)).astype(BF16)
            parts.append(mm(p, v_ref[0, :, hls[h]]))
            if h % 2 == 1:
                yield
        att = jnp.concatenate([p.astype(BF16) for p in parts], axis=1)
        proj = mm(att, wxo)
        yield
        h_ref[rows, :] = h1 + proj

    _interleave([rows_gen(i * sub) for i in range(n_sub)])


def _mix_attn(x2, o, y, w_out, gx, wq, kt, v, wxo, bsz, seq, *, tm):
    nt, d = x2.shape
    dn_w = o.shape[1]
    n_mem = v.shape[1]
    n_t = seq // tm
    row = lambda b, t: (b * n_t + t, 0)
    c2 = lambda b, t: (0, 0)
    body = functools.partial(_mix_attn_body, xh=X_HEADS, n_sub=2)
    return pl.pallas_call(
        body,
        grid=(bsz, n_t),
        in_specs=[
            pl.BlockSpec((tm, d), row),
            pl.BlockSpec((tm, dn_w), row),
            pl.BlockSpec((tm, y.shape[1]), row),
            pl.BlockSpec(w_out.shape, c2, pipeline_mode=pl.Buffered(1)),
            pl.BlockSpec((1, d), c2),
            pl.BlockSpec((d, d), c2, pipeline_mode=pl.Buffered(1)),
            pl.BlockSpec((1, d, n_mem), lambda b, t: (b, 0, 0)),
            pl.BlockSpec((1, n_mem, d), lambda b, t: (b, 0, 0)),
            pl.BlockSpec((d, d), c2, pipeline_mode=pl.Buffered(1)),
        ],
        out_specs=pl.BlockSpec((tm, d), row),
        out_shape=jax.ShapeDtypeStruct((nt, d), F32),
        compiler_params=_params("arbitrary", "arbitrary"),
        name="mix_attn",
    )(x2, o, y, w_out, gx, wq, kt, v, wxo)


def _ffn_body(h_ref, gf_ref, wg_ref, wu_ref, wd_ref, gl_ref, out_ref, *, final, fc):
    h = h_ref[...]
    hn = _rms(h, gf_ref[...]).astype(BF16)
    dff = wg_ref.shape[1]

    def down(gate, up, c0):
        act = (gate * jax.nn.sigmoid(gate) * up).astype(BF16)
        return jnp.dot(act, wd_ref[c0:c0 + fc, :].astype(BF16), preferred_element_type=F32)

    h3 = h
    prev = None
    for c0 in range(0, dff, fc):
        gate = jnp.dot(hn, wg_ref[:, c0:c0 + fc].astype(BF16), preferred_element_type=F32)
        up = jnp.dot(hn, wu_ref[:, c0:c0 + fc].astype(BF16), preferred_element_type=F32)
        if prev is not None:
            h3 = h3 + down(*prev)
        prev = (gate, up, c0)
    h3 = h3 + down(*prev)
    out_ref[...] = _rms(h3, gl_ref[...]) if final else h3


def _ffn(h, gf, wg, wu, wd, gl, *, tm, final):
    nt, d = h.shape
    dff = wg.shape[1]
    c2 = lambda i: (0, 0)
    row = lambda i: (i, 0)
    return pl.pallas_call(
        functools.partial(_ffn_body, final=final, fc=2 * LANES),
        grid=(nt // tm,),
        in_specs=[
            pl.BlockSpec((tm, d), row),
            pl.BlockSpec((1, d), c2),
            pl.BlockSpec((d, dff), c2, pipeline_mode=pl.Buffered(1)),
            pl.BlockSpec((d, dff), c2, pipeline_mode=pl.Buffered(1)),
            pl.BlockSpec((dff, d), c2, pipeline_mode=pl.Buffered(1)),
            pl.BlockSpec((1, d), c2),
        ],
        out_specs=pl.BlockSpec((tm, d), row),
        out_shape=jax.ShapeDtypeStruct((nt, d), F32),
        compiler_params=_params("arbitrary"),
        name="ffn",
    )(h, gf, wg, wu, wd, gl)


def _layer(h2, mem, bsz, seq, norm_mix_g, w_in, conv_w, dn_a_log, dn_dt_bias, dn_norm_g,
           s5_a_re, s5_a_im, s5_b_re, s5_b_im, s5_c_re, s5_c_im, s5_d, s5_log_dt,
           s5_w_glu, s5_b_glu, w_out, norm_x_g, norm_mem_g, w_xq, w_xk, w_xv, w_xo):
    d = h2.shape[1]
    nh = dn_a_log.shape[0]
    dh = dn_norm_g.shape[0]
    dn_w = nh * dh
    s5_w = s5_a_re.shape[0] * s5_b_re.shape[-1]

    off_a = 4 * dn_w
    off_u = off_a + 2 * nh
    w_ab = jnp.pad(w_in[:, off_a:off_u], ((0, 0), (0, LANES - 2 * nh)))
    w_cat = jnp.concatenate([w_in[:, :off_a], w_in[:, off_u:], w_ab], axis=1).astype(BF16)
    alog = jnp.pad(dn_a_log, (0, LANES - nh)).reshape(1, LANES)
    dtb = jnp.pad(dn_dt_bias, (0, LANES - nh)).reshape(1, LANES)

    u, o = _mixer_in(h2, seq, norm_mix_g.reshape(1, d), w_cat, conv_w, alog, dtb,
                     dn_norm_g.reshape(1, dh), nh=nh, dh=dh, tb=256)

    wb, wc, *tabs = _s5_prep(s5_a_re, s5_a_im, s5_log_dt, s5_b_re, s5_b_im, s5_c_re, s5_c_im,
                             chunk=S5_CHUNK, n_half=2)
    y = _s5(u, wb, wc, tabs, s5_d.reshape(1, s5_w), s5_w_glu,
            s5_b_glu.reshape(1, s5_w), bsz, seq, ts=512, chunk=S5_CHUNK)

    kt_mem, v_mem = _mem_kv(mem, norm_mem_g.reshape(1, d), w_xk, w_xv)
    return _mix_attn(h2, o, y, w_out, norm_x_g.reshape(1, d), w_xq, kt_mem, v_mem, w_xo,
                     bsz, seq, tm=1024)


def kernel(x, mem, norm_mix_g, w_in, conv_w, dn_a_log, dn_dt_bias, dn_norm_g, s5_a_re, s5_a_im,
           s5_b_re, s5_b_im, s5_c_re, s5_c_im, s5_d, s5_log_dt, s5_w_glu, s5_b_glu, w_out,
           norm_x_g, norm_mem_g, w_xq, w_xk, w_xv, w_xo, norm_ffn_g, w_gate, w_up, w_down,
           norm_final_g):
    bsz, seq, d = x.shape
    depth = w_in.shape[0]
    h = x.reshape(bsz * seq, d)
    for l in range(depth):
        h = _layer(h, mem, bsz, seq, norm_mix_g[l], w_in[l], conv_w[l], dn_a_log[l],
                   dn_dt_bias[l], dn_norm_g[l], s5_a_re[l], s5_a_im[l], s5_b_re[l], s5_b_im[l],
                   s5_c_re[l], s5_c_im[l], s5_d[l], s5_log_dt[l], s5_w_glu[l], s5_b_glu[l],
                   w_out[l], norm_x_g[l], norm_mem_g[l], w_xq[l], w_xk[l], w_xv[l], w_xo[l])
        h = _ffn(h, norm_ffn_g[l].reshape(1, d), w_gate[l], w_up[l], w_down[l],
                 norm_final_g.reshape(1, d), tm=1024, final=l == depth - 1)
    return h.reshape(bsz, seq, d)
```

```python
import functools

import jax
import jax.numpy as jnp
from jax import lax
from jax.experimental import pallas as pl
from jax.experimental.pallas import tpu as pltpu

F32 = jnp.float32
BF16 = jnp.bfloat16
EPS = 1e-6
HIGHEST = lax.Precision.HIGHEST

LANES = 128
SUBLANES = 8
DN_CHUNK = 64
S5_CHUNK = 64
X_HEADS = 4
VMEM_LIMIT = 58 * 1024 * 1024


def _bdot(a, b):
    return jnp.dot(a.astype(BF16), b.astype(BF16), preferred_element_type=F32)


def _fdot(a, b):
    return jnp.dot(a, b, precision=HIGHEST, preferred_element_type=F32)


def _rms(x, g):
    return x * lax.rsqrt(jnp.mean(x * x, axis=-1, keepdims=True) + EPS) * g


def _interleave(gens, stages_per_round=None):
    pending = [(gen, 1 if stages_per_round is None else stages_per_round[i])
               for i, gen in enumerate(gens)]
    while pending:
        for item in list(pending):
            gen, n = item
            try:
                for _ in range(n):
                    next(gen)
            except StopIteration:
                pending.remove(item)


def _params(*sem):
    return pltpu.CompilerParams(dimension_semantics=sem, vmem_limit_bytes=VMEM_LIMIT)


def _in_proj_stages(x_ref, g_ref, w_ref, cw_ref, alog_ref, dtb_ref,
                    q_ref, k_ref, kt_ref, v_ref, z_ref, u_ref, gt_ref,
                    buf_ref, *, tm, dn_w, nh, dh, cw_n):
    xnb = _rms(x_ref[...], g_ref[...]).astype(BF16)
    gw = 2 * dh

    def conv_silu(p, cols):
        outs = []
        for i in range(gw // LANES):
            c0 = cols.start + i * LANES
            s = c0 // LANES
            pc = p[:, i * LANES:(i + 1) * LANES]
            buf_ref[s, SUBLANES:SUBLANES + tm, :] = pc
            acc = cw_ref[cw_n - 1:cw_n, c0:c0 + LANES] * pc
            for j in range(cw_n - 1):
                off = SUBLANES - (cw_n - 1) + j
                acc = acc + cw_ref[j:j + 1, c0:c0 + LANES] * buf_ref[s, off:off + tm, :]
            buf_ref[s, 0:SUBLANES, :] = buf_ref[s, tm:tm + SUBLANES, :]
            outs.append(acc * jax.nn.sigmoid(acc))
        return jnp.concatenate(outs, axis=1)

    def l2n(a, scale):
        parts = []
        for i in range(gw // dh):
            ah = a[:, i * dh:(i + 1) * dh]
            parts.append(ah * (lax.rsqrt(jnp.sum(ah * ah, -1, keepdims=True) + EPS) * scale))
        return jnp.concatenate(parts, axis=1)

    def ep_q(p, c0):
        q_ref[:, c0:c0 + gw] = l2n(conv_silu(p, slice(c0, c0 + gw)), dh ** -0.5)

    def ep_k(p, c0):
        kn = l2n(conv_silu(p, slice(dn_w + c0, dn_w + c0 + gw)), 1.0)
        k_ref[:, c0:c0 + gw] = kn
        kt_ref[c0:c0 + gw, :] = kn.T

    def ep_v(p, c0):
        v_ref[:, c0:c0 + gw] = conv_silu(p, slice(2 * dn_w + c0, 2 * dn_w + c0 + gw))

    def ep_z(p, c0):
        z_ref[:, c0:c0 + gw] = p

    def ep_u(p, c0):
        u_ref[:, c0:c0 + gw] = p

    def ep_gates(ab, c0):
        lane = lax.broadcasted_iota(jnp.int32, ab.shape, 1)
        g = -jnp.exp(alog_ref[...]) * jnp.logaddexp(ab + dtb_ref[...], 0.0)
        gates = jnp.where(lane < nh, g, jax.nn.sigmoid(ab))
        gt_ref[...] = gates.T[0:SUBLANES, :]

    groups = []
    for i, ep in enumerate((ep_q, ep_k, ep_v, ep_z, ep_u)):
        groups += [(ep, i * dn_w + c0, gw, c0) for c0 in range(0, dn_w, gw)]
    groups.append((ep_gates, 5 * dn_w, LANES, 0))

    prev = None
    for ep, w0, width, c0 in groups:
        p = jnp.dot(xnb, w_ref[:, w0:w0 + width], preferred_element_type=F32)
        if prev is not None:
            prev[0](prev[1], prev[2])
        prev = (ep, p, c0)
        yield
    prev[0](prev[1], prev[2])


def _delta_stages(q_ref, k_ref, kt_ref, v_ref, z_ref, gt_ref, ng_ref,
                  o_ref, s_ref, u_sc, w_sc, qg_sc, attn_sc, kdt_sc, gtot_sc, wr, rd,
                  *, tb, nh, dh, ch):
    nc = tb // ch
    heads = range(nh)
    hls = [slice(h * dh, (h + 1) * dh) for h in heads]
    mm = functools.partial(jnp.dot, preferred_element_type=F32)

    def phase_b():
        gtot_b = gtot_sc[rd]
        s_all = [s_ref[h] for h in heads]
        for cidx in range(nc):
            rows = slice(cidx * ch, (cidx + 1) * ch)
            sb = [s_all[h].astype(BF16) for h in heads]
            ws = [mm(w_sc[rd, h, rows, :], sb[h]) for h in heads]
            qs = [mm(qg_sc[rd, h, rows, :], sb[h]) for h in heads]
            yield
            vb = [(u_sc[rd, h, rows, :] - ws[h]).astype(BF16) for h in heads]
            av = [mm(attn_sc[rd, h, rows, rows], vb[h]) for h in heads]
            kv = [mm(kdt_sc[rd, h, :, rows], vb[h]) for h in heads]
            yield
            outs = []
            for h in heads:
                a_last = jnp.exp(gtot_b[h:h + 1, cidx * ch:cidx * ch + 1])
                s_all[h] = s_all[h] * a_last + kv[h]
                o = qs[h] + av[h]
                o = o * lax.rsqrt(jnp.mean(o * o, -1, keepdims=True) + EPS)
                zh = z_ref[rows, hls[h]]
                outs.append(o * ng_ref[...] * (zh * jax.nn.sigmoid(zh)))
            o_ref[rows, :] = jnp.concatenate(outs, axis=1)
        for h in heads:
            s_ref[h] = s_all[h]

    r = lax.broadcasted_iota(jnp.int32, (tb, tb), 0)
    c = lax.broadcasted_iota(jnp.int32, (tb, tb), 1)
    same = (r // ch) == (c // ch)
    causal_bd = same & (c <= r)
    strict_bd = same & (c < r)
    gt = gt_ref[...]
    gcum_t = _fdot(gt, jnp.where(same & (r <= c), 1.0, 0.0))
    gtot_t = _fdot(gt, jnp.where(same, 1.0, 0.0))
    gcum = gcum_t.T
    gates = gt.T

    def pack(x):
        acc = x[0:ch, :]
        for i in range(1, nc):
            acc = acc + x[i * ch:(i + 1) * ch, :]
        return acc

    def unpack(x):
        return jnp.where(same, jnp.concatenate([x] * nc, axis=0), 0.0).astype(BF16)

    n_sq = max(1, (ch - 1).bit_length() - 1)

    def phase_a():
        gcol = [gcum[:, h:h + 1] for h in heads]
        grow = [gcum_t[h:h + 1, :] for h in heads]
        beta = [gates[:, nh + h:nh + h + 1] for h in heads]
        kb = [k_ref[:, hls[h]] * beta[h] for h in heads]
        gram = [mm(jnp.concatenate([kb[h], q_ref[:, hls[h]]], axis=0).astype(BF16),
                   kt_ref[hls[h], :].astype(BF16)) for h in heads]
        yield
        ppk, apk = [], []
        for h in heads:
            decay = jnp.exp(jnp.where(causal_bd, gcol[h] - grow[h], -jnp.inf))
            p_bd = -jnp.where(strict_bd, gram[h][:tb] * decay, 0.0)
            attn_sc[wr, h] = (gram[h][tb:] * decay).astype(BF16)
            pk = pack(p_bd)
            apk.append(pk)
            ppk.append(mm(pk.astype(BF16), p_bd.astype(BF16)))
        yield
        for j in range(1, n_sq + 1):
            res = []
            for h in heads:
                lhs = jnp.concatenate([apk[h], ppk[h]], axis=0) if j < n_sq else apk[h]
                res.append(mm(lhs.astype(BF16), unpack(ppk[h])))
            yield
            for h in heads:
                apk[h] = apk[h] + ppk[h] + res[h][:ch]
                if j < n_sq:
                    ppk[h] = res[h][ch:]
        egc = [jnp.exp(gcol[h]) for h in heads]
        rhs = [jnp.concatenate([v_ref[:, hls[h]] * beta[h], kb[h] * egc[h]], axis=1) for h in heads]
        prod = [mm(unpack(apk[h]), rhs[h].astype(BF16)) for h in heads]
        yield
        for h in heads:
            uw = rhs[h] + prod[h]
            u_sc[wr, h] = uw[:, :dh]
            w_sc[wr, h] = uw[:, dh:].astype(BF16)
            qg_sc[wr, h] = (q_ref[:, hls[h]] * egc[h]).astype(BF16)
            kdt_sc[wr, h] = (kt_ref[hls[h], :] * jnp.exp(gtot_t[h:h + 1, :] - grow[h])).astype(BF16)
        gtot_sc[wr] = gtot_t

    return phase_b(), phase_a()


def _mixer_in_body(x_ref, g_ref, w_ref, cw_ref, alog_ref, dtb_ref, ng_ref,
                   u_ref, o_ref,
                   buf_ref, q_sc, k_sc, kt_sc, v_sc, z_sc, gt_sc,
                   s_ref, u_sc, w_sc, qg_sc, attn_sc, kdt_sc, gtot_sc,
                   *, tb, n_t, nh, dh, cw_n):
    s = pl.program_id(0)
    dn_w = nh * dh

    @pl.when(s == 0)
    def _():
        for ref in (buf_ref, q_sc, k_sc, kt_sc, v_sc, z_sc, gt_sc,
                    s_ref, u_sc, w_sc, qg_sc, attn_sc, kdt_sc, gtot_sc):
            ref[...] = jnp.zeros(ref.shape, ref.dtype)

    @pl.when(s % n_t == 0)
    def _():
        buf_ref[:, 0:SUBLANES, :] = jnp.zeros((3 * dn_w // LANES, SUBLANES, LANES), F32)

    @pl.when((s - 2) % n_t == 0)
    def _():
        s_ref[...] = jnp.zeros(s_ref.shape, F32)

    cur = s % 2
    prv = 1 - cur
    zw = s % 3
    zr = (s + 1) % 3
    proj = _in_proj_stages(
        x_ref, g_ref, w_ref, cw_ref, alog_ref, dtb_ref,
        q_sc.at[cur], k_sc.at[cur], kt_sc.at[cur], v_sc.at[cur], z_sc.at[zw], u_ref, gt_sc.at[cur],
        buf_ref, tm=tb, dn_w=dn_w, nh=nh, dh=dh, cw_n=cw_n)
    phase_b, phase_a = _delta_stages(
        q_sc.at[prv], k_sc.at[prv], kt_sc.at[prv], v_sc.at[prv], z_sc.at[zr], gt_sc.at[prv], ng_ref,
        o_ref, s_ref, u_sc, w_sc, qg_sc, attn_sc, kdt_sc, gtot_sc, cur, prv,
        tb=tb, nh=nh, dh=dh, ch=DN_CHUNK)
    _interleave([proj, phase_b, phase_a])


def _mixer_in(x2, seq, g, w_cat, conv_w, alog, dtb, ng, *, nh, dh, tb):
    nt, d = x2.shape
    dn_w = nh * dh
    n_blk = nt // tb
    cw_n = conv_w.shape[0]
    const = lambda s: (0, 0)
    in_blk = lambda s: (jnp.minimum(s, n_blk - 1), 0)
    out_blk = lambda s: (jnp.maximum(s - 2, 0), 0)
    body = functools.partial(_mixer_in_body, tb=tb, n_t=seq // tb, nh=nh, dh=dh, cw_n=cw_n)
    two = lambda *shape: pltpu.VMEM((2,) + shape, F32)
    return pl.pallas_call(
        body,
        grid=(n_blk + 2,),
        in_specs=[pl.BlockSpec((tb, d), in_blk)]
        + [pl.BlockSpec(a.shape, const) for a in (g, w_cat, conv_w, alog, dtb, ng)],
        out_specs=[pl.BlockSpec((tb, dn_w), in_blk), pl.BlockSpec((tb, dn_w), out_blk)],
        out_shape=[jax.ShapeDtypeStruct((nt, dn_w), F32), jax.ShapeDtypeStruct((nt, dn_w), F32)],
        scratch_shapes=[
            pltpu.VMEM((3 * dn_w // LANES, tb + SUBLANES, LANES), F32),
            two(tb, dn_w), two(tb, dn_w), two(dn_w, tb), two(tb, dn_w),
            pltpu.VMEM((3, tb, dn_w), F32),
            two(SUBLANES, tb),
            pltpu.VMEM((nh, dh, dh), F32),
            pltpu.VMEM((2, nh, tb, dh), F32),
            pltpu.VMEM((2, nh, tb, dh), BF16),
            pltpu.VMEM((2, nh, tb, dh), BF16),
            pltpu.VMEM((2, nh, tb, tb), BF16),
            pltpu.VMEM((2, nh, dh, tb), BF16),
            pltpu.VMEM((2, SUBLANES, tb), F32),
        ],
        compiler_params=_params("arbitrary"),
        name="mixer_in",
    )(x2, g, w_cat, conv_w, alog, dtb, ng)


def _s5_prep_body(are_ref, aim_ref, ldt_ref, btr_ref, bti_ref, ctr_ref, cti_ref,
                  wb_ref, wc_ref, pin_re, pin_im, pout_re, pout_im, al_re, al_im,
                  *, chunk, n_half, n_state, cg):
    a_re = are_ref[...]
    a_im = aim_ref[...]
    dt = jnp.exp(ldt_ref[...])
    lre = a_re * dt
    lim = a_im * dt
    mag = jnp.exp(lre)
    nr = mag * jnp.cos(lim) - 1.0
    ni = mag * jnp.sin(lim)
    den = a_re * a_re + a_im * a_im
    cor = (nr * a_re + ni * a_im) / den
    coi = (ni * a_re - nr * a_im) / den
    bb_re = cor * btr_ref[...] - coi * bti_ref[...]
    bb_im = cor * bti_ref[...] + coi * btr_ref[...]

    n = lax.broadcasted_iota(jnp.int32, (chunk, 1), 0).astype(F32) - float(chunk // 2)
    m_out = jnp.exp(lre * n)
    m_in = jnp.exp(-(lre * n))
    ang = lim * n
    tabs = (m_in * jnp.cos(ang), -(m_in * jnp.sin(ang)), m_out * jnp.cos(ang), m_out * jnp.sin(ang))
    m_al = jnp.exp(lre * float(chunk))
    alr = m_al * jnp.cos(lim * float(chunk))
    ali = m_al * jnp.sin(lim * float(chunk))

    gp = a_re.shape[1]
    sw = gp // n_half
    gh = sw // n_state
    hw = gh * cg
    r = lax.broadcasted_iota(jnp.int32, (hw, sw), 0)
    c = lax.broadcasted_iota(jnp.int32, (hw, sw), 1)
    on_b = (r // cg) == (c // n_state)
    r2 = lax.broadcasted_iota(jnp.int32, (sw, hw), 0)
    c2 = lax.broadcasted_iota(jnp.int32, (sw, hw), 1)
    on_c = (r2 // n_state) == (c2 // cg)
    for hf in range(n_half):
        ls = slice(hf * sw, (hf + 1) * sw)
        for ref, tab in zip((pin_re, pin_im, pout_re, pout_im), tabs):
            ref[hf] = tab[:, ls].astype(ref.dtype)
        al_re[hf] = alr[:, ls]
        al_im[hf] = ali[:, ls]
        tile_b = lambda x: jnp.concatenate([x[:, ls]] * gh, axis=0)
        wb_ref[hf, :, :sw] = jnp.where(on_b, tile_b(bb_re), 0.0).astype(wb_ref.dtype)
        wb_ref[hf, :, sw:] = jnp.where(on_b, tile_b(bb_im), 0.0).astype(wb_ref.dtype)
        cs = slice(hf * hw, (hf + 1) * hw)
        tile_c = lambda x: jnp.concatenate([x[:, cs]] * gh, axis=0)
        wc_ref[hf, :sw, :] = jnp.where(on_c, tile_c(ctr_ref[...]), 0.0).astype(wc_ref.dtype)
        wc_ref[hf, sw:, :] = jnp.where(on_c, -tile_c(cti_ref[...]), 0.0).astype(wc_ref.dtype)


def _s5_prep(a_re, a_im, log_dt, b_re, b_im, c_re, c_im, *, chunk, n_half):
    g, p = a_re.shape
    cg = b_re.shape[-1]
    gp = g * p
    sw = gp // n_half
    hw = g * cg // n_half
    row = lambda a: a.reshape(1, gp)
    tab = jax.ShapeDtypeStruct((n_half, chunk, sw), BF16)
    one = jax.ShapeDtypeStruct((n_half, 1, sw), F32)
    body = functools.partial(_s5_prep_body, chunk=chunk, n_half=n_half, n_state=p, cg=cg)
    return pl.pallas_call(
        body,
        out_shape=[jax.ShapeDtypeStruct((n_half, hw, 2 * sw), BF16),
                   jax.ShapeDtypeStruct((n_half, 2 * sw, hw), BF16),
                   tab, tab, tab, tab, one, one],
        compiler_params=pltpu.CompilerParams(vmem_limit_bytes=VMEM_LIMIT),
        name="s5_prep",
    )(row(a_re), row(a_im), row(jnp.broadcast_to(log_dt[:, None], (g, p))),
      b_re.reshape(gp, cg).T, b_im.reshape(gp, cg).T,
      c_re.transpose(2, 0, 1).reshape(p, g * cg), c_im.transpose(2, 0, 1).reshape(p, g * cg))


def _s5_body(u_ref, wb_ref, wc_ref, pin_re, pin_im, pout_re, pout_im, al_re, al_im,
             d_ref, wglu_ref, bglu_ref, y_ref, carry_ref, xs_ref, *, ts, chunk, n_half):
    t = pl.program_id(1)

    @pl.when(t == 0)
    def _():
        carry_ref[...] = jnp.zeros(carry_ref.shape, F32)

    u = u_ref[...]
    ub = u.astype(BF16)
    hw = u.shape[1] // n_half
    sw = pin_re.shape[2]
    ri = lax.broadcasted_iota(jnp.int32, (chunk, chunk), 0)
    ci = lax.broadcasted_iota(jnp.int32, (chunk, chunk), 1)
    tri = jnp.where(ci <= ri, 1.0, 0.0).astype(BF16)
    ys = [None] * n_half

    def half(hf):
        bu = jnp.dot(ub[:, hf * hw:(hf + 1) * hw], wb_ref[hf], preferred_element_type=F32)
        yield
        pr, pi = pin_re[hf], pin_im[hf]
        qr, qi = pout_re[hf], pout_im[hf]
        ar, ai = al_re[hf], al_im[hf]
        carry = [carry_ref[2 * hf:2 * hf + 1, :], carry_ref[2 * hf + 1:2 * hf + 2, :]]

        def prefix(cidx):
            rows = slice(cidx * chunk, (cidx + 1) * chunk)
            bre = bu[rows, :sw].astype(BF16)
            bim = bu[rows, sw:].astype(BF16)
            sre = pr * bre - pi * bim
            sim = pr * bim + pi * bre
            return jnp.dot(tri, jnp.concatenate([sre, sim], axis=1), preferred_element_type=F32)

        def finish(cidx, cs):
            rows = slice(cidx * chunk, (cidx + 1) * chunk)
            tre = cs[:, :sw] + carry[0]
            tim = cs[:, sw:] + carry[1]
            trb = tre.astype(BF16)
            tib = tim.astype(BF16)
            xs_ref[hf, rows, :sw] = qr * trb - qi * tib
            xs_ref[hf, rows, sw:] = qr * tib + qi * trb
            lre = tre[chunk - 1:chunk, :]
            lim = tim[chunk - 1:chunk, :]
            carry[0] = ar * lre - ai * lim
            carry[1] = ar * lim + ai * lre

        prev = None
        for cidx in range(ts // chunk):
            cs = prefix(cidx)
            if prev is not None:
                finish(*prev)
            prev = (cidx, cs)
            yield
        finish(*prev)
        carry_ref[2 * hf:2 * hf + 1, :] = carry[0]
        carry_ref[2 * hf + 1:2 * hf + 2, :] = carry[1]
        yield
        ys[hf] = jnp.dot(xs_ref[hf], wc_ref[hf], preferred_element_type=F32)

    _interleave([half(hf) for hf in range(n_half)])
    y = jnp.concatenate(ys, axis=1) + d_ref[...] * u
    y = jax.nn.gelu(y)
    gate = jax.nn.sigmoid(jnp.dot(y.astype(BF16), wglu_ref[...].astype(BF16),
                                  preferred_element_type=F32)
                          + bglu_ref[...])
    y_ref[...] = y * gate


def _s5(u, wb, wc, tabs, d, wglu, bglu, bsz, seq, *, ts, chunk):
    nt, width = u.shape
    n_half = wb.shape[0]
    sw2 = wb.shape[2]
    sw = sw2 // 2
    n_t = seq // ts
    row = lambda b, t: (b * n_t + t, 0)
    c2 = lambda b, t: (0, 0)
    c3 = lambda b, t: (0, 0, 0)
    tab = pl.BlockSpec((n_half, chunk, sw), c3)
    one = pl.BlockSpec((n_half, 1, sw), c3)
    body = functools.partial(_s5_body, ts=ts, chunk=chunk, n_half=n_half)
    return pl.pallas_call(
        body,
        grid=(bsz, n_t),
        in_specs=[
            pl.BlockSpec((ts, width), row),
            pl.BlockSpec(wb.shape, c3),
            pl.BlockSpec(wc.shape, c3),
            tab, tab, tab, tab, one, one,
            pl.BlockSpec((1, width), c2),
            pl.BlockSpec(wglu.shape, c2),
            pl.BlockSpec((1, width), c2),
        ],
        out_specs=pl.BlockSpec((ts, width), row),
        out_shape=jax.ShapeDtypeStruct((nt, width), F32),
        scratch_shapes=[pltpu.VMEM((2 * n_half, sw), F32), pltpu.VMEM((n_half, ts, sw2), BF16)],
        compiler_params=_params("arbitrary", "arbitrary"),
        name="s5",
    )(u, wb, wc, *tabs, d, wglu, bglu)


def _mem_kv_body(m_ref, g_ref, wk_ref, wv_ref, wq_ref, wo_ref, qk_ref, vw_ref, *, xh):
    bsz, d, _ = qk_ref.shape
    n_mem = m_ref.shape[0] // bsz
    xd = d // xh
    scale = xd ** -0.5
    mn = _rms(m_ref[...], g_ref[...]).astype(BF16)
    k = jnp.dot(mn, wk_ref[...].astype(BF16), preferred_element_type=F32).astype(BF16)
    v = jnp.dot(mn, wv_ref[...].astype(BF16), preferred_element_type=F32).astype(BF16)
    wq = wq_ref[...].astype(BF16)
    wo = wo_ref[...].astype(BF16)
    nt_dims = (((1,), (1,)), ((), ()))
    for b in range(bsz):
        rows = slice(b * n_mem, (b + 1) * n_mem)
        for h in range(xh):
            hl = slice(h * xd, (h + 1) * xd)
            ml = slice(h * n_mem, (h + 1) * n_mem)
            qk = lax.dot_general(wq[:, hl], k[rows, hl], nt_dims, preferred_element_type=F32)
            qk_ref[b, :, ml] = (qk * scale).astype(BF16)
            vw_ref[b, ml, :] = jnp.dot(v[rows, hl], wo[hl, :],
                                       preferred_element_type=F32).astype(BF16)


def _mem_kv(mem, g, wk, wv, wq, wo):
    bsz, n_mem, d = mem.shape
    return pl.pallas_call(
        functools.partial(_mem_kv_body, xh=X_HEADS),
        out_shape=[
            jax.ShapeDtypeStruct((bsz, d, X_HEADS * n_mem), BF16),
            jax.ShapeDtypeStruct((bsz, X_HEADS * n_mem, d), BF16),
        ],
        compiler_params=pltpu.CompilerParams(vmem_limit_bytes=VMEM_LIMIT),
        name="mem_kv",
    )(mem.reshape(bsz * n_mem, d), g, wk, wv, wq, wo)


def _mix_attn_body(x_ref, o_ref, y_ref, wo_ref, gx_ref, qk_ref, vw_ref, h_ref, *, xh, n_sub):
    dn_w = o_ref.shape[1]
    n_mem = qk_ref.shape[2] // xh
    sub = x_ref.shape[0] // n_sub
    mm = functools.partial(jnp.dot, preferred_element_type=F32)
    wo = wo_ref[...].astype(BF16)

    def rows_gen(r0):
        rows = slice(r0, r0 + sub)
        mix = (mm(o_ref[rows, :].astype(BF16), wo[0:dn_w, :])
               + mm(y_ref[rows, :].astype(BF16), wo[dn_w:, :]))
        yield
        h1 = x_ref[rows, :] + mix
        s = mm(_rms(h1, gx_ref[...]).astype(BF16), qk_ref[0])
        yield
        parts = []
        for h in range(xh):
            sh = s[:, h * n_mem:(h + 1) * n_mem]
            e = jnp.exp(sh - jnp.max(sh, axis=-1, keepdims=True))
            parts.append((e / jnp.sum(e, axis=-1, keepdims=True)).astype(BF16))
        att = mm(jnp.concatenate(parts, axis=1), vw_ref[0])
        yield
        h_ref[rows, :] = h1 + att

    _interleave([rows_gen(i * sub) for i in range(n_sub)])


def _mix_attn(x2, o, y, w_out, gx, qk, vw, bsz, seq, *, tm):
    nt, d = x2.shape
    dn_w = o.shape[1]
    n_t = seq // tm
    row = lambda b, t: (b * n_t + t, 0)
    c2 = lambda b, t: (0, 0)
    per_batch = lambda a: pl.BlockSpec((1,) + a.shape[1:], lambda b, t: (b, 0, 0))
    body = functools.partial(_mix_attn_body, xh=X_HEADS, n_sub=2)
    return pl.pallas_call(
        body,
        grid=(bsz, n_t),
        in_specs=[
            pl.BlockSpec((tm, d), row),
            pl.BlockSpec((tm, dn_w), row),
            pl.BlockSpec((tm, y.shape[1]), row),
            pl.BlockSpec(w_out.shape, c2, pipeline_mode=pl.Buffered(1)),
            pl.BlockSpec((1, d), c2),
            per_batch(qk),
            per_batch(vw),
        ],
        out_specs=pl.BlockSpec((tm, d), row),
        out_shape=jax.ShapeDtypeStruct((nt, d), F32),
        compiler_params=_params("arbitrary", "arbitrary"),
        name="mix_attn",
    )(x2, o, y, w_out, gx, qk, vw)


def _ffn_body(h_ref, gf_ref, wg_ref, wu_ref, wd_ref, gl_ref, out_ref, *, final, fc):
    h = h_ref[...]
    hn = _rms(h, gf_ref[...]).astype(BF16)
    dff = wg_ref.shape[1]

    def down(gate, up, c0):
        act = (gate * jax.nn.sigmoid(gate) * up).astype(BF16)
        return jnp.dot(act, wd_ref[c0:c0 + fc, :].astype(BF16), preferred_element_type=F32)

    h3 = h
    prev = None
    for c0 in range(0, dff, fc):
        gate = jnp.dot(hn, wg_ref[:, c0:c0 + fc].astype(BF16), preferred_element_type=F32)
        up = jnp.dot(hn, wu_ref[:, c0:c0 + fc].astype(BF16), preferred_element_type=F32)
        if prev is not None:
            h3 = h3 + down(*prev)
        prev = (gate, up, c0)
    h3 = h3 + down(*prev)
    out_ref[...] = _rms(h3, gl_ref[...]) if final else h3


def _ffn(h, gf, wg, wu, wd, gl, *, tm, final):
    nt, d = h.shape
    dff = wg.shape[1]
    c2 = lambda i: (0, 0)
    row = lambda i: (i, 0)
    return pl.pallas_call(
        functools.partial(_ffn_body, final=final, fc=2 * LANES),
        grid=(nt // tm,),
        in_specs=[
            pl.BlockSpec((tm, d), row),
            pl.BlockSpec((1, d), c2),
            pl.BlockSpec((d, dff), c2, pipeline_mode=pl.Buffered(1)),
            pl.BlockSpec((d, dff), c2, pipeline_mode=pl.Buffered(1)),
            pl.BlockSpec((dff, d), c2, pipeline_mode=pl.Buffered(1)),
            pl.BlockSpec((1, d), c2),
        ],
        out_specs=pl.BlockSpec((tm, d), row),
        out_shape=jax.ShapeDtypeStruct((nt, d), F32),
        compiler_params=_params("arbitrary"),
        name="ffn",
    )(h, gf, wg, wu, wd, gl)


def _layer(h2, mem, bsz, seq, norm_mix_g, w_in, conv_w, dn_a_log, dn_dt_bias, dn_norm_g,
           s5_a_re, s5_a_im, s5_b_re, s5_b_im, s5_c_re, s5_c_im, s5_d, s5_log_dt,
           s5_w_glu, s5_b_glu, w_out, norm_x_g, norm_mem_g, w_xq, w_xk, w_xv, w_xo):
    d = h2.shape[1]
    nh = dn_a_log.shape[0]
    dh = dn_norm_g.shape[0]
    dn_w = nh * dh
    s5_w = s5_a_re.shape[0] * s5_b_re.shape[-1]

    off_a = 4 * dn_w
    off_u = off_a + 2 * nh
    w_ab = jnp.pad(w_in[:, off_a:off_u], ((0, 0), (0, LANES - 2 * nh)))
    w_cat = jnp.concatenate([w_in[:, :off_a], w_in[:, off_u:], w_ab], axis=1).astype(BF16)
    alog = jnp.pad(dn_a_log, (0, LANES - nh)).reshape(1, LANES)
    dtb = jnp.pad(dn_dt_bias, (0, LANES - nh)).reshape(1, LANES)

    u, o = _mixer_in(h2, seq, norm_mix_g.reshape(1, d), w_cat, conv_w, alog, dtb,
                     dn_norm_g.reshape(1, dh), nh=nh, dh=dh, tb=256)

    wb, wc, *tabs = _s5_prep(s5_a_re, s5_a_im, s5_log_dt, s5_b_re, s5_b_im, s5_c_re, s5_c_im,
                             chunk=S5_CHUNK, n_half=2)
    y = _s5(u, wb, wc, tabs, s5_d.reshape(1, s5_w), s5_w_glu,
            s5_b_glu.reshape(1, s5_w), bsz, seq, ts=512, chunk=S5_CHUNK)

    qk_mem, vw_mem = _mem_kv(mem, norm_mem_g.reshape(1, d), w_xk, w_xv, w_xq, w_xo)
    return _mix_attn(h2, o, y, w_out, norm_x_g.reshape(1, d), qk_mem, vw_mem, bsz, seq, tm=1024)


def kernel(x, mem, norm_mix_g, w_in, conv_w, dn_a_log, dn_dt_bias, dn_norm_g, s5_a_re, s5_a_im,
           s5_b_re, s5_b_im, s5_c_re, s5_c_im, s5_d, s5_log_dt, s5_w_glu, s5_b_glu, w_out,
           norm_x_g, norm_mem_g, w_xq, w_xk, w_xv, w_xo, norm_ffn_g, w_gate, w_up, w_down,
           norm_final_g):
    bsz, seq, d = x.shape
    depth = w_in.shape[0]
    h = x.reshape(bsz * seq, d)
    for l in range(depth):
        h = _layer(h, mem, bsz, seq, norm_mix_g[l], w_in[l], conv_w[l], dn_a_log[l],
                   dn_dt_bias[l], dn_norm_g[l], s5_a_re[l], s5_a_im[l], s5_b_re[l], s5_b_im[l],
                   s5_c_re[l], s5_c_im[l], s5_d[l], s5_log_dt[l], s5_w_glu[l], s5_b_glu[l],
                   w_out[l], norm_x_g[l], norm_mem_g[l], w_xq[l], w_xk[l], w_xv[l], w_xo[l])
        h = _ffn(h, norm_ffn_g[l].reshape(1, d), w_gate[l], w_up[l], w_down[l],
                 norm_final_g.reshape(1, d), tm=1024, final=l == depth - 1)
    return h.reshape(bsz, seq, d)
```

```python
import functools

import jax
import jax.numpy as jnp
from jax import lax
from jax.experimental import pallas as pl
from jax.experimental.pallas import tpu as pltpu

F32 = jnp.float32
BF16 = jnp.bfloat16
EPS = 1e-6
HIGHEST = lax.Precision.HIGHEST

LANES = 128
SUBLANES = 8
DN_CHUNK = 64
S5_CHUNK = 64
X_HEADS = 4
VMEM_LIMIT = 58 * 1024 * 1024


def _bdot(a, b):
    return jnp.dot(a.astype(BF16), b.astype(BF16), preferred_element_type=F32)


def _fdot(a, b):
    return jnp.dot(a, b, precision=HIGHEST, preferred_element_type=F32)


def _rms(x, g):
    return x * lax.rsqrt(jnp.mean(x * x, axis=-1, keepdims=True) + EPS) * g


def _interleave(gens, stages_per_round=None):
    pending = [(gen, 1 if stages_per_round is None else stages_per_round[i])
               for i, gen in enumerate(gens)]
    while pending:
        for item in list(pending):
            gen, n = item
            try:
                for _ in range(n):
                    next(gen)
            except StopIteration:
                pending.remove(item)


def _params(*sem):
    return pltpu.CompilerParams(dimension_semantics=sem, vmem_limit_bytes=VMEM_LIMIT)


def _in_proj_stages(x_ref, g_ref, w_ref, cw_ref, alog_ref, dtb_ref,
                    q_ref, k_ref, kt_ref, v_ref, z_ref, u_ref, gt_ref,
                    buf_ref, *, tm, dn_w, nh, dh, cw_n):
    xnb = _rms(x_ref[...], g_ref[...]).astype(BF16)
    gw = 2 * dh

    def conv_silu(p, cols):
        outs = []
        for i in range(gw // LANES):
            c0 = cols.start + i * LANES
            s = c0 // LANES
            pc = p[:, i * LANES:(i + 1) * LANES]
            buf_ref[s, SUBLANES:SUBLANES + tm, :] = pc
            acc = cw_ref[cw_n - 1:cw_n, c0:c0 + LANES] * pc
            for j in range(cw_n - 1):
                off = SUBLANES - (cw_n - 1) + j
                acc = acc + cw_ref[j:j + 1, c0:c0 + LANES] * buf_ref[s, off:off + tm, :]
            buf_ref[s, 0:SUBLANES, :] = buf_ref[s, tm:tm + SUBLANES, :]
            outs.append(acc * jax.nn.sigmoid(acc))
        return jnp.concatenate(outs, axis=1)

    def l2n(a, scale):
        parts = []
        for i in range(gw // dh):
            ah = a[:, i * dh:(i + 1) * dh]
            parts.append(ah * (lax.rsqrt(jnp.sum(ah * ah, -1, keepdims=True) + EPS) * scale))
        return jnp.concatenate(parts, axis=1)

    def ep_q(p, c0):
        q_ref[:, c0:c0 + gw] = l2n(conv_silu(p, slice(c0, c0 + gw)), dh ** -0.5)

    def ep_k(p, c0):
        kn = l2n(conv_silu(p, slice(dn_w + c0, dn_w + c0 + gw)), 1.0)
        k_ref[:, c0:c0 + gw] = kn
        kt_ref[c0:c0 + gw, :] = kn.T

    def ep_v(p, c0):
        v_ref[:, c0:c0 + gw] = conv_silu(p, slice(2 * dn_w + c0, 2 * dn_w + c0 + gw))

    def ep_z(p, c0):
        z_ref[:, c0:c0 + gw] = p

    def ep_u(p, c0):
        for j in range(gw // LANES):
            u_ref[c0 // LANES + j] = p[:, j * LANES:(j + 1) * LANES]

    def ep_gates(ab, c0):
        lane = lax.broadcasted_iota(jnp.int32, ab.shape, 1)
        g = -jnp.exp(alog_ref[...]) * jnp.logaddexp(ab + dtb_ref[...], 0.0)
        gates = jnp.where(lane < nh, g, jax.nn.sigmoid(ab))
        gt_ref[...] = gates.T[0:SUBLANES, :]

    groups = []
    for i, ep in enumerate((ep_q, ep_k, ep_v, ep_z, ep_u)):
        groups += [(ep, i * dn_w + c0, gw, c0) for c0 in range(0, dn_w, gw)]
    groups.append((ep_gates, 5 * dn_w, LANES, 0))

    prev = None
    for ep, w0, width, c0 in groups:
        p = jnp.dot(xnb, w_ref[:, w0:w0 + width], preferred_element_type=F32)
        if prev is not None:
            prev[0](prev[1], prev[2])
        prev = (ep, p, c0)
        yield
    prev[0](prev[1], prev[2])


def _delta_stages(q_ref, k_ref, kt_ref, v_ref, z_ref, gt_ref, ng_ref,
                  o_ref, s_ref, u_sc, w_sc, qg_sc, attn_sc, kdt_sc, gtot_sc, wr, rd,
                  *, tb, nh, dh, ch):
    nc = tb // ch
    heads = range(nh)
    hls = [slice(h * dh, (h + 1) * dh) for h in heads]
    mm = functools.partial(jnp.dot, preferred_element_type=F32)

    def phase_b():
        gtot_b = gtot_sc[rd]
        s_all = [s_ref[h] for h in heads]
        for cidx in range(nc):
            rows = slice(cidx * ch, (cidx + 1) * ch)
            sb = [s_all[h].astype(BF16) for h in heads]
            ws = [mm(w_sc[rd, h, rows, :], sb[h]) for h in heads]
            qs = [mm(qg_sc[rd, h, rows, :], sb[h]) for h in heads]
            yield
            vb = [(u_sc[rd, h, rows, :] - ws[h]).astype(BF16) for h in heads]
            av = [mm(attn_sc[rd, h, rows, rows], vb[h]) for h in heads]
            kv = [mm(kdt_sc[rd, h, :, rows], vb[h]) for h in heads]
            yield
            outs = []
            for h in heads:
                a_last = jnp.exp(gtot_b[h:h + 1, cidx * ch:cidx * ch + 1])
                s_all[h] = s_all[h] * a_last + kv[h]
                o = qs[h] + av[h]
                o = o * lax.rsqrt(jnp.mean(o * o, -1, keepdims=True) + EPS)
                zh = z_ref[rows, hls[h]]
                outs.append(o * ng_ref[...] * (zh * jax.nn.sigmoid(zh)))
            o_ref[rows, :] = jnp.concatenate(outs, axis=1)
        for h in heads:
            s_ref[h] = s_all[h]

    r = lax.broadcasted_iota(jnp.int32, (tb, tb), 0)
    c = lax.broadcasted_iota(jnp.int32, (tb, tb), 1)
    same = (r // ch) == (c // ch)
    causal_bd = same & (c <= r)
    strict_bd = same & (c < r)
    gt = gt_ref[...]
    gcum_t = _fdot(gt, jnp.where(same & (r <= c), 1.0, 0.0))
    gtot_t = _fdot(gt, jnp.where(same, 1.0, 0.0))
    gcum = gcum_t.T
    gates = gt.T

    def pack(x):
        acc = x[0:ch, :]
        for i in range(1, nc):
            acc = acc + x[i * ch:(i + 1) * ch, :]
        return acc

    def unpack(x):
        return jnp.where(same, jnp.concatenate([x] * nc, axis=0), 0.0).astype(BF16)

    n_sq = max(1, (ch - 1).bit_length() - 1)

    def phase_a():
        gcol = [gcum[:, h:h + 1] for h in heads]
        grow = [gcum_t[h:h + 1, :] for h in heads]
        beta = [gates[:, nh + h:nh + h + 1] for h in heads]
        kb = [k_ref[:, hls[h]] * beta[h] for h in heads]
        gram = [mm(jnp.concatenate([kb[h], q_ref[:, hls[h]]], axis=0).astype(BF16),
                   kt_ref[hls[h], :].astype(BF16)) for h in heads]
        yield
        ppk, apk = [], []
        for h in heads:
            decay = jnp.exp(jnp.where(causal_bd, gcol[h] - grow[h], -jnp.inf))
            p_bd = -jnp.where(strict_bd, gram[h][:tb] * decay, 0.0)
            attn_sc[wr, h] = (gram[h][tb:] * decay).astype(BF16)
            pk = pack(p_bd)
            apk.append(pk)
            ppk.append(mm(pk.astype(BF16), p_bd.astype(BF16)))
        yield
        for j in range(1, n_sq + 1):
            res = []
            for h in heads:
                lhs = jnp.concatenate([apk[h], ppk[h]], axis=0) if j < n_sq else apk[h]
                res.append(mm(lhs.astype(BF16), unpack(ppk[h])))
            yield
            for h in heads:
                apk[h] = apk[h] + ppk[h] + res[h][:ch]
                if j < n_sq:
                    ppk[h] = res[h][ch:]
        egc = [jnp.exp(gcol[h]) for h in heads]
        rhs = [jnp.concatenate([v_ref[:, hls[h]] * beta[h], kb[h] * egc[h]], axis=1) for h in heads]
        prod = [mm(unpack(apk[h]), rhs[h].astype(BF16)) for h in heads]
        yield
        for h in heads:
            uw = rhs[h] + prod[h]
            u_sc[wr, h] = uw[:, :dh]
            w_sc[wr, h] = uw[:, dh:].astype(BF16)
            qg_sc[wr, h] = (q_ref[:, hls[h]] * egc[h]).astype(BF16)
            kdt_sc[wr, h] = (kt_ref[hls[h], :] * jnp.exp(gtot_t[h:h + 1, :] - grow[h])).astype(BF16)
        gtot_sc[wr] = gtot_t

    return phase_b(), phase_a()


def _mixer_in_body(x_ref, g_ref, w_ref, cw_ref, alog_ref, dtb_ref, ng_ref,
                   u_ref, o_ref,
                   buf_ref, q_sc, k_sc, kt_sc, v_sc, z_sc, gt_sc,
                   s_ref, u_sc, w_sc, qg_sc, attn_sc, kdt_sc, gtot_sc,
                   *, tb, n_t, nh, dh, cw_n):
    s = pl.program_id(0)
    dn_w = nh * dh

    @pl.when(s == 0)
    def _():
        for ref in (buf_ref, q_sc, k_sc, kt_sc, v_sc, z_sc, gt_sc,
                    s_ref, u_sc, w_sc, qg_sc, attn_sc, kdt_sc, gtot_sc):
            ref[...] = jnp.zeros(ref.shape, ref.dtype)

    @pl.when(s % n_t == 0)
    def _():
        buf_ref[:, 0:SUBLANES, :] = jnp.zeros((3 * dn_w // LANES, SUBLANES, LANES), F32)

    @pl.when((s - 2) % n_t == 0)
    def _():
        s_ref[...] = jnp.zeros(s_ref.shape, F32)

    cur = s % 2
    prv = 1 - cur
    zw = s % 3
    zr = (s + 1) % 3
    proj = _in_proj_stages(
        x_ref, g_ref, w_ref, cw_ref, alog_ref, dtb_ref,
        q_sc.at[cur], k_sc.at[cur], kt_sc.at[cur], v_sc.at[cur], z_sc.at[zw], u_ref, gt_sc.at[cur],
        buf_ref, tm=tb, dn_w=dn_w, nh=nh, dh=dh, cw_n=cw_n)
    phase_b, phase_a = _delta_stages(
        q_sc.at[prv], k_sc.at[prv], kt_sc.at[prv], v_sc.at[prv], z_sc.at[zr], gt_sc.at[prv], ng_ref,
        o_ref, s_ref, u_sc, w_sc, qg_sc, attn_sc, kdt_sc, gtot_sc, cur, prv,
        tb=tb, nh=nh, dh=dh, ch=DN_CHUNK)
    _interleave([proj, phase_b, phase_a])


def _mixer_in(x2, seq, g, w_cat, conv_w, alog, dtb, ng, *, nh, dh, tb):
    nt, d = x2.shape
    dn_w = nh * dh
    n_blk = nt // tb
    cw_n = conv_w.shape[0]
    const = lambda s: (0, 0)
    in_blk = lambda s: (jnp.minimum(s, n_blk - 1), 0)
    out_blk = lambda s: (jnp.maximum(s - 2, 0), 0)
    body = functools.partial(_mixer_in_body, tb=tb, n_t=seq // tb, nh=nh, dh=dh, cw_n=cw_n)
    two = lambda *shape: pltpu.VMEM((2,) + shape, F32)
    return pl.pallas_call(
        body,
        grid=(n_blk + 2,),
        in_specs=[pl.BlockSpec((tb, d), in_blk)]
        + [pl.BlockSpec(a.shape, const) for a in (g, w_cat, conv_w, alog, dtb, ng)],
        out_specs=[pl.BlockSpec((dn_w // LANES, tb, LANES),
                                lambda s: (0, jnp.minimum(s, n_blk - 1), 0)),
                   pl.BlockSpec((tb, dn_w), out_blk)],
        out_shape=[jax.ShapeDtypeStruct((dn_w // LANES, nt, LANES), F32),
                   jax.ShapeDtypeStruct((nt, dn_w), F32)],
        scratch_shapes=[
            pltpu.VMEM((3 * dn_w // LANES, tb + SUBLANES, LANES), F32),
            two(tb, dn_w), two(tb, dn_w), two(dn_w, tb), two(tb, dn_w),
            pltpu.VMEM((3, tb, dn_w), F32),
            two(SUBLANES, tb),
            pltpu.VMEM((nh, dh, dh), F32),
            pltpu.VMEM((2, nh, tb, dh), F32),
            pltpu.VMEM((2, nh, tb, dh), BF16),
            pltpu.VMEM((2, nh, tb, dh), BF16),
            pltpu.VMEM((2, nh, tb, tb), BF16),
            pltpu.VMEM((2, nh, dh, tb), BF16),
            pltpu.VMEM((2, SUBLANES, tb), F32),
        ],
        compiler_params=_params("arbitrary"),
        name="mixer_in",
    )(x2, g, w_cat, conv_w, alog, dtb, ng)


def _s5_prep_body(are_ref, aim_ref, ldt_ref, btr_ref, bti_ref, ctr_ref, cti_ref,
                  wb_ref, wc_ref, pin_re, pin_im, pout_re, pout_im, al_re, al_im,
                  *, chunk, n_half, n_state, cg):
    a_re = are_ref[...]
    a_im = aim_ref[...]
    dt = jnp.exp(ldt_ref[...])
    lre = a_re * dt
    lim = a_im * dt
    mag = jnp.exp(lre)
    nr = mag * jnp.cos(lim) - 1.0
    ni = mag * jnp.sin(lim)
    den = a_re * a_re + a_im * a_im
    cor = (nr * a_re + ni * a_im) / den
    coi = (ni * a_re - nr * a_im) / den
    bb_re = cor * btr_ref[...] - coi * bti_ref[...]
    bb_im = cor * bti_ref[...] + coi * btr_ref[...]

    n = lax.broadcasted_iota(jnp.int32, (chunk, 1), 0).astype(F32) - float(chunk // 2)
    m_out = jnp.exp(lre * n)
    m_in = jnp.exp(-(lre * n))
    ang = lim * n
    tabs = (m_in * jnp.cos(ang), -(m_in * jnp.sin(ang)), m_out * jnp.cos(ang), m_out * jnp.sin(ang))
    m_al = jnp.exp(lre * float(chunk))
    alr = m_al * jnp.cos(lim * float(chunk))
    ali = m_al * jnp.sin(lim * float(chunk))

    gp = a_re.shape[1]
    sw = gp // n_half
    gh = sw // n_state
    hw = gh * cg
    r = lax.broadcasted_iota(jnp.int32, (hw, sw), 0)
    c = lax.broadcasted_iota(jnp.int32, (hw, sw), 1)
    on_b = (r // cg) == (c // n_state)
    r2 = lax.broadcasted_iota(jnp.int32, (sw, hw), 0)
    c2 = lax.broadcasted_iota(jnp.int32, (sw, hw), 1)
    on_c = (r2 // n_state) == (c2 // cg)
    for hf in range(n_half):
        ls = slice(hf * sw, (hf + 1) * sw)
        for ref, tab in zip((pin_re, pin_im, pout_re, pout_im), tabs):
            ref[hf] = tab[:, ls].astype(ref.dtype)
        al_re[hf] = alr[:, ls]
        al_im[hf] = ali[:, ls]
        tile_b = lambda x: jnp.concatenate([x[:, ls]] * gh, axis=0)
        wb_ref[hf, :, :sw] = jnp.where(on_b, tile_b(bb_re), 0.0).astype(wb_ref.dtype)
        wb_ref[hf, :, sw:] = jnp.where(on_b, tile_b(bb_im), 0.0).astype(wb_ref.dtype)
        cs = slice(hf * hw, (hf + 1) * hw)
        tile_c = lambda x: jnp.concatenate([x[:, cs]] * gh, axis=0)
        wc_ref[hf, :sw, :] = jnp.where(on_c, tile_c(ctr_ref[...]), 0.0).astype(wc_ref.dtype)
        wc_ref[hf, sw:, :] = jnp.where(on_c, -tile_c(cti_ref[...]), 0.0).astype(wc_ref.dtype)


def _s5_prep(a_re, a_im, log_dt, b_re, b_im, c_re, c_im, *, chunk, n_half):
    g, p = a_re.shape
    cg = b_re.shape[-1]
    gp = g * p
    sw = gp // n_half
    hw = g * cg // n_half
    row = lambda a: a.reshape(1, gp)
    tab = jax.ShapeDtypeStruct((n_half, chunk, sw), BF16)
    one = jax.ShapeDtypeStruct((n_half, 1, sw), F32)
    body = functools.partial(_s5_prep_body, chunk=chunk, n_half=n_half, n_state=p, cg=cg)
    return pl.pallas_call(
        body,
        out_shape=[jax.ShapeDtypeStruct((n_half, hw, 2 * sw), BF16),
                   jax.ShapeDtypeStruct((n_half, 2 * sw, hw), BF16),
                   tab, tab, tab, tab, one, one],
        compiler_params=pltpu.CompilerParams(vmem_limit_bytes=VMEM_LIMIT),
        name="s5_prep",
    )(row(a_re), row(a_im), row(jnp.broadcast_to(log_dt[:, None], (g, p))),
      b_re.reshape(gp, cg).T, b_im.reshape(gp, cg).T,
      c_re.transpose(2, 0, 1).reshape(p, g * cg), c_im.transpose(2, 0, 1).reshape(p, g * cg))


def _s5_body(u_ref, wb_ref, wc_ref, pin_re, pin_im, pout_re, pout_im, al_re, al_im,
             d_ref, wglu_ref, bglu_ref, y_ref, carry_ref, xs_ref, *, ts, chunk, n_half):
    t = pl.program_id(1)

    @pl.when(t == 0)
    def _():
        carry_ref[...] = jnp.zeros(carry_ref.shape, F32)

    u = u_ref[...]
    ub = u.astype(BF16)
    hw = u.shape[1] // n_half
    sw = pin_re.shape[2]
    ri = lax.broadcasted_iota(jnp.int32, (chunk, chunk), 0)
    ci = lax.broadcasted_iota(jnp.int32, (chunk, chunk), 1)
    tri = jnp.where(ci <= ri, 1.0, 0.0).astype(BF16)
    ys = [None] * n_half

    def half(hf):
        bu = jnp.dot(ub[:, hf * hw:(hf + 1) * hw], wb_ref[hf], preferred_element_type=F32)
        yield
        pr, pi = pin_re[hf], pin_im[hf]
        qr, qi = pout_re[hf], pout_im[hf]
        ar, ai = al_re[hf], al_im[hf]
        carry = [carry_ref[2 * hf:2 * hf + 1, :], carry_ref[2 * hf + 1:2 * hf + 2, :]]

        def prefix(cidx):
            rows = slice(cidx * chunk, (cidx + 1) * chunk)
            bre = bu[rows, :sw].astype(BF16)
            bim = bu[rows, sw:].astype(BF16)
            sre = pr * bre - pi * bim
            sim = pr * bim + pi * bre
            return jnp.dot(tri, jnp.concatenate([sre, sim], axis=1), preferred_element_type=F32)

        def finish(cidx, cs):
            rows = slice(cidx * chunk, (cidx + 1) * chunk)
            tre = cs[:, :sw] + carry[0]
            tim = cs[:, sw:] + carry[1]
            trb = tre.astype(BF16)
            tib = tim.astype(BF16)
            xs_ref[hf, rows, :sw] = qr * trb - qi * tib
            xs_ref[hf, rows, sw:] = qr * tib + qi * trb
            lre = tre[chunk - 1:chunk, :]
            lim = tim[chunk - 1:chunk, :]
            carry[0] = ar * lre - ai * lim
            carry[1] = ar * lim + ai * lre

        prev = None
        for cidx in range(ts // chunk):
            cs = prefix(cidx)
            if prev is not None:
                finish(*prev)
            prev = (cidx, cs)
            yield
        finish(*prev)
        carry_ref[2 * hf:2 * hf + 1, :] = carry[0]
        carry_ref[2 * hf + 1:2 * hf + 2, :] = carry[1]
        yield
        ys[hf] = jnp.dot(xs_ref[hf], wc_ref[hf], preferred_element_type=F32)

    _interleave([half(hf) for hf in range(n_half)])
    y = jnp.concatenate(ys, axis=1) + d_ref[...] * u
    y = jax.nn.gelu(y)
    gate = jax.nn.sigmoid(jnp.dot(y.astype(BF16), wglu_ref[...].astype(BF16),
                                  preferred_element_type=F32)
                          + bglu_ref[...])
    y_ref[...] = y * gate


def _s5(u, wb, wc, tabs, d, wglu, bglu, bsz, seq, *, ts, chunk):
    nt, width = u.shape
    n_half = wb.shape[0]
    sw2 = wb.shape[2]
    sw = sw2 // 2
    n_t = seq // ts
    row = lambda b, t: (b * n_t + t, 0)
    c2 = lambda b, t: (0, 0)
    c3 = lambda b, t: (0, 0, 0)
    tab = pl.BlockSpec((n_half, chunk, sw), c3)
    one = pl.BlockSpec((n_half, 1, sw), c3)
    body = functools.partial(_s5_body, ts=ts, chunk=chunk, n_half=n_half)
    return pl.pallas_call(
        body,
        grid=(bsz, n_t),
        in_specs=[
            pl.BlockSpec((ts, width), row),
            pl.BlockSpec(wb.shape, c3),
            pl.BlockSpec(wc.shape, c3),
            tab, tab, tab, tab, one, one,
            pl.BlockSpec((1, width), c2),
            pl.BlockSpec(wglu.shape, c2),
            pl.BlockSpec((1, width), c2),
        ],
        out_specs=pl.BlockSpec((ts, width), row),
        out_shape=jax.ShapeDtypeStruct((nt, width), F32),
        scratch_shapes=[pltpu.VMEM((2 * n_half, sw), F32), pltpu.VMEM((n_half, ts, sw2), BF16)],
        compiler_params=_params("arbitrary", "arbitrary"),
        name="s5",
    )(u, wb, wc, *tabs, d, wglu, bglu)


T_CHUNK = 16
S_CHUNK = 16


def _ftdot(a, b):
    return lax.dot_general(a.astype(BF16), b.astype(BF16), (((0,), (0,)), ((), ())),
                           preferred_element_type=F32)


def _cmul(xr, xi, yr, yi):
    return xr * yr - xi * yi, xr * yi + xi * yr


def _s5t_prep_body(are_ref, aim_ref, ldt_ref, btr_ref, bti_ref, ctr_ref, cti_ref, w1_ref, co_ref,
                   *, tc, cg, n_p):
    n_grp = w1_ref.shape[0]
    kw = tc * cg
    a_re = are_ref[...]
    a_im = aim_ref[...]
    dt = jnp.exp(ldt_ref[...])
    lre = a_re * dt
    lim = a_im * dt
    mag = jnp.exp(lre)
    ab = (mag * jnp.cos(lim), mag * jnp.sin(lim))
    nr, ni = ab[0] - 1.0, ab[1]
    den = a_re * a_re + a_im * a_im
    co = ((nr * a_re + ni * a_im) / den, (ni * a_re - nr * a_im) / den)
    inv_mag2 = jnp.exp(-2.0 * lre)
    ai = (ab[0] * inv_mag2, -ab[1] * inv_mag2)
    n_bits = tc.bit_length() - 1
    pos, neg = [ab], [ai]
    for _ in range(n_bits - 1):
        pos.append(_cmul(*pos[-1], *pos[-1]))
        neg.append(_cmul(*neg[-1], *neg[-1]))
    top = pos[0]
    for f in pos[1:]:
        top = _cmul(*top, *f)
    rows = [co, top] + pos + neg
    flat = [part for pair in rows for part in pair]
    cols = []
    for i in range(0, len(flat), SUBLANES):
        blk = flat[i:i + SUBLANES]
        blk = blk + [blk[-1]] * (SUBLANES - len(blk))
        t = jnp.concatenate(blk, axis=0).T
        cols += [t[:, j:j + 1] for j in range(SUBLANES)]
    pair = lambda k: (cols[2 * k], cols[2 * k + 1])
    co_c, top_c = pair(0), pair(1)
    pos_c = [pair(2 + k) for k in range(n_bits)]
    neg_c = [pair(2 + n_bits + k) for k in range(n_bits)]

    slot = lax.broadcasted_iota(jnp.int32, (1, kw), 1) // cg

    def power(factors):
        acc = None
        for k, (fr, fi) in enumerate(factors):
            bit = ((slot >> k) & 1) == 1
            term = (jnp.where(bit, fr, 1.0), jnp.where(bit, fi, 0.0))
            acc = term if acc is None else _cmul(*acc, *term)
        return acc

    bt = _cmul(*co_c, btr_ref[...], bti_ref[...])
    pb = _cmul(*bt, *power(neg_c))
    qc = _cmul(ctr_ref[...], cti_ref[...], *power(pos_c))
    q1 = _cmul(*qc, *pos_c[0])
    bi = _cmul(*pb, *top_c)
    r2 = lax.broadcasted_iota(jnp.int32, (kw, kw), 0) // cg
    c2 = lax.broadcasted_iota(jnp.int32, (kw, kw), 1) // cg
    eye = jnp.where(lax.broadcasted_iota(jnp.int32, (n_p, n_p), 0)
                    == lax.broadcasted_iota(jnp.int32, (n_p, n_p), 1), 1.0, 0.0)
    for g in range(n_grp):
        rs = slice(g * n_p, (g + 1) * n_p)
        m_full = _ftdot(pb[0][rs], qc[0][rs]) - _ftdot(pb[1][rs], qc[1][rs])
        w1_ref[g, :, 0:kw] = jnp.where(r2 <= c2, m_full, 0.0).astype(w1_ref.dtype)
        w1_ref[g, :, kw:kw + n_p] = _ftdot(bi[0][rs], eye).astype(w1_ref.dtype)
        w1_ref[g, :, kw + n_p:kw + 2 * n_p] = _ftdot(bi[1][rs], eye).astype(w1_ref.dtype)
        co_ref[g, 0:n_p, :] = q1[0][rs].astype(co_ref.dtype)
        co_ref[g, n_p:2 * n_p, :] = (-q1[1][rs]).astype(co_ref.dtype)


def _s5t_prep(a_re, a_im, log_dt, b_re, b_im, c_re, c_im):
    g, p = a_re.shape
    cg = b_re.shape[-1]
    kw = T_CHUNK * cg
    row = lambda a: a.reshape(1, g * p)
    slots = lambda a: jnp.tile(a, (1, 1, T_CHUNK)).reshape(g * p, kw)
    return pl.pallas_call(
        functools.partial(_s5t_prep_body, tc=T_CHUNK, cg=cg, n_p=p),
        out_shape=[jax.ShapeDtypeStruct((g, kw, kw + 2 * p), BF16),
                   jax.ShapeDtypeStruct((g, 2 * p, kw), BF16)],
        compiler_params=pltpu.CompilerParams(vmem_limit_bytes=VMEM_LIMIT),
        name="s5t_prep",
    )(row(a_re), row(a_im), row(jnp.broadcast_to(log_dt[:, None], (g, p))),
      slots(b_re), slots(b_im), slots(c_re.transpose(0, 2, 1)), slots(c_im.transpose(0, 2, 1)))


def _s5t_tables_body(are_ref, aim_ref, ldt_ref, pin_re, pin_im, pout_re, pout_im, al_re, al_im,
                     *, tc, sc):
    lre = are_ref[...] * jnp.exp(ldt_ref[...]) * float(tc)
    lim = aim_ref[...] * jnp.exp(ldt_ref[...]) * float(tc)
    lane = lax.broadcasted_iota(jnp.int32, lre.shape, 1)
    sign = jnp.where((lane % LANES) < LANES // 2, -1.0, 1.0)
    n = lax.broadcasted_iota(jnp.int32, (sc, 1), 0).astype(F32) - float(sc // 2)
    m_out = jnp.exp(lre * n)
    m_in = jnp.exp(-(lre * n))
    ang = lim * n
    pin_re[...] = (m_in * jnp.cos(ang)).astype(pin_re.dtype)
    pin_im[...] = (sign * (-(m_in * jnp.sin(ang)))).astype(pin_im.dtype)
    pout_re[...] = m_out * jnp.cos(ang)
    pout_im[...] = sign * (m_out * jnp.sin(ang))
    m_al = jnp.exp(lre * float(sc))
    al_re[...] = m_al * jnp.cos(lim * float(sc))
    al_im[...] = sign * (m_al * jnp.sin(lim * float(sc)))


def _s5t_tables(a_re, a_im, log_dt):
    g, p = a_re.shape
    lanes = g * 2 * p
    spread = lambda a: jnp.broadcast_to(a[:, None, :], (g, 2, p)).reshape(1, lanes)
    tab16 = jax.ShapeDtypeStruct((S_CHUNK, lanes), BF16)
    tab32 = jax.ShapeDtypeStruct((S_CHUNK, lanes), F32)
    one = jax.ShapeDtypeStruct((1, lanes), F32)
    return pl.pallas_call(
        functools.partial(_s5t_tables_body, tc=T_CHUNK, sc=S_CHUNK),
        out_shape=[tab16, tab16, tab32, tab32, one, one], name="s5t_tables",
    )(spread(a_re), spread(a_im), spread(jnp.broadcast_to(log_dt[:, None], (g, p))))


def _atom_transpose(regs, atom, width):
    regs = list(regs)
    n = len(regs)
    d = n // 2
    while d:
        low = (atom & d) == 0
        for j in range(n):
            if j & d:
                continue
            a, b = regs[j], regs[j + d]
            regs[j] = jnp.where(low, a, pltpu.roll(b, d * width, axis=1))
            regs[j + d] = jnp.where(low, pltpu.roll(a, LANES - d * width, axis=1), b)
        d //= 2
    return regs


def _swap_halves(x):
    tiles = [pltpu.roll(x[:, t * LANES:(t + 1) * LANES], LANES // 2, axis=1)
             for t in range(x.shape[1] // LANES)]
    return jnp.concatenate(tiles, axis=1)


def _s5t_body(u_ref, w1_ref, co_ref, pin_re, pin_im, pout_re, pout_im, al_re, al_im,
              d_ref, wglu_ref, bglu_ref, y_ref,
              upk_ref, yloc_ref, sin_ref, xs_ref, carry_ref, udi_ref, *, rc, tc, sc, n_grp, cg):
    t = pl.program_id(1)
    kw = tc * cg
    n_q = u_ref.shape[0]
    atoms = LANES // cg

    @pl.when(t == 0)
    def _():
        carry_ref[...] = jnp.zeros(carry_ref.shape, F32)
        xs_ref[:, 0:SUBLANES, :] = jnp.zeros((n_grp, SUBLANES, LANES), F32)

    atom = lax.broadcasted_iota(jnp.int32, (rc, LANES), 1) // cg

    for q in range(n_q):
        for a in range(tc // atoms):
            planes = [u_ref[q, pl.ds(a * atoms + k, rc, stride=tc), :] for k in range(atoms)]
            for k in range(atoms):
                udi_ref[q, a * atoms + k] = planes[k]
            for r, tile in enumerate(_atom_transpose(planes, atom, cg)):
                upk_ref[q * atoms + r, :, a * LANES:(a + 1) * LANES] = tile.astype(BF16)

    for g in range(n_grp):
        res = jnp.dot(upk_ref[g], w1_ref[g], preferred_element_type=F32)
        yloc_ref[g] = res[:, :kw]
        sin_ref[g] = res[:, kw:]

    ri = lax.broadcasted_iota(jnp.int32, (sc, sc), 0)
    ci = lax.broadcasted_iota(jnp.int32, (sc, sc), 1)
    tri = jnp.where(ci <= ri, 1.0, 0.0).astype(BF16)
    carry = carry_ref[...]
    for c2 in range(rc // sc):
        rows = slice(c2 * sc, (c2 + 1) * sc)
        s = jnp.concatenate([sin_ref[g, rows, :] for g in range(n_grp)], axis=1)
        xin = pin_re[...] * s.astype(BF16) + pin_im[...] * _swap_halves(s).astype(BF16)
        tt = jnp.dot(tri, xin, preferred_element_type=F32) + carry
        x = pout_re[...] * tt + pout_im[...] * _swap_halves(tt)
        for g in range(n_grp):
            xs_ref[g, SUBLANES + c2 * sc:SUBLANES + (c2 + 1) * sc, :] = x[:, g * LANES:(g + 1) * LANES]
        last = tt[sc - 1:sc, :]
        carry = al_re[...] * last + al_im[...] * _swap_halves(last)
    carry_ref[...] = carry

    for g in range(n_grp):
        xprev = xs_ref[g, SUBLANES - 1:SUBLANES - 1 + rc, :].astype(BF16)
        yloc_ref[g] = yloc_ref[g] + jnp.dot(xprev, co_ref[g], preferred_element_type=F32)
    xs_ref[:, SUBLANES - 1:SUBLANES, :] = xs_ref[:, SUBLANES - 1 + rc:SUBLANES + rc, :]

    wglu = wglu_ref[...].astype(BF16)
    for a in range(tc // atoms):
        planes = [[None] * n_q for _ in range(atoms)]
        for q in range(n_q):
            tiles = [yloc_ref[q * atoms + r, :, a * LANES:(a + 1) * LANES] for r in range(atoms)]
            for k, tile in enumerate(_atom_transpose(tiles, atom, cg)):
                planes[k][q] = tile
        for k in range(atoms):
            i = a * atoms + k
            uu = jnp.concatenate([udi_ref[q, i] for q in range(n_q)], axis=1)
            y = jax.nn.gelu(jnp.concatenate(planes[k], axis=1) + d_ref[...] * uu)
            gate = jax.nn.sigmoid(jnp.dot(y.astype(BF16), wglu, preferred_element_type=F32)
                                  + bglu_ref[...])
            out = y * gate
            for q in range(n_q):
                y_ref[q, pl.ds(i, rc, stride=tc), :] = out[:, q * LANES:(q + 1) * LANES]


def _s5t(u4, w1, co, tabs, d, wglu, bglu, bsz, seq, *, rc):
    n_q, nt, _ = u4.shape
    n_grp, kw, wcols = w1.shape
    n_state = (wcols - kw) // 2
    cg = kw // T_CHUNK
    rows = rc * T_CHUNK
    n_t = seq // rows
    blk = pl.BlockSpec((n_q, rows, LANES), lambda b, t: (0, b * n_t + t, 0))
    c2 = lambda b, t: (0, 0)
    c3 = lambda b, t: (0, 0, 0)
    whole = lambda a: pl.BlockSpec(a.shape, c3 if a.ndim == 3 else c2)
    body = functools.partial(_s5t_body, rc=rc, tc=T_CHUNK, sc=S_CHUNK, n_grp=n_grp, cg=cg)
    return pl.pallas_call(
        body,
        grid=(bsz, n_t),
        in_specs=[blk] + [whole(a) for a in (w1, co, *tabs, d, wglu, bglu)],
        out_specs=blk,
        out_shape=jax.ShapeDtypeStruct((n_q, nt, LANES), F32),
        scratch_shapes=[
            pltpu.VMEM((n_grp, rc, kw), BF16),
            pltpu.VMEM((n_grp, rc, kw), F32),
            pltpu.VMEM((n_grp, rc, 2 * n_state), F32),
            pltpu.VMEM((n_grp, rc + SUBLANES, 2 * n_state), F32),
            pltpu.VMEM((1, n_grp * 2 * n_state), F32),
            pltpu.VMEM((n_q, T_CHUNK, rc, LANES), F32),
        ],
        compiler_params=_params("arbitrary", "arbitrary"),
        name="s5t",
    )(u4, w1, co, *tabs, d, wglu, bglu)


def _mem_kv_body(m_ref, g_ref, wk_ref, wv_ref, wq_ref, wo_ref, qk_ref, vw_ref, *, xh):
    bsz, d, _ = qk_ref.shape
    n_mem = m_ref.shape[0] // bsz
    xd = d // xh
    scale = xd ** -0.5
    mn = _rms(m_ref[...], g_ref[...]).astype(BF16)
    k = jnp.dot(mn, wk_ref[...].astype(BF16), preferred_element_type=F32).astype(BF16)
    v = jnp.dot(mn, wv_ref[...].astype(BF16), preferred_element_type=F32).astype(BF16)
    wq = wq_ref[...].astype(BF16)
    wo = wo_ref[...].astype(BF16)
    nt_dims = (((1,), (1,)), ((), ()))
    for b in range(bsz):
        rows = slice(b * n_mem, (b + 1) * n_mem)
        for h in range(xh):
            hl = slice(h * xd, (h + 1) * xd)
            ml = slice(h * n_mem, (h + 1) * n_mem)
            qk = lax.dot_general(wq[:, hl], k[rows, hl], nt_dims, preferred_element_type=F32)
            qk_ref[b, :, ml] = (qk * scale).astype(BF16)
            vw_ref[b, ml, :] = jnp.dot(v[rows, hl], wo[hl, :],
                                       preferred_element_type=F32).astype(BF16)


def _mem_kv(mem, g, wk, wv, wq, wo):
    bsz, n_mem, d = mem.shape
    return pl.pallas_call(
        functools.partial(_mem_kv_body, xh=X_HEADS),
        out_shape=[
            jax.ShapeDtypeStruct((bsz, d, X_HEADS * n_mem), BF16),
            jax.ShapeDtypeStruct((bsz, X_HEADS * n_mem, d), BF16),
        ],
        compiler_params=pltpu.CompilerParams(vmem_limit_bytes=VMEM_LIMIT),
        name="mem_kv",
    )(mem.reshape(bsz * n_mem, d), g, wk, wv, wq, wo)


def _mix_attn_body(x_ref, o_ref, y_ref, wo_ref, gx_ref, qk_ref, vw_ref, h_ref, *, xh, n_sub):
    dn_w = o_ref.shape[1]
    n_mem = qk_ref.shape[2] // xh
    sub = x_ref.shape[0] // n_sub
    mm = functools.partial(jnp.dot, preferred_element_type=F32)
    wo = wo_ref[...].astype(BF16)

    def rows_gen(r0):
        rows = slice(r0, r0 + sub)
        y = jnp.concatenate([y_ref[q, rows, :] for q in range(y_ref.shape[0])], axis=1)
        mix = mm(o_ref[rows, :].astype(BF16), wo[0:dn_w, :]) + mm(y.astype(BF16), wo[dn_w:, :])
        yield
        h1 = x_ref[rows, :] + mix
        s = mm(_rms(h1, gx_ref[...]).astype(BF16), qk_ref[0])
        yield
        parts = []
        for h in range(xh):
            sh = s[:, h * n_mem:(h + 1) * n_mem]
            e = jnp.exp(sh - jnp.max(sh, axis=-1, keepdims=True))
            parts.append((e / jnp.sum(e, axis=-1, keepdims=True)).astype(BF16))
        att = mm(jnp.concatenate(parts, axis=1), vw_ref[0])
        yield
        h_ref[rows, :] = h1 + att

    _interleave([rows_gen(i * sub) for i in range(n_sub)])


def _mix_attn(x2, o, y, w_out, gx, qk, vw, bsz, seq, *, tm):
    nt, d = x2.shape
    dn_w = o.shape[1]
    n_t = seq // tm
    row = lambda b, t: (b * n_t + t, 0)
    c2 = lambda b, t: (0, 0)
    per_batch = lambda a: pl.BlockSpec((1,) + a.shape[1:], lambda b, t: (b, 0, 0))
    body = functools.partial(_mix_attn_body, xh=X_HEADS, n_sub=2)
    return pl.pallas_call(
        body,
        grid=(bsz, n_t),
        in_specs=[
            pl.BlockSpec((tm, d), row),
            pl.BlockSpec((tm, dn_w), row),
            pl.BlockSpec((y.shape[0], tm, LANES), lambda b, t: (0, b * n_t + t, 0)),
            pl.BlockSpec(w_out.shape, c2, pipeline_mode=pl.Buffered(1)),
            pl.BlockSpec((1, d), c2),
            per_batch(qk),
            per_batch(vw),
        ],
        out_specs=pl.BlockSpec((tm, d), row),
        out_shape=jax.ShapeDtypeStruct((nt, d), F32),
        compiler_params=_params("arbitrary", "arbitrary"),
        name="mix_attn",
    )(x2, o, y, w_out, gx, qk, vw)


def _ffn_body(h_ref, gf_ref, wg_ref, wu_ref, wd_ref, gl_ref, out_ref, *, final, fc):
    h = h_ref[...]
    hn = _rms(h, gf_ref[...]).astype(BF16)
    dff = wg_ref.shape[1]

    def down(gate, up, c0):
        act = (gate * jax.nn.sigmoid(gate) * up).astype(BF16)
        return jnp.dot(act, wd_ref[c0:c0 + fc, :].astype(BF16), preferred_element_type=F32)

    h3 = h
    prev = None
    for c0 in range(0, dff, fc):
        gate = jnp.dot(hn, wg_ref[:, c0:c0 + fc].astype(BF16), preferred_element_type=F32)
        up = jnp.dot(hn, wu_ref[:, c0:c0 + fc].astype(BF16), preferred_element_type=F32)
        if prev is not None:
            h3 = h3 + down(*prev)
        prev = (gate, up, c0)
    h3 = h3 + down(*prev)
    out_ref[...] = _rms(h3, gl_ref[...]) if final else h3


def _ffn(h, gf, wg, wu, wd, gl, *, tm, final):
    nt, d = h.shape
    dff = wg.shape[1]
    c2 = lambda i: (0, 0)
    row = lambda i: (i, 0)
    return pl.pallas_call(
        functools.partial(_ffn_body, final=final, fc=2 * LANES),
        grid=(nt // tm,),
        in_specs=[
            pl.BlockSpec((tm, d), row),
            pl.BlockSpec((1, d), c2),
            pl.BlockSpec((d, dff), c2, pipeline_mode=pl.Buffered(1)),
            pl.BlockSpec((d, dff), c2, pipeline_mode=pl.Buffered(1)),
            pl.BlockSpec((dff, d), c2, pipeline_mode=pl.Buffered(1)),
            pl.BlockSpec((1, d), c2),
        ],
        out_specs=pl.BlockSpec((tm, d), row),
        out_shape=jax.ShapeDtypeStruct((nt, d), F32),
        compiler_params=_params("arbitrary"),
        name="ffn",
    )(h, gf, wg, wu, wd, gl)


def _layer(h2, mem, bsz, seq, norm_mix_g, w_in, conv_w, dn_a_log, dn_dt_bias, dn_norm_g,
           s5_a_re, s5_a_im, s5_b_re, s5_b_im, s5_c_re, s5_c_im, s5_d, s5_log_dt,
           s5_w_glu, s5_b_glu, w_out, norm_x_g, norm_mem_g, w_xq, w_xk, w_xv, w_xo):
    d = h2.shape[1]
    nh = dn_a_log.shape[0]
    dh = dn_norm_g.shape[0]
    dn_w = nh * dh
    s5_w = s5_a_re.shape[0] * s5_b_re.shape[-1]

    off_a = 4 * dn_w
    off_u = off_a + 2 * nh
    w_ab = jnp.pad(w_in[:, off_a:off_u], ((0, 0), (0, LANES - 2 * nh)))
    w_cat = jnp.concatenate([w_in[:, :off_a], w_in[:, off_u:], w_ab], axis=1).astype(BF16)
    alog = jnp.pad(dn_a_log, (0, LANES - nh)).reshape(1, LANES)
    dtb = jnp.pad(dn_dt_bias, (0, LANES - nh)).reshape(1, LANES)

    u, o = _mixer_in(h2, seq, norm_mix_g.reshape(1, d), w_cat, conv_w, alog, dtb,
                     dn_norm_g.reshape(1, dh), nh=nh, dh=dh, tb=256)

    w1, co = _s5t_prep(s5_a_re, s5_a_im, s5_log_dt, s5_b_re, s5_b_im, s5_c_re, s5_c_im)
    tabs = _s5t_tables(s5_a_re, s5_a_im, s5_log_dt)
    y = _s5t(u, w1, co, tabs, s5_d.reshape(1, s5_w), s5_w_glu, s5_b_glu.reshape(1, s5_w),
             bsz, seq, rc=128)

    qk_mem, vw_mem = _mem_kv(mem, norm_mem_g.reshape(1, d), w_xk, w_xv, w_xq, w_xo)
    return _mix_attn(h2, o, y, w_out, norm_x_g.reshape(1, d), qk_mem, vw_mem, bsz, seq, tm=1024)


def kernel(x, mem, norm_mix_g, w_in, conv_w, dn_a_log, dn_dt_bias, dn_norm_g, s5_a_re, s5_a_im,
           s5_b_re, s5_b_im, s5_c_re, s5_c_im, s5_d, s5_log_dt, s5_w_glu, s5_b_glu, w_out,
           norm_x_g, norm_mem_g, w_xq, w_xk, w_xv, w_xo, norm_ffn_g, w_gate, w_up, w_down,
           norm_final_g):
    bsz, seq, d = x.shape
    depth = w_in.shape[0]
    h = x.reshape(bsz * seq, d)
    for l in range(depth):
        h = _layer(h, mem, bsz, seq, norm_mix_g[l], w_in[l], conv_w[l], dn_a_log[l],
                   dn_dt_bias[l], dn_norm_g[l], s5_a_re[l], s5_a_im[l], s5_b_re[l], s5_b_im[l],
                   s5_c_re[l], s5_c_im[l], s5_d[l], s5_log_dt[l], s5_w_glu[l], s5_b_glu[l],
                   w_out[l], norm_x_g[l], norm_mem_g[l], w_xq[l], w_xk[l], w_xv[l], w_xo[l])
        h = _ffn(h, norm_ffn_g[l].reshape(1, d), w_gate[l], w_up[l], w_down[l],
                 norm_final_g.reshape(1, d), tm=1024, final=l == depth - 1)
    return h.reshape(bsz, seq, d)
```

```python
import functools

import jax
import jax.numpy as jnp
from jax import lax
from jax.experimental import pallas as pl
from jax.experimental.pallas import tpu as pltpu

F32 = jnp.float32
BF16 = jnp.bfloat16
EPS = 1e-6
HIGHEST = lax.Precision.HIGHEST

LANES = 128
SUBLANES = 8
DN_CHUNK = 64
S5_CHUNK = 64
X_HEADS = 4
VMEM_LIMIT = 58 * 1024 * 1024


def _bdot(a, b):
    return jnp.dot(a.astype(BF16), b.astype(BF16), preferred_element_type=F32)


def _fdot(a, b):
    return jnp.dot(a, b, precision=HIGHEST, preferred_element_type=F32)


def _rms(x, g):
    return x * lax.rsqrt(jnp.mean(x * x, axis=-1, keepdims=True) + EPS) * g


def _interleave(gens, stages_per_round=None):
    pending = [(gen, 1 if stages_per_round is None else stages_per_round[i])
               for i, gen in enumerate(gens)]
    while pending:
        for item in list(pending):
            gen, n = item
            try:
                for _ in range(n):
                    next(gen)
            except StopIteration:
                pending.remove(item)


def _params(*sem):
    return pltpu.CompilerParams(dimension_semantics=sem, vmem_limit_bytes=VMEM_LIMIT)


def _in_proj_stages(x_ref, g_ref, w_ref, cw_ref, alog_ref, dtb_ref,
                    q_ref, k_ref, kt_ref, v_ref, z_ref, u_ref, gt_ref,
                    buf_ref, *, tm, dn_w, nh, dh, cw_n):
    xnb = _rms(x_ref[...], g_ref[...]).astype(BF16)
    gw = 2 * dh

    def conv_silu(p, cols):
        outs = []
        for i in range(gw // LANES):
            c0 = cols.start + i * LANES
            s = c0 // LANES
            pc = p[:, i * LANES:(i + 1) * LANES]
            buf_ref[s, SUBLANES:SUBLANES + tm, :] = pc
            acc = cw_ref[cw_n - 1:cw_n, c0:c0 + LANES] * pc
            for j in range(cw_n - 1):
                off = SUBLANES - (cw_n - 1) + j
                acc = acc + cw_ref[j:j + 1, c0:c0 + LANES] * buf_ref[s, off:off + tm, :]
            buf_ref[s, 0:SUBLANES, :] = buf_ref[s, tm:tm + SUBLANES, :]
            outs.append(acc * jax.nn.sigmoid(acc))
        return jnp.concatenate(outs, axis=1)

    def l2n(a, scale):
        parts = []
        for i in range(gw // dh):
            ah = a[:, i * dh:(i + 1) * dh]
            parts.append(ah * (lax.rsqrt(jnp.sum(ah * ah, -1, keepdims=True) + EPS) * scale))
        return jnp.concatenate(parts, axis=1)

    def ep_q(p, c0):
        q_ref[:, c0:c0 + gw] = l2n(conv_silu(p, slice(c0, c0 + gw)), dh ** -0.5)

    def ep_k(p, c0):
        kn = l2n(conv_silu(p, slice(dn_w + c0, dn_w + c0 + gw)), 1.0)
        k_ref[:, c0:c0 + gw] = kn
        kt_ref[c0:c0 + gw, :] = kn.T

    def ep_v(p, c0):
        v_ref[:, c0:c0 + gw] = conv_silu(p, slice(2 * dn_w + c0, 2 * dn_w + c0 + gw))

    def ep_z(p, c0):
        z_ref[:, c0:c0 + gw] = p

    def ep_u(p, c0):
        for j in range(gw // LANES):
            u_ref[c0 // LANES + j] = p[:, j * LANES:(j + 1) * LANES]

    def ep_gates(ab, c0):
        lane = lax.broadcasted_iota(jnp.int32, ab.shape, 1)
        g = -jnp.exp(alog_ref[...]) * jnp.logaddexp(ab + dtb_ref[...], 0.0)
        gates = jnp.where(lane < nh, g, jax.nn.sigmoid(ab))
        gt_ref[...] = gates.T[0:SUBLANES, :]

    groups = []
    for i, ep in enumerate((ep_q, ep_k, ep_v, ep_z, ep_u)):
        groups += [(ep, i * dn_w + c0, gw, c0) for c0 in range(0, dn_w, gw)]
    groups.append((ep_gates, 5 * dn_w, LANES, 0))

    prev = None
    for ep, w0, width, c0 in groups:
        p = jnp.dot(xnb, w_ref[:, w0:w0 + width], preferred_element_type=F32)
        if prev is not None:
            prev[0](prev[1], prev[2])
        prev = (ep, p, c0)
        yield
    prev[0](prev[1], prev[2])


def _delta_stages(q_ref, k_ref, kt_ref, v_ref, z_ref, gt_ref, ng_ref,
                  o_ref, s_ref, u_sc, w_sc, qg_sc, attn_sc, kdt_sc, gtot_sc, wr, rd,
                  *, tb, nh, dh, ch):
    nc = tb // ch
    heads = range(nh)
    hls = [slice(h * dh, (h + 1) * dh) for h in heads]
    mm = functools.partial(jnp.dot, preferred_element_type=F32)

    def phase_b():
        gtot_b = gtot_sc[rd]
        s_all = [s_ref[h] for h in heads]
        for cidx in range(nc):
            rows = slice(cidx * ch, (cidx + 1) * ch)
            sb = [s_all[h].astype(BF16) for h in heads]
            ws = [mm(w_sc[rd, h, rows, :], sb[h]) for h in heads]
            qs = [mm(qg_sc[rd, h, rows, :], sb[h]) for h in heads]
            yield
            vb = [(u_sc[rd, h, rows, :] - ws[h]).astype(BF16) for h in heads]
            av = [mm(attn_sc[rd, h, rows, rows], vb[h]) for h in heads]
            kv = [mm(kdt_sc[rd, h, :, rows], vb[h]) for h in heads]
            yield
            outs = []
            for h in heads:
                a_last = jnp.exp(gtot_b[h:h + 1, cidx * ch:cidx * ch + 1])
                s_all[h] = s_all[h] * a_last + kv[h]
                o = qs[h] + av[h]
                o = o * lax.rsqrt(jnp.mean(o * o, -1, keepdims=True) + EPS)
                zh = z_ref[rows, hls[h]]
                outs.append(o * ng_ref[...] * (zh * jax.nn.sigmoid(zh)))
            o_ref[rows, :] = jnp.concatenate(outs, axis=1)
        for h in heads:
            s_ref[h] = s_all[h]

    r = lax.broadcasted_iota(jnp.int32, (tb, tb), 0)
    c = lax.broadcasted_iota(jnp.int32, (tb, tb), 1)
    same = (r // ch) == (c // ch)
    causal_bd = same & (c <= r)
    strict_bd = same & (c < r)
    gt = gt_ref[...]
    gcum_t = _fdot(gt, jnp.where(same & (r <= c), 1.0, 0.0))
    gtot_t = _fdot(gt, jnp.where(same, 1.0, 0.0))
    gcum = gcum_t.T
    gates = gt.T

    def pack(x):
        acc = x[0:ch, :]
        for i in range(1, nc):
            acc = acc + x[i * ch:(i + 1) * ch, :]
        return acc

    def unpack(x):
        return jnp.where(same, jnp.concatenate([x] * nc, axis=0), 0.0).astype(BF16)

    n_sq = max(1, (ch - 1).bit_length() - 1)

    def phase_a():
        gcol = [gcum[:, h:h + 1] for h in heads]
        grow = [gcum_t[h:h + 1, :] for h in heads]
        beta = [gates[:, nh + h:nh + h + 1] for h in heads]
        kb = [k_ref[:, hls[h]] * beta[h] for h in heads]
        gram = [mm(jnp.concatenate([kb[h], q_ref[:, hls[h]]], axis=0).astype(BF16),
                   kt_ref[hls[h], :].astype(BF16)) for h in heads]
        yield
        ppk, apk = [], []
        for h in heads:
            decay = jnp.exp(jnp.where(causal_bd, gcol[h] - grow[h], -jnp.inf))
            p_bd = -jnp.where(strict_bd, gram[h][:tb] * decay, 0.0)
            attn_sc[wr, h] = (gram[h][tb:] * decay).astype(BF16)
            pk = pack(p_bd)
            apk.append(pk)
            ppk.append(mm(pk.astype(BF16), p_bd.astype(BF16)))
        yield
        for j in range(1, n_sq + 1):
            res = []
            for h in heads:
                lhs = jnp.concatenate([apk[h], ppk[h]], axis=0) if j < n_sq else apk[h]
                res.append(mm(lhs.astype(BF16), unpack(ppk[h])))
            yield
            for h in heads:
                apk[h] = apk[h] + ppk[h] + res[h][:ch]
                if j < n_sq:
                    ppk[h] = res[h][ch:]
        egc = [jnp.exp(gcol[h]) for h in heads]
        rhs = [jnp.concatenate([v_ref[:, hls[h]] * beta[h], kb[h] * egc[h]], axis=1) for h in heads]
        prod = [mm(unpack(apk[h]), rhs[h].astype(BF16)) for h in heads]
        yield
        for h in heads:
            uw = rhs[h] + prod[h]
            u_sc[wr, h] = uw[:, :dh]
            w_sc[wr, h] = uw[:, dh:].astype(BF16)
            qg_sc[wr, h] = (q_ref[:, hls[h]] * egc[h]).astype(BF16)
            kdt_sc[wr, h] = (kt_ref[hls[h], :] * jnp.exp(gtot_t[h:h + 1, :] - grow[h])).astype(BF16)
        gtot_sc[wr] = gtot_t

    return phase_b(), phase_a()


def _mixer_in_body(x_ref, g_ref, w_ref, cw_ref, alog_ref, dtb_ref, ng_ref,
                   u_ref, o_ref,
                   buf_ref, q_sc, k_sc, kt_sc, v_sc, z_sc, gt_sc,
                   s_ref, u_sc, w_sc, qg_sc, attn_sc, kdt_sc, gtot_sc,
                   *, tb, n_t, nh, dh, cw_n):
    s = pl.program_id(0)
    dn_w = nh * dh

    @pl.when(s == 0)
    def _():
        for ref in (buf_ref, q_sc, k_sc, kt_sc, v_sc, z_sc, gt_sc,
                    s_ref, u_sc, w_sc, qg_sc, attn_sc, kdt_sc, gtot_sc):
            ref[...] = jnp.zeros(ref.shape, ref.dtype)

    @pl.when(s % n_t == 0)
    def _():
        buf_ref[:, 0:SUBLANES, :] = jnp.zeros((3 * dn_w // LANES, SUBLANES, LANES), F32)

    @pl.when((s - 2) % n_t == 0)
    def _():
        s_ref[...] = jnp.zeros(s_ref.shape, F32)

    cur = s % 2
    prv = 1 - cur
    zw = s % 3
    zr = (s + 1) % 3
    proj = _in_proj_stages(
        x_ref, g_ref, w_ref, cw_ref, alog_ref, dtb_ref,
        q_sc.at[cur], k_sc.at[cur], kt_sc.at[cur], v_sc.at[cur], z_sc.at[zw], u_ref, gt_sc.at[cur],
        buf_ref, tm=tb, dn_w=dn_w, nh=nh, dh=dh, cw_n=cw_n)
    phase_b, phase_a = _delta_stages(
        q_sc.at[prv], k_sc.at[prv], kt_sc.at[prv], v_sc.at[prv], z_sc.at[zr], gt_sc.at[prv], ng_ref,
        o_ref, s_ref, u_sc, w_sc, qg_sc, attn_sc, kdt_sc, gtot_sc, cur, prv,
        tb=tb, nh=nh, dh=dh, ch=DN_CHUNK)
    _interleave([proj, phase_b, phase_a])


def _mixer_in(x2, seq, g, w_cat, conv_w, alog, dtb, ng, *, nh, dh, tb):
    nt, d = x2.shape
    dn_w = nh * dh
    n_blk = nt // tb
    cw_n = conv_w.shape[0]
    const = lambda s: (0, 0)
    in_blk = lambda s: (jnp.minimum(s, n_blk - 1), 0)
    out_blk = lambda s: (jnp.maximum(s - 2, 0), 0)
    body = functools.partial(_mixer_in_body, tb=tb, n_t=seq // tb, nh=nh, dh=dh, cw_n=cw_n)
    two = lambda *shape: pltpu.VMEM((2,) + shape, F32)
    return pl.pallas_call(
        body,
        grid=(n_blk + 2,),
        in_specs=[pl.BlockSpec((tb, d), in_blk)]
        + [pl.BlockSpec(a.shape, const) for a in (g, w_cat, conv_w, alog, dtb, ng)],
        out_specs=[pl.BlockSpec((dn_w // LANES, tb, LANES),
                                lambda s: (0, jnp.minimum(s, n_blk - 1), 0)),
                   pl.BlockSpec((tb, dn_w), out_blk)],
        out_shape=[jax.ShapeDtypeStruct((dn_w // LANES, nt, LANES), F32),
                   jax.ShapeDtypeStruct((nt, dn_w), F32)],
        scratch_shapes=[
            pltpu.VMEM((3 * dn_w // LANES, tb + SUBLANES, LANES), F32),
            two(tb, dn_w), two(tb, dn_w), two(dn_w, tb), two(tb, dn_w),
            pltpu.VMEM((3, tb, dn_w), F32),
            two(SUBLANES, tb),
            pltpu.VMEM((nh, dh, dh), F32),
            pltpu.VMEM((2, nh, tb, dh), F32),
            pltpu.VMEM((2, nh, tb, dh), BF16),
            pltpu.VMEM((2, nh, tb, dh), BF16),
            pltpu.VMEM((2, nh, tb, tb), BF16),
            pltpu.VMEM((2, nh, dh, tb), BF16),
            pltpu.VMEM((2, SUBLANES, tb), F32),
        ],
        compiler_params=_params("arbitrary"),
        name="mixer_in",
    )(x2, g, w_cat, conv_w, alog, dtb, ng)


def _s5_prep_body(are_ref, aim_ref, ldt_ref, btr_ref, bti_ref, ctr_ref, cti_ref,
                  wb_ref, wc_ref, pin_re, pin_im, pout_re, pout_im, al_re, al_im,
                  *, chunk, n_half, n_state, cg):
    a_re = are_ref[...]
    a_im = aim_ref[...]
    dt = jnp.exp(ldt_ref[...])
    lre = a_re * dt
    lim = a_im * dt
    mag = jnp.exp(lre)
    nr = mag * jnp.cos(lim) - 1.0
    ni = mag * jnp.sin(lim)
    den = a_re * a_re + a_im * a_im
    cor = (nr * a_re + ni * a_im) / den
    coi = (ni * a_re - nr * a_im) / den
    bb_re = cor * btr_ref[...] - coi * bti_ref[...]
    bb_im = cor * bti_ref[...] + coi * btr_ref[...]

    n = lax.broadcasted_iota(jnp.int32, (chunk, 1), 0).astype(F32) - float(chunk // 2)
    m_out = jnp.exp(lre * n)
    m_in = jnp.exp(-(lre * n))
    ang = lim * n
    tabs = (m_in * jnp.cos(ang), -(m_in * jnp.sin(ang)), m_out * jnp.cos(ang), m_out * jnp.sin(ang))
    m_al = jnp.exp(lre * float(chunk))
    alr = m_al * jnp.cos(lim * float(chunk))
    ali = m_al * jnp.sin(lim * float(chunk))

    gp = a_re.shape[1]
    sw = gp // n_half
    gh = sw // n_state
    hw = gh * cg
    r = lax.broadcasted_iota(jnp.int32, (hw, sw), 0)
    c = lax.broadcasted_iota(jnp.int32, (hw, sw), 1)
    on_b = (r // cg) == (c // n_state)
    r2 = lax.broadcasted_iota(jnp.int32, (sw, hw), 0)
    c2 = lax.broadcasted_iota(jnp.int32, (sw, hw), 1)
    on_c = (r2 // n_state) == (c2 // cg)
    for hf in range(n_half):
        ls = slice(hf * sw, (hf + 1) * sw)
        for ref, tab in zip((pin_re, pin_im, pout_re, pout_im), tabs):
            ref[hf] = tab[:, ls].astype(ref.dtype)
        al_re[hf] = alr[:, ls]
        al_im[hf] = ali[:, ls]
        tile_b = lambda x: jnp.concatenate([x[:, ls]] * gh, axis=0)
        wb_ref[hf, :, :sw] = jnp.where(on_b, tile_b(bb_re), 0.0).astype(wb_ref.dtype)
        wb_ref[hf, :, sw:] = jnp.where(on_b, tile_b(bb_im), 0.0).astype(wb_ref.dtype)
        cs = slice(hf * hw, (hf + 1) * hw)
        tile_c = lambda x: jnp.concatenate([x[:, cs]] * gh, axis=0)
        wc_ref[hf, :sw, :] = jnp.where(on_c, tile_c(ctr_ref[...]), 0.0).astype(wc_ref.dtype)
        wc_ref[hf, sw:, :] = jnp.where(on_c, -tile_c(cti_ref[...]), 0.0).astype(wc_ref.dtype)


def _s5_prep(a_re, a_im, log_dt, b_re, b_im, c_re, c_im, *, chunk, n_half):
    g, p = a_re.shape
    cg = b_re.shape[-1]
    gp = g * p
    sw = gp // n_half
    hw = g * cg // n_half
    row = lambda a: a.reshape(1, gp)
    tab = jax.ShapeDtypeStruct((n_half, chunk, sw), BF16)
    one = jax.ShapeDtypeStruct((n_half, 1, sw), F32)
    body = functools.partial(_s5_prep_body, chunk=chunk, n_half=n_half, n_state=p, cg=cg)
    return pl.pallas_call(
        body,
        out_shape=[jax.ShapeDtypeStruct((n_half, hw, 2 * sw), BF16),
                   jax.ShapeDtypeStruct((n_half, 2 * sw, hw), BF16),
                   tab, tab, tab, tab, one, one],
        compiler_params=pltpu.CompilerParams(vmem_limit_bytes=VMEM_LIMIT),
        name="s5_prep",
    )(row(a_re), row(a_im), row(jnp.broadcast_to(log_dt[:, None], (g, p))),
      b_re.reshape(gp, cg).T, b_im.reshape(gp, cg).T,
      c_re.transpose(2, 0, 1).reshape(p, g * cg), c_im.transpose(2, 0, 1).reshape(p, g * cg))


def _s5_body(u_ref, wb_ref, wc_ref, pin_re, pin_im, pout_re, pout_im, al_re, al_im,
             d_ref, wglu_ref, bglu_ref, y_ref, carry_ref, xs_ref, *, ts, chunk, n_half):
    t = pl.program_id(1)

    @pl.when(t == 0)
    def _():
        carry_ref[...] = jnp.zeros(carry_ref.shape, F32)

    u = u_ref[...]
    ub = u.astype(BF16)
    hw = u.shape[1] // n_half
    sw = pin_re.shape[2]
    ri = lax.broadcasted_iota(jnp.int32, (chunk, chunk), 0)
    ci = lax.broadcasted_iota(jnp.int32, (chunk, chunk), 1)
    tri = jnp.where(ci <= ri, 1.0, 0.0).astype(BF16)
    ys = [None] * n_half

    def half(hf):
        bu = jnp.dot(ub[:, hf * hw:(hf + 1) * hw], wb_ref[hf], preferred_element_type=F32)
        yield
        pr, pi = pin_re[hf], pin_im[hf]
        qr, qi = pout_re[hf], pout_im[hf]
        ar, ai = al_re[hf], al_im[hf]
        carry = [carry_ref[2 * hf:2 * hf + 1, :], carry_ref[2 * hf + 1:2 * hf + 2, :]]

        def prefix(cidx):
            rows = slice(cidx * chunk, (cidx + 1) * chunk)
            bre = bu[rows, :sw].astype(BF16)
            bim = bu[rows, sw:].astype(BF16)
            sre = pr * bre - pi * bim
            sim = pr * bim + pi * bre
            return jnp.dot(tri, jnp.concatenate([sre, sim], axis=1), preferred_element_type=F32)

        def finish(cidx, cs):
            rows = slice(cidx * chunk, (cidx + 1) * chunk)
            tre = cs[:, :sw] + carry[0]
            tim = cs[:, sw:] + carry[1]
            trb = tre.astype(BF16)
            tib = tim.astype(BF16)
            xs_ref[hf, rows, :sw] = qr * trb - qi * tib
            xs_ref[hf, rows, sw:] = qr * tib + qi * trb
            lre = tre[chunk - 1:chunk, :]
            lim = tim[chunk - 1:chunk, :]
            carry[0] = ar * lre - ai * lim
            carry[1] = ar * lim + ai * lre

        prev = None
        for cidx in range(ts // chunk):
            cs = prefix(cidx)
            if prev is not None:
                finish(*prev)
            prev = (cidx, cs)
            yield
        finish(*prev)
        carry_ref[2 * hf:2 * hf + 1, :] = carry[0]
        carry_ref[2 * hf + 1:2 * hf + 2, :] = carry[1]
        yield
        ys[hf] = jnp.dot(xs_ref[hf], wc_ref[hf], preferred_element_type=F32)

    _interleave([half(hf) for hf in range(n_half)])
    y = jnp.concatenate(ys, axis=1) + d_ref[...] * u
    y = jax.nn.gelu(y)
    gate = jax.nn.sigmoid(jnp.dot(y.astype(BF16), wglu_ref[...].astype(BF16),
                                  preferred_element_type=F32)
                          + bglu_ref[...])
    y_ref[...] = y * gate


def _s5(u, wb, wc, tabs, d, wglu, bglu, bsz, seq, *, ts, chunk):
    nt, width = u.shape
    n_half = wb.shape[0]
    sw2 = wb.shape[2]
    sw = sw2 // 2
    n_t = seq // ts
    row = lambda b, t: (b * n_t + t, 0)
    c2 = lambda b, t: (0, 0)
    c3 = lambda b, t: (0, 0, 0)
    tab = pl.BlockSpec((n_half, chunk, sw), c3)
    one = pl.BlockSpec((n_half, 1, sw), c3)
    body = functools.partial(_s5_body, ts=ts, chunk=chunk, n_half=n_half)
    return pl.pallas_call(
        body,
        grid=(bsz, n_t),
        in_specs=[
            pl.BlockSpec((ts, width), row),
            pl.BlockSpec(wb.shape, c3),
            pl.BlockSpec(wc.shape, c3),
            tab, tab, tab, tab, one, one,
            pl.BlockSpec((1, width), c2),
            pl.BlockSpec(wglu.shape, c2),
            pl.BlockSpec((1, width), c2),
        ],
        out_specs=pl.BlockSpec((ts, width), row),
        out_shape=jax.ShapeDtypeStruct((nt, width), F32),
        scratch_shapes=[pltpu.VMEM((2 * n_half, sw), F32), pltpu.VMEM((n_half, ts, sw2), BF16)],
        compiler_params=_params("arbitrary", "arbitrary"),
        name="s5",
    )(u, wb, wc, *tabs, d, wglu, bglu)


T_CHUNK = 16
S_CHUNK = 16


def _ftdot(a, b):
    return lax.dot_general(a.astype(BF16), b.astype(BF16), (((0,), (0,)), ((), ())),
                           preferred_element_type=F32)


def _cmul(xr, xi, yr, yi):
    return xr * yr - xi * yi, xr * yi + xi * yr


def _s5t_prep_body(are_ref, aim_ref, ldt_ref, btr_ref, bti_ref, ctr_ref, cti_ref, w1_ref, co_ref,
                   *, tc, cg, n_p):
    n_grp = w1_ref.shape[0]
    kw = tc * cg
    a_re = are_ref[...]
    a_im = aim_ref[...]
    dt = jnp.exp(ldt_ref[...])
    lre = a_re * dt
    lim = a_im * dt
    mag = jnp.exp(lre)
    ab = (mag * jnp.cos(lim), mag * jnp.sin(lim))
    nr, ni = ab[0] - 1.0, ab[1]
    den = a_re * a_re + a_im * a_im
    co = ((nr * a_re + ni * a_im) / den, (ni * a_re - nr * a_im) / den)
    inv_mag2 = jnp.exp(-2.0 * lre)
    ai = (ab[0] * inv_mag2, -ab[1] * inv_mag2)
    n_bits = tc.bit_length() - 1
    pos, neg = [ab], [ai]
    for _ in range(n_bits - 1):
        pos.append(_cmul(*pos[-1], *pos[-1]))
        neg.append(_cmul(*neg[-1], *neg[-1]))
    top = pos[0]
    for f in pos[1:]:
        top = _cmul(*top, *f)
    rows = [co, top] + pos + neg
    flat = [part for pair in rows for part in pair]
    cols = []
    for i in range(0, len(flat), SUBLANES):
        blk = flat[i:i + SUBLANES]
        blk = blk + [blk[-1]] * (SUBLANES - len(blk))
        t = jnp.concatenate(blk, axis=0).T
        cols += [t[:, j:j + 1] for j in range(SUBLANES)]
    pair = lambda k: (cols[2 * k], cols[2 * k + 1])
    co_c, top_c = pair(0), pair(1)
    pos_c = [pair(2 + k) for k in range(n_bits)]
    neg_c = [pair(2 + n_bits + k) for k in range(n_bits)]

    slot = lax.broadcasted_iota(jnp.int32, (1, kw), 1) // cg

    def power(factors):
        acc = None
        for k, (fr, fi) in enumerate(factors):
            bit = ((slot >> k) & 1) == 1
            term = (jnp.where(bit, fr, 1.0), jnp.where(bit, fi, 0.0))
            acc = term if acc is None else _cmul(*acc, *term)
        return acc

    place = jnp.where(lax.broadcasted_iota(jnp.int32, (cg, kw), 0)
                      == lax.broadcasted_iota(jnp.int32, (cg, kw), 1) % cg, 1.0, 0.0)
    rep = lambda ref: _fdot(ref[...], place)
    bt = _cmul(*co_c, rep(btr_ref), rep(bti_ref))
    pb = _cmul(*bt, *power(neg_c))
    qc = _cmul(rep(ctr_ref), rep(cti_ref), *power(pos_c))
    q1 = _cmul(*qc, *pos_c[0])
    bi = _cmul(*pb, *top_c)
    r2 = lax.broadcasted_iota(jnp.int32, (kw, kw), 0) // cg
    c2 = lax.broadcasted_iota(jnp.int32, (kw, kw), 1) // cg
    eye = jnp.where(lax.broadcasted_iota(jnp.int32, (n_p, n_p), 0)
                    == lax.broadcasted_iota(jnp.int32, (n_p, n_p), 1), 1.0, 0.0)
    for g in range(n_grp):
        rs = slice(g * n_p, (g + 1) * n_p)
        m_full = _ftdot(pb[0][rs], qc[0][rs]) - _ftdot(pb[1][rs], qc[1][rs])
        w1_ref[g, :, 0:kw] = jnp.where(r2 <= c2, m_full, 0.0).astype(w1_ref.dtype)
        w1_ref[g, :, kw:kw + n_p] = _ftdot(bi[0][rs], eye).astype(w1_ref.dtype)
        w1_ref[g, :, kw + n_p:kw + 2 * n_p] = _ftdot(bi[1][rs], eye).astype(w1_ref.dtype)
        co_ref[g, 0:n_p, :] = q1[0][rs].astype(co_ref.dtype)
        co_ref[g, n_p:2 * n_p, :] = (-q1[1][rs]).astype(co_ref.dtype)


def _s5t_prep(a_re, a_im, log_dt, b_re, b_im, c_re, c_im):
    g, p = a_re.shape
    cg = b_re.shape[-1]
    kw = T_CHUNK * cg
    row = lambda a: a.reshape(1, g * p)
    slots = lambda a: a.reshape(g * p, cg)
    return pl.pallas_call(
        functools.partial(_s5t_prep_body, tc=T_CHUNK, cg=cg, n_p=p),
        out_shape=[jax.ShapeDtypeStruct((g, kw, kw + 2 * p), BF16),
                   jax.ShapeDtypeStruct((g, 2 * p, kw), BF16)],
        compiler_params=pltpu.CompilerParams(vmem_limit_bytes=VMEM_LIMIT),
        name="s5t_prep",
    )(row(a_re), row(a_im), row(jnp.broadcast_to(log_dt[:, None], (g, p))),
      slots(b_re), slots(b_im), slots(c_re.transpose(0, 2, 1)), slots(c_im.transpose(0, 2, 1)))


def _s5t_tables_body(are_ref, aim_ref, ldt_ref, pin_re, pin_im, pout_re, pout_im, al_re, al_im,
                     *, tc, sc):
    lre = are_ref[...] * jnp.exp(ldt_ref[...]) * float(tc)
    lim = aim_ref[...] * jnp.exp(ldt_ref[...]) * float(tc)
    lane = lax.broadcasted_iota(jnp.int32, lre.shape, 1)
    sign = jnp.where((lane % LANES) < LANES // 2, -1.0, 1.0)
    n = lax.broadcasted_iota(jnp.int32, (sc, 1), 0).astype(F32) - float(sc // 2)
    m_out = jnp.exp(lre * n)
    m_in = jnp.exp(-(lre * n))
    ang = lim * n
    pin_re[...] = (m_in * jnp.cos(ang)).astype(pin_re.dtype)
    pin_im[...] = (sign * (-(m_in * jnp.sin(ang)))).astype(pin_im.dtype)
    pout_re[...] = m_out * jnp.cos(ang)
    pout_im[...] = sign * (m_out * jnp.sin(ang))
    m_al = jnp.exp(lre * float(sc))
    al_re[...] = m_al * jnp.cos(lim * float(sc))
    al_im[...] = sign * (m_al * jnp.sin(lim * float(sc)))


def _s5t_tables(a_re, a_im, log_dt):
    g, p = a_re.shape
    lanes = g * 2 * p
    spread = lambda a: jnp.broadcast_to(a[:, None, :], (g, 2, p)).reshape(1, lanes)
    tab16 = jax.ShapeDtypeStruct((S_CHUNK, lanes), BF16)
    tab32 = jax.ShapeDtypeStruct((S_CHUNK, lanes), F32)
    one = jax.ShapeDtypeStruct((1, lanes), F32)
    return pl.pallas_call(
        functools.partial(_s5t_tables_body, tc=T_CHUNK, sc=S_CHUNK),
        out_shape=[tab16, tab16, tab32, tab32, one, one], name="s5t_tables",
    )(spread(a_re), spread(a_im), spread(jnp.broadcast_to(log_dt[:, None], (g, p))))


def _atom_transpose(regs, atom, width):
    regs = list(regs)
    n = len(regs)
    d = n // 2
    while d:
        low = (atom & d) == 0
        for j in range(n):
            if j & d:
                continue
            a, b = regs[j], regs[j + d]
            regs[j] = jnp.where(low, a, pltpu.roll(b, d * width, axis=1))
            regs[j + d] = jnp.where(low, pltpu.roll(a, LANES - d * width, axis=1), b)
        d //= 2
    return regs


def _swap_halves(x):
    tiles = [pltpu.roll(x[:, t * LANES:(t + 1) * LANES], LANES // 2, axis=1)
             for t in range(x.shape[1] // LANES)]
    return jnp.concatenate(tiles, axis=1)


def _s5t_body(u_ref, w1_ref, co_ref, pin_re, pin_im, pout_re, pout_im, al_re, al_im,
              d_ref, wglu_ref, bglu_ref, y_ref,
              upk_ref, yloc_ref, sin_ref, xs_ref, carry_ref, udi_ref, *, rc, tc, sc, n_grp, cg):
    t = pl.program_id(1)
    kw = tc * cg
    n_q = u_ref.shape[0]
    atoms = LANES // cg

    @pl.when(t == 0)
    def _():
        carry_ref[...] = jnp.zeros(carry_ref.shape, F32)
        xs_ref[:, 0:SUBLANES, :] = jnp.zeros((n_grp, SUBLANES, LANES), F32)

    atom = lax.broadcasted_iota(jnp.int32, (rc, LANES), 1) // cg

    n_a = tc // atoms

    def pack():
        for q in range(n_q):
            for a in range(n_a):
                planes = [u_ref[q, pl.ds(a * atoms + k, rc, stride=tc), :] for k in range(atoms)]
                for k in range(atoms):
                    udi_ref[q, a * atoms + k] = planes[k]
                for r, tile in enumerate(_atom_transpose(planes, atom, cg)):
                    upk_ref[q * atoms + r, :, a * LANES:(a + 1) * LANES] = tile.astype(BF16)
                yield

    per_stage = atoms // n_a

    def local():
        for _ in range(n_a):
            yield
        for g in range(n_grp):
            res = jnp.dot(upk_ref[g], w1_ref[g], preferred_element_type=F32)
            yloc_ref[g] = res[:, :kw]
            sin_ref[g] = res[:, kw:]
            if g % per_stage == per_stage - 1:
                yield

    _interleave([pack(), local()])

    ri = lax.broadcasted_iota(jnp.int32, (sc, sc), 0)
    ci = lax.broadcasted_iota(jnp.int32, (sc, sc), 1)
    tri = jnp.where(ci <= ri, 1.0, 0.0).astype(BF16)
    n_c2 = rc // sc
    ss = [jnp.concatenate([sin_ref[g, c2 * sc:(c2 + 1) * sc, :] for g in range(n_grp)], axis=1)
          for c2 in range(n_c2)]
    ss_sw = [_swap_halves(s) for s in ss]
    xin = [pin_re[...] * ss[c2].astype(BF16) + pin_im[...] * ss_sw[c2].astype(BF16)
           for c2 in range(n_c2)]
    psum = [jnp.dot(tri, x, preferred_element_type=F32) for x in xin]
    psum_sw = [_swap_halves(p) for p in psum]
    carry = carry_ref[...]
    carry_sw = _swap_halves(carry)
    carries = []
    for c2 in range(rc // sc):
        carries.append((carry, carry_sw))
        last = psum[c2][sc - 1:sc, :] + carry
        last_sw = psum_sw[c2][sc - 1:sc, :] + carry_sw
        carry = al_re[...] * last + al_im[...] * last_sw
        carry_sw = al_re[...] * last_sw - al_im[...] * last
    carry_ref[...] = carry
    for c2 in range(rc // sc):
        x = (pout_re[...] * (psum[c2] + carries[c2][0])
             + pout_im[...] * (psum_sw[c2] + carries[c2][1]))
        for g in range(n_grp):
            xs_ref[g, SUBLANES + c2 * sc:SUBLANES + (c2 + 1) * sc, :] = x[:, g * LANES:(g + 1) * LANES]

    for g in range(n_grp):
        xprev = xs_ref[g, SUBLANES - 1:SUBLANES - 1 + rc, :].astype(BF16)
        yloc_ref[g] = yloc_ref[g] + jnp.dot(xprev, co_ref[g], preferred_element_type=F32)
    xs_ref[:, SUBLANES - 1:SUBLANES, :] = xs_ref[:, SUBLANES - 1 + rc:SUBLANES + rc, :]

    wglu = wglu_ref[...].astype(BF16)

    def unpack(a):
        for _ in range(a * n_q):
            yield
        planes = [[None] * n_q for _ in range(atoms)]
        for q in range(n_q):
            tiles = [yloc_ref[q * atoms + r, :, a * LANES:(a + 1) * LANES] for r in range(atoms)]
            for k, tile in enumerate(_atom_transpose(tiles, atom, cg)):
                planes[k][q] = tile
            yield
        for k in range(atoms):
            i = a * atoms + k
            uu = jnp.concatenate([udi_ref[q, i] for q in range(n_q)], axis=1)
            y = jax.nn.gelu(jnp.concatenate(planes[k], axis=1) + d_ref[...] * uu)
            gate = jax.nn.sigmoid(jnp.dot(y.astype(BF16), wglu, preferred_element_type=F32)
                                  + bglu_ref[...])
            out = y * gate
            for q in range(n_q):
                y_ref[q, pl.ds(i, rc, stride=tc), :] = out[:, q * LANES:(q + 1) * LANES]
            if k % 2 == 1:
                yield

    _interleave([unpack(a) for a in range(n_a)])


def _s5t(u4, w1, co, tabs, d, wglu, bglu, bsz, seq, *, rc):
    n_q, nt, _ = u4.shape
    n_grp, kw, wcols = w1.shape
    n_state = (wcols - kw) // 2
    cg = kw // T_CHUNK
    rows = rc * T_CHUNK
    n_t = seq // rows
    blk = pl.BlockSpec((n_q, rows, LANES), lambda b, t: (0, b * n_t + t, 0))
    c2 = lambda b, t: (0, 0)
    c3 = lambda b, t: (0, 0, 0)
    whole = lambda a: pl.BlockSpec(a.shape, c3 if a.ndim == 3 else c2)
    body = functools.partial(_s5t_body, rc=rc, tc=T_CHUNK, sc=S_CHUNK, n_grp=n_grp, cg=cg)
    return pl.pallas_call(
        body,
        grid=(bsz, n_t),
        in_specs=[blk] + [whole(a) for a in (w1, co, *tabs, d, wglu, bglu)],
        out_specs=blk,
        out_shape=jax.ShapeDtypeStruct((n_q, nt, LANES), F32),
        scratch_shapes=[
            pltpu.VMEM((n_grp, rc, kw), BF16),
            pltpu.VMEM((n_grp, rc, kw), F32),
            pltpu.VMEM((n_grp, rc, 2 * n_state), F32),
            pltpu.VMEM((n_grp, rc + SUBLANES, 2 * n_state), F32),
            pltpu.VMEM((1, n_grp * 2 * n_state), F32),
            pltpu.VMEM((n_q, T_CHUNK, rc, LANES), F32),
        ],
        compiler_params=_params("arbitrary", "arbitrary"),
        name="s5t",
    )(u4, w1, co, *tabs, d, wglu, bglu)


def _mem_kv_body(m_ref, g_ref, wk_ref, wv_ref, wq_ref, wo_ref, qk_ref, vw_ref, *, xh):
    bsz, d, _ = qk_ref.shape
    n_mem = m_ref.shape[0] // bsz
    xd = d // xh
    scale = xd ** -0.5
    mn = _rms(m_ref[...], g_ref[...]).astype(BF16)
    k = jnp.dot(mn, wk_ref[...].astype(BF16), preferred_element_type=F32).astype(BF16)
    v = jnp.dot(mn, wv_ref[...].astype(BF16), preferred_element_type=F32).astype(BF16)
    wq = wq_ref[...].astype(BF16)
    wo = wo_ref[...].astype(BF16)
    nt_dims = (((1,), (1,)), ((), ()))
    for b in range(bsz):
        rows = slice(b * n_mem, (b + 1) * n_mem)
        for h in range(xh):
            hl = slice(h * xd, (h + 1) * xd)
            ml = slice(h * n_mem, (h + 1) * n_mem)
            qk = lax.dot_general(wq[:, hl], k[rows, hl], nt_dims, preferred_element_type=F32)
            qk_ref[b, :, ml] = (qk * scale).astype(BF16)
            vw_ref[b, ml, :] = jnp.dot(v[rows, hl], wo[hl, :],
                                       preferred_element_type=F32).astype(BF16)


def _mem_kv(mem, g, wk, wv, wq, wo):
    bsz, n_mem, d = mem.shape
    return pl.pallas_call(
        functools.partial(_mem_kv_body, xh=X_HEADS),
        out_shape=[
            jax.ShapeDtypeStruct((bsz, d, X_HEADS * n_mem), BF16),
            jax.ShapeDtypeStruct((bsz, X_HEADS * n_mem, d), BF16),
        ],
        compiler_params=pltpu.CompilerParams(vmem_limit_bytes=VMEM_LIMIT),
        name="mem_kv",
    )(mem.reshape(bsz * n_mem, d), g, wk, wv, wq, wo)


def _mix_attn_body(x_ref, o_ref, y_ref, wo_ref, gx_ref, qk_ref, vw_ref, h_ref, *, xh, n_sub):
    dn_w = o_ref.shape[1]
    n_mem = qk_ref.shape[2] // xh
    sub = x_ref.shape[0] // n_sub
    mm = functools.partial(jnp.dot, preferred_element_type=F32)
    wo = wo_ref[...].astype(BF16)

    def rows_gen(r0):
        rows = slice(r0, r0 + sub)
        y = jnp.concatenate([y_ref[q, rows, :] for q in range(y_ref.shape[0])], axis=1)
        mix = mm(o_ref[rows, :].astype(BF16), wo[0:dn_w, :]) + mm(y.astype(BF16), wo[dn_w:, :])
        yield
        h1 = x_ref[rows, :] + mix
        s = mm(_rms(h1, gx_ref[...]).astype(BF16), qk_ref[0])
        yield
        parts = []
        for h in range(xh):
            sh = s[:, h * n_mem:(h + 1) * n_mem]
            e = jnp.exp(sh - jnp.max(sh, axis=-1, keepdims=True))
            parts.append((e / jnp.sum(e, axis=-1, keepdims=True)).astype(BF16))
        att = mm(jnp.concatenate(parts, axis=1), vw_ref[0])
        yield
        h_ref[rows, :] = h1 + att

    _interleave([rows_gen(i * sub) for i in range(n_sub)])


def _mix_attn(x2, o, y, w_out, gx, qk, vw, bsz, seq, *, tm):
    nt, d = x2.shape
    dn_w = o.shape[1]
    n_t = seq // tm
    row = lambda b, t: (b * n_t + t, 0)
    c2 = lambda b, t: (0, 0)
    per_batch = lambda a: pl.BlockSpec((1,) + a.shape[1:], lambda b, t: (b, 0, 0))
    body = functools.partial(_mix_attn_body, xh=X_HEADS, n_sub=2)
    return pl.pallas_call(
        body,
        grid=(bsz, n_t),
        in_specs=[
            pl.BlockSpec((tm, d), row),
            pl.BlockSpec((tm, dn_w), row),
            pl.BlockSpec((y.shape[0], tm, LANES), lambda b, t: (0, b * n_t + t, 0)),
            pl.BlockSpec(w_out.shape, c2, pipeline_mode=pl.Buffered(1)),
            pl.BlockSpec((1, d), c2),
            per_batch(qk),
            per_batch(vw),
        ],
        out_specs=pl.BlockSpec((tm, d), row),
        out_shape=jax.ShapeDtypeStruct((nt, d), F32),
        compiler_params=_params("arbitrary", "arbitrary"),
        name="mix_attn",
    )(x2, o, y, w_out, gx, qk, vw)


def _ffn_body(h_ref, gf_ref, wg_ref, wu_ref, wd_ref, gl_ref, out_ref, *, final, fc):
    h = h_ref[...]
    hn = _rms(h, gf_ref[...]).astype(BF16)
    dff = wg_ref.shape[1]

    def down(gate, up, c0):
        act = (gate * jax.nn.sigmoid(gate) * up).astype(BF16)
        return jnp.dot(act, wd_ref[c0:c0 + fc, :].astype(BF16), preferred_element_type=F32)

    h3 = h
    prev = None
    for c0 in range(0, dff, fc):
        gate = jnp.dot(hn, wg_ref[:, c0:c0 + fc].astype(BF16), preferred_element_type=F32)
        up = jnp.dot(hn, wu_ref[:, c0:c0 + fc].astype(BF16), preferred_element_type=F32)
        if prev is not None:
            h3 = h3 + down(*prev)
        prev = (gate, up, c0)
    h3 = h3 + down(*prev)
    out_ref[...] = _rms(h3, gl_ref[...]) if final else h3


def _ffn(h, gf, wg, wu, wd, gl, *, tm, final):
    nt, d = h.shape
    dff = wg.shape[1]
    c2 = lambda i: (0, 0)
    row = lambda i: (i, 0)
    return pl.pallas_call(
        functools.partial(_ffn_body, final=final, fc=2 * LANES),
        grid=(nt // tm,),
        in_specs=[
            pl.BlockSpec((tm, d), row),
            pl.BlockSpec((1, d), c2),
            pl.BlockSpec((d, dff), c2, pipeline_mode=pl.Buffered(1)),
            pl.BlockSpec((d, dff), c2, pipeline_mode=pl.Buffered(1)),
            pl.BlockSpec((dff, d), c2, pipeline_mode=pl.Buffered(1)),
            pl.BlockSpec((1, d), c2),
        ],
        out_specs=pl.BlockSpec((tm, d), row),
        out_shape=jax.ShapeDtypeStruct((nt, d), F32),
        compiler_params=_params("arbitrary"),
        name="ffn",
    )(h, gf, wg, wu, wd, gl)


def _layer(h2, mem, bsz, seq, norm_mix_g, w_in, conv_w, dn_a_log, dn_dt_bias, dn_norm_g,
           s5_a_re, s5_a_im, s5_b_re, s5_b_im, s5_c_re, s5_c_im, s5_d, s5_log_dt,
           s5_w_glu, s5_b_glu, w_out, norm_x_g, norm_mem_g, w_xq, w_xk, w_xv, w_xo):
    d = h2.shape[1]
    nh = dn_a_log.shape[0]
    dh = dn_norm_g.shape[0]
    dn_w = nh * dh
    s5_w = s5_a_re.shape[0] * s5_b_re.shape[-1]

    off_a = 4 * dn_w
    off_u = off_a + 2 * nh
    w_ab = jnp.pad(w_in[:, off_a:off_u], ((0, 0), (0, LANES - 2 * nh)))
    w_cat = jnp.concatenate([w_in[:, :off_a], w_in[:, off_u:], w_ab], axis=1).astype(BF16)
    alog = jnp.pad(dn_a_log, (0, LANES - nh)).reshape(1, LANES)
    dtb = jnp.pad(dn_dt_bias, (0, LANES - nh)).reshape(1, LANES)

    u, o = _mixer_in(h2, seq, norm_mix_g.reshape(1, d), w_cat, conv_w, alog, dtb,
                     dn_norm_g.reshape(1, dh), nh=nh, dh=dh, tb=256)

    w1, co = _s5t_prep(s5_a_re, s5_a_im, s5_log_dt, s5_b_re, s5_b_im, s5_c_re, s5_c_im)
    tabs = _s5t_tables(s5_a_re, s5_a_im, s5_log_dt)
    y = _s5t(u, w1, co, tabs, s5_d.reshape(1, s5_w), s5_w_glu, s5_b_glu.reshape(1, s5_w),
             bsz, seq, rc=128)

    qk_mem, vw_mem = _mem_kv(mem, norm_mem_g.reshape(1, d), w_xk, w_xv, w_xq, w_xo)
    return _mix_attn(h2, o, y, w_out, norm_x_g.reshape(1, d), qk_mem, vw_mem, bsz, seq, tm=1024)


def kernel(x, mem, norm_mix_g, w_in, conv_w, dn_a_log, dn_dt_bias, dn_norm_g, s5_a_re, s5_a_im,
           s5_b_re, s5_b_im, s5_c_re, s5_c_im, s5_d, s5_log_dt, s5_w_glu, s5_b_glu, w_out,
           norm_x_g, norm_mem_g, w_xq, w_xk, w_xv, w_xo, norm_ffn_g, w_gate, w_up, w_down,
           norm_final_g):
    bsz, seq, d = x.shape
    depth = w_in.shape[0]
    h = x.reshape(bsz * seq, d)
    for l in range(depth):
        h = _layer(h, mem, bsz, seq, norm_mix_g[l], w_in[l], conv_w[l], dn_a_log[l],
                   dn_dt_bias[l], dn_norm_g[l], s5_a_re[l], s5_a_im[l], s5_b_re[l], s5_b_im[l],
                   s5_c_re[l], s5_c_im[l], s5_d[l], s5_log_dt[l], s5_w_glu[l], s5_b_glu[l],
                   w_out[l], norm_x_g[l], norm_mem_g[l], w_xq[l], w_xk[l], w_xv[l], w_xo[l])
        h = _ffn(h, norm_ffn_g[l].reshape(1, d), w_gate[l], w_up[l], w_down[l],
                 norm_final_g.reshape(1, d), tm=1024, final=l == depth - 1)
    return h.reshape(bsz, seq, d)
```

```python
import functools

import jax
import jax.numpy as jnp
from jax import lax
from jax.experimental import pallas as pl
from jax.experimental.pallas import tpu as pltpu

F32 = jnp.float32
BF16 = jnp.bfloat16
EPS = 1e-6
HIGHEST = lax.Precision.HIGHEST

LANES = 128
SUBLANES = 8
DN_CHUNK = 64
T_CHUNK = 16
S_CHUNK = 16
X_HEADS = 4
VMEM_LIMIT = 58 * 1024 * 1024

MIXER_ROWS = 256
S5_BLOCKS = 128
ATTN_ROWS = 1024
FFN_ROWS = 1024


def _fdot(a, b):
    return jnp.dot(a, b, precision=HIGHEST, preferred_element_type=F32)


def _rms(x, g):
    return x * lax.rsqrt(jnp.mean(x * x, axis=-1, keepdims=True) + EPS) * g


def _interleave(gens, stages_per_round=None):
    pending = [(gen, 1 if stages_per_round is None else stages_per_round[i])
               for i, gen in enumerate(gens)]
    while pending:
        for item in list(pending):
            gen, n = item
            try:
                for _ in range(n):
                    next(gen)
            except StopIteration:
                pending.remove(item)


def _params(*sem):
    return pltpu.CompilerParams(dimension_semantics=sem, vmem_limit_bytes=VMEM_LIMIT)


def _in_proj_stages(x_ref, g_ref, w_ref, cw_ref, alog_ref, dtb_ref,
                    q_ref, k_ref, kt_ref, v_ref, z_ref, u_ref, gt_ref,
                    buf_ref, *, tm, dn_w, nh, dh, cw_n):
    xnb = _rms(x_ref[...], g_ref[...]).astype(BF16)
    gw = 2 * dh

    def conv_silu(p, cols):
        outs = []
        for i in range(gw // LANES):
            c0 = cols.start + i * LANES
            s = c0 // LANES
            pc = p[:, i * LANES:(i + 1) * LANES]
            buf_ref[s, SUBLANES:SUBLANES + tm, :] = pc
            acc = cw_ref[cw_n - 1:cw_n, c0:c0 + LANES] * pc
            for j in range(cw_n - 1):
                off = SUBLANES - (cw_n - 1) + j
                acc = acc + cw_ref[j:j + 1, c0:c0 + LANES] * buf_ref[s, off:off + tm, :]
            buf_ref[s, 0:SUBLANES, :] = buf_ref[s, tm:tm + SUBLANES, :]
            outs.append(acc * jax.nn.sigmoid(acc))
        return jnp.concatenate(outs, axis=1)

    def l2n(a, scale):
        parts = []
        for i in range(gw // dh):
            ah = a[:, i * dh:(i + 1) * dh]
            parts.append(ah * (lax.rsqrt(jnp.sum(ah * ah, -1, keepdims=True) + EPS) * scale))
        return jnp.concatenate(parts, axis=1)

    def ep_q(p, c0):
        q_ref[:, c0:c0 + gw] = l2n(conv_silu(p, slice(c0, c0 + gw)), dh ** -0.5)

    def ep_k(p, c0):
        kn = l2n(conv_silu(p, slice(dn_w + c0, dn_w + c0 + gw)), 1.0)
        k_ref[:, c0:c0 + gw] = kn
        kt_ref[c0:c0 + gw, :] = kn.T

    def ep_v(p, c0):
        v_ref[:, c0:c0 + gw] = conv_silu(p, slice(2 * dn_w + c0, 2 * dn_w + c0 + gw))

    def ep_z(p, c0):
        z_ref[:, c0:c0 + gw] = p

    def ep_u(p, c0):
        for j in range(gw // LANES):
            u_ref[c0 // LANES + j] = p[:, j * LANES:(j + 1) * LANES]

    def ep_gates(ab, c0):
        lane = lax.broadcasted_iota(jnp.int32, ab.shape, 1)
        g = -jnp.exp(alog_ref[...]) * jnp.logaddexp(ab + dtb_ref[...], 0.0)
        gates = jnp.where(lane < nh, g, jax.nn.sigmoid(ab))
        gt_ref[...] = gates.T[0:SUBLANES, :]

    groups = []
    for i, ep in enumerate((ep_q, ep_k, ep_v, ep_z, ep_u)):
        groups += [(ep, i * dn_w + c0, gw, c0) for c0 in range(0, dn_w, gw)]
    groups.append((ep_gates, 5 * dn_w, LANES, 0))

    prev = None
    for ep, w0, width, c0 in groups:
        p = jnp.dot(xnb, w_ref[:, w0:w0 + width], preferred_element_type=F32)
        if prev is not None:
            prev[0](prev[1], prev[2])
        prev = (ep, p, c0)
        yield
    prev[0](prev[1], prev[2])


def _delta_stages(q_ref, k_ref, kt_ref, v_ref, z_ref, gt_ref, ng_ref,
                  o_ref, s_ref, u_sc, w_sc, qg_sc, attn_sc, kdt_sc, gtot_sc, wr, rd,
                  *, tb, nh, dh, ch):
    nc = tb // ch
    heads = range(nh)
    hls = [slice(h * dh, (h + 1) * dh) for h in heads]
    mm = functools.partial(jnp.dot, preferred_element_type=F32)

    def phase_b():
        gtot_b = gtot_sc[rd]
        s_all = [s_ref[h] for h in heads]
        for cidx in range(nc):
            rows = slice(cidx * ch, (cidx + 1) * ch)
            sb = [s_all[h].astype(BF16) for h in heads]
            ws = [mm(w_sc[rd, h, rows, :], sb[h]) for h in heads]
            qs = [mm(qg_sc[rd, h, rows, :], sb[h]) for h in heads]
            yield
            vb = [(u_sc[rd, h, rows, :] - ws[h]).astype(BF16) for h in heads]
            av = [mm(attn_sc[rd, h, rows, rows], vb[h]) for h in heads]
            kv = [mm(kdt_sc[rd, h, :, rows], vb[h]) for h in heads]
            yield
            outs = []
            for h in heads:
                a_last = jnp.exp(gtot_b[h:h + 1, cidx * ch:cidx * ch + 1])
                s_all[h] = s_all[h] * a_last + kv[h]
                o = qs[h] + av[h]
                o = o * lax.rsqrt(jnp.mean(o * o, -1, keepdims=True) + EPS)
                zh = z_ref[rows, hls[h]]
                outs.append(o * ng_ref[...] * (zh * jax.nn.sigmoid(zh)))
            o_ref[rows, :] = jnp.concatenate(outs, axis=1)
        for h in heads:
            s_ref[h] = s_all[h]

    r = lax.broadcasted_iota(jnp.int32, (tb, tb), 0)
    c = lax.broadcasted_iota(jnp.int32, (tb, tb), 1)
    same = (r // ch) == (c // ch)
    causal_bd = same & (c <= r)
    strict_bd = same & (c < r)
    gt = gt_ref[...]
    gcum_t = _fdot(gt, jnp.where(same & (r <= c), 1.0, 0.0))
    gtot_t = _fdot(gt, jnp.where(same, 1.0, 0.0))
    gcum = gcum_t.T
    gates = gt.T

    def pack(x):
        acc = x[0:ch, :]
        for i in range(1, nc):
            acc = acc + x[i * ch:(i + 1) * ch, :]
        return acc

    def unpack(x):
        return jnp.where(same, jnp.concatenate([x] * nc, axis=0), 0.0).astype(BF16)

    n_sq = max(1, (ch - 1).bit_length() - 1)

    def phase_a():
        gcol = [gcum[:, h:h + 1] for h in heads]
        grow = [gcum_t[h:h + 1, :] for h in heads]
        beta = [gates[:, nh + h:nh + h + 1] for h in heads]
        kb = [k_ref[:, hls[h]] * beta[h] for h in heads]
        gram = [mm(jnp.concatenate([kb[h], q_ref[:, hls[h]]], axis=0).astype(BF16),
                   kt_ref[hls[h], :].astype(BF16)) for h in heads]
        yield
        ppk, apk = [], []
        for h in heads:
            decay = jnp.exp(jnp.where(causal_bd, gcol[h] - grow[h], -jnp.inf))
            p_bd = -jnp.where(strict_bd, gram[h][:tb] * decay, 0.0)
            attn_sc[wr, h] = (gram[h][tb:] * decay).astype(BF16)
            pk = pack(p_bd)
            apk.append(pk)
            ppk.append(mm(pk.astype(BF16), p_bd.astype(BF16)))
        yield
        for j in range(1, n_sq + 1):
            res = []
            for h in heads:
                lhs = jnp.concatenate([apk[h], ppk[h]], axis=0) if j < n_sq else apk[h]
                res.append(mm(lhs.astype(BF16), unpack(ppk[h])))
            yield
            for h in heads:
                apk[h] = apk[h] + ppk[h] + res[h][:ch]
                if j < n_sq:
                    ppk[h] = res[h][ch:]
        egc = [jnp.exp(gcol[h]) for h in heads]
        rhs = [jnp.concatenate([v_ref[:, hls[h]] * beta[h], kb[h] * egc[h]], axis=1) for h in heads]
        prod = [mm(unpack(apk[h]), rhs[h].astype(BF16)) for h in heads]
        yield
        for h in heads:
            uw = rhs[h] + prod[h]
            u_sc[wr, h] = uw[:, :dh]
            w_sc[wr, h] = uw[:, dh:].astype(BF16)
            qg_sc[wr, h] = (q_ref[:, hls[h]] * egc[h]).astype(BF16)
            kdt_sc[wr, h] = (kt_ref[hls[h], :] * jnp.exp(gtot_t[h:h + 1, :] - grow[h])).astype(BF16)
        gtot_sc[wr] = gtot_t

    return phase_b(), phase_a()


def _mixer_in_body(x_ref, g_ref, w_ref, cw_ref, alog_ref, dtb_ref, ng_ref,
                   u_ref, o_ref,
                   buf_ref, q_sc, k_sc, kt_sc, v_sc, z_sc, gt_sc,
                   s_ref, u_sc, w_sc, qg_sc, attn_sc, kdt_sc, gtot_sc,
                   *, tb, n_t, nh, dh, cw_n):
    s = pl.program_id(0)
    dn_w = nh * dh

    @pl.when(s == 0)
    def _():
        for ref in (buf_ref, q_sc, k_sc, kt_sc, v_sc, z_sc, gt_sc,
                    s_ref, u_sc, w_sc, qg_sc, attn_sc, kdt_sc, gtot_sc):
            ref[...] = jnp.zeros(ref.shape, ref.dtype)

    @pl.when(s % n_t == 0)
    def _():
        buf_ref[:, 0:SUBLANES, :] = jnp.zeros((3 * dn_w // LANES, SUBLANES, LANES), F32)

    @pl.when((s - 2) % n_t == 0)
    def _():
        s_ref[...] = jnp.zeros(s_ref.shape, F32)

    cur = s % 2
    prv = 1 - cur
    zw = s % 3
    zr = (s + 1) % 3
    proj = _in_proj_stages(
        x_ref, g_ref, w_ref, cw_ref, alog_ref, dtb_ref,
        q_sc.at[cur], k_sc.at[cur], kt_sc.at[cur], v_sc.at[cur], z_sc.at[zw], u_ref, gt_sc.at[cur],
        buf_ref, tm=tb, dn_w=dn_w, nh=nh, dh=dh, cw_n=cw_n)
    phase_b, phase_a = _delta_stages(
        q_sc.at[prv], k_sc.at[prv], kt_sc.at[prv], v_sc.at[prv], z_sc.at[zr], gt_sc.at[prv], ng_ref,
        o_ref, s_ref, u_sc, w_sc, qg_sc, attn_sc, kdt_sc, gtot_sc, cur, prv,
        tb=tb, nh=nh, dh=dh, ch=DN_CHUNK)
    _interleave([proj, phase_b, phase_a])


def _mixer_in(x2, seq, g, w_cat, conv_w, alog, dtb, ng, *, nh, dh, tb):
    nt, d = x2.shape
    dn_w = nh * dh
    n_blk = nt // tb
    cw_n = conv_w.shape[0]
    const = lambda s: (0, 0)
    in_blk = lambda s: (jnp.minimum(s, n_blk - 1), 0)
    out_blk = lambda s: (jnp.maximum(s - 2, 0), 0)
    body = functools.partial(_mixer_in_body, tb=tb, n_t=seq // tb, nh=nh, dh=dh, cw_n=cw_n)
    two = lambda *shape: pltpu.VMEM((2,) + shape, F32)
    return pl.pallas_call(
        body,
        grid=(n_blk + 2,),
        in_specs=[pl.BlockSpec((tb, d), in_blk)]
        + [pl.BlockSpec(a.shape, const) for a in (g, w_cat, conv_w, alog, dtb, ng)],
        out_specs=[pl.BlockSpec((dn_w // LANES, tb, LANES),
                                lambda s: (0, jnp.minimum(s, n_blk - 1), 0)),
                   pl.BlockSpec((tb, dn_w), out_blk)],
        out_shape=[jax.ShapeDtypeStruct((dn_w // LANES, nt, LANES), F32),
                   jax.ShapeDtypeStruct((nt, dn_w), F32)],
        scratch_shapes=[
            pltpu.VMEM((3 * dn_w // LANES, tb + SUBLANES, LANES), F32),
            two(tb, dn_w), two(tb, dn_w), two(dn_w, tb), two(tb, dn_w),
            pltpu.VMEM((3, tb, dn_w), F32),
            two(SUBLANES, tb),
            pltpu.VMEM((nh, dh, dh), F32),
            pltpu.VMEM((2, nh, tb, dh), F32),
            pltpu.VMEM((2, nh, tb, dh), BF16),
            pltpu.VMEM((2, nh, tb, dh), BF16),
            pltpu.VMEM((2, nh, tb, tb), BF16),
            pltpu.VMEM((2, nh, dh, tb), BF16),
            pltpu.VMEM((2, SUBLANES, tb), F32),
        ],
        compiler_params=_params("arbitrary"),
        name="mixer_in",
    )(x2, g, w_cat, conv_w, alog, dtb, ng)


def _ftdot(a, b):
    return lax.dot_general(a.astype(BF16), b.astype(BF16), (((0,), (0,)), ((), ())),
                           preferred_element_type=F32)


def _cmul(xr, xi, yr, yi):
    return xr * yr - xi * yi, xr * yi + xi * yr


def _s5t_prep_body(are_ref, aim_ref, ldt_ref, btr_ref, bti_ref, ctr_ref, cti_ref, w1_ref, co_ref,
                   *, tc, cg, n_p):
    n_grp = w1_ref.shape[0]
    kw = tc * cg
    a_re = are_ref[...]
    a_im = aim_ref[...]
    dt = jnp.exp(ldt_ref[...])
    lre = a_re * dt
    lim = a_im * dt
    mag = jnp.exp(lre)
    ab = (mag * jnp.cos(lim), mag * jnp.sin(lim))
    nr, ni = ab[0] - 1.0, ab[1]
    den = a_re * a_re + a_im * a_im
    co = ((nr * a_re + ni * a_im) / den, (ni * a_re - nr * a_im) / den)
    inv_mag2 = jnp.exp(-2.0 * lre)
    ai = (ab[0] * inv_mag2, -ab[1] * inv_mag2)
    n_bits = tc.bit_length() - 1
    pos, neg = [ab], [ai]
    for _ in range(n_bits - 1):
        pos.append(_cmul(*pos[-1], *pos[-1]))
        neg.append(_cmul(*neg[-1], *neg[-1]))
    top = pos[0]
    for f in pos[1:]:
        top = _cmul(*top, *f)
    rows = [co, top] + pos + neg
    flat = [part for pair in rows for part in pair]
    cols = []
    for i in range(0, len(flat), SUBLANES):
        blk = flat[i:i + SUBLANES]
        blk = blk + [blk[-1]] * (SUBLANES - len(blk))
        t = jnp.concatenate(blk, axis=0).T
        cols += [t[:, j:j + 1] for j in range(SUBLANES)]
    pair = lambda k: (cols[2 * k], cols[2 * k + 1])
    co_c, top_c = pair(0), pair(1)
    pos_c = [pair(2 + k) for k in range(n_bits)]
    neg_c = [pair(2 + n_bits + k) for k in range(n_bits)]

    slot = lax.broadcasted_iota(jnp.int32, (1, kw), 1) // cg

    def power(factors):
        acc = None
        for k, (fr, fi) in enumerate(factors):
            bit = ((slot >> k) & 1) == 1
            term = (jnp.where(bit, fr, 1.0), jnp.where(bit, fi, 0.0))
            acc = term if acc is None else _cmul(*acc, *term)
        return acc

    place = jnp.where(lax.broadcasted_iota(jnp.int32, (cg, kw), 0)
                      == lax.broadcasted_iota(jnp.int32, (cg, kw), 1) % cg, 1.0, 0.0)
    place = place.astype(BF16)

    def rep(ref):
        x = ref[...]
        hi = x.astype(BF16)
        r1 = x - hi.astype(F32)
        mid = r1.astype(BF16)
        lo = (r1 - mid.astype(F32)).astype(BF16)
        return sum(jnp.dot(part, place, preferred_element_type=F32) for part in (hi, mid, lo))
    bt = _cmul(*co_c, rep(btr_ref), rep(bti_ref))
    pb = _cmul(*bt, *power(neg_c))
    qc = _cmul(rep(ctr_ref), rep(cti_ref), *power(pos_c))
    q1 = _cmul(*qc, *pos_c[0])
    bi = _cmul(*pb, *top_c)
    r2 = lax.broadcasted_iota(jnp.int32, (kw, kw), 0) // cg
    c2 = lax.broadcasted_iota(jnp.int32, (kw, kw), 1) // cg
    eye = jnp.where(lax.broadcasted_iota(jnp.int32, (n_p, n_p), 0)
                    == lax.broadcasted_iota(jnp.int32, (n_p, n_p), 1), 1.0, 0.0)
    for g in range(n_grp):
        rs = slice(g * n_p, (g + 1) * n_p)
        m_full = _ftdot(pb[0][rs], qc[0][rs]) - _ftdot(pb[1][rs], qc[1][rs])
        w1_ref[g, :, 0:kw] = jnp.where(r2 <= c2, m_full, 0.0).astype(w1_ref.dtype)
        w1_ref[g, :, kw:kw + n_p] = _ftdot(bi[0][rs], eye).astype(w1_ref.dtype)
        w1_ref[g, :, kw + n_p:kw + 2 * n_p] = _ftdot(bi[1][rs], eye).astype(w1_ref.dtype)
        co_ref[g, 0:n_p, :] = q1[0][rs].astype(co_ref.dtype)
        co_ref[g, n_p:2 * n_p, :] = (-q1[1][rs]).astype(co_ref.dtype)


def _s5t_prep(a_re, a_im, log_dt, b_re, b_im, c_re, c_im):
    g, p = a_re.shape
    cg = b_re.shape[-1]
    kw = T_CHUNK * cg
    row = lambda a: a.reshape(1, g * p)
    slots = lambda a: a.reshape(g * p, cg)
    return pl.pallas_call(
        functools.partial(_s5t_prep_body, tc=T_CHUNK, cg=cg, n_p=p),
        out_shape=[jax.ShapeDtypeStruct((g, kw, kw + 2 * p), BF16),
                   jax.ShapeDtypeStruct((g, 2 * p, kw), BF16)],
        compiler_params=pltpu.CompilerParams(vmem_limit_bytes=VMEM_LIMIT),
        name="s5t_prep",
    )(row(a_re), row(a_im), row(jnp.broadcast_to(log_dt[:, None], (g, p))),
      slots(b_re), slots(b_im), slots(c_re.transpose(0, 2, 1)), slots(c_im.transpose(0, 2, 1)))


def _s5t_tables_body(are_ref, aim_ref, ldt_ref, pin_re, pin_im, pout_re, pout_im, al_re, al_im,
                     *, tc, sc):
    lre = are_ref[...] * jnp.exp(ldt_ref[...]) * float(tc)
    lim = aim_ref[...] * jnp.exp(ldt_ref[...]) * float(tc)
    lane = lax.broadcasted_iota(jnp.int32, lre.shape, 1)
    sign = jnp.where((lane % LANES) < LANES // 2, -1.0, 1.0)
    n = lax.broadcasted_iota(jnp.int32, (sc, 1), 0).astype(F32) - float(sc // 2)
    m_out = jnp.exp(lre * n)
    m_in = jnp.exp(-(lre * n))
    ang = lim * n
    pin_re[...] = (m_in * jnp.cos(ang)).astype(pin_re.dtype)
    pin_im[...] = (sign * (-(m_in * jnp.sin(ang)))).astype(pin_im.dtype)
    pout_re[...] = m_out * jnp.cos(ang)
    pout_im[...] = sign * (m_out * jnp.sin(ang))
    m_al = jnp.exp(lre * float(sc))
    al_re[...] = m_al * jnp.cos(lim * float(sc))
    al_im[...] = sign * (m_al * jnp.sin(lim * float(sc)))


def _s5t_tables(a_re, a_im, log_dt):
    g, p = a_re.shape
    lanes = g * 2 * p
    spread = lambda a: jnp.broadcast_to(a[:, None, :], (g, 2, p)).reshape(1, lanes)
    tab16 = jax.ShapeDtypeStruct((S_CHUNK, lanes), BF16)
    tab32 = jax.ShapeDtypeStruct((S_CHUNK, lanes), F32)
    one = jax.ShapeDtypeStruct((1, lanes), F32)
    return pl.pallas_call(
        functools.partial(_s5t_tables_body, tc=T_CHUNK, sc=S_CHUNK),
        out_shape=[tab16, tab16, tab32, tab32, one, one], name="s5t_tables",
    )(spread(a_re), spread(a_im), spread(jnp.broadcast_to(log_dt[:, None], (g, p))))


def _atom_transpose(regs, atom, width):
    regs = list(regs)
    n = len(regs)
    d = n // 2
    while d:
        low = (atom & d) == 0
        for j in range(n):
            if j & d:
                continue
            a, b = regs[j], regs[j + d]
            regs[j] = jnp.where(low, a, pltpu.roll(b, d * width, axis=1))
            regs[j + d] = jnp.where(low, pltpu.roll(a, LANES - d * width, axis=1), b)
        d //= 2
    return regs


def _swap_halves(x):
    tiles = [pltpu.roll(x[:, t * LANES:(t + 1) * LANES], LANES // 2, axis=1)
             for t in range(x.shape[1] // LANES)]
    return jnp.concatenate(tiles, axis=1)


def _s5t_body(u_ref, w1_ref, co_ref, pin_re, pin_im, pout_re, pout_im, al_re, al_im,
              d_ref, wglu_ref, bglu_ref, y_ref,
              upk_ref, yloc_ref, sin_ref, xs_ref, carry_ref, udi_ref, *, rc, tc, sc, n_grp, cg):
    t = pl.program_id(1)
    kw = tc * cg
    n_q = u_ref.shape[0]
    atoms = LANES // cg

    @pl.when(t == 0)
    def _():
        carry_ref[...] = jnp.zeros(carry_ref.shape, F32)
        xs_ref[:, 0:SUBLANES, :] = jnp.zeros((n_grp, SUBLANES, LANES), F32)

    atom = lax.broadcasted_iota(jnp.int32, (rc, LANES), 1) // cg

    n_a = tc // atoms

    def pack():
        for q in range(n_q):
            for a in range(n_a):
                planes = [u_ref[q, pl.ds(a * atoms + k, rc, stride=tc), :] for k in range(atoms)]
                for k in range(atoms):
                    udi_ref[q, a * atoms + k] = planes[k]
                for r, tile in enumerate(_atom_transpose(planes, atom, cg)):
                    upk_ref[q * atoms + r, :, a * LANES:(a + 1) * LANES] = tile.astype(BF16)
                yield

    per_stage = atoms // n_a

    def local():
        for _ in range(n_a):
            yield
        for g in range(n_grp):
            res = jnp.dot(upk_ref[g], w1_ref[g], preferred_element_type=F32)
            yloc_ref[g] = res[:, :kw]
            sin_ref[g] = res[:, kw:]
            if g % per_stage == per_stage - 1:
                yield

    _interleave([pack(), local()])

    ri = lax.broadcasted_iota(jnp.int32, (sc, sc), 0)
    ci = lax.broadcasted_iota(jnp.int32, (sc, sc), 1)
    tri = jnp.where(ci <= ri, 1.0, 0.0).astype(BF16)
    n_c2 = rc // sc
    ss = [jnp.concatenate([sin_ref[g, c2 * sc:(c2 + 1) * sc, :] for g in range(n_grp)], axis=1)
          for c2 in range(n_c2)]
    ss_sw = [_swap_halves(s) for s in ss]
    xin = [pin_re[...] * ss[c2].astype(BF16) + pin_im[...] * ss_sw[c2].astype(BF16)
           for c2 in range(n_c2)]
    psum = [jnp.dot(tri, x, preferred_element_type=F32) for x in xin]
    psum_sw = [_swap_halves(p) for p in psum]
    carry = carry_ref[...]
    carry_sw = _swap_halves(carry)
    carries = []
    for c2 in range(rc // sc):
        carries.append((carry, carry_sw))
        last = psum[c2][sc - 1:sc, :] + carry
        last_sw = psum_sw[c2][sc - 1:sc, :] + carry_sw
        carry = al_re[...] * last + al_im[...] * last_sw
        carry_sw = al_re[...] * last_sw - al_im[...] * last
    carry_ref[...] = carry
    for c2 in range(rc // sc):
        x = (pout_re[...] * (psum[c2] + carries[c2][0])
             + pout_im[...] * (psum_sw[c2] + carries[c2][1]))
        for g in range(n_grp):
            xs_ref[g, SUBLANES + c2 * sc:SUBLANES + (c2 + 1) * sc, :] = x[:, g * LANES:(g + 1) * LANES]

    for g in range(n_grp):
        xprev = xs_ref[g, SUBLANES - 1:SUBLANES - 1 + rc, :].astype(BF16)
        yloc_ref[g] = yloc_ref[g] + jnp.dot(xprev, co_ref[g], preferred_element_type=F32)
    xs_ref[:, SUBLANES - 1:SUBLANES, :] = xs_ref[:, SUBLANES - 1 + rc:SUBLANES + rc, :]

    wglu = wglu_ref[...].astype(BF16)

    def unpack(a):
        for _ in range(a * n_q):
            yield
        planes = [[None] * n_q for _ in range(atoms)]
        for q in range(n_q):
            tiles = [yloc_ref[q * atoms + r, :, a * LANES:(a + 1) * LANES] for r in range(atoms)]
            for k, tile in enumerate(_atom_transpose(tiles, atom, cg)):
                planes[k][q] = tile
            yield
        for k in range(atoms):
            i = a * atoms + k
            uu = jnp.concatenate([udi_ref[q, i] for q in range(n_q)], axis=1)
            y = jax.nn.gelu(jnp.concatenate(planes[k], axis=1) + d_ref[...] * uu)
            gate = jax.nn.sigmoid(jnp.dot(y.astype(BF16), wglu, preferred_element_type=F32)
                                  + bglu_ref[...])
            out = y * gate
            for q in range(n_q):
                y_ref[q, pl.ds(i, rc, stride=tc), :] = out[:, q * LANES:(q + 1) * LANES]
            if k % 2 == 1:
                yield

    _interleave([unpack(a) for a in range(n_a)])


def _s5t(u4, w1, co, tabs, d, wglu, bglu, bsz, seq, *, rc):
    n_q, nt, _ = u4.shape
    n_grp, kw, wcols = w1.shape
    n_state = (wcols - kw) // 2
    cg = kw // T_CHUNK
    rows = rc * T_CHUNK
    n_t = seq // rows
    blk = pl.BlockSpec((n_q, rows, LANES), lambda b, t: (0, b * n_t + t, 0))
    c2 = lambda b, t: (0, 0)
    c3 = lambda b, t: (0, 0, 0)
    whole = lambda a: pl.BlockSpec(a.shape, c3 if a.ndim == 3 else c2)
    body = functools.partial(_s5t_body, rc=rc, tc=T_CHUNK, sc=S_CHUNK, n_grp=n_grp, cg=cg)
    return pl.pallas_call(
        body,
        grid=(bsz, n_t),
        in_specs=[blk] + [whole(a) for a in (w1, co, *tabs, d, wglu, bglu)],
        out_specs=blk,
        out_shape=jax.ShapeDtypeStruct((n_q, nt, LANES), F32),
        scratch_shapes=[
            pltpu.VMEM((n_grp, rc, kw), BF16),
            pltpu.VMEM((n_grp, rc, kw), F32),
            pltpu.VMEM((n_grp, rc, 2 * n_state), F32),
            pltpu.VMEM((n_grp, rc + SUBLANES, 2 * n_state), F32),
            pltpu.VMEM((1, n_grp * 2 * n_state), F32),
            pltpu.VMEM((n_q, T_CHUNK, rc, LANES), F32),
        ],
        compiler_params=_params("arbitrary", "arbitrary"),
        name="s5t",
    )(u4, w1, co, *tabs, d, wglu, bglu)


def _mem_kv_body(m_ref, g_ref, wk_ref, wv_ref, wq_ref, wo_ref, qk_ref, vw_ref, w_sc, *, xh):
    @pl.when(pl.program_id(0) == 0)
    def _():
        for i, ref in enumerate((wk_ref, wv_ref, wq_ref, wo_ref)):
            w_sc[i] = ref[...].astype(BF16)

    _, n_mem, d = m_ref.shape
    xd = d // xh
    scale = xd ** -0.5
    mn = _rms(m_ref[0], g_ref[...]).astype(BF16)
    k = jnp.dot(mn, w_sc[0], preferred_element_type=F32).astype(BF16)
    v = jnp.dot(mn, w_sc[1], preferred_element_type=F32).astype(BF16)
    nt_dims = (((1,), (1,)), ((), ()))
    for h in range(xh):
        hl = slice(h * xd, (h + 1) * xd)
        ml = slice(h * n_mem, (h + 1) * n_mem)
        qk = lax.dot_general(w_sc[2, :, hl], k[:, hl], nt_dims, preferred_element_type=F32)
        qk_ref[0, :, ml] = (qk * scale).astype(BF16)
        vw_ref[0, ml, :] = jnp.dot(v[:, hl], w_sc[3, hl, :],
                                   preferred_element_type=F32).astype(BF16)


def _mem_kv(mem, g, wk, wv, wq, wo):
    bsz, n_mem, d = mem.shape
    hm = X_HEADS * n_mem
    const = lambda b: (0, 0)
    weight = pl.BlockSpec((d, d), const, pipeline_mode=pl.Buffered(1))
    return pl.pallas_call(
        functools.partial(_mem_kv_body, xh=X_HEADS),
        grid=(bsz,),
        in_specs=[pl.BlockSpec((1, n_mem, d), lambda b: (b, 0, 0)), pl.BlockSpec((1, d), const),
                  weight, weight, weight, weight],
        out_specs=[pl.BlockSpec((1, d, hm), lambda b: (b, 0, 0)),
                   pl.BlockSpec((1, hm, d), lambda b: (b, 0, 0))],
        out_shape=[
            jax.ShapeDtypeStruct((bsz, d, hm), BF16),
            jax.ShapeDtypeStruct((bsz, hm, d), BF16),
        ],
        scratch_shapes=[pltpu.VMEM((4, d, d), BF16)],
        compiler_params=_params("arbitrary"),
        name="mem_kv",
    )(mem, g, wk, wv, wq, wo)


def _mix_attn_body(x_ref, o_ref, y_ref, wo_ref, gx_ref, qk_ref, vw_ref, h_ref, *, xh, n_sub):
    dn_w = o_ref.shape[1]
    n_mem = qk_ref.shape[2] // xh
    sub = x_ref.shape[0] // n_sub
    mm = functools.partial(jnp.dot, preferred_element_type=F32)
    wo = wo_ref[...].astype(BF16)

    def rows_gen(r0):
        rows = slice(r0, r0 + sub)
        y = jnp.concatenate([y_ref[q, rows, :] for q in range(y_ref.shape[0])], axis=1)
        mix = mm(o_ref[rows, :].astype(BF16), wo[0:dn_w, :]) + mm(y.astype(BF16), wo[dn_w:, :])
        yield
        h1 = x_ref[rows, :] + mix
        s = mm(_rms(h1, gx_ref[...]).astype(BF16), qk_ref[0])
        yield
        parts = []
        for h in range(xh):
            sh = s[:, h * n_mem:(h + 1) * n_mem]
            e = jnp.exp(sh - jnp.max(sh, axis=-1, keepdims=True))
            parts.append((e / jnp.sum(e, axis=-1, keepdims=True)).astype(BF16))
        att = mm(jnp.concatenate(parts, axis=1), vw_ref[0])
        yield
        h_ref[rows, :] = h1 + att

    _interleave([rows_gen(i * sub) for i in range(n_sub)])


def _mix_attn(x2, o, y, w_out, gx, qk, vw, bsz, seq, *, tm):
    nt, d = x2.shape
    dn_w = o.shape[1]
    n_t = seq // tm
    row = lambda b, t: (b * n_t + t, 0)
    c2 = lambda b, t: (0, 0)
    per_batch = lambda a: pl.BlockSpec((1,) + a.shape[1:], lambda b, t: (b, 0, 0))
    body = functools.partial(_mix_attn_body, xh=X_HEADS, n_sub=2)
    return pl.pallas_call(
        body,
        grid=(bsz, n_t),
        in_specs=[
            pl.BlockSpec((tm, d), row),
            pl.BlockSpec((tm, dn_w), row),
            pl.BlockSpec((y.shape[0], tm, LANES), lambda b, t: (0, b * n_t + t, 0)),
            pl.BlockSpec(w_out.shape, c2, pipeline_mode=pl.Buffered(1)),
            pl.BlockSpec((1, d), c2),
            per_batch(qk),
            per_batch(vw),
        ],
        out_specs=pl.BlockSpec((tm, d), row),
        out_shape=jax.ShapeDtypeStruct((nt, d), F32),
        compiler_params=_params("arbitrary", "arbitrary"),
        name="mix_attn",
    )(x2, o, y, w_out, gx, qk, vw)


def _ffn_body(h_ref, gf_ref, wg_ref, wu_ref, wd_ref, gl_ref, out_ref, *, final, fc):
    h = h_ref[...]
    hn = _rms(h, gf_ref[...]).astype(BF16)
    dff = wg_ref.shape[1]

    def down(gate, up, c0):
        act = (gate * jax.nn.sigmoid(gate) * up).astype(BF16)
        return jnp.dot(act, wd_ref[c0:c0 + fc, :].astype(BF16), preferred_element_type=F32)

    h3 = h
    prev = None
    for c0 in range(0, dff, fc):
        gate = jnp.dot(hn, wg_ref[:, c0:c0 + fc].astype(BF16), preferred_element_type=F32)
        up = jnp.dot(hn, wu_ref[:, c0:c0 + fc].astype(BF16), preferred_element_type=F32)
        if prev is not None:
            h3 = h3 + down(*prev)
        prev = (gate, up, c0)
    h3 = h3 + down(*prev)
    out_ref[...] = _rms(h3, gl_ref[...]) if final else h3


def _ffn(h, gf, wg, wu, wd, gl, *, tm, final):
    nt, d = h.shape
    dff = wg.shape[1]
    c2 = lambda i: (0, 0)
    row = lambda i: (i, 0)
    return pl.pallas_call(
        functools.partial(_ffn_body, final=final, fc=2 * LANES),
        grid=(nt // tm,),
        in_specs=[
            pl.BlockSpec((tm, d), row),
            pl.BlockSpec((1, d), c2),
            pl.BlockSpec((d, dff), c2, pipeline_mode=pl.Buffered(1)),
            pl.BlockSpec((d, dff), c2, pipeline_mode=pl.Buffered(1)),
            pl.BlockSpec((dff, d), c2, pipeline_mode=pl.Buffered(1)),
            pl.BlockSpec((1, d), c2),
        ],
        out_specs=pl.BlockSpec((tm, d), row),
        out_shape=jax.ShapeDtypeStruct((nt, d), F32),
        compiler_params=_params("arbitrary"),
        name="ffn",
    )(h, gf, wg, wu, wd, gl)


def _layer(h2, mem, bsz, seq, norm_mix_g, w_in, conv_w, dn_a_log, dn_dt_bias, dn_norm_g,
           s5_a_re, s5_a_im, s5_b_re, s5_b_im, s5_c_re, s5_c_im, s5_d, s5_log_dt,
           s5_w_glu, s5_b_glu, w_out, norm_x_g, norm_mem_g, w_xq, w_xk, w_xv, w_xo):
    d = h2.shape[1]
    nh = dn_a_log.shape[0]
    dh = dn_norm_g.shape[0]
    dn_w = nh * dh
    s5_w = s5_a_re.shape[0] * s5_b_re.shape[-1]

    off_a = 4 * dn_w
    off_u = off_a + 2 * nh
    w_ab = jnp.pad(w_in[:, off_a:off_u], ((0, 0), (0, LANES - 2 * nh)))
    w_cat = jnp.concatenate([w_in[:, :off_a], w_in[:, off_u:], w_ab], axis=1).astype(BF16)
    alog = jnp.pad(dn_a_log, (0, LANES - nh)).reshape(1, LANES)
    dtb = jnp.pad(dn_dt_bias, (0, LANES - nh)).reshape(1, LANES)

    u, o = _mixer_in(h2, seq, norm_mix_g.reshape(1, d), w_cat, conv_w, alog, dtb,
                     dn_norm_g.reshape(1, dh), nh=nh, dh=dh, tb=MIXER_ROWS)

    w1, co = _s5t_prep(s5_a_re, s5_a_im, s5_log_dt, s5_b_re, s5_b_im, s5_c_re, s5_c_im)
    tabs = _s5t_tables(s5_a_re, s5_a_im, s5_log_dt)
    y = _s5t(u, w1, co, tabs, s5_d.reshape(1, s5_w), s5_w_glu, s5_b_glu.reshape(1, s5_w),
             bsz, seq, rc=S5_BLOCKS)

    qk_mem, vw_mem = _mem_kv(mem, norm_mem_g.reshape(1, d), w_xk, w_xv, w_xq, w_xo)
    return _mix_attn(h2, o, y, w_out, norm_x_g.reshape(1, d), qk_mem, vw_mem, bsz, seq,
                     tm=ATTN_ROWS)


def kernel(x, mem, norm_mix_g, w_in, conv_w, dn_a_log, dn_dt_bias, dn_norm_g, s5_a_re, s5_a_im,
           s5_b_re, s5_b_im, s5_c_re, s5_c_im, s5_d, s5_log_dt, s5_w_glu, s5_b_glu, w_out,
           norm_x_g, norm_mem_g, w_xq, w_xk, w_xv, w_xo, norm_ffn_g, w_gate, w_up, w_down,
           norm_final_g):
    bsz, seq, d = x.shape
    depth = w_in.shape[0]
    h = x.reshape(bsz * seq, d)
    for l in range(depth):
        h = _layer(h, mem, bsz, seq, norm_mix_g[l], w_in[l], conv_w[l], dn_a_log[l],
                   dn_dt_bias[l], dn_norm_g[l], s5_a_re[l], s5_a_im[l], s5_b_re[l], s5_b_im[l],
                   s5_c_re[l], s5_c_im[l], s5_d[l], s5_log_dt[l], s5_w_glu[l], s5_b_glu[l],
                   w_out[l], norm_x_g[l], norm_mem_g[l], w_xq[l], w_xk[l], w_xv[l], w_xo[l])
        h = _ffn(h, norm_ffn_g[l].reshape(1, d), w_gate[l], w_up[l], w_down[l],
                 norm_final_g.reshape(1, d), tm=FFN_ROWS, final=l == depth - 1)
    return h.reshape(bsz, seq, d)
```

```python
import functools

import jax
import jax.numpy as jnp
from jax import lax
from jax.experimental import pallas as pl
from jax.experimental.pallas import tpu as pltpu

F32 = jnp.float32
BF16 = jnp.bfloat16
EPS = 1e-6
HIGHEST = lax.Precision.HIGHEST

LANES = 128
SUBLANES = 8
DN_CHUNK = 64
T_CHUNK = 16
S_CHUNK = 16
X_HEADS = 4
VMEM_LIMIT = 58 * 1024 * 1024

MIXER_ROWS = 256
S5_BLOCKS = 128
ATTN_ROWS = 1024
FFN_ROWS = 1024


def _fdot(a, b):
    return jnp.dot(a, b, precision=HIGHEST, preferred_element_type=F32)


def _rms(x, g):
    return x * lax.rsqrt(jnp.mean(x * x, axis=-1, keepdims=True) + EPS) * g


def _interleave(gens, stages_per_round=None):
    pending = [(gen, 1 if stages_per_round is None else stages_per_round[i])
               for i, gen in enumerate(gens)]
    while pending:
        for item in list(pending):
            gen, n = item
            try:
                for _ in range(n):
                    next(gen)
            except StopIteration:
                pending.remove(item)


def _params(*sem):
    return pltpu.CompilerParams(dimension_semantics=sem, vmem_limit_bytes=VMEM_LIMIT)


def _in_proj_stages(x_ref, g_ref, w_ref, cw_ref, alog_ref, dtb_ref,
                    q_ref, k_ref, kt_ref, v_ref, z_ref, u_ref, gt_ref,
                    buf_ref, *, tm, dn_w, nh, dh, cw_n):
    xnb = _rms(x_ref[...], g_ref[...]).astype(BF16)
    gw = 2 * dh

    def conv_silu(p, cols):
        outs = []
        for i in range(gw // LANES):
            c0 = cols.start + i * LANES
            s = c0 // LANES
            pc = p[:, i * LANES:(i + 1) * LANES]
            buf_ref[s, SUBLANES:SUBLANES + tm, :] = pc
            acc = cw_ref[cw_n - 1:cw_n, c0:c0 + LANES] * pc
            for j in range(cw_n - 1):
                off = SUBLANES - (cw_n - 1) + j
                acc = acc + cw_ref[j:j + 1, c0:c0 + LANES] * buf_ref[s, off:off + tm, :]
            buf_ref[s, 0:SUBLANES, :] = buf_ref[s, tm:tm + SUBLANES, :]
            outs.append(acc * jax.nn.sigmoid(acc))
        return jnp.concatenate(outs, axis=1)

    def l2n(a, scale):
        parts = []
        for i in range(gw // dh):
            ah = a[:, i * dh:(i + 1) * dh]
            parts.append(ah * (lax.rsqrt(jnp.sum(ah * ah, -1, keepdims=True) + EPS) * scale))
        return jnp.concatenate(parts, axis=1)

    def ep_q(p, c0):
        q_ref[:, c0:c0 + gw] = l2n(conv_silu(p, slice(c0, c0 + gw)), dh ** -0.5)

    def ep_k(p, c0):
        kn = l2n(conv_silu(p, slice(dn_w + c0, dn_w + c0 + gw)), 1.0)
        k_ref[:, c0:c0 + gw] = kn
        kt_ref[c0:c0 + gw, :] = kn.T

    def ep_v(p, c0):
        v_ref[:, c0:c0 + gw] = conv_silu(p, slice(2 * dn_w + c0, 2 * dn_w + c0 + gw))

    def ep_z(p, c0):
        z_ref[:, c0:c0 + gw] = p

    def ep_u(p, c0):
        for j in range(gw // LANES):
            u_ref[c0 // LANES + j] = p[:, j * LANES:(j + 1) * LANES]

    def ep_gates(ab, c0):
        lane = lax.broadcasted_iota(jnp.int32, ab.shape, 1)
        g = -jnp.exp(alog_ref[...]) * jnp.logaddexp(ab + dtb_ref[...], 0.0)
        gates = jnp.where(lane < nh, g, jax.nn.sigmoid(ab))
        gt_ref[...] = gates.T[0:SUBLANES, :]

    groups = []
    for i, ep in enumerate((ep_q, ep_k, ep_v, ep_z, ep_u)):
        groups += [(ep, i * dn_w + c0, gw, c0) for c0 in range(0, dn_w, gw)]
    groups.append((ep_gates, 5 * dn_w, LANES, 0))

    prev = None
    for ep, w0, width, c0 in groups:
        p = jnp.dot(xnb, w_ref[:, w0:w0 + width], preferred_element_type=F32)
        if prev is not None:
            prev[0](prev[1], prev[2])
        prev = (ep, p, c0)
        yield
    prev[0](prev[1], prev[2])


def _delta_stages(q_ref, k_ref, kt_ref, v_ref, z_ref, gt_ref, ng_ref,
                  o_ref, s_ref, u_sc, w_sc, qg_sc, attn_sc, kdt_sc, gtot_sc, wr, rd,
                  *, tb, nh, dh, ch):
    nc = tb // ch
    heads = range(nh)
    hls = [slice(h * dh, (h + 1) * dh) for h in heads]
    mm = functools.partial(jnp.dot, preferred_element_type=F32)

    def phase_b():
        gtot_b = gtot_sc[rd]
        s_all = [s_ref[h] for h in heads]
        for cidx in range(nc):
            rows = slice(cidx * ch, (cidx + 1) * ch)
            sb = [s_all[h].astype(BF16) for h in heads]
            ws = [mm(w_sc[rd, h, rows, :], sb[h]) for h in heads]
            qs = [mm(qg_sc[rd, h, rows, :], sb[h]) for h in heads]
            yield
            vb = [(u_sc[rd, h, rows, :] - ws[h]).astype(BF16) for h in heads]
            av = [mm(attn_sc[rd, h, rows, rows], vb[h]) for h in heads]
            kv = [mm(kdt_sc[rd, h, :, rows], vb[h]) for h in heads]
            yield
            outs = []
            for h in heads:
                a_last = jnp.exp(gtot_b[h:h + 1, cidx * ch:cidx * ch + 1])
                s_all[h] = s_all[h] * a_last + kv[h]
                o = qs[h] + av[h]
                o = o * lax.rsqrt(jnp.mean(o * o, -1, keepdims=True) + EPS)
                zh = z_ref[rows, hls[h]]
                outs.append(o * ng_ref[...] * (zh * jax.nn.sigmoid(zh)))
            o_ref[rows, :] = jnp.concatenate(outs, axis=1)
        for h in heads:
            s_ref[h] = s_all[h]

    r = lax.broadcasted_iota(jnp.int32, (tb, tb), 0)
    c = lax.broadcasted_iota(jnp.int32, (tb, tb), 1)
    same = (r // ch) == (c // ch)
    causal_bd = same & (c <= r)
    strict_bd = same & (c < r)
    gt = gt_ref[...]
    gcum_t = _fdot(gt, jnp.where(same & (r <= c), 1.0, 0.0))
    gtot_t = _fdot(gt, jnp.where(same, 1.0, 0.0))
    gcum = gcum_t.T
    gates = gt.T

    def pack(x):
        acc = x[0:ch, :]
        for i in range(1, nc):
            acc = acc + x[i * ch:(i + 1) * ch, :]
        return acc

    def unpack(x):
        return jnp.where(same, jnp.concatenate([x] * nc, axis=0), 0.0).astype(BF16)

    n_sq = max(1, (ch - 1).bit_length() - 1)

    def phase_a():
        gcol = [gcum[:, h:h + 1] for h in heads]
        grow = [gcum_t[h:h + 1, :] for h in heads]
        beta = [gates[:, nh + h:nh + h + 1] for h in heads]
        kb = [k_ref[:, hls[h]] * beta[h] for h in heads]
        gram = [mm(jnp.concatenate([kb[h], q_ref[:, hls[h]]], axis=0).astype(BF16),
                   kt_ref[hls[h], :].astype(BF16)) for h in heads]
        yield
        ppk, apk = [], []
        for h in heads:
            decay = jnp.exp(jnp.where(causal_bd, gcol[h] - grow[h], -jnp.inf))
            p_bd = -jnp.where(strict_bd, gram[h][:tb] * decay, 0.0)
            attn_sc[wr, h] = (gram[h][tb:] * decay).astype(BF16)
            pk = pack(p_bd)
            apk.append(pk)
            ppk.append(mm(pk.astype(BF16), p_bd.astype(BF16)))
        yield
        for j in range(1, n_sq + 1):
            res = []
            for h in heads:
                lhs = jnp.concatenate([apk[h], ppk[h]], axis=0) if j < n_sq else apk[h]
                res.append(mm(lhs.astype(BF16), unpack(ppk[h])))
            yield
            for h in heads:
                apk[h] = apk[h] + ppk[h] + res[h][:ch]
                if j < n_sq:
                    ppk[h] = res[h][ch:]
        egc = [jnp.exp(gcol[h]) for h in heads]
        rhs = [jnp.concatenate([v_ref[:, hls[h]] * beta[h], kb[h] * egc[h]], axis=1) for h in heads]
        prod = [mm(unpack(apk[h]), rhs[h].astype(BF16)) for h in heads]
        yield
        for h in heads:
            uw = rhs[h] + prod[h]
            u_sc[wr, h] = uw[:, :dh]
            w_sc[wr, h] = uw[:, dh:].astype(BF16)
            qg_sc[wr, h] = (q_ref[:, hls[h]] * egc[h]).astype(BF16)
            kdt_sc[wr, h] = (kt_ref[hls[h], :] * jnp.exp(gtot_t[h:h + 1, :] - grow[h])).astype(BF16)
        gtot_sc[wr] = gtot_t

    return phase_b(), phase_a()


def _mixer_in_body(x_ref, g_ref, w_ref, cw_ref, alog_ref, dtb_ref, ng_ref,
                   u_ref, o_ref,
                   buf_ref, q_sc, k_sc, kt_sc, v_sc, z_sc, gt_sc,
                   s_ref, u_sc, w_sc, qg_sc, attn_sc, kdt_sc, gtot_sc,
                   *, tb, n_t, nh, dh, cw_n):
    s = pl.program_id(0)
    dn_w = nh * dh

    @pl.when(s == 0)
    def _():
        for ref in (buf_ref, q_sc, k_sc, kt_sc, v_sc, z_sc, gt_sc,
                    s_ref, u_sc, w_sc, qg_sc, attn_sc, kdt_sc, gtot_sc):
            ref[...] = jnp.zeros(ref.shape, ref.dtype)

    @pl.when(s % n_t == 0)
    def _():
        buf_ref[:, 0:SUBLANES, :] = jnp.zeros((3 * dn_w // LANES, SUBLANES, LANES), F32)

    @pl.when((s - 2) % n_t == 0)
    def _():
        s_ref[...] = jnp.zeros(s_ref.shape, F32)

    cur = s % 2
    prv = 1 - cur
    zw = s % 3
    zr = (s + 1) % 3
    proj = _in_proj_stages(
        x_ref, g_ref, w_ref, cw_ref, alog_ref, dtb_ref,
        q_sc.at[cur], k_sc.at[cur], kt_sc.at[cur], v_sc.at[cur], z_sc.at[zw], u_ref, gt_sc.at[cur],
        buf_ref, tm=tb, dn_w=dn_w, nh=nh, dh=dh, cw_n=cw_n)
    phase_b, phase_a = _delta_stages(
        q_sc.at[prv], k_sc.at[prv], kt_sc.at[prv], v_sc.at[prv], z_sc.at[zr], gt_sc.at[prv], ng_ref,
        o_ref, s_ref, u_sc, w_sc, qg_sc, attn_sc, kdt_sc, gtot_sc, cur, prv,
        tb=tb, nh=nh, dh=dh, ch=DN_CHUNK)
    _interleave([proj, phase_b, phase_a])


def _mixer_in(x2, seq, g, w_cat, conv_w, alog, dtb, ng, *, nh, dh, tb):
    nt, d = x2.shape
    dn_w = nh * dh
    n_blk = nt // tb
    cw_n = conv_w.shape[0]
    const = lambda s: (0, 0)
    in_blk = lambda s: (jnp.minimum(s, n_blk - 1), 0)
    out_blk = lambda s: (jnp.maximum(s - 2, 0), 0)
    body = functools.partial(_mixer_in_body, tb=tb, n_t=seq // tb, nh=nh, dh=dh, cw_n=cw_n)
    two = lambda *shape: pltpu.VMEM((2,) + shape, F32)
    return pl.pallas_call(
        body,
        grid=(n_blk + 2,),
        in_specs=[pl.BlockSpec((tb, d), in_blk)]
        + [pl.BlockSpec(a.shape, const) for a in (g, w_cat, conv_w, alog, dtb, ng)],
        out_specs=[pl.BlockSpec((dn_w // LANES, tb, LANES),
                                lambda s: (0, jnp.minimum(s, n_blk - 1), 0)),
                   pl.BlockSpec((tb, dn_w), out_blk)],
        out_shape=[jax.ShapeDtypeStruct((dn_w // LANES, nt, LANES), F32),
                   jax.ShapeDtypeStruct((nt, dn_w), F32)],
        scratch_shapes=[
            pltpu.VMEM((3 * dn_w // LANES, tb + SUBLANES, LANES), F32),
            two(tb, dn_w), two(tb, dn_w), two(dn_w, tb), two(tb, dn_w),
            pltpu.VMEM((3, tb, dn_w), F32),
            two(SUBLANES, tb),
            pltpu.VMEM((nh, dh, dh), F32),
            pltpu.VMEM((2, nh, tb, dh), F32),
            pltpu.VMEM((2, nh, tb, dh), BF16),
            pltpu.VMEM((2, nh, tb, dh), BF16),
            pltpu.VMEM((2, nh, tb, tb), BF16),
            pltpu.VMEM((2, nh, dh, tb), BF16),
            pltpu.VMEM((2, SUBLANES, tb), F32),
        ],
        compiler_params=_params("arbitrary"),
        name="mixer_in",
    )(x2, g, w_cat, conv_w, alog, dtb, ng)


def _ftdot(a, b):
    return lax.dot_general(a.astype(BF16), b.astype(BF16), (((0,), (0,)), ((), ())),
                           preferred_element_type=F32)


def _cmul(xr, xi, yr, yi):
    return xr * yr - xi * yi, xr * yi + xi * yr


def _s5t_prep_body(are_ref, aim_ref, ldt_ref, btr_ref, bti_ref, ctr_ref, cti_ref, w1_ref, co_ref,
                   *, tc, cg, n_p):
    n_grp = w1_ref.shape[0]
    kw = tc * cg
    a_re = are_ref[...]
    a_im = aim_ref[...]
    dt = jnp.exp(ldt_ref[...])
    lre = a_re * dt
    lim = a_im * dt
    mag = jnp.exp(lre)
    ab = (mag * jnp.cos(lim), mag * jnp.sin(lim))
    nr, ni = ab[0] - 1.0, ab[1]
    den = a_re * a_re + a_im * a_im
    co = ((nr * a_re + ni * a_im) / den, (ni * a_re - nr * a_im) / den)
    inv_mag2 = jnp.exp(-2.0 * lre)
    ai = (ab[0] * inv_mag2, -ab[1] * inv_mag2)
    n_bits = tc.bit_length() - 1
    pos, neg = [ab], [ai]
    for _ in range(n_bits - 1):
        pos.append(_cmul(*pos[-1], *pos[-1]))
        neg.append(_cmul(*neg[-1], *neg[-1]))
    top = pos[0]
    for f in pos[1:]:
        top = _cmul(*top, *f)
    rows = [co, top] + pos + neg
    flat = [part for pair in rows for part in pair]
    cols = []
    for i in range(0, len(flat), SUBLANES):
        blk = flat[i:i + SUBLANES]
        blk = blk + [blk[-1]] * (SUBLANES - len(blk))
        t = jnp.concatenate(blk, axis=0).T
        cols += [t[:, j:j + 1] for j in range(SUBLANES)]
    pair = lambda k: (cols[2 * k], cols[2 * k + 1])
    co_c, top_c = pair(0), pair(1)
    pos_c = [pair(2 + k) for k in range(n_bits)]
    neg_c = [pair(2 + n_bits + k) for k in range(n_bits)]

    slot = lax.broadcasted_iota(jnp.int32, (1, kw), 1) // cg

    def power(factors):
        acc = None
        for k, (fr, fi) in enumerate(factors):
            bit = ((slot >> k) & 1) == 1
            term = (jnp.where(bit, fr, 1.0), jnp.where(bit, fi, 0.0))
            acc = term if acc is None else _cmul(*acc, *term)
        return acc

    place = jnp.where(lax.broadcasted_iota(jnp.int32, (cg, kw), 0)
                      == lax.broadcasted_iota(jnp.int32, (cg, kw), 1) % cg, 1.0, 0.0)
    place = place.astype(BF16)

    def rep(ref):
        x = ref[...]
        hi = x.astype(BF16)
        r1 = x - hi.astype(F32)
        mid = r1.astype(BF16)
        lo = (r1 - mid.astype(F32)).astype(BF16)
        return sum(jnp.dot(part, place, preferred_element_type=F32) for part in (hi, mid, lo))
    bt = _cmul(*co_c, rep(btr_ref), rep(bti_ref))
    pb = _cmul(*bt, *power(neg_c))
    qc = _cmul(rep(ctr_ref), rep(cti_ref), *power(pos_c))
    q1 = _cmul(*qc, *pos_c[0])
    bi = _cmul(*pb, *top_c)
    r2 = lax.broadcasted_iota(jnp.int32, (kw, kw), 0) // cg
    c2 = lax.broadcasted_iota(jnp.int32, (kw, kw), 1) // cg
    eye = jnp.where(lax.broadcasted_iota(jnp.int32, (n_p, n_p), 0)
                    == lax.broadcasted_iota(jnp.int32, (n_p, n_p), 1), 1.0, 0.0)
    for g in range(n_grp):
        rs = slice(g * n_p, (g + 1) * n_p)
        m_full = _ftdot(pb[0][rs], qc[0][rs]) - _ftdot(pb[1][rs], qc[1][rs])
        w1_ref[g, :, 0:kw] = jnp.where(r2 <= c2, m_full, 0.0).astype(w1_ref.dtype)
        w1_ref[g, :, kw:kw + n_p] = _ftdot(bi[0][rs], eye).astype(w1_ref.dtype)
        w1_ref[g, :, kw + n_p:kw + 2 * n_p] = _ftdot(bi[1][rs], eye).astype(w1_ref.dtype)
        co_ref[g, 0:n_p, :] = q1[0][rs].astype(co_ref.dtype)
        co_ref[g, n_p:2 * n_p, :] = (-q1[1][rs]).astype(co_ref.dtype)


def _s5t_prep(a_re, a_im, log_dt, b_re, b_im, c_re, c_im):
    g, p = a_re.shape
    cg = b_re.shape[-1]
    kw = T_CHUNK * cg
    row = lambda a: a.reshape(1, g * p)
    slots = lambda a: a.reshape(g * p, cg)
    return pl.pallas_call(
        functools.partial(_s5t_prep_body, tc=T_CHUNK, cg=cg, n_p=p),
        out_shape=[jax.ShapeDtypeStruct((g, kw, kw + 2 * p), BF16),
                   jax.ShapeDtypeStruct((g, 2 * p, kw), BF16)],
        compiler_params=pltpu.CompilerParams(vmem_limit_bytes=VMEM_LIMIT),
        name="s5t_prep",
    )(row(a_re), row(a_im), row(jnp.broadcast_to(log_dt[:, None], (g, p))),
      slots(b_re), slots(b_im), slots(c_re.transpose(0, 2, 1)), slots(c_im.transpose(0, 2, 1)))


def _s5t_tables_body(are_ref, aim_ref, ldt_ref, pin_re, pin_im, pout_re, pout_im, al_re, al_im,
                     *, tc, sc):
    lre = are_ref[...] * jnp.exp(ldt_ref[...]) * float(tc)
    lim = aim_ref[...] * jnp.exp(ldt_ref[...]) * float(tc)
    lane = lax.broadcasted_iota(jnp.int32, lre.shape, 1)
    sign = jnp.where((lane % LANES) < LANES // 2, -1.0, 1.0)
    n = lax.broadcasted_iota(jnp.int32, (sc, 1), 0).astype(F32) - float(sc // 2)
    m_out = jnp.exp(lre * n)
    m_in = jnp.exp(-(lre * n))
    ang = lim * n
    pin_re[...] = (m_in * jnp.cos(ang)).astype(pin_re.dtype)
    pin_im[...] = (sign * (-(m_in * jnp.sin(ang)))).astype(pin_im.dtype)
    pout_re[...] = m_out * jnp.cos(ang)
    pout_im[...] = sign * (m_out * jnp.sin(ang))
    m_al = jnp.exp(lre * float(sc))
    al_re[...] = m_al * jnp.cos(lim * float(sc))
    al_im[...] = sign * (m_al * jnp.sin(lim * float(sc)))


def _s5t_tables(a_re, a_im, log_dt):
    g, p = a_re.shape
    lanes = g * 2 * p
    spread = lambda a: jnp.broadcast_to(a[:, None, :], (g, 2, p)).reshape(1, lanes)
    tab16 = jax.ShapeDtypeStruct((S_CHUNK, lanes), BF16)
    tab32 = jax.ShapeDtypeStruct((S_CHUNK, lanes), F32)
    one = jax.ShapeDtypeStruct((1, lanes), F32)
    return pl.pallas_call(
        functools.partial(_s5t_tables_body, tc=T_CHUNK, sc=S_CHUNK),
        out_shape=[tab16, tab16, tab32, tab32, one, one], name="s5t_tables",
    )(spread(a_re), spread(a_im), spread(jnp.broadcast_to(log_dt[:, None], (g, p))))


def _atom_transpose(regs, atom, width):
    regs = list(regs)
    n = len(regs)
    d = n // 2
    while d:
        low = (atom & d) == 0
        for j in range(n):
            if j & d:
                continue
            a, b = regs[j], regs[j + d]
            regs[j] = jnp.where(low, a, pltpu.roll(b, d * width, axis=1))
            regs[j + d] = jnp.where(low, pltpu.roll(a, LANES - d * width, axis=1), b)
        d //= 2
    return regs


def _swap_halves(x):
    tiles = [pltpu.roll(x[:, t * LANES:(t + 1) * LANES], LANES // 2, axis=1)
             for t in range(x.shape[1] // LANES)]
    return jnp.concatenate(tiles, axis=1)


def _s5t_body(u_ref, w1_ref, co_ref, pin_re, pin_im, pout_re, pout_im, al_re, al_im,
              d_ref, wglu_ref, bglu_ref, y_ref,
              upk_ref, yloc_ref, sin_ref, xs_ref, carry_ref, udi_ref, *, rc, tc, sc, n_grp, cg):
    t = pl.program_id(1)
    kw = tc * cg
    n_q = u_ref.shape[0]
    atoms = LANES // cg

    @pl.when(t == 0)
    def _():
        carry_ref[...] = jnp.zeros(carry_ref.shape, F32)
        xs_ref[:, 0:SUBLANES, :] = jnp.zeros((n_grp, SUBLANES, LANES), F32)

    atom = lax.broadcasted_iota(jnp.int32, (rc, LANES), 1) // cg

    n_a = tc // atoms

    def pack():
        for q in range(n_q):
            for a in range(n_a):
                planes = [u_ref[q, pl.ds(a * atoms + k, rc, stride=tc), :] for k in range(atoms)]
                for k in range(atoms):
                    udi_ref[q, a * atoms + k] = planes[k]
                for r, tile in enumerate(_atom_transpose(planes, atom, cg)):
                    upk_ref[q * atoms + r, :, a * LANES:(a + 1) * LANES] = tile.astype(BF16)
                yield

    per_stage = atoms // n_a

    def local():
        for _ in range(n_a):
            yield
        for g in range(n_grp):
            res = jnp.dot(upk_ref[g], w1_ref[g], preferred_element_type=F32)
            yloc_ref[g] = res[:, :kw]
            sin_ref[g] = res[:, kw:]
            if g % per_stage == per_stage - 1:
                yield

    _interleave([pack(), local()])

    ri = lax.broadcasted_iota(jnp.int32, (sc, sc), 0)
    ci = lax.broadcasted_iota(jnp.int32, (sc, sc), 1)
    tri = jnp.where(ci <= ri, 1.0, 0.0).astype(BF16)
    n_c2 = rc // sc
    ss = [jnp.concatenate([sin_ref[g, c2 * sc:(c2 + 1) * sc, :] for g in range(n_grp)], axis=1)
          for c2 in range(n_c2)]
    ss_sw = [_swap_halves(s) for s in ss]
    xin = [pin_re[...] * ss[c2].astype(BF16) + pin_im[...] * ss_sw[c2].astype(BF16)
           for c2 in range(n_c2)]
    psum = [jnp.dot(tri, x, preferred_element_type=F32) for x in xin]
    psum_sw = [_swap_halves(p) for p in psum]
    carry = carry_ref[...]
    carry_sw = _swap_halves(carry)
    carries = []
    for c2 in range(rc // sc):
        carries.append((carry, carry_sw))
        last = psum[c2][sc - 1:sc, :] + carry
        last_sw = psum_sw[c2][sc - 1:sc, :] + carry_sw
        carry = al_re[...] * last + al_im[...] * last_sw
        carry_sw = al_re[...] * last_sw - al_im[...] * last
    carry_ref[...] = carry
    for c2 in range(rc // sc):
        x = (pout_re[...] * (psum[c2] + carries[c2][0])
             + pout_im[...] * (psum_sw[c2] + carries[c2][1]))
        for g in range(n_grp):
            xs_ref[g, SUBLANES + c2 * sc:SUBLANES + (c2 + 1) * sc, :] = x[:, g * LANES:(g + 1) * LANES]

    for g in range(n_grp):
        xprev = xs_ref[g, SUBLANES - 1:SUBLANES - 1 + rc, :].astype(BF16)
        yloc_ref[g] = yloc_ref[g] + jnp.dot(xprev, co_ref[g], preferred_element_type=F32)
    xs_ref[:, SUBLANES - 1:SUBLANES, :] = xs_ref[:, SUBLANES - 1 + rc:SUBLANES + rc, :]

    wglu = wglu_ref[...].astype(BF16)

    def unpack(a):
        for _ in range(a * n_q):
            yield
        planes = [[None] * n_q for _ in range(atoms)]
        for q in range(n_q):
            tiles = [yloc_ref[q * atoms + r, :, a * LANES:(a + 1) * LANES] for r in range(atoms)]
            for k, tile in enumerate(_atom_transpose(tiles, atom, cg)):
                planes[k][q] = tile
            yield
        for k in range(atoms):
            i = a * atoms + k
            uu = jnp.concatenate([udi_ref[q, i] for q in range(n_q)], axis=1)
            y = jax.nn.gelu(jnp.concatenate(planes[k], axis=1) + d_ref[...] * uu)
            gate = jax.nn.sigmoid(jnp.dot(y.astype(BF16), wglu, preferred_element_type=F32)
                                  + bglu_ref[...])
            out = y * gate
            for q in range(n_q):
                y_ref[q, pl.ds(i, rc, stride=tc), :] = out[:, q * LANES:(q + 1) * LANES]
            if k % 2 == 1:
                yield

    _interleave([unpack(a) for a in range(n_a)])


def _s5t(u4, w1, co, tabs, d, wglu, bglu, bsz, seq, *, rc):
    n_q, nt, _ = u4.shape
    n_grp, kw, wcols = w1.shape
    n_state = (wcols - kw) // 2
    cg = kw // T_CHUNK
    rows = rc * T_CHUNK
    n_t = seq // rows
    blk = pl.BlockSpec((n_q, rows, LANES), lambda b, t: (0, b * n_t + t, 0))
    c2 = lambda b, t: (0, 0)
    c3 = lambda b, t: (0, 0, 0)
    whole = lambda a: pl.BlockSpec(a.shape, c3 if a.ndim == 3 else c2)
    body = functools.partial(_s5t_body, rc=rc, tc=T_CHUNK, sc=S_CHUNK, n_grp=n_grp, cg=cg)
    return pl.pallas_call(
        body,
        grid=(bsz, n_t),
        in_specs=[blk] + [whole(a) for a in (w1, co, *tabs, d, wglu, bglu)],
        out_specs=blk,
        out_shape=jax.ShapeDtypeStruct((n_q, nt, LANES), F32),
        scratch_shapes=[
            pltpu.VMEM((n_grp, rc, kw), BF16),
            pltpu.VMEM((n_grp, rc, kw), F32),
            pltpu.VMEM((n_grp, rc, 2 * n_state), F32),
            pltpu.VMEM((n_grp, rc + SUBLANES, 2 * n_state), F32),
            pltpu.VMEM((1, n_grp * 2 * n_state), F32),
            pltpu.VMEM((n_q, T_CHUNK, rc, LANES), F32),
        ],
        compiler_params=_params("arbitrary", "arbitrary"),
        name="s5t",
    )(u4, w1, co, *tabs, d, wglu, bglu)


def _mem_kv_body(m_ref, g_ref, wk_ref, wv_ref, wq_ref, wo_ref, qk_ref, vw_ref, w_sc, *, xh):
    @pl.when(pl.program_id(0) == 0)
    def _():
        for i, ref in enumerate((wk_ref, wv_ref, wq_ref, wo_ref)):
            w_sc[i] = ref[...].astype(BF16)

    _, n_mem, d = m_ref.shape
    xd = d // xh
    scale = xd ** -0.5
    mn = _rms(m_ref[0], g_ref[...]).astype(BF16)
    k = jnp.dot(mn, w_sc[0], preferred_element_type=F32).astype(BF16)
    v = jnp.dot(mn, w_sc[1], preferred_element_type=F32).astype(BF16)
    nt_dims = (((1,), (1,)), ((), ()))
    for h in range(xh):
        hl = slice(h * xd, (h + 1) * xd)
        ml = slice(h * n_mem, (h + 1) * n_mem)
        qk = lax.dot_general(w_sc[2, :, hl], k[:, hl], nt_dims, preferred_element_type=F32)
        qk_ref[0, :, ml] = (qk * scale).astype(BF16)
        vw_ref[0, ml, :] = jnp.dot(v[:, hl], w_sc[3, hl, :],
                                   preferred_element_type=F32).astype(BF16)


def _mem_kv(mem, g, wk, wv, wq, wo):
    bsz, n_mem, d = mem.shape
    hm = X_HEADS * n_mem
    const = lambda b: (0, 0)
    weight = pl.BlockSpec((d, d), const, pipeline_mode=pl.Buffered(1))
    return pl.pallas_call(
        functools.partial(_mem_kv_body, xh=X_HEADS),
        grid=(bsz,),
        in_specs=[pl.BlockSpec((1, n_mem, d), lambda b: (b, 0, 0)), pl.BlockSpec((1, d), const),
                  weight, weight, weight, weight],
        out_specs=[pl.BlockSpec((1, d, hm), lambda b: (b, 0, 0)),
                   pl.BlockSpec((1, hm, d), lambda b: (b, 0, 0))],
        out_shape=[
            jax.ShapeDtypeStruct((bsz, d, hm), BF16),
            jax.ShapeDtypeStruct((bsz, hm, d), BF16),
        ],
        scratch_shapes=[pltpu.VMEM((4, d, d), BF16)],
        compiler_params=_params("arbitrary"),
        name="mem_kv",
    )(mem, g, wk, wv, wq, wo)


def _mix_attn_body(x_ref, o_ref, y_ref, wo_ref, gx_ref, qk_ref, vw_ref, h_ref, *, xh, n_sub):
    dn_w = o_ref.shape[1]
    n_mem = qk_ref.shape[2] // xh
    sub = x_ref.shape[0] // n_sub
    mm = functools.partial(jnp.dot, preferred_element_type=F32)
    wo = wo_ref[...].astype(BF16)

    def rows_gen(r0):
        rows = slice(r0, r0 + sub)
        y = jnp.concatenate([y_ref[q, rows, :] for q in range(y_ref.shape[0])], axis=1)
        mix = mm(o_ref[rows, :].astype(BF16), wo[0:dn_w, :]) + mm(y.astype(BF16), wo[dn_w:, :])
        yield
        h1 = x_ref[rows, :] + mix
        s = mm(_rms(h1, gx_ref[...]).astype(BF16), qk_ref[0])
        yield
        parts = []
        for h in range(xh):
            sh = s[:, h * n_mem:(h + 1) * n_mem]
            e = jnp.exp(sh - jnp.max(sh, axis=-1, keepdims=True))
            parts.append((e / jnp.sum(e, axis=-1, keepdims=True)).astype(BF16))
        att = mm(jnp.concatenate(parts, axis=1), vw_ref[0])
        yield
        h_ref[rows, :] = h1 + att

    _interleave([rows_gen(i * sub) for i in range(n_sub)])


def _mix_attn(x2, o, y, w_out, gx, qk, vw, bsz, seq, *, tm):
    nt, d = x2.shape
    dn_w = o.shape[1]
    n_t = seq // tm
    row = lambda b, t: (b * n_t + t, 0)
    c2 = lambda b, t: (0, 0)
    per_batch = lambda a: pl.BlockSpec((1,) + a.shape[1:], lambda b, t: (b, 0, 0))
    body = functools.partial(_mix_attn_body, xh=X_HEADS, n_sub=2)
    return pl.pallas_call(
        body,
        grid=(bsz, n_t),
        in_specs=[
            pl.BlockSpec((tm, d), row),
            pl.BlockSpec((tm, dn_w), row),
            pl.BlockSpec((y.shape[0], tm, LANES), lambda b, t: (0, b * n_t + t, 0)),
            pl.BlockSpec(w_out.shape, c2, pipeline_mode=pl.Buffered(1)),
            pl.BlockSpec((1, d), c2),
            per_batch(qk),
            per_batch(vw),
        ],
        out_specs=pl.BlockSpec((tm, d), row),
        out_shape=jax.ShapeDtypeStruct((nt, d), F32),
        compiler_params=_params("arbitrary", "arbitrary"),
        name="mix_attn",
    )(x2, o, y, w_out, gx, qk, vw)


def _ffn_body(h_ref, gf_ref, wg_ref, wu_ref, wd_ref, gl_ref, out_ref, *, final, fc):
    h = h_ref[...]
    hn = _rms(h, gf_ref[...]).astype(BF16)
    dff = wg_ref.shape[1]

    def down(gate, up, c0):
        act = (gate * jax.nn.sigmoid(gate) * up).astype(BF16)
        return jnp.dot(act, wd_ref[c0:c0 + fc, :].astype(BF16), preferred_element_type=F32)

    h3 = h
    prev = None
    for c0 in range(0, dff, fc):
        gate = jnp.dot(hn, wg_ref[:, c0:c0 + fc].astype(BF16), preferred_element_type=F32)
        up = jnp.dot(hn, wu_ref[:, c0:c0 + fc].astype(BF16), preferred_element_type=F32)
        if prev is not None:
            h3 = h3 + down(*prev)
        prev = (gate, up, c0)
    h3 = h3 + down(*prev)
    out_ref[...] = _rms(h3, gl_ref[...]) if final else h3


def _ffn(h, gf, wg, wu, wd, gl, *, tm, final):
    nt, d = h.shape
    dff = wg.shape[1]
    c2 = lambda i: (0, 0)
    row = lambda i: (i, 0)
    return pl.pallas_call(
        functools.partial(_ffn_body, final=final, fc=2 * LANES),
        grid=(nt // tm,),
        in_specs=[
            pl.BlockSpec((tm, d), row),
            pl.BlockSpec((1, d), c2),
            pl.BlockSpec((d, dff), c2, pipeline_mode=pl.Buffered(1)),
            pl.BlockSpec((d, dff), c2, pipeline_mode=pl.Buffered(1)),
            pl.BlockSpec((dff, d), c2, pipeline_mode=pl.Buffered(1)),
            pl.BlockSpec((1, d), c2),
        ],
        out_specs=pl.BlockSpec((tm, d), row),
        out_shape=jax.ShapeDtypeStruct((nt, d), F32),
        compiler_params=_params("arbitrary"),
        name="ffn",
    )(h, gf, wg, wu, wd, gl)


def _layer(h2, mem, bsz, seq, norm_mix_g, w_in, conv_w, dn_a_log, dn_dt_bias, dn_norm_g,
           s5_a_re, s5_a_im, s5_b_re, s5_b_im, s5_c_re, s5_c_im, s5_d, s5_log_dt,
           s5_w_glu, s5_b_glu, w_out, norm_x_g, norm_mem_g, w_xq, w_xk, w_xv, w_xo):
    d = h2.shape[1]
    nh = dn_a_log.shape[0]
    dh = dn_norm_g.shape[0]
    dn_w = nh * dh
    s5_w = s5_a_re.shape[0] * s5_b_re.shape[-1]

    off_a = 4 * dn_w
    off_u = off_a + 2 * nh
    w_ab = jnp.pad(w_in[:, off_a:off_u].astype(BF16), ((0, 0), (0, LANES - 2 * nh)))
    w_cat = jnp.concatenate([w_in[:, :off_a].astype(BF16), w_in[:, off_u:].astype(BF16), w_ab],
                            axis=1)
    alog = jnp.pad(dn_a_log, (0, LANES - nh)).reshape(1, LANES)
    dtb = jnp.pad(dn_dt_bias, (0, LANES - nh)).reshape(1, LANES)

    u, o = _mixer_in(h2, seq, norm_mix_g.reshape(1, d), w_cat, conv_w, alog, dtb,
                     dn_norm_g.reshape(1, dh), nh=nh, dh=dh, tb=MIXER_ROWS)

    w1, co = _s5t_prep(s5_a_re, s5_a_im, s5_log_dt, s5_b_re, s5_b_im, s5_c_re, s5_c_im)
    tabs = _s5t_tables(s5_a_re, s5_a_im, s5_log_dt)
    y = _s5t(u, w1, co, tabs, s5_d.reshape(1, s5_w), s5_w_glu, s5_b_glu.reshape(1, s5_w),
             bsz, seq, rc=S5_BLOCKS)

    qk_mem, vw_mem = _mem_kv(mem, norm_mem_g.reshape(1, d), w_xk, w_xv, w_xq, w_xo)
    return _mix_attn(h2, o, y, w_out, norm_x_g.reshape(1, d), qk_mem, vw_mem, bsz, seq,
                     tm=ATTN_ROWS)


def kernel(x, mem, norm_mix_g, w_in, conv_w, dn_a_log, dn_dt_bias, dn_norm_g, s5_a_re, s5_a_im,
           s5_b_re, s5_b_im, s5_c_re, s5_c_im, s5_d, s5_log_dt, s5_w_glu, s5_b_glu, w_out,
           norm_x_g, norm_mem_g, w_xq, w_xk, w_xv, w_xo, norm_ffn_g, w_gate, w_up, w_down,
           norm_final_g):
    bsz, seq, d = x.shape
    depth = w_in.shape[0]
    h = x.reshape(bsz * seq, d)
    for l in range(depth):
        h = _layer(h, mem, bsz, seq, norm_mix_g[l], w_in[l], conv_w[l], dn_a_log[l],
                   dn_dt_bias[l], dn_norm_g[l], s5_a_re[l], s5_a_im[l], s5_b_re[l], s5_b_im[l],
                   s5_c_re[l], s5_c_im[l], s5_d[l], s5_log_dt[l], s5_w_glu[l], s5_b_glu[l],
                   w_out[l], norm_x_g[l], norm_mem_g[l], w_xq[l], w_xk[l], w_xv[l], w_xo[l])
        h = _ffn(h, norm_ffn_g[l].reshape(1, d), w_gate[l], w_up[l], w_down[l],
                 norm_final_g.reshape(1, d), tm=FFN_ROWS, final=l == depth - 1)
    return h.reshape(bsz, seq, d)
```

```python
import functools

import jax
import jax.numpy as jnp
from jax import lax
from jax.experimental import pallas as pl
from jax.experimental.pallas import tpu as pltpu

F32 = jnp.float32
BF16 = jnp.bfloat16
EPS = 1e-6
HIGHEST = lax.Precision.HIGHEST

LANES = 128
SUBLANES = 8
DN_CHUNK = 64
T_CHUNK = 16
S_CHUNK = 16
X_HEADS = 4
VMEM_LIMIT = 58 * 1024 * 1024

MIXER_ROWS = 256
S5_BLOCKS = 128
ATTN_ROWS = 1024
FFN_ROWS = 1024


def _fdot(a, b):
    return jnp.dot(a, b, precision=HIGHEST, preferred_element_type=F32)


def _rms(x, g):
    return x * lax.rsqrt(jnp.mean(x * x, axis=-1, keepdims=True) + EPS) * g


def _interleave(gens, stages_per_round=None):
    pending = [(gen, 1 if stages_per_round is None else stages_per_round[i])
               for i, gen in enumerate(gens)]
    while pending:
        for item in list(pending):
            gen, n = item
            try:
                for _ in range(n):
                    next(gen)
            except StopIteration:
                pending.remove(item)


def _params(*sem):
    return pltpu.CompilerParams(dimension_semantics=sem, vmem_limit_bytes=VMEM_LIMIT)


def _in_proj_stages(x_ref, g_ref, w_ref, cw_ref, alog_ref, dtb_ref,
                    q_ref, k_ref, kt_ref, v_ref, z_ref, u_ref, gt_ref,
                    buf_ref, *, tm, dn_w, nh, dh, cw_n):
    xnb = _rms(x_ref[...], g_ref[...]).astype(BF16)
    gw = 2 * dh

    def conv_silu(p, cols):
        outs = []
        for i in range(gw // LANES):
            c0 = cols.start + i * LANES
            s = c0 // LANES
            pc = p[:, i * LANES:(i + 1) * LANES]
            buf_ref[s, SUBLANES:SUBLANES + tm, :] = pc
            acc = cw_ref[cw_n - 1:cw_n, c0:c0 + LANES] * pc
            for j in range(cw_n - 1):
                off = SUBLANES - (cw_n - 1) + j
                acc = acc + cw_ref[j:j + 1, c0:c0 + LANES] * buf_ref[s, off:off + tm, :]
            buf_ref[s, 0:SUBLANES, :] = buf_ref[s, tm:tm + SUBLANES, :]
            outs.append(acc * jax.nn.sigmoid(acc))
        return jnp.concatenate(outs, axis=1)

    def l2n(a, scale):
        parts = []
        for i in range(gw // dh):
            ah = a[:, i * dh:(i + 1) * dh]
            parts.append(ah * (lax.rsqrt(jnp.sum(ah * ah, -1, keepdims=True) + EPS) * scale))
        return jnp.concatenate(parts, axis=1)

    def ep_q(p, c0):
        q_ref[:, c0:c0 + gw] = l2n(conv_silu(p, slice(c0, c0 + gw)), dh ** -0.5)

    def ep_k(p, c0):
        kn = l2n(conv_silu(p, slice(dn_w + c0, dn_w + c0 + gw)), 1.0)
        k_ref[:, c0:c0 + gw] = kn
        kt_ref[c0:c0 + gw, :] = kn.T

    def ep_v(p, c0):
        v_ref[:, c0:c0 + gw] = conv_silu(p, slice(2 * dn_w + c0, 2 * dn_w + c0 + gw))

    def ep_z(p, c0):
        z_ref[:, c0:c0 + gw] = p

    def ep_u(p, c0):
        for j in range(gw // LANES):
            u_ref[c0 // LANES + j] = p[:, j * LANES:(j + 1) * LANES]

    def ep_gates(ab, c0):
        lane = lax.broadcasted_iota(jnp.int32, ab.shape, 1)
        g = -jnp.exp(alog_ref[...]) * jnp.logaddexp(ab + dtb_ref[...], 0.0)
        gates = jnp.where(lane < nh, g, jax.nn.sigmoid(ab))
        gt_ref[...] = gates.T[0:SUBLANES, :]

    groups = []
    for i, ep in enumerate((ep_q, ep_k, ep_v, ep_z, ep_u)):
        groups += [(ep, i * dn_w + c0, gw, c0) for c0 in range(0, dn_w, gw)]
    groups.append((ep_gates, 5 * dn_w, LANES, 0))

    prev = None
    for ep, w0, width, c0 in groups:
        p = jnp.dot(xnb, w_ref[:, w0:w0 + width], preferred_element_type=F32)
        if prev is not None:
            prev[0](prev[1], prev[2])
        prev = (ep, p, c0)
        yield
    prev[0](prev[1], prev[2])


def _delta_stages(q_ref, k_ref, kt_ref, v_ref, z_ref, gt_ref, ng_ref,
                  o_ref, s_ref, u_sc, w_sc, qg_sc, attn_sc, kdt_sc, gtot_sc, wr, rd,
                  *, tb, nh, dh, ch):
    nc = tb // ch
    heads = range(nh)
    hls = [slice(h * dh, (h + 1) * dh) for h in heads]
    mm = functools.partial(jnp.dot, preferred_element_type=F32)

    def phase_b():
        gtot_b = gtot_sc[rd]
        s_all = [s_ref[h] for h in heads]
        for cidx in range(nc):
            rows = slice(cidx * ch, (cidx + 1) * ch)
            sb = [s_all[h].astype(BF16) for h in heads]
            ws = [mm(w_sc[rd, h, rows, :], sb[h]) for h in heads]
            qs = [mm(qg_sc[rd, h, rows, :], sb[h]) for h in heads]
            yield
            vb = [(u_sc[rd, h, rows, :] - ws[h]).astype(BF16) for h in heads]
            av = [mm(attn_sc[rd, h, rows, rows], vb[h]) for h in heads]
            kv = [mm(kdt_sc[rd, h, :, rows], vb[h]) for h in heads]
            yield
            outs = []
            for h in heads:
                a_last = jnp.exp(gtot_b[h:h + 1, cidx * ch:cidx * ch + 1])
                s_all[h] = s_all[h] * a_last + kv[h]
                o = qs[h] + av[h]
                o = o * lax.rsqrt(jnp.mean(o * o, -1, keepdims=True) + EPS)
                zh = z_ref[rows, hls[h]]
                outs.append(o * ng_ref[...] * (zh * jax.nn.sigmoid(zh)))
            o_ref[rows, :] = jnp.concatenate(outs, axis=1)
        for h in heads:
            s_ref[h] = s_all[h]

    r = lax.broadcasted_iota(jnp.int32, (tb, tb), 0)
    c = lax.broadcasted_iota(jnp.int32, (tb, tb), 1)
    same = (r // ch) == (c // ch)
    causal_bd = same & (c <= r)
    strict_bd = same & (c < r)
    gt = gt_ref[...]
    gcum_t = _fdot(gt, jnp.where(same & (r <= c), 1.0, 0.0))
    gtot_t = _fdot(gt, jnp.where(same, 1.0, 0.0))
    gcum = gcum_t.T
    gates = gt.T

    def pack(x):
        acc = x[0:ch, :]
        for i in range(1, nc):
            acc = acc + x[i * ch:(i + 1) * ch, :]
        return acc

    def unpack(x):
        return jnp.where(same, jnp.concatenate([x] * nc, axis=0), 0.0).astype(BF16)

    n_sq = max(1, (ch - 1).bit_length() - 1)

    def phase_a():
        gcol = [gcum[:, h:h + 1] for h in heads]
        grow = [gcum_t[h:h + 1, :] for h in heads]
        beta = [gates[:, nh + h:nh + h + 1] for h in heads]
        kb = [k_ref[:, hls[h]] * beta[h] for h in heads]
        gram = [mm(jnp.concatenate([kb[h], q_ref[:, hls[h]]], axis=0).astype(BF16),
                   kt_ref[hls[h], :].astype(BF16)) for h in heads]
        yield
        ppk, apk = [], []
        for h in heads:
            decay = jnp.exp(jnp.where(causal_bd, gcol[h] - grow[h], -jnp.inf))
            p_bd = -jnp.where(strict_bd, gram[h][:tb] * decay, 0.0)
            attn_sc[wr, h] = (gram[h][tb:] * decay).astype(BF16)
            pk = pack(p_bd)
            apk.append(pk)
            ppk.append(mm(pk.astype(BF16), p_bd.astype(BF16)))
        yield
        for j in range(1, n_sq + 1):
            res = []
            for h in heads:
                lhs = jnp.concatenate([apk[h], ppk[h]], axis=0) if j < n_sq else apk[h]
                res.append(mm(lhs.astype(BF16), unpack(ppk[h])))
            yield
            for h in heads:
                apk[h] = apk[h] + ppk[h] + res[h][:ch]
                if j < n_sq:
                    ppk[h] = res[h][ch:]
        egc = [jnp.exp(gcol[h]) for h in heads]
        rhs = [jnp.concatenate([v_ref[:, hls[h]] * beta[h], kb[h] * egc[h]], axis=1) for h in heads]
        prod = [mm(unpack(apk[h]), rhs[h].astype(BF16)) for h in heads]
        yield
        for h in heads:
            uw = rhs[h] + prod[h]
            u_sc[wr, h] = uw[:, :dh]
            w_sc[wr, h] = uw[:, dh:].astype(BF16)
            qg_sc[wr, h] = (q_ref[:, hls[h]] * egc[h]).astype(BF16)
            kdt_sc[wr, h] = (kt_ref[hls[h], :] * jnp.exp(gtot_t[h:h + 1, :] - grow[h])).astype(BF16)
        gtot_sc[wr] = gtot_t

    return phase_b(), phase_a()


def _mixer_in_body(x_ref, g_ref, w_ref, cw_ref, alog_ref, dtb_ref, ng_ref,
                   u_ref, o_ref,
                   buf_ref, q_sc, k_sc, kt_sc, v_sc, z_sc, gt_sc,
                   s_ref, u_sc, w_sc, qg_sc, attn_sc, kdt_sc, gtot_sc,
                   *, tb, n_t, nh, dh, cw_n):
    s = pl.program_id(0)
    dn_w = nh * dh

    @pl.when(s == 0)
    def _():
        for ref in (buf_ref, q_sc, k_sc, kt_sc, v_sc, z_sc, gt_sc,
                    s_ref, u_sc, w_sc, qg_sc, attn_sc, kdt_sc, gtot_sc):
            ref[...] = jnp.zeros(ref.shape, ref.dtype)

    @pl.when(s % n_t == 0)
    def _():
        buf_ref[:, 0:SUBLANES, :] = jnp.zeros((3 * dn_w // LANES, SUBLANES, LANES), F32)

    @pl.when((s - 2) % n_t == 0)
    def _():
        s_ref[...] = jnp.zeros(s_ref.shape, F32)

    cur = s % 2
    prv = 1 - cur
    zw = s % 3
    zr = (s + 1) % 3
    proj = _in_proj_stages(
        x_ref, g_ref, w_ref, cw_ref, alog_ref, dtb_ref,
        q_sc.at[cur], k_sc.at[cur], kt_sc.at[cur], v_sc.at[cur], z_sc.at[zw], u_ref.at[0],
        gt_sc.at[cur],
        buf_ref, tm=tb, dn_w=dn_w, nh=nh, dh=dh, cw_n=cw_n)
    phase_b, phase_a = _delta_stages(
        q_sc.at[prv], k_sc.at[prv], kt_sc.at[prv], v_sc.at[prv], z_sc.at[zr], gt_sc.at[prv], ng_ref,
        o_ref, s_ref, u_sc, w_sc, qg_sc, attn_sc, kdt_sc, gtot_sc, cur, prv,
        tb=tb, nh=nh, dh=dh, ch=DN_CHUNK)
    _interleave([proj, phase_b, phase_a])


def _mixer_in(x2, seq, g, w_cat, conv_w, alog, dtb, ng, *, nh, dh, tb, u_tile):
    nt, d = x2.shape
    dn_w = nh * dh
    n_blk = nt // tb
    per = u_tile // tb
    cw_n = conv_w.shape[0]
    const = lambda s: (0, 0)
    in_blk = lambda s: (jnp.minimum(s, n_blk - 1), 0)
    out_blk = lambda s: (jnp.maximum(s - 2, 0), 0)

    def u_blk(s):
        blk = jnp.minimum(s, n_blk - 1)
        return (blk // per, 0, blk % per, 0)
    body = functools.partial(_mixer_in_body, tb=tb, n_t=seq // tb, nh=nh, dh=dh, cw_n=cw_n)
    two = lambda *shape: pltpu.VMEM((2,) + shape, F32)
    return pl.pallas_call(
        body,
        grid=(n_blk + 2,),
        in_specs=[pl.BlockSpec((tb, d), in_blk)]
        + [pl.BlockSpec(a.shape, const) for a in (g, w_cat, conv_w, alog, dtb, ng)],
        out_specs=[pl.BlockSpec((1, dn_w // LANES, tb, LANES), u_blk),
                   pl.BlockSpec((tb, dn_w), out_blk)],
        out_shape=[jax.ShapeDtypeStruct((nt // u_tile, dn_w // LANES, u_tile, LANES), F32),
                   jax.ShapeDtypeStruct((nt, dn_w), F32)],
        scratch_shapes=[
            pltpu.VMEM((3 * dn_w // LANES, tb + SUBLANES, LANES), F32),
            two(tb, dn_w), two(tb, dn_w), two(dn_w, tb), two(tb, dn_w),
            pltpu.VMEM((3, tb, dn_w), F32),
            two(SUBLANES, tb),
            pltpu.VMEM((nh, dh, dh), F32),
            pltpu.VMEM((2, nh, tb, dh), F32),
            pltpu.VMEM((2, nh, tb, dh), BF16),
            pltpu.VMEM((2, nh, tb, dh), BF16),
            pltpu.VMEM((2, nh, tb, tb), BF16),
            pltpu.VMEM((2, nh, dh, tb), BF16),
            pltpu.VMEM((2, SUBLANES, tb), F32),
        ],
        compiler_params=_params("arbitrary"),
        name="mixer_in",
    )(x2, g, w_cat, conv_w, alog, dtb, ng)


def _ftdot(a, b):
    return lax.dot_general(a.astype(BF16), b.astype(BF16), (((0,), (0,)), ((), ())),
                           preferred_element_type=F32)


def _cmul(xr, xi, yr, yi):
    return xr * yr - xi * yi, xr * yi + xi * yr


def _s5t_prep_body(are_ref, aim_ref, ldt_ref, btr_ref, bti_ref, ctr_ref, cti_ref, w1_ref, co_ref,
                   *, tc, cg, n_p):
    n_grp = w1_ref.shape[0]
    kw = tc * cg
    a_re = are_ref[...]
    a_im = aim_ref[...]
    dt = jnp.exp(ldt_ref[...])
    lre = a_re * dt
    lim = a_im * dt
    mag = jnp.exp(lre)
    ab = (mag * jnp.cos(lim), mag * jnp.sin(lim))
    nr, ni = ab[0] - 1.0, ab[1]
    den = a_re * a_re + a_im * a_im
    co = ((nr * a_re + ni * a_im) / den, (ni * a_re - nr * a_im) / den)
    inv_mag2 = jnp.exp(-2.0 * lre)
    ai = (ab[0] * inv_mag2, -ab[1] * inv_mag2)
    n_bits = tc.bit_length() - 1
    pos, neg = [ab], [ai]
    for _ in range(n_bits - 1):
        pos.append(_cmul(*pos[-1], *pos[-1]))
        neg.append(_cmul(*neg[-1], *neg[-1]))
    top = pos[0]
    for f in pos[1:]:
        top = _cmul(*top, *f)
    rows = [co, top] + pos + neg
    flat = [part for pair in rows for part in pair]
    cols = []
    for i in range(0, len(flat), SUBLANES):
        blk = flat[i:i + SUBLANES]
        blk = blk + [blk[-1]] * (SUBLANES - len(blk))
        t = jnp.concatenate(blk, axis=0).T
        cols += [t[:, j:j + 1] for j in range(SUBLANES)]
    pair = lambda k: (cols[2 * k], cols[2 * k + 1])
    co_c, top_c = pair(0), pair(1)
    pos_c = [pair(2 + k) for k in range(n_bits)]
    neg_c = [pair(2 + n_bits + k) for k in range(n_bits)]

    slot = lax.broadcasted_iota(jnp.int32, (1, kw), 1) // cg

    def power(factors):
        acc = None
        for k, (fr, fi) in enumerate(factors):
            bit = ((slot >> k) & 1) == 1
            term = (jnp.where(bit, fr, 1.0), jnp.where(bit, fi, 0.0))
            acc = term if acc is None else _cmul(*acc, *term)
        return acc

    place = jnp.where(lax.broadcasted_iota(jnp.int32, (cg, kw), 0)
                      == lax.broadcasted_iota(jnp.int32, (cg, kw), 1) % cg, 1.0, 0.0)
    place = place.astype(BF16)

    def rep(ref):
        x = ref[...]
        hi = x.astype(BF16)
        r1 = x - hi.astype(F32)
        mid = r1.astype(BF16)
        lo = (r1 - mid.astype(F32)).astype(BF16)
        return sum(jnp.dot(part, place, preferred_element_type=F32) for part in (hi, mid, lo))
    bt = _cmul(*co_c, rep(btr_ref), rep(bti_ref))
    pb = _cmul(*bt, *power(neg_c))
    qc = _cmul(rep(ctr_ref), rep(cti_ref), *power(pos_c))
    q1 = _cmul(*qc, *pos_c[0])
    bi = _cmul(*pb, *top_c)
    r2 = lax.broadcasted_iota(jnp.int32, (kw, kw), 0) // cg
    c2 = lax.broadcasted_iota(jnp.int32, (kw, kw), 1) // cg
    eye = jnp.where(lax.broadcasted_iota(jnp.int32, (n_p, n_p), 0)
                    == lax.broadcasted_iota(jnp.int32, (n_p, n_p), 1), 1.0, 0.0)
    for g in range(n_grp):
        rs = slice(g * n_p, (g + 1) * n_p)
        m_full = _ftdot(pb[0][rs], qc[0][rs]) - _ftdot(pb[1][rs], qc[1][rs])
        w1_ref[g, :, 0:kw] = jnp.where(r2 <= c2, m_full, 0.0).astype(w1_ref.dtype)
        w1_ref[g, :, kw:kw + n_p] = _ftdot(bi[0][rs], eye).astype(w1_ref.dtype)
        w1_ref[g, :, kw + n_p:kw + 2 * n_p] = _ftdot(bi[1][rs], eye).astype(w1_ref.dtype)
        co_ref[g, 0:n_p, :] = q1[0][rs].astype(co_ref.dtype)
        co_ref[g, n_p:2 * n_p, :] = (-q1[1][rs]).astype(co_ref.dtype)


def _s5t_prep(a_re, a_im, log_dt, b_re, b_im, c_re, c_im):
    g, p = a_re.shape
    cg = b_re.shape[-1]
    kw = T_CHUNK * cg
    row = lambda a: a.reshape(1, g * p)
    slots = lambda a: a.reshape(g * p, cg)
    return pl.pallas_call(
        functools.partial(_s5t_prep_body, tc=T_CHUNK, cg=cg, n_p=p),
        out_shape=[jax.ShapeDtypeStruct((g, kw, kw + 2 * p), BF16),
                   jax.ShapeDtypeStruct((g, 2 * p, kw), BF16)],
        compiler_params=pltpu.CompilerParams(vmem_limit_bytes=VMEM_LIMIT),
        name="s5t_prep",
    )(row(a_re), row(a_im), row(jnp.broadcast_to(log_dt[:, None], (g, p))),
      slots(b_re), slots(b_im), slots(c_re.transpose(0, 2, 1)), slots(c_im.transpose(0, 2, 1)))


def _s5t_tables_body(are_ref, aim_ref, ldt_ref, pin_re, pin_im, pout_re, pout_im, al_re, al_im,
                     *, tc, sc):
    lre = are_ref[...] * jnp.exp(ldt_ref[...]) * float(tc)
    lim = aim_ref[...] * jnp.exp(ldt_ref[...]) * float(tc)
    lane = lax.broadcasted_iota(jnp.int32, lre.shape, 1)
    sign = jnp.where((lane % LANES) < LANES // 2, -1.0, 1.0)
    n = lax.broadcasted_iota(jnp.int32, (sc, 1), 0).astype(F32) - float(sc // 2)
    m_out = jnp.exp(lre * n)
    m_in = jnp.exp(-(lre * n))
    ang = lim * n
    pin_re[...] = (m_in * jnp.cos(ang)).astype(pin_re.dtype)
    pin_im[...] = (sign * (-(m_in * jnp.sin(ang)))).astype(pin_im.dtype)
    pout_re[...] = m_out * jnp.cos(ang)
    pout_im[...] = sign * (m_out * jnp.sin(ang))
    m_al = jnp.exp(lre * float(sc))
    al_re[...] = m_al * jnp.cos(lim * float(sc))
    al_im[...] = sign * (m_al * jnp.sin(lim * float(sc)))


def _s5t_tables(a_re, a_im, log_dt):
    g, p = a_re.shape
    lanes = g * 2 * p
    spread = lambda a: jnp.broadcast_to(a[:, None, :], (g, 2, p)).reshape(1, lanes)
    tab16 = jax.ShapeDtypeStruct((S_CHUNK, lanes), BF16)
    tab32 = jax.ShapeDtypeStruct((S_CHUNK, lanes), F32)
    one = jax.ShapeDtypeStruct((1, lanes), F32)
    return pl.pallas_call(
        functools.partial(_s5t_tables_body, tc=T_CHUNK, sc=S_CHUNK),
        out_shape=[tab16, tab16, tab32, tab32, one, one], name="s5t_tables",
    )(spread(a_re), spread(a_im), spread(jnp.broadcast_to(log_dt[:, None], (g, p))))


def _atom_transpose(regs, atom, width):
    regs = list(regs)
    n = len(regs)
    d = n // 2
    while d:
        low = (atom & d) == 0
        for j in range(n):
            if j & d:
                continue
            a, b = regs[j], regs[j + d]
            regs[j] = jnp.where(low, a, pltpu.roll(b, d * width, axis=1))
            regs[j + d] = jnp.where(low, pltpu.roll(a, LANES - d * width, axis=1), b)
        d //= 2
    return regs


def _swap_halves(x):
    tiles = [pltpu.roll(x[:, t * LANES:(t + 1) * LANES], LANES // 2, axis=1)
             for t in range(x.shape[1] // LANES)]
    return jnp.concatenate(tiles, axis=1)


def _s5t_body(u_ref, w1_ref, co_ref, pin_re, pin_im, pout_re, pout_im, al_re, al_im,
              d_ref, wglu_ref, bglu_ref, y_ref,
              upk_ref, yloc_ref, sin_ref, xs_ref, carry_ref, udi_ref, *, rc, tc, sc, n_grp, cg):
    t = pl.program_id(1)
    kw = tc * cg
    u_ref = u_ref.at[0]
    y_ref = y_ref.at[0]
    n_q = u_ref.shape[0]
    atoms = LANES // cg

    @pl.when(t == 0)
    def _():
        carry_ref[...] = jnp.zeros(carry_ref.shape, F32)
        xs_ref[:, 0:SUBLANES, :] = jnp.zeros((n_grp, SUBLANES, LANES), F32)

    atom = lax.broadcasted_iota(jnp.int32, (rc, LANES), 1) // cg

    n_a = tc // atoms

    def pack():
        for q in range(n_q):
            for a in range(n_a):
                planes = [u_ref[q, pl.ds(a * atoms + k, rc, stride=tc), :] for k in range(atoms)]
                for k in range(atoms):
                    udi_ref[q, a * atoms + k] = planes[k]
                for r, tile in enumerate(_atom_transpose(planes, atom, cg)):
                    upk_ref[q * atoms + r, :, a * LANES:(a + 1) * LANES] = tile.astype(BF16)
                yield

    per_stage = atoms // n_a

    def local():
        for _ in range(n_a):
            yield
        for g in range(n_grp):
            res = jnp.dot(upk_ref[g], w1_ref[g], preferred_element_type=F32)
            yloc_ref[g] = res[:, :kw]
            sin_ref[g] = res[:, kw:]
            if g % per_stage == per_stage - 1:
                yield

    _interleave([pack(), local()])

    ri = lax.broadcasted_iota(jnp.int32, (sc, sc), 0)
    ci = lax.broadcasted_iota(jnp.int32, (sc, sc), 1)
    tri = jnp.where(ci <= ri, 1.0, 0.0).astype(BF16)
    n_c2 = rc // sc
    ss = [jnp.concatenate([sin_ref[g, c2 * sc:(c2 + 1) * sc, :] for g in range(n_grp)], axis=1)
          for c2 in range(n_c2)]
    ss_sw = [_swap_halves(s) for s in ss]
    xin = [pin_re[...] * ss[c2].astype(BF16) + pin_im[...] * ss_sw[c2].astype(BF16)
           for c2 in range(n_c2)]
    psum = [jnp.dot(tri, x, preferred_element_type=F32) for x in xin]
    psum_sw = [_swap_halves(p) for p in psum]
    carry = carry_ref[...]
    carry_sw = _swap_halves(carry)
    carries = []
    for c2 in range(rc // sc):
        carries.append((carry, carry_sw))
        last = psum[c2][sc - 1:sc, :] + carry
        last_sw = psum_sw[c2][sc - 1:sc, :] + carry_sw
        carry = al_re[...] * last + al_im[...] * last_sw
        carry_sw = al_re[...] * last_sw - al_im[...] * last
    carry_ref[...] = carry
    for c2 in range(rc // sc):
        x = (pout_re[...] * (psum[c2] + carries[c2][0])
             + pout_im[...] * (psum_sw[c2] + carries[c2][1]))
        for g in range(n_grp):
            xs_ref[g, SUBLANES + c2 * sc:SUBLANES + (c2 + 1) * sc, :] = x[:, g * LANES:(g + 1) * LANES]

    for g in range(n_grp):
        xprev = xs_ref[g, SUBLANES - 1:SUBLANES - 1 + rc, :].astype(BF16)
        yloc_ref[g] = yloc_ref[g] + jnp.dot(xprev, co_ref[g], preferred_element_type=F32)
    xs_ref[:, SUBLANES - 1:SUBLANES, :] = xs_ref[:, SUBLANES - 1 + rc:SUBLANES + rc, :]

    wglu = wglu_ref[...].astype(BF16)

    def unpack(a):
        for _ in range(a * n_q):
            yield
        planes = [[None] * n_q for _ in range(atoms)]
        for q in range(n_q):
            tiles = [yloc_ref[q * atoms + r, :, a * LANES:(a + 1) * LANES] for r in range(atoms)]
            for k, tile in enumerate(_atom_transpose(tiles, atom, cg)):
                planes[k][q] = tile
            yield
        for k in range(atoms):
            i = a * atoms + k
            uu = jnp.concatenate([udi_ref[q, i] for q in range(n_q)], axis=1)
            y = jax.nn.gelu(jnp.concatenate(planes[k], axis=1) + d_ref[...] * uu)
            gate = jax.nn.sigmoid(jnp.dot(y.astype(BF16), wglu, preferred_element_type=F32)
                                  + bglu_ref[...])
            out = y * gate
            for q in range(n_q):
                y_ref[q, pl.ds(i, rc, stride=tc), :] = out[:, q * LANES:(q + 1) * LANES]
            if k % 2 == 1:
                yield

    _interleave([unpack(a) for a in range(n_a)])


def _s5t(u4, w1, co, tabs, d, wglu, bglu, bsz, seq, *, rc):
    _, n_q, rows, _ = u4.shape
    assert rows == rc * T_CHUNK
    n_grp, kw, wcols = w1.shape
    n_state = (wcols - kw) // 2
    cg = kw // T_CHUNK
    n_t = seq // rows
    blk = pl.BlockSpec((1, n_q, rows, LANES), lambda b, t: (b * n_t + t, 0, 0, 0))
    c2 = lambda b, t: (0, 0)
    c3 = lambda b, t: (0, 0, 0)
    whole = lambda a: pl.BlockSpec(a.shape, c3 if a.ndim == 3 else c2)
    body = functools.partial(_s5t_body, rc=rc, tc=T_CHUNK, sc=S_CHUNK, n_grp=n_grp, cg=cg)
    return pl.pallas_call(
        body,
        grid=(bsz, n_t),
        in_specs=[blk] + [whole(a) for a in (w1, co, *tabs, d, wglu, bglu)],
        out_specs=blk,
        out_shape=jax.ShapeDtypeStruct(u4.shape, F32),
        scratch_shapes=[
            pltpu.VMEM((n_grp, rc, kw), BF16),
            pltpu.VMEM((n_grp, rc, kw), F32),
            pltpu.VMEM((n_grp, rc, 2 * n_state), F32),
            pltpu.VMEM((n_grp, rc + SUBLANES, 2 * n_state), F32),
            pltpu.VMEM((1, n_grp * 2 * n_state), F32),
            pltpu.VMEM((n_q, T_CHUNK, rc, LANES), F32),
        ],
        compiler_params=_params("arbitrary", "arbitrary"),
        name="s5t",
    )(u4, w1, co, *tabs, d, wglu, bglu)


def _mem_kv_body(m_ref, g_ref, wk_ref, wv_ref, wq_ref, wo_ref, qk_ref, vw_ref, w_sc, *, xh):
    @pl.when(pl.program_id(0) == 0)
    def _():
        for i, ref in enumerate((wk_ref, wv_ref, wq_ref, wo_ref)):
            w_sc[i] = ref[...].astype(BF16)

    _, n_mem, d = m_ref.shape
    xd = d // xh
    scale = xd ** -0.5
    mn = _rms(m_ref[0], g_ref[...]).astype(BF16)
    k = jnp.dot(mn, w_sc[0], preferred_element_type=F32).astype(BF16)
    v = jnp.dot(mn, w_sc[1], preferred_element_type=F32).astype(BF16)
    nt_dims = (((1,), (1,)), ((), ()))
    for h in range(xh):
        hl = slice(h * xd, (h + 1) * xd)
        ml = slice(h * n_mem, (h + 1) * n_mem)
        qk = lax.dot_general(w_sc[2, :, hl], k[:, hl], nt_dims, preferred_element_type=F32)
        qk_ref[0, :, ml] = (qk * scale).astype(BF16)
        vw_ref[0, ml, :] = jnp.dot(v[:, hl], w_sc[3, hl, :],
                                   preferred_element_type=F32).astype(BF16)


def _mem_kv(mem, g, wk, wv, wq, wo):
    bsz, n_mem, d = mem.shape
    hm = X_HEADS * n_mem
    const = lambda b: (0, 0)
    weight = pl.BlockSpec((d, d), const, pipeline_mode=pl.Buffered(1))
    return pl.pallas_call(
        functools.partial(_mem_kv_body, xh=X_HEADS),
        grid=(bsz,),
        in_specs=[pl.BlockSpec((1, n_mem, d), lambda b: (b, 0, 0)), pl.BlockSpec((1, d), const),
                  weight, weight, weight, weight],
        out_specs=[pl.BlockSpec((1, d, hm), lambda b: (b, 0, 0)),
                   pl.BlockSpec((1, hm, d), lambda b: (b, 0, 0))],
        out_shape=[
            jax.ShapeDtypeStruct((bsz, d, hm), BF16),
            jax.ShapeDtypeStruct((bsz, hm, d), BF16),
        ],
        scratch_shapes=[pltpu.VMEM((4, d, d), BF16)],
        compiler_params=_params("arbitrary"),
        name="mem_kv",
    )(mem, g, wk, wv, wq, wo)


def _mix_attn_body(x_ref, o_ref, y_ref, wo_ref, gx_ref, qk_ref, vw_ref, h_ref, *, xh, n_sub):
    dn_w = o_ref.shape[1]
    n_mem = qk_ref.shape[2] // xh
    sub = x_ref.shape[0] // n_sub
    mm = functools.partial(jnp.dot, preferred_element_type=F32)
    wo = wo_ref[...].astype(BF16)

    def rows_gen(r0):
        rows = slice(r0, r0 + sub)
        y = jnp.concatenate([y_ref[0, q, rows, :] for q in range(y_ref.shape[1])], axis=1)
        mix = mm(o_ref[rows, :].astype(BF16), wo[0:dn_w, :]) + mm(y.astype(BF16), wo[dn_w:, :])
        yield
        h1 = x_ref[rows, :] + mix
        s = mm(_rms(h1, gx_ref[...]).astype(BF16), qk_ref[0])
        yield
        parts = []
        for h in range(xh):
            sh = s[:, h * n_mem:(h + 1) * n_mem]
            e = jnp.exp(sh - jnp.max(sh, axis=-1, keepdims=True))
            parts.append((e / jnp.sum(e, axis=-1, keepdims=True)).astype(BF16))
        att = mm(jnp.concatenate(parts, axis=1), vw_ref[0])
        yield
        h_ref[rows, :] = h1 + att

    _interleave([rows_gen(i * sub) for i in range(n_sub)])


def _mix_attn(x2, o, y, w_out, gx, qk, vw, bsz, seq, *, tm):
    nt, d = x2.shape
    dn_w = o.shape[1]
    n_t = seq // tm
    y_per = y.shape[2] // tm
    row = lambda b, t: (b * n_t + t, 0)
    c2 = lambda b, t: (0, 0)
    per_batch = lambda a: pl.BlockSpec((1,) + a.shape[1:], lambda b, t: (b, 0, 0))
    body = functools.partial(_mix_attn_body, xh=X_HEADS, n_sub=2)
    return pl.pallas_call(
        body,
        grid=(bsz, n_t),
        in_specs=[
            pl.BlockSpec((tm, d), row),
            pl.BlockSpec((tm, dn_w), row),
            pl.BlockSpec((1, y.shape[1], tm, LANES),
                         lambda b, t: ((b * n_t + t) // y_per, 0, (b * n_t + t) % y_per, 0)),
            pl.BlockSpec(w_out.shape, c2, pipeline_mode=pl.Buffered(1)),
            pl.BlockSpec((1, d), c2),
            per_batch(qk),
            per_batch(vw),
        ],
        out_specs=pl.BlockSpec((tm, d), row),
        out_shape=jax.ShapeDtypeStruct((nt, d), F32),
        compiler_params=_params("arbitrary", "arbitrary"),
        name="mix_attn",
    )(x2, o, y, w_out, gx, qk, vw)


def _ffn_body(h_ref, gf_ref, wg_ref, wu_ref, wd_ref, gl_ref, out_ref, *, final, fc):
    h = h_ref[...]
    hn = _rms(h, gf_ref[...]).astype(BF16)
    dff = wg_ref.shape[1]

    def down(gate, up, c0):
        act = (gate * jax.nn.sigmoid(gate) * up).astype(BF16)
        return jnp.dot(act, wd_ref[c0:c0 + fc, :].astype(BF16), preferred_element_type=F32)

    h3 = h
    prev = None
    for c0 in range(0, dff, fc):
        gate = jnp.dot(hn, wg_ref[:, c0:c0 + fc].astype(BF16), preferred_element_type=F32)
        up = jnp.dot(hn, wu_ref[:, c0:c0 + fc].astype(BF16), preferred_element_type=F32)
        if prev is not None:
            h3 = h3 + down(*prev)
        prev = (gate, up, c0)
    h3 = h3 + down(*prev)
    out_ref[...] = _rms(h3, gl_ref[...]) if final else h3


def _ffn(h, gf, wg, wu, wd, gl, *, tm, final):
    nt, d = h.shape
    dff = wg.shape[1]
    c2 = lambda i: (0, 0)
    row = lambda i: (i, 0)
    return pl.pallas_call(
        functools.partial(_ffn_body, final=final, fc=2 * LANES),
        grid=(nt // tm,),
        in_specs=[
            pl.BlockSpec((tm, d), row),
            pl.BlockSpec((1, d), c2),
            pl.BlockSpec((d, dff), c2, pipeline_mode=pl.Buffered(1)),
            pl.BlockSpec((d, dff), c2, pipeline_mode=pl.Buffered(1)),
            pl.BlockSpec((dff, d), c2, pipeline_mode=pl.Buffered(1)),
            pl.BlockSpec((1, d), c2),
        ],
        out_specs=pl.BlockSpec((tm, d), row),
        out_shape=jax.ShapeDtypeStruct((nt, d), F32),
        compiler_params=_params("arbitrary"),
        name="ffn",
    )(h, gf, wg, wu, wd, gl)


def _layer(h2, mem, bsz, seq, norm_mix_g, w_in, conv_w, dn_a_log, dn_dt_bias, dn_norm_g,
           s5_a_re, s5_a_im, s5_b_re, s5_b_im, s5_c_re, s5_c_im, s5_d, s5_log_dt,
           s5_w_glu, s5_b_glu, w_out, norm_x_g, norm_mem_g, w_xq, w_xk, w_xv, w_xo):
    d = h2.shape[1]
    nh = dn_a_log.shape[0]
    dh = dn_norm_g.shape[0]
    dn_w = nh * dh
    s5_w = s5_a_re.shape[0] * s5_b_re.shape[-1]

    off_a = 4 * dn_w
    off_u = off_a + 2 * nh
    w_ab = jnp.pad(w_in[:, off_a:off_u].astype(BF16), ((0, 0), (0, LANES - 2 * nh)))
    w_cat = jnp.concatenate([w_in[:, :off_a].astype(BF16), w_in[:, off_u:].astype(BF16), w_ab],
                            axis=1)
    alog = jnp.pad(dn_a_log, (0, LANES - nh)).reshape(1, LANES)
    dtb = jnp.pad(dn_dt_bias, (0, LANES - nh)).reshape(1, LANES)

    u, o = _mixer_in(h2, seq, norm_mix_g.reshape(1, d), w_cat, conv_w, alog, dtb,
                     dn_norm_g.reshape(1, dh), nh=nh, dh=dh, tb=MIXER_ROWS,
                     u_tile=S5_BLOCKS * T_CHUNK)

    w1, co = _s5t_prep(s5_a_re, s5_a_im, s5_log_dt, s5_b_re, s5_b_im, s5_c_re, s5_c_im)
    tabs = _s5t_tables(s5_a_re, s5_a_im, s5_log_dt)
    y = _s5t(u, w1, co, tabs, s5_d.reshape(1, s5_w), s5_w_glu, s5_b_glu.reshape(1, s5_w),
             bsz, seq, rc=S5_BLOCKS)

    qk_mem, vw_mem = _mem_kv(mem, norm_mem_g.reshape(1, d), w_xk, w_xv, w_xq, w_xo)
    return _mix_attn(h2, o, y, w_out, norm_x_g.reshape(1, d), qk_mem, vw_mem, bsz, seq,
                     tm=ATTN_ROWS)


def kernel(x, mem, norm_mix_g, w_in, conv_w, dn_a_log, dn_dt_bias, dn_norm_g, s5_a_re, s5_a_im,
           s5_b_re, s5_b_im, s5_c_re, s5_c_im, s5_d, s5_log_dt, s5_w_glu, s5_b_glu, w_out,
           norm_x_g, norm_mem_g, w_xq, w_xk, w_xv, w_xo, norm_ffn_g, w_gate, w_up, w_down,
           norm_final_g):
    bsz, seq, d = x.shape
    depth = w_in.shape[0]
    h = x.reshape(bsz * seq, d)
    for l in range(depth):
        h = _layer(h, mem, bsz, seq, norm_mix_g[l], w_in[l], conv_w[l], dn_a_log[l],
                   dn_dt_bias[l], dn_norm_g[l], s5_a_re[l], s5_a_im[l], s5_b_re[l], s5_b_im[l],
                   s5_c_re[l], s5_c_im[l], s5_d[l], s5_log_dt[l], s5_w_glu[l], s5_b_glu[l],
                   w_out[l], norm_x_g[l], norm_mem_g[l], w_xq[l], w_xk[l], w_xv[l], w_xo[l])
        h = _ffn(h, norm_ffn_g[l].reshape(1, d), w_gate[l], w_up[l], w_down[l],
                 norm_final_g.reshape(1, d), tm=FFN_ROWS, final=l == depth - 1)
    return h.reshape(bsz, seq, d)
```

```python
import functools

import jax
import jax.numpy as jnp
from jax import lax
from jax.experimental import pallas as pl
from jax.experimental.pallas import tpu as pltpu

F32 = jnp.float32
BF16 = jnp.bfloat16
EPS = 1e-6
HIGHEST = lax.Precision.HIGHEST

LANES = 128
SUBLANES = 8
DN_CHUNK = 64
T_CHUNK = 16
S_CHUNK = 16
X_HEADS = 4
VMEM_LIMIT = 58 * 1024 * 1024

MIXER_ROWS = 256
S5_BLOCKS = 128
ATTN_ROWS = 1024
FFN_ROWS = 1024


def _fdot(a, b):
    return jnp.dot(a, b, precision=HIGHEST, preferred_element_type=F32)


def _rms(x, g):
    return x * lax.rsqrt(jnp.mean(x * x, axis=-1, keepdims=True) + EPS) * g


def _interleave(gens, stages_per_round=None):
    pending = [(gen, 1 if stages_per_round is None else stages_per_round[i])
               for i, gen in enumerate(gens)]
    while pending:
        for item in list(pending):
            gen, n = item
            try:
                for _ in range(n):
                    next(gen)
            except StopIteration:
                pending.remove(item)


def _params(*sem):
    return pltpu.CompilerParams(dimension_semantics=sem, vmem_limit_bytes=VMEM_LIMIT)


def _in_proj_stages(x_ref, g_ref, w_ref, cw_ref, alog_ref, dtb_ref,
                    q_ref, k_ref, kt_ref, v_ref, z_ref, u_ref, gt_ref,
                    buf_ref, *, tm, dn_w, nh, dh, cw_n):
    xnb = _rms(x_ref[...], g_ref[...]).astype(BF16)
    gw = 2 * dh

    def conv_silu(p, cols):
        outs = []
        for i in range(gw // LANES):
            c0 = cols.start + i * LANES
            s = c0 // LANES
            pc = p[:, i * LANES:(i + 1) * LANES]
            buf_ref[s, SUBLANES:SUBLANES + tm, :] = pc
            acc = cw_ref[cw_n - 1:cw_n, c0:c0 + LANES] * pc
            for j in range(cw_n - 1):
                off = SUBLANES - (cw_n - 1) + j
                acc = acc + cw_ref[j:j + 1, c0:c0 + LANES] * buf_ref[s, off:off + tm, :]
            buf_ref[s, 0:SUBLANES, :] = buf_ref[s, tm:tm + SUBLANES, :]
            outs.append(acc * jax.nn.sigmoid(acc))
        return jnp.concatenate(outs, axis=1)

    def l2n(a, scale):
        parts = []
        for i in range(gw // dh):
            ah = a[:, i * dh:(i + 1) * dh]
            parts.append(ah * (lax.rsqrt(jnp.sum(ah * ah, -1, keepdims=True) + EPS) * scale))
        return jnp.concatenate(parts, axis=1)

    def ep_q(p, c0):
        q_ref[:, c0:c0 + gw] = l2n(conv_silu(p, slice(c0, c0 + gw)), dh ** -0.5)

    def ep_k(p, c0):
        kn = l2n(conv_silu(p, slice(dn_w + c0, dn_w + c0 + gw)), 1.0)
        k_ref[:, c0:c0 + gw] = kn
        kt_ref[c0:c0 + gw, :] = kn.T

    def ep_v(p, c0):
        v_ref[:, c0:c0 + gw] = conv_silu(p, slice(2 * dn_w + c0, 2 * dn_w + c0 + gw))

    def ep_z(p, c0):
        z_ref[:, c0:c0 + gw] = p

    def ep_u(p, c0):
        for j in range(gw // LANES):
            u_ref[c0 // LANES + j] = p[:, j * LANES:(j + 1) * LANES]

    def ep_gates(ab, c0):
        lane = lax.broadcasted_iota(jnp.int32, ab.shape, 1)
        g = -jnp.exp(alog_ref[...]) * jnp.logaddexp(ab + dtb_ref[...], 0.0)
        gates = jnp.where(lane < nh, g, jax.nn.sigmoid(ab))
        gt_ref[...] = gates.T[0:SUBLANES, :]

    groups = []
    for i, ep in enumerate((ep_q, ep_k, ep_v, ep_z, ep_u)):
        groups += [(ep, i * dn_w + c0, gw, c0) for c0 in range(0, dn_w, gw)]
    groups.append((ep_gates, 5 * dn_w, LANES, 0))

    prev = None
    for ep, w0, width, c0 in groups:
        p = jnp.dot(xnb, w_ref[:, w0:w0 + width], preferred_element_type=F32)
        if prev is not None:
            prev[0](prev[1], prev[2])
        prev = (ep, p, c0)
        yield
    prev[0](prev[1], prev[2])


def _delta_stages(q_ref, k_ref, kt_ref, v_ref, z_ref, gt_ref, ng_ref,
                  o_ref, s_ref, u_sc, w_sc, qg_sc, attn_sc, kdt_sc, gtot_sc, wr, rd,
                  *, tb, nh, dh, ch):
    nc = tb // ch
    heads = range(nh)
    hls = [slice(h * dh, (h + 1) * dh) for h in heads]
    mm = functools.partial(jnp.dot, preferred_element_type=F32)

    def phase_b():
        gtot_b = gtot_sc[rd]
        s_all = [s_ref[h] for h in heads]
        for cidx in range(nc):
            rows = slice(cidx * ch, (cidx + 1) * ch)
            sb = [s_all[h].astype(BF16) for h in heads]
            ws = [mm(w_sc[rd, h, rows, :], sb[h]) for h in heads]
            qs = [mm(qg_sc[rd, h, rows, :], sb[h]) for h in heads]
            yield
            vb = [(u_sc[rd, h, rows, :] - ws[h]).astype(BF16) for h in heads]
            av = [mm(attn_sc[rd, h, rows, rows], vb[h]) for h in heads]
            kv = [mm(kdt_sc[rd, h, :, rows], vb[h]) for h in heads]
            yield
            outs = []
            for h in heads:
                a_last = jnp.exp(gtot_b[h:h + 1, cidx * ch:cidx * ch + 1])
                s_all[h] = s_all[h] * a_last + kv[h]
                o = qs[h] + av[h]
                o = o * lax.rsqrt(jnp.mean(o * o, -1, keepdims=True) + EPS)
                zh = z_ref[rows, hls[h]]
                outs.append(o * ng_ref[...] * (zh * jax.nn.sigmoid(zh)))
            o_ref[rows, :] = jnp.concatenate(outs, axis=1)
        for h in heads:
            s_ref[h] = s_all[h]

    r = lax.broadcasted_iota(jnp.int32, (tb, tb), 0)
    c = lax.broadcasted_iota(jnp.int32, (tb, tb), 1)
    same = (r // ch) == (c // ch)
    causal_bd = same & (c <= r)
    strict_bd = same & (c < r)
    gt = gt_ref[...]
    gcum_t = _fdot(gt, jnp.where(same & (r <= c), 1.0, 0.0))
    gtot_t = _fdot(gt, jnp.where(same, 1.0, 0.0))
    gcum = gcum_t.T
    gates = gt.T

    def pack(x):
        acc = x[0:ch, :]
        for i in range(1, nc):
            acc = acc + x[i * ch:(i + 1) * ch, :]
        return acc

    def unpack(x):
        return jnp.where(same, jnp.concatenate([x] * nc, axis=0), 0.0).astype(BF16)

    n_sq = max(1, (ch - 1).bit_length() - 1)

    def phase_a():
        gcol = [gcum[:, h:h + 1] for h in heads]
        grow = [gcum_t[h:h + 1, :] for h in heads]
        beta = [gates[:, nh + h:nh + h + 1] for h in heads]
        kb = [k_ref[:, hls[h]] * beta[h] for h in heads]
        gram = [mm(jnp.concatenate([kb[h], q_ref[:, hls[h]]], axis=0).astype(BF16),
                   kt_ref[hls[h], :].astype(BF16)) for h in heads]
        yield
        ppk, apk = [], []
        for h in heads:
            decay = jnp.exp(jnp.where(causal_bd, gcol[h] - grow[h], -jnp.inf))
            p_bd = -jnp.where(strict_bd, gram[h][:tb] * decay, 0.0)
            attn_sc[wr, h] = (gram[h][tb:] * decay).astype(BF16)
            pk = pack(p_bd)
            apk.append(pk)
            ppk.append(mm(pk.astype(BF16), p_bd.astype(BF16)))
        yield
        for j in range(1, n_sq + 1):
            res = []
            for h in heads:
                lhs = jnp.concatenate([apk[h], ppk[h]], axis=0) if j < n_sq else apk[h]
                res.append(mm(lhs.astype(BF16), unpack(ppk[h])))
            yield
            for h in heads:
                apk[h] = apk[h] + ppk[h] + res[h][:ch]
                if j < n_sq:
                    ppk[h] = res[h][ch:]
        egc = [jnp.exp(gcol[h]) for h in heads]
        rhs = [jnp.concatenate([v_ref[:, hls[h]] * beta[h], kb[h] * egc[h]], axis=1) for h in heads]
        prod = [mm(unpack(apk[h]), rhs[h].astype(BF16)) for h in heads]
        yield
        for h in heads:
            uw = rhs[h] + prod[h]
            u_sc[wr, h] = uw[:, :dh]
            w_sc[wr, h] = uw[:, dh:].astype(BF16)
            qg_sc[wr, h] = (q_ref[:, hls[h]] * egc[h]).astype(BF16)
            kdt_sc[wr, h] = (kt_ref[hls[h], :] * jnp.exp(gtot_t[h:h + 1, :] - grow[h])).astype(BF16)
        gtot_sc[wr] = gtot_t

    return phase_b(), phase_a()


def _mixer_in_body(x_ref, g_ref, w_ref, cw_ref, alog_ref, dtb_ref, ng_ref,
                   u_ref, o_ref,
                   buf_ref, q_sc, k_sc, kt_sc, v_sc, z_sc, gt_sc,
                   s_ref, u_sc, w_sc, qg_sc, attn_sc, kdt_sc, gtot_sc,
                   *, tb, n_t, nh, dh, cw_n):
    s = pl.program_id(0)
    dn_w = nh * dh

    @pl.when(s == 0)
    def _():
        for ref in (buf_ref, q_sc, k_sc, kt_sc, v_sc, z_sc, gt_sc,
                    s_ref, u_sc, w_sc, qg_sc, attn_sc, kdt_sc, gtot_sc):
            ref[...] = jnp.zeros(ref.shape, ref.dtype)

    @pl.when(s % n_t == 0)
    def _():
        buf_ref[:, 0:SUBLANES, :] = jnp.zeros((3 * dn_w // LANES, SUBLANES, LANES), F32)

    @pl.when((s - 2) % n_t == 0)
    def _():
        s_ref[...] = jnp.zeros(s_ref.shape, F32)

    cur = s % 2
    prv = 1 - cur
    zw = s % 3
    zr = (s + 1) % 3
    proj = _in_proj_stages(
        x_ref, g_ref, w_ref, cw_ref, alog_ref, dtb_ref,
        q_sc.at[cur], k_sc.at[cur], kt_sc.at[cur], v_sc.at[cur], z_sc.at[zw], u_ref, gt_sc.at[cur],
        buf_ref, tm=tb, dn_w=dn_w, nh=nh, dh=dh, cw_n=cw_n)
    phase_b, phase_a = _delta_stages(
        q_sc.at[prv], k_sc.at[prv], kt_sc.at[prv], v_sc.at[prv], z_sc.at[zr], gt_sc.at[prv], ng_ref,
        o_ref, s_ref, u_sc, w_sc, qg_sc, attn_sc, kdt_sc, gtot_sc, cur, prv,
        tb=tb, nh=nh, dh=dh, ch=DN_CHUNK)
    _interleave([proj, phase_b, phase_a])


def _mixer_in(x2, seq, g, w_cat, conv_w, alog, dtb, ng, *, nh, dh, tb):
    nt, d = x2.shape
    dn_w = nh * dh
    n_blk = nt // tb
    cw_n = conv_w.shape[0]
    const = lambda s: (0, 0)
    in_blk = lambda s: (jnp.minimum(s, n_blk - 1), 0)
    out_blk = lambda s: (jnp.maximum(s - 2, 0), 0)
    body = functools.partial(_mixer_in_body, tb=tb, n_t=seq // tb, nh=nh, dh=dh, cw_n=cw_n)
    two = lambda *shape: pltpu.VMEM((2,) + shape, F32)
    return pl.pallas_call(
        body,
        grid=(n_blk + 2,),
        in_specs=[pl.BlockSpec((tb, d), in_blk)]
        + [pl.BlockSpec(a.shape, const) for a in (g, w_cat, conv_w, alog, dtb, ng)],
        out_specs=[pl.BlockSpec((dn_w // LANES, tb, LANES),
                                lambda s: (0, jnp.minimum(s, n_blk - 1), 0)),
                   pl.BlockSpec((tb, dn_w), out_blk)],
        out_shape=[jax.ShapeDtypeStruct((dn_w // LANES, nt, LANES), F32),
                   jax.ShapeDtypeStruct((nt, dn_w), F32)],
        scratch_shapes=[
            pltpu.VMEM((3 * dn_w // LANES, tb + SUBLANES, LANES), F32),
            two(tb, dn_w), two(tb, dn_w), two(dn_w, tb), two(tb, dn_w),
            pltpu.VMEM((3, tb, dn_w), F32),
            two(SUBLANES, tb),
            pltpu.VMEM((nh, dh, dh), F32),
            pltpu.VMEM((2, nh, tb, dh), F32),
            pltpu.VMEM((2, nh, tb, dh), BF16),
            pltpu.VMEM((2, nh, tb, dh), BF16),
            pltpu.VMEM((2, nh, tb, tb), BF16),
            pltpu.VMEM((2, nh, dh, tb), BF16),
            pltpu.VMEM((2, SUBLANES, tb), F32),
        ],
        compiler_params=_params("arbitrary"),
        name="mixer_in",
    )(x2, g, w_cat, conv_w, alog, dtb, ng)


def _ftdot(a, b):
    return lax.dot_general(a.astype(BF16), b.astype(BF16), (((0,), (0,)), ((), ())),
                           preferred_element_type=F32)


def _cmul(xr, xi, yr, yi):
    return xr * yr - xi * yi, xr * yi + xi * yr


def _s5t_prep_body(are_ref, aim_ref, ldt_ref, btr_ref, bti_ref, ctr_ref, cti_ref, w1_ref, co_ref,
                   *, tc, cg, n_p):
    n_grp = w1_ref.shape[0]
    kw = tc * cg
    a_re = are_ref[...]
    a_im = aim_ref[...]
    dt = jnp.exp(ldt_ref[...])
    lre = a_re * dt
    lim = a_im * dt
    mag = jnp.exp(lre)
    ab = (mag * jnp.cos(lim), mag * jnp.sin(lim))
    nr, ni = ab[0] - 1.0, ab[1]
    den = a_re * a_re + a_im * a_im
    co = ((nr * a_re + ni * a_im) / den, (ni * a_re - nr * a_im) / den)
    inv_mag2 = jnp.exp(-2.0 * lre)
    ai = (ab[0] * inv_mag2, -ab[1] * inv_mag2)
    n_bits = tc.bit_length() - 1
    pos, neg = [ab], [ai]
    for _ in range(n_bits - 1):
        pos.append(_cmul(*pos[-1], *pos[-1]))
        neg.append(_cmul(*neg[-1], *neg[-1]))
    top = pos[0]
    for f in pos[1:]:
        top = _cmul(*top, *f)
    rows = [co, top] + pos + neg
    flat = [part for pair in rows for part in pair]
    cols = []
    for i in range(0, len(flat), SUBLANES):
        blk = flat[i:i + SUBLANES]
        blk = blk + [blk[-1]] * (SUBLANES - len(blk))
        t = jnp.concatenate(blk, axis=0).T
        cols += [t[:, j:j + 1] for j in range(SUBLANES)]
    pair = lambda k: (cols[2 * k], cols[2 * k + 1])
    co_c, top_c = pair(0), pair(1)
    pos_c = [pair(2 + k) for k in range(n_bits)]
    neg_c = [pair(2 + n_bits + k) for k in range(n_bits)]

    slot = lax.broadcasted_iota(jnp.int32, (1, kw), 1) // cg

    def power(factors):
        acc = None
        for k, (fr, fi) in enumerate(factors):
            bit = ((slot >> k) & 1) == 1
            term = (jnp.where(bit, fr, 1.0), jnp.where(bit, fi, 0.0))
            acc = term if acc is None else _cmul(*acc, *term)
        return acc

    place = jnp.where(lax.broadcasted_iota(jnp.int32, (cg, kw), 0)
                      == lax.broadcasted_iota(jnp.int32, (cg, kw), 1) % cg, 1.0, 0.0)
    place = place.astype(BF16)

    def rep(ref):
        x = ref[...]
        hi = x.astype(BF16)
        r1 = x - hi.astype(F32)
        mid = r1.astype(BF16)
        lo = (r1 - mid.astype(F32)).astype(BF16)
        return sum(jnp.dot(part, place, preferred_element_type=F32) for part in (hi, mid, lo))
    bt = _cmul(*co_c, rep(btr_ref), rep(bti_ref))
    pb = _cmul(*bt, *power(neg_c))
    qc = _cmul(rep(ctr_ref), rep(cti_ref), *power(pos_c))
    q1 = _cmul(*qc, *pos_c[0])
    bi = _cmul(*pb, *top_c)
    r2 = lax.broadcasted_iota(jnp.int32, (kw, kw), 0) // cg
    c2 = lax.broadcasted_iota(jnp.int32, (kw, kw), 1) // cg
    eye = jnp.where(lax.broadcasted_iota(jnp.int32, (n_p, n_p), 0)
                    == lax.broadcasted_iota(jnp.int32, (n_p, n_p), 1), 1.0, 0.0)
    for g in range(n_grp):
        rs = slice(g * n_p, (g + 1) * n_p)
        m_full = _ftdot(pb[0][rs], qc[0][rs]) - _ftdot(pb[1][rs], qc[1][rs])
        w1_ref[g, :, 0:kw] = jnp.where(r2 <= c2, m_full, 0.0).astype(w1_ref.dtype)
        w1_ref[g, :, kw:kw + n_p] = _ftdot(bi[0][rs], eye).astype(w1_ref.dtype)
        w1_ref[g, :, kw + n_p:kw + 2 * n_p] = _ftdot(bi[1][rs], eye).astype(w1_ref.dtype)
        co_ref[g, 0:n_p, :] = q1[0][rs].astype(co_ref.dtype)
        co_ref[g, n_p:2 * n_p, :] = (-q1[1][rs]).astype(co_ref.dtype)


def _s5t_prep(a_re, a_im, log_dt, b_re, b_im, c_re, c_im):
    g, p = a_re.shape
    cg = b_re.shape[-1]
    kw = T_CHUNK * cg
    row = lambda a: a.reshape(1, g * p)
    slots = lambda a: a.reshape(g * p, cg)
    return pl.pallas_call(
        functools.partial(_s5t_prep_body, tc=T_CHUNK, cg=cg, n_p=p),
        out_shape=[jax.ShapeDtypeStruct((g, kw, kw + 2 * p), BF16),
                   jax.ShapeDtypeStruct((g, 2 * p, kw), BF16)],
        compiler_params=pltpu.CompilerParams(vmem_limit_bytes=VMEM_LIMIT),
        name="s5t_prep",
    )(row(a_re), row(a_im), row(jnp.broadcast_to(log_dt[:, None], (g, p))),
      slots(b_re), slots(b_im), slots(c_re.transpose(0, 2, 1)), slots(c_im.transpose(0, 2, 1)))


def _s5t_tables_body(are_ref, aim_ref, ldt_ref, pin_re, pin_im, pout_re, pout_im, al_re, al_im,
                     *, tc, sc):
    lre = are_ref[...] * jnp.exp(ldt_ref[...]) * float(tc)
    lim = aim_ref[...] * jnp.exp(ldt_ref[...]) * float(tc)
    lane = lax.broadcasted_iota(jnp.int32, lre.shape, 1)
    sign = jnp.where((lane % LANES) < LANES // 2, -1.0, 1.0)
    n = lax.broadcasted_iota(jnp.int32, (sc, 1), 0).astype(F32) - float(sc // 2)
    m_out = jnp.exp(lre * n)
    m_in = jnp.exp(-(lre * n))
    ang = lim * n
    pin_re[...] = (m_in * jnp.cos(ang)).astype(pin_re.dtype)
    pin_im[...] = (sign * (-(m_in * jnp.sin(ang)))).astype(pin_im.dtype)
    pout_re[...] = m_out * jnp.cos(ang)
    pout_im[...] = sign * (m_out * jnp.sin(ang))
    m_al = jnp.exp(lre * float(sc))
    al_re[...] = m_al * jnp.cos(lim * float(sc))
    al_im[...] = sign * (m_al * jnp.sin(lim * float(sc)))


def _s5t_tables(a_re, a_im, log_dt):
    g, p = a_re.shape
    lanes = g * 2 * p
    spread = lambda a: jnp.broadcast_to(a[:, None, :], (g, 2, p)).reshape(1, lanes)
    tab16 = jax.ShapeDtypeStruct((S_CHUNK, lanes), BF16)
    tab32 = jax.ShapeDtypeStruct((S_CHUNK, lanes), F32)
    one = jax.ShapeDtypeStruct((1, lanes), F32)
    return pl.pallas_call(
        functools.partial(_s5t_tables_body, tc=T_CHUNK, sc=S_CHUNK),
        out_shape=[tab16, tab16, tab32, tab32, one, one], name="s5t_tables",
    )(spread(a_re), spread(a_im), spread(jnp.broadcast_to(log_dt[:, None], (g, p))))


def _atom_transpose(regs, atom, width):
    regs = list(regs)
    n = len(regs)
    d = n // 2
    while d:
        low = (atom & d) == 0
        for j in range(n):
            if j & d:
                continue
            a, b = regs[j], regs[j + d]
            regs[j] = jnp.where(low, a, pltpu.roll(b, d * width, axis=1))
            regs[j + d] = jnp.where(low, pltpu.roll(a, LANES - d * width, axis=1), b)
        d //= 2
    return regs


def _swap_halves(x):
    tiles = [pltpu.roll(x[:, t * LANES:(t + 1) * LANES], LANES // 2, axis=1)
             for t in range(x.shape[1] // LANES)]
    return jnp.concatenate(tiles, axis=1)


def _s5t_body(u_ref, w1_ref, co_ref, pin_re, pin_im, pout_re, pout_im, al_re, al_im,
              d_ref, wglu_ref, bglu_ref, y_ref,
              upk_ref, yloc_ref, sin_ref, xs_ref, carry_ref, udi_ref, *, rc, tc, sc, n_grp, cg):
    t = pl.program_id(1)
    kw = tc * cg
    n_q = u_ref.shape[0]
    atoms = LANES // cg

    @pl.when(t == 0)
    def _():
        carry_ref[...] = jnp.zeros(carry_ref.shape, F32)
        xs_ref[:, 0:SUBLANES, :] = jnp.zeros((n_grp, SUBLANES, LANES), F32)

    atom = lax.broadcasted_iota(jnp.int32, (rc, LANES), 1) // cg

    n_a = tc // atoms

    def pack():
        for q in range(n_q):
            for a in range(n_a):
                planes = [u_ref[q, pl.ds(a * atoms + k, rc, stride=tc), :] for k in range(atoms)]
                for k in range(atoms):
                    udi_ref[q, a * atoms + k] = planes[k]
                for r, tile in enumerate(_atom_transpose(planes, atom, cg)):
                    upk_ref[q * atoms + r, :, a * LANES:(a + 1) * LANES] = tile.astype(BF16)
                yield

    per_stage = atoms // n_a

    def local():
        for _ in range(n_a):
            yield
        for g in range(n_grp):
            res = jnp.dot(upk_ref[g], w1_ref[g], preferred_element_type=F32)
            yloc_ref[g] = res[:, :kw]
            sin_ref[g] = res[:, kw:]
            if g % per_stage == per_stage - 1:
                yield

    _interleave([pack(), local()])

    ri = lax.broadcasted_iota(jnp.int32, (sc, sc), 0)
    ci = lax.broadcasted_iota(jnp.int32, (sc, sc), 1)
    tri = jnp.where(ci <= ri, 1.0, 0.0).astype(BF16)
    n_c2 = rc // sc
    ss = [jnp.concatenate([sin_ref[g, c2 * sc:(c2 + 1) * sc, :] for g in range(n_grp)], axis=1)
          for c2 in range(n_c2)]
    ss_sw = [_swap_halves(s) for s in ss]
    xin = [pin_re[...] * ss[c2].astype(BF16) + pin_im[...] * ss_sw[c2].astype(BF16)
           for c2 in range(n_c2)]
    psum = [jnp.dot(tri, x, preferred_element_type=F32) for x in xin]
    psum_sw = [_swap_halves(p) for p in psum]
    carry = carry_ref[...]
    carry_sw = _swap_halves(carry)
    carries = []
    for c2 in range(rc // sc):
        carries.append((carry, carry_sw))
        last = psum[c2][sc - 1:sc, :] + carry
        last_sw = psum_sw[c2][sc - 1:sc, :] + carry_sw
        carry = al_re[...] * last + al_im[...] * last_sw
        carry_sw = al_re[...] * last_sw - al_im[...] * last
    carry_ref[...] = carry
    for c2 in range(rc // sc):
        x = (pout_re[...] * (psum[c2] + carries[c2][0])
             + pout_im[...] * (psum_sw[c2] + carries[c2][1]))
        for g in range(n_grp):
            xs_ref[g, SUBLANES + c2 * sc:SUBLANES + (c2 + 1) * sc, :] = x[:, g * LANES:(g + 1) * LANES]

    for g in range(n_grp):
        xprev = xs_ref[g, SUBLANES - 1:SUBLANES - 1 + rc, :].astype(BF16)
        yloc_ref[g] = yloc_ref[g] + jnp.dot(xprev, co_ref[g], preferred_element_type=F32)
    xs_ref[:, SUBLANES - 1:SUBLANES, :] = xs_ref[:, SUBLANES - 1 + rc:SUBLANES + rc, :]

    wglu = wglu_ref[...].astype(BF16)

    def unpack(a):
        for _ in range(a * n_q):
            yield
        planes = [[None] * n_q for _ in range(atoms)]
        for q in range(n_q):
            tiles = [yloc_ref[q * atoms + r, :, a * LANES:(a + 1) * LANES] for r in range(atoms)]
            for k, tile in enumerate(_atom_transpose(tiles, atom, cg)):
                planes[k][q] = tile
            yield
        for k in range(atoms):
            i = a * atoms + k
            uu = jnp.concatenate([udi_ref[q, i] for q in range(n_q)], axis=1)
            y = jax.nn.gelu(jnp.concatenate(planes[k], axis=1) + d_ref[...] * uu)
            gate = jax.nn.sigmoid(jnp.dot(y.astype(BF16), wglu, preferred_element_type=F32)
                                  + bglu_ref[...])
            out = y * gate
            for q in range(n_q):
                y_ref[q, pl.ds(i, rc, stride=tc), :] = out[:, q * LANES:(q + 1) * LANES]
            if k % 2 == 1:
                yield

    _interleave([unpack(a) for a in range(n_a)])


def _s5t(u4, w1, co, tabs, d, wglu, bglu, bsz, seq, *, rc):
    n_q, nt, _ = u4.shape
    n_grp, kw, wcols = w1.shape
    n_state = (wcols - kw) // 2
    cg = kw // T_CHUNK
    rows = rc * T_CHUNK
    n_t = seq // rows
    blk = pl.BlockSpec((n_q, rows, LANES), lambda b, t: (0, b * n_t + t, 0))
    c2 = lambda b, t: (0, 0)
    c3 = lambda b, t: (0, 0, 0)
    whole = lambda a: pl.BlockSpec(a.shape, c3 if a.ndim == 3 else c2)
    body = functools.partial(_s5t_body, rc=rc, tc=T_CHUNK, sc=S_CHUNK, n_grp=n_grp, cg=cg)
    return pl.pallas_call(
        body,
        grid=(bsz, n_t),
        in_specs=[blk] + [whole(a) for a in (w1, co, *tabs, d, wglu, bglu)],
        out_specs=blk,
        out_shape=jax.ShapeDtypeStruct((n_q, nt, LANES), F32),
        scratch_shapes=[
            pltpu.VMEM((n_grp, rc, kw), BF16),
            pltpu.VMEM((n_grp, rc, kw), F32),
            pltpu.VMEM((n_grp, rc, 2 * n_state), F32),
            pltpu.VMEM((n_grp, rc + SUBLANES, 2 * n_state), F32),
            pltpu.VMEM((1, n_grp * 2 * n_state), F32),
            pltpu.VMEM((n_q, T_CHUNK, rc, LANES), F32),
        ],
        compiler_params=_params("arbitrary", "arbitrary"),
        name="s5t",
    )(u4, w1, co, *tabs, d, wglu, bglu)


def _mem_kv_body(m_ref, g_ref, wk_ref, wv_ref, wq_ref, wo_ref, qk_ref, vw_ref, *, xh):
    bsz, d, _ = qk_ref.shape
    n_mem = m_ref.shape[0] // bsz
    xd = d // xh
    scale = xd ** -0.5
    mn = _rms(m_ref[...], g_ref[...]).astype(BF16)
    k = jnp.dot(mn, wk_ref[...].astype(BF16), preferred_element_type=F32).astype(BF16)
    v = jnp.dot(mn, wv_ref[...].astype(BF16), preferred_element_type=F32).astype(BF16)
    wq = wq_ref[...].astype(BF16)
    wo = wo_ref[...].astype(BF16)
    nt_dims = (((1,), (1,)), ((), ()))
    for b in range(bsz):
        rows = slice(b * n_mem, (b + 1) * n_mem)
        for h in range(xh):
            hl = slice(h * xd, (h + 1) * xd)
            ml = slice(h * n_mem, (h + 1) * n_mem)
            qk = lax.dot_general(wq[:, hl], k[rows, hl], nt_dims, preferred_element_type=F32)
            qk_ref[b, :, ml] = (qk * scale).astype(BF16)
            vw_ref[b, ml, :] = jnp.dot(v[rows, hl], wo[hl, :],
                                       preferred_element_type=F32).astype(BF16)


def _mem_kv(mem, g, wk, wv, wq, wo):
    bsz, n_mem, d = mem.shape
    return pl.pallas_call(
        functools.partial(_mem_kv_body, xh=X_HEADS),
        out_shape=[
            jax.ShapeDtypeStruct((bsz, d, X_HEADS * n_mem), BF16),
            jax.ShapeDtypeStruct((bsz, X_HEADS * n_mem, d), BF16),
        ],
        compiler_params=pltpu.CompilerParams(vmem_limit_bytes=VMEM_LIMIT),
        name="mem_kv",
    )(mem.reshape(bsz * n_mem, d), g, wk, wv, wq, wo)


def _mix_attn_body(x_ref, o_ref, y_ref, wo_ref, gx_ref, qk_ref, vw_ref, h_ref, *, xh, n_sub):
    dn_w = o_ref.shape[1]
    n_mem = qk_ref.shape[2] // xh
    sub = x_ref.shape[0] // n_sub
    mm = functools.partial(jnp.dot, preferred_element_type=F32)
    wo = wo_ref[...].astype(BF16)

    def rows_gen(r0):
        rows = slice(r0, r0 + sub)
        y = jnp.concatenate([y_ref[q, rows, :] for q in range(y_ref.shape[0])], axis=1)
        mix = mm(o_ref[rows, :].astype(BF16), wo[0:dn_w, :]) + mm(y.astype(BF16), wo[dn_w:, :])
        yield
        h1 = x_ref[rows, :] + mix
        s = mm(_rms(h1, gx_ref[...]).astype(BF16), qk_ref[0])
        yield
        parts = []
        for h in range(xh):
            sh = s[:, h * n_mem:(h + 1) * n_mem]
            e = jnp.exp(sh - jnp.max(sh, axis=-1, keepdims=True))
            parts.append((e / jnp.sum(e, axis=-1, keepdims=True)).astype(BF16))
        att = mm(jnp.concatenate(parts, axis=1), vw_ref[0])
        yield
        h_ref[rows, :] = h1 + att

    _interleave([rows_gen(i * sub) for i in range(n_sub)])


def _mix_attn(x2, o, y, w_out, gx, qk, vw, bsz, seq, *, tm):
    nt, d = x2.shape
    dn_w = o.shape[1]
    n_t = seq // tm
    row = lambda b, t: (b * n_t + t, 0)
    c2 = lambda b, t: (0, 0)
    per_batch = lambda a: pl.BlockSpec((1,) + a.shape[1:], lambda b, t: (b, 0, 0))
    body = functools.partial(_mix_attn_body, xh=X_HEADS, n_sub=2)
    return pl.pallas_call(
        body,
        grid=(bsz, n_t),
        in_specs=[
            pl.BlockSpec((tm, d), row),
            pl.BlockSpec((tm, dn_w), row),
            pl.BlockSpec((y.shape[0], tm, LANES), lambda b, t: (0, b * n_t + t, 0)),
            pl.BlockSpec(w_out.shape, c2, pipeline_mode=pl.Buffered(1)),
            pl.BlockSpec((1, d), c2),
            per_batch(qk),
            per_batch(vw),
        ],
        out_specs=pl.BlockSpec((tm, d), row),
        out_shape=jax.ShapeDtypeStruct((nt, d), F32),
        compiler_params=_params("arbitrary", "arbitrary"),
        name="mix_attn",
    )(x2, o, y, w_out, gx, qk, vw)


def _ffn_body(h_ref, gf_ref, wg_ref, wu_ref, wd_ref, gl_ref, out_ref, *, final, fc):
    h = h_ref[...]
    hn = _rms(h, gf_ref[...]).astype(BF16)
    dff = wg_ref.shape[1]

    def down(gate, up, c0):
        act = (gate * jax.nn.sigmoid(gate) * up).astype(BF16)
        return jnp.dot(act, wd_ref[c0:c0 + fc, :].astype(BF16), preferred_element_type=F32)

    h3 = h
    prev = None
    for c0 in range(0, dff, fc):
        gate = jnp.dot(hn, wg_ref[:, c0:c0 + fc].astype(BF16), preferred_element_type=F32)
        up = jnp.dot(hn, wu_ref[:, c0:c0 + fc].astype(BF16), preferred_element_type=F32)
        if prev is not None:
            h3 = h3 + down(*prev)
        prev = (gate, up, c0)
    h3 = h3 + down(*prev)
    out_ref[...] = _rms(h3, gl_ref[...]) if final else h3


def _ffn(h, gf, wg, wu, wd, gl, *, tm, final):
    nt, d = h.shape
    dff = wg.shape[1]
    c2 = lambda i: (0, 0)
    row = lambda i: (i, 0)
    return pl.pallas_call(
        functools.partial(_ffn_body, final=final, fc=2 * LANES),
        grid=(nt // tm,),
        in_specs=[
            pl.BlockSpec((tm, d), row),
            pl.BlockSpec((1, d), c2),
            pl.BlockSpec((d, dff), c2, pipeline_mode=pl.Buffered(1)),
            pl.BlockSpec((d, dff), c2, pipeline_mode=pl.Buffered(1)),
            pl.BlockSpec((dff, d), c2, pipeline_mode=pl.Buffered(1)),
            pl.BlockSpec((1, d), c2),
        ],
        out_specs=pl.BlockSpec((tm, d), row),
        out_shape=jax.ShapeDtypeStruct((nt, d), F32),
        compiler_params=_params("arbitrary"),
        name="ffn",
    )(h, gf, wg, wu, wd, gl)


def _layer(h2, mem, bsz, seq, norm_mix_g, w_in, conv_w, dn_a_log, dn_dt_bias, dn_norm_g,
           s5_a_re, s5_a_im, s5_b_re, s5_b_im, s5_c_re, s5_c_im, s5_d, s5_log_dt,
           s5_w_glu, s5_b_glu, w_out, norm_x_g, norm_mem_g, w_xq, w_xk, w_xv, w_xo):
    d = h2.shape[1]
    nh = dn_a_log.shape[0]
    dh = dn_norm_g.shape[0]
    dn_w = nh * dh
    s5_w = s5_a_re.shape[0] * s5_b_re.shape[-1]

    off_a = 4 * dn_w
    off_u = off_a + 2 * nh
    w_ab = jnp.pad(w_in[:, off_a:off_u].astype(BF16), ((0, 0), (0, LANES - 2 * nh)))
    w_cat = jnp.concatenate([w_in[:, :off_a].astype(BF16), w_in[:, off_u:].astype(BF16), w_ab],
                            axis=1)
    alog = jnp.pad(dn_a_log, (0, LANES - nh)).reshape(1, LANES)
    dtb = jnp.pad(dn_dt_bias, (0, LANES - nh)).reshape(1, LANES)

    u, o = _mixer_in(h2, seq, norm_mix_g.reshape(1, d), w_cat, conv_w, alog, dtb,
                     dn_norm_g.reshape(1, dh), nh=nh, dh=dh, tb=MIXER_ROWS)

    w1, co = _s5t_prep(s5_a_re, s5_a_im, s5_log_dt, s5_b_re, s5_b_im, s5_c_re, s5_c_im)
    tabs = _s5t_tables(s5_a_re, s5_a_im, s5_log_dt)
    y = _s5t(u, w1, co, tabs, s5_d.reshape(1, s5_w), s5_w_glu, s5_b_glu.reshape(1, s5_w),
             bsz, seq, rc=S5_BLOCKS)

    qk_mem, vw_mem = _mem_kv(mem, norm_mem_g.reshape(1, d), w_xk, w_xv, w_xq, w_xo)
    return _mix_attn(h2, o, y, w_out, norm_x_g.reshape(1, d), qk_mem, vw_mem, bsz, seq,
                     tm=ATTN_ROWS)


def kernel(x, mem, norm_mix_g, w_in, conv_w, dn_a_log, dn_dt_bias, dn_norm_g, s5_a_re, s5_a_im,
           s5_b_re, s5_b_im, s5_c_re, s5_c_im, s5_d, s5_log_dt, s5_w_glu, s5_b_glu, w_out,
           norm_x_g, norm_mem_g, w_xq, w_xk, w_xv, w_xo, norm_ffn_g, w_gate, w_up, w_down,
           norm_final_g):
    bsz, seq, d = x.shape
    depth = w_in.shape[0]
    h = x.reshape(bsz * seq, d)
    for l in range(depth):
        h = _layer(h, mem, bsz, seq, norm_mix_g[l], w_in[l], conv_w[l], dn_a_log[l],
                   dn_dt_bias[l], dn_norm_g[l], s5_a_re[l], s5_a_im[l], s5_b_re[l], s5_b_im[l],
                   s5_c_re[l], s5_c_im[l], s5_d[l], s5_log_dt[l], s5_w_glu[l], s5_b_glu[l],
                   w_out[l], norm_x_g[l], norm_mem_g[l], w_xq[l], w_xk[l], w_xv[l], w_xo[l])
        h = _ffn(h, norm_ffn_g[l].reshape(1, d), w_gate[l], w_up[l], w_down[l],
                 norm_final_g.reshape(1, d), tm=FFN_ROWS, final=l == depth - 1)
    return h.reshape(bsz, seq, d)
```

```python
import functools

import jax
import jax.numpy as jnp
from jax import lax
from jax.experimental import pallas as pl
from jax.experimental.pallas import tpu as pltpu

F32 = jnp.float32
BF16 = jnp.bfloat16
EPS = 1e-6
HIGHEST = lax.Precision.HIGHEST

LANES = 128
SUBLANES = 8
DN_CHUNK = 64
T_CHUNK = 16
S_CHUNK = 16
X_HEADS = 4
VMEM_LIMIT = 58 * 1024 * 1024

MIXER_ROWS = 256
S5_BLOCKS = 128
ATTN_ROWS = 1024
FFN_ROWS = 1024


def _fdot(a, b):
    return jnp.dot(a, b, precision=HIGHEST, preferred_element_type=F32)


def _rms(x, g):
    return x * lax.rsqrt(jnp.mean(x * x, axis=-1, keepdims=True) + EPS) * g


def _interleave(gens, stages_per_round=None):
    pending = [(gen, 1 if stages_per_round is None else stages_per_round[i])
               for i, gen in enumerate(gens)]
    while pending:
        for item in list(pending):
            gen, n = item
            try:
                for _ in range(n):
                    next(gen)
            except StopIteration:
                pending.remove(item)


def _params(*sem):
    return pltpu.CompilerParams(dimension_semantics=sem, vmem_limit_bytes=VMEM_LIMIT)


def _in_proj_stages(x_ref, g_ref, w_ref, cw_ref, alog_ref, dtb_ref,
                    q_ref, k_ref, kt_ref, v_ref, z_ref, u_ref, gt_ref,
                    buf_ref, *, tm, dn_w, nh, dh, cw_n):
    xnb = _rms(x_ref[...], g_ref[...]).astype(BF16)
    gw = 2 * dh

    def conv_silu(p, cols):
        outs = []
        for i in range(gw // LANES):
            c0 = cols.start + i * LANES
            s = c0 // LANES
            pc = p[:, i * LANES:(i + 1) * LANES]
            buf_ref[s, SUBLANES:SUBLANES + tm, :] = pc
            acc = cw_ref[cw_n - 1:cw_n, c0:c0 + LANES] * pc
            for j in range(cw_n - 1):
                off = SUBLANES - (cw_n - 1) + j
                acc = acc + cw_ref[j:j + 1, c0:c0 + LANES] * buf_ref[s, off:off + tm, :]
            buf_ref[s, 0:SUBLANES, :] = buf_ref[s, tm:tm + SUBLANES, :]
            outs.append(acc * jax.nn.sigmoid(acc))
        return jnp.concatenate(outs, axis=1)

    def l2n(a, scale):
        parts = []
        for i in range(gw // dh):
            ah = a[:, i * dh:(i + 1) * dh]
            parts.append(ah * (lax.rsqrt(jnp.sum(ah * ah, -1, keepdims=True) + EPS) * scale))
        return jnp.concatenate(parts, axis=1)

    def ep_q(p, c0):
        q_ref[:, c0:c0 + gw] = l2n(conv_silu(p, slice(c0, c0 + gw)), dh ** -0.5)

    def ep_k(p, c0):
        kn = l2n(conv_silu(p, slice(dn_w + c0, dn_w + c0 + gw)), 1.0)
        k_ref[:, c0:c0 + gw] = kn
        kt_ref[c0:c0 + gw, :] = kn.T

    def ep_v(p, c0):
        v_ref[:, c0:c0 + gw] = conv_silu(p, slice(2 * dn_w + c0, 2 * dn_w + c0 + gw))

    def ep_z(p, c0):
        z_ref[:, c0:c0 + gw] = p

    def ep_u(p, c0):
        for j in range(gw // LANES):
            u_ref[c0 // LANES + j] = p[:, j * LANES:(j + 1) * LANES]

    def ep_gates(ab, c0):
        lane = lax.broadcasted_iota(jnp.int32, ab.shape, 1)
        g = -jnp.exp(alog_ref[...]) * jnp.logaddexp(ab + dtb_ref[...], 0.0)
        gates = jnp.where(lane < nh, g, jax.nn.sigmoid(ab))
        gt_ref[...] = gates.T[0:SUBLANES, :]

    groups = []
    for i, ep in enumerate((ep_q, ep_k, ep_v, ep_z, ep_u)):
        groups += [(ep, i * dn_w + c0, gw, c0) for c0 in range(0, dn_w, gw)]
    groups.append((ep_gates, 5 * dn_w, LANES, 0))

    prev = None
    for ep, w0, width, c0 in groups:
        p = jnp.dot(xnb, w_ref[:, w0:w0 + width], preferred_element_type=F32)
        if prev is not None:
            prev[0](prev[1], prev[2])
        prev = (ep, p, c0)
        yield
    prev[0](prev[1], prev[2])


def _delta_stages(q_ref, k_ref, kt_ref, v_ref, z_ref, gt_ref, ng_ref,
                  o_ref, s_ref, u_sc, w_sc, qg_sc, attn_sc, kdt_sc, gtot_sc, wr, rd,
                  *, tb, nh, dh, ch):
    nc = tb // ch
    heads = range(nh)
    hls = [slice(h * dh, (h + 1) * dh) for h in heads]
    mm = functools.partial(jnp.dot, preferred_element_type=F32)

    def phase_b():
        gtot_b = gtot_sc[rd]
        s_all = [s_ref[h] for h in heads]
        for cidx in range(nc):
            rows = slice(cidx * ch, (cidx + 1) * ch)
            sb = [s_all[h].astype(BF16) for h in heads]
            ws = [mm(w_sc[rd, h, rows, :], sb[h]) for h in heads]
            qs = [mm(qg_sc[rd, h, rows, :], sb[h]) for h in heads]
            yield
            vb = [(u_sc[rd, h, rows, :] - ws[h]).astype(BF16) for h in heads]
            av = [mm(attn_sc[rd, h, rows, rows], vb[h]) for h in heads]
            kv = [mm(kdt_sc[rd, h, :, rows], vb[h]) for h in heads]
            yield
            outs = []
            for h in heads:
                a_last = jnp.exp(gtot_b[h:h + 1, cidx * ch:cidx * ch + 1])
                s_all[h] = s_all[h] * a_last + kv[h]
                o = qs[h] + av[h]
                o = o * lax.rsqrt(jnp.mean(o * o, -1, keepdims=True) + EPS)
                zh = z_ref[rows, hls[h]]
                outs.append(o * ng_ref[...] * (zh * jax.nn.sigmoid(zh)))
            o_ref[rows, :] = jnp.concatenate(outs, axis=1)
        for h in heads:
            s_ref[h] = s_all[h]

    r = lax.broadcasted_iota(jnp.int32, (tb, tb), 0)
    c = lax.broadcasted_iota(jnp.int32, (tb, tb), 1)
    same = (r // ch) == (c // ch)
    causal_bd = same & (c <= r)
    strict_bd = same & (c < r)
    gt = gt_ref[...]
    gcum_t = _fdot(gt, jnp.where(same & (r <= c), 1.0, 0.0))
    gtot_t = _fdot(gt, jnp.where(same, 1.0, 0.0))
    gcum = gcum_t.T
    gates = gt.T

    def pack(x):
        acc = x[0:ch, :]
        for i in range(1, nc):
            acc = acc + x[i * ch:(i + 1) * ch, :]
        return acc

    def unpack(x):
        return jnp.where(same, jnp.concatenate([x] * nc, axis=0), 0.0).astype(BF16)

    n_sq = max(1, (ch - 1).bit_length() - 1)

    def phase_a():
        gcol = [gcum[:, h:h + 1] for h in heads]
        grow = [gcum_t[h:h + 1, :] for h in heads]
        beta = [gates[:, nh + h:nh + h + 1] for h in heads]
        kb = [k_ref[:, hls[h]] * beta[h] for h in heads]
        gram = [mm(jnp.concatenate([kb[h], q_ref[:, hls[h]]], axis=0).astype(BF16),
                   kt_ref[hls[h], :].astype(BF16)) for h in heads]
        yield
        ppk, apk = [], []
        for h in heads:
            decay = jnp.exp(jnp.where(causal_bd, gcol[h] - grow[h], -jnp.inf))
            p_bd = -jnp.where(strict_bd, gram[h][:tb] * decay, 0.0)
            attn_sc[wr, h] = (gram[h][tb:] * decay).astype(BF16)
            pk = pack(p_bd)
            apk.append(pk)
            ppk.append(mm(pk.astype(BF16), p_bd.astype(BF16)))
        yield
        for j in range(1, n_sq + 1):
            res = []
            for h in heads:
                lhs = jnp.concatenate([apk[h], ppk[h]], axis=0) if j < n_sq else apk[h]
                res.append(mm(lhs.astype(BF16), unpack(ppk[h])))
            yield
            for h in heads:
                apk[h] = apk[h] + ppk[h] + res[h][:ch]
                if j < n_sq:
                    ppk[h] = res[h][ch:]
        egc = [jnp.exp(gcol[h]) for h in heads]
        rhs = [jnp.concatenate([v_ref[:, hls[h]] * beta[h], kb[h] * egc[h]], axis=1) for h in heads]
        prod = [mm(unpack(apk[h]), rhs[h].astype(BF16)) for h in heads]
        yield
        for h in heads:
            uw = rhs[h] + prod[h]
            u_sc[wr, h] = uw[:, :dh]
            w_sc[wr, h] = uw[:, dh:].astype(BF16)
            qg_sc[wr, h] = (q_ref[:, hls[h]] * egc[h]).astype(BF16)
            kdt_sc[wr, h] = (kt_ref[hls[h], :] * jnp.exp(gtot_t[h:h + 1, :] - grow[h])).astype(BF16)
        gtot_sc[wr] = gtot_t

    return phase_b(), phase_a()


def _mixer_in_body(x_ref, g_ref, w_ref, cw_ref, alog_ref, dtb_ref, ng_ref,
                   u_ref, o_ref,
                   buf_ref, q_sc, k_sc, kt_sc, v_sc, z_sc, gt_sc,
                   s_ref, u_sc, w_sc, qg_sc, attn_sc, kdt_sc, gtot_sc,
                   *, tb, n_t, nh, dh, cw_n):
    s = pl.program_id(0)
    dn_w = nh * dh

    @pl.when(s == 0)
    def _():
        for ref in (buf_ref, q_sc, k_sc, kt_sc, v_sc, z_sc, gt_sc,
                    s_ref, u_sc, w_sc, qg_sc, attn_sc, kdt_sc, gtot_sc):
            ref[...] = jnp.zeros(ref.shape, ref.dtype)

    @pl.when(s % n_t == 0)
    def _():
        buf_ref[:, 0:SUBLANES, :] = jnp.zeros((3 * dn_w // LANES, SUBLANES, LANES), F32)

    @pl.when((s - 2) % n_t == 0)
    def _():
        s_ref[...] = jnp.zeros(s_ref.shape, F32)

    cur = s % 2
    prv = 1 - cur
    zw = s % 3
    zr = (s + 1) % 3
    proj = _in_proj_stages(
        x_ref, g_ref, w_ref, cw_ref, alog_ref, dtb_ref,
        q_sc.at[cur], k_sc.at[cur], kt_sc.at[cur], v_sc.at[cur], z_sc.at[zw], u_ref, gt_sc.at[cur],
        buf_ref, tm=tb, dn_w=dn_w, nh=nh, dh=dh, cw_n=cw_n)
    phase_b, phase_a = _delta_stages(
        q_sc.at[prv], k_sc.at[prv], kt_sc.at[prv], v_sc.at[prv], z_sc.at[zr], gt_sc.at[prv], ng_ref,
        o_ref, s_ref, u_sc, w_sc, qg_sc, attn_sc, kdt_sc, gtot_sc, cur, prv,
        tb=tb, nh=nh, dh=dh, ch=DN_CHUNK)
    _interleave([proj, phase_b, phase_a])


def _mixer_in(x2, seq, g, w_cat, conv_w, alog, dtb, ng, *, nh, dh, tb):
    nt, d = x2.shape
    dn_w = nh * dh
    n_blk = nt // tb
    cw_n = conv_w.shape[0]
    const = lambda s: (0, 0)
    in_blk = lambda s: (jnp.minimum(s, n_blk - 1), 0)
    out_blk = lambda s: (jnp.maximum(s - 2, 0), 0)
    body = functools.partial(_mixer_in_body, tb=tb, n_t=seq // tb, nh=nh, dh=dh, cw_n=cw_n)
    two = lambda *shape: pltpu.VMEM((2,) + shape, F32)
    return pl.pallas_call(
        body,
        grid=(n_blk + 2,),
        in_specs=[pl.BlockSpec((tb, d), in_blk)]
        + [pl.BlockSpec(a.shape, const) for a in (g, w_cat, conv_w, alog, dtb, ng)],
        out_specs=[pl.BlockSpec((dn_w // LANES, tb, LANES),
                                lambda s: (0, jnp.minimum(s, n_blk - 1), 0)),
                   pl.BlockSpec((tb, dn_w), out_blk)],
        out_shape=[jax.ShapeDtypeStruct((dn_w // LANES, nt, LANES), F32),
                   jax.ShapeDtypeStruct((nt, dn_w), F32)],
        scratch_shapes=[
            pltpu.VMEM((3 * dn_w // LANES, tb + SUBLANES, LANES), F32),
            two(tb, dn_w), two(tb, dn_w), two(dn_w, tb), two(tb, dn_w),
            pltpu.VMEM((3, tb, dn_w), F32),
            two(SUBLANES, tb),
            pltpu.VMEM((nh, dh, dh), F32),
            pltpu.VMEM((2, nh, tb, dh), F32),
            pltpu.VMEM((2, nh, tb, dh), BF16),
            pltpu.VMEM((2, nh, tb, dh), BF16),
            pltpu.VMEM((2, nh, tb, tb), BF16),
            pltpu.VMEM((2, nh, dh, tb), BF16),
            pltpu.VMEM((2, SUBLANES, tb), F32),
        ],
        compiler_params=_params("arbitrary"),
        name="mixer_in",
    )(x2, g, w_cat, conv_w, alog, dtb, ng)


def _ftdot(a, b):
    return lax.dot_general(a.astype(BF16), b.astype(BF16), (((0,), (0,)), ((), ())),
                           preferred_element_type=F32)


def _cmul(xr, xi, yr, yi):
    return xr * yr - xi * yi, xr * yi + xi * yr


def _s5t_prep_body(are_ref, aim_ref, ldt_ref, btr_ref, bti_ref, ctr_ref, cti_ref, w1_ref, co_ref,
                   *, tc, cg, n_p):
    n_grp = w1_ref.shape[0]
    kw = tc * cg
    a_re = are_ref[...]
    a_im = aim_ref[...]
    dt = jnp.exp(ldt_ref[...])
    lre = a_re * dt
    lim = a_im * dt
    mag = jnp.exp(lre)
    ab = (mag * jnp.cos(lim), mag * jnp.sin(lim))
    nr, ni = ab[0] - 1.0, ab[1]
    den = a_re * a_re + a_im * a_im
    co = ((nr * a_re + ni * a_im) / den, (ni * a_re - nr * a_im) / den)
    inv_mag2 = jnp.exp(-2.0 * lre)
    ai = (ab[0] * inv_mag2, -ab[1] * inv_mag2)
    n_bits = tc.bit_length() - 1
    pos, neg = [ab], [ai]
    for _ in range(n_bits - 1):
        pos.append(_cmul(*pos[-1], *pos[-1]))
        neg.append(_cmul(*neg[-1], *neg[-1]))
    top = pos[0]
    for f in pos[1:]:
        top = _cmul(*top, *f)
    rows = [co, top] + pos + neg
    flat = [part for pair in rows for part in pair]
    cols = []
    for i in range(0, len(flat), SUBLANES):
        blk = flat[i:i + SUBLANES]
        blk = blk + [blk[-1]] * (SUBLANES - len(blk))
        t = jnp.concatenate(blk, axis=0).T
        cols += [t[:, j:j + 1] for j in range(SUBLANES)]
    pair = lambda k: (cols[2 * k], cols[2 * k + 1])
    co_c, top_c = pair(0), pair(1)
    pos_c = [pair(2 + k) for k in range(n_bits)]
    neg_c = [pair(2 + n_bits + k) for k in range(n_bits)]

    slot = lax.broadcasted_iota(jnp.int32, (1, kw), 1) // cg

    def power(factors):
        acc = None
        for k, (fr, fi) in enumerate(factors):
            bit = ((slot >> k) & 1) == 1
            term = (jnp.where(bit, fr, 1.0), jnp.where(bit, fi, 0.0))
            acc = term if acc is None else _cmul(*acc, *term)
        return acc

    place = jnp.where(lax.broadcasted_iota(jnp.int32, (cg, kw), 0)
                      == lax.broadcasted_iota(jnp.int32, (cg, kw), 1) % cg, 1.0, 0.0)
    place = place.astype(BF16)

    def rep(ref):
        x = ref[...]
        hi = x.astype(BF16)
        r1 = x - hi.astype(F32)
        mid = r1.astype(BF16)
        lo = (r1 - mid.astype(F32)).astype(BF16)
        return sum(jnp.dot(part, place, preferred_element_type=F32) for part in (hi, mid, lo))
    bt = _cmul(*co_c, rep(btr_ref), rep(bti_ref))
    pb = _cmul(*bt, *power(neg_c))
    qc = _cmul(rep(ctr_ref), rep(cti_ref), *power(pos_c))
    q1 = _cmul(*qc, *pos_c[0])
    bi = _cmul(*pb, *top_c)
    r2 = lax.broadcasted_iota(jnp.int32, (kw, kw), 0) // cg
    c2 = lax.broadcasted_iota(jnp.int32, (kw, kw), 1) // cg
    eye = jnp.where(lax.broadcasted_iota(jnp.int32, (n_p, n_p), 0)
                    == lax.broadcasted_iota(jnp.int32, (n_p, n_p), 1), 1.0, 0.0)
    for g in range(n_grp):
        rs = slice(g * n_p, (g + 1) * n_p)
        m_full = _ftdot(pb[0][rs], qc[0][rs]) - _ftdot(pb[1][rs], qc[1][rs])
        w1_ref[g, :, 0:kw] = jnp.where(r2 <= c2, m_full, 0.0).astype(w1_ref.dtype)
        w1_ref[g, :, kw:kw + n_p] = _ftdot(bi[0][rs], eye).astype(w1_ref.dtype)
        w1_ref[g, :, kw + n_p:kw + 2 * n_p] = _ftdot(bi[1][rs], eye).astype(w1_ref.dtype)
        co_ref[g, 0:n_p, :] = q1[0][rs].astype(co_ref.dtype)
        co_ref[g, n_p:2 * n_p, :] = (-q1[1][rs]).astype(co_ref.dtype)


def _s5t_prep(a_re, a_im, log_dt, b_re, b_im, c_re, c_im):
    g, p = a_re.shape
    cg = b_re.shape[-1]
    kw = T_CHUNK * cg
    row = lambda a: a.reshape(1, g * p)
    slots = lambda a: a.reshape(g * p, cg)
    return pl.pallas_call(
        functools.partial(_s5t_prep_body, tc=T_CHUNK, cg=cg, n_p=p),
        out_shape=[jax.ShapeDtypeStruct((g, kw, kw + 2 * p), BF16),
                   jax.ShapeDtypeStruct((g, 2 * p, kw), BF16)],
        compiler_params=pltpu.CompilerParams(vmem_limit_bytes=VMEM_LIMIT),
        name="s5t_prep",
    )(row(a_re), row(a_im), row(jnp.broadcast_to(log_dt[:, None], (g, p))),
      slots(b_re), slots(b_im), slots(c_re.transpose(0, 2, 1)), slots(c_im.transpose(0, 2, 1)))


def _s5t_tables_body(are_ref, aim_ref, ldt_ref, pin_re, pin_im, pout_re, pout_im, al_re, al_im,
                     *, tc, sc):
    lre = are_ref[...] * jnp.exp(ldt_ref[...]) * float(tc)
    lim = aim_ref[...] * jnp.exp(ldt_ref[...]) * float(tc)
    lane = lax.broadcasted_iota(jnp.int32, lre.shape, 1)
    sign = jnp.where((lane % LANES) < LANES // 2, -1.0, 1.0)
    n = lax.broadcasted_iota(jnp.int32, (sc, 1), 0).astype(F32) - float(sc // 2)
    m_out = jnp.exp(lre * n)
    m_in = jnp.exp(-(lre * n))
    ang = lim * n
    pin_re[...] = (m_in * jnp.cos(ang)).astype(pin_re.dtype)
    pin_im[...] = (sign * (-(m_in * jnp.sin(ang)))).astype(pin_im.dtype)
    pout_re[...] = m_out * jnp.cos(ang)
    pout_im[...] = sign * (m_out * jnp.sin(ang))
    m_al = jnp.exp(lre * float(sc))
    al_re[...] = m_al * jnp.cos(lim * float(sc))
    al_im[...] = sign * (m_al * jnp.sin(lim * float(sc)))


def _s5t_tables(a_re, a_im, log_dt):
    g, p = a_re.shape
    lanes = g * 2 * p
    spread = lambda a: jnp.broadcast_to(a[:, None, :], (g, 2, p)).reshape(1, lanes)
    tab16 = jax.ShapeDtypeStruct((S_CHUNK, lanes), BF16)
    tab32 = jax.ShapeDtypeStruct((S_CHUNK, lanes), F32)
    one = jax.ShapeDtypeStruct((1, lanes), F32)
    return pl.pallas_call(
        functools.partial(_s5t_tables_body, tc=T_CHUNK, sc=S_CHUNK),
        out_shape=[tab16, tab16, tab32, tab32, one, one], name="s5t_tables",
    )(spread(a_re), spread(a_im), spread(jnp.broadcast_to(log_dt[:, None], (g, p))))


def _atom_transpose(regs, atom, width):
    regs = list(regs)
    n = len(regs)
    d = n // 2
    while d:
        low = (atom & d) == 0
        for j in range(n):
            if j & d:
                continue
            a, b = regs[j], regs[j + d]
            regs[j] = jnp.where(low, a, pltpu.roll(b, d * width, axis=1))
            regs[j + d] = jnp.where(low, pltpu.roll(a, LANES - d * width, axis=1), b)
        d //= 2
    return regs


def _sublane_transpose(regs, srow):
    regs = list(regs)
    d = SUBLANES // 2
    while d:
        low = (srow & d) == 0
        for j in range(SUBLANES):
            if j & d:
                continue
            a, b = regs[j], regs[j + d]
            regs[j] = jnp.where(low, a, pltpu.roll(b, d, axis=0))
            regs[j + d] = jnp.where(low, pltpu.roll(a, SUBLANES - d, axis=0), b)
        d //= 2
    return regs


def _swap_halves(x):
    tiles = [pltpu.roll(x[:, t * LANES:(t + 1) * LANES], LANES // 2, axis=1)
             for t in range(x.shape[1] // LANES)]
    return jnp.concatenate(tiles, axis=1)


def _s5t_body(u_ref, w1_ref, co_ref, pin_re, pin_im, pout_re, pout_im, al_re, al_im,
              d_ref, wglu_ref, bglu_ref, y_ref,
              upk_ref, yloc_ref, sin_ref, xs_ref, carry_ref, udi_ref, *, rc, tc, sc, n_grp, cg):
    t = pl.program_id(1)
    kw = tc * cg
    n_q = u_ref.shape[0]
    atoms = LANES // cg

    @pl.when(t == 0)
    def _():
        carry_ref[...] = jnp.zeros(carry_ref.shape, F32)
        xs_ref[:, 0:SUBLANES, :] = jnp.zeros((n_grp, SUBLANES, LANES), F32)

    atom = lax.broadcasted_iota(jnp.int32, (rc, LANES), 1) // cg

    n_a = tc // atoms

    assert atoms == SUBLANES
    srow = lax.broadcasted_iota(jnp.int32, (SUBLANES, LANES), 0)

    def rows_to_planes(q, a):
        pieces = [[] for _ in range(atoms)]
        for nb in range(rc // SUBLANES):
            nat = [u_ref[q, (nb * SUBLANES + m) * tc + a * atoms:
                         (nb * SUBLANES + m) * tc + (a + 1) * atoms, :] for m in range(SUBLANES)]
            for k, piece in enumerate(_sublane_transpose(nat, srow)):
                pieces[k].append(piece)
        return [jnp.concatenate(p, axis=0) for p in pieces]

    def planes_to_rows(q, a, planes):
        for nb in range(rc // SUBLANES):
            regs = [p[nb * SUBLANES:(nb + 1) * SUBLANES, :] for p in planes]
            for m, nat in enumerate(_sublane_transpose(regs, srow)):
                r0 = (nb * SUBLANES + m) * tc + a * atoms
                y_ref[q, r0:r0 + atoms, :] = nat

    def pack():
        for q in range(n_q):
            for a in range(n_a):
                planes = rows_to_planes(q, a)
                for k in range(atoms):
                    udi_ref[q, a * atoms + k] = planes[k]
                for r, tile in enumerate(_atom_transpose(planes, atom, cg)):
                    upk_ref[q * atoms + r, :, a * LANES:(a + 1) * LANES] = tile.astype(BF16)
                yield

    per_stage = atoms // n_a

    def local():
        for _ in range(n_a):
            yield
        for g in range(n_grp):
            res = jnp.dot(upk_ref[g], w1_ref[g], preferred_element_type=F32)
            yloc_ref[g] = res[:, :kw]
            sin_ref[g] = res[:, kw:]
            if g % per_stage == per_stage - 1:
                yield

    _interleave([pack(), local()])

    ri = lax.broadcasted_iota(jnp.int32, (sc, sc), 0)
    ci = lax.broadcasted_iota(jnp.int32, (sc, sc), 1)
    tri = jnp.where(ci <= ri, 1.0, 0.0).astype(BF16)
    n_c2 = rc // sc
    ss = [jnp.concatenate([sin_ref[g, c2 * sc:(c2 + 1) * sc, :] for g in range(n_grp)], axis=1)
          for c2 in range(n_c2)]
    ss_sw = [_swap_halves(s) for s in ss]
    xin = [pin_re[...] * ss[c2].astype(BF16) + pin_im[...] * ss_sw[c2].astype(BF16)
           for c2 in range(n_c2)]
    psum = [jnp.dot(tri, x, preferred_element_type=F32) for x in xin]
    psum_sw = [_swap_halves(p) for p in psum]
    carry = carry_ref[...]
    carry_sw = _swap_halves(carry)
    carries = []
    for c2 in range(rc // sc):
        carries.append((carry, carry_sw))
        last = psum[c2][sc - 1:sc, :] + carry
        last_sw = psum_sw[c2][sc - 1:sc, :] + carry_sw
        carry = al_re[...] * last + al_im[...] * last_sw
        carry_sw = al_re[...] * last_sw - al_im[...] * last
    carry_ref[...] = carry
    for c2 in range(rc // sc):
        x = (pout_re[...] * (psum[c2] + carries[c2][0])
             + pout_im[...] * (psum_sw[c2] + carries[c2][1]))
        for g in range(n_grp):
            xs_ref[g, SUBLANES + c2 * sc:SUBLANES + (c2 + 1) * sc, :] = x[:, g * LANES:(g + 1) * LANES]

    for g in range(n_grp):
        xprev = xs_ref[g, SUBLANES - 1:SUBLANES - 1 + rc, :].astype(BF16)
        yloc_ref[g] = yloc_ref[g] + jnp.dot(xprev, co_ref[g], preferred_element_type=F32)
    xs_ref[:, SUBLANES - 1:SUBLANES, :] = xs_ref[:, SUBLANES - 1 + rc:SUBLANES + rc, :]

    wglu = wglu_ref[...].astype(BF16)

    def unpack(a):
        for _ in range(a * n_q):
            yield
        planes = [[None] * n_q for _ in range(atoms)]
        for q in range(n_q):
            tiles = [yloc_ref[q * atoms + r, :, a * LANES:(a + 1) * LANES] for r in range(atoms)]
            for k, tile in enumerate(_atom_transpose(tiles, atom, cg)):
                planes[k][q] = tile
            yield
        outs = []
        for k in range(atoms):
            i = a * atoms + k
            uu = jnp.concatenate([udi_ref[q, i] for q in range(n_q)], axis=1)
            y = jax.nn.gelu(jnp.concatenate(planes[k], axis=1) + d_ref[...] * uu)
            gate = jax.nn.sigmoid(jnp.dot(y.astype(BF16), wglu, preferred_element_type=F32)
                                  + bglu_ref[...])
            outs.append(y * gate)
            if k % 2 == 1:
                yield
        for q in range(n_q):
            planes_to_rows(q, a, [o[:, q * LANES:(q + 1) * LANES] for o in outs])
            yield

    _interleave([unpack(a) for a in range(n_a)])


def _s5t(u4, w1, co, tabs, d, wglu, bglu, bsz, seq, *, rc):
    n_q, nt, _ = u4.shape
    n_grp, kw, wcols = w1.shape
    n_state = (wcols - kw) // 2
    cg = kw // T_CHUNK
    rows = rc * T_CHUNK
    n_t = seq // rows
    blk = pl.BlockSpec((n_q, rows, LANES), lambda b, t: (0, b * n_t + t, 0))
    c2 = lambda b, t: (0, 0)
    c3 = lambda b, t: (0, 0, 0)
    whole = lambda a: pl.BlockSpec(a.shape, c3 if a.ndim == 3 else c2)
    body = functools.partial(_s5t_body, rc=rc, tc=T_CHUNK, sc=S_CHUNK, n_grp=n_grp, cg=cg)
    return pl.pallas_call(
        body,
        grid=(bsz, n_t),
        in_specs=[blk] + [whole(a) for a in (w1, co, *tabs, d, wglu, bglu)],
        out_specs=blk,
        out_shape=jax.ShapeDtypeStruct((n_q, nt, LANES), F32),
        scratch_shapes=[
            pltpu.VMEM((n_grp, rc, kw), BF16),
            pltpu.VMEM((n_grp, rc, kw), F32),
            pltpu.VMEM((n_grp, rc, 2 * n_state), F32),
            pltpu.VMEM((n_grp, rc + SUBLANES, 2 * n_state), F32),
            pltpu.VMEM((1, n_grp * 2 * n_state), F32),
            pltpu.VMEM((n_q, T_CHUNK, rc, LANES), F32),
        ],
        compiler_params=_params("arbitrary", "arbitrary"),
        name="s5t",
    )(u4, w1, co, *tabs, d, wglu, bglu)


def _mem_kv_body(m_ref, g_ref, wk_ref, wv_ref, wq_ref, wo_ref, qk_ref, vw_ref, *, xh):
    bsz, d, _ = qk_ref.shape
    n_mem = m_ref.shape[0] // bsz
    xd = d // xh
    scale = xd ** -0.5
    mn = _rms(m_ref[...], g_ref[...]).astype(BF16)
    k = jnp.dot(mn, wk_ref[...].astype(BF16), preferred_element_type=F32).astype(BF16)
    v = jnp.dot(mn, wv_ref[...].astype(BF16), preferred_element_type=F32).astype(BF16)
    wq = wq_ref[...].astype(BF16)
    wo = wo_ref[...].astype(BF16)
    nt_dims = (((1,), (1,)), ((), ()))
    for b in range(bsz):
        rows = slice(b * n_mem, (b + 1) * n_mem)
        for h in range(xh):
            hl = slice(h * xd, (h + 1) * xd)
            ml = slice(h * n_mem, (h + 1) * n_mem)
            qk = lax.dot_general(wq[:, hl], k[rows, hl], nt_dims, preferred_element_type=F32)
            qk_ref[b, :, ml] = (qk * scale).astype(BF16)
            vw_ref[b, ml, :] = jnp.dot(v[rows, hl], wo[hl, :],
                                       preferred_element_type=F32).astype(BF16)


def _mem_kv(mem, g, wk, wv, wq, wo):
    bsz, n_mem, d = mem.shape
    return pl.pallas_call(
        functools.partial(_mem_kv_body, xh=X_HEADS),
        out_shape=[
            jax.ShapeDtypeStruct((bsz, d, X_HEADS * n_mem), BF16),
            jax.ShapeDtypeStruct((bsz, X_HEADS * n_mem, d), BF16),
        ],
        compiler_params=pltpu.CompilerParams(vmem_limit_bytes=VMEM_LIMIT),
        name="mem_kv",
    )(mem.reshape(bsz * n_mem, d), g, wk, wv, wq, wo)


def _mix_attn_body(x_ref, o_ref, y_ref, wo_ref, gx_ref, qk_ref, vw_ref, h_ref, *, xh, n_sub):
    dn_w = o_ref.shape[1]
    n_mem = qk_ref.shape[2] // xh
    sub = x_ref.shape[0] // n_sub
    mm = functools.partial(jnp.dot, preferred_element_type=F32)
    wo = wo_ref[...].astype(BF16)

    def rows_gen(r0):
        rows = slice(r0, r0 + sub)
        y = jnp.concatenate([y_ref[q, rows, :] for q in range(y_ref.shape[0])], axis=1)
        mix = mm(o_ref[rows, :].astype(BF16), wo[0:dn_w, :]) + mm(y.astype(BF16), wo[dn_w:, :])
        yield
        h1 = x_ref[rows, :] + mix
        s = mm(_rms(h1, gx_ref[...]).astype(BF16), qk_ref[0])
        yield
        parts = []
        for h in range(xh):
            sh = s[:, h * n_mem:(h + 1) * n_mem]
            e = jnp.exp(sh - jnp.max(sh, axis=-1, keepdims=True))
            parts.append((e / jnp.sum(e, axis=-1, keepdims=True)).astype(BF16))
        att = mm(jnp.concatenate(parts, axis=1), vw_ref[0])
        yield
        h_ref[rows, :] = h1 + att

    _interleave([rows_gen(i * sub) for i in range(n_sub)])


def _mix_attn(x2, o, y, w_out, gx, qk, vw, bsz, seq, *, tm):
    nt, d = x2.shape
    dn_w = o.shape[1]
    n_t = seq // tm
    row = lambda b, t: (b * n_t + t, 0)
    c2 = lambda b, t: (0, 0)
    per_batch = lambda a: pl.BlockSpec((1,) + a.shape[1:], lambda b, t: (b, 0, 0))
    body = functools.partial(_mix_attn_body, xh=X_HEADS, n_sub=2)
    return pl.pallas_call(
        body,
        grid=(bsz, n_t),
        in_specs=[
            pl.BlockSpec((tm, d), row),
            pl.BlockSpec((tm, dn_w), row),
            pl.BlockSpec((y.shape[0], tm, LANES), lambda b, t: (0, b * n_t + t, 0)),
            pl.BlockSpec(w_out.shape, c2, pipeline_mode=pl.Buffered(1)),
            pl.BlockSpec((1, d), c2),
            per_batch(qk),
            per_batch(vw),
        ],
        out_specs=pl.BlockSpec((tm, d), row),
        out_shape=jax.ShapeDtypeStruct((nt, d), F32),
        compiler_params=_params("arbitrary", "arbitrary"),
        name="mix_attn",
    )(x2, o, y, w_out, gx, qk, vw)


def _ffn_body(h_ref, gf_ref, wg_ref, wu_ref, wd_ref, gl_ref, out_ref, *, final, fc):
    h = h_ref[...]
    hn = _rms(h, gf_ref[...]).astype(BF16)
    dff = wg_ref.shape[1]

    def down(gate, up, c0):
        act = (gate * jax.nn.sigmoid(gate) * up).astype(BF16)
        return jnp.dot(act, wd_ref[c0:c0 + fc, :].astype(BF16), preferred_element_type=F32)

    h3 = h
    prev = None
    for c0 in range(0, dff, fc):
        gate = jnp.dot(hn, wg_ref[:, c0:c0 + fc].astype(BF16), preferred_element_type=F32)
        up = jnp.dot(hn, wu_ref[:, c0:c0 + fc].astype(BF16), preferred_element_type=F32)
        if prev is not None:
            h3 = h3 + down(*prev)
        prev = (gate, up, c0)
    h3 = h3 + down(*prev)
    out_ref[...] = _rms(h3, gl_ref[...]) if final else h3


def _ffn(h, gf, wg, wu, wd, gl, *, tm, final):
    nt, d = h.shape
    dff = wg.shape[1]
    c2 = lambda i: (0, 0)
    row = lambda i: (i, 0)
    return pl.pallas_call(
        functools.partial(_ffn_body, final=final, fc=2 * LANES),
        grid=(nt // tm,),
        in_specs=[
            pl.BlockSpec((tm, d), row),
            pl.BlockSpec((1, d), c2),
            pl.BlockSpec((d, dff), c2, pipeline_mode=pl.Buffered(1)),
            pl.BlockSpec((d, dff), c2, pipeline_mode=pl.Buffered(1)),
            pl.BlockSpec((dff, d), c2, pipeline_mode=pl.Buffered(1)),
            pl.BlockSpec((1, d), c2),
        ],
        out_specs=pl.BlockSpec((tm, d), row),
        out_shape=jax.ShapeDtypeStruct((nt, d), F32),
        compiler_params=_params("arbitrary"),
        name="ffn",
    )(h, gf, wg, wu, wd, gl)


def _layer(h2, mem, bsz, seq, norm_mix_g, w_in, conv_w, dn_a_log, dn_dt_bias, dn_norm_g,
           s5_a_re, s5_a_im, s5_b_re, s5_b_im, s5_c_re, s5_c_im, s5_d, s5_log_dt,
           s5_w_glu, s5_b_glu, w_out, norm_x_g, norm_mem_g, w_xq, w_xk, w_xv, w_xo):
    d = h2.shape[1]
    nh = dn_a_log.shape[0]
    dh = dn_norm_g.shape[0]
    dn_w = nh * dh
    s5_w = s5_a_re.shape[0] * s5_b_re.shape[-1]

    off_a = 4 * dn_w
    off_u = off_a + 2 * nh
    w_ab = jnp.pad(w_in[:, off_a:off_u].astype(BF16), ((0, 0), (0, LANES - 2 * nh)))
    w_cat = jnp.concatenate([w_in[:, :off_a].astype(BF16), w_in[:, off_u:].astype(BF16), w_ab],
                            axis=1)
    alog = jnp.pad(dn_a_log, (0, LANES - nh)).reshape(1, LANES)
    dtb = jnp.pad(dn_dt_bias, (0, LANES - nh)).reshape(1, LANES)

    u, o = _mixer_in(h2, seq, norm_mix_g.reshape(1, d), w_cat, conv_w, alog, dtb,
                     dn_norm_g.reshape(1, dh), nh=nh, dh=dh, tb=MIXER_ROWS)

    w1, co = _s5t_prep(s5_a_re, s5_a_im, s5_log_dt, s5_b_re, s5_b_im, s5_c_re, s5_c_im)
    tabs = _s5t_tables(s5_a_re, s5_a_im, s5_log_dt)
    y = _s5t(u, w1, co, tabs, s5_d.reshape(1, s5_w), s5_w_glu, s5_b_glu.reshape(1, s5_w),
             bsz, seq, rc=S5_BLOCKS)

    qk_mem, vw_mem = _mem_kv(mem, norm_mem_g.reshape(1, d), w_xk, w_xv, w_xq, w_xo)
    return _mix_attn(h2, o, y, w_out, norm_x_g.reshape(1, d), qk_mem, vw_mem, bsz, seq,
                     tm=ATTN_ROWS)


def kernel(x, mem, norm_mix_g, w_in, conv_w, dn_a_log, dn_dt_bias, dn_norm_g, s5_a_re, s5_a_im,
           s5_b_re, s5_b_im, s5_c_re, s5_c_im, s5_d, s5_log_dt, s5_w_glu, s5_b_glu, w_out,
           norm_x_g, norm_mem_g, w_xq, w_xk, w_xv, w_xo, norm_ffn_g, w_gate, w_up, w_down,
           norm_final_g):
    bsz, seq, d = x.shape
    depth = w_in.shape[0]
    h = x.reshape(bsz * seq, d)
    for l in range(depth):
        h = _layer(h, mem, bsz, seq, norm_mix_g[l], w_in[l], conv_w[l], dn_a_log[l],
                   dn_dt_bias[l], dn_norm_g[l], s5_a_re[l], s5_a_im[l], s5_b_re[l], s5_b_im[l],
                   s5_c_re[l], s5_c_im[l], s5_d[l], s5_log_dt[l], s5_w_glu[l], s5_b_glu[l],
                   w_out[l], norm_x_g[l], norm_mem_g[l], w_xq[l], w_xk[l], w_xv[l], w_xo[l])
        h = _ffn(h, norm_ffn_g[l].reshape(1, d), w_gate[l], w_up[l], w_down[l],
                 norm_final_g.reshape(1, d), tm=FFN_ROWS, final=l == depth - 1)
    return h.reshape(bsz, seq, d)
```

```python
import functools

import jax
import jax.numpy as jnp
from jax import lax
from jax.experimental import pallas as pl
from jax.experimental.pallas import tpu as pltpu

F32 = jnp.float32
BF16 = jnp.bfloat16
EPS = 1e-6
HIGHEST = lax.Precision.HIGHEST

LANES = 128
SUBLANES = 8
DN_CHUNK = 64
T_CHUNK = 16
S_CHUNK = 16
X_HEADS = 4
VMEM_LIMIT = 58 * 1024 * 1024

MIXER_ROWS = 256
S5_BLOCKS = 128
ATTN_ROWS = 1024
FFN_ROWS = 1024


def _fdot(a, b):
    return jnp.dot(a, b, precision=HIGHEST, preferred_element_type=F32)


def _rms(x, g):
    return x * lax.rsqrt(jnp.mean(x * x, axis=-1, keepdims=True) + EPS) * g


def _interleave(gens, stages_per_round=None):
    pending = [(gen, 1 if stages_per_round is None else stages_per_round[i])
               for i, gen in enumerate(gens)]
    while pending:
        for item in list(pending):
            gen, n = item
            try:
                for _ in range(n):
                    next(gen)
            except StopIteration:
                pending.remove(item)


def _params(*sem):
    return pltpu.CompilerParams(dimension_semantics=sem, vmem_limit_bytes=VMEM_LIMIT)


def _in_proj_stages(x_ref, g_ref, w_ref, cw_ref, alog_ref, dtb_ref,
                    q_ref, k_ref, kt_ref, v_ref, z_ref, u_ref, gt_ref,
                    buf_ref, *, tm, dn_w, nh, dh, cw_n):
    xnb = _rms(x_ref[...], g_ref[...]).astype(BF16)
    gw = 2 * dh

    def conv_silu(p, cols):
        outs = []
        for i in range(gw // LANES):
            c0 = cols.start + i * LANES
            s = c0 // LANES
            pc = p[:, i * LANES:(i + 1) * LANES]
            buf_ref[s, SUBLANES:SUBLANES + tm, :] = pc
            acc = cw_ref[cw_n - 1:cw_n, c0:c0 + LANES] * pc
            for j in range(cw_n - 1):
                off = SUBLANES - (cw_n - 1) + j
                acc = acc + cw_ref[j:j + 1, c0:c0 + LANES] * buf_ref[s, off:off + tm, :]
            buf_ref[s, 0:SUBLANES, :] = buf_ref[s, tm:tm + SUBLANES, :]
            outs.append(acc * jax.nn.sigmoid(acc))
        return jnp.concatenate(outs, axis=1)

    def l2n(a, scale):
        parts = []
        for i in range(gw // dh):
            ah = a[:, i * dh:(i + 1) * dh]
            parts.append(ah * (lax.rsqrt(jnp.sum(ah * ah, -1, keepdims=True) + EPS) * scale))
        return jnp.concatenate(parts, axis=1)

    def ep_q(p, c0):
        q_ref[:, c0:c0 + gw] = l2n(conv_silu(p, slice(c0, c0 + gw)), dh ** -0.5)

    def ep_k(p, c0):
        kn = l2n(conv_silu(p, slice(dn_w + c0, dn_w + c0 + gw)), 1.0)
        k_ref[:, c0:c0 + gw] = kn
        kt_ref[c0:c0 + gw, :] = kn.T

    def ep_v(p, c0):
        v_ref[:, c0:c0 + gw] = conv_silu(p, slice(2 * dn_w + c0, 2 * dn_w + c0 + gw))

    def ep_z(p, c0):
        z_ref[:, c0:c0 + gw] = p

    def ep_u(p, c0):
        for j in range(gw // LANES):
            u_ref[c0 // LANES + j] = p[:, j * LANES:(j + 1) * LANES]

    def ep_gates(ab, c0):
        lane = lax.broadcasted_iota(jnp.int32, ab.shape, 1)
        g = -jnp.exp(alog_ref[...]) * jnp.logaddexp(ab + dtb_ref[...], 0.0)
        gates = jnp.where(lane < nh, g, jax.nn.sigmoid(ab))
        gt_ref[...] = gates.T[0:SUBLANES, :]

    groups = []
    for i, ep in enumerate((ep_q, ep_k, ep_v, ep_z, ep_u)):
        groups += [(ep, i * dn_w + c0, gw, c0) for c0 in range(0, dn_w, gw)]
    groups.append((ep_gates, 5 * dn_w, LANES, 0))

    prev = None
    for ep, w0, width, c0 in groups:
        p = jnp.dot(xnb, w_ref[:, w0:w0 + width], preferred_element_type=F32)
        if prev is not None:
            prev[0](prev[1], prev[2])
        prev = (ep, p, c0)
        yield
    prev[0](prev[1], prev[2])


def _delta_stages(q_ref, k_ref, kt_ref, v_ref, z_ref, gt_ref, ng_ref,
                  o_ref, s_ref, u_sc, w_sc, qg_sc, attn_sc, kdt_sc, gtot_sc, wr, rd,
                  *, tb, nh, dh, ch):
    nc = tb // ch
    heads = range(nh)
    hls = [slice(h * dh, (h + 1) * dh) for h in heads]
    mm = functools.partial(jnp.dot, preferred_element_type=F32)

    def phase_b():
        gtot_b = gtot_sc[rd]
        s_all = [s_ref[h] for h in heads]
        for cidx in range(nc):
            rows = slice(cidx * ch, (cidx + 1) * ch)
            sb = [s_all[h].astype(BF16) for h in heads]
            ws = [mm(w_sc[rd, h, rows, :], sb[h]) for h in heads]
            qs = [mm(qg_sc[rd, h, rows, :], sb[h]) for h in heads]
            yield
            vb = [(u_sc[rd, h, rows, :] - ws[h]).astype(BF16) for h in heads]
            av = [mm(attn_sc[rd, h, rows, rows], vb[h]) for h in heads]
            kv = [mm(kdt_sc[rd, h, :, rows], vb[h]) for h in heads]
            yield
            outs = []
            for h in heads:
                a_last = jnp.exp(gtot_b[h:h + 1, cidx * ch:cidx * ch + 1])
                s_all[h] = s_all[h] * a_last + kv[h]
                o = qs[h] + av[h]
                o = o * lax.rsqrt(jnp.mean(o * o, -1, keepdims=True) + EPS)
                zh = z_ref[rows, hls[h]]
                outs.append(o * ng_ref[...] * (zh * jax.nn.sigmoid(zh)))
            o_ref[rows, :] = jnp.concatenate(outs, axis=1)
        for h in heads:
            s_ref[h] = s_all[h]

    r = lax.broadcasted_iota(jnp.int32, (tb, tb), 0)
    c = lax.broadcasted_iota(jnp.int32, (tb, tb), 1)
    same = (r // ch) == (c // ch)
    causal_bd = same & (c <= r)
    strict_bd = same & (c < r)
    gt = gt_ref[...]
    gcum_t = _fdot(gt, jnp.where(same & (r <= c), 1.0, 0.0))
    gtot_t = _fdot(gt, jnp.where(same, 1.0, 0.0))
    gcum = gcum_t.T
    gates = gt.T

    def pack(x):
        acc = x[0:ch, :]
        for i in range(1, nc):
            acc = acc + x[i * ch:(i + 1) * ch, :]
        return acc

    def unpack(x):
        return jnp.where(same, jnp.concatenate([x] * nc, axis=0), 0.0).astype(BF16)

    n_sq = max(1, (ch - 1).bit_length() - 1)

    def phase_a():
        gcol = [gcum[:, h:h + 1] for h in heads]
        grow = [gcum_t[h:h + 1, :] for h in heads]
        beta = [gates[:, nh + h:nh + h + 1] for h in heads]
        kb = [k_ref[:, hls[h]] * beta[h] for h in heads]
        gram = [mm(jnp.concatenate([kb[h], q_ref[:, hls[h]]], axis=0).astype(BF16),
                   kt_ref[hls[h], :].astype(BF16)) for h in heads]
        yield
        ppk, apk = [], []
        for h in heads:
            decay = jnp.exp(jnp.where(causal_bd, gcol[h] - grow[h], -jnp.inf))
            p_bd = -jnp.where(strict_bd, gram[h][:tb] * decay, 0.0)
            attn_sc[wr, h] = (gram[h][tb:] * decay).astype(BF16)
            pk = pack(p_bd)
            apk.append(pk)
            ppk.append(mm(pk.astype(BF16), p_bd.astype(BF16)))
        yield
        for j in range(1, n_sq + 1):
            res = []
            for h in heads:
                lhs = jnp.concatenate([apk[h], ppk[h]], axis=0) if j < n_sq else apk[h]
                res.append(mm(lhs.astype(BF16), unpack(ppk[h])))
            yield
            for h in heads:
                apk[h] = apk[h] + ppk[h] + res[h][:ch]
                if j < n_sq:
                    ppk[h] = res[h][ch:]
        egc = [jnp.exp(gcol[h]) for h in heads]
        rhs = [jnp.concatenate([v_ref[:, hls[h]] * beta[h], kb[h] * egc[h]], axis=1) for h in heads]
        prod = [mm(unpack(apk[h]), rhs[h].astype(BF16)) for h in heads]
        yield
        for h in heads:
            uw = rhs[h] + prod[h]
            u_sc[wr, h] = uw[:, :dh]
            w_sc[wr, h] = uw[:, dh:].astype(BF16)
            qg_sc[wr, h] = (q_ref[:, hls[h]] * egc[h]).astype(BF16)
            kdt_sc[wr, h] = (kt_ref[hls[h], :] * jnp.exp(gtot_t[h:h + 1, :] - grow[h])).astype(BF16)
        gtot_sc[wr] = gtot_t

    return phase_b(), phase_a()


def _mixer_in_body(x_ref, g_ref, w_ref, cw_ref, alog_ref, dtb_ref, ng_ref,
                   u_ref, o_ref,
                   buf_ref, q_sc, k_sc, kt_sc, v_sc, z_sc, gt_sc,
                   s_ref, u_sc, w_sc, qg_sc, attn_sc, kdt_sc, gtot_sc,
                   *, tb, n_t, nh, dh, cw_n):
    s = pl.program_id(0)
    dn_w = nh * dh

    @pl.when(s == 0)
    def _():
        for ref in (buf_ref, q_sc, k_sc, kt_sc, v_sc, z_sc, gt_sc,
                    s_ref, u_sc, w_sc, qg_sc, attn_sc, kdt_sc, gtot_sc):
            ref[...] = jnp.zeros(ref.shape, ref.dtype)

    @pl.when(s % n_t == 0)
    def _():
        buf_ref[:, 0:SUBLANES, :] = jnp.zeros((3 * dn_w // LANES, SUBLANES, LANES), F32)

    @pl.when((s - 2) % n_t == 0)
    def _():
        s_ref[...] = jnp.zeros(s_ref.shape, F32)

    cur = s % 2
    prv = 1 - cur
    zw = s % 3
    zr = (s + 1) % 3
    proj = _in_proj_stages(
        x_ref, g_ref, w_ref, cw_ref, alog_ref, dtb_ref,
        q_sc.at[cur], k_sc.at[cur], kt_sc.at[cur], v_sc.at[cur], z_sc.at[zw], u_ref, gt_sc.at[cur],
        buf_ref, tm=tb, dn_w=dn_w, nh=nh, dh=dh, cw_n=cw_n)
    phase_b, phase_a = _delta_stages(
        q_sc.at[prv], k_sc.at[prv], kt_sc.at[prv], v_sc.at[prv], z_sc.at[zr], gt_sc.at[prv], ng_ref,
        o_ref, s_ref, u_sc, w_sc, qg_sc, attn_sc, kdt_sc, gtot_sc, cur, prv,
        tb=tb, nh=nh, dh=dh, ch=DN_CHUNK)
    _interleave([proj, phase_b, phase_a])


def _mixer_in(x2, seq, g, w_cat, conv_w, alog, dtb, ng, *, nh, dh, tb):
    nt, d = x2.shape
    dn_w = nh * dh
    n_blk = nt // tb
    cw_n = conv_w.shape[0]
    const = lambda s: (0, 0)
    in_blk = lambda s: (jnp.minimum(s, n_blk - 1), 0)
    out_blk = lambda s: (jnp.maximum(s - 2, 0), 0)
    body = functools.partial(_mixer_in_body, tb=tb, n_t=seq // tb, nh=nh, dh=dh, cw_n=cw_n)
    two = lambda *shape: pltpu.VMEM((2,) + shape, F32)
    return pl.pallas_call(
        body,
        grid=(n_blk + 2,),
        in_specs=[pl.BlockSpec((tb, d), in_blk)]
        + [pl.BlockSpec(a.shape, const) for a in (g, w_cat, conv_w, alog, dtb, ng)],
        out_specs=[pl.BlockSpec((dn_w // LANES, tb, LANES),
                                lambda s: (0, jnp.minimum(s, n_blk - 1), 0)),
                   pl.BlockSpec((tb, dn_w), out_blk)],
        out_shape=[jax.ShapeDtypeStruct((dn_w // LANES, nt, LANES), F32),
                   jax.ShapeDtypeStruct((nt, dn_w), F32)],
        scratch_shapes=[
            pltpu.VMEM((3 * dn_w // LANES, tb + SUBLANES, LANES), F32),
            two(tb, dn_w), two(tb, dn_w), two(dn_w, tb), two(tb, dn_w),
            pltpu.VMEM((3, tb, dn_w), F32),
            two(SUBLANES, tb),
            pltpu.VMEM((nh, dh, dh), F32),
            pltpu.VMEM((2, nh, tb, dh), F32),
            pltpu.VMEM((2, nh, tb, dh), BF16),
            pltpu.VMEM((2, nh, tb, dh), BF16),
            pltpu.VMEM((2, nh, tb, tb), BF16),
            pltpu.VMEM((2, nh, dh, tb), BF16),
            pltpu.VMEM((2, SUBLANES, tb), F32),
        ],
        compiler_params=_params("arbitrary"),
        name="mixer_in",
    )(x2, g, w_cat, conv_w, alog, dtb, ng)


def _ftdot(a, b):
    return lax.dot_general(a.astype(BF16), b.astype(BF16), (((0,), (0,)), ((), ())),
                           preferred_element_type=F32)


def _cmul(xr, xi, yr, yi):
    return xr * yr - xi * yi, xr * yi + xi * yr


def _s5t_prep_body(are_ref, aim_ref, ldt_ref, btr_ref, bti_ref, ctr_ref, cti_ref, w1_ref, co_ref,
                   *, tc, cg, n_p):
    n_grp = w1_ref.shape[0]
    kw = tc * cg
    a_re = are_ref[...]
    a_im = aim_ref[...]
    dt = jnp.exp(ldt_ref[...])
    lre = a_re * dt
    lim = a_im * dt
    mag = jnp.exp(lre)
    ab = (mag * jnp.cos(lim), mag * jnp.sin(lim))
    nr, ni = ab[0] - 1.0, ab[1]
    den = a_re * a_re + a_im * a_im
    co = ((nr * a_re + ni * a_im) / den, (ni * a_re - nr * a_im) / den)
    inv_mag2 = jnp.exp(-2.0 * lre)
    ai = (ab[0] * inv_mag2, -ab[1] * inv_mag2)
    n_bits = tc.bit_length() - 1
    pos, neg = [ab], [ai]
    for _ in range(n_bits - 1):
        pos.append(_cmul(*pos[-1], *pos[-1]))
        neg.append(_cmul(*neg[-1], *neg[-1]))
    top = pos[0]
    for f in pos[1:]:
        top = _cmul(*top, *f)
    rows = [co, top] + pos + neg
    flat = [part for pair in rows for part in pair]
    cols = []
    for i in range(0, len(flat), SUBLANES):
        blk = flat[i:i + SUBLANES]
        blk = blk + [blk[-1]] * (SUBLANES - len(blk))
        t = jnp.concatenate(blk, axis=0).T
        cols += [t[:, j:j + 1] for j in range(SUBLANES)]
    pair = lambda k: (cols[2 * k], cols[2 * k + 1])
    co_c, top_c = pair(0), pair(1)
    pos_c = [pair(2 + k) for k in range(n_bits)]
    neg_c = [pair(2 + n_bits + k) for k in range(n_bits)]

    slot = lax.broadcasted_iota(jnp.int32, (1, kw), 1) // cg

    def power(factors):
        acc = None
        for k, (fr, fi) in enumerate(factors):
            bit = ((slot >> k) & 1) == 1
            term = (jnp.where(bit, fr, 1.0), jnp.where(bit, fi, 0.0))
            acc = term if acc is None else _cmul(*acc, *term)
        return acc

    place = jnp.where(lax.broadcasted_iota(jnp.int32, (cg, kw), 0)
                      == lax.broadcasted_iota(jnp.int32, (cg, kw), 1) % cg, 1.0, 0.0)
    place = place.astype(BF16)

    def rep(ref):
        x = ref[...]
        hi = x.astype(BF16)
        r1 = x - hi.astype(F32)
        mid = r1.astype(BF16)
        lo = (r1 - mid.astype(F32)).astype(BF16)
        return sum(jnp.dot(part, place, preferred_element_type=F32) for part in (hi, mid, lo))
    bt = _cmul(*co_c, rep(btr_ref), rep(bti_ref))
    pb = _cmul(*bt, *power(neg_c))
    qc = _cmul(rep(ctr_ref), rep(cti_ref), *power(pos_c))
    q1 = _cmul(*qc, *pos_c[0])
    bi = _cmul(*pb, *top_c)
    r2 = lax.broadcasted_iota(jnp.int32, (kw, kw), 0) // cg
    c2 = lax.broadcasted_iota(jnp.int32, (kw, kw), 1) // cg
    eye = jnp.where(lax.broadcasted_iota(jnp.int32, (n_p, n_p), 0)
                    == lax.broadcasted_iota(jnp.int32, (n_p, n_p), 1), 1.0, 0.0)
    for g in range(n_grp):
        rs = slice(g * n_p, (g + 1) * n_p)
        m_full = _ftdot(pb[0][rs], qc[0][rs]) - _ftdot(pb[1][rs], qc[1][rs])
        w1_ref[g, :, 0:kw] = jnp.where(r2 <= c2, m_full, 0.0).astype(w1_ref.dtype)
        w1_ref[g, :, kw:kw + n_p] = _ftdot(bi[0][rs], eye).astype(w1_ref.dtype)
        w1_ref[g, :, kw + n_p:kw + 2 * n_p] = _ftdot(bi[1][rs], eye).astype(w1_ref.dtype)
        co_ref[g, 0:n_p, :] = q1[0][rs].astype(co_ref.dtype)
        co_ref[g, n_p:2 * n_p, :] = (-q1[1][rs]).astype(co_ref.dtype)


def _s5t_prep(a_re, a_im, log_dt, b_re, b_im, c_re, c_im):
    g, p = a_re.shape
    cg = b_re.shape[-1]
    kw = T_CHUNK * cg
    row = lambda a: a.reshape(1, g * p)
    slots = lambda a: a.reshape(g * p, cg)
    return pl.pallas_call(
        functools.partial(_s5t_prep_body, tc=T_CHUNK, cg=cg, n_p=p),
        out_shape=[jax.ShapeDtypeStruct((g, kw, kw + 2 * p), BF16),
                   jax.ShapeDtypeStruct((g, 2 * p, kw), BF16)],
        compiler_params=pltpu.CompilerParams(vmem_limit_bytes=VMEM_LIMIT),
        name="s5t_prep",
    )(row(a_re), row(a_im), row(jnp.broadcast_to(log_dt[:, None], (g, p))),
      slots(b_re), slots(b_im), slots(c_re.transpose(0, 2, 1)), slots(c_im.transpose(0, 2, 1)))


def _s5t_tables_body(are_ref, aim_ref, ldt_ref, pin_re, pin_im, pout_re, pout_im, al_re, al_im,
                     *, tc, sc):
    lre = are_ref[...] * jnp.exp(ldt_ref[...]) * float(tc)
    lim = aim_ref[...] * jnp.exp(ldt_ref[...]) * float(tc)
    lane = lax.broadcasted_iota(jnp.int32, lre.shape, 1)
    sign = jnp.where((lane % LANES) < LANES // 2, -1.0, 1.0)
    n = lax.broadcasted_iota(jnp.int32, (sc, 1), 0).astype(F32) - float(sc // 2)
    m_out = jnp.exp(lre * n)
    m_in = jnp.exp(-(lre * n))
    ang = lim * n
    pin_re[...] = (m_in * jnp.cos(ang)).astype(pin_re.dtype)
    pin_im[...] = (sign * (-(m_in * jnp.sin(ang)))).astype(pin_im.dtype)
    pout_re[...] = m_out * jnp.cos(ang)
    pout_im[...] = sign * (m_out * jnp.sin(ang))
    m_al = jnp.exp(lre * float(sc))
    al_re[...] = m_al * jnp.cos(lim * float(sc))
    al_im[...] = sign * (m_al * jnp.sin(lim * float(sc)))


def _s5t_tables(a_re, a_im, log_dt):
    g, p = a_re.shape
    lanes = g * 2 * p
    spread = lambda a: jnp.broadcast_to(a[:, None, :], (g, 2, p)).reshape(1, lanes)
    tab16 = jax.ShapeDtypeStruct((S_CHUNK, lanes), BF16)
    tab32 = jax.ShapeDtypeStruct((S_CHUNK, lanes), F32)
    one = jax.ShapeDtypeStruct((1, lanes), F32)
    return pl.pallas_call(
        functools.partial(_s5t_tables_body, tc=T_CHUNK, sc=S_CHUNK),
        out_shape=[tab16, tab16, tab32, tab32, one, one], name="s5t_tables",
    )(spread(a_re), spread(a_im), spread(jnp.broadcast_to(log_dt[:, None], (g, p))))


def _atom_transpose(regs, atom, width):
    regs = list(regs)
    n = len(regs)
    d = n // 2
    while d:
        low = (atom & d) == 0
        for j in range(n):
            if j & d:
                continue
            a, b = regs[j], regs[j + d]
            regs[j] = jnp.where(low, a, pltpu.roll(b, d * width, axis=1))
            regs[j + d] = jnp.where(low, pltpu.roll(a, LANES - d * width, axis=1), b)
        d //= 2
    return regs


def _sublane_transpose(regs, srow):
    regs = list(regs)
    d = SUBLANES // 2
    while d:
        low = (srow & d) == 0
        for j in range(SUBLANES):
            if j & d:
                continue
            a, b = regs[j], regs[j + d]
            regs[j] = jnp.where(low, a, pltpu.roll(b, d, axis=0))
            regs[j + d] = jnp.where(low, pltpu.roll(a, SUBLANES - d, axis=0), b)
        d //= 2
    return regs


def _swap_halves(x):
    tiles = [pltpu.roll(x[:, t * LANES:(t + 1) * LANES], LANES // 2, axis=1)
             for t in range(x.shape[1] // LANES)]
    return jnp.concatenate(tiles, axis=1)


def _s5t_body(u_ref, w1_ref, co_ref, pin_re, pin_im, pout_re, pout_im, al_re, al_im,
              d_ref, wglu_ref, bglu_ref, y_ref,
              upk_ref, yloc_ref, sin_ref, xs_ref, carry_ref, udi_ref, *, rc, tc, sc, n_grp, cg):
    t = pl.program_id(1)
    kw = tc * cg
    n_q = u_ref.shape[0]
    atoms = LANES // cg

    @pl.when(t == 0)
    def _():
        carry_ref[...] = jnp.zeros(carry_ref.shape, F32)
        xs_ref[:, 0:SUBLANES, :] = jnp.zeros((n_grp, SUBLANES, LANES), F32)

    atom = lax.broadcasted_iota(jnp.int32, (rc, LANES), 1) // cg

    n_a = tc // atoms

    assert atoms == SUBLANES
    srow = lax.broadcasted_iota(jnp.int32, (SUBLANES, LANES), 0)

    def rows_to_planes(q, a):
        pieces = [[] for _ in range(atoms)]
        for nb in range(rc // SUBLANES):
            nat = [u_ref[q, (nb * SUBLANES + m) * tc + a * atoms:
                         (nb * SUBLANES + m) * tc + (a + 1) * atoms, :] for m in range(SUBLANES)]
            for k, piece in enumerate(_sublane_transpose(nat, srow)):
                pieces[k].append(piece)
        return [jnp.concatenate(p, axis=0) for p in pieces]

    def planes_to_rows(q, a, planes):
        for nb in range(rc // SUBLANES):
            regs = [p[nb * SUBLANES:(nb + 1) * SUBLANES, :] for p in planes]
            for m, nat in enumerate(_sublane_transpose(regs, srow)):
                r0 = (nb * SUBLANES + m) * tc + a * atoms
                y_ref[q, r0:r0 + atoms, :] = nat

    def pack():
        for q in range(n_q):
            for a in range(n_a):
                planes = rows_to_planes(q, a)
                for k in range(atoms):
                    udi_ref[q, a * atoms + k] = planes[k]
                for r, tile in enumerate(_atom_transpose(planes, atom, cg)):
                    upk_ref[q * atoms + r, :, a * LANES:(a + 1) * LANES] = tile.astype(BF16)
                yield

    per_stage = atoms // n_a

    def local():
        for _ in range(n_a):
            yield
        for g in range(n_grp):
            res = jnp.dot(upk_ref[g], w1_ref[g], preferred_element_type=F32)
            yloc_ref[g] = res[:, :kw]
            sin_ref[g] = res[:, kw:]
            if g % per_stage == per_stage - 1:
                yield

    _interleave([pack(), local()])

    ri = lax.broadcasted_iota(jnp.int32, (sc, sc), 0)
    ci = lax.broadcasted_iota(jnp.int32, (sc, sc), 1)
    tri = jnp.where(ci <= ri, 1.0, 0.0).astype(BF16)
    n_c2 = rc // sc
    ss = [jnp.concatenate([sin_ref[g, c2 * sc:(c2 + 1) * sc, :] for g in range(n_grp)], axis=1)
          for c2 in range(n_c2)]
    ss_sw = [_swap_halves(s) for s in ss]
    xin = [pin_re[...] * ss[c2].astype(BF16) + pin_im[...] * ss_sw[c2].astype(BF16)
           for c2 in range(n_c2)]
    psum = [jnp.dot(tri, x, preferred_element_type=F32) for x in xin]
    psum_sw = [_swap_halves(p) for p in psum]
    carry = carry_ref[...]
    carry_sw = _swap_halves(carry)
    carries = []
    for c2 in range(rc // sc):
        carries.append((carry, carry_sw))
        last = psum[c2][sc - 1:sc, :] + carry
        last_sw = psum_sw[c2][sc - 1:sc, :] + carry_sw
        carry = al_re[...] * last + al_im[...] * last_sw
        carry_sw = al_re[...] * last_sw - al_im[...] * last
    carry_ref[...] = carry
    for c2 in range(rc // sc):
        x = (pout_re[...] * (psum[c2] + carries[c2][0])
             + pout_im[...] * (psum_sw[c2] + carries[c2][1]))
        for g in range(n_grp):
            xs_ref[g, SUBLANES + c2 * sc:SUBLANES + (c2 + 1) * sc, :] = x[:, g * LANES:(g + 1) * LANES]

    for g in range(n_grp):
        xprev = xs_ref[g, SUBLANES - 1:SUBLANES - 1 + rc, :].astype(BF16)
        yloc_ref[g] = yloc_ref[g] + jnp.dot(xprev, co_ref[g], preferred_element_type=F32)
    xs_ref[:, SUBLANES - 1:SUBLANES, :] = xs_ref[:, SUBLANES - 1 + rc:SUBLANES + rc, :]

    wglu = wglu_ref[...].astype(BF16)

    def unpack(a):
        for _ in range(a * n_q):
            yield
        planes = [[None] * n_q for _ in range(atoms)]
        for q in range(n_q):
            tiles = [yloc_ref[q * atoms + r, :, a * LANES:(a + 1) * LANES] for r in range(atoms)]
            for k, tile in enumerate(_atom_transpose(tiles, atom, cg)):
                planes[k][q] = tile
            yield
        outs = []
        for k in range(atoms):
            i = a * atoms + k
            uu = jnp.concatenate([udi_ref[q, i] for q in range(n_q)], axis=1)
            y = jax.nn.gelu(jnp.concatenate(planes[k], axis=1) + d_ref[...] * uu)
            gate = jax.nn.sigmoid(jnp.dot(y.astype(BF16), wglu, preferred_element_type=F32)
                                  + bglu_ref[...])
            outs.append(y * gate)
            if k % 2 == 1:
                yield
        for q in range(n_q):
            planes_to_rows(q, a, [o[:, q * LANES:(q + 1) * LANES] for o in outs])
            yield

    _interleave([unpack(a) for a in range(n_a)])


def _s5t(u4, w1, co, tabs, d, wglu, bglu, bsz, seq, *, rc):
    n_q, nt, _ = u4.shape
    n_grp, kw, wcols = w1.shape
    n_state = (wcols - kw) // 2
    cg = kw // T_CHUNK
    rows = rc * T_CHUNK
    n_t = seq // rows
    blk = pl.BlockSpec((n_q, rows, LANES), lambda b, t: (0, b * n_t + t, 0))
    c2 = lambda b, t: (0, 0)
    c3 = lambda b, t: (0, 0, 0)
    whole = lambda a: pl.BlockSpec(a.shape, c3 if a.ndim == 3 else c2)
    body = functools.partial(_s5t_body, rc=rc, tc=T_CHUNK, sc=S_CHUNK, n_grp=n_grp, cg=cg)
    return pl.pallas_call(
        body,
        grid=(bsz, n_t),
        in_specs=[blk] + [whole(a) for a in (w1, co, *tabs, d, wglu, bglu)],
        out_specs=blk,
        out_shape=jax.ShapeDtypeStruct((n_q, nt, LANES), F32),
        scratch_shapes=[
            pltpu.VMEM((n_grp, rc, kw), BF16),
            pltpu.VMEM((n_grp, rc, kw), F32),
            pltpu.VMEM((n_grp, rc, 2 * n_state), F32),
            pltpu.VMEM((n_grp, rc + SUBLANES, 2 * n_state), F32),
            pltpu.VMEM((1, n_grp * 2 * n_state), F32),
            pltpu.VMEM((n_q, T_CHUNK, rc, LANES), F32),
        ],
        compiler_params=_params("arbitrary", "arbitrary"),
        name="s5t",
    )(u4, w1, co, *tabs, d, wglu, bglu)


def _mem_kv_body(m_ref, g_ref, wk_ref, wv_ref, wq_ref, wo_ref, qk_ref, vw_ref, w_sc, *, xh):
    @pl.when(pl.program_id(0) == 0)
    def _():
        for i, ref in enumerate((wk_ref, wv_ref, wq_ref, wo_ref)):
            w_sc[i] = ref[...].astype(BF16)

    _, n_mem, d = m_ref.shape
    xd = d // xh
    scale = xd ** -0.5
    mn = _rms(m_ref[0], g_ref[...]).astype(BF16)
    k = jnp.dot(mn, w_sc[0], preferred_element_type=F32).astype(BF16)
    v = jnp.dot(mn, w_sc[1], preferred_element_type=F32).astype(BF16)
    nt_dims = (((1,), (1,)), ((), ()))
    for h in range(xh):
        hl = slice(h * xd, (h + 1) * xd)
        ml = slice(h * n_mem, (h + 1) * n_mem)
        qk = lax.dot_general(w_sc[2, :, hl], k[:, hl], nt_dims, preferred_element_type=F32)
        qk_ref[0, :, ml] = (qk * scale).astype(BF16)
        vw_ref[0, ml, :] = jnp.dot(v[:, hl], w_sc[3, hl, :],
                                   preferred_element_type=F32).astype(BF16)


def _mem_kv(mem, g, wk, wv, wq, wo):
    bsz, n_mem, d = mem.shape
    hm = X_HEADS * n_mem
    const = lambda b: (0, 0)
    weight = pl.BlockSpec((d, d), const, pipeline_mode=pl.Buffered(1))
    return pl.pallas_call(
        functools.partial(_mem_kv_body, xh=X_HEADS),
        grid=(bsz,),
        in_specs=[pl.BlockSpec((1, n_mem, d), lambda b: (b, 0, 0)), pl.BlockSpec((1, d), const),
                  weight, weight, weight, weight],
        out_specs=[pl.BlockSpec((1, d, hm), lambda b: (b, 0, 0)),
                   pl.BlockSpec((1, hm, d), lambda b: (b, 0, 0))],
        out_shape=[
            jax.ShapeDtypeStruct((bsz, d, hm), BF16),
            jax.ShapeDtypeStruct((bsz, hm, d), BF16),
        ],
        scratch_shapes=[pltpu.VMEM((4, d, d), BF16)],
        compiler_params=_params("arbitrary"),
        name="mem_kv",
    )(mem, g, wk, wv, wq, wo)


def _mix_attn_body(x_ref, o_ref, y_ref, wo_ref, gx_ref, qk_ref, vw_ref, h_ref, *, xh, n_sub):
    dn_w = o_ref.shape[1]
    n_mem = qk_ref.shape[2] // xh
    sub = x_ref.shape[0] // n_sub
    mm = functools.partial(jnp.dot, preferred_element_type=F32)
    wo = wo_ref[...].astype(BF16)

    def rows_gen(r0):
        rows = slice(r0, r0 + sub)
        y = jnp.concatenate([y_ref[q, rows, :] for q in range(y_ref.shape[0])], axis=1)
        mix = mm(o_ref[rows, :].astype(BF16), wo[0:dn_w, :]) + mm(y.astype(BF16), wo[dn_w:, :])
        yield
        h1 = x_ref[rows, :] + mix
        s = mm(_rms(h1, gx_ref[...]).astype(BF16), qk_ref[0])
        yield
        parts = []
        for h in range(xh):
            sh = s[:, h * n_mem:(h + 1) * n_mem]
            e = jnp.exp(sh - jnp.max(sh, axis=-1, keepdims=True))
            parts.append((e / jnp.sum(e, axis=-1, keepdims=True)).astype(BF16))
        att = mm(jnp.concatenate(parts, axis=1), vw_ref[0])
        yield
        h_ref[rows, :] = h1 + att

    _interleave([rows_gen(i * sub) for i in range(n_sub)])


def _mix_attn(x2, o, y, w_out, gx, qk, vw, bsz, seq, *, tm):
    nt, d = x2.shape
    dn_w = o.shape[1]
    n_t = seq // tm
    row = lambda b, t: (b * n_t + t, 0)
    c2 = lambda b, t: (0, 0)
    per_batch = lambda a: pl.BlockSpec((1,) + a.shape[1:], lambda b, t: (b, 0, 0))
    body = functools.partial(_mix_attn_body, xh=X_HEADS, n_sub=2)
    return pl.pallas_call(
        body,
        grid=(bsz, n_t),
        in_specs=[
            pl.BlockSpec((tm, d), row),
            pl.BlockSpec((tm, dn_w), row),
            pl.BlockSpec((y.shape[0], tm, LANES), lambda b, t: (0, b * n_t + t, 0)),
            pl.BlockSpec(w_out.shape, c2, pipeline_mode=pl.Buffered(1)),
            pl.BlockSpec((1, d), c2),
            per_batch(qk),
            per_batch(vw),
        ],
        out_specs=pl.BlockSpec((tm, d), row),
        out_shape=jax.ShapeDtypeStruct((nt, d), F32),
        compiler_params=_params("arbitrary", "arbitrary"),
        name="mix_attn",
    )(x2, o, y, w_out, gx, qk, vw)


def _ffn_body(h_ref, gf_ref, wg_ref, wu_ref, wd_ref, gl_ref, out_ref, *, final, fc):
    h = h_ref[...]
    hn = _rms(h, gf_ref[...]).astype(BF16)
    dff = wg_ref.shape[1]

    def down(gate, up, c0):
        act = (gate * jax.nn.sigmoid(gate) * up).astype(BF16)
        return jnp.dot(act, wd_ref[c0:c0 + fc, :].astype(BF16), preferred_element_type=F32)

    h3 = h
    prev = None
    for c0 in range(0, dff, fc):
        gate = jnp.dot(hn, wg_ref[:, c0:c0 + fc].astype(BF16), preferred_element_type=F32)
        up = jnp.dot(hn, wu_ref[:, c0:c0 + fc].astype(BF16), preferred_element_type=F32)
        if prev is not None:
            h3 = h3 + down(*prev)
        prev = (gate, up, c0)
    h3 = h3 + down(*prev)
    out_ref[...] = _rms(h3, gl_ref[...]) if final else h3


def _ffn(h, gf, wg, wu, wd, gl, *, tm, final):
    nt, d = h.shape
    dff = wg.shape[1]
    c2 = lambda i: (0, 0)
    row = lambda i: (i, 0)
    return pl.pallas_call(
        functools.partial(_ffn_body, final=final, fc=2 * LANES),
        grid=(nt // tm,),
        in_specs=[
            pl.BlockSpec((tm, d), row),
            pl.BlockSpec((1, d), c2),
            pl.BlockSpec((d, dff), c2, pipeline_mode=pl.Buffered(1)),
            pl.BlockSpec((d, dff), c2, pipeline_mode=pl.Buffered(1)),
            pl.BlockSpec((dff, d), c2, pipeline_mode=pl.Buffered(1)),
            pl.BlockSpec((1, d), c2),
        ],
        out_specs=pl.BlockSpec((tm, d), row),
        out_shape=jax.ShapeDtypeStruct((nt, d), F32),
        compiler_params=_params("arbitrary"),
        name="ffn",
    )(h, gf, wg, wu, wd, gl)


def _layer(h2, mem, bsz, seq, norm_mix_g, w_in, conv_w, dn_a_log, dn_dt_bias, dn_norm_g,
           s5_a_re, s5_a_im, s5_b_re, s5_b_im, s5_c_re, s5_c_im, s5_d, s5_log_dt,
           s5_w_glu, s5_b_glu, w_out, norm_x_g, norm_mem_g, w_xq, w_xk, w_xv, w_xo):
    d = h2.shape[1]
    nh = dn_a_log.shape[0]
    dh = dn_norm_g.shape[0]
    dn_w = nh * dh
    s5_w = s5_a_re.shape[0] * s5_b_re.shape[-1]

    off_a = 4 * dn_w
    off_u = off_a + 2 * nh
    w_ab = jnp.pad(w_in[:, off_a:off_u].astype(BF16), ((0, 0), (0, LANES - 2 * nh)))
    w_cat = jnp.concatenate([w_in[:, :off_a].astype(BF16), w_in[:, off_u:].astype(BF16), w_ab],
                            axis=1)
    alog = jnp.pad(dn_a_log, (0, LANES - nh)).reshape(1, LANES)
    dtb = jnp.pad(dn_dt_bias, (0, LANES - nh)).reshape(1, LANES)

    u, o = _mixer_in(h2, seq, norm_mix_g.reshape(1, d), w_cat, conv_w, alog, dtb,
                     dn_norm_g.reshape(1, dh), nh=nh, dh=dh, tb=MIXER_ROWS)

    w1, co = _s5t_prep(s5_a_re, s5_a_im, s5_log_dt, s5_b_re, s5_b_im, s5_c_re, s5_c_im)
    tabs = _s5t_tables(s5_a_re, s5_a_im, s5_log_dt)
    y = _s5t(u, w1, co, tabs, s5_d.reshape(1, s5_w), s5_w_glu, s5_b_glu.reshape(1, s5_w),
             bsz, seq, rc=S5_BLOCKS)

    qk_mem, vw_mem = _mem_kv(mem, norm_mem_g.reshape(1, d), w_xk, w_xv, w_xq, w_xo)
    return _mix_attn(h2, o, y, w_out, norm_x_g.reshape(1, d), qk_mem, vw_mem, bsz, seq,
                     tm=ATTN_ROWS)


def kernel(x, mem, norm_mix_g, w_in, conv_w, dn_a_log, dn_dt_bias, dn_norm_g, s5_a_re, s5_a_im,
           s5_b_re, s5_b_im, s5_c_re, s5_c_im, s5_d, s5_log_dt, s5_w_glu, s5_b_glu, w_out,
           norm_x_g, norm_mem_g, w_xq, w_xk, w_xv, w_xo, norm_ffn_g, w_gate, w_up, w_down,
           norm_final_g):
    bsz, seq, d = x.shape
    depth = w_in.shape[0]
    h = x.reshape(bsz * seq, d)
    for l in range(depth):
        h = _layer(h, mem, bsz, seq, norm_mix_g[l], w_in[l], conv_w[l], dn_a_log[l],
                   dn_dt_bias[l], dn_norm_g[l], s5_a_re[l], s5_a_im[l], s5_b_re[l], s5_b_im[l],
                   s5_c_re[l], s5_c_im[l], s5_d[l], s5_log_dt[l], s5_w_glu[l], s5_b_glu[l],
                   w_out[l], norm_x_g[l], norm_mem_g[l], w_xq[l], w_xk[l], w_xv[l], w_xo[l])
        h = _ffn(h, norm_ffn_g[l].reshape(1, d), w_gate[l], w_up[l], w_down[l],
                 norm_final_g.reshape(1, d), tm=FFN_ROWS, final=l == depth - 1)
    return h.reshape(bsz, seq, d)
```

```python
import functools

import jax
import jax.numpy as jnp
from jax import lax
from jax.experimental import pallas as pl
from jax.experimental.pallas import tpu as pltpu

F32 = jnp.float32
BF16 = jnp.bfloat16
EPS = 1e-6
HIGHEST = lax.Precision.HIGHEST

LANES = 128
SUBLANES = 8
DN_CHUNK = 64
T_CHUNK = 16
S_CHUNK = 16
X_HEADS = 4
VMEM_LIMIT = 58 * 1024 * 1024

MIXER_ROWS = 256
S5_BLOCKS = 128
ATTN_ROWS = 1024
FFN_ROWS = 1024


def _fdot(a, b):
    return jnp.dot(a, b, precision=HIGHEST, preferred_element_type=F32)


def _rms(x, g):
    return x * lax.rsqrt(jnp.mean(x * x, axis=-1, keepdims=True) + EPS) * g


def _interleave(gens, stages_per_round=None):
    pending = [(gen, 1 if stages_per_round is None else stages_per_round[i])
               for i, gen in enumerate(gens)]
    while pending:
        for item in list(pending):
            gen, n = item
            try:
                for _ in range(n):
                    next(gen)
            except StopIteration:
                pending.remove(item)


def _params(*sem):
    return pltpu.CompilerParams(dimension_semantics=sem, vmem_limit_bytes=VMEM_LIMIT)


def _in_proj_stages(x_ref, g_ref, w_ref, cw_ref, alog_ref, dtb_ref,
                    q_ref, k_ref, kt_ref, v_ref, z_ref, u_ref, gt_ref,
                    buf_ref, *, tm, dn_w, nh, dh, cw_n):
    xnb = _rms(x_ref[...], g_ref[...]).astype(BF16)
    gw = 2 * dh

    def conv_silu(p, cols):
        outs = []
        for i in range(gw // LANES):
            c0 = cols.start + i * LANES
            s = c0 // LANES
            pc = p[:, i * LANES:(i + 1) * LANES]
            buf_ref[s, SUBLANES:SUBLANES + tm, :] = pc
            acc = cw_ref[cw_n - 1:cw_n, c0:c0 + LANES] * pc
            for j in range(cw_n - 1):
                off = SUBLANES - (cw_n - 1) + j
                acc = acc + cw_ref[j:j + 1, c0:c0 + LANES] * buf_ref[s, off:off + tm, :]
            buf_ref[s, 0:SUBLANES, :] = buf_ref[s, tm:tm + SUBLANES, :]
            outs.append(acc * jax.nn.sigmoid(acc))
        return jnp.concatenate(outs, axis=1)

    def l2n(a, scale):
        parts = []
        for i in range(gw // dh):
            ah = a[:, i * dh:(i + 1) * dh]
            parts.append(ah * (lax.rsqrt(jnp.sum(ah * ah, -1, keepdims=True) + EPS) * scale))
        return jnp.concatenate(parts, axis=1)

    def ep_q(p, c0):
        q_ref[:, c0:c0 + gw] = l2n(conv_silu(p, slice(c0, c0 + gw)), dh ** -0.5)

    def ep_k(p, c0):
        kn = l2n(conv_silu(p, slice(dn_w + c0, dn_w + c0 + gw)), 1.0)
        k_ref[:, c0:c0 + gw] = kn
        kt_ref[c0:c0 + gw, :] = kn.T

    def ep_v(p, c0):
        v_ref[:, c0:c0 + gw] = conv_silu(p, slice(2 * dn_w + c0, 2 * dn_w + c0 + gw))

    def ep_z(p, c0):
        z_ref[:, c0:c0 + gw] = p

    def ep_u(p, c0):
        for j in range(gw // LANES):
            u_ref[c0 // LANES + j] = p[:, j * LANES:(j + 1) * LANES]

    def ep_gates(ab, c0):
        lane = lax.broadcasted_iota(jnp.int32, ab.shape, 1)
        g = -jnp.exp(alog_ref[...]) * jnp.logaddexp(ab + dtb_ref[...], 0.0)
        gates = jnp.where(lane < nh, g, jax.nn.sigmoid(ab))
        gt_ref[...] = gates.T[0:SUBLANES, :]

    groups = []
    for i, ep in enumerate((ep_q, ep_k, ep_v, ep_z, ep_u)):
        groups += [(ep, i * dn_w + c0, gw, c0) for c0 in range(0, dn_w, gw)]
    groups.append((ep_gates, 5 * dn_w, LANES, 0))

    prev = None
    for ep, w0, width, c0 in groups:
        p = jnp.dot(xnb, w_ref[:, w0:w0 + width], preferred_element_type=F32)
        if prev is not None:
            prev[0](prev[1], prev[2])
        prev = (ep, p, c0)
        yield
    prev[0](prev[1], prev[2])


def _delta_stages(q_ref, k_ref, kt_ref, v_ref, z_ref, gt_ref, ng_ref,
                  o_ref, s_ref, u_sc, w_sc, qg_sc, attn_sc, kdt_sc, gtot_sc, wr, rd,
                  *, tb, nh, dh, ch):
    nc = tb // ch
    heads = range(nh)
    hls = [slice(h * dh, (h + 1) * dh) for h in heads]
    mm = functools.partial(jnp.dot, preferred_element_type=F32)

    def phase_b():
        gtot_b = gtot_sc[rd]
        s_all = [s_ref[h] for h in heads]
        for cidx in range(nc):
            rows = slice(cidx * ch, (cidx + 1) * ch)
            sb = [s_all[h].astype(BF16) for h in heads]
            ws = [mm(w_sc[rd, h, rows, :], sb[h]) for h in heads]
            qs = [mm(qg_sc[rd, h, rows, :], sb[h]) for h in heads]
            yield
            vb = [(u_sc[rd, h, rows, :] - ws[h]).astype(BF16) for h in heads]
            av = [mm(attn_sc[rd, h, rows, rows], vb[h]) for h in heads]
            kv = [mm(kdt_sc[rd, h, :, rows], vb[h]) for h in heads]
            yield
            outs = []
            for h in heads:
                a_last = jnp.exp(gtot_b[h:h + 1, cidx * ch:cidx * ch + 1])
                s_all[h] = s_all[h] * a_last + kv[h]
                o = qs[h] + av[h]
                o = o * lax.rsqrt(jnp.mean(o * o, -1, keepdims=True) + EPS)
                zh = z_ref[rows, hls[h]]
                outs.append(o * ng_ref[...] * (zh * jax.nn.sigmoid(zh)))
            o_ref[rows, :] = jnp.concatenate(outs, axis=1)
        for h in heads:
            s_ref[h] = s_all[h]

    r = lax.broadcasted_iota(jnp.int32, (tb, tb), 0)
    c = lax.broadcasted_iota(jnp.int32, (tb, tb), 1)
    same = (r // ch) == (c // ch)
    causal_bd = same & (c <= r)
    strict_bd = same & (c < r)
    gt = gt_ref[...]
    gcum_t = _fdot(gt, jnp.where(same & (r <= c), 1.0, 0.0))
    gtot_t = _fdot(gt, jnp.where(same, 1.0, 0.0))
    gcum = gcum_t.T
    gates = gt.T

    def pack(x):
        acc = x[0:ch, :]
        for i in range(1, nc):
            acc = acc + x[i * ch:(i + 1) * ch, :]
        return acc

    def unpack(x):
        return jnp.where(same, jnp.concatenate([x] * nc, axis=0), 0.0).astype(BF16)

    n_sq = max(1, (ch - 1).bit_length() - 1)

    def phase_a():
        gcol = [gcum[:, h:h + 1] for h in heads]
        grow = [gcum_t[h:h + 1, :] for h in heads]
        beta = [gates[:, nh + h:nh + h + 1] for h in heads]
        kb = [k_ref[:, hls[h]] * beta[h] for h in heads]
        gram = [mm(jnp.concatenate([kb[h], q_ref[:, hls[h]]], axis=0).astype(BF16),
                   kt_ref[hls[h], :].astype(BF16)) for h in heads]
        yield
        ppk, apk = [], []
        for h in heads:
            decay = jnp.exp(jnp.where(causal_bd, gcol[h] - grow[h], -jnp.inf))
            p_bd = -jnp.where(strict_bd, gram[h][:tb] * decay, 0.0)
            attn_sc[wr, h] = (gram[h][tb:] * decay).astype(BF16)
            pk = pack(p_bd)
            apk.append(pk)
            ppk.append(mm(pk.astype(BF16), p_bd.astype(BF16)))
        yield
        for j in range(1, n_sq + 1):
            res = []
            for h in heads:
                lhs = jnp.concatenate([apk[h], ppk[h]], axis=0) if j < n_sq else apk[h]
                res.append(mm(lhs.astype(BF16), unpack(ppk[h])))
            yield
            for h in heads:
                apk[h] = apk[h] + ppk[h] + res[h][:ch]
                if j < n_sq:
                    ppk[h] = res[h][ch:]
        egc = [jnp.exp(gcol[h]) for h in heads]
        rhs = [jnp.concatenate([v_ref[:, hls[h]] * beta[h], kb[h] * egc[h]], axis=1) for h in heads]
        prod = [mm(unpack(apk[h]), rhs[h].astype(BF16)) for h in heads]
        yield
        for h in heads:
            uw = rhs[h] + prod[h]
            u_sc[wr, h] = uw[:, :dh]
            w_sc[wr, h] = uw[:, dh:].astype(BF16)
            qg_sc[wr, h] = (q_ref[:, hls[h]] * egc[h]).astype(BF16)
            kdt_sc[wr, h] = (kt_ref[hls[h], :] * jnp.exp(gtot_t[h:h + 1, :] - grow[h])).astype(BF16)
        gtot_sc[wr] = gtot_t

    return phase_b(), phase_a()


def _mixer_in_body(x_ref, g_ref, w_ref, cw_ref, alog_ref, dtb_ref, ng_ref,
                   u_ref, o_ref,
                   buf_ref, q_sc, k_sc, kt_sc, v_sc, z_sc, gt_sc,
                   s_ref, u_sc, w_sc, qg_sc, attn_sc, kdt_sc, gtot_sc,
                   *, tb, n_t, nh, dh, cw_n):
    s = pl.program_id(0)
    dn_w = nh * dh

    @pl.when(s == 0)
    def _():
        for ref in (buf_ref, q_sc, k_sc, kt_sc, v_sc, z_sc, gt_sc,
                    s_ref, u_sc, w_sc, qg_sc, attn_sc, kdt_sc, gtot_sc):
            ref[...] = jnp.zeros(ref.shape, ref.dtype)

    @pl.when(s % n_t == 0)
    def _():
        buf_ref[:, 0:SUBLANES, :] = jnp.zeros((3 * dn_w // LANES, SUBLANES, LANES), F32)

    @pl.when((s - 2) % n_t == 0)
    def _():
        s_ref[...] = jnp.zeros(s_ref.shape, F32)

    cur = s % 2
    prv = 1 - cur
    zw = s % 3
    zr = (s + 1) % 3
    proj = _in_proj_stages(
        x_ref, g_ref, w_ref, cw_ref, alog_ref, dtb_ref,
        q_sc.at[cur], k_sc.at[cur], kt_sc.at[cur], v_sc.at[cur], z_sc.at[zw], u_ref, gt_sc.at[cur],
        buf_ref, tm=tb, dn_w=dn_w, nh=nh, dh=dh, cw_n=cw_n)
    phase_b, phase_a = _delta_stages(
        q_sc.at[prv], k_sc.at[prv], kt_sc.at[prv], v_sc.at[prv], z_sc.at[zr], gt_sc.at[prv], ng_ref,
        o_ref, s_ref, u_sc, w_sc, qg_sc, attn_sc, kdt_sc, gtot_sc, cur, prv,
        tb=tb, nh=nh, dh=dh, ch=DN_CHUNK)
    _interleave([proj, phase_b, phase_a])


def _mixer_in(x2, seq, g, w_cat, conv_w, alog, dtb, ng, *, nh, dh, tb):
    nt, d = x2.shape
    dn_w = nh * dh
    n_blk = nt // tb
    cw_n = conv_w.shape[0]
    const = lambda s: (0, 0)
    in_blk = lambda s: (jnp.minimum(s, n_blk - 1), 0)
    out_blk = lambda s: (jnp.maximum(s - 2, 0), 0)
    body = functools.partial(_mixer_in_body, tb=tb, n_t=seq // tb, nh=nh, dh=dh, cw_n=cw_n)
    two = lambda *shape: pltpu.VMEM((2,) + shape, F32)
    return pl.pallas_call(
        body,
        grid=(n_blk + 2,),
        in_specs=[pl.BlockSpec((tb, d), in_blk)]
        + [pl.BlockSpec(a.shape, const) for a in (g, w_cat, conv_w, alog, dtb, ng)],
        out_specs=[pl.BlockSpec((dn_w // LANES, tb, LANES),
                                lambda s: (0, jnp.minimum(s, n_blk - 1), 0)),
                   pl.BlockSpec((tb, dn_w), out_blk)],
        out_shape=[jax.ShapeDtypeStruct((dn_w // LANES, nt, LANES), F32),
                   jax.ShapeDtypeStruct((nt, dn_w), F32)],
        scratch_shapes=[
            pltpu.VMEM((3 * dn_w // LANES, tb + SUBLANES, LANES), F32),
            two(tb, dn_w), two(tb, dn_w), two(dn_w, tb), two(tb, dn_w),
            pltpu.VMEM((3, tb, dn_w), F32),
            two(SUBLANES, tb),
            pltpu.VMEM((nh, dh, dh), F32),
            pltpu.VMEM((2, nh, tb, dh), F32),
            pltpu.VMEM((2, nh, tb, dh), BF16),
            pltpu.VMEM((2, nh, tb, dh), BF16),
            pltpu.VMEM((2, nh, tb, tb), BF16),
            pltpu.VMEM((2, nh, dh, tb), BF16),
            pltpu.VMEM((2, SUBLANES, tb), F32),
        ],
        compiler_params=_params("arbitrary"),
        name="mixer_in",
    )(x2, g, w_cat, conv_w, alog, dtb, ng)


def _ftdot(a, b):
    return lax.dot_general(a.astype(BF16), b.astype(BF16), (((0,), (0,)), ((), ())),
                           preferred_element_type=F32)


def _cmul(xr, xi, yr, yi):
    return xr * yr - xi * yi, xr * yi + xi * yr


def _s5t_prep_body(are_ref, aim_ref, ldt_ref, btr_ref, bti_ref, ctr_ref, cti_ref,
                   are2_ref, aim2_ref, ldt2_ref,
                   w1_ref, co_ref, pin_re, pin_im, pout_re, pout_im, al_re, al_im,
                   *, tc, sc, cg, n_p):
    n_grp = w1_ref.shape[0]
    kw = tc * cg
    a_re = are_ref[...]
    a_im = aim_ref[...]
    dt = jnp.exp(ldt_ref[...])
    lre = a_re * dt
    lim = a_im * dt
    mag = jnp.exp(lre)
    ab = (mag * jnp.cos(lim), mag * jnp.sin(lim))
    nr, ni = ab[0] - 1.0, ab[1]
    den = a_re * a_re + a_im * a_im
    co = ((nr * a_re + ni * a_im) / den, (ni * a_re - nr * a_im) / den)
    inv_mag2 = jnp.exp(-2.0 * lre)
    ai = (ab[0] * inv_mag2, -ab[1] * inv_mag2)
    n_bits = tc.bit_length() - 1
    pos, neg = [ab], [ai]
    for _ in range(n_bits - 1):
        pos.append(_cmul(*pos[-1], *pos[-1]))
        neg.append(_cmul(*neg[-1], *neg[-1]))
    top = pos[0]
    for f in pos[1:]:
        top = _cmul(*top, *f)
    rows = [co, top] + pos + neg
    flat = [part for pair in rows for part in pair]
    cols = []
    for i in range(0, len(flat), SUBLANES):
        blk = flat[i:i + SUBLANES]
        blk = blk + [blk[-1]] * (SUBLANES - len(blk))
        t = jnp.concatenate(blk, axis=0).T
        cols += [t[:, j:j + 1] for j in range(SUBLANES)]
    pair = lambda k: (cols[2 * k], cols[2 * k + 1])
    co_c, top_c = pair(0), pair(1)
    pos_c = [pair(2 + k) for k in range(n_bits)]
    neg_c = [pair(2 + n_bits + k) for k in range(n_bits)]

    slot = lax.broadcasted_iota(jnp.int32, (1, kw), 1) // cg

    def power(factors):
        acc = None
        for k, (fr, fi) in enumerate(factors):
            bit = ((slot >> k) & 1) == 1
            term = (jnp.where(bit, fr, 1.0), jnp.where(bit, fi, 0.0))
            acc = term if acc is None else _cmul(*acc, *term)
        return acc

    place = jnp.where(lax.broadcasted_iota(jnp.int32, (cg, kw), 0)
                      == lax.broadcasted_iota(jnp.int32, (cg, kw), 1) % cg, 1.0, 0.0)
    place = place.astype(BF16)

    def rep(ref):
        x = ref[...]
        hi = x.astype(BF16)
        r1 = x - hi.astype(F32)
        mid = r1.astype(BF16)
        lo = (r1 - mid.astype(F32)).astype(BF16)
        return sum(jnp.dot(part, place, preferred_element_type=F32) for part in (hi, mid, lo))
    bt = _cmul(*co_c, rep(btr_ref), rep(bti_ref))
    pb = _cmul(*bt, *power(neg_c))
    qc = _cmul(rep(ctr_ref), rep(cti_ref), *power(pos_c))
    q1 = _cmul(*qc, *pos_c[0])
    bi = _cmul(*pb, *top_c)
    r2 = lax.broadcasted_iota(jnp.int32, (kw, kw), 0) // cg
    c2 = lax.broadcasted_iota(jnp.int32, (kw, kw), 1) // cg
    eye = jnp.where(lax.broadcasted_iota(jnp.int32, (n_p, n_p), 0)
                    == lax.broadcasted_iota(jnp.int32, (n_p, n_p), 1), 1.0, 0.0)
    for g in range(n_grp):
        rs = slice(g * n_p, (g + 1) * n_p)
        m_full = _ftdot(pb[0][rs], qc[0][rs]) - _ftdot(pb[1][rs], qc[1][rs])
        w1_ref[g, :, 0:kw] = jnp.where(r2 <= c2, m_full, 0.0).astype(w1_ref.dtype)
        w1_ref[g, :, kw:kw + n_p] = _ftdot(bi[0][rs], eye).astype(w1_ref.dtype)
        w1_ref[g, :, kw + n_p:kw + 2 * n_p] = _ftdot(bi[1][rs], eye).astype(w1_ref.dtype)
        co_ref[g, 0:n_p, :] = q1[0][rs].astype(co_ref.dtype)
        co_ref[g, n_p:2 * n_p, :] = (-q1[1][rs]).astype(co_ref.dtype)

    _s5t_tables_body(are2_ref, aim2_ref, ldt2_ref, pin_re, pin_im, pout_re, pout_im, al_re, al_im,
                     tc=tc, sc=sc)


def _s5t_prep(a_re, a_im, log_dt, b_re, b_im, c_re, c_im):
    g, p = a_re.shape
    cg = b_re.shape[-1]
    kw = T_CHUNK * cg
    lanes = g * 2 * p
    ldt = jnp.broadcast_to(log_dt[:, None], (g, p))
    row = lambda a: a.reshape(1, g * p)
    spread = lambda a: jnp.broadcast_to(a[:, None, :], (g, 2, p)).reshape(1, lanes)
    slots = lambda a: a.reshape(g * p, cg)
    tab16 = jax.ShapeDtypeStruct((S_CHUNK, lanes), BF16)
    tab32 = jax.ShapeDtypeStruct((S_CHUNK, lanes), F32)
    one = jax.ShapeDtypeStruct((1, lanes), F32)
    w1, co, *tabs = pl.pallas_call(
        functools.partial(_s5t_prep_body, tc=T_CHUNK, sc=S_CHUNK, cg=cg, n_p=p),
        out_shape=[jax.ShapeDtypeStruct((g, kw, kw + 2 * p), BF16),
                   jax.ShapeDtypeStruct((g, 2 * p, kw), BF16),
                   tab16, tab16, tab32, tab32, one, one],
        compiler_params=pltpu.CompilerParams(vmem_limit_bytes=VMEM_LIMIT),
        name="s5t_prep",
    )(row(a_re), row(a_im), row(ldt),
      slots(b_re), slots(b_im), slots(c_re.transpose(0, 2, 1)), slots(c_im.transpose(0, 2, 1)),
      spread(a_re), spread(a_im), spread(ldt))
    return w1, co, tabs


def _s5t_tables_body(are_ref, aim_ref, ldt_ref, pin_re, pin_im, pout_re, pout_im, al_re, al_im,
                     *, tc, sc):
    lre = are_ref[...] * jnp.exp(ldt_ref[...]) * float(tc)
    lim = aim_ref[...] * jnp.exp(ldt_ref[...]) * float(tc)
    lane = lax.broadcasted_iota(jnp.int32, lre.shape, 1)
    sign = jnp.where((lane % LANES) < LANES // 2, -1.0, 1.0)
    n = lax.broadcasted_iota(jnp.int32, (sc, 1), 0).astype(F32) - float(sc // 2)
    m_out = jnp.exp(lre * n)
    m_in = jnp.exp(-(lre * n))
    ang = lim * n
    pin_re[...] = (m_in * jnp.cos(ang)).astype(pin_re.dtype)
    pin_im[...] = (sign * (-(m_in * jnp.sin(ang)))).astype(pin_im.dtype)
    pout_re[...] = m_out * jnp.cos(ang)
    pout_im[...] = sign * (m_out * jnp.sin(ang))
    m_al = jnp.exp(lre * float(sc))
    al_re[...] = m_al * jnp.cos(lim * float(sc))
    al_im[...] = sign * (m_al * jnp.sin(lim * float(sc)))


def _atom_transpose(regs, atom, width):
    regs = list(regs)
    n = len(regs)
    d = n // 2
    while d:
        low = (atom & d) == 0
        for j in range(n):
            if j & d:
                continue
            a, b = regs[j], regs[j + d]
            regs[j] = jnp.where(low, a, pltpu.roll(b, d * width, axis=1))
            regs[j + d] = jnp.where(low, pltpu.roll(a, LANES - d * width, axis=1), b)
        d //= 2
    return regs


def _sublane_transpose(regs, srow):
    regs = list(regs)
    d = SUBLANES // 2
    while d:
        low = (srow & d) == 0
        for j in range(SUBLANES):
            if j & d:
                continue
            a, b = regs[j], regs[j + d]
            regs[j] = jnp.where(low, a, pltpu.roll(b, d, axis=0))
            regs[j + d] = jnp.where(low, pltpu.roll(a, SUBLANES - d, axis=0), b)
        d //= 2
    return regs


def _swap_halves(x):
    tiles = [pltpu.roll(x[:, t * LANES:(t + 1) * LANES], LANES // 2, axis=1)
             for t in range(x.shape[1] // LANES)]
    return jnp.concatenate(tiles, axis=1)


def _s5t_body(u_ref, w1_ref, co_ref, pin_re, pin_im, pout_re, pout_im, al_re, al_im,
              d_ref, wglu_ref, bglu_ref, y_ref,
              upk_ref, yloc_ref, sin_ref, xs_ref, carry_ref, udi_ref, *, rc, tc, sc, n_grp, cg):
    t = pl.program_id(1)
    kw = tc * cg
    n_q = u_ref.shape[0]
    atoms = LANES // cg

    @pl.when(t == 0)
    def _():
        carry_ref[...] = jnp.zeros(carry_ref.shape, F32)
        xs_ref[:, 0:SUBLANES, :] = jnp.zeros((n_grp, SUBLANES, LANES), F32)

    atom = lax.broadcasted_iota(jnp.int32, (rc, LANES), 1) // cg

    n_a = tc // atoms

    assert atoms == SUBLANES
    srow = lax.broadcasted_iota(jnp.int32, (SUBLANES, LANES), 0)

    def rows_to_planes(q, a):
        pieces = [[] for _ in range(atoms)]
        for nb in range(rc // SUBLANES):
            nat = [u_ref[q, (nb * SUBLANES + m) * tc + a * atoms:
                         (nb * SUBLANES + m) * tc + (a + 1) * atoms, :] for m in range(SUBLANES)]
            for k, piece in enumerate(_sublane_transpose(nat, srow)):
                pieces[k].append(piece)
        return [jnp.concatenate(p, axis=0) for p in pieces]

    def planes_to_rows(q, a, planes):
        for nb in range(rc // SUBLANES):
            regs = [p[nb * SUBLANES:(nb + 1) * SUBLANES, :] for p in planes]
            for m, nat in enumerate(_sublane_transpose(regs, srow)):
                r0 = (nb * SUBLANES + m) * tc + a * atoms
                y_ref[q, r0:r0 + atoms, :] = nat

    def pack():
        for q in range(n_q):
            for a in range(n_a):
                planes = rows_to_planes(q, a)
                for k in range(atoms):
                    udi_ref[q, a * atoms + k] = planes[k]
                for r, tile in enumerate(_atom_transpose(planes, atom, cg)):
                    upk_ref[q * atoms + r, :, a * LANES:(a + 1) * LANES] = tile.astype(BF16)
                yield

    per_stage = atoms // n_a

    def local():
        for _ in range(n_a):
            yield
        for g in range(n_grp):
            res = jnp.dot(upk_ref[g], w1_ref[g], preferred_element_type=F32)
            yloc_ref[g] = res[:, :kw]
            sin_ref[g] = res[:, kw:]
            if g % per_stage == per_stage - 1:
                yield

    _interleave([pack(), local()])

    ri = lax.broadcasted_iota(jnp.int32, (sc, sc), 0)
    ci = lax.broadcasted_iota(jnp.int32, (sc, sc), 1)
    tri = jnp.where(ci <= ri, 1.0, 0.0).astype(BF16)
    n_c2 = rc // sc
    ss = [jnp.concatenate([sin_ref[g, c2 * sc:(c2 + 1) * sc, :] for g in range(n_grp)], axis=1)
          for c2 in range(n_c2)]
    ss_sw = [_swap_halves(s) for s in ss]
    xin = [pin_re[...] * ss[c2].astype(BF16) + pin_im[...] * ss_sw[c2].astype(BF16)
           for c2 in range(n_c2)]
    psum = [jnp.dot(tri, x, preferred_element_type=F32) for x in xin]
    psum_sw = [_swap_halves(p) for p in psum]
    carry = carry_ref[...]
    carry_sw = _swap_halves(carry)
    carries = []
    for c2 in range(rc // sc):
        carries.append((carry, carry_sw))
        last = psum[c2][sc - 1:sc, :] + carry
        last_sw = psum_sw[c2][sc - 1:sc, :] + carry_sw
        carry = al_re[...] * last + al_im[...] * last_sw
        carry_sw = al_re[...] * last_sw - al_im[...] * last
    carry_ref[...] = carry
    for c2 in range(rc // sc):
        x = (pout_re[...] * (psum[c2] + carries[c2][0])
             + pout_im[...] * (psum_sw[c2] + carries[c2][1]))
        for g in range(n_grp):
            xs_ref[g, SUBLANES + c2 * sc:SUBLANES + (c2 + 1) * sc, :] = x[:, g * LANES:(g + 1) * LANES]

    for g in range(n_grp):
        xprev = xs_ref[g, SUBLANES - 1:SUBLANES - 1 + rc, :].astype(BF16)
        yloc_ref[g] = yloc_ref[g] + jnp.dot(xprev, co_ref[g], preferred_element_type=F32)
    xs_ref[:, SUBLANES - 1:SUBLANES, :] = xs_ref[:, SUBLANES - 1 + rc:SUBLANES + rc, :]

    wglu = wglu_ref[...].astype(BF16)

    def unpack(a):
        for _ in range(a * n_q):
            yield
        planes = [[None] * n_q for _ in range(atoms)]
        for q in range(n_q):
            tiles = [yloc_ref[q * atoms + r, :, a * LANES:(a + 1) * LANES] for r in range(atoms)]
            for k, tile in enumerate(_atom_transpose(tiles, atom, cg)):
                planes[k][q] = tile
            yield
        outs = []
        for k in range(atoms):
            i = a * atoms + k
            uu = jnp.concatenate([udi_ref[q, i] for q in range(n_q)], axis=1)
            y = jax.nn.gelu(jnp.concatenate(planes[k], axis=1) + d_ref[...] * uu)
            gate = jax.nn.sigmoid(jnp.dot(y.astype(BF16), wglu, preferred_element_type=F32)
                                  + bglu_ref[...])
            outs.append(y * gate)
            if k % 2 == 1:
                yield
        for q in range(n_q):
            planes_to_rows(q, a, [o[:, q * LANES:(q + 1) * LANES] for o in outs])
            yield

    _interleave([unpack(a) for a in range(n_a)])


def _s5t(u4, w1, co, tabs, d, wglu, bglu, bsz, seq, *, rc):
    n_q, nt, _ = u4.shape
    n_grp, kw, wcols = w1.shape
    n_state = (wcols - kw) // 2
    cg = kw // T_CHUNK
    rows = rc * T_CHUNK
    n_t = seq // rows
    blk = pl.BlockSpec((n_q, rows, LANES), lambda b, t: (0, b * n_t + t, 0))
    c2 = lambda b, t: (0, 0)
    c3 = lambda b, t: (0, 0, 0)
    whole = lambda a: pl.BlockSpec(a.shape, c3 if a.ndim == 3 else c2)
    body = functools.partial(_s5t_body, rc=rc, tc=T_CHUNK, sc=S_CHUNK, n_grp=n_grp, cg=cg)
    return pl.pallas_call(
        body,
        grid=(bsz, n_t),
        in_specs=[blk] + [whole(a) for a in (w1, co, *tabs, d, wglu, bglu)],
        out_specs=blk,
        out_shape=jax.ShapeDtypeStruct((n_q, nt, LANES), F32),
        scratch_shapes=[
            pltpu.VMEM((n_grp, rc, kw), BF16),
            pltpu.VMEM((n_grp, rc, kw), F32),
            pltpu.VMEM((n_grp, rc, 2 * n_state), F32),
            pltpu.VMEM((n_grp, rc + SUBLANES, 2 * n_state), F32),
            pltpu.VMEM((1, n_grp * 2 * n_state), F32),
            pltpu.VMEM((n_q, T_CHUNK, rc, LANES), F32),
        ],
        compiler_params=_params("arbitrary", "arbitrary"),
        name="s5t",
    )(u4, w1, co, *tabs, d, wglu, bglu)


def _mem_kv_body(m_ref, g_ref, wk_ref, wv_ref, wq_ref, wo_ref, qk_ref, vw_ref, w_sc, *, xh):
    @pl.when(pl.program_id(0) == 0)
    def _():
        for i, ref in enumerate((wk_ref, wv_ref, wq_ref, wo_ref)):
            w_sc[i] = ref[...].astype(BF16)

    _, n_mem, d = m_ref.shape
    xd = d // xh
    scale = xd ** -0.5
    mn = _rms(m_ref[0], g_ref[...]).astype(BF16)
    k = jnp.dot(mn, w_sc[0], preferred_element_type=F32).astype(BF16)
    v = jnp.dot(mn, w_sc[1], preferred_element_type=F32).astype(BF16)
    nt_dims = (((1,), (1,)), ((), ()))
    for h in range(xh):
        hl = slice(h * xd, (h + 1) * xd)
        ml = slice(h * n_mem, (h + 1) * n_mem)
        qk = lax.dot_general(w_sc[2, :, hl], k[:, hl], nt_dims, preferred_element_type=F32)
        qk_ref[0, :, ml] = (qk * scale).astype(BF16)
        vw_ref[0, ml, :] = jnp.dot(v[:, hl], w_sc[3, hl, :],
                                   preferred_element_type=F32).astype(BF16)


def _mem_kv(mem, g, wk, wv, wq, wo):
    bsz, n_mem, d = mem.shape
    hm = X_HEADS * n_mem
    const = lambda b: (0, 0)
    weight = pl.BlockSpec((d, d), const, pipeline_mode=pl.Buffered(1))
    return pl.pallas_call(
        functools.partial(_mem_kv_body, xh=X_HEADS),
        grid=(bsz,),
        in_specs=[pl.BlockSpec((1, n_mem, d), lambda b: (b, 0, 0)), pl.BlockSpec((1, d), const),
                  weight, weight, weight, weight],
        out_specs=[pl.BlockSpec((1, d, hm), lambda b: (b, 0, 0)),
                   pl.BlockSpec((1, hm, d), lambda b: (b, 0, 0))],
        out_shape=[
            jax.ShapeDtypeStruct((bsz, d, hm), BF16),
            jax.ShapeDtypeStruct((bsz, hm, d), BF16),
        ],
        scratch_shapes=[pltpu.VMEM((4, d, d), BF16)],
        compiler_params=_params("arbitrary"),
        name="mem_kv",
    )(mem, g, wk, wv, wq, wo)


def _mix_attn_body(x_ref, o_ref, y_ref, wo_ref, gx_ref, qk_ref, vw_ref, h_ref, *, xh, n_sub):
    dn_w = o_ref.shape[1]
    n_mem = qk_ref.shape[2] // xh
    sub = x_ref.shape[0] // n_sub
    mm = functools.partial(jnp.dot, preferred_element_type=F32)
    wo = wo_ref[...].astype(BF16)

    def rows_gen(r0):
        rows = slice(r0, r0 + sub)
        y = jnp.concatenate([y_ref[q, rows, :] for q in range(y_ref.shape[0])], axis=1)
        mix = mm(o_ref[rows, :].astype(BF16), wo[0:dn_w, :]) + mm(y.astype(BF16), wo[dn_w:, :])
        yield
        h1 = x_ref[rows, :] + mix
        s = mm(_rms(h1, gx_ref[...]).astype(BF16), qk_ref[0])
        yield
        parts = []
        for h in range(xh):
            sh = s[:, h * n_mem:(h + 1) * n_mem]
            e = jnp.exp(sh - jnp.max(sh, axis=-1, keepdims=True))
            parts.append((e / jnp.sum(e, axis=-1, keepdims=True)).astype(BF16))
        att = mm(jnp.concatenate(parts, axis=1), vw_ref[0])
        yield
        h_ref[rows, :] = h1 + att

    _interleave([rows_gen(i * sub) for i in range(n_sub)])


def _mix_attn(x2, o, y, w_out, gx, qk, vw, bsz, seq, *, tm):
    nt, d = x2.shape
    dn_w = o.shape[1]
    n_t = seq // tm
    row = lambda b, t: (b * n_t + t, 0)
    c2 = lambda b, t: (0, 0)
    per_batch = lambda a: pl.BlockSpec((1,) + a.shape[1:], lambda b, t: (b, 0, 0))
    body = functools.partial(_mix_attn_body, xh=X_HEADS, n_sub=2)
    return pl.pallas_call(
        body,
        grid=(bsz, n_t),
        in_specs=[
            pl.BlockSpec((tm, d), row),
            pl.BlockSpec((tm, dn_w), row),
            pl.BlockSpec((y.shape[0], tm, LANES), lambda b, t: (0, b * n_t + t, 0)),
            pl.BlockSpec(w_out.shape, c2, pipeline_mode=pl.Buffered(1)),
            pl.BlockSpec((1, d), c2),
            per_batch(qk),
            per_batch(vw),
        ],
        out_specs=pl.BlockSpec((tm, d), row),
        out_shape=jax.ShapeDtypeStruct((nt, d), F32),
        compiler_params=_params("arbitrary", "arbitrary"),
        name="mix_attn",
    )(x2, o, y, w_out, gx, qk, vw)


def _ffn_body(h_ref, gf_ref, wg_ref, wu_ref, wd_ref, gl_ref, out_ref, *, final, fc):
    h = h_ref[...]
    hn = _rms(h, gf_ref[...]).astype(BF16)
    dff = wg_ref.shape[1]

    def down(gate, up, c0):
        act = (gate * jax.nn.sigmoid(gate) * up).astype(BF16)
        return jnp.dot(act, wd_ref[c0:c0 + fc, :].astype(BF16), preferred_element_type=F32)

    h3 = h
    prev = None
    for c0 in range(0, dff, fc):
        gate = jnp.dot(hn, wg_ref[:, c0:c0 + fc].astype(BF16), preferred_element_type=F32)
        up = jnp.dot(hn, wu_ref[:, c0:c0 + fc].astype(BF16), preferred_element_type=F32)
        if prev is not None:
            h3 = h3 + down(*prev)
        prev = (gate, up, c0)
    h3 = h3 + down(*prev)
    out_ref[...] = _rms(h3, gl_ref[...]) if final else h3


def _ffn(h, gf, wg, wu, wd, gl, *, tm, final):
    nt, d = h.shape
    dff = wg.shape[1]
    c2 = lambda i: (0, 0)
    row = lambda i: (i, 0)
    return pl.pallas_call(
        functools.partial(_ffn_body, final=final, fc=2 * LANES),
        grid=(nt // tm,),
        in_specs=[
            pl.BlockSpec((tm, d), row),
            pl.BlockSpec((1, d), c2),
            pl.BlockSpec((d, dff), c2, pipeline_mode=pl.Buffered(1)),
            pl.BlockSpec((d, dff), c2, pipeline_mode=pl.Buffered(1)),
            pl.BlockSpec((dff, d), c2, pipeline_mode=pl.Buffered(1)),
            pl.BlockSpec((1, d), c2),
        ],
        out_specs=pl.BlockSpec((tm, d), row),
        out_shape=jax.ShapeDtypeStruct((nt, d), F32),
        compiler_params=_params("arbitrary"),
        name="ffn",
    )(h, gf, wg, wu, wd, gl)


def _layer(h2, mem, bsz, seq, norm_mix_g, w_in, conv_w, dn_a_log, dn_dt_bias, dn_norm_g,
           s5_a_re, s5_a_im, s5_b_re, s5_b_im, s5_c_re, s5_c_im, s5_d, s5_log_dt,
           s5_w_glu, s5_b_glu, w_out, norm_x_g, norm_mem_g, w_xq, w_xk, w_xv, w_xo):
    d = h2.shape[1]
    nh = dn_a_log.shape[0]
    dh = dn_norm_g.shape[0]
    dn_w = nh * dh
    s5_w = s5_a_re.shape[0] * s5_b_re.shape[-1]

    off_a = 4 * dn_w
    off_u = off_a + 2 * nh
    w_ab = jnp.pad(w_in[:, off_a:off_u].astype(BF16), ((0, 0), (0, LANES - 2 * nh)))
    w_cat = jnp.concatenate([w_in[:, :off_a].astype(BF16), w_in[:, off_u:].astype(BF16), w_ab],
                            axis=1)
    alog = jnp.pad(dn_a_log, (0, LANES - nh)).reshape(1, LANES)
    dtb = jnp.pad(dn_dt_bias, (0, LANES - nh)).reshape(1, LANES)

    u, o = _mixer_in(h2, seq, norm_mix_g.reshape(1, d), w_cat, conv_w, alog, dtb,
                     dn_norm_g.reshape(1, dh), nh=nh, dh=dh, tb=MIXER_ROWS)

    w1, co, tabs = _s5t_prep(s5_a_re, s5_a_im, s5_log_dt, s5_b_re, s5_b_im, s5_c_re, s5_c_im)
    y = _s5t(u, w1, co, tabs, s5_d.reshape(1, s5_w), s5_w_glu, s5_b_glu.reshape(1, s5_w),
             bsz, seq, rc=S5_BLOCKS)

    qk_mem, vw_mem = _mem_kv(mem, norm_mem_g.reshape(1, d), w_xk, w_xv, w_xq, w_xo)
    return _mix_attn(h2, o, y, w_out, norm_x_g.reshape(1, d), qk_mem, vw_mem, bsz, seq,
                     tm=ATTN_ROWS)


def kernel(x, mem, norm_mix_g, w_in, conv_w, dn_a_log, dn_dt_bias, dn_norm_g, s5_a_re, s5_a_im,
           s5_b_re, s5_b_im, s5_c_re, s5_c_im, s5_d, s5_log_dt, s5_w_glu, s5_b_glu, w_out,
           norm_x_g, norm_mem_g, w_xq, w_xk, w_xv, w_xo, norm_ffn_g, w_gate, w_up, w_down,
           norm_final_g):
    bsz, seq, d = x.shape
    depth = w_in.shape[0]
    h = x.reshape(bsz * seq, d)
    for l in range(depth):
        h = _layer(h, mem, bsz, seq, norm_mix_g[l], w_in[l], conv_w[l], dn_a_log[l],
                   dn_dt_bias[l], dn_norm_g[l], s5_a_re[l], s5_a_im[l], s5_b_re[l], s5_b_im[l],
                   s5_c_re[l], s5_c_im[l], s5_d[l], s5_log_dt[l], s5_w_glu[l], s5_b_glu[l],
                   w_out[l], norm_x_g[l], norm_mem_g[l], w_xq[l], w_xk[l], w_xv[l], w_xo[l])
        h = _ffn(h, norm_ffn_g[l].reshape(1, d), w_gate[l], w_up[l], w_down[l],
                 norm_final_g.reshape(1, d), tm=FFN_ROWS, final=l == depth - 1)
    return h.reshape(bsz, seq, d)
```

```python
import functools

import jax
import jax.numpy as jnp
from jax import lax
from jax.experimental import pallas as pl
from jax.experimental.pallas import tpu as pltpu

F32 = jnp.float32
BF16 = jnp.bfloat16
EPS = 1e-6
HIGHEST = lax.Precision.HIGHEST

LANES = 128
SUBLANES = 8
DN_CHUNK = 64
T_CHUNK = 16
S_CHUNK = 16
X_HEADS = 4
VMEM_LIMIT = 58 * 1024 * 1024

MIXER_ROWS = 256
S5_BLOCKS = 128
ATTN_ROWS = 1024
FFN_ROWS = 1024


def _fdot(a, b):
    return jnp.dot(a, b, precision=HIGHEST, preferred_element_type=F32)


def _rms(x, g):
    return x * lax.rsqrt(jnp.mean(x * x, axis=-1, keepdims=True) + EPS) * g


def _interleave(gens, stages_per_round=None):
    pending = [(gen, 1 if stages_per_round is None else stages_per_round[i])
               for i, gen in enumerate(gens)]
    while pending:
        for item in list(pending):
            gen, n = item
            try:
                for _ in range(n):
                    next(gen)
            except StopIteration:
                pending.remove(item)


def _params(*sem):
    return pltpu.CompilerParams(dimension_semantics=sem, vmem_limit_bytes=VMEM_LIMIT)


def _in_proj_stages(x_ref, g_ref, w_ref, cw_ref, alog_ref, dtb_ref,
                    q_ref, k_ref, kt_ref, v_ref, z_ref, u_ref, gt_ref,
                    buf_ref, *, tm, dn_w, nh, dh, cw_n):
    xnb = _rms(x_ref[...], g_ref[...]).astype(BF16)
    gw = 2 * dh

    def conv_silu(p, cols):
        outs = []
        for i in range(gw // LANES):
            c0 = cols.start + i * LANES
            s = c0 // LANES
            pc = p[:, i * LANES:(i + 1) * LANES]
            buf_ref[s, SUBLANES:SUBLANES + tm, :] = pc
            acc = cw_ref[cw_n - 1:cw_n, c0:c0 + LANES] * pc
            for j in range(cw_n - 1):
                off = SUBLANES - (cw_n - 1) + j
                acc = acc + cw_ref[j:j + 1, c0:c0 + LANES] * buf_ref[s, off:off + tm, :]
            buf_ref[s, 0:SUBLANES, :] = buf_ref[s, tm:tm + SUBLANES, :]
            outs.append(acc * jax.nn.sigmoid(acc))
        return jnp.concatenate(outs, axis=1)

    def l2n(a, scale):
        parts = []
        for i in range(gw // dh):
            ah = a[:, i * dh:(i + 1) * dh]
            parts.append(ah * (lax.rsqrt(jnp.sum(ah * ah, -1, keepdims=True) + EPS) * scale))
        return jnp.concatenate(parts, axis=1)

    def ep_q(p, c0):
        q_ref[:, c0:c0 + gw] = l2n(conv_silu(p, slice(c0, c0 + gw)), dh ** -0.5)

    def ep_k(p, c0):
        kn = l2n(conv_silu(p, slice(dn_w + c0, dn_w + c0 + gw)), 1.0)
        k_ref[:, c0:c0 + gw] = kn
        kt_ref[c0:c0 + gw, :] = kn.T

    def ep_v(p, c0):
        v_ref[:, c0:c0 + gw] = conv_silu(p, slice(2 * dn_w + c0, 2 * dn_w + c0 + gw))

    def ep_z(p, c0):
        z_ref[:, c0:c0 + gw] = p

    def ep_u(p, c0):
        for j in range(gw // LANES):
            u_ref[c0 // LANES + j] = p[:, j * LANES:(j + 1) * LANES]

    def ep_gates(ab, c0):
        lane = lax.broadcasted_iota(jnp.int32, ab.shape, 1)
        g = -jnp.exp(alog_ref[...]) * jnp.logaddexp(ab + dtb_ref[...], 0.0)
        gates = jnp.where(lane < nh, g, jax.nn.sigmoid(ab))
        gt_ref[...] = gates.T[0:SUBLANES, :]

    groups = []
    for i, ep in enumerate((ep_q, ep_k, ep_v, ep_z, ep_u)):
        groups += [(ep, i * dn_w + c0, gw, c0) for c0 in range(0, dn_w, gw)]
    groups.append((ep_gates, 5 * dn_w, LANES, 0))

    prev = None
    for ep, w0, width, c0 in groups:
        p = jnp.dot(xnb, w_ref[:, w0:w0 + width], preferred_element_type=F32)
        if prev is not None:
            prev[0](prev[1], prev[2])
        prev = (ep, p, c0)
        yield
    prev[0](prev[1], prev[2])


def _delta_stages(q_ref, k_ref, kt_ref, v_ref, z_ref, gt_ref, ng_ref,
                  o_ref, s_ref, u_sc, w_sc, qg_sc, attn_sc, kdt_sc, gtot_sc, wr, rd,
                  *, tb, nh, dh, ch):
    nc = tb // ch
    heads = range(nh)
    hls = [slice(h * dh, (h + 1) * dh) for h in heads]
    mm = functools.partial(jnp.dot, preferred_element_type=F32)

    def phase_b():
        gtot_b = gtot_sc[rd]
        s_all = [s_ref[h] for h in heads]
        for cidx in range(nc):
            rows = slice(cidx * ch, (cidx + 1) * ch)
            sb = [s_all[h].astype(BF16) for h in heads]
            ws = [mm(w_sc[rd, h, rows, :], sb[h]) for h in heads]
            qs = [mm(qg_sc[rd, h, rows, :], sb[h]) for h in heads]
            yield
            vb = [(u_sc[rd, h, rows, :] - ws[h]).astype(BF16) for h in heads]
            av = [mm(attn_sc[rd, h, rows, rows], vb[h]) for h in heads]
            kv = [mm(kdt_sc[rd, h, :, rows], vb[h]) for h in heads]
            yield
            outs = []
            for h in heads:
                a_last = jnp.exp(gtot_b[h:h + 1, cidx * ch:cidx * ch + 1])
                s_all[h] = s_all[h] * a_last + kv[h]
                o = qs[h] + av[h]
                o = o * lax.rsqrt(jnp.mean(o * o, -1, keepdims=True) + EPS)
                zh = z_ref[rows, hls[h]]
                outs.append(o * ng_ref[...] * (zh * jax.nn.sigmoid(zh)))
            o_ref[rows, :] = jnp.concatenate(outs, axis=1)
        for h in heads:
            s_ref[h] = s_all[h]

    r = lax.broadcasted_iota(jnp.int32, (tb, tb), 0)
    c = lax.broadcasted_iota(jnp.int32, (tb, tb), 1)
    same = (r // ch) == (c // ch)
    causal_bd = same & (c <= r)
    strict_bd = same & (c < r)

    def pack(x):
        acc = x[0:ch, :]
        for i in range(1, nc):
            acc = acc + x[i * ch:(i + 1) * ch, :]
        return acc

    def unpack(x):
        return jnp.where(same, jnp.concatenate([x] * nc, axis=0), 0.0).astype(BF16)

    n_sq = max(1, (ch - 1).bit_length() - 1)

    def phase_a():
        gt = gt_ref[...]
        gcum_t = _fdot(gt, jnp.where(same & (r <= c), 1.0, 0.0))
        gtot_t = _fdot(gt, jnp.where(same, 1.0, 0.0))
        yield
        gcum = gcum_t.T
        gates = gt.T
        gcol = [gcum[:, h:h + 1] for h in heads]
        grow = [gcum_t[h:h + 1, :] for h in heads]
        beta = [gates[:, nh + h:nh + h + 1] for h in heads]
        kb = [k_ref[:, hls[h]] * beta[h] for h in heads]
        gram = [mm(jnp.concatenate([kb[h], q_ref[:, hls[h]]], axis=0).astype(BF16),
                   kt_ref[hls[h], :].astype(BF16)) for h in heads]
        yield
        ppk, apk = [], []
        for h in heads:
            decay = jnp.exp(jnp.where(causal_bd, gcol[h] - grow[h], -jnp.inf))
            p_bd = -jnp.where(strict_bd, gram[h][:tb] * decay, 0.0)
            attn_sc[wr, h] = (gram[h][tb:] * decay).astype(BF16)
            pk = pack(p_bd)
            apk.append(pk)
            ppk.append(mm(pk.astype(BF16), p_bd.astype(BF16)))
        yield
        for j in range(1, n_sq + 1):
            res = []
            for h in heads:
                lhs = jnp.concatenate([apk[h], ppk[h]], axis=0) if j < n_sq else apk[h]
                res.append(mm(lhs.astype(BF16), unpack(ppk[h])))
            yield
            for h in heads:
                apk[h] = apk[h] + ppk[h] + res[h][:ch]
                if j < n_sq:
                    ppk[h] = res[h][ch:]
        egc = [jnp.exp(gcol[h]) for h in heads]
        rhs = [jnp.concatenate([v_ref[:, hls[h]] * beta[h], kb[h] * egc[h]], axis=1) for h in heads]
        prod = [mm(unpack(apk[h]), rhs[h].astype(BF16)) for h in heads]
        yield
        for h in heads:
            uw = rhs[h] + prod[h]
            u_sc[wr, h] = uw[:, :dh]
            w_sc[wr, h] = uw[:, dh:].astype(BF16)
            qg_sc[wr, h] = (q_ref[:, hls[h]] * egc[h]).astype(BF16)
            kdt_sc[wr, h] = (kt_ref[hls[h], :] * jnp.exp(gtot_t[h:h + 1, :] - grow[h])).astype(BF16)
        gtot_sc[wr] = gtot_t

    return phase_b(), phase_a()


def _mixer_in_body(x_ref, g_ref, w_ref, cw_ref, alog_ref, dtb_ref, ng_ref,
                   u_ref, o_ref,
                   buf_ref, q_sc, k_sc, kt_sc, v_sc, z_sc, gt_sc,
                   s_ref, u_sc, w_sc, qg_sc, attn_sc, kdt_sc, gtot_sc,
                   *, tb, n_t, nh, dh, cw_n):
    s = pl.program_id(0)
    dn_w = nh * dh

    @pl.when(s == 0)
    def _():
        for ref in (buf_ref, q_sc, k_sc, kt_sc, v_sc, z_sc, gt_sc,
                    s_ref, u_sc, w_sc, qg_sc, attn_sc, kdt_sc, gtot_sc):
            ref[...] = jnp.zeros(ref.shape, ref.dtype)

    @pl.when(s % n_t == 0)
    def _():
        buf_ref[:, 0:SUBLANES, :] = jnp.zeros((3 * dn_w // LANES, SUBLANES, LANES), F32)

    @pl.when((s - 2) % n_t == 0)
    def _():
        s_ref[...] = jnp.zeros(s_ref.shape, F32)

    cur = s % 2
    prv = 1 - cur
    zw = s % 3
    zr = (s + 1) % 3
    proj = _in_proj_stages(
        x_ref, g_ref, w_ref, cw_ref, alog_ref, dtb_ref,
        q_sc.at[cur], k_sc.at[cur], kt_sc.at[cur], v_sc.at[cur], z_sc.at[zw], u_ref, gt_sc.at[cur],
        buf_ref, tm=tb, dn_w=dn_w, nh=nh, dh=dh, cw_n=cw_n)
    phase_b, phase_a = _delta_stages(
        q_sc.at[prv], k_sc.at[prv], kt_sc.at[prv], v_sc.at[prv], z_sc.at[zr], gt_sc.at[prv], ng_ref,
        o_ref, s_ref, u_sc, w_sc, qg_sc, attn_sc, kdt_sc, gtot_sc, cur, prv,
        tb=tb, nh=nh, dh=dh, ch=DN_CHUNK)
    _interleave([proj, phase_b, phase_a])


def _mixer_in(x2, seq, g, w_cat, conv_w, alog, dtb, ng, *, nh, dh, tb):
    nt, d = x2.shape
    dn_w = nh * dh
    n_blk = nt // tb
    cw_n = conv_w.shape[0]
    const = lambda s: (0, 0)
    in_blk = lambda s: (jnp.minimum(s, n_blk - 1), 0)
    out_blk = lambda s: (jnp.maximum(s - 2, 0), 0)
    body = functools.partial(_mixer_in_body, tb=tb, n_t=seq // tb, nh=nh, dh=dh, cw_n=cw_n)
    two = lambda *shape: pltpu.VMEM((2,) + shape, F32)
    return pl.pallas_call(
        body,
        grid=(n_blk + 2,),
        in_specs=[pl.BlockSpec((tb, d), in_blk)]
        + [pl.BlockSpec(a.shape, const) for a in (g, w_cat, conv_w, alog, dtb, ng)],
        out_specs=[pl.BlockSpec((dn_w // LANES, tb, LANES),
                                lambda s: (0, jnp.minimum(s, n_blk - 1), 0)),
                   pl.BlockSpec((tb, dn_w), out_blk)],
        out_shape=[jax.ShapeDtypeStruct((dn_w // LANES, nt, LANES), F32),
                   jax.ShapeDtypeStruct((nt, dn_w), F32)],
        scratch_shapes=[
            pltpu.VMEM((3 * dn_w // LANES, tb + SUBLANES, LANES), F32),
            two(tb, dn_w), two(tb, dn_w), two(dn_w, tb), two(tb, dn_w),
            pltpu.VMEM((3, tb, dn_w), F32),
            two(SUBLANES, tb),
            pltpu.VMEM((nh, dh, dh), F32),
            pltpu.VMEM((2, nh, tb, dh), F32),
            pltpu.VMEM((2, nh, tb, dh), BF16),
            pltpu.VMEM((2, nh, tb, dh), BF16),
            pltpu.VMEM((2, nh, tb, tb), BF16),
            pltpu.VMEM((2, nh, dh, tb), BF16),
            pltpu.VMEM((2, SUBLANES, tb), F32),
        ],
        compiler_params=_params("arbitrary"),
        name="mixer_in",
    )(x2, g, w_cat, conv_w, alog, dtb, ng)


def _ftdot(a, b):
    return lax.dot_general(a.astype(BF16), b.astype(BF16), (((0,), (0,)), ((), ())),
                           preferred_element_type=F32)


def _cmul(xr, xi, yr, yi):
    return xr * yr - xi * yi, xr * yi + xi * yr


def _s5t_prep_body(are_ref, aim_ref, ldt_ref, btr_ref, bti_ref, ctr_ref, cti_ref,
                   are2_ref, aim2_ref, ldt2_ref,
                   w1_ref, co_ref, pin_re, pin_im, pout_re, pout_im, al_re, al_im,
                   *, tc, sc, cg, n_p):
    n_grp = w1_ref.shape[0]
    kw = tc * cg
    a_re = are_ref[...]
    a_im = aim_ref[...]
    dt = jnp.exp(ldt_ref[...])
    lre = a_re * dt
    lim = a_im * dt
    mag = jnp.exp(lre)
    ab = (mag * jnp.cos(lim), mag * jnp.sin(lim))
    nr, ni = ab[0] - 1.0, ab[1]
    den = a_re * a_re + a_im * a_im
    co = ((nr * a_re + ni * a_im) / den, (ni * a_re - nr * a_im) / den)
    inv_mag2 = jnp.exp(-2.0 * lre)
    ai = (ab[0] * inv_mag2, -ab[1] * inv_mag2)
    n_bits = tc.bit_length() - 1
    pos, neg = [ab], [ai]
    for _ in range(n_bits - 1):
        pos.append(_cmul(*pos[-1], *pos[-1]))
        neg.append(_cmul(*neg[-1], *neg[-1]))
    top = pos[0]
    for f in pos[1:]:
        top = _cmul(*top, *f)
    rows = [co, top] + pos + neg
    flat = [part for pair in rows for part in pair]
    cols = []
    for i in range(0, len(flat), SUBLANES):
        blk = flat[i:i + SUBLANES]
        blk = blk + [blk[-1]] * (SUBLANES - len(blk))
        t = jnp.concatenate(blk, axis=0).T
        cols += [t[:, j:j + 1] for j in range(SUBLANES)]
    pair = lambda k: (cols[2 * k], cols[2 * k + 1])
    co_c, top_c = pair(0), pair(1)
    pos_c = [pair(2 + k) for k in range(n_bits)]
    neg_c = [pair(2 + n_bits + k) for k in range(n_bits)]

    slot = lax.broadcasted_iota(jnp.int32, (1, kw), 1) // cg

    def power(factors):
        acc = None
        for k, (fr, fi) in enumerate(factors):
            bit = ((slot >> k) & 1) == 1
            term = (jnp.where(bit, fr, 1.0), jnp.where(bit, fi, 0.0))
            acc = term if acc is None else _cmul(*acc, *term)
        return acc

    place = jnp.where(lax.broadcasted_iota(jnp.int32, (cg, kw), 0)
                      == lax.broadcasted_iota(jnp.int32, (cg, kw), 1) % cg, 1.0, 0.0)
    place = place.astype(BF16)

    def rep(ref, lhs_contract):
        dims = (((lhs_contract,), (0,)), ((), ()))
        out = []
        for g in range(n_grp):
            x = ref[g]
            hi = x.astype(BF16)
            r1 = x - hi.astype(F32)
            mid = r1.astype(BF16)
            lo = (r1 - mid.astype(F32)).astype(BF16)
            out.append(sum(lax.dot_general(part, place, dims, preferred_element_type=F32)
                           for part in (hi, mid, lo)))
        return jnp.concatenate(out, axis=0)

    bt = _cmul(*co_c, rep(btr_ref, 1), rep(bti_ref, 1))
    pb = _cmul(*bt, *power(neg_c))
    qc = _cmul(rep(ctr_ref, 0), rep(cti_ref, 0), *power(pos_c))
    q1 = _cmul(*qc, *pos_c[0])
    bi = _cmul(*pb, *top_c)
    r2 = lax.broadcasted_iota(jnp.int32, (kw, kw), 0) // cg
    c2 = lax.broadcasted_iota(jnp.int32, (kw, kw), 1) // cg
    eye = jnp.where(lax.broadcasted_iota(jnp.int32, (n_p, n_p), 0)
                    == lax.broadcasted_iota(jnp.int32, (n_p, n_p), 1), 1.0, 0.0)
    for g in range(n_grp):
        rs = slice(g * n_p, (g + 1) * n_p)
        m_full = _ftdot(pb[0][rs], qc[0][rs]) - _ftdot(pb[1][rs], qc[1][rs])
        w1_ref[g, :, 0:kw] = jnp.where(r2 <= c2, m_full, 0.0).astype(w1_ref.dtype)
        w1_ref[g, :, kw:kw + n_p] = _ftdot(bi[0][rs], eye).astype(w1_ref.dtype)
        w1_ref[g, :, kw + n_p:kw + 2 * n_p] = _ftdot(bi[1][rs], eye).astype(w1_ref.dtype)
        co_ref[g, 0:n_p, :] = q1[0][rs].astype(co_ref.dtype)
        co_ref[g, n_p:2 * n_p, :] = (-q1[1][rs]).astype(co_ref.dtype)

    _s5t_tables_body(are2_ref, aim2_ref, ldt2_ref, pin_re, pin_im, pout_re, pout_im, al_re, al_im,
                     tc=tc, sc=sc)


def _s5t_prep(a_re, a_im, log_dt, b_re, b_im, c_re, c_im):
    g, p = a_re.shape
    cg = b_re.shape[-1]
    kw = T_CHUNK * cg
    lanes = g * 2 * p
    ldt = jnp.broadcast_to(log_dt[:, None], (g, p))
    row = lambda a: a.reshape(1, g * p)
    spread = lambda a: jnp.broadcast_to(a[:, None, :], (g, 2, p)).reshape(1, lanes)
    tab16 = jax.ShapeDtypeStruct((S_CHUNK, lanes), BF16)
    tab32 = jax.ShapeDtypeStruct((S_CHUNK, lanes), F32)
    one = jax.ShapeDtypeStruct((1, lanes), F32)
    w1, co, *tabs = pl.pallas_call(
        functools.partial(_s5t_prep_body, tc=T_CHUNK, sc=S_CHUNK, cg=cg, n_p=p),
        out_shape=[jax.ShapeDtypeStruct((g, kw, kw + 2 * p), BF16),
                   jax.ShapeDtypeStruct((g, 2 * p, kw), BF16),
                   tab16, tab16, tab32, tab32, one, one],
        compiler_params=pltpu.CompilerParams(vmem_limit_bytes=VMEM_LIMIT),
        name="s5t_prep",
    )(row(a_re), row(a_im), row(ldt),
      b_re, b_im, c_re, c_im, spread(a_re), spread(a_im), spread(ldt))
    return w1, co, tabs


def _s5t_tables_body(are_ref, aim_ref, ldt_ref, pin_re, pin_im, pout_re, pout_im, al_re, al_im,
                     *, tc, sc):
    lre = are_ref[...] * jnp.exp(ldt_ref[...]) * float(tc)
    lim = aim_ref[...] * jnp.exp(ldt_ref[...]) * float(tc)
    lane = lax.broadcasted_iota(jnp.int32, lre.shape, 1)
    sign = jnp.where((lane % LANES) < LANES // 2, -1.0, 1.0)
    n = lax.broadcasted_iota(jnp.int32, (sc, 1), 0).astype(F32) - float(sc // 2)
    m_out = jnp.exp(lre * n)
    m_in = jnp.exp(-(lre * n))
    ang = lim * n
    pin_re[...] = (m_in * jnp.cos(ang)).astype(pin_re.dtype)
    pin_im[...] = (sign * (-(m_in * jnp.sin(ang)))).astype(pin_im.dtype)
    pout_re[...] = m_out * jnp.cos(ang)
    pout_im[...] = sign * (m_out * jnp.sin(ang))
    m_al = jnp.exp(lre * float(sc))
    al_re[...] = m_al * jnp.cos(lim * float(sc))
    al_im[...] = sign * (m_al * jnp.sin(lim * float(sc)))


def _atom_transpose(regs, atom, width):
    regs = list(regs)
    n = len(regs)
    d = n // 2
    while d:
        low = (atom & d) == 0
        for j in range(n):
            if j & d:
                continue
            a, b = regs[j], regs[j + d]
            regs[j] = jnp.where(low, a, pltpu.roll(b, d * width, axis=1))
            regs[j + d] = jnp.where(low, pltpu.roll(a, LANES - d * width, axis=1), b)
        d //= 2
    return regs


def _sublane_transpose(regs, srow):
    regs = list(regs)
    d = SUBLANES // 2
    while d:
        low = (srow & d) == 0
        for j in range(SUBLANES):
            if j & d:
                continue
            a, b = regs[j], regs[j + d]
            regs[j] = jnp.where(low, a, pltpu.roll(b, d, axis=0))
            regs[j + d] = jnp.where(low, pltpu.roll(a, SUBLANES - d, axis=0), b)
        d //= 2
    return regs


def _swap_halves(x):
    tiles = [pltpu.roll(x[:, t * LANES:(t + 1) * LANES], LANES // 2, axis=1)
             for t in range(x.shape[1] // LANES)]
    return jnp.concatenate(tiles, axis=1)


def _s5t_body(u_ref, w1_ref, co_ref, pin_re, pin_im, pout_re, pout_im, al_re, al_im,
              d_ref, wglu_ref, bglu_ref, y_ref,
              upk_ref, yloc_ref, sin_ref, xs_ref, carry_ref, udi_ref, *, rc, tc, sc, n_grp, cg):
    t = pl.program_id(1)
    kw = tc * cg
    n_q = u_ref.shape[0]
    atoms = LANES // cg

    @pl.when(t == 0)
    def _():
        carry_ref[...] = jnp.zeros(carry_ref.shape, F32)
        xs_ref[:, 0:SUBLANES, :] = jnp.zeros((n_grp, SUBLANES, LANES), F32)

    atom = lax.broadcasted_iota(jnp.int32, (rc, LANES), 1) // cg

    n_a = tc // atoms

    assert atoms == SUBLANES
    srow = lax.broadcasted_iota(jnp.int32, (SUBLANES, LANES), 0)

    def rows_to_planes(q, a):
        pieces = [[] for _ in range(atoms)]
        for nb in range(rc // SUBLANES):
            nat = [u_ref[q, (nb * SUBLANES + m) * tc + a * atoms:
                         (nb * SUBLANES + m) * tc + (a + 1) * atoms, :] for m in range(SUBLANES)]
            for k, piece in enumerate(_sublane_transpose(nat, srow)):
                pieces[k].append(piece)
        return [jnp.concatenate(p, axis=0) for p in pieces]

    def planes_to_rows(q, a, planes):
        for nb in range(rc // SUBLANES):
            regs = [p[nb * SUBLANES:(nb + 1) * SUBLANES, :] for p in planes]
            for m, nat in enumerate(_sublane_transpose(regs, srow)):
                r0 = (nb * SUBLANES + m) * tc + a * atoms
                y_ref[q, r0:r0 + atoms, :] = nat

    def pack():
        for q in range(n_q):
            for a in range(n_a):
                planes = rows_to_planes(q, a)
                for k in range(atoms):
                    udi_ref[q, a * atoms + k] = planes[k]
                for r, tile in enumerate(_atom_transpose(planes, atom, cg)):
                    upk_ref[q * atoms + r, :, a * LANES:(a + 1) * LANES] = tile.astype(BF16)
                yield

    per_stage = atoms // n_a

    def local():
        for _ in range(n_a):
            yield
        for g in range(n_grp):
            res = jnp.dot(upk_ref[g], w1_ref[g], preferred_element_type=F32)
            yloc_ref[g] = res[:, :kw]
            sin_ref[g] = res[:, kw:]
            if g % per_stage == per_stage - 1:
                yield

    _interleave([pack(), local()])

    ri = lax.broadcasted_iota(jnp.int32, (sc, sc), 0)
    ci = lax.broadcasted_iota(jnp.int32, (sc, sc), 1)
    tri = jnp.where(ci <= ri, 1.0, 0.0).astype(BF16)
    n_c2 = rc // sc
    ss = [jnp.concatenate([sin_ref[g, c2 * sc:(c2 + 1) * sc, :] for g in range(n_grp)], axis=1)
          for c2 in range(n_c2)]
    ss_sw = [_swap_halves(s) for s in ss]
    xin = [pin_re[...] * ss[c2].astype(BF16) + pin_im[...] * ss_sw[c2].astype(BF16)
           for c2 in range(n_c2)]
    psum = [jnp.dot(tri, x, preferred_element_type=F32) for x in xin]
    psum_sw = [_swap_halves(p) for p in psum]
    carry = carry_ref[...]
    carry_sw = _swap_halves(carry)
    carries = []
    for c2 in range(rc // sc):
        carries.append((carry, carry_sw))
        last = psum[c2][sc - 1:sc, :] + carry
        last_sw = psum_sw[c2][sc - 1:sc, :] + carry_sw
        carry = al_re[...] * last + al_im[...] * last_sw
        carry_sw = al_re[...] * last_sw - al_im[...] * last
    carry_ref[...] = carry
    for c2 in range(rc // sc):
        x = (pout_re[...] * (psum[c2] + carries[c2][0])
             + pout_im[...] * (psum_sw[c2] + carries[c2][1]))
        for g in range(n_grp):
            xs_ref[g, SUBLANES + c2 * sc:SUBLANES + (c2 + 1) * sc, :] = x[:, g * LANES:(g + 1) * LANES]

    for g in range(n_grp):
        xprev = xs_ref[g, SUBLANES - 1:SUBLANES - 1 + rc, :].astype(BF16)
        yloc_ref[g] = yloc_ref[g] + jnp.dot(xprev, co_ref[g], preferred_element_type=F32)
    xs_ref[:, SUBLANES - 1:SUBLANES, :] = xs_ref[:, SUBLANES - 1 + rc:SUBLANES + rc, :]

    wglu = wglu_ref[...].astype(BF16)

    def unpack(a):
        for _ in range(a * n_q):
            yield
        planes = [[None] * n_q for _ in range(atoms)]
        for q in range(n_q):
            tiles = [yloc_ref[q * atoms + r, :, a * LANES:(a + 1) * LANES] for r in range(atoms)]
            for k, tile in enumerate(_atom_transpose(tiles, atom, cg)):
                planes[k][q] = tile
            yield
        outs = []
        for k in range(atoms):
            i = a * atoms + k
            uu = jnp.concatenate([udi_ref[q, i] for q in range(n_q)], axis=1)
            y = jax.nn.gelu(jnp.concatenate(planes[k], axis=1) + d_ref[...] * uu)
            gate = jax.nn.sigmoid(jnp.dot(y.astype(BF16), wglu, preferred_element_type=F32)
                                  + bglu_ref[...])
            outs.append(y * gate)
            if k % 2 == 1:
                yield
        for q in range(n_q):
            planes_to_rows(q, a, [o[:, q * LANES:(q + 1) * LANES] for o in outs])
            yield

    _interleave([unpack(a) for a in range(n_a)])


def _s5t(u4, w1, co, tabs, d, wglu, bglu, bsz, seq, *, rc):
    n_q, nt, _ = u4.shape
    n_grp, kw, wcols = w1.shape
    n_state = (wcols - kw) // 2
    cg = kw // T_CHUNK
    rows = rc * T_CHUNK
    n_t = seq // rows
    blk = pl.BlockSpec((n_q, rows, LANES), lambda b, t: (0, b * n_t + t, 0))
    c2 = lambda b, t: (0, 0)
    c3 = lambda b, t: (0, 0, 0)
    whole = lambda a: pl.BlockSpec(a.shape, c3 if a.ndim == 3 else c2)
    body = functools.partial(_s5t_body, rc=rc, tc=T_CHUNK, sc=S_CHUNK, n_grp=n_grp, cg=cg)
    return pl.pallas_call(
        body,
        grid=(bsz, n_t),
        in_specs=[blk] + [whole(a) for a in (w1, co, *tabs, d, wglu, bglu)],
        out_specs=blk,
        out_shape=jax.ShapeDtypeStruct((n_q, nt, LANES), F32),
        scratch_shapes=[
            pltpu.VMEM((n_grp, rc, kw), BF16),
            pltpu.VMEM((n_grp, rc, kw), F32),
            pltpu.VMEM((n_grp, rc, 2 * n_state), F32),
            pltpu.VMEM((n_grp, rc + SUBLANES, 2 * n_state), F32),
            pltpu.VMEM((1, n_grp * 2 * n_state), F32),
            pltpu.VMEM((n_q, T_CHUNK, rc, LANES), F32),
        ],
        compiler_params=_params("arbitrary", "arbitrary"),
        name="s5t",
    )(u4, w1, co, *tabs, d, wglu, bglu)


def _mem_kv_body(m_ref, g_ref, wk_ref, wv_ref, wq_ref, wo_ref, qk_ref, vw_ref, w_sc, *, xh):
    @pl.when(pl.program_id(0) == 0)
    def _():
        for i, ref in enumerate((wk_ref, wv_ref, wq_ref, wo_ref)):
            w_sc[i] = ref[...].astype(BF16)

    _, n_mem, d = m_ref.shape
    xd = d // xh
    scale = xd ** -0.5
    mn = _rms(m_ref[0], g_ref[...]).astype(BF16)
    k = jnp.dot(mn, w_sc[0], preferred_element_type=F32).astype(BF16)
    v = jnp.dot(mn, w_sc[1], preferred_element_type=F32).astype(BF16)
    nt_dims = (((1,), (1,)), ((), ()))
    for h in range(xh):
        hl = slice(h * xd, (h + 1) * xd)
        ml = slice(h * n_mem, (h + 1) * n_mem)
        qk = lax.dot_general(w_sc[2, :, hl], k[:, hl], nt_dims, preferred_element_type=F32)
        qk_ref[0, :, ml] = (qk * scale).astype(BF16)
        vw_ref[0, ml, :] = jnp.dot(v[:, hl], w_sc[3, hl, :],
                                   preferred_element_type=F32).astype(BF16)


def _mem_kv(mem, g, wk, wv, wq, wo):
    bsz, n_mem, d = mem.shape
    hm = X_HEADS * n_mem
    const = lambda b: (0, 0)
    weight = pl.BlockSpec((d, d), const, pipeline_mode=pl.Buffered(1))
    return pl.pallas_call(
        functools.partial(_mem_kv_body, xh=X_HEADS),
        grid=(bsz,),
        in_specs=[pl.BlockSpec((1, n_mem, d), lambda b: (b, 0, 0)), pl.BlockSpec((1, d), const),
                  weight, weight, weight, weight],
        out_specs=[pl.BlockSpec((1, d, hm), lambda b: (b, 0, 0)),
                   pl.BlockSpec((1, hm, d), lambda b: (b, 0, 0))],
        out_shape=[
            jax.ShapeDtypeStruct((bsz, d, hm), BF16),
            jax.ShapeDtypeStruct((bsz, hm, d), BF16),
        ],
        scratch_shapes=[pltpu.VMEM((4, d, d), BF16)],
        compiler_params=_params("arbitrary"),
        name="mem_kv",
    )(mem, g, wk, wv, wq, wo)


def _mix_attn_body(x_ref, o_ref, y_ref, wo_ref, gx_ref, qk_ref, vw_ref, h_ref, *, xh, n_sub):
    dn_w = o_ref.shape[1]
    n_mem = qk_ref.shape[2] // xh
    sub = x_ref.shape[0] // n_sub
    mm = functools.partial(jnp.dot, preferred_element_type=F32)
    wo = wo_ref[...].astype(BF16)

    def rows_gen(r0):
        rows = slice(r0, r0 + sub)
        y = jnp.concatenate([y_ref[q, rows, :] for q in range(y_ref.shape[0])], axis=1)
        mix = mm(o_ref[rows, :].astype(BF16), wo[0:dn_w, :]) + mm(y.astype(BF16), wo[dn_w:, :])
        yield
        h1 = x_ref[rows, :] + mix
        s = mm(_rms(h1, gx_ref[...]).astype(BF16), qk_ref[0])
        yield
        parts = []
        for h in range(xh):
            sh = s[:, h * n_mem:(h + 1) * n_mem]
            e = jnp.exp(sh - jnp.max(sh, axis=-1, keepdims=True))
            parts.append((e / jnp.sum(e, axis=-1, keepdims=True)).astype(BF16))
        att = mm(jnp.concatenate(parts, axis=1), vw_ref[0])
        yield
        h_ref[rows, :] = h1 + att

    _interleave([rows_gen(i * sub) for i in range(n_sub)])


def _mix_attn(x2, o, y, w_out, gx, qk, vw, bsz, seq, *, tm):
    nt, d = x2.shape
    dn_w = o.shape[1]
    n_t = seq // tm
    row = lambda b, t: (b * n_t + t, 0)
    c2 = lambda b, t: (0, 0)
    per_batch = lambda a: pl.BlockSpec((1,) + a.shape[1:], lambda b, t: (b, 0, 0))
    body = functools.partial(_mix_attn_body, xh=X_HEADS, n_sub=2)
    return pl.pallas_call(
        body,
        grid=(bsz, n_t),
        in_specs=[
            pl.BlockSpec((tm, d), row),
            pl.BlockSpec((tm, dn_w), row),
            pl.BlockSpec((y.shape[0], tm, LANES), lambda b, t: (0, b * n_t + t, 0)),
            pl.BlockSpec(w_out.shape, c2, pipeline_mode=pl.Buffered(1)),
            pl.BlockSpec((1, d), c2),
            per_batch(qk),
            per_batch(vw),
        ],
        out_specs=pl.BlockSpec((tm, d), row),
        out_shape=jax.ShapeDtypeStruct((nt, d), F32),
        compiler_params=_params("arbitrary", "arbitrary"),
        name="mix_attn",
    )(x2, o, y, w_out, gx, qk, vw)


def _ffn_body(h_ref, gf_ref, wg_ref, wu_ref, wd_ref, gl_ref, out_ref, *, final, fc):
    h = h_ref[...]
    hn = _rms(h, gf_ref[...]).astype(BF16)
    dff = wg_ref.shape[1]

    def down(gate, up, c0):
        act = (gate * jax.nn.sigmoid(gate) * up).astype(BF16)
        return jnp.dot(act, wd_ref[c0:c0 + fc, :].astype(BF16), preferred_element_type=F32)

    h3 = h
    prev = None
    for c0 in range(0, dff, fc):
        gate = jnp.dot(hn, wg_ref[:, c0:c0 + fc].astype(BF16), preferred_element_type=F32)
        up = jnp.dot(hn, wu_ref[:, c0:c0 + fc].astype(BF16), preferred_element_type=F32)
        if prev is not None:
            h3 = h3 + down(*prev)
        prev = (gate, up, c0)
    h3 = h3 + down(*prev)
    out_ref[...] = _rms(h3, gl_ref[...]) if final else h3


def _ffn(h, gf, wg, wu, wd, gl, *, tm, final):
    nt, d = h.shape
    dff = wg.shape[1]
    c2 = lambda i: (0, 0)
    row = lambda i: (i, 0)
    return pl.pallas_call(
        functools.partial(_ffn_body, final=final, fc=2 * LANES),
        grid=(nt // tm,),
        in_specs=[
            pl.BlockSpec((tm, d), row),
            pl.BlockSpec((1, d), c2),
            pl.BlockSpec((d, dff), c2, pipeline_mode=pl.Buffered(1)),
            pl.BlockSpec((d, dff), c2, pipeline_mode=pl.Buffered(1)),
            pl.BlockSpec((dff, d), c2, pipeline_mode=pl.Buffered(1)),
            pl.BlockSpec((1, d), c2),
        ],
        out_specs=pl.BlockSpec((tm, d), row),
        out_shape=jax.ShapeDtypeStruct((nt, d), F32),
        compiler_params=_params("arbitrary"),
        name="ffn",
    )(h, gf, wg, wu, wd, gl)


def _layer(h2, mem, bsz, seq, norm_mix_g, w_in, conv_w, dn_a_log, dn_dt_bias, dn_norm_g,
           s5_a_re, s5_a_im, s5_b_re, s5_b_im, s5_c_re, s5_c_im, s5_d, s5_log_dt,
           s5_w_glu, s5_b_glu, w_out, norm_x_g, norm_mem_g, w_xq, w_xk, w_xv, w_xo):
    d = h2.shape[1]
    nh = dn_a_log.shape[0]
    dh = dn_norm_g.shape[0]
    dn_w = nh * dh
    s5_w = s5_a_re.shape[0] * s5_b_re.shape[-1]

    off_a = 4 * dn_w
    off_u = off_a + 2 * nh
    w_ab = jnp.pad(w_in[:, off_a:off_u].astype(BF16), ((0, 0), (0, LANES - 2 * nh)))
    w_cat = jnp.concatenate([w_in[:, :off_a].astype(BF16), w_in[:, off_u:].astype(BF16), w_ab],
                            axis=1)
    alog = jnp.pad(dn_a_log, (0, LANES - nh)).reshape(1, LANES)
    dtb = jnp.pad(dn_dt_bias, (0, LANES - nh)).reshape(1, LANES)

    u, o = _mixer_in(h2, seq, norm_mix_g.reshape(1, d), w_cat, conv_w, alog, dtb,
                     dn_norm_g.reshape(1, dh), nh=nh, dh=dh, tb=MIXER_ROWS)

    w1, co, tabs = _s5t_prep(s5_a_re, s5_a_im, s5_log_dt, s5_b_re, s5_b_im, s5_c_re, s5_c_im)
    y = _s5t(u, w1, co, tabs, s5_d.reshape(1, s5_w), s5_w_glu, s5_b_glu.reshape(1, s5_w),
             bsz, seq, rc=S5_BLOCKS)

    qk_mem, vw_mem = _mem_kv(mem, norm_mem_g.reshape(1, d), w_xk, w_xv, w_xq, w_xo)
    return _mix_attn(h2, o, y, w_out, norm_x_g.reshape(1, d), qk_mem, vw_mem, bsz, seq,
                     tm=ATTN_ROWS)


def kernel(x, mem, norm_mix_g, w_in, conv_w, dn_a_log, dn_dt_bias, dn_norm_g, s5_a_re, s5_a_im,
           s5_b_re, s5_b_im, s5_c_re, s5_c_im, s5_d, s5_log_dt, s5_w_glu, s5_b_glu, w_out,
           norm_x_g, norm_mem_g, w_xq, w_xk, w_xv, w_xo, norm_ffn_g, w_gate, w_up, w_down,
           norm_final_g):
    bsz, seq, d = x.shape
    depth = w_in.shape[0]
    h = x.reshape(bsz * seq, d)
    for l in range(depth):
        h = _layer(h, mem, bsz, seq, norm_mix_g[l], w_in[l], conv_w[l], dn_a_log[l],
                   dn_dt_bias[l], dn_norm_g[l], s5_a_re[l], s5_a_im[l], s5_b_re[l], s5_b_im[l],
                   s5_c_re[l], s5_c_im[l], s5_d[l], s5_log_dt[l], s5_w_glu[l], s5_b_glu[l],
                   w_out[l], norm_x_g[l], norm_mem_g[l], w_xq[l], w_xk[l], w_xv[l], w_xo[l])
        h = _ffn(h, norm_ffn_g[l].reshape(1, d), w_gate[l], w_up[l], w_down[l],
                 norm_final_g.reshape(1, d), tm=FFN_ROWS, final=l == depth - 1)
    return h.reshape(bsz, seq, d)
```

```python
import functools

import jax
import jax.numpy as jnp
from jax import lax
from jax.experimental import pallas as pl
from jax.experimental.pallas import tpu as pltpu

F32 = jnp.float32
BF16 = jnp.bfloat16
EPS = 1e-6
HIGHEST = lax.Precision.HIGHEST

LANES = 128
SUBLANES = 8
DN_CHUNK = 64
T_CHUNK = 16
S_CHUNK = 16
X_HEADS = 4
VMEM_LIMIT = 58 * 1024 * 1024
MIXER_VMEM = 32 * 1024 * 1024

MIXER_ROWS = 256
S5_BLOCKS = 128
ATTN_ROWS = 1024
FFN_ROWS = 1024


def _fdot(a, b):
    return jnp.dot(a, b, precision=HIGHEST, preferred_element_type=F32)


def _rms(x, g):
    return x * lax.rsqrt(jnp.mean(x * x, axis=-1, keepdims=True) + EPS) * g


def _interleave(gens, stages_per_round=None):
    pending = [(gen, 1 if stages_per_round is None else stages_per_round[i])
               for i, gen in enumerate(gens)]
    while pending:
        for item in list(pending):
            gen, n = item
            try:
                for _ in range(n):
                    next(gen)
            except StopIteration:
                pending.remove(item)


def _params(*sem, vmem=VMEM_LIMIT):
    return pltpu.CompilerParams(dimension_semantics=sem, vmem_limit_bytes=vmem)


def _in_proj_stages(x_ref, g_ref, w_ref, cw_ref, alog_ref, dtb_ref,
                    q_ref, k_ref, kt_ref, v_ref, z_ref, u_ref, gt_ref,
                    buf_ref, *, tm, dn_w, nh, dh, cw_n):
    xnb = _rms(x_ref[...], g_ref[...]).astype(BF16)
    gw = 2 * dh

    def conv_silu(p, cols):
        outs = []
        for i in range(gw // LANES):
            c0 = cols.start + i * LANES
            s = c0 // LANES
            pc = p[:, i * LANES:(i + 1) * LANES]
            buf_ref[s, SUBLANES:SUBLANES + tm, :] = pc
            acc = cw_ref[cw_n - 1:cw_n, c0:c0 + LANES] * pc
            for j in range(cw_n - 1):
                off = SUBLANES - (cw_n - 1) + j
                acc = acc + cw_ref[j:j + 1, c0:c0 + LANES] * buf_ref[s, off:off + tm, :]
            buf_ref[s, 0:SUBLANES, :] = buf_ref[s, tm:tm + SUBLANES, :]
            outs.append(acc * jax.nn.sigmoid(acc))
        return jnp.concatenate(outs, axis=1)

    def l2n(a, scale):
        parts = []
        for i in range(gw // dh):
            ah = a[:, i * dh:(i + 1) * dh]
            parts.append(ah * (lax.rsqrt(jnp.sum(ah * ah, -1, keepdims=True) + EPS) * scale))
        return jnp.concatenate(parts, axis=1)

    def ep_q(p, c0):
        q_ref[:, c0:c0 + gw] = l2n(conv_silu(p, slice(c0, c0 + gw)), dh ** -0.5)

    def ep_k(p, c0):
        kn = l2n(conv_silu(p, slice(dn_w + c0, dn_w + c0 + gw)), 1.0)
        k_ref[:, c0:c0 + gw] = kn
        kt_ref[c0:c0 + gw, :] = kn.T

    def ep_v(p, c0):
        v_ref[:, c0:c0 + gw] = conv_silu(p, slice(2 * dn_w + c0, 2 * dn_w + c0 + gw))

    def ep_z(p, c0):
        z_ref[:, c0:c0 + gw] = p

    def ep_u(p, c0):
        for j in range(gw // LANES):
            u_ref[c0 // LANES + j] = p[:, j * LANES:(j + 1) * LANES]

    def ep_gates(ab, c0):
        lane = lax.broadcasted_iota(jnp.int32, ab.shape, 1)
        g = -jnp.exp(alog_ref[...]) * jnp.logaddexp(ab + dtb_ref[...], 0.0)
        gates = jnp.where(lane < nh, g, jax.nn.sigmoid(ab))
        gt_ref[...] = gates.T[0:SUBLANES, :]

    groups = []
    for i, ep in enumerate((ep_q, ep_k, ep_v, ep_z, ep_u)):
        groups += [(ep, i * dn_w + c0, gw, c0) for c0 in range(0, dn_w, gw)]
    groups.append((ep_gates, 5 * dn_w, LANES, 0))

    prev = None
    for ep, w0, width, c0 in groups:
        p = jnp.dot(xnb, w_ref[:, w0:w0 + width], preferred_element_type=F32)
        if prev is not None:
            prev[0](prev[1], prev[2])
        prev = (ep, p, c0)
        yield
    prev[0](prev[1], prev[2])


def _delta_stages(q_ref, k_ref, kt_ref, v_ref, z_ref, gt_ref, ng_ref,
                  o_ref, s_ref, u_sc, w_sc, qg_sc, attn_sc, kdt_sc, gtot_sc, wr, rd,
                  *, tb, nh, dh, ch):
    nc = tb // ch
    heads = range(nh)
    hls = [slice(h * dh, (h + 1) * dh) for h in heads]
    mm = functools.partial(jnp.dot, preferred_element_type=F32)

    def phase_b():
        gtot_b = gtot_sc[rd]
        s_all = [s_ref[h] for h in heads]
        for cidx in range(nc):
            rows = slice(cidx * ch, (cidx + 1) * ch)
            sb = [s_all[h].astype(BF16) for h in heads]
            ws = [mm(w_sc[rd, h, rows, :], sb[h]) for h in heads]
            qs = [mm(qg_sc[rd, h, rows, :], sb[h]) for h in heads]
            yield
            vb = [(u_sc[rd, h, rows, :] - ws[h]).astype(BF16) for h in heads]
            av = [mm(attn_sc[rd, h, rows, rows], vb[h]) for h in heads]
            kv = [mm(kdt_sc[rd, h, :, rows], vb[h]) for h in heads]
            yield
            outs = []
            for h in heads:
                a_last = jnp.exp(gtot_b[h:h + 1, cidx * ch:cidx * ch + 1])
                s_all[h] = s_all[h] * a_last + kv[h]
                o = qs[h] + av[h]
                o = o * lax.rsqrt(jnp.mean(o * o, -1, keepdims=True) + EPS)
                zh = z_ref[rows, hls[h]]
                outs.append(o * ng_ref[...] * (zh * jax.nn.sigmoid(zh)))
            o_ref[rows, :] = jnp.concatenate(outs, axis=1)
        for h in heads:
            s_ref[h] = s_all[h]

    r = lax.broadcasted_iota(jnp.int32, (tb, tb), 0)
    c = lax.broadcasted_iota(jnp.int32, (tb, tb), 1)
    same = (r // ch) == (c // ch)
    causal_bd = same & (c <= r)
    strict_bd = same & (c < r)
    gt = gt_ref[...]
    gcum_t = _fdot(gt, jnp.where(same & (r <= c), 1.0, 0.0))
    gtot_t = _fdot(gt, jnp.where(same, 1.0, 0.0))
    gcum = gcum_t.T
    gates = gt.T

    def pack(x):
        acc = x[0:ch, :]
        for i in range(1, nc):
            acc = acc + x[i * ch:(i + 1) * ch, :]
        return acc

    def unpack(x):
        return jnp.where(same, jnp.concatenate([x] * nc, axis=0), 0.0).astype(BF16)

    n_sq = max(1, (ch - 1).bit_length() - 1)

    def phase_a():
        gcol = [gcum[:, h:h + 1] for h in heads]
        grow = [gcum_t[h:h + 1, :] for h in heads]
        beta = [gates[:, nh + h:nh + h + 1] for h in heads]
        kb = [k_ref[:, hls[h]] * beta[h] for h in heads]
        gram = [mm(jnp.concatenate([kb[h], q_ref[:, hls[h]]], axis=0).astype(BF16),
                   kt_ref[hls[h], :].astype(BF16)) for h in heads]
        yield
        ppk, apk = [], []
        for h in heads:
            decay = jnp.exp(jnp.where(causal_bd, gcol[h] - grow[h], -jnp.inf))
            p_bd = -jnp.where(strict_bd, gram[h][:tb] * decay, 0.0)
            attn_sc[wr, h] = (gram[h][tb:] * decay).astype(BF16)
            pk = pack(p_bd)
            apk.append(pk)
            ppk.append(mm(pk.astype(BF16), p_bd.astype(BF16)))
        yield
        for j in range(1, n_sq + 1):
            res = []
            for h in heads:
                lhs = jnp.concatenate([apk[h], ppk[h]], axis=0) if j < n_sq else apk[h]
                res.append(mm(lhs.astype(BF16), unpack(ppk[h])))
            yield
            for h in heads:
                apk[h] = apk[h] + ppk[h] + res[h][:ch]
                if j < n_sq:
                    ppk[h] = res[h][ch:]
        egc = [jnp.exp(gcol[h]) for h in heads]
        rhs = [jnp.concatenate([v_ref[:, hls[h]] * beta[h], kb[h] * egc[h]], axis=1) for h in heads]
        prod = [mm(unpack(apk[h]), rhs[h].astype(BF16)) for h in heads]
        yield
        for h in heads:
            uw = rhs[h] + prod[h]
            u_sc[wr, h] = uw[:, :dh]
            w_sc[wr, h] = uw[:, dh:].astype(BF16)
            qg_sc[wr, h] = (q_ref[:, hls[h]] * egc[h]).astype(BF16)
            kdt_sc[wr, h] = (kt_ref[hls[h], :] * jnp.exp(gtot_t[h:h + 1, :] - grow[h])).astype(BF16)
        gtot_sc[wr] = gtot_t

    return phase_b(), phase_a()


def _mixer_in_body(x_ref, g_ref, w_ref, cw_ref, alog_ref, dtb_ref, ng_ref,
                   u_ref, o_ref,
                   buf_ref, q_sc, k_sc, kt_sc, v_sc, z_sc, gt_sc,
                   s_ref, u_sc, w_sc, qg_sc, attn_sc, kdt_sc, gtot_sc,
                   *, tb, n_t, nh, dh, cw_n):
    s = pl.program_id(0)
    dn_w = nh * dh

    @pl.when(s == 0)
    def _():
        for ref in (buf_ref, q_sc, k_sc, kt_sc, v_sc, z_sc, gt_sc,
                    s_ref, u_sc, w_sc, qg_sc, attn_sc, kdt_sc, gtot_sc):
            ref[...] = jnp.zeros(ref.shape, ref.dtype)

    @pl.when(s % n_t == 0)
    def _():
        buf_ref[:, 0:SUBLANES, :] = jnp.zeros((3 * dn_w // LANES, SUBLANES, LANES), F32)

    @pl.when((s - 2) % n_t == 0)
    def _():
        s_ref[...] = jnp.zeros(s_ref.shape, F32)

    cur = s % 2
    prv = 1 - cur
    zw = s % 3
    zr = (s + 1) % 3
    proj = _in_proj_stages(
        x_ref, g_ref, w_ref, cw_ref, alog_ref, dtb_ref,
        q_sc.at[cur], k_sc.at[cur], kt_sc.at[cur], v_sc.at[cur], z_sc.at[zw], u_ref, gt_sc.at[cur],
        buf_ref, tm=tb, dn_w=dn_w, nh=nh, dh=dh, cw_n=cw_n)
    phase_b, phase_a = _delta_stages(
        q_sc.at[prv], k_sc.at[prv], kt_sc.at[prv], v_sc.at[prv], z_sc.at[zr], gt_sc.at[prv], ng_ref,
        o_ref, s_ref, u_sc, w_sc, qg_sc, attn_sc, kdt_sc, gtot_sc, cur, prv,
        tb=tb, nh=nh, dh=dh, ch=DN_CHUNK)
    _interleave([proj, phase_b, phase_a])


def _mixer_in(x2, seq, g, w_cat, conv_w, alog, dtb, ng, *, nh, dh, tb):
    nt, d = x2.shape
    dn_w = nh * dh
    n_blk = nt // tb
    cw_n = conv_w.shape[0]
    const = lambda s: (0, 0)
    in_blk = lambda s: (jnp.minimum(s, n_blk - 1), 0)
    out_blk = lambda s: (jnp.maximum(s - 2, 0), 0)
    body = functools.partial(_mixer_in_body, tb=tb, n_t=seq // tb, nh=nh, dh=dh, cw_n=cw_n)
    two = lambda *shape: pltpu.VMEM((2,) + shape, F32)
    return pl.pallas_call(
        body,
        grid=(n_blk + 2,),
        in_specs=[pl.BlockSpec((tb, d), in_blk)]
        + [pl.BlockSpec(a.shape, const) for a in (g, w_cat, conv_w, alog, dtb, ng)],
        out_specs=[pl.BlockSpec((dn_w // LANES, tb, LANES),
                                lambda s: (0, jnp.minimum(s, n_blk - 1), 0)),
                   pl.BlockSpec((tb, dn_w), out_blk)],
        out_shape=[jax.ShapeDtypeStruct((dn_w // LANES, nt, LANES), F32),
                   jax.ShapeDtypeStruct((nt, dn_w), F32)],
        scratch_shapes=[
            pltpu.VMEM((3 * dn_w // LANES, tb + SUBLANES, LANES), F32),
            two(tb, dn_w), two(tb, dn_w), two(dn_w, tb), two(tb, dn_w),
            pltpu.VMEM((3, tb, dn_w), F32),
            two(SUBLANES, tb),
            pltpu.VMEM((nh, dh, dh), F32),
            pltpu.VMEM((2, nh, tb, dh), F32),
            pltpu.VMEM((2, nh, tb, dh), BF16),
            pltpu.VMEM((2, nh, tb, dh), BF16),
            pltpu.VMEM((2, nh, tb, tb), BF16),
            pltpu.VMEM((2, nh, dh, tb), BF16),
            pltpu.VMEM((2, SUBLANES, tb), F32),
        ],
        compiler_params=_params("arbitrary", vmem=MIXER_VMEM),
        name="mixer_in",
    )(x2, g, w_cat, conv_w, alog, dtb, ng)


def _ftdot(a, b):
    return lax.dot_general(a.astype(BF16), b.astype(BF16), (((0,), (0,)), ((), ())),
                           preferred_element_type=F32)


def _cmul(xr, xi, yr, yi):
    return xr * yr - xi * yi, xr * yi + xi * yr


def _s5t_prep_body(are_ref, aim_ref, ldt_ref, btr_ref, bti_ref, ctr_ref, cti_ref,
                   are2_ref, aim2_ref, ldt2_ref,
                   w1_ref, co_ref, pin_re, pin_im, pout_re, pout_im, al_re, al_im,
                   *, tc, sc, cg, n_p):
    n_grp = w1_ref.shape[0]
    kw = tc * cg
    a_re = are_ref[...]
    a_im = aim_ref[...]
    dt = jnp.exp(ldt_ref[...])
    lre = a_re * dt
    lim = a_im * dt
    mag = jnp.exp(lre)
    ab = (mag * jnp.cos(lim), mag * jnp.sin(lim))
    nr, ni = ab[0] - 1.0, ab[1]
    den = a_re * a_re + a_im * a_im
    co = ((nr * a_re + ni * a_im) / den, (ni * a_re - nr * a_im) / den)
    inv_mag2 = jnp.exp(-2.0 * lre)
    ai = (ab[0] * inv_mag2, -ab[1] * inv_mag2)
    n_bits = tc.bit_length() - 1
    pos, neg = [ab], [ai]
    for _ in range(n_bits - 1):
        pos.append(_cmul(*pos[-1], *pos[-1]))
        neg.append(_cmul(*neg[-1], *neg[-1]))
    top = pos[0]
    for f in pos[1:]:
        top = _cmul(*top, *f)
    rows = [co, top] + pos + neg
    flat = [part for pair in rows for part in pair]
    cols = []
    for i in range(0, len(flat), SUBLANES):
        blk = flat[i:i + SUBLANES]
        blk = blk + [blk[-1]] * (SUBLANES - len(blk))
        t = jnp.concatenate(blk, axis=0).T
        cols += [t[:, j:j + 1] for j in range(SUBLANES)]
    pair = lambda k: (cols[2 * k], cols[2 * k + 1])
    co_c, top_c = pair(0), pair(1)
    pos_c = [pair(2 + k) for k in range(n_bits)]
    neg_c = [pair(2 + n_bits + k) for k in range(n_bits)]

    slot = lax.broadcasted_iota(jnp.int32, (1, kw), 1) // cg

    def power(factors):
        acc = None
        for k, (fr, fi) in enumerate(factors):
            bit = ((slot >> k) & 1) == 1
            term = (jnp.where(bit, fr, 1.0), jnp.where(bit, fi, 0.0))
            acc = term if acc is None else _cmul(*acc, *term)
        return acc

    place = jnp.where(lax.broadcasted_iota(jnp.int32, (cg, kw), 0)
                      == lax.broadcasted_iota(jnp.int32, (cg, kw), 1) % cg, 1.0, 0.0)
    place = place.astype(BF16)

    def rep(ref, lhs_contract):
        dims = (((lhs_contract,), (0,)), ((), ()))
        out = []
        for g in range(n_grp):
            x = ref[g]
            hi = x.astype(BF16)
            r1 = x - hi.astype(F32)
            mid = r1.astype(BF16)
            lo = (r1 - mid.astype(F32)).astype(BF16)
            out.append(sum(lax.dot_general(part, place, dims, preferred_element_type=F32)
                           for part in (hi, mid, lo)))
        return jnp.concatenate(out, axis=0)

    bt = _cmul(*co_c, rep(btr_ref, 1), rep(bti_ref, 1))
    pb = _cmul(*bt, *power(neg_c))
    qc = _cmul(rep(ctr_ref, 0), rep(cti_ref, 0), *power(pos_c))
    q1 = _cmul(*qc, *pos_c[0])
    bi = _cmul(*pb, *top_c)
    r2 = lax.broadcasted_iota(jnp.int32, (kw, kw), 0) // cg
    c2 = lax.broadcasted_iota(jnp.int32, (kw, kw), 1) // cg
    eye = jnp.where(lax.broadcasted_iota(jnp.int32, (n_p, n_p), 0)
                    == lax.broadcasted_iota(jnp.int32, (n_p, n_p), 1), 1.0, 0.0)
    for g in range(n_grp):
        rs = slice(g * n_p, (g + 1) * n_p)
        m_full = _ftdot(pb[0][rs], qc[0][rs]) - _ftdot(pb[1][rs], qc[1][rs])
        w1_ref[g, :, 0:kw] = jnp.where(r2 <= c2, m_full, 0.0).astype(w1_ref.dtype)
        w1_ref[g, :, kw:kw + n_p] = _ftdot(bi[0][rs], eye).astype(w1_ref.dtype)
        w1_ref[g, :, kw + n_p:kw + 2 * n_p] = _ftdot(bi[1][rs], eye).astype(w1_ref.dtype)
        co_ref[g, 0:n_p, :] = q1[0][rs].astype(co_ref.dtype)
        co_ref[g, n_p:2 * n_p, :] = (-q1[1][rs]).astype(co_ref.dtype)

    _s5t_tables_body(are2_ref, aim2_ref, ldt2_ref, pin_re, pin_im, pout_re, pout_im, al_re, al_im,
                     tc=tc, sc=sc)


def _s5t_prep(a_re, a_im, log_dt, b_re, b_im, c_re, c_im):
    g, p = a_re.shape
    cg = b_re.shape[-1]
    kw = T_CHUNK * cg
    lanes = g * 2 * p
    ldt = jnp.broadcast_to(log_dt[:, None], (g, p))
    row = lambda a: a.reshape(1, g * p)
    spread = lambda a: jnp.broadcast_to(a[:, None, :], (g, 2, p)).reshape(1, lanes)
    tab16 = jax.ShapeDtypeStruct((S_CHUNK, lanes), BF16)
    tab32 = jax.ShapeDtypeStruct((S_CHUNK, lanes), F32)
    one = jax.ShapeDtypeStruct((1, lanes), F32)
    w1, co, *tabs = pl.pallas_call(
        functools.partial(_s5t_prep_body, tc=T_CHUNK, sc=S_CHUNK, cg=cg, n_p=p),
        out_shape=[jax.ShapeDtypeStruct((g, kw, kw + 2 * p), BF16),
                   jax.ShapeDtypeStruct((g, 2 * p, kw), BF16),
                   tab16, tab16, tab32, tab32, one, one],
        compiler_params=pltpu.CompilerParams(vmem_limit_bytes=VMEM_LIMIT),
        name="s5t_prep",
    )(row(a_re), row(a_im), row(ldt),
      b_re, b_im, c_re, c_im, spread(a_re), spread(a_im), spread(ldt))
    return w1, co, tabs


def _s5t_tables_body(are_ref, aim_ref, ldt_ref, pin_re, pin_im, pout_re, pout_im, al_re, al_im,
                     *, tc, sc):
    lre = are_ref[...] * jnp.exp(ldt_ref[...]) * float(tc)
    lim = aim_ref[...] * jnp.exp(ldt_ref[...]) * float(tc)
    lane = lax.broadcasted_iota(jnp.int32, lre.shape, 1)
    sign = jnp.where((lane % LANES) < LANES // 2, -1.0, 1.0)
    n = lax.broadcasted_iota(jnp.int32, (sc, 1), 0).astype(F32) - float(sc // 2)
    m_out = jnp.exp(lre * n)
    m_in = jnp.exp(-(lre * n))
    ang = lim * n
    pin_re[...] = (m_in * jnp.cos(ang)).astype(pin_re.dtype)
    pin_im[...] = (sign * (-(m_in * jnp.sin(ang)))).astype(pin_im.dtype)
    pout_re[...] = m_out * jnp.cos(ang)
    pout_im[...] = sign * (m_out * jnp.sin(ang))
    m_al = jnp.exp(lre * float(sc))
    al_re[...] = m_al * jnp.cos(lim * float(sc))
    al_im[...] = sign * (m_al * jnp.sin(lim * float(sc)))


def _atom_transpose(regs, atom, width):
    regs = list(regs)
    n = len(regs)
    d = n // 2
    while d:
        low = (atom & d) == 0
        for j in range(n):
            if j & d:
                continue
            a, b = regs[j], regs[j + d]
            regs[j] = jnp.where(low, a, pltpu.roll(b, d * width, axis=1))
            regs[j + d] = jnp.where(low, pltpu.roll(a, LANES - d * width, axis=1), b)
        d //= 2
    return regs


def _sublane_transpose(regs, srow):
    regs = list(regs)
    d = SUBLANES // 2
    while d:
        low = (srow & d) == 0
        for j in range(SUBLANES):
            if j & d:
                continue
            a, b = regs[j], regs[j + d]
            regs[j] = jnp.where(low, a, pltpu.roll(b, d, axis=0))
            regs[j + d] = jnp.where(low, pltpu.roll(a, SUBLANES - d, axis=0), b)
        d //= 2
    return regs


def _swap_halves(x):
    tiles = [pltpu.roll(x[:, t * LANES:(t + 1) * LANES], LANES // 2, axis=1)
             for t in range(x.shape[1] // LANES)]
    return jnp.concatenate(tiles, axis=1)


def _s5t_body(u_ref, w1_ref, co_ref, pin_re, pin_im, pout_re, pout_im, al_re, al_im,
              d_ref, wglu_ref, bglu_ref, y_ref,
              upk_ref, yloc_ref, sin_ref, xs_ref, carry_ref, udi_ref, *, rc, tc, sc, n_grp, cg):
    t = pl.program_id(1)
    kw = tc * cg
    n_q = u_ref.shape[0]
    atoms = LANES // cg

    @pl.when(t == 0)
    def _():
        carry_ref[...] = jnp.zeros(carry_ref.shape, F32)
        xs_ref[:, 0:SUBLANES, :] = jnp.zeros((n_grp, SUBLANES, LANES), F32)

    atom = lax.broadcasted_iota(jnp.int32, (rc, LANES), 1) // cg

    n_a = tc // atoms

    assert atoms == SUBLANES
    srow = lax.broadcasted_iota(jnp.int32, (SUBLANES, LANES), 0)

    def rows_to_planes(q, a):
        pieces = [[] for _ in range(atoms)]
        for nb in range(rc // SUBLANES):
            nat = [u_ref[q, (nb * SUBLANES + m) * tc + a * atoms:
                         (nb * SUBLANES + m) * tc + (a + 1) * atoms, :] for m in range(SUBLANES)]
            for k, piece in enumerate(_sublane_transpose(nat, srow)):
                pieces[k].append(piece)
        return [jnp.concatenate(p, axis=0) for p in pieces]

    def planes_to_rows(q, a, planes):
        for nb in range(rc // SUBLANES):
            regs = [p[nb * SUBLANES:(nb + 1) * SUBLANES, :] for p in planes]
            for m, nat in enumerate(_sublane_transpose(regs, srow)):
                r0 = (nb * SUBLANES + m) * tc + a * atoms
                y_ref[q, r0:r0 + atoms, :] = nat

    def pack():
        for q in range(n_q):
            for a in range(n_a):
                planes = rows_to_planes(q, a)
                for k in range(atoms):
                    udi_ref[q, a * atoms + k] = planes[k]
                for r, tile in enumerate(_atom_transpose(planes, atom, cg)):
                    upk_ref[q * atoms + r, :, a * LANES:(a + 1) * LANES] = tile.astype(BF16)
                yield

    per_stage = atoms // n_a

    def local():
        for _ in range(n_a):
            yield
        for g in range(n_grp):
            res = jnp.dot(upk_ref[g], w1_ref[g], preferred_element_type=F32)
            yloc_ref[g] = res[:, :kw]
            sin_ref[g] = res[:, kw:]
            if g % per_stage == per_stage - 1:
                yield

    _interleave([pack(), local()])

    ri = lax.broadcasted_iota(jnp.int32, (sc, sc), 0)
    ci = lax.broadcasted_iota(jnp.int32, (sc, sc), 1)
    tri = jnp.where(ci <= ri, 1.0, 0.0).astype(BF16)
    n_c2 = rc // sc
    ss = [jnp.concatenate([sin_ref[g, c2 * sc:(c2 + 1) * sc, :] for g in range(n_grp)], axis=1)
          for c2 in range(n_c2)]
    ss_sw = [_swap_halves(s) for s in ss]
    xin = [pin_re[...] * ss[c2].astype(BF16) + pin_im[...] * ss_sw[c2].astype(BF16)
           for c2 in range(n_c2)]
    psum = [jnp.dot(tri, x, preferred_element_type=F32) for x in xin]
    psum_sw = [_swap_halves(p) for p in psum]
    carry = carry_ref[...]
    carry_sw = _swap_halves(carry)
    carries = []
    for c2 in range(rc // sc):
        carries.append((carry, carry_sw))
        last = psum[c2][sc - 1:sc, :] + carry
        last_sw = psum_sw[c2][sc - 1:sc, :] + carry_sw
        carry = al_re[...] * last + al_im[...] * last_sw
        carry_sw = al_re[...] * last_sw - al_im[...] * last
    carry_ref[...] = carry
    for c2 in range(rc // sc):
        x = (pout_re[...] * (psum[c2] + carries[c2][0])
             + pout_im[...] * (psum_sw[c2] + carries[c2][1]))
        for g in range(n_grp):
            xs_ref[g, SUBLANES + c2 * sc:SUBLANES + (c2 + 1) * sc, :] = x[:, g * LANES:(g + 1) * LANES]

    for g in range(n_grp):
        xprev = xs_ref[g, SUBLANES - 1:SUBLANES - 1 + rc, :].astype(BF16)
        yloc_ref[g] = yloc_ref[g] + jnp.dot(xprev, co_ref[g], preferred_element_type=F32)
    xs_ref[:, SUBLANES - 1:SUBLANES, :] = xs_ref[:, SUBLANES - 1 + rc:SUBLANES + rc, :]

    wglu = wglu_ref[...].astype(BF16)

    def unpack(a):
        for _ in range(a * n_q):
            yield
        planes = [[None] * n_q for _ in range(atoms)]
        for q in range(n_q):
            tiles = [yloc_ref[q * atoms + r, :, a * LANES:(a + 1) * LANES] for r in range(atoms)]
            for k, tile in enumerate(_atom_transpose(tiles, atom, cg)):
                planes[k][q] = tile
            yield
        outs = []
        for k in range(atoms):
            i = a * atoms + k
            uu = jnp.concatenate([udi_ref[q, i] for q in range(n_q)], axis=1)
            y = jax.nn.gelu(jnp.concatenate(planes[k], axis=1) + d_ref[...] * uu)
            gate = jax.nn.sigmoid(jnp.dot(y.astype(BF16), wglu, preferred_element_type=F32)
                                  + bglu_ref[...])
            outs.append(y * gate)
            if k % 2 == 1:
                yield
        for q in range(n_q):
            planes_to_rows(q, a, [o[:, q * LANES:(q + 1) * LANES] for o in outs])
            yield

    _interleave([unpack(a) for a in range(n_a)])


def _s5t(u4, w1, co, tabs, d, wglu, bglu, bsz, seq, *, rc):
    n_q, nt, _ = u4.shape
    n_grp, kw, wcols = w1.shape
    n_state = (wcols - kw) // 2
    cg = kw // T_CHUNK
    rows = rc * T_CHUNK
    n_t = seq // rows
    blk = pl.BlockSpec((n_q, rows, LANES), lambda b, t: (0, b * n_t + t, 0))
    c2 = lambda b, t: (0, 0)
    c3 = lambda b, t: (0, 0, 0)
    whole = lambda a: pl.BlockSpec(a.shape, c3 if a.ndim == 3 else c2)
    body = functools.partial(_s5t_body, rc=rc, tc=T_CHUNK, sc=S_CHUNK, n_grp=n_grp, cg=cg)
    return pl.pallas_call(
        body,
        grid=(bsz, n_t),
        in_specs=[blk] + [whole(a) for a in (w1, co, *tabs, d, wglu, bglu)],
        out_specs=blk,
        out_shape=jax.ShapeDtypeStruct((n_q, nt, LANES), F32),
        scratch_shapes=[
            pltpu.VMEM((n_grp, rc, kw), BF16),
            pltpu.VMEM((n_grp, rc, kw), F32),
            pltpu.VMEM((n_grp, rc, 2 * n_state), F32),
            pltpu.VMEM((n_grp, rc + SUBLANES, 2 * n_state), F32),
            pltpu.VMEM((1, n_grp * 2 * n_state), F32),
            pltpu.VMEM((n_q, T_CHUNK, rc, LANES), F32),
        ],
        compiler_params=_params("arbitrary", "arbitrary"),
        name="s5t",
    )(u4, w1, co, *tabs, d, wglu, bglu)


def _mem_kv_body(m_ref, g_ref, wk_ref, wv_ref, wq_ref, wo_ref, qk_ref, vw_ref, w_sc, *, xh):
    @pl.when(pl.program_id(0) == 0)
    def _():
        for i, ref in enumerate((wk_ref, wv_ref, wq_ref, wo_ref)):
            w_sc[i] = ref[...].astype(BF16)

    _, n_mem, d = m_ref.shape
    xd = d // xh
    scale = xd ** -0.5
    mn = _rms(m_ref[0], g_ref[...]).astype(BF16)
    k = jnp.dot(mn, w_sc[0], preferred_element_type=F32).astype(BF16)
    v = jnp.dot(mn, w_sc[1], preferred_element_type=F32).astype(BF16)
    nt_dims = (((1,), (1,)), ((), ()))
    for h in range(xh):
        hl = slice(h * xd, (h + 1) * xd)
        ml = slice(h * n_mem, (h + 1) * n_mem)
        qk = lax.dot_general(w_sc[2, :, hl], k[:, hl], nt_dims, preferred_element_type=F32)
        qk_ref[0, :, ml] = (qk * scale).astype(BF16)
        vw_ref[0, ml, :] = jnp.dot(v[:, hl], w_sc[3, hl, :],
                                   preferred_element_type=F32).astype(BF16)


def _mem_kv(mem, g, wk, wv, wq, wo):
    bsz, n_mem, d = mem.shape
    hm = X_HEADS * n_mem
    const = lambda b: (0, 0)
    weight = pl.BlockSpec((d, d), const, pipeline_mode=pl.Buffered(1))
    return pl.pallas_call(
        functools.partial(_mem_kv_body, xh=X_HEADS),
        grid=(bsz,),
        in_specs=[pl.BlockSpec((1, n_mem, d), lambda b: (b, 0, 0)), pl.BlockSpec((1, d), const),
                  weight, weight, weight, weight],
        out_specs=[pl.BlockSpec((1, d, hm), lambda b: (b, 0, 0)),
                   pl.BlockSpec((1, hm, d), lambda b: (b, 0, 0))],
        out_shape=[
            jax.ShapeDtypeStruct((bsz, d, hm), BF16),
            jax.ShapeDtypeStruct((bsz, hm, d), BF16),
        ],
        scratch_shapes=[pltpu.VMEM((4, d, d), BF16)],
        compiler_params=_params("arbitrary"),
        name="mem_kv",
    )(mem, g, wk, wv, wq, wo)


def _mix_attn_body(x_ref, o_ref, y_ref, wo_ref, gx_ref, qk_ref, vw_ref, h_ref, *, xh, n_sub):
    dn_w = o_ref.shape[1]
    n_mem = qk_ref.shape[2] // xh
    sub = x_ref.shape[0] // n_sub
    mm = functools.partial(jnp.dot, preferred_element_type=F32)
    wo = wo_ref[...].astype(BF16)

    def rows_gen(r0):
        rows = slice(r0, r0 + sub)
        y = jnp.concatenate([y_ref[q, rows, :] for q in range(y_ref.shape[0])], axis=1)
        mix = mm(o_ref[rows, :].astype(BF16), wo[0:dn_w, :]) + mm(y.astype(BF16), wo[dn_w:, :])
        yield
        h1 = x_ref[rows, :] + mix
        s = mm(_rms(h1, gx_ref[...]).astype(BF16), qk_ref[0])
        yield
        parts = []
        for h in range(xh):
            sh = s[:, h * n_mem:(h + 1) * n_mem]
            e = jnp.exp(sh - jnp.max(sh, axis=-1, keepdims=True))
            parts.append((e / jnp.sum(e, axis=-1, keepdims=True)).astype(BF16))
        att = mm(jnp.concatenate(parts, axis=1), vw_ref[0])
        yield
        h_ref[rows, :] = h1 + att

    _interleave([rows_gen(i * sub) for i in range(n_sub)])


def _mix_attn(x2, o, y, w_out, gx, qk, vw, bsz, seq, *, tm):
    nt, d = x2.shape
    dn_w = o.shape[1]
    n_t = seq // tm
    row = lambda b, t: (b * n_t + t, 0)
    c2 = lambda b, t: (0, 0)
    per_batch = lambda a: pl.BlockSpec((1,) + a.shape[1:], lambda b, t: (b, 0, 0))
    body = functools.partial(_mix_attn_body, xh=X_HEADS, n_sub=2)
    return pl.pallas_call(
        body,
        grid=(bsz, n_t),
        in_specs=[
            pl.BlockSpec((tm, d), row),
            pl.BlockSpec((tm, dn_w), row),
            pl.BlockSpec((y.shape[0], tm, LANES), lambda b, t: (0, b * n_t + t, 0)),
            pl.BlockSpec(w_out.shape, c2, pipeline_mode=pl.Buffered(1)),
            pl.BlockSpec((1, d), c2),
            per_batch(qk),
            per_batch(vw),
        ],
        out_specs=pl.BlockSpec((tm, d), row),
        out_shape=jax.ShapeDtypeStruct((nt, d), F32),
        compiler_params=_params("arbitrary", "arbitrary"),
        name="mix_attn",
    )(x2, o, y, w_out, gx, qk, vw)


def _ffn_body(h_ref, gf_ref, wg_ref, wu_ref, wd_ref, gl_ref, out_ref, *, final, fc):
    h = h_ref[...]
    hn = _rms(h, gf_ref[...]).astype(BF16)
    dff = wg_ref.shape[1]

    def down(gate, up, c0):
        act = (gate * jax.nn.sigmoid(gate) * up).astype(BF16)
        return jnp.dot(act, wd_ref[c0:c0 + fc, :].astype(BF16), preferred_element_type=F32)

    h3 = h
    prev = None
    for c0 in range(0, dff, fc):
        gate = jnp.dot(hn, wg_ref[:, c0:c0 + fc].astype(BF16), preferred_element_type=F32)
        up = jnp.dot(hn, wu_ref[:, c0:c0 + fc].astype(BF16), preferred_element_type=F32)
        if prev is not None:
            h3 = h3 + down(*prev)
        prev = (gate, up, c0)
    h3 = h3 + down(*prev)
    out_ref[...] = _rms(h3, gl_ref[...]) if final else h3


def _ffn(h, gf, wg, wu, wd, gl, *, tm, final):
    nt, d = h.shape
    dff = wg.shape[1]
    c2 = lambda i: (0, 0)
    row = lambda i: (i, 0)
    return pl.pallas_call(
        functools.partial(_ffn_body, final=final, fc=2 * LANES),
        grid=(nt // tm,),
        in_specs=[
            pl.BlockSpec((tm, d), row),
            pl.BlockSpec((1, d), c2),
            pl.BlockSpec((d, dff), c2, pipeline_mode=pl.Buffered(1)),
            pl.BlockSpec((d, dff), c2, pipeline_mode=pl.Buffered(1)),
            pl.BlockSpec((dff, d), c2, pipeline_mode=pl.Buffered(1)),
            pl.BlockSpec((1, d), c2),
        ],
        out_specs=pl.BlockSpec((tm, d), row),
        out_shape=jax.ShapeDtypeStruct((nt, d), F32),
        compiler_params=_params("arbitrary"),
        name="ffn",
    )(h, gf, wg, wu, wd, gl)


def _layer(h2, mem, bsz, seq, norm_mix_g, w_in, conv_w, dn_a_log, dn_dt_bias, dn_norm_g,
           s5_a_re, s5_a_im, s5_b_re, s5_b_im, s5_c_re, s5_c_im, s5_d, s5_log_dt,
           s5_w_glu, s5_b_glu, w_out, norm_x_g, norm_mem_g, w_xq, w_xk, w_xv, w_xo):
    d = h2.shape[1]
    nh = dn_a_log.shape[0]
    dh = dn_norm_g.shape[0]
    dn_w = nh * dh
    s5_w = s5_a_re.shape[0] * s5_b_re.shape[-1]

    off_a = 4 * dn_w
    off_u = off_a + 2 * nh
    w_ab = jnp.pad(w_in[:, off_a:off_u].astype(BF16), ((0, 0), (0, LANES - 2 * nh)))
    w_cat = jnp.concatenate([w_in[:, :off_a].astype(BF16), w_in[:, off_u:].astype(BF16), w_ab],
                            axis=1)
    alog = jnp.pad(dn_a_log, (0, LANES - nh)).reshape(1, LANES)
    dtb = jnp.pad(dn_dt_bias, (0, LANES - nh)).reshape(1, LANES)

    u, o = _mixer_in(h2, seq, norm_mix_g.reshape(1, d), w_cat, conv_w, alog, dtb,
                     dn_norm_g.reshape(1, dh), nh=nh, dh=dh, tb=MIXER_ROWS)

    w1, co, tabs = _s5t_prep(s5_a_re, s5_a_im, s5_log_dt, s5_b_re, s5_b_im, s5_c_re, s5_c_im)
    y = _s5t(u, w1, co, tabs, s5_d.reshape(1, s5_w), s5_w_glu, s5_b_glu.reshape(1, s5_w),
             bsz, seq, rc=S5_BLOCKS)

    qk_mem, vw_mem = _mem_kv(mem, norm_mem_g.reshape(1, d), w_xk, w_xv, w_xq, w_xo)
    return _mix_attn(h2, o, y, w_out, norm_x_g.reshape(1, d), qk_mem, vw_mem, bsz, seq,
                     tm=ATTN_ROWS)


def kernel(x, mem, norm_mix_g, w_in, conv_w, dn_a_log, dn_dt_bias, dn_norm_g, s5_a_re, s5_a_im,
           s5_b_re, s5_b_im, s5_c_re, s5_c_im, s5_d, s5_log_dt, s5_w_glu, s5_b_glu, w_out,
           norm_x_g, norm_mem_g, w_xq, w_xk, w_xv, w_xo, norm_ffn_g, w_gate, w_up, w_down,
           norm_final_g):
    bsz, seq, d = x.shape
    depth = w_in.shape[0]
    h = x.reshape(bsz * seq, d)
    for l in range(depth):
        h = _layer(h, mem, bsz, seq, norm_mix_g[l], w_in[l], conv_w[l], dn_a_log[l],
                   dn_dt_bias[l], dn_norm_g[l], s5_a_re[l], s5_a_im[l], s5_b_re[l], s5_b_im[l],
                   s5_c_re[l], s5_c_im[l], s5_d[l], s5_log_dt[l], s5_w_glu[l], s5_b_glu[l],
                   w_out[l], norm_x_g[l], norm_mem_g[l], w_xq[l], w_xk[l], w_xv[l], w_xo[l])
        h = _ffn(h, norm_ffn_g[l].reshape(1, d), w_gate[l], w_up[l], w_down[l],
                 norm_final_g.reshape(1, d), tm=FFN_ROWS, final=l == depth - 1)
    return h.reshape(bsz, seq, d)
```

```python
import functools

import jax
import jax.numpy as jnp
from jax import lax
from jax.experimental import pallas as pl
from jax.experimental.pallas import tpu as pltpu

F32 = jnp.float32
BF16 = jnp.bfloat16
EPS = 1e-6
HIGHEST = lax.Precision.HIGHEST

LANES = 128
SUBLANES = 8
DN_CHUNK = 64
T_CHUNK = 16
S_CHUNK = 16
X_HEADS = 4
VMEM_LIMIT = 58 * 1024 * 1024
MIXER_VMEM = 48 * 1024 * 1024

MIXER_ROWS = 256
S5_BLOCKS = 128
ATTN_ROWS = 1024
FFN_ROWS = 1024


def _fdot(a, b):
    return jnp.dot(a, b, precision=HIGHEST, preferred_element_type=F32)


def _rms(x, g):
    return x * lax.rsqrt(jnp.mean(x * x, axis=-1, keepdims=True) + EPS) * g


def _interleave(gens, stages_per_round=None):
    pending = [(gen, 1 if stages_per_round is None else stages_per_round[i])
               for i, gen in enumerate(gens)]
    while pending:
        for item in list(pending):
            gen, n = item
            try:
                for _ in range(n):
                    next(gen)
            except StopIteration:
                pending.remove(item)


def _params(*sem, vmem=VMEM_LIMIT):
    return pltpu.CompilerParams(dimension_semantics=sem, vmem_limit_bytes=vmem)


def _in_proj_stages(x_ref, g_ref, w_ref, cw_ref, alog_ref, dtb_ref,
                    q_ref, k_ref, kt_ref, v_ref, z_ref, u_ref, gt_ref,
                    buf_ref, *, tm, dn_w, nh, dh, cw_n):
    xnb = _rms(x_ref[...], g_ref[...]).astype(BF16)
    gw = 2 * dh

    def conv_silu(p, cols):
        outs = []
        for i in range(gw // LANES):
            c0 = cols.start + i * LANES
            s = c0 // LANES
            pc = p[:, i * LANES:(i + 1) * LANES]
            buf_ref[s, SUBLANES:SUBLANES + tm, :] = pc
            acc = cw_ref[cw_n - 1:cw_n, c0:c0 + LANES] * pc
            for j in range(cw_n - 1):
                off = SUBLANES - (cw_n - 1) + j
                acc = acc + cw_ref[j:j + 1, c0:c0 + LANES] * buf_ref[s, off:off + tm, :]
            buf_ref[s, 0:SUBLANES, :] = buf_ref[s, tm:tm + SUBLANES, :]
            outs.append(acc * jax.nn.sigmoid(acc))
        return jnp.concatenate(outs, axis=1)

    def l2n(a, scale):
        parts = []
        for i in range(gw // dh):
            ah = a[:, i * dh:(i + 1) * dh]
            parts.append(ah * (lax.rsqrt(jnp.sum(ah * ah, -1, keepdims=True) + EPS) * scale))
        return jnp.concatenate(parts, axis=1)

    def ep_q(p, c0):
        q_ref[:, c0:c0 + gw] = l2n(conv_silu(p, slice(c0, c0 + gw)), dh ** -0.5)

    def ep_k(p, c0):
        kn = l2n(conv_silu(p, slice(dn_w + c0, dn_w + c0 + gw)), 1.0)
        k_ref[:, c0:c0 + gw] = kn
        kt_ref[c0:c0 + gw, :] = kn.T

    def ep_v(p, c0):
        v_ref[:, c0:c0 + gw] = conv_silu(p, slice(2 * dn_w + c0, 2 * dn_w + c0 + gw))

    def ep_z(p, c0):
        z_ref[:, c0:c0 + gw] = p

    def ep_u(p, c0):
        for j in range(gw // LANES):
            u_ref[c0 // LANES + j] = p[:, j * LANES:(j + 1) * LANES]

    def ep_gates(ab, c0):
        lane = lax.broadcasted_iota(jnp.int32, ab.shape, 1)
        g = -jnp.exp(alog_ref[...]) * jnp.logaddexp(ab + dtb_ref[...], 0.0)
        gates = jnp.where(lane < nh, g, jax.nn.sigmoid(ab))
        gt_ref[...] = gates.T[0:SUBLANES, :]

    groups = []
    for i, ep in enumerate((ep_q, ep_k, ep_v, ep_z, ep_u)):
        groups += [(ep, i * dn_w + c0, gw, c0) for c0 in range(0, dn_w, gw)]
    groups.append((ep_gates, 5 * dn_w, LANES, 0))

    prev = None
    for ep, w0, width, c0 in groups:
        p = jnp.dot(xnb, w_ref[:, w0:w0 + width], preferred_element_type=F32)
        if prev is not None:
            prev[0](prev[1], prev[2])
        prev = (ep, p, c0)
        yield
    prev[0](prev[1], prev[2])


def _delta_stages(q_ref, k_ref, kt_ref, v_ref, z_ref, gt_ref, ng_ref,
                  o_ref, s_ref, u_sc, w_sc, qg_sc, attn_sc, kdt_sc, gtot_sc, wr, rd,
                  *, tb, nh, dh, ch):
    nc = tb // ch
    heads = range(nh)
    hls = [slice(h * dh, (h + 1) * dh) for h in heads]
    mm = functools.partial(jnp.dot, preferred_element_type=F32)

    def phase_b():
        gtot_b = gtot_sc[rd]
        s_all = [s_ref[h] for h in heads]
        for cidx in range(nc):
            rows = slice(cidx * ch, (cidx + 1) * ch)
            sb = [s_all[h].astype(BF16) for h in heads]
            ws = [mm(w_sc[rd, h, rows, :], sb[h]) for h in heads]
            qs = [mm(qg_sc[rd, h, rows, :], sb[h]) for h in heads]
            yield
            vb = [(u_sc[rd, h, rows, :] - ws[h]).astype(BF16) for h in heads]
            av = [mm(attn_sc[rd, h, rows, rows], vb[h]) for h in heads]
            kv = [mm(kdt_sc[rd, h, :, rows], vb[h]) for h in heads]
            yield
            outs = []
            for h in heads:
                a_last = jnp.exp(gtot_b[h:h + 1, cidx * ch:cidx * ch + 1])
                s_all[h] = s_all[h] * a_last + kv[h]
                o = qs[h] + av[h]
                o = o * lax.rsqrt(jnp.mean(o * o, -1, keepdims=True) + EPS)
                zh = z_ref[rows, hls[h]]
                outs.append(o * ng_ref[...] * (zh * jax.nn.sigmoid(zh)))
            o_ref[rows, :] = jnp.concatenate(outs, axis=1)
        for h in heads:
            s_ref[h] = s_all[h]

    r = lax.broadcasted_iota(jnp.int32, (tb, tb), 0)
    c = lax.broadcasted_iota(jnp.int32, (tb, tb), 1)
    same = (r // ch) == (c // ch)
    causal_bd = same & (c <= r)
    strict_bd = same & (c < r)
    gt = gt_ref[...]
    gcum_t = _fdot(gt, jnp.where(same & (r <= c), 1.0, 0.0))
    gtot_t = _fdot(gt, jnp.where(same, 1.0, 0.0))
    gcum = gcum_t.T
    gates = gt.T

    def pack(x):
        acc = x[0:ch, :]
        for i in range(1, nc):
            acc = acc + x[i * ch:(i + 1) * ch, :]
        return acc

    def unpack(x):
        return jnp.where(same, jnp.concatenate([x] * nc, axis=0), 0.0).astype(BF16)

    n_sq = max(1, (ch - 1).bit_length() - 1)

    def phase_a():
        gcol = [gcum[:, h:h + 1] for h in heads]
        grow = [gcum_t[h:h + 1, :] for h in heads]
        beta = [gates[:, nh + h:nh + h + 1] for h in heads]
        kb = [k_ref[:, hls[h]] * beta[h] for h in heads]
        gram = [mm(jnp.concatenate([kb[h], q_ref[:, hls[h]]], axis=0).astype(BF16),
                   kt_ref[hls[h], :].astype(BF16)) for h in heads]
        yield
        ppk, apk = [], []
        for h in heads:
            decay = jnp.exp(jnp.where(causal_bd, gcol[h] - grow[h], -jnp.inf))
            p_bd = -jnp.where(strict_bd, gram[h][:tb] * decay, 0.0)
            attn_sc[wr, h] = (gram[h][tb:] * decay).astype(BF16)
            pk = pack(p_bd)
            apk.append(pk)
            ppk.append(mm(pk.astype(BF16), p_bd.astype(BF16)))
        yield
        for j in range(1, n_sq + 1):
            res = []
            for h in heads:
                lhs = jnp.concatenate([apk[h], ppk[h]], axis=0) if j < n_sq else apk[h]
                res.append(mm(lhs.astype(BF16), unpack(ppk[h])))
            yield
            for h in heads:
                apk[h] = apk[h] + ppk[h] + res[h][:ch]
                if j < n_sq:
                    ppk[h] = res[h][ch:]
        egc = [jnp.exp(gcol[h]) for h in heads]
        rhs = [jnp.concatenate([v_ref[:, hls[h]] * beta[h], kb[h] * egc[h]], axis=1) for h in heads]
        prod = [mm(unpack(apk[h]), rhs[h].astype(BF16)) for h in heads]
        yield
        for h in heads:
            uw = rhs[h] + prod[h]
            u_sc[wr, h] = uw[:, :dh]
            w_sc[wr, h] = uw[:, dh:].astype(BF16)
            qg_sc[wr, h] = (q_ref[:, hls[h]] * egc[h]).astype(BF16)
            kdt_sc[wr, h] = (kt_ref[hls[h], :] * jnp.exp(gtot_t[h:h + 1, :] - grow[h])).astype(BF16)
        gtot_sc[wr] = gtot_t

    return phase_b(), phase_a()


def _mixer_in_body(x_ref, g_ref, w_ref, cw_ref, alog_ref, dtb_ref, ng_ref,
                   u_ref, o_ref,
                   buf_ref, q_sc, k_sc, kt_sc, v_sc, z_sc, gt_sc,
                   s_ref, u_sc, w_sc, qg_sc, attn_sc, kdt_sc, gtot_sc,
                   *, tb, n_t, nh, dh, cw_n):
    s = pl.program_id(0)
    dn_w = nh * dh

    @pl.when(s == 0)
    def _():
        for ref in (buf_ref, q_sc, k_sc, kt_sc, v_sc, z_sc, gt_sc,
                    s_ref, u_sc, w_sc, qg_sc, attn_sc, kdt_sc, gtot_sc):
            ref[...] = jnp.zeros(ref.shape, ref.dtype)

    @pl.when(s % n_t == 0)
    def _():
        buf_ref[:, 0:SUBLANES, :] = jnp.zeros((3 * dn_w // LANES, SUBLANES, LANES), F32)

    @pl.when((s - 2) % n_t == 0)
    def _():
        s_ref[...] = jnp.zeros(s_ref.shape, F32)

    cur = s % 2
    prv = 1 - cur
    zw = s % 3
    zr = (s + 1) % 3
    proj = _in_proj_stages(
        x_ref, g_ref, w_ref, cw_ref, alog_ref, dtb_ref,
        q_sc.at[cur], k_sc.at[cur], kt_sc.at[cur], v_sc.at[cur], z_sc.at[zw], u_ref, gt_sc.at[cur],
        buf_ref, tm=tb, dn_w=dn_w, nh=nh, dh=dh, cw_n=cw_n)
    phase_b, phase_a = _delta_stages(
        q_sc.at[prv], k_sc.at[prv], kt_sc.at[prv], v_sc.at[prv], z_sc.at[zr], gt_sc.at[prv], ng_ref,
        o_ref, s_ref, u_sc, w_sc, qg_sc, attn_sc, kdt_sc, gtot_sc, cur, prv,
        tb=tb, nh=nh, dh=dh, ch=DN_CHUNK)
    _interleave([proj, phase_b, phase_a])


def _mixer_in(x2, seq, g, w_cat, conv_w, alog, dtb, ng, *, nh, dh, tb):
    nt, d = x2.shape
    dn_w = nh * dh
    n_blk = nt // tb
    cw_n = conv_w.shape[0]
    const = lambda s: (0, 0)
    in_blk = lambda s: (jnp.minimum(s, n_blk - 1), 0)
    out_blk = lambda s: (jnp.maximum(s - 2, 0), 0)
    body = functools.partial(_mixer_in_body, tb=tb, n_t=seq // tb, nh=nh, dh=dh, cw_n=cw_n)
    two = lambda *shape: pltpu.VMEM((2,) + shape, F32)
    return pl.pallas_call(
        body,
        grid=(n_blk + 2,),
        in_specs=[pl.BlockSpec((tb, d), in_blk)]
        + [pl.BlockSpec(a.shape, const) for a in (g, w_cat, conv_w, alog, dtb, ng)],
        out_specs=[pl.BlockSpec((dn_w // LANES, tb, LANES),
                                lambda s: (0, jnp.minimum(s, n_blk - 1), 0)),
                   pl.BlockSpec((tb, dn_w), out_blk)],
        out_shape=[jax.ShapeDtypeStruct((dn_w // LANES, nt, LANES), F32),
                   jax.ShapeDtypeStruct((nt, dn_w), F32)],
        scratch_shapes=[
            pltpu.VMEM((3 * dn_w // LANES, tb + SUBLANES, LANES), F32),
            two(tb, dn_w), two(tb, dn_w), two(dn_w, tb), two(tb, dn_w),
            pltpu.VMEM((3, tb, dn_w), F32),
            two(SUBLANES, tb),
            pltpu.VMEM((nh, dh, dh), F32),
            pltpu.VMEM((2, nh, tb, dh), F32),
            pltpu.VMEM((2, nh, tb, dh), BF16),
            pltpu.VMEM((2, nh, tb, dh), BF16),
            pltpu.VMEM((2, nh, tb, tb), BF16),
            pltpu.VMEM((2, nh, dh, tb), BF16),
            pltpu.VMEM((2, SUBLANES, tb), F32),
        ],
        compiler_params=_params("arbitrary", vmem=MIXER_VMEM),
        name="mixer_in",
    )(x2, g, w_cat, conv_w, alog, dtb, ng)


def _ftdot(a, b):
    return lax.dot_general(a.astype(BF16), b.astype(BF16), (((0,), (0,)), ((), ())),
                           preferred_element_type=F32)


def _cmul(xr, xi, yr, yi):
    return xr * yr - xi * yi, xr * yi + xi * yr


def _s5t_prep_body(are_ref, aim_ref, ldt_ref, btr_ref, bti_ref, ctr_ref, cti_ref,
                   are2_ref, aim2_ref, ldt2_ref,
                   w1_ref, co_ref, pin_re, pin_im, pout_re, pout_im, al_re, al_im,
                   *, tc, sc, cg, n_p):
    n_grp = w1_ref.shape[0]
    kw = tc * cg
    a_re = are_ref[...]
    a_im = aim_ref[...]
    dt = jnp.exp(ldt_ref[...])
    lre = a_re * dt
    lim = a_im * dt
    mag = jnp.exp(lre)
    ab = (mag * jnp.cos(lim), mag * jnp.sin(lim))
    nr, ni = ab[0] - 1.0, ab[1]
    den = a_re * a_re + a_im * a_im
    co = ((nr * a_re + ni * a_im) / den, (ni * a_re - nr * a_im) / den)
    inv_mag2 = jnp.exp(-2.0 * lre)
    ai = (ab[0] * inv_mag2, -ab[1] * inv_mag2)
    n_bits = tc.bit_length() - 1
    pos, neg = [ab], [ai]
    for _ in range(n_bits - 1):
        pos.append(_cmul(*pos[-1], *pos[-1]))
        neg.append(_cmul(*neg[-1], *neg[-1]))
    top = pos[0]
    for f in pos[1:]:
        top = _cmul(*top, *f)
    rows = [co, top] + pos + neg
    flat = [part for pair in rows for part in pair]
    cols = []
    for i in range(0, len(flat), SUBLANES):
        blk = flat[i:i + SUBLANES]
        blk = blk + [blk[-1]] * (SUBLANES - len(blk))
        t = jnp.concatenate(blk, axis=0).T
        cols += [t[:, j:j + 1] for j in range(SUBLANES)]
    pair = lambda k: (cols[2 * k], cols[2 * k + 1])
    co_c, top_c = pair(0), pair(1)
    pos_c = [pair(2 + k) for k in range(n_bits)]
    neg_c = [pair(2 + n_bits + k) for k in range(n_bits)]

    slot = lax.broadcasted_iota(jnp.int32, (1, kw), 1) // cg

    def power(factors):
        acc = None
        for k, (fr, fi) in enumerate(factors):
            bit = ((slot >> k) & 1) == 1
            term = (jnp.where(bit, fr, 1.0), jnp.where(bit, fi, 0.0))
            acc = term if acc is None else _cmul(*acc, *term)
        return acc

    place = jnp.where(lax.broadcasted_iota(jnp.int32, (cg, kw), 0)
                      == lax.broadcasted_iota(jnp.int32, (cg, kw), 1) % cg, 1.0, 0.0)
    place = place.astype(BF16)

    def rep(ref, lhs_contract):
        dims = (((lhs_contract,), (0,)), ((), ()))
        out = []
        for g in range(n_grp):
            x = ref[g]
            hi = x.astype(BF16)
            r1 = x - hi.astype(F32)
            mid = r1.astype(BF16)
            lo = (r1 - mid.astype(F32)).astype(BF16)
            out.append(sum(lax.dot_general(part, place, dims, preferred_element_type=F32)
                           for part in (hi, mid, lo)))
        return jnp.concatenate(out, axis=0)

    bt = _cmul(*co_c, rep(btr_ref, 1), rep(bti_ref, 1))
    pb = _cmul(*bt, *power(neg_c))
    qc = _cmul(rep(ctr_ref, 0), rep(cti_ref, 0), *power(pos_c))
    q1 = _cmul(*qc, *pos_c[0])
    bi = _cmul(*pb, *top_c)
    r2 = lax.broadcasted_iota(jnp.int32, (kw, kw), 0) // cg
    c2 = lax.broadcasted_iota(jnp.int32, (kw, kw), 1) // cg
    eye = jnp.where(lax.broadcasted_iota(jnp.int32, (n_p, n_p), 0)
                    == lax.broadcasted_iota(jnp.int32, (n_p, n_p), 1), 1.0, 0.0)
    for g in range(n_grp):
        rs = slice(g * n_p, (g + 1) * n_p)
        m_full = _ftdot(pb[0][rs], qc[0][rs]) - _ftdot(pb[1][rs], qc[1][rs])
        w1_ref[g, :, 0:kw] = jnp.where(r2 <= c2, m_full, 0.0).astype(w1_ref.dtype)
        w1_ref[g, :, kw:kw + n_p] = _ftdot(bi[0][rs], eye).astype(w1_ref.dtype)
        w1_ref[g, :, kw + n_p:kw + 2 * n_p] = _ftdot(bi[1][rs], eye).astype(w1_ref.dtype)
        co_ref[g, 0:n_p, :] = q1[0][rs].astype(co_ref.dtype)
        co_ref[g, n_p:2 * n_p, :] = (-q1[1][rs]).astype(co_ref.dtype)

    _s5t_tables_body(are2_ref, aim2_ref, ldt2_ref, pin_re, pin_im, pout_re, pout_im, al_re, al_im,
                     tc=tc, sc=sc)


def _s5t_prep(a_re, a_im, log_dt, b_re, b_im, c_re, c_im):
    g, p = a_re.shape
    cg = b_re.shape[-1]
    kw = T_CHUNK * cg
    lanes = g * 2 * p
    ldt = jnp.broadcast_to(log_dt[:, None], (g, p))
    row = lambda a: a.reshape(1, g * p)
    spread = lambda a: jnp.broadcast_to(a[:, None, :], (g, 2, p)).reshape(1, lanes)
    tab16 = jax.ShapeDtypeStruct((S_CHUNK, lanes), BF16)
    tab32 = jax.ShapeDtypeStruct((S_CHUNK, lanes), F32)
    one = jax.ShapeDtypeStruct((1, lanes), F32)
    w1, co, *tabs = pl.pallas_call(
        functools.partial(_s5t_prep_body, tc=T_CHUNK, sc=S_CHUNK, cg=cg, n_p=p),
        out_shape=[jax.ShapeDtypeStruct((g, kw, kw + 2 * p), BF16),
                   jax.ShapeDtypeStruct((g, 2 * p, kw), BF16),
                   tab16, tab16, tab32, tab32, one, one],
        compiler_params=pltpu.CompilerParams(vmem_limit_bytes=VMEM_LIMIT),
        name="s5t_prep",
    )(row(a_re), row(a_im), row(ldt),
      b_re, b_im, c_re, c_im, spread(a_re), spread(a_im), spread(ldt))
    return w1, co, tabs


def _s5t_tables_body(are_ref, aim_ref, ldt_ref, pin_re, pin_im, pout_re, pout_im, al_re, al_im,
                     *, tc, sc):
    lre = are_ref[...] * jnp.exp(ldt_ref[...]) * float(tc)
    lim = aim_ref[...] * jnp.exp(ldt_ref[...]) * float(tc)
    lane = lax.broadcasted_iota(jnp.int32, lre.shape, 1)
    sign = jnp.where((lane % LANES) < LANES // 2, -1.0, 1.0)
    n = lax.broadcasted_iota(jnp.int32, (sc, 1), 0).astype(F32) - float(sc // 2)
    m_out = jnp.exp(lre * n)
    m_in = jnp.exp(-(lre * n))
    ang = lim * n
    pin_re[...] = (m_in * jnp.cos(ang)).astype(pin_re.dtype)
    pin_im[...] = (sign * (-(m_in * jnp.sin(ang)))).astype(pin_im.dtype)
    pout_re[...] = m_out * jnp.cos(ang)
    pout_im[...] = sign * (m_out * jnp.sin(ang))
    m_al = jnp.exp(lre * float(sc))
    al_re[...] = m_al * jnp.cos(lim * float(sc))
    al_im[...] = sign * (m_al * jnp.sin(lim * float(sc)))


def _atom_transpose(regs, atom, width):
    regs = list(regs)
    n = len(regs)
    d = n // 2
    while d:
        low = (atom & d) == 0
        for j in range(n):
            if j & d:
                continue
            a, b = regs[j], regs[j + d]
            regs[j] = jnp.where(low, a, pltpu.roll(b, d * width, axis=1))
            regs[j + d] = jnp.where(low, pltpu.roll(a, LANES - d * width, axis=1), b)
        d //= 2
    return regs


def _sublane_transpose(regs, srow):
    regs = list(regs)
    d = SUBLANES // 2
    while d:
        low = (srow & d) == 0
        for j in range(SUBLANES):
            if j & d:
                continue
            a, b = regs[j], regs[j + d]
            regs[j] = jnp.where(low, a, pltpu.roll(b, d, axis=0))
            regs[j + d] = jnp.where(low, pltpu.roll(a, SUBLANES - d, axis=0), b)
        d //= 2
    return regs


def _swap_halves(x):
    tiles = [pltpu.roll(x[:, t * LANES:(t + 1) * LANES], LANES // 2, axis=1)
             for t in range(x.shape[1] // LANES)]
    return jnp.concatenate(tiles, axis=1)


def _s5t_body(u_ref, w1_ref, co_ref, pin_re, pin_im, pout_re, pout_im, al_re, al_im,
              d_ref, wglu_ref, bglu_ref, y_ref,
              upk_ref, yloc_ref, sin_ref, xs_ref, carry_ref, udi_ref, *, rc, tc, sc, n_grp, cg):
    t = pl.program_id(1)
    kw = tc * cg
    n_q = u_ref.shape[0]
    atoms = LANES // cg

    @pl.when(t == 0)
    def _():
        carry_ref[...] = jnp.zeros(carry_ref.shape, F32)
        xs_ref[:, 0:SUBLANES, :] = jnp.zeros((n_grp, SUBLANES, LANES), F32)

    atom = lax.broadcasted_iota(jnp.int32, (rc, LANES), 1) // cg

    n_a = tc // atoms

    assert atoms == SUBLANES
    srow = lax.broadcasted_iota(jnp.int32, (SUBLANES, LANES), 0)

    def rows_to_planes(q, a):
        pieces = [[] for _ in range(atoms)]
        for nb in range(rc // SUBLANES):
            nat = [u_ref[q, (nb * SUBLANES + m) * tc + a * atoms:
                         (nb * SUBLANES + m) * tc + (a + 1) * atoms, :] for m in range(SUBLANES)]
            for k, piece in enumerate(_sublane_transpose(nat, srow)):
                pieces[k].append(piece)
        return [jnp.concatenate(p, axis=0) for p in pieces]

    def planes_to_rows(q, a, planes):
        for nb in range(rc // SUBLANES):
            regs = [p[nb * SUBLANES:(nb + 1) * SUBLANES, :] for p in planes]
            for m, nat in enumerate(_sublane_transpose(regs, srow)):
                r0 = (nb * SUBLANES + m) * tc + a * atoms
                y_ref[q, r0:r0 + atoms, :] = nat

    def pack():
        for q in range(n_q):
            for a in range(n_a):
                planes = rows_to_planes(q, a)
                for k in range(atoms):
                    udi_ref[q, a * atoms + k] = planes[k]
                for r, tile in enumerate(_atom_transpose(planes, atom, cg)):
                    upk_ref[q * atoms + r, :, a * LANES:(a + 1) * LANES] = tile.astype(BF16)
                yield

    per_stage = atoms // n_a

    def local():
        for _ in range(n_a):
            yield
        for g in range(n_grp):
            res = jnp.dot(upk_ref[g], w1_ref[g], preferred_element_type=F32)
            yloc_ref[g] = res[:, :kw]
            sin_ref[g] = res[:, kw:]
            if g % per_stage == per_stage - 1:
                yield

    _interleave([pack(), local()])

    ri = lax.broadcasted_iota(jnp.int32, (sc, sc), 0)
    ci = lax.broadcasted_iota(jnp.int32, (sc, sc), 1)
    tri = jnp.where(ci <= ri, 1.0, 0.0).astype(BF16)
    n_c2 = rc // sc
    ss = [jnp.concatenate([sin_ref[g, c2 * sc:(c2 + 1) * sc, :] for g in range(n_grp)], axis=1)
          for c2 in range(n_c2)]
    ss_sw = [_swap_halves(s) for s in ss]
    xin = [pin_re[...] * ss[c2].astype(BF16) + pin_im[...] * ss_sw[c2].astype(BF16)
           for c2 in range(n_c2)]
    psum = [jnp.dot(tri, x, preferred_element_type=F32) for x in xin]
    psum_sw = [_swap_halves(p) for p in psum]
    carry = carry_ref[...]
    carry_sw = _swap_halves(carry)
    carries = []
    for c2 in range(rc // sc):
        carries.append((carry, carry_sw))
        last = psum[c2][sc - 1:sc, :] + carry
        last_sw = psum_sw[c2][sc - 1:sc, :] + carry_sw
        carry = al_re[...] * last + al_im[...] * last_sw
        carry_sw = al_re[...] * last_sw - al_im[...] * last
    carry_ref[...] = carry
    for c2 in range(rc // sc):
        x = (pout_re[...] * (psum[c2] + carries[c2][0])
             + pout_im[...] * (psum_sw[c2] + carries[c2][1]))
        for g in range(n_grp):
            xs_ref[g, SUBLANES + c2 * sc:SUBLANES + (c2 + 1) * sc, :] = x[:, g * LANES:(g + 1) * LANES]

    for g in range(n_grp):
        xprev = xs_ref[g, SUBLANES - 1:SUBLANES - 1 + rc, :].astype(BF16)
        yloc_ref[g] = yloc_ref[g] + jnp.dot(xprev, co_ref[g], preferred_element_type=F32)
    xs_ref[:, SUBLANES - 1:SUBLANES, :] = xs_ref[:, SUBLANES - 1 + rc:SUBLANES + rc, :]

    wglu = wglu_ref[...].astype(BF16)

    def unpack(a):
        for _ in range(a * n_q):
            yield
        planes = [[None] * n_q for _ in range(atoms)]
        for q in range(n_q):
            tiles = [yloc_ref[q * atoms + r, :, a * LANES:(a + 1) * LANES] for r in range(atoms)]
            for k, tile in enumerate(_atom_transpose(tiles, atom, cg)):
                planes[k][q] = tile
            yield
        outs = []
        for k in range(atoms):
            i = a * atoms + k
            uu = jnp.concatenate([udi_ref[q, i] for q in range(n_q)], axis=1)
            y = jax.nn.gelu(jnp.concatenate(planes[k], axis=1) + d_ref[...] * uu)
            gate = jax.nn.sigmoid(jnp.dot(y.astype(BF16), wglu, preferred_element_type=F32)
                                  + bglu_ref[...])
            outs.append(y * gate)
            if k % 2 == 1:
                yield
        for q in range(n_q):
            planes_to_rows(q, a, [o[:, q * LANES:(q + 1) * LANES] for o in outs])
            yield

    _interleave([unpack(a) for a in range(n_a)])


def _s5t(u4, w1, co, tabs, d, wglu, bglu, bsz, seq, *, rc):
    n_q, nt, _ = u4.shape
    n_grp, kw, wcols = w1.shape
    n_state = (wcols - kw) // 2
    cg = kw // T_CHUNK
    rows = rc * T_CHUNK
    n_t = seq // rows
    blk = pl.BlockSpec((n_q, rows, LANES), lambda b, t: (0, b * n_t + t, 0))
    c2 = lambda b, t: (0, 0)
    c3 = lambda b, t: (0, 0, 0)
    whole = lambda a: pl.BlockSpec(a.shape, c3 if a.ndim == 3 else c2)
    body = functools.partial(_s5t_body, rc=rc, tc=T_CHUNK, sc=S_CHUNK, n_grp=n_grp, cg=cg)
    return pl.pallas_call(
        body,
        grid=(bsz, n_t),
        in_specs=[blk] + [whole(a) for a in (w1, co, *tabs, d, wglu, bglu)],
        out_specs=blk,
        out_shape=jax.ShapeDtypeStruct((n_q, nt, LANES), F32),
        scratch_shapes=[
            pltpu.VMEM((n_grp, rc, kw), BF16),
            pltpu.VMEM((n_grp, rc, kw), F32),
            pltpu.VMEM((n_grp, rc, 2 * n_state), F32),
            pltpu.VMEM((n_grp, rc + SUBLANES, 2 * n_state), F32),
            pltpu.VMEM((1, n_grp * 2 * n_state), F32),
            pltpu.VMEM((n_q, T_CHUNK, rc, LANES), F32),
        ],
        compiler_params=_params("arbitrary", "arbitrary"),
        name="s5t",
    )(u4, w1, co, *tabs, d, wglu, bglu)


def _mem_kv_body(m_ref, g_ref, wk_ref, wv_ref, wq_ref, wo_ref, qk_ref, vw_ref, w_sc, *, xh):
    @pl.when(pl.program_id(0) == 0)
    def _():
        for i, ref in enumerate((wk_ref, wv_ref, wq_ref, wo_ref)):
            w_sc[i] = ref[...].astype(BF16)

    _, n_mem, d = m_ref.shape
    xd = d // xh
    scale = xd ** -0.5
    mn = _rms(m_ref[0], g_ref[...]).astype(BF16)
    k = jnp.dot(mn, w_sc[0], preferred_element_type=F32).astype(BF16)
    v = jnp.dot(mn, w_sc[1], preferred_element_type=F32).astype(BF16)
    nt_dims = (((1,), (1,)), ((), ()))
    for h in range(xh):
        hl = slice(h * xd, (h + 1) * xd)
        ml = slice(h * n_mem, (h + 1) * n_mem)
        qk = lax.dot_general(w_sc[2, :, hl], k[:, hl], nt_dims, preferred_element_type=F32)
        qk_ref[0, :, ml] = (qk * scale).astype(BF16)
        vw_ref[0, ml, :] = jnp.dot(v[:, hl], w_sc[3, hl, :],
                                   preferred_element_type=F32).astype(BF16)


def _mem_kv(mem, g, wk, wv, wq, wo):
    bsz, n_mem, d = mem.shape
    hm = X_HEADS * n_mem
    const = lambda b: (0, 0)
    weight = pl.BlockSpec((d, d), const, pipeline_mode=pl.Buffered(1))
    return pl.pallas_call(
        functools.partial(_mem_kv_body, xh=X_HEADS),
        grid=(bsz,),
        in_specs=[pl.BlockSpec((1, n_mem, d), lambda b: (b, 0, 0)), pl.BlockSpec((1, d), const),
                  weight, weight, weight, weight],
        out_specs=[pl.BlockSpec((1, d, hm), lambda b: (b, 0, 0)),
                   pl.BlockSpec((1, hm, d), lambda b: (b, 0, 0))],
        out_shape=[
            jax.ShapeDtypeStruct((bsz, d, hm), BF16),
            jax.ShapeDtypeStruct((bsz, hm, d), BF16),
        ],
        scratch_shapes=[pltpu.VMEM((4, d, d), BF16)],
        compiler_params=_params("arbitrary"),
        name="mem_kv",
    )(mem, g, wk, wv, wq, wo)


def _mix_attn_body(x_ref, o_ref, y_ref, wo_ref, gx_ref, qk_ref, vw_ref, h_ref, *, xh, n_sub):
    dn_w = o_ref.shape[1]
    n_mem = qk_ref.shape[2] // xh
    sub = x_ref.shape[0] // n_sub
    mm = functools.partial(jnp.dot, preferred_element_type=F32)
    wo = wo_ref[...].astype(BF16)

    def rows_gen(r0):
        rows = slice(r0, r0 + sub)
        y = jnp.concatenate([y_ref[q, rows, :] for q in range(y_ref.shape[0])], axis=1)
        mix = mm(o_ref[rows, :].astype(BF16), wo[0:dn_w, :]) + mm(y.astype(BF16), wo[dn_w:, :])
        yield
        h1 = x_ref[rows, :] + mix
        s = mm(_rms(h1, gx_ref[...]).astype(BF16), qk_ref[0])
        yield
        parts = []
        for h in range(xh):
            sh = s[:, h * n_mem:(h + 1) * n_mem]
            e = jnp.exp(sh - jnp.max(sh, axis=-1, keepdims=True))
            parts.append((e / jnp.sum(e, axis=-1, keepdims=True)).astype(BF16))
        att = mm(jnp.concatenate(parts, axis=1), vw_ref[0])
        yield
        h_ref[rows, :] = h1 + att

    _interleave([rows_gen(i * sub) for i in range(n_sub)])


def _mix_attn(x2, o, y, w_out, gx, qk, vw, bsz, seq, *, tm):
    nt, d = x2.shape
    dn_w = o.shape[1]
    n_t = seq // tm
    row = lambda b, t: (b * n_t + t, 0)
    c2 = lambda b, t: (0, 0)
    per_batch = lambda a: pl.BlockSpec((1,) + a.shape[1:], lambda b, t: (b, 0, 0))
    body = functools.partial(_mix_attn_body, xh=X_HEADS, n_sub=2)
    return pl.pallas_call(
        body,
        grid=(bsz, n_t),
        in_specs=[
            pl.BlockSpec((tm, d), row),
            pl.BlockSpec((tm, dn_w), row),
            pl.BlockSpec((y.shape[0], tm, LANES), lambda b, t: (0, b * n_t + t, 0)),
            pl.BlockSpec(w_out.shape, c2, pipeline_mode=pl.Buffered(1)),
            pl.BlockSpec((1, d), c2),
            per_batch(qk),
            per_batch(vw),
        ],
        out_specs=pl.BlockSpec((tm, d), row),
        out_shape=jax.ShapeDtypeStruct((nt, d), F32),
        compiler_params=_params("arbitrary", "arbitrary"),
        name="mix_attn",
    )(x2, o, y, w_out, gx, qk, vw)


def _ffn_body(h_ref, gf_ref, wg_ref, wu_ref, wd_ref, gl_ref, out_ref, *, final, fc):
    h = h_ref[...]
    hn = _rms(h, gf_ref[...]).astype(BF16)
    dff = wg_ref.shape[1]

    def down(gate, up, c0):
        act = (gate * jax.nn.sigmoid(gate) * up).astype(BF16)
        return jnp.dot(act, wd_ref[c0:c0 + fc, :].astype(BF16), preferred_element_type=F32)

    h3 = h
    prev = None
    for c0 in range(0, dff, fc):
        gate = jnp.dot(hn, wg_ref[:, c0:c0 + fc].astype(BF16), preferred_element_type=F32)
        up = jnp.dot(hn, wu_ref[:, c0:c0 + fc].astype(BF16), preferred_element_type=F32)
        if prev is not None:
            h3 = h3 + down(*prev)
        prev = (gate, up, c0)
    h3 = h3 + down(*prev)
    out_ref[...] = _rms(h3, gl_ref[...]) if final else h3


def _ffn(h, gf, wg, wu, wd, gl, *, tm, final):
    nt, d = h.shape
    dff = wg.shape[1]
    c2 = lambda i: (0, 0)
    row = lambda i: (i, 0)
    return pl.pallas_call(
        functools.partial(_ffn_body, final=final, fc=2 * LANES),
        grid=(nt // tm,),
        in_specs=[
            pl.BlockSpec((tm, d), row),
            pl.BlockSpec((1, d), c2),
            pl.BlockSpec((d, dff), c2, pipeline_mode=pl.Buffered(1)),
            pl.BlockSpec((d, dff), c2, pipeline_mode=pl.Buffered(1)),
            pl.BlockSpec((dff, d), c2, pipeline_mode=pl.Buffered(1)),
            pl.BlockSpec((1, d), c2),
        ],
        out_specs=pl.BlockSpec((tm, d), row),
        out_shape=jax.ShapeDtypeStruct((nt, d), F32),
        compiler_params=_params("arbitrary"),
        name="ffn",
    )(h, gf, wg, wu, wd, gl)


def _layer(h2, mem, bsz, seq, norm_mix_g, w_in, conv_w, dn_a_log, dn_dt_bias, dn_norm_g,
           s5_a_re, s5_a_im, s5_b_re, s5_b_im, s5_c_re, s5_c_im, s5_d, s5_log_dt,
           s5_w_glu, s5_b_glu, w_out, norm_x_g, norm_mem_g, w_xq, w_xk, w_xv, w_xo):
    d = h2.shape[1]
    nh = dn_a_log.shape[0]
    dh = dn_norm_g.shape[0]
    dn_w = nh * dh
    s5_w = s5_a_re.shape[0] * s5_b_re.shape[-1]

    off_a = 4 * dn_w
    off_u = off_a + 2 * nh
    w_ab = jnp.pad(w_in[:, off_a:off_u].astype(BF16), ((0, 0), (0, LANES - 2 * nh)))
    w_cat = jnp.concatenate([w_in[:, :off_a].astype(BF16), w_in[:, off_u:].astype(BF16), w_ab],
                            axis=1)
    alog = jnp.pad(dn_a_log, (0, LANES - nh)).reshape(1, LANES)
    dtb = jnp.pad(dn_dt_bias, (0, LANES - nh)).reshape(1, LANES)

    u, o = _mixer_in(h2, seq, norm_mix_g.reshape(1, d), w_cat, conv_w, alog, dtb,
                     dn_norm_g.reshape(1, dh), nh=nh, dh=dh, tb=MIXER_ROWS)

    w1, co, tabs = _s5t_prep(s5_a_re, s5_a_im, s5_log_dt, s5_b_re, s5_b_im, s5_c_re, s5_c_im)
    y = _s5t(u, w1, co, tabs, s5_d.reshape(1, s5_w), s5_w_glu, s5_b_glu.reshape(1, s5_w),
             bsz, seq, rc=S5_BLOCKS)

    qk_mem, vw_mem = _mem_kv(mem, norm_mem_g.reshape(1, d), w_xk, w_xv, w_xq, w_xo)
    return _mix_attn(h2, o, y, w_out, norm_x_g.reshape(1, d), qk_mem, vw_mem, bsz, seq,
                     tm=ATTN_ROWS)


def kernel(x, mem, norm_mix_g, w_in, conv_w, dn_a_log, dn_dt_bias, dn_norm_g, s5_a_re, s5_a_im,
           s5_b_re, s5_b_im, s5_c_re, s5_c_im, s5_d, s5_log_dt, s5_w_glu, s5_b_glu, w_out,
           norm_x_g, norm_mem_g, w_xq, w_xk, w_xv, w_xo, norm_ffn_g, w_gate, w_up, w_down,
           norm_final_g):
    bsz, seq, d = x.shape
    depth = w_in.shape[0]
    h = x.reshape(bsz * seq, d)
    for l in range(depth):
        h = _layer(h, mem, bsz, seq, norm_mix_g[l], w_in[l], conv_w[l], dn_a_log[l],
                   dn_dt_bias[l], dn_norm_g[l], s5_a_re[l], s5_a_im[l], s5_b_re[l], s5_b_im[l],
                   s5_c_re[l], s5_c_im[l], s5_d[l], s5_log_dt[l], s5_w_glu[l], s5_b_glu[l],
                   w_out[l], norm_x_g[l], norm_mem_g[l], w_xq[l], w_xk[l], w_xv[l], w_xo[l])
        h = _ffn(h, norm_ffn_g[l].reshape(1, d), w_gate[l], w_up[l], w_down[l],
                 norm_final_g.reshape(1, d), tm=FFN_ROWS, final=l == depth - 1)
    return h.reshape(bsz, seq, d)
```

```python
import functools

import jax
import jax.numpy as jnp
from jax import lax
from jax.experimental import pallas as pl
from jax.experimental.pallas import tpu as pltpu

F32 = jnp.float32
BF16 = jnp.bfloat16
EPS = 1e-6
HIGHEST = lax.Precision.HIGHEST

LANES = 128
SUBLANES = 8
DN_CHUNK = 64
T_CHUNK = 16
S_CHUNK = 16
X_HEADS = 4
VMEM_LIMIT = 58 * 1024 * 1024
MIXER_VMEM = 62 * 1024 * 1024

MIXER_ROWS = 256
S5_BLOCKS = 128
ATTN_ROWS = 1024
FFN_ROWS = 1024


def _fdot(a, b):
    return jnp.dot(a, b, precision=HIGHEST, preferred_element_type=F32)


def _rms(x, g):
    return x * lax.rsqrt(jnp.mean(x * x, axis=-1, keepdims=True) + EPS) * g


def _interleave(gens, stages_per_round=None):
    pending = [(gen, 1 if stages_per_round is None else stages_per_round[i])
               for i, gen in enumerate(gens)]
    while pending:
        for item in list(pending):
            gen, n = item
            try:
                for _ in range(n):
                    next(gen)
            except StopIteration:
                pending.remove(item)


def _params(*sem, vmem=VMEM_LIMIT):
    return pltpu.CompilerParams(dimension_semantics=sem, vmem_limit_bytes=vmem)


def _in_proj_stages(x_ref, g_ref, w_ref, cw_ref, alog_ref, dtb_ref,
                    q_ref, k_ref, kt_ref, v_ref, z_ref, u_ref, gt_ref,
                    buf_ref, *, tm, dn_w, nh, dh, cw_n):
    xnb = _rms(x_ref[...], g_ref[...]).astype(BF16)
    gw = 2 * dh

    def conv_silu(p, cols):
        outs = []
        for i in range(gw // LANES):
            c0 = cols.start + i * LANES
            s = c0 // LANES
            pc = p[:, i * LANES:(i + 1) * LANES]
            buf_ref[s, SUBLANES:SUBLANES + tm, :] = pc
            acc = cw_ref[cw_n - 1:cw_n, c0:c0 + LANES] * pc
            for j in range(cw_n - 1):
                off = SUBLANES - (cw_n - 1) + j
                acc = acc + cw_ref[j:j + 1, c0:c0 + LANES] * buf_ref[s, off:off + tm, :]
            buf_ref[s, 0:SUBLANES, :] = buf_ref[s, tm:tm + SUBLANES, :]
            outs.append(acc * jax.nn.sigmoid(acc))
        return jnp.concatenate(outs, axis=1)

    def l2n(a, scale):
        parts = []
        for i in range(gw // dh):
            ah = a[:, i * dh:(i + 1) * dh]
            parts.append(ah * (lax.rsqrt(jnp.sum(ah * ah, -1, keepdims=True) + EPS) * scale))
        return jnp.concatenate(parts, axis=1)

    def ep_q(p, c0):
        q_ref[:, c0:c0 + gw] = l2n(conv_silu(p, slice(c0, c0 + gw)), dh ** -0.5)

    def ep_k(p, c0):
        kn = l2n(conv_silu(p, slice(dn_w + c0, dn_w + c0 + gw)), 1.0)
        k_ref[:, c0:c0 + gw] = kn
        kt_ref[c0:c0 + gw, :] = kn.T

    def ep_v(p, c0):
        v_ref[:, c0:c0 + gw] = conv_silu(p, slice(2 * dn_w + c0, 2 * dn_w + c0 + gw))

    def ep_z(p, c0):
        z_ref[:, c0:c0 + gw] = p

    def ep_u(p, c0):
        for j in range(gw // LANES):
            u_ref[c0 // LANES + j] = p[:, j * LANES:(j + 1) * LANES]

    def ep_gates(ab, c0):
        lane = lax.broadcasted_iota(jnp.int32, ab.shape, 1)
        g = -jnp.exp(alog_ref[...]) * jnp.logaddexp(ab + dtb_ref[...], 0.0)
        gates = jnp.where(lane < nh, g, jax.nn.sigmoid(ab))
        gt_ref[...] = gates.T[0:SUBLANES, :]

    groups = []
    for i, ep in enumerate((ep_q, ep_k, ep_v, ep_z, ep_u)):
        groups += [(ep, i * dn_w + c0, gw, c0) for c0 in range(0, dn_w, gw)]
    groups.append((ep_gates, 5 * dn_w, LANES, 0))

    prev = None
    for ep, w0, width, c0 in groups:
        p = jnp.dot(xnb, w_ref[:, w0:w0 + width], preferred_element_type=F32)
        if prev is not None:
            prev[0](prev[1], prev[2])
        prev = (ep, p, c0)
        yield
    prev[0](prev[1], prev[2])


def _delta_stages(q_ref, k_ref, kt_ref, v_ref, z_ref, gt_ref, ng_ref,
                  o_ref, s_ref, u_sc, w_sc, qg_sc, attn_sc, kdt_sc, gtot_sc, wr, rd,
                  *, tb, nh, dh, ch):
    nc = tb // ch
    heads = range(nh)
    hls = [slice(h * dh, (h + 1) * dh) for h in heads]
    mm = functools.partial(jnp.dot, preferred_element_type=F32)

    def phase_b():
        gtot_b = gtot_sc[rd]
        s_all = [s_ref[h] for h in heads]
        for cidx in range(nc):
            rows = slice(cidx * ch, (cidx + 1) * ch)
            sb = [s_all[h].astype(BF16) for h in heads]
            ws = [mm(w_sc[rd, h, rows, :], sb[h]) for h in heads]
            qs = [mm(qg_sc[rd, h, rows, :], sb[h]) for h in heads]
            yield
            vb = [(u_sc[rd, h, rows, :] - ws[h]).astype(BF16) for h in heads]
            av = [mm(attn_sc[rd, h, rows, rows], vb[h]) for h in heads]
            kv = [mm(kdt_sc[rd, h, :, rows], vb[h]) for h in heads]
            yield
            outs = []
            for h in heads:
                a_last = jnp.exp(gtot_b[h:h + 1, cidx * ch:cidx * ch + 1])
                s_all[h] = s_all[h] * a_last + kv[h]
                o = qs[h] + av[h]
                o = o * lax.rsqrt(jnp.mean(o * o, -1, keepdims=True) + EPS)
                zh = z_ref[rows, hls[h]]
                outs.append(o * ng_ref[...] * (zh * jax.nn.sigmoid(zh)))
            o_ref[rows, :] = jnp.concatenate(outs, axis=1)
        for h in heads:
            s_ref[h] = s_all[h]

    r = lax.broadcasted_iota(jnp.int32, (tb, tb), 0)
    c = lax.broadcasted_iota(jnp.int32, (tb, tb), 1)
    same = (r // ch) == (c // ch)
    causal_bd = same & (c <= r)
    strict_bd = same & (c < r)
    gt = gt_ref[...]
    gcum_t = _fdot(gt, jnp.where(same & (r <= c), 1.0, 0.0))
    gtot_t = _fdot(gt, jnp.where(same, 1.0, 0.0))
    gcum = gcum_t.T
    gates = gt.T

    def pack(x):
        acc = x[0:ch, :]
        for i in range(1, nc):
            acc = acc + x[i * ch:(i + 1) * ch, :]
        return acc

    def unpack(x):
        return jnp.where(same, jnp.concatenate([x] * nc, axis=0), 0.0).astype(BF16)

    n_sq = max(1, (ch - 1).bit_length() - 1)

    def phase_a():
        gcol = [gcum[:, h:h + 1] for h in heads]
        grow = [gcum_t[h:h + 1, :] for h in heads]
        beta = [gates[:, nh + h:nh + h + 1] for h in heads]
        kb = [k_ref[:, hls[h]] * beta[h] for h in heads]
        gram = [mm(jnp.concatenate([kb[h], q_ref[:, hls[h]]], axis=0).astype(BF16),
                   kt_ref[hls[h], :].astype(BF16)) for h in heads]
        yield
        ppk, apk = [], []
        for h in heads:
            decay = jnp.exp(jnp.where(causal_bd, gcol[h] - grow[h], -jnp.inf))
            p_bd = -jnp.where(strict_bd, gram[h][:tb] * decay, 0.0)
            attn_sc[wr, h] = (gram[h][tb:] * decay).astype(BF16)
            pk = pack(p_bd)
            apk.append(pk)
            ppk.append(mm(pk.astype(BF16), p_bd.astype(BF16)))
        yield
        for j in range(1, n_sq + 1):
            res = []
            for h in heads:
                lhs = jnp.concatenate([apk[h], ppk[h]], axis=0) if j < n_sq else apk[h]
                res.append(mm(lhs.astype(BF16), unpack(ppk[h])))
            yield
            for h in heads:
                apk[h] = apk[h] + ppk[h] + res[h][:ch]
                if j < n_sq:
                    ppk[h] = res[h][ch:]
        egc = [jnp.exp(gcol[h]) for h in heads]
        rhs = [jnp.concatenate([v_ref[:, hls[h]] * beta[h], kb[h] * egc[h]], axis=1) for h in heads]
        prod = [mm(unpack(apk[h]), rhs[h].astype(BF16)) for h in heads]
        yield
        for h in heads:
            uw = rhs[h] + prod[h]
            u_sc[wr, h] = uw[:, :dh]
            w_sc[wr, h] = uw[:, dh:].astype(BF16)
            qg_sc[wr, h] = (q_ref[:, hls[h]] * egc[h]).astype(BF16)
            kdt_sc[wr, h] = (kt_ref[hls[h], :] * jnp.exp(gtot_t[h:h + 1, :] - grow[h])).astype(BF16)
        gtot_sc[wr] = gtot_t

    return phase_b(), phase_a()


def _mixer_in_body(x_ref, g_ref, w_ref, cw_ref, alog_ref, dtb_ref, ng_ref,
                   u_ref, o_ref,
                   buf_ref, q_sc, k_sc, kt_sc, v_sc, z_sc, gt_sc,
                   s_ref, u_sc, w_sc, qg_sc, attn_sc, kdt_sc, gtot_sc,
                   *, tb, n_t, nh, dh, cw_n):
    s = pl.program_id(0)
    dn_w = nh * dh

    @pl.when(s == 0)
    def _():
        for ref in (buf_ref, q_sc, k_sc, kt_sc, v_sc, z_sc, gt_sc,
                    s_ref, u_sc, w_sc, qg_sc, attn_sc, kdt_sc, gtot_sc):
            ref[...] = jnp.zeros(ref.shape, ref.dtype)

    @pl.when(s % n_t == 0)
    def _():
        buf_ref[:, 0:SUBLANES, :] = jnp.zeros((3 * dn_w // LANES, SUBLANES, LANES), F32)

    @pl.when((s - 2) % n_t == 0)
    def _():
        s_ref[...] = jnp.zeros(s_ref.shape, F32)

    cur = s % 2
    prv = 1 - cur
    zw = s % 3
    zr = (s + 1) % 3
    proj = _in_proj_stages(
        x_ref, g_ref, w_ref, cw_ref, alog_ref, dtb_ref,
        q_sc.at[cur], k_sc.at[cur], kt_sc.at[cur], v_sc.at[cur], z_sc.at[zw], u_ref, gt_sc.at[cur],
        buf_ref, tm=tb, dn_w=dn_w, nh=nh, dh=dh, cw_n=cw_n)
    phase_b, phase_a = _delta_stages(
        q_sc.at[prv], k_sc.at[prv], kt_sc.at[prv], v_sc.at[prv], z_sc.at[zr], gt_sc.at[prv], ng_ref,
        o_ref, s_ref, u_sc, w_sc, qg_sc, attn_sc, kdt_sc, gtot_sc, cur, prv,
        tb=tb, nh=nh, dh=dh, ch=DN_CHUNK)
    _interleave([proj, phase_b, phase_a])


def _mixer_in(x2, seq, g, w_cat, conv_w, alog, dtb, ng, *, nh, dh, tb):
    nt, d = x2.shape
    dn_w = nh * dh
    n_blk = nt // tb
    cw_n = conv_w.shape[0]
    const = lambda s: (0, 0)
    in_blk = lambda s: (jnp.minimum(s, n_blk - 1), 0)
    out_blk = lambda s: (jnp.maximum(s - 2, 0), 0)
    body = functools.partial(_mixer_in_body, tb=tb, n_t=seq // tb, nh=nh, dh=dh, cw_n=cw_n)
    two = lambda *shape: pltpu.VMEM((2,) + shape, F32)
    return pl.pallas_call(
        body,
        grid=(n_blk + 2,),
        in_specs=[pl.BlockSpec((tb, d), in_blk)]
        + [pl.BlockSpec(a.shape, const) for a in (g, w_cat, conv_w, alog, dtb, ng)],
        out_specs=[pl.BlockSpec((dn_w // LANES, tb, LANES),
                                lambda s: (0, jnp.minimum(s, n_blk - 1), 0)),
                   pl.BlockSpec((tb, dn_w), out_blk)],
        out_shape=[jax.ShapeDtypeStruct((dn_w // LANES, nt, LANES), F32),
                   jax.ShapeDtypeStruct((nt, dn_w), F32)],
        scratch_shapes=[
            pltpu.VMEM((3 * dn_w // LANES, tb + SUBLANES, LANES), F32),
            two(tb, dn_w), two(tb, dn_w), two(dn_w, tb), two(tb, dn_w),
            pltpu.VMEM((3, tb, dn_w), F32),
            two(SUBLANES, tb),
            pltpu.VMEM((nh, dh, dh), F32),
            pltpu.VMEM((2, nh, tb, dh), F32),
            pltpu.VMEM((2, nh, tb, dh), BF16),
            pltpu.VMEM((2, nh, tb, dh), BF16),
            pltpu.VMEM((2, nh, tb, tb), BF16),
            pltpu.VMEM((2, nh, dh, tb), BF16),
            pltpu.VMEM((2, SUBLANES, tb), F32),
        ],
        compiler_params=_params("arbitrary", vmem=MIXER_VMEM),
        name="mixer_in",
    )(x2, g, w_cat, conv_w, alog, dtb, ng)


def _ftdot(a, b):
    return lax.dot_general(a.astype(BF16), b.astype(BF16), (((0,), (0,)), ((), ())),
                           preferred_element_type=F32)


def _cmul(xr, xi, yr, yi):
    return xr * yr - xi * yi, xr * yi + xi * yr


def _s5t_prep_body(are_ref, aim_ref, ldt_ref, btr_ref, bti_ref, ctr_ref, cti_ref,
                   are2_ref, aim2_ref, ldt2_ref,
                   w1_ref, co_ref, pin_re, pin_im, pout_re, pout_im, al_re, al_im,
                   *, tc, sc, cg, n_p):
    n_grp = w1_ref.shape[0]
    kw = tc * cg
    a_re = are_ref[...]
    a_im = aim_ref[...]
    dt = jnp.exp(ldt_ref[...])
    lre = a_re * dt
    lim = a_im * dt
    mag = jnp.exp(lre)
    ab = (mag * jnp.cos(lim), mag * jnp.sin(lim))
    nr, ni = ab[0] - 1.0, ab[1]
    den = a_re * a_re + a_im * a_im
    co = ((nr * a_re + ni * a_im) / den, (ni * a_re - nr * a_im) / den)
    inv_mag2 = jnp.exp(-2.0 * lre)
    ai = (ab[0] * inv_mag2, -ab[1] * inv_mag2)
    n_bits = tc.bit_length() - 1
    pos, neg = [ab], [ai]
    for _ in range(n_bits - 1):
        pos.append(_cmul(*pos[-1], *pos[-1]))
        neg.append(_cmul(*neg[-1], *neg[-1]))
    top = pos[0]
    for f in pos[1:]:
        top = _cmul(*top, *f)
    rows = [co, top] + pos + neg
    flat = [part for pair in rows for part in pair]
    cols = []
    for i in range(0, len(flat), SUBLANES):
        blk = flat[i:i + SUBLANES]
        blk = blk + [blk[-1]] * (SUBLANES - len(blk))
        t = jnp.concatenate(blk, axis=0).T
        cols += [t[:, j:j + 1] for j in range(SUBLANES)]
    pair = lambda k: (cols[2 * k], cols[2 * k + 1])
    co_c, top_c = pair(0), pair(1)
    pos_c = [pair(2 + k) for k in range(n_bits)]
    neg_c = [pair(2 + n_bits + k) for k in range(n_bits)]

    slot = lax.broadcasted_iota(jnp.int32, (1, kw), 1) // cg

    def power(factors):
        acc = None
        for k, (fr, fi) in enumerate(factors):
            bit = ((slot >> k) & 1) == 1
            term = (jnp.where(bit, fr, 1.0), jnp.where(bit, fi, 0.0))
            acc = term if acc is None else _cmul(*acc, *term)
        return acc

    place = jnp.where(lax.broadcasted_iota(jnp.int32, (cg, kw), 0)
                      == lax.broadcasted_iota(jnp.int32, (cg, kw), 1) % cg, 1.0, 0.0)
    place = place.astype(BF16)

    def rep(ref, lhs_contract):
        dims = (((lhs_contract,), (0,)), ((), ()))
        out = []
        for g in range(n_grp):
            x = ref[g]
            hi = x.astype(BF16)
            r1 = x - hi.astype(F32)
            mid = r1.astype(BF16)
            lo = (r1 - mid.astype(F32)).astype(BF16)
            out.append(sum(lax.dot_general(part, place, dims, preferred_element_type=F32)
                           for part in (hi, mid, lo)))
        return jnp.concatenate(out, axis=0)

    bt = _cmul(*co_c, rep(btr_ref, 1), rep(bti_ref, 1))
    pb = _cmul(*bt, *power(neg_c))
    qc = _cmul(rep(ctr_ref, 0), rep(cti_ref, 0), *power(pos_c))
    q1 = _cmul(*qc, *pos_c[0])
    bi = _cmul(*pb, *top_c)
    r2 = lax.broadcasted_iota(jnp.int32, (kw, kw), 0) // cg
    c2 = lax.broadcasted_iota(jnp.int32, (kw, kw), 1) // cg
    eye = jnp.where(lax.broadcasted_iota(jnp.int32, (n_p, n_p), 0)
                    == lax.broadcasted_iota(jnp.int32, (n_p, n_p), 1), 1.0, 0.0)
    for g in range(n_grp):
        rs = slice(g * n_p, (g + 1) * n_p)
        m_full = _ftdot(pb[0][rs], qc[0][rs]) - _ftdot(pb[1][rs], qc[1][rs])
        w1_ref[g, :, 0:kw] = jnp.where(r2 <= c2, m_full, 0.0).astype(w1_ref.dtype)
        w1_ref[g, :, kw:kw + n_p] = _ftdot(bi[0][rs], eye).astype(w1_ref.dtype)
        w1_ref[g, :, kw + n_p:kw + 2 * n_p] = _ftdot(bi[1][rs], eye).astype(w1_ref.dtype)
        co_ref[g, 0:n_p, :] = q1[0][rs].astype(co_ref.dtype)
        co_ref[g, n_p:2 * n_p, :] = (-q1[1][rs]).astype(co_ref.dtype)

    _s5t_tables_body(are2_ref, aim2_ref, ldt2_ref, pin_re, pin_im, pout_re, pout_im, al_re, al_im,
                     tc=tc, sc=sc)


def _s5t_prep(a_re, a_im, log_dt, b_re, b_im, c_re, c_im):
    g, p = a_re.shape
    cg = b_re.shape[-1]
    kw = T_CHUNK * cg
    lanes = g * 2 * p
    ldt = jnp.broadcast_to(log_dt[:, None], (g, p))
    row = lambda a: a.reshape(1, g * p)
    spread = lambda a: jnp.broadcast_to(a[:, None, :], (g, 2, p)).reshape(1, lanes)
    tab16 = jax.ShapeDtypeStruct((S_CHUNK, lanes), BF16)
    tab32 = jax.ShapeDtypeStruct((S_CHUNK, lanes), F32)
    one = jax.ShapeDtypeStruct((1, lanes), F32)
    w1, co, *tabs = pl.pallas_call(
        functools.partial(_s5t_prep_body, tc=T_CHUNK, sc=S_CHUNK, cg=cg, n_p=p),
        out_shape=[jax.ShapeDtypeStruct((g, kw, kw + 2 * p), BF16),
                   jax.ShapeDtypeStruct((g, 2 * p, kw), BF16),
                   tab16, tab16, tab32, tab32, one, one],
        compiler_params=pltpu.CompilerParams(vmem_limit_bytes=VMEM_LIMIT),
        name="s5t_prep",
    )(row(a_re), row(a_im), row(ldt),
      b_re, b_im, c_re, c_im, spread(a_re), spread(a_im), spread(ldt))
    return w1, co, tabs


def _s5t_tables_body(are_ref, aim_ref, ldt_ref, pin_re, pin_im, pout_re, pout_im, al_re, al_im,
                     *, tc, sc):
    lre = are_ref[...] * jnp.exp(ldt_ref[...]) * float(tc)
    lim = aim_ref[...] * jnp.exp(ldt_ref[...]) * float(tc)
    lane = lax.broadcasted_iota(jnp.int32, lre.shape, 1)
    sign = jnp.where((lane % LANES) < LANES // 2, -1.0, 1.0)
    n = lax.broadcasted_iota(jnp.int32, (sc, 1), 0).astype(F32) - float(sc // 2)
    m_out = jnp.exp(lre * n)
    m_in = jnp.exp(-(lre * n))
    ang = lim * n
    pin_re[...] = (m_in * jnp.cos(ang)).astype(pin_re.dtype)
    pin_im[...] = (sign * (-(m_in * jnp.sin(ang)))).astype(pin_im.dtype)
    pout_re[...] = m_out * jnp.cos(ang)
    pout_im[...] = sign * (m_out * jnp.sin(ang))
    m_al = jnp.exp(lre * float(sc))
    al_re[...] = m_al * jnp.cos(lim * float(sc))
    al_im[...] = sign * (m_al * jnp.sin(lim * float(sc)))


def _atom_transpose(regs, atom, width):
    regs = list(regs)
    n = len(regs)
    d = n // 2
    while d:
        low = (atom & d) == 0
        for j in range(n):
            if j & d:
                continue
            a, b = regs[j], regs[j + d]
            regs[j] = jnp.where(low, a, pltpu.roll(b, d * width, axis=1))
            regs[j + d] = jnp.where(low, pltpu.roll(a, LANES - d * width, axis=1), b)
        d //= 2
    return regs


def _sublane_transpose(regs, srow):
    regs = list(regs)
    d = SUBLANES // 2
    while d:
        low = (srow & d) == 0
        for j in range(SUBLANES):
            if j & d:
                continue
            a, b = regs[j], regs[j + d]
            regs[j] = jnp.where(low, a, pltpu.roll(b, d, axis=0))
            regs[j + d] = jnp.where(low, pltpu.roll(a, SUBLANES - d, axis=0), b)
        d //= 2
    return regs


def _swap_halves(x):
    tiles = [pltpu.roll(x[:, t * LANES:(t + 1) * LANES], LANES // 2, axis=1)
             for t in range(x.shape[1] // LANES)]
    return jnp.concatenate(tiles, axis=1)


def _s5t_body(u_ref, w1_ref, co_ref, pin_re, pin_im, pout_re, pout_im, al_re, al_im,
              d_ref, wglu_ref, bglu_ref, y_ref,
              upk_ref, yloc_ref, sin_ref, xs_ref, carry_ref, udi_ref, *, rc, tc, sc, n_grp, cg):
    t = pl.program_id(1)
    kw = tc * cg
    n_q = u_ref.shape[0]
    atoms = LANES // cg

    @pl.when(t == 0)
    def _():
        carry_ref[...] = jnp.zeros(carry_ref.shape, F32)
        xs_ref[:, 0:SUBLANES, :] = jnp.zeros((n_grp, SUBLANES, LANES), F32)

    atom = lax.broadcasted_iota(jnp.int32, (rc, LANES), 1) // cg

    n_a = tc // atoms

    assert atoms == SUBLANES
    srow = lax.broadcasted_iota(jnp.int32, (SUBLANES, LANES), 0)

    def rows_to_planes(q, a):
        pieces = [[] for _ in range(atoms)]
        for nb in range(rc // SUBLANES):
            nat = [u_ref[q, (nb * SUBLANES + m) * tc + a * atoms:
                         (nb * SUBLANES + m) * tc + (a + 1) * atoms, :] for m in range(SUBLANES)]
            for k, piece in enumerate(_sublane_transpose(nat, srow)):
                pieces[k].append(piece)
        return [jnp.concatenate(p, axis=0) for p in pieces]

    def planes_to_rows(q, a, planes):
        for nb in range(rc // SUBLANES):
            regs = [p[nb * SUBLANES:(nb + 1) * SUBLANES, :] for p in planes]
            for m, nat in enumerate(_sublane_transpose(regs, srow)):
                r0 = (nb * SUBLANES + m) * tc + a * atoms
                y_ref[q, r0:r0 + atoms, :] = nat

    def pack():
        for q in range(n_q):
            for a in range(n_a):
                planes = rows_to_planes(q, a)
                for k in range(atoms):
                    udi_ref[q, a * atoms + k] = planes[k]
                for r, tile in enumerate(_atom_transpose(planes, atom, cg)):
                    upk_ref[q * atoms + r, :, a * LANES:(a + 1) * LANES] = tile.astype(BF16)
                yield

    per_stage = atoms // n_a

    def local():
        for _ in range(n_a):
            yield
        for g in range(n_grp):
            res = jnp.dot(upk_ref[g], w1_ref[g], preferred_element_type=F32)
            yloc_ref[g] = res[:, :kw]
            sin_ref[g] = res[:, kw:]
            if g % per_stage == per_stage - 1:
                yield

    _interleave([pack(), local()])

    ri = lax.broadcasted_iota(jnp.int32, (sc, sc), 0)
    ci = lax.broadcasted_iota(jnp.int32, (sc, sc), 1)
    tri = jnp.where(ci <= ri, 1.0, 0.0).astype(BF16)
    n_c2 = rc // sc
    ss = [jnp.concatenate([sin_ref[g, c2 * sc:(c2 + 1) * sc, :] for g in range(n_grp)], axis=1)
          for c2 in range(n_c2)]
    ss_sw = [_swap_halves(s) for s in ss]
    xin = [pin_re[...] * ss[c2].astype(BF16) + pin_im[...] * ss_sw[c2].astype(BF16)
           for c2 in range(n_c2)]
    psum = [jnp.dot(tri, x, preferred_element_type=F32) for x in xin]
    psum_sw = [_swap_halves(p) for p in psum]
    carry = carry_ref[...]
    carry_sw = _swap_halves(carry)
    carries = []
    for c2 in range(rc // sc):
        carries.append((carry, carry_sw))
        last = psum[c2][sc - 1:sc, :] + carry
        last_sw = psum_sw[c2][sc - 1:sc, :] + carry_sw
        carry = al_re[...] * last + al_im[...] * last_sw
        carry_sw = al_re[...] * last_sw - al_im[...] * last
    carry_ref[...] = carry
    for c2 in range(rc // sc):
        x = (pout_re[...] * (psum[c2] + carries[c2][0])
             + pout_im[...] * (psum_sw[c2] + carries[c2][1]))
        for g in range(n_grp):
            xs_ref[g, SUBLANES + c2 * sc:SUBLANES + (c2 + 1) * sc, :] = x[:, g * LANES:(g + 1) * LANES]

    for g in range(n_grp):
        xprev = xs_ref[g, SUBLANES - 1:SUBLANES - 1 + rc, :].astype(BF16)
        yloc_ref[g] = yloc_ref[g] + jnp.dot(xprev, co_ref[g], preferred_element_type=F32)
    xs_ref[:, SUBLANES - 1:SUBLANES, :] = xs_ref[:, SUBLANES - 1 + rc:SUBLANES + rc, :]

    wglu = wglu_ref[...].astype(BF16)

    def unpack(a):
        for _ in range(a * n_q):
            yield
        planes = [[None] * n_q for _ in range(atoms)]
        for q in range(n_q):
            tiles = [yloc_ref[q * atoms + r, :, a * LANES:(a + 1) * LANES] for r in range(atoms)]
            for k, tile in enumerate(_atom_transpose(tiles, atom, cg)):
                planes[k][q] = tile
            yield
        outs = []
        for k in range(atoms):
            i = a * atoms + k
            uu = jnp.concatenate([udi_ref[q, i] for q in range(n_q)], axis=1)
            y = jax.nn.gelu(jnp.concatenate(planes[k], axis=1) + d_ref[...] * uu)
            gate = jax.nn.sigmoid(jnp.dot(y.astype(BF16), wglu, preferred_element_type=F32)
                                  + bglu_ref[...])
            outs.append(y * gate)
            if k % 2 == 1:
                yield
        for q in range(n_q):
            planes_to_rows(q, a, [o[:, q * LANES:(q + 1) * LANES] for o in outs])
            yield

    _interleave([unpack(a) for a in range(n_a)])


def _s5t(u4, w1, co, tabs, d, wglu, bglu, bsz, seq, *, rc):
    n_q, nt, _ = u4.shape
    n_grp, kw, wcols = w1.shape
    n_state = (wcols - kw) // 2
    cg = kw // T_CHUNK
    rows = rc * T_CHUNK
    n_t = seq // rows
    blk = pl.BlockSpec((n_q, rows, LANES), lambda b, t: (0, b * n_t + t, 0))
    c2 = lambda b, t: (0, 0)
    c3 = lambda b, t: (0, 0, 0)
    whole = lambda a: pl.BlockSpec(a.shape, c3 if a.ndim == 3 else c2)
    body = functools.partial(_s5t_body, rc=rc, tc=T_CHUNK, sc=S_CHUNK, n_grp=n_grp, cg=cg)
    return pl.pallas_call(
        body,
        grid=(bsz, n_t),
        in_specs=[blk] + [whole(a) for a in (w1, co, *tabs, d, wglu, bglu)],
        out_specs=blk,
        out_shape=jax.ShapeDtypeStruct((n_q, nt, LANES), F32),
        scratch_shapes=[
            pltpu.VMEM((n_grp, rc, kw), BF16),
            pltpu.VMEM((n_grp, rc, kw), F32),
            pltpu.VMEM((n_grp, rc, 2 * n_state), F32),
            pltpu.VMEM((n_grp, rc + SUBLANES, 2 * n_state), F32),
            pltpu.VMEM((1, n_grp * 2 * n_state), F32),
            pltpu.VMEM((n_q, T_CHUNK, rc, LANES), F32),
        ],
        compiler_params=_params("arbitrary", "arbitrary"),
        name="s5t",
    )(u4, w1, co, *tabs, d, wglu, bglu)


def _mem_kv_body(m_ref, g_ref, wk_ref, wv_ref, wq_ref, wo_ref, qk_ref, vw_ref, w_sc, *, xh):
    @pl.when(pl.program_id(0) == 0)
    def _():
        for i, ref in enumerate((wk_ref, wv_ref, wq_ref, wo_ref)):
            w_sc[i] = ref[...].astype(BF16)

    _, n_mem, d = m_ref.shape
    xd = d // xh
    scale = xd ** -0.5
    mn = _rms(m_ref[0], g_ref[...]).astype(BF16)
    k = jnp.dot(mn, w_sc[0], preferred_element_type=F32).astype(BF16)
    v = jnp.dot(mn, w_sc[1], preferred_element_type=F32).astype(BF16)
    nt_dims = (((1,), (1,)), ((), ()))
    for h in range(xh):
        hl = slice(h * xd, (h + 1) * xd)
        ml = slice(h * n_mem, (h + 1) * n_mem)
        qk = lax.dot_general(w_sc[2, :, hl], k[:, hl], nt_dims, preferred_element_type=F32)
        qk_ref[0, :, ml] = (qk * scale).astype(BF16)
        vw_ref[0, ml, :] = jnp.dot(v[:, hl], w_sc[3, hl, :],
                                   preferred_element_type=F32).astype(BF16)


def _mem_kv(mem, g, wk, wv, wq, wo):
    bsz, n_mem, d = mem.shape
    hm = X_HEADS * n_mem
    const = lambda b: (0, 0)
    weight = pl.BlockSpec((d, d), const, pipeline_mode=pl.Buffered(1))
    return pl.pallas_call(
        functools.partial(_mem_kv_body, xh=X_HEADS),
        grid=(bsz,),
        in_specs=[pl.BlockSpec((1, n_mem, d), lambda b: (b, 0, 0)), pl.BlockSpec((1, d), const),
                  weight, weight, weight, weight],
        out_specs=[pl.BlockSpec((1, d, hm), lambda b: (b, 0, 0)),
                   pl.BlockSpec((1, hm, d), lambda b: (b, 0, 0))],
        out_shape=[
            jax.ShapeDtypeStruct((bsz, d, hm), BF16),
            jax.ShapeDtypeStruct((bsz, hm, d), BF16),
        ],
        scratch_shapes=[pltpu.VMEM((4, d, d), BF16)],
        compiler_params=_params("arbitrary"),
        name="mem_kv",
    )(mem, g, wk, wv, wq, wo)


def _mix_attn_body(x_ref, o_ref, y_ref, wo_ref, gx_ref, qk_ref, vw_ref, h_ref, *, xh, n_sub):
    dn_w = o_ref.shape[1]
    n_mem = qk_ref.shape[2] // xh
    sub = x_ref.shape[0] // n_sub
    mm = functools.partial(jnp.dot, preferred_element_type=F32)
    wo = wo_ref[...].astype(BF16)

    def rows_gen(r0):
        rows = slice(r0, r0 + sub)
        y = jnp.concatenate([y_ref[q, rows, :] for q in range(y_ref.shape[0])], axis=1)
        mix = mm(o_ref[rows, :].astype(BF16), wo[0:dn_w, :]) + mm(y.astype(BF16), wo[dn_w:, :])
        yield
        h1 = x_ref[rows, :] + mix
        s = mm(_rms(h1, gx_ref[...]).astype(BF16), qk_ref[0])
        yield
        parts = []
        for h in range(xh):
            sh = s[:, h * n_mem:(h + 1) * n_mem]
            e = jnp.exp(sh - jnp.max(sh, axis=-1, keepdims=True))
            parts.append((e / jnp.sum(e, axis=-1, keepdims=True)).astype(BF16))
        att = mm(jnp.concatenate(parts, axis=1), vw_ref[0])
        yield
        h_ref[rows, :] = h1 + att

    _interleave([rows_gen(i * sub) for i in range(n_sub)])


def _mix_attn(x2, o, y, w_out, gx, qk, vw, bsz, seq, *, tm):
    nt, d = x2.shape
    dn_w = o.shape[1]
    n_t = seq // tm
    row = lambda b, t: (b * n_t + t, 0)
    c2 = lambda b, t: (0, 0)
    per_batch = lambda a: pl.BlockSpec((1,) + a.shape[1:], lambda b, t: (b, 0, 0))
    body = functools.partial(_mix_attn_body, xh=X_HEADS, n_sub=2)
    return pl.pallas_call(
        body,
        grid=(bsz, n_t),
        in_specs=[
            pl.BlockSpec((tm, d), row),
            pl.BlockSpec((tm, dn_w), row),
            pl.BlockSpec((y.shape[0], tm, LANES), lambda b, t: (0, b * n_t + t, 0)),
            pl.BlockSpec(w_out.shape, c2, pipeline_mode=pl.Buffered(1)),
            pl.BlockSpec((1, d), c2),
            per_batch(qk),
            per_batch(vw),
        ],
        out_specs=pl.BlockSpec((tm, d), row),
        out_shape=jax.ShapeDtypeStruct((nt, d), F32),
        compiler_params=_params("arbitrary", "arbitrary"),
        name="mix_attn",
    )(x2, o, y, w_out, gx, qk, vw)


def _ffn_body(h_ref, gf_ref, wg_ref, wu_ref, wd_ref, gl_ref, out_ref, *, final, fc):
    h = h_ref[...]
    hn = _rms(h, gf_ref[...]).astype(BF16)
    dff = wg_ref.shape[1]

    def down(gate, up, c0):
        act = (gate * jax.nn.sigmoid(gate) * up).astype(BF16)
        return jnp.dot(act, wd_ref[c0:c0 + fc, :].astype(BF16), preferred_element_type=F32)

    h3 = h
    prev = None
    for c0 in range(0, dff, fc):
        gate = jnp.dot(hn, wg_ref[:, c0:c0 + fc].astype(BF16), preferred_element_type=F32)
        up = jnp.dot(hn, wu_ref[:, c0:c0 + fc].astype(BF16), preferred_element_type=F32)
        if prev is not None:
            h3 = h3 + down(*prev)
        prev = (gate, up, c0)
    h3 = h3 + down(*prev)
    out_ref[...] = _rms(h3, gl_ref[...]) if final else h3


def _ffn(h, gf, wg, wu, wd, gl, *, tm, final):
    nt, d = h.shape
    dff = wg.shape[1]
    c2 = lambda i: (0, 0)
    row = lambda i: (i, 0)
    return pl.pallas_call(
        functools.partial(_ffn_body, final=final, fc=2 * LANES),
        grid=(nt // tm,),
        in_specs=[
            pl.BlockSpec((tm, d), row),
            pl.BlockSpec((1, d), c2),
            pl.BlockSpec((d, dff), c2, pipeline_mode=pl.Buffered(1)),
            pl.BlockSpec((d, dff), c2, pipeline_mode=pl.Buffered(1)),
            pl.BlockSpec((dff, d), c2, pipeline_mode=pl.Buffered(1)),
            pl.BlockSpec((1, d), c2),
        ],
        out_specs=pl.BlockSpec((tm, d), row),
        out_shape=jax.ShapeDtypeStruct((nt, d), F32),
        compiler_params=_params("arbitrary"),
        name="ffn",
    )(h, gf, wg, wu, wd, gl)


def _layer(h2, mem, bsz, seq, norm_mix_g, w_in, conv_w, dn_a_log, dn_dt_bias, dn_norm_g,
           s5_a_re, s5_a_im, s5_b_re, s5_b_im, s5_c_re, s5_c_im, s5_d, s5_log_dt,
           s5_w_glu, s5_b_glu, w_out, norm_x_g, norm_mem_g, w_xq, w_xk, w_xv, w_xo):
    d = h2.shape[1]
    nh = dn_a_log.shape[0]
    dh = dn_norm_g.shape[0]
    dn_w = nh * dh
    s5_w = s5_a_re.shape[0] * s5_b_re.shape[-1]

    off_a = 4 * dn_w
    off_u = off_a + 2 * nh
    w_ab = jnp.pad(w_in[:, off_a:off_u].astype(BF16), ((0, 0), (0, LANES - 2 * nh)))
    w_cat = jnp.concatenate([w_in[:, :off_a].astype(BF16), w_in[:, off_u:].astype(BF16), w_ab],
                            axis=1)
    alog = jnp.pad(dn_a_log, (0, LANES - nh)).reshape(1, LANES)
    dtb = jnp.pad(dn_dt_bias, (0, LANES - nh)).reshape(1, LANES)

    u, o = _mixer_in(h2, seq, norm_mix_g.reshape(1, d), w_cat, conv_w, alog, dtb,
                     dn_norm_g.reshape(1, dh), nh=nh, dh=dh, tb=MIXER_ROWS)

    w1, co, tabs = _s5t_prep(s5_a_re, s5_a_im, s5_log_dt, s5_b_re, s5_b_im, s5_c_re, s5_c_im)
    y = _s5t(u, w1, co, tabs, s5_d.reshape(1, s5_w), s5_w_glu, s5_b_glu.reshape(1, s5_w),
             bsz, seq, rc=S5_BLOCKS)

    qk_mem, vw_mem = _mem_kv(mem, norm_mem_g.reshape(1, d), w_xk, w_xv, w_xq, w_xo)
    return _mix_attn(h2, o, y, w_out, norm_x_g.reshape(1, d), qk_mem, vw_mem, bsz, seq,
                     tm=ATTN_ROWS)


def kernel(x, mem, norm_mix_g, w_in, conv_w, dn_a_log, dn_dt_bias, dn_norm_g, s5_a_re, s5_a_im,
           s5_b_re, s5_b_im, s5_c_re, s5_c_im, s5_d, s5_log_dt, s5_w_glu, s5_b_glu, w_out,
           norm_x_g, norm_mem_g, w_xq, w_xk, w_xv, w_xo, norm_ffn_g, w_gate, w_up, w_down,
           norm_final_g):
    bsz, seq, d = x.shape
    depth = w_in.shape[0]
    h = x.reshape(bsz * seq, d)
    for l in range(depth):
        h = _layer(h, mem, bsz, seq, norm_mix_g[l], w_in[l], conv_w[l], dn_a_log[l],
                   dn_dt_bias[l], dn_norm_g[l], s5_a_re[l], s5_a_im[l], s5_b_re[l], s5_b_im[l],
                   s5_c_re[l], s5_c_im[l], s5_d[l], s5_log_dt[l], s5_w_glu[l], s5_b_glu[l],
                   w_out[l], norm_x_g[l], norm_mem_g[l], w_xq[l], w_xk[l], w_xv[l], w_xo[l])
        h = _ffn(h, norm_ffn_g[l].reshape(1, d), w_gate[l], w_up[l], w_down[l],
                 norm_final_g.reshape(1, d), tm=FFN_ROWS, final=l == depth - 1)
    return h.reshape(bsz, seq, d)
```
